```python
import jax, jax.numpy as jnp
from jax import lax
import numpy as np

D_MODEL = 1024
BATCH = 8
SEQ = 8192
DEPTH = 1

D_MIX = D_MODEL
D_SGU = D_MIX // 2
D_CONV = D_MIX - D_SGU
HEAD_DIM = 64
N_SGU_HEADS = D_SGU // HEAD_DIM
N_CONV_GROUPS = D_CONV // HEAD_DIM
CHUNK = 128
CONV_WIDTH = 31
D_FF = -(-8 * D_MODEL // (3 * 256)) * 256
ALPHA = (2.0 * DEPTH) ** 0.25
BETA = (8.0 * DEPTH) ** -0.25
LN_EPS = 1e-5

kernel_name = "hybrid_sgu_conformer_deepnorm"


def layer_norm(x, g, b):
    xf = x.astype(jnp.float32)
    mu = jnp.mean(xf, axis=-1, keepdims=True)
    var = jnp.mean(jnp.square(xf - mu), axis=-1, keepdims=True)
    y = (xf - mu) * lax.rsqrt(var + LN_EPS)
    return (y * g.astype(jnp.float32) + b.astype(jnp.float32)).astype(x.dtype)


def spatial_gating(z, ln_g, ln_b, w_s, b_s):
    u, v = jnp.split(z, 2, axis=-1)
    v = layer_norm(v, ln_g, ln_b)
    bsz, seq, _ = v.shape
    n_chunks = seq // CHUNK
    v = v.reshape(bsz, n_chunks, CHUNK, N_SGU_HEADS, HEAD_DIM)
    causal = jnp.tril(jnp.ones((CHUNK, CHUNK), dtype=bool))
    w = jnp.where(causal, w_s, 0).astype(v.dtype)
    mixed = jnp.einsum('hts,bcshd->bcthd', w, v) + b_s.T[None, None, :, :, None]
    return u * mixed.reshape(bsz, seq, D_SGU)


def conv_module(a, g, conv_w, conv_b, ln_g, ln_b):
    h = a * jax.nn.sigmoid(g)
    y = lax.conv_general_dilated(
        h, conv_w[:, None, :].astype(h.dtype),
        window_strides=(1,), padding=[(CONV_WIDTH - 1, 0)],
        dimension_numbers=('NWC', 'WIO', 'NWC'),
        feature_group_count=D_CONV) + conv_b
    y = layer_norm(y, ln_g, ln_b)
    return jax.nn.silu(y)


def hybrid_layer(x, w_in, sgu_ln_g, sgu_ln_b, w_s, b_s, conv_w, conv_b, conv_ln_g, conv_ln_b,
                 w_out, ln1_g, ln1_b, w_gate, w_up, w_down, ln2_g, ln2_b):
    proj = jnp.einsum('bsd,de->bse', x, w_in)
    z_sgu = jax.nn.gelu(proj[..., :2 * D_SGU], approximate=False)
    a_conv, g_conv = jnp.split(proj[..., 2 * D_SGU:], 2, axis=-1)
    y = jnp.concatenate([
        spatial_gating(z_sgu, sgu_ln_g, sgu_ln_b, w_s, b_s),
        conv_module(a_conv, g_conv, conv_w, conv_b, conv_ln_g, conv_ln_b),
    ], axis=-1)
    x = layer_norm(ALPHA * x + jnp.einsum('bse,ed->bsd', y, w_out), ln1_g, ln1_b)
    h = jax.nn.silu(jnp.einsum('bsd,df->bsf', x, w_gate)) * jnp.einsum('bsd,df->bsf', x, w_up)
    x = layer_norm(ALPHA * x + jnp.einsum('bsf,fd->bsd', h, w_down), ln2_g, ln2_b)
    return x


def _fwd_setup_inputs(seed: int = 0) -> dict:
    key = jax.random.key(seed)
    ks = jax.random.split(key, 20)
    f32 = jnp.float32

    def nrm(k, shape, scale):
        return jax.random.normal(k, shape, f32) * scale

    L = DEPTH
    return {
        "x": nrm(ks[0], (BATCH, SEQ, D_MODEL), 1.0),
        "w_in": nrm(ks[1], (L, D_MODEL, 2 * D_SGU + 2 * D_CONV), D_MODEL ** -0.5),
        "sgu_ln_g": 1.0 + nrm(ks[2], (L, D_SGU), 0.02),
        "sgu_ln_b": nrm(ks[3], (L, D_SGU), 0.02),
        "w_s": nrm(ks[4], (L, N_SGU_HEADS, CHUNK, CHUNK), CHUNK ** -0.5),
        "b_s": 1.0 + nrm(ks[5], (L, N_SGU_HEADS, CHUNK), 0.02),
        "conv_w": nrm(ks[6], (L, CONV_WIDTH, D_CONV), CONV_WIDTH ** -0.5),
        "conv_b": nrm(ks[7], (L, D_CONV), 0.02),
        "conv_ln_g": 1.0 + nrm(ks[8], (L, D_CONV), 0.02),
        "conv_ln_b": nrm(ks[9], (L, D_CONV), 0.02),
        "w_out": nrm(ks[10], (L, D_MIX, D_MODEL), BETA * D_MIX ** -0.5),
        "ln1_g": 1.0 + nrm(ks[11], (L, D_MODEL), 0.02),
        "ln1_b": nrm(ks[12], (L, D_MODEL), 0.02),
        "w_gate": nrm(ks[13], (L, D_MODEL, D_FF), D_MODEL ** -0.5),
        "w_up": nrm(ks[14], (L, D_MODEL, D_FF), D_MODEL ** -0.5),
        "w_down": nrm(ks[15], (L, D_FF, D_MODEL), BETA * D_FF ** -0.5),
        "ln2_g": 1.0 + nrm(ks[16], (L, D_MODEL), 0.02),
        "ln2_b": nrm(ks[17], (L, D_MODEL), 0.02),
    }


def _fwd_reference(x, w_in, sgu_ln_g, sgu_ln_b, w_s, b_s, conv_w, conv_b, conv_ln_g, conv_ln_b,
              w_out, ln1_g, ln1_b, w_gate, w_up, w_down, ln2_g, ln2_b):
    for l in range(DEPTH):
        x = hybrid_layer(x, w_in[l], sgu_ln_g[l], sgu_ln_b[l], w_s[l], b_s[l], conv_w[l], conv_b[l],
                         conv_ln_g[l], conv_ln_b[l], w_out[l], ln1_g[l], ln1_b[l],
                         w_gate[l], w_up[l], w_down[l], ln2_g[l], ln2_b[l])
    return x


import jax as _jax
import jax.numpy as _jnp

TWIN_FORMAT = 'train_step'
FWD_PARAMS = ['x', 'w_in', 'sgu_ln_g', 'sgu_ln_b', 'w_s', 'b_s', 'conv_w', 'conv_b', 'conv_ln_g', 'conv_ln_b', 'w_out', 'ln1_g', 'ln1_b', 'w_gate', 'w_up', 'w_down', 'ln2_g', 'ln2_b']
TWIN_WEIGHTS = ['w_in', 'sgu_ln_g', 'sgu_ln_b', 'w_s', 'b_s', 'conv_w', 'conv_b', 'conv_ln_g', 'conv_ln_b', 'w_out', 'ln1_g', 'ln1_b', 'w_gate', 'w_up', 'w_down', 'ln2_g', 'ln2_b']
TWIN_DIFF_INPUT = 'x'
TWIN_INPUTS = ['x', 'w_in', 'sgu_ln_g', 'sgu_ln_b', 'w_s', 'b_s', 'conv_w', 'conv_b', 'conv_ln_g', 'conv_ln_b', 'w_out', 'ln1_g', 'ln1_b', 'w_gate', 'w_up', 'w_down', 'ln2_g', 'ln2_b', 'loss_target', 'm_w_in', 'm_sgu_ln_g', 'm_sgu_ln_b', 'm_w_s', 'm_b_s', 'm_conv_w', 'm_conv_b', 'm_conv_ln_g', 'm_conv_ln_b', 'm_w_out', 'm_ln1_g', 'm_ln1_b', 'm_w_gate', 'm_w_up', 'm_w_down', 'm_ln2_g', 'm_ln2_b', 'v_w_in', 'v_sgu_ln_g', 'v_sgu_ln_b', 'v_w_s', 'v_b_s', 'v_conv_w', 'v_conv_b', 'v_conv_ln_g', 'v_conv_ln_b', 'v_w_out', 'v_ln1_g', 'v_ln1_b', 'v_w_gate', 'v_w_up', 'v_w_down', 'v_ln2_g', 'v_ln2_b']
TWIN_OUTPUTS = ['loss', 'grad_x', 'grad_w_in', 'grad_sgu_ln_g', 'grad_sgu_ln_b', 'grad_w_s', 'grad_b_s', 'grad_conv_w', 'grad_conv_b', 'grad_conv_ln_g', 'grad_conv_ln_b', 'grad_w_out', 'grad_ln1_g', 'grad_ln1_b', 'grad_w_gate', 'grad_w_up', 'grad_w_down', 'grad_ln2_g', 'grad_ln2_b', 'delta_w_in', 'delta_sgu_ln_g', 'delta_sgu_ln_b', 'delta_w_s', 'delta_b_s', 'delta_conv_w', 'delta_conv_b', 'delta_conv_ln_g', 'delta_conv_ln_b', 'delta_w_out', 'delta_ln1_g', 'delta_ln1_b', 'delta_w_gate', 'delta_w_up', 'delta_w_down', 'delta_ln2_g', 'delta_ln2_b', 'new_m_w_in', 'new_m_sgu_ln_g', 'new_m_sgu_ln_b', 'new_m_w_s', 'new_m_b_s', 'new_m_conv_w', 'new_m_conv_b', 'new_m_conv_ln_g', 'new_m_conv_ln_b', 'new_m_w_out', 'new_m_ln1_g', 'new_m_ln1_b', 'new_m_w_gate', 'new_m_w_up', 'new_m_w_down', 'new_m_ln2_g', 'new_m_ln2_b', 'new_v_w_in', 'new_v_sgu_ln_g', 'new_v_sgu_ln_b', 'new_v_w_s', 'new_v_b_s', 'new_v_conv_w', 'new_v_conv_b', 'new_v_conv_ln_g', 'new_v_conv_ln_b', 'new_v_w_out', 'new_v_ln1_g', 'new_v_ln1_b', 'new_v_w_gate', 'new_v_w_up', 'new_v_w_down', 'new_v_ln2_g', 'new_v_ln2_b']
TWIN_LEAF_KINDS = {'loss': 'loss', 'grad_x': 'grad_x', 'grad_w_in': 'grad_w', 'grad_sgu_ln_g': 'grad_w', 'grad_sgu_ln_b': 'grad_w', 'grad_w_s': 'grad_w', 'grad_b_s': 'grad_w', 'grad_conv_w': 'grad_w', 'grad_conv_b': 'grad_w', 'grad_conv_ln_g': 'grad_w', 'grad_conv_ln_b': 'grad_w', 'grad_w_out': 'grad_w', 'grad_ln1_g': 'grad_w', 'grad_ln1_b': 'grad_w', 'grad_w_gate': 'grad_w', 'grad_w_up': 'grad_w', 'grad_w_down': 'grad_w', 'grad_ln2_g': 'grad_w', 'grad_ln2_b': 'grad_w', 'delta_w_in': 'delta_w', 'delta_sgu_ln_g': 'delta_w', 'delta_sgu_ln_b': 'delta_w', 'delta_w_s': 'delta_w', 'delta_b_s': 'delta_w', 'delta_conv_w': 'delta_w', 'delta_conv_b': 'delta_w', 'delta_conv_ln_g': 'delta_w', 'delta_conv_ln_b': 'delta_w', 'delta_w_out': 'delta_w', 'delta_ln1_g': 'delta_w', 'delta_ln1_b': 'delta_w', 'delta_w_gate': 'delta_w', 'delta_w_up': 'delta_w', 'delta_w_down': 'delta_w', 'delta_ln2_g': 'delta_w', 'delta_ln2_b': 'delta_w', 'new_m_w_in': 'new_m', 'new_m_sgu_ln_g': 'new_m', 'new_m_sgu_ln_b': 'new_m', 'new_m_w_s': 'new_m', 'new_m_b_s': 'new_m', 'new_m_conv_w': 'new_m', 'new_m_conv_b': 'new_m', 'new_m_conv_ln_g': 'new_m', 'new_m_conv_ln_b': 'new_m', 'new_m_w_out': 'new_m', 'new_m_ln1_g': 'new_m', 'new_m_ln1_b': 'new_m', 'new_m_w_gate': 'new_m', 'new_m_w_up': 'new_m', 'new_m_w_down': 'new_m', 'new_m_ln2_g': 'new_m', 'new_m_ln2_b': 'new_m', 'new_v_w_in': 'new_v', 'new_v_sgu_ln_g': 'new_v', 'new_v_sgu_ln_b': 'new_v', 'new_v_w_s': 'new_v', 'new_v_b_s': 'new_v', 'new_v_conv_w': 'new_v', 'new_v_conv_b': 'new_v', 'new_v_conv_ln_g': 'new_v', 'new_v_conv_ln_b': 'new_v', 'new_v_w_out': 'new_v', 'new_v_ln1_g': 'new_v', 'new_v_ln1_b': 'new_v', 'new_v_w_gate': 'new_v', 'new_v_w_up': 'new_v', 'new_v_w_down': 'new_v', 'new_v_ln2_g': 'new_v', 'new_v_ln2_b': 'new_v'}


def _forward(args):
    return _fwd_reference(*[args[k] for k in FWD_PARAMS])


def _output_shape():
    def fwd():
        inp = _fwd_setup_inputs(0)
        return _fwd_reference(*[inp[k] for k in FWD_PARAMS])
    out = _jax.eval_shape(fwd)
    return out.shape, out.dtype

N_MICROBATCH = 1
ADAM_LR = 0.001
ADAM_B1 = 0.9
ADAM_B2 = 0.999
ADAM_EPS = 1e-08
ADAM_WD = 0.01
ADAM_STEP = 10
PER_EXAMPLE_BATCH_AXIS = {'x': 0, 'loss_target': 0}
SHARED_INPUTS = []
_WEIGHT_DTYPES = {'w_in': _jnp.float32, 'sgu_ln_g': _jnp.float32, 'sgu_ln_b': _jnp.float32, 'w_s': _jnp.float32, 'b_s': _jnp.float32, 'conv_w': _jnp.float32, 'conv_b': _jnp.float32, 'conv_ln_g': _jnp.float32, 'conv_ln_b': _jnp.float32, 'w_out': _jnp.float32, 'ln1_g': _jnp.float32, 'ln1_b': _jnp.float32, 'w_gate': _jnp.float32, 'w_up': _jnp.float32, 'w_down': _jnp.float32, 'ln2_g': _jnp.float32, 'ln2_b': _jnp.float32}
MOMENT_SCALE = {'w_in': 7.328822e-02, 'sgu_ln_g': 5.906527e-02, 'sgu_ln_b': 6.077783e-02, 'w_s': 4.054482e-02, 'b_s': 5.669426e-02, 'conv_w': 8.224324e-02, 'conv_b': 3.709161e-01, 'conv_ln_g': 1.772962e-01, 'conv_ln_b': 2.295701e-01, 'w_out': 2.044147e-01, 'ln1_g': 2.020363e+00, 'ln1_b': 1.002080e+00, 'w_gate': 4.476860e-02, 'w_up': 4.361988e-02, 'w_down': 1.214731e-01, 'ln2_g': 6.409269e+01, 'ln2_b': 7.116559e+00}


def _to_microbatches(a, axis):
    t = _jnp.moveaxis(a, axis, 0)
    t = t.reshape((N_MICROBATCH, t.shape[0] // N_MICROBATCH) + t.shape[1:])
    return _jnp.moveaxis(t, 1, axis + 1)


def setup_inputs(seed: int = 0) -> dict:
    inp = _fwd_setup_inputs(seed)
    key = _jax.random.fold_in(_jax.random.key(seed), 7919)
    shape, _ = _output_shape()
    out = dict(inp)
    out["loss_target"] = _jax.random.normal(_jax.random.fold_in(key, 0), shape, _jnp.float32)
    for i, name in enumerate(TWIN_WEIGHTS):
        w = inp[name].astype(_jnp.float32)
        if MOMENT_SCALE is None:
            s = _jnp.sqrt(_jnp.mean(_jnp.square(w)) + 1e-30)
        else:
            s = MOMENT_SCALE[name]
        km, kv = _jax.random.split(_jax.random.fold_in(key, i + 1))
        out[name] = w
        out["m_" + name] = s * _jax.random.normal(km, w.shape, _jnp.float32)
        out["v_" + name] = (s * s) * _jax.random.uniform(kv, w.shape, _jnp.float32, 0.5, 1.5)
    if N_MICROBATCH > 1:
        for name, axis in PER_EXAMPLE_BATCH_AXIS.items():
            out[name] = _to_microbatches(out[name], axis)
    return {'x': out['x'], 'w_in': out['w_in'], 'sgu_ln_g': out['sgu_ln_g'], 'sgu_ln_b': out['sgu_ln_b'], 'w_s': out['w_s'], 'b_s': out['b_s'], 'conv_w': out['conv_w'], 'conv_b': out['conv_b'], 'conv_ln_g': out['conv_ln_g'], 'conv_ln_b': out['conv_ln_b'], 'w_out': out['w_out'], 'ln1_g': out['ln1_g'], 'ln1_b': out['ln1_b'], 'w_gate': out['w_gate'], 'w_up': out['w_up'], 'w_down': out['w_down'], 'ln2_g': out['ln2_g'], 'ln2_b': out['ln2_b'], 'loss_target': out['loss_target'], 'm_w_in': out['m_w_in'], 'm_sgu_ln_g': out['m_sgu_ln_g'], 'm_sgu_ln_b': out['m_sgu_ln_b'], 'm_w_s': out['m_w_s'], 'm_b_s': out['m_b_s'], 'm_conv_w': out['m_conv_w'], 'm_conv_b': out['m_conv_b'], 'm_conv_ln_g': out['m_conv_ln_g'], 'm_conv_ln_b': out['m_conv_ln_b'], 'm_w_out': out['m_w_out'], 'm_ln1_g': out['m_ln1_g'], 'm_ln1_b': out['m_ln1_b'], 'm_w_gate': out['m_w_gate'], 'm_w_up': out['m_w_up'], 'm_w_down': out['m_w_down'], 'm_ln2_g': out['m_ln2_g'], 'm_ln2_b': out['m_ln2_b'], 'v_w_in': out['v_w_in'], 'v_sgu_ln_g': out['v_sgu_ln_g'], 'v_sgu_ln_b': out['v_sgu_ln_b'], 'v_w_s': out['v_w_s'], 'v_b_s': out['v_b_s'], 'v_conv_w': out['v_conv_w'], 'v_conv_b': out['v_conv_b'], 'v_conv_ln_g': out['v_conv_ln_g'], 'v_conv_ln_b': out['v_conv_ln_b'], 'v_w_out': out['v_w_out'], 'v_ln1_g': out['v_ln1_g'], 'v_ln1_b': out['v_ln1_b'], 'v_w_gate': out['v_w_gate'], 'v_w_up': out['v_w_up'], 'v_w_down': out['v_w_down'], 'v_ln2_g': out['v_ln2_g'], 'v_ln2_b': out['v_ln2_b']}


def _loss(weights, diff, rest, loss_target):
    with _jax.named_scope("forward"):
        args = {**rest, TWIN_DIFF_INPUT: diff, **{k: w.astype(_WEIGHT_DTYPES[k]) for k, w in weights.items()}}
        y = _forward(args)
    with _jax.named_scope("loss_head"):
        err = _jnp.square(y.astype(_jnp.float32) - loss_target)
        return 0.5 * _jnp.sum(_jnp.mean(err, axis=-1)) if err.ndim else 0.5 * err


def _adamw(w, g, m, v):
    m = ADAM_B1 * m + (1.0 - ADAM_B1) * g
    v = ADAM_B2 * v + (1.0 - ADAM_B2) * _jnp.square(g)
    m_hat = m / (1.0 - ADAM_B1 ** ADAM_STEP)
    v_hat = v / (1.0 - ADAM_B2 ** ADAM_STEP)
    delta = -ADAM_LR * (m_hat / (_jnp.sqrt(v_hat) + ADAM_EPS) + ADAM_WD * w)
    return delta, m, v


def reference(x, w_in, sgu_ln_g, sgu_ln_b, w_s, b_s, conv_w, conv_b, conv_ln_g, conv_ln_b, w_out, ln1_g, ln1_b, w_gate, w_up, w_down, ln2_g, ln2_b, loss_target, m_w_in, m_sgu_ln_g, m_sgu_ln_b, m_w_s, m_b_s, m_conv_w, m_conv_b, m_conv_ln_g, m_conv_ln_b, m_w_out, m_ln1_g, m_ln1_b, m_w_gate, m_w_up, m_w_down, m_ln2_g, m_ln2_b, v_w_in, v_sgu_ln_g, v_sgu_ln_b, v_w_s, v_b_s, v_conv_w, v_conv_b, v_conv_ln_g, v_conv_ln_b, v_w_out, v_ln1_g, v_ln1_b, v_w_gate, v_w_up, v_w_down, v_ln2_g, v_ln2_b):
    given = dict(x=x, w_in=w_in, sgu_ln_g=sgu_ln_g, sgu_ln_b=sgu_ln_b, w_s=w_s, b_s=b_s, conv_w=conv_w, conv_b=conv_b, conv_ln_g=conv_ln_g, conv_ln_b=conv_ln_b, w_out=w_out, ln1_g=ln1_g, ln1_b=ln1_b, w_gate=w_gate, w_up=w_up, w_down=w_down, ln2_g=ln2_g, ln2_b=ln2_b, loss_target=loss_target, m_w_in=m_w_in, m_sgu_ln_g=m_sgu_ln_g, m_sgu_ln_b=m_sgu_ln_b, m_w_s=m_w_s, m_b_s=m_b_s, m_conv_w=m_conv_w, m_conv_b=m_conv_b, m_conv_ln_g=m_conv_ln_g, m_conv_ln_b=m_conv_ln_b, m_w_out=m_w_out, m_ln1_g=m_ln1_g, m_ln1_b=m_ln1_b, m_w_gate=m_w_gate, m_w_up=m_w_up, m_w_down=m_w_down, m_ln2_g=m_ln2_g, m_ln2_b=m_ln2_b, v_w_in=v_w_in, v_sgu_ln_g=v_sgu_ln_g, v_sgu_ln_b=v_sgu_ln_b, v_w_s=v_w_s, v_b_s=v_b_s, v_conv_w=v_conv_w, v_conv_b=v_conv_b, v_conv_ln_g=v_conv_ln_g, v_conv_ln_b=v_conv_ln_b, v_w_out=v_w_out, v_ln1_g=v_ln1_g, v_ln1_b=v_ln1_b, v_w_gate=v_w_gate, v_w_up=v_w_up, v_w_down=v_w_down, v_ln2_g=v_ln2_g, v_ln2_b=v_ln2_b)
    weights = {n: given[n] for n in TWIN_WEIGHTS}
    shared = {n: given[n] for n in SHARED_INPUTS}
    per_example = {n: given[n] for n in ['x']}
    grad_fn = _jax.value_and_grad(_loss, argnums=(0, 1))

    def one_microbatch(ex, loss_target):
        ex = dict(ex)
        diff = ex.pop(TWIN_DIFF_INPUT)
        return grad_fn(weights, diff, {**shared, **ex}, loss_target)

    if N_MICROBATCH == 1:
        loss, (grad_w, grad_x) = one_microbatch(per_example, given["loss_target"])
    else:
        def body(carry, xs):
            loss_sum, grad_sum = carry
            l_k, (gw_k, gx_k) = one_microbatch(xs[0], xs[1])
            with _jax.named_scope("update"):
                return (loss_sum + l_k, _jax.tree.map(_jnp.add, grad_sum, gw_k)), gx_k

        init = (_jnp.zeros((), _jnp.float32), _jax.tree.map(_jnp.zeros_like, weights))
        (loss, grad_w), grad_x = _jax.lax.scan(body, init, (per_example, given["loss_target"]))
    with _jax.named_scope("update"):
        delta_w, new_m, new_v = {}, {}, {}
        for n in TWIN_WEIGHTS:
            delta_w[n], new_m[n], new_v[n] = _adamw(weights[n], grad_w[n], given["m_" + n], given["v_" + n])
    return (loss, grad_x, *[grad_w[n] for n in TWIN_WEIGHTS], *[delta_w[n] for n in TWIN_WEIGHTS],
            *[new_m[n] for n in TWIN_WEIGHTS], *[new_v[n] for n in TWIN_WEIGHTS])
```

```python
import jax
import jax.numpy as jnp
from jax import lax
from jax.experimental import pallas as pl
from jax.experimental.pallas import tpu as pltpu

F32 = jnp.float32
BF16 = jnp.bfloat16

D_MODEL = 1024
D_SGU = 512
D_CONV = 512
N_HEADS = 8
CHUNK = 128
CONV_WIDTH = 31
CONV_ROWS = 32
HALO = 32
D_FF = 2816
N_DEV = 8
FF_SHARD = D_FF // N_DEV
ALPHA = (2.0 * 1) ** 0.25
LN_EPS = 1e-5
INV_SQRT2 = 0.7071067811865476
INV_SQRT_2PI = 0.3989422804014327

ADAM_LR = 0.001
ADAM_B1 = 0.9
ADAM_B2 = 0.999
ADAM_EPS = 1e-08
ADAM_WD = 0.01
ADAM_STEP = 10

MXU_COLS = 256
CONV_ROW_BLOCK = 64
MIB = 1024 * 1024

VMEM = pl.BlockSpec(memory_space=pltpu.VMEM)
ANY = pl.BlockSpec(memory_space=pl.ANY)
MESH = pl.DeviceIdType.MESH


def _params(vmem_mib, grid_dims=0):
    kw = dict(vmem_limit_bytes=vmem_mib * MIB)
    if grid_dims:
        kw["dimension_semantics"] = ("arbitrary",) * grid_dims
    return pltpu.CompilerParams(**kw)


def _full(shape):
    return pl.BlockSpec(shape, lambda i: (0,) * len(shape))


def _dot(a, b):
    return jnp.dot(a, b, preferred_element_type=F32)


def _dot_nt(a, b):
    return lax.dot_general(a, b, (((1,), (1,)), ((), ())), preferred_element_type=F32)


def _dot_tn(a, b):
    return lax.dot_general(a, b, (((0,), (0,)), ((), ())), preferred_element_type=F32)


def _gelu(x):
    return 0.5 * x * (1.0 + lax.erf(x * INV_SQRT2))


def _gelu_grad(x):
    return 0.5 * (1.0 + lax.erf(x * INV_SQRT2)) + x * jnp.exp(-0.5 * x * x) * INV_SQRT_2PI


def _ln_fwd(v):
    mu = jnp.mean(v, axis=-1, keepdims=True)
    d = v - mu
    var = jnp.mean(d * d, axis=-1, keepdims=True)
    rstd = lax.rsqrt(var + LN_EPS)
    return d * rstd, rstd


def _ln_bwd(dyhat, yhat, rstd):
    m1 = jnp.mean(dyhat, axis=-1, keepdims=True)
    m2 = jnp.mean(dyhat * yhat, axis=-1, keepdims=True)
    return rstd * (dyhat - m1 - yhat * m2)


def _colsum(v):
    return jnp.sum(v, axis=0, keepdims=True)


def _head_pair_stack(v, lo):
    return jnp.concatenate([jnp.where(lo, v, 0.0), jnp.where(lo, 0.0, v)], axis=0).astype(BF16)


def _lo_mask():
    return lax.broadcasted_iota(jnp.int32, (CHUNK, CHUNK), 1) < (CHUNK // 2)


def _head_selector():
    head = lax.broadcasted_iota(jnp.int32, (N_HEADS, D_SGU), 0)
    lane = lax.broadcasted_iota(jnp.int32, (N_HEADS, D_SGU), 1)
    width = D_SGU // N_HEADS
    return ((lane >= head * width) & (lane < (head + 1) * width)).astype(F32)


def _causal_conv(pad_ref, w_ref, out_ref, rows, offset_of_tap, bias=None):
    for r in range(0, rows, CONV_ROW_BLOCK):
        if bias is None:
            acc = jnp.zeros((CONV_ROW_BLOCK, D_CONV), F32)
        else:
            acc = jnp.broadcast_to(bias, (CONV_ROW_BLOCK, D_CONV))
        for k in range(CONV_WIDTH):
            acc = acc + w_ref[k:k + 1, :] * pad_ref[pl.ds(r + offset_of_tap(k), CONV_ROW_BLOCK), :]
        out_ref[r:r + CONV_ROW_BLOCK, :] = acc


def _prep_weights(w_in, w_out, w_gate, w_up, w_down, conv_w, w_s, b_s):
    def body(win_ref, wout_ref, wg_ref, wu_ref, wd_ref, cw_ref, ws_ref, bs_ref,
             win_o, wout_o, wgt_o, wut_o, wd_o, cw_o, wcat_o, wcatt_o, bsf_o):
        win_o[...] = win_ref[...].astype(BF16)
        wout_o[...] = wout_ref[...].astype(BF16)
        wgt_o[...] = wg_ref[...].T.astype(BF16)
        wut_o[...] = wu_ref[...].T.astype(BF16)
        wd_o[...] = wd_ref[...].astype(BF16)
        cw_o[...] = jnp.zeros(cw_o.shape, F32)
        cw_o[0:CONV_WIDTH, 0:D_CONV // N_DEV] = cw_ref[...]
        row = lax.broadcasted_iota(jnp.int32, (CHUNK, CHUNK), 0)
        col = lax.broadcasted_iota(jnp.int32, (CHUNK, CHUNK), 1)
        causal = row >= col
        for h in range(N_HEADS):
            w = jnp.where(causal, ws_ref[h], 0.0)
            p, half = h // 2, (h % 2) * CHUNK
            wcat_o[p, :, half:half + CHUNK] = w.astype(BF16)
            wcatt_o[p, :, half:half + CHUNK] = w.T.astype(BF16)
        bsf_o[...] = lax.dot_general(bs_ref[...], _head_selector(), (((0,), (0,)), ((), ())),
                                     preferred_element_type=F32, precision=lax.Precision.HIGHEST)

    S = jax.ShapeDtypeStruct
    return pl.pallas_call(
        body, name="prep_weights",
        out_shape=(S((D_MODEL, 256), BF16), S((128, D_MODEL), BF16), S((FF_SHARD, D_MODEL), BF16),
                   S((FF_SHARD, D_MODEL), BF16), S((FF_SHARD, D_MODEL), BF16), S((CONV_ROWS, 128), F32),
                   S((4, CHUNK, 2 * CHUNK), BF16), S((4, CHUNK, 2 * CHUNK), BF16), S((CHUNK, D_SGU), F32)),
        in_specs=[VMEM] * 8, out_specs=(VMEM,) * 9,
        compiler_params=_params(32),
    )(w_in, w_out, w_gate, w_up, w_down, conv_w, w_s, b_s)


def _mesh_position():
    x, y, c = lax.axis_index("x"), lax.axis_index("y"), lax.axis_index("c")
    return x, y, c


def _peers(x, y, c):
    out = []
    for k in range(1, N_DEV):
        px = 1 - x if (k >> 2) & 1 else x
        py = 1 - y if (k >> 1) & 1 else y
        pc = 1 - c if k & 1 else c
        out.append(((px, py, pc), 4 * px + 2 * py + pc))
    return out


def _exchange(name, scatter, gather):
    arrays = list(scatter) + list(gather)
    n_sc = len(scatter)
    n = len(arrays)

    def body(*refs):
        src, dst = refs[:n], refs[n:2 * n]
        send_sems, recv_sems, local_sems = refs[2 * n:]
        x, y, c = _mesh_position()
        me = 4 * x + 2 * y + c
        peers = _peers(x, y, c)

        def piece(a, blk):
            return src[a].at[blk] if a < n_sc else src[a]

        sends, recvs, locals_ = [], [], []
        for a in range(n):
            cp = pltpu.make_async_copy(piece(a, me), dst[a].at[me], local_sems.at[a])
            cp.start()
            locals_.append(cp)
            for k, (peer, pid) in enumerate(peers):
                s = a * (N_DEV - 1) + k
                cp = pltpu.make_async_remote_copy(src_ref=piece(a, pid), dst_ref=dst[a].at[me],
                                                  send_sem=send_sems.at[s], recv_sem=recv_sems.at[s],
                                                  device_id=peer, device_id_type=MESH)
                cp.start()
                sends.append(cp)
                recvs.append(pltpu.make_async_remote_copy(src_ref=piece(a, pid), dst_ref=dst[a].at[pid],
                                                          send_sem=send_sems.at[s], recv_sem=recv_sems.at[s],
                                                          device_id=peer, device_id_type=MESH))
        for cp in recvs:
            cp.wait_recv()
        for cp in sends:
            cp.wait_send()
        for cp in locals_:
            cp.wait()

    out_shape = []
    for a, arr in enumerate(arrays):
        shp = arr.shape if a < n_sc else (N_DEV,) + arr.shape
        out_shape.append(jax.ShapeDtypeStruct(shp, arr.dtype))
    n_remote = n * (N_DEV - 1)
    return pl.pallas_call(
        body, name=name, out_shape=tuple(out_shape),
        in_specs=[ANY] * n, out_specs=(ANY,) * n,
        scratch_shapes=[pltpu.SemaphoreType.DMA((n_remote,)), pltpu.SemaphoreType.DMA((n_remote,)),
                        pltpu.SemaphoreType.DMA((n,))],
    )(*arrays)


def _fwd_mix(x, win_g, wout_g, sgu_g, sgu_b, wcat, bs_full, cw, cb, cg, cbeta, tm):
    T = x.shape[0]
    nt = T // tm

    def body(x_ref, win_ref, wout_ref, sg_ref, sb_ref, wcat_ref, bs_ref, cw_ref, cb_ref, cg_ref, cbeta_ref,
             proj_ref, ycat_ref, n1_ref, rstd1_ref, hpad, ybuf):
        i = pl.program_id(0)
        xf = x_ref[...]
        xb = xf.astype(BF16)
        for j in range(N_DEV):
            proj_ref[:, 256 * j:256 * (j + 1)] = _dot(xb, win_ref[j])
        u = _gelu(proj_ref[:, 0:D_SGU])
        v = _gelu(proj_ref[:, D_SGU:2 * D_SGU])
        vhat, _ = _ln_fwd(v)
        vn = vhat * sg_ref[...] + sb_ref[...]
        lo = _lo_mask()
        for c in range(tm // CHUNK):
            rows = slice(CHUNK * c, CHUNK * (c + 1))
            for p in range(4):
                lanes = slice(CHUNK * p, CHUNK * (p + 1))
                mixed = _dot(wcat_ref[p], _head_pair_stack(vn[rows, lanes], lo)) + bs_ref[:, lanes]
                ycat_ref[rows, lanes] = (u[rows, lanes] * mixed).astype(BF16)
        base = 2 * D_SGU
        a = proj_ref[:, base:base + D_CONV]
        g = proj_ref[:, base + D_CONV:base + 2 * D_CONV]

        @pl.when(i == 0)
        def _():
            hpad[0:HALO, :] = jnp.zeros((HALO, D_CONV), F32)

        hpad[HALO:HALO + tm, :] = a * jax.nn.sigmoid(g)
        _causal_conv(hpad, cw_ref, ybuf, tm, lambda k: HALO - (CONV_WIDTH - 1) + k, bias=cb_ref[...])
        hpad[0:HALO, :] = hpad[tm:tm + HALO, :]
        yhat, _ = _ln_fwd(ybuf[...])
        yn = yhat * cg_ref[...] + cbeta_ref[...]
        ycat_ref[:, D_SGU:D_SGU + D_CONV] = (yn * jax.nn.sigmoid(yn)).astype(BF16)
        r1 = ALPHA * xf + _dot(ycat_ref[...], wout_ref[...])
        n1, rstd1 = _ln_fwd(r1)
        n1_ref[...] = n1
        rstd1_ref[...] = rstd1

    S = jax.ShapeDtypeStruct
    row = lambda w: pl.BlockSpec((tm, w), lambda i: (i, 0))
    return pl.pallas_call(
        body, name="fwd_mix", grid=(nt,),
        in_specs=[row(D_MODEL), _full(win_g.shape), _full(wout_g.shape), _full(sgu_g.shape), _full(sgu_b.shape),
                  _full(wcat.shape), _full(bs_full.shape), _full(cw.shape), _full(cb.shape), _full(cg.shape),
                  _full(cbeta.shape)],
        out_specs=(row(2 * D_MODEL), row(D_MODEL), row(D_MODEL), row(1)),
        out_shape=(S((T, 2 * D_MODEL), F32), S((T, D_MODEL), BF16), S((T, D_MODEL), F32), S((T, 1), F32)),
        scratch_shapes=[pltpu.VMEM((tm + HALO, D_CONV), F32), pltpu.VMEM((tm, D_CONV), F32)],
        compiler_params=_params(56, 1),
    )(x, win_g, wout_g, sgu_g, sgu_b, wcat, bs_full, cw, cb, cg, cbeta)


def _load_resident(pairs, sems):
    cps = [pltpu.make_async_copy(s, d, sems.at[k]) for k, (s, d) in enumerate(pairs)]
    for cp in cps:
        cp.start()
    for cp in cps:
        cp.wait()


def _fwd_mlp(n1, tgt, l1g, l1b, l2g, l2b, wgt, wut, wd, tm):
    T = n1.shape[0]
    nt = T // tm
    nf = D_FF // MXU_COLS

    def body(n1_ref, tgt_ref, l1g_ref, l1b_ref, l2g_ref, l2b_ref, wg_hbm, wu_hbm, wd_hbm,
             gate_ref, up_ref, x1b_ref, dr2_ref, stat_ref, wg_s, wu_s, wd_s, sems):
        i = pl.program_id(0)

        @pl.when(i == 0)
        def _():
            _load_resident([(wg_hbm, wg_s), (wu_hbm, wu_s), (wd_hbm, wd_s)], sems)
            stat_ref[...] = jnp.zeros(stat_ref.shape, F32)

        x1 = n1_ref[...] * l1g_ref[...] + l1b_ref[...]
        x1b = x1.astype(BF16)
        x1b_ref[...] = x1b
        acc = jnp.zeros((tm, D_MODEL), F32)
        for f in range(nf):
            cols = slice(MXU_COLS * f, MXU_COLS * (f + 1))
            gt = _dot_nt(x1b, wg_s[cols, :])
            ut = _dot_nt(x1b, wu_s[cols, :])
            gate_ref[:, cols] = gt.astype(BF16)
            up_ref[:, cols] = ut.astype(BF16)
            hh = (gt * jax.nn.sigmoid(gt) * ut).astype(BF16)
            acc = acc + _dot(hh, wd_s[cols, :])
        r2 = ALPHA * x1 + acc
        n2, rstd2 = _ln_fwd(r2)
        x2 = n2 * l2g_ref[...] + l2b_ref[...]
        diff = x2 - tgt_ref[...]
        dx2 = diff * (1.0 / D_MODEL)
        stat_ref[0:1, :] += _colsum(diff * diff)
        stat_ref[1:2, :] += _colsum(dx2 * n2)
        stat_ref[2:3, :] += _colsum(dx2)
        dr2_ref[...] = _ln_bwd(dx2 * l2g_ref[...], n2, rstd2)

    S = jax.ShapeDtypeStruct
    row = lambda w: pl.BlockSpec((tm, w), lambda i: (i, 0))
    vec = _full((1, D_MODEL))
    return pl.pallas_call(
        body, name="fwd_mlp", grid=(nt,),
        in_specs=[row(D_MODEL), row(D_MODEL), vec, vec, vec, vec, ANY, ANY, ANY],
        out_specs=(row(D_FF), row(D_FF), row(D_MODEL), row(D_MODEL), _full((8, D_MODEL))),
        out_shape=(S((T, D_FF), BF16), S((T, D_FF), BF16), S((T, D_MODEL), BF16), S((T, D_MODEL), F32),
                   S((8, D_MODEL), F32)),
        scratch_shapes=[pltpu.VMEM((D_FF, D_MODEL), BF16)] * 3 + [pltpu.SemaphoreType.DMA((3,))],
        compiler_params=_params(56, 1),
    )(n1, tgt, l1g, l1b, l2g, l2b, wgt, wut, wd)


def _bwd_mlp(dr2, gate, up, n1, rstd1, l1g, wgt, wut, wd, tm):
    T = n1.shape[0]
    nt = T // tm
    nf = D_FF // MXU_COLS

    def body(dr2_ref, gate_ref, up_ref, n1_ref, rstd1_ref, l1g_ref, wg_hbm, wu_hbm, wd_hbm,
             dgate_ref, dup_ref, hh_ref, dr1_ref, stat_ref, wg_s, wu_s, wd_s, sems):
        i = pl.program_id(0)

        @pl.when(i == 0)
        def _():
            _load_resident([(wg_hbm, wg_s), (wu_hbm, wu_s), (wd_hbm, wd_s)], sems)
            stat_ref[...] = jnp.zeros(stat_ref.shape, F32)

        dr2 = dr2_ref[...]
        dr2b = dr2.astype(BF16)
        acc = jnp.zeros((tm, D_MODEL), F32)
        for f in range(nf):
            cols = slice(MXU_COLS * f, MXU_COLS * (f + 1))
            dhh = _dot_nt(dr2b, wd_s[cols, :])
            gt = gate_ref[:, cols].astype(F32)
            ut = up_ref[:, cols].astype(F32)
            sg = jax.nn.sigmoid(gt)
            silu = gt * sg
            dg = (dhh * ut * (sg * (1.0 + gt * (1.0 - sg)))).astype(BF16)
            du = (dhh * silu).astype(BF16)
            dgate_ref[:, cols] = dg
            dup_ref[:, cols] = du
            hh_ref[:, cols] = (silu * ut).astype(BF16)
            acc = acc + _dot(dg, wg_s[cols, :]) + _dot(du, wu_s[cols, :])
        dx1 = ALPHA * dr2 + acc
        n1 = n1_ref[...]
        stat_ref[0:1, :] += _colsum(dx1 * n1)
        stat_ref[1:2, :] += _colsum(dx1)
        dr1_ref[...] = _ln_bwd(dx1 * l1g_ref[...], n1, rstd1_ref[...])

    S = jax.ShapeDtypeStruct
    row = lambda w: pl.BlockSpec((tm, w), lambda i: (i, 0))
    return pl.pallas_call(
        body, name="bwd_mlp", grid=(nt,),
        in_specs=[row(D_MODEL), row(D_FF), row(D_FF), row(D_MODEL), row(1), _full((1, D_MODEL)), ANY, ANY, ANY],
        out_specs=(row(D_FF), row(D_FF), row(D_FF), row(D_MODEL), _full((8, D_MODEL))),
        out_shape=(S((T, D_FF), BF16), S((T, D_FF), BF16), S((T, D_FF), BF16), S((T, D_MODEL), F32),
                   S((8, D_MODEL), F32)),
        scratch_shapes=[pltpu.VMEM((D_FF, D_MODEL), BF16)] * 3 + [pltpu.SemaphoreType.DMA((3,))],
        compiler_params=_params(56, 1),
    )(dr2, gate, up, n1, rstd1, l1g, wgt, wut, wd)


def _bwd_mix(dr1, proj, win_g, wout_g, sgu_g, sgu_b, wcat, wcatt, bs_full, cw, cb, cg, cbeta, tm):
    T = dr1.shape[0]
    nt = T // tm
    halo_blocks = tm // HALO

    def body(dr1_ref, proj_ref, halo_ref, win_ref, wout_ref, sg_ref, sb_ref, wcat_ref, wcatt_ref, bs_ref,
             cw_ref, cb_ref, cg_ref, cbeta_ref,
             gx_ref, dproj_ref, gws_ref, gbs_ref, gcw_ref, vec_ref,
             hpad, dypad, ybuf, dhbuf, dubuf, dvnbuf, gcw_acc):
        i = pl.program_id(0)
        tile = nt - 1 - i

        @pl.when(i == 0)
        def _():
            gws_ref[...] = jnp.zeros(gws_ref.shape, F32)
            gbs_ref[...] = jnp.zeros(gbs_ref.shape, F32)
            gcw_ref[...] = jnp.zeros(gcw_ref.shape, F32)
            vec_ref[...] = jnp.zeros(vec_ref.shape, F32)
            gcw_acc[...] = jnp.zeros(gcw_acc.shape, F32)
            dypad[tm:tm + HALO, :] = jnp.zeros((HALO, D_CONV), F32)

        dr1 = dr1_ref[...]
        dycat = _dot_nt(dr1.astype(BF16), wout_ref[...])
        pu = proj_ref[:, 0:D_SGU]
        pv = proj_ref[:, D_SGU:2 * D_SGU]
        u = _gelu(pu)
        vhat, rstd_v = _ln_fwd(_gelu(pv))
        vn = vhat * sg_ref[...] + sb_ref[...]
        lo = _lo_mask()
        for c in range(tm // CHUNK):
            rows = slice(CHUNK * c, CHUNK * (c + 1))
            for p in range(4):
                lanes = slice(CHUNK * p, CHUNK * (p + 1))
                vstack = _head_pair_stack(vn[rows, lanes], lo)
                mixed = _dot(wcat_ref[p], vstack) + bs_ref[:, lanes]
                d_a = dycat[rows, lanes]
                dubuf[rows, lanes] = d_a * mixed
                dm = d_a * u[rows, lanes]
                gbs_ref[:, lanes] += dm
                dstack = _head_pair_stack(dm, lo)
                gws_ref[2 * CHUNK * p:2 * CHUNK * (p + 1), :] += _dot_nt(dstack, vn[rows, lanes].astype(BF16))
                dvnbuf[rows, lanes] = _dot(wcatt_ref[p], dstack)
        dvn = dvnbuf[...]
        vec_ref[0:1, :] += _colsum(dvn * vhat)
        vec_ref[1:2, :] += _colsum(dvn)
        dv = _ln_bwd(dvn * sg_ref[...], vhat, rstd_v)
        dproj_ref[:, 0:D_SGU] = (dubuf[...] * _gelu_grad(pu)).astype(BF16)
        dproj_ref[:, D_SGU:2 * D_SGU] = (dv * _gelu_grad(pv)).astype(BF16)
        base = 2 * D_SGU
        a = proj_ref[:, base:base + D_CONV]
        sgm = jax.nn.sigmoid(proj_ref[:, base + D_CONV:base + 2 * D_CONV])
        h_before = halo_ref[:, 0:D_CONV] * jax.nn.sigmoid(halo_ref[:, D_CONV:2 * D_CONV])
        hpad[0:HALO, :] = jnp.where(tile > 0, h_before, 0.0)
        hpad[HALO:HALO + tm, :] = a * sgm
        _causal_conv(hpad, cw_ref, ybuf, tm, lambda k: HALO - (CONV_WIDTH - 1) + k, bias=cb_ref[...])
        yhat, rstd_y = _ln_fwd(ybuf[...])
        yn = yhat * cg_ref[...] + cbeta_ref[...]
        s = jax.nn.sigmoid(yn)
        dyn = dycat[:, D_SGU:D_SGU + D_CONV] * (s * (1.0 + yn * (1.0 - s)))
        vec_ref[3:4, :] += _colsum(dyn * yhat)
        vec_ref[4:5, :] += _colsum(dyn)
        dy = _ln_bwd(dyn * cg_ref[...], yhat, rstd_y)
        vec_ref[2:3, :] += _colsum(dy)
        dypad[0:tm, :] = dy
        _causal_conv(dypad, cw_ref, dhbuf, tm, lambda k: (CONV_WIDTH - 1) - k)
        for r in range(0, tm, CONV_ROW_BLOCK):
            dyb = dypad[r:r + CONV_ROW_BLOCK, :]
            for k in range(CONV_WIDTH):
                off = r + HALO - (CONV_WIDTH - 1) + k
                pr = dyb * hpad[pl.ds(off, CONV_ROW_BLOCK), :]
                gcw_acc[k] += pr.reshape(CONV_ROW_BLOCK // 8, 8, D_CONV).sum(axis=0)
        dypad[tm:tm + HALO, :] = dypad[0:HALO, :]
        dh = dhbuf[...]
        dproj_ref[:, base:base + D_CONV] = (dh * sgm).astype(BF16)
        dproj_ref[:, base + D_CONV:base + 2 * D_CONV] = (dh * a * sgm * (1.0 - sgm)).astype(BF16)
        gx = ALPHA * dr1
        for j in range(N_DEV):
            gx = gx + _dot_nt(dproj_ref[:, 256 * j:256 * (j + 1)], win_ref[j])
        gx_ref[...] = gx

        @pl.when(i == nt - 1)
        def _():
            gcw_ref[...] = gcw_acc[...].sum(axis=1)

    S = jax.ShapeDtypeStruct
    row = lambda w: pl.BlockSpec((tm, w), lambda i: (nt - 1 - i, 0))
    halo = pl.BlockSpec((HALO, D_MODEL), lambda i: (jnp.maximum((nt - 1 - i) * halo_blocks - 1, 0), 1))
    return pl.pallas_call(
        body, name="bwd_mix", grid=(nt,),
        in_specs=[row(D_MODEL), row(2 * D_MODEL), halo, _full(win_g.shape), _full(wout_g.shape), _full(sgu_g.shape),
                  _full(sgu_b.shape), _full(wcat.shape), _full(wcatt.shape), _full(bs_full.shape), _full(cw.shape),
                  _full(cb.shape), _full(cg.shape), _full(cbeta.shape)],
        out_specs=(row(D_MODEL), row(2 * D_MODEL), _full((N_HEADS * CHUNK, CHUNK)), _full((CHUNK, D_SGU)),
                   _full((CONV_ROWS, D_CONV)), _full((8, D_CONV))),
        out_shape=(S((T, D_MODEL), F32), S((T, 2 * D_MODEL), BF16), S((N_HEADS * CHUNK, CHUNK), F32),
                   S((CHUNK, D_SGU), F32), S((CONV_ROWS, D_CONV), F32), S((8, D_CONV), F32)),
        scratch_shapes=[pltpu.VMEM((tm + HALO, D_CONV), F32), pltpu.VMEM((tm + HALO, D_CONV), F32),
                        pltpu.VMEM((tm, D_CONV), F32), pltpu.VMEM((tm, D_CONV), F32), pltpu.VMEM((tm, D_SGU), F32),
                        pltpu.VMEM((tm, D_SGU), F32), pltpu.VMEM((CONV_ROWS, 8, D_CONV), F32)],
        compiler_params=_params(56, 1),
    )(dr1, proj, proj, win_g, wout_g, sgu_g, sgu_b, wcat, wcatt, bs_full, cw, cb, cg, cbeta)


def _wgrad(name, a, b, blocks, tk):
    T, M = a.shape
    N = b.shape[1]
    nk = T // tk
    by_rows = blocks > 0
    nb = abs(blocks)
    out_shape = (nb, M // nb, N) if by_rows else (nb, M, N // nb)

    def body(a_ref, b_ref, o_ref, acc):
        i = pl.program_id(0)

        @pl.when(i == 0)
        def _():
            acc[...] = jnp.zeros(acc.shape, F32)

        acc[...] += _dot_tn(a_ref[...].astype(BF16), b_ref[...].astype(BF16))

        @pl.when(i == nk - 1)
        def _():
            for j in range(nb):
                if by_rows:
                    o_ref[j] = acc[(M // nb) * j:(M // nb) * (j + 1), :].astype(BF16)
                else:
                    o_ref[j] = acc[:, (N // nb) * j:(N // nb) * (j + 1)].astype(BF16)

    return pl.pallas_call(
        body, name=name, grid=(nk,),
        in_specs=[pl.BlockSpec((tk, M), lambda i: (i, 0)), pl.BlockSpec((tk, N), lambda i: (i, 0))],
        out_specs=_full(out_shape), out_shape=jax.ShapeDtypeStruct(out_shape, BF16),
        scratch_shapes=[pltpu.VMEM((M, N), F32)],
        compiler_params=_params(56, 1),
    )(a, b)


def _adamw(w, g, m, v):
    m2 = ADAM_B1 * m + (1.0 - ADAM_B1) * g
    v2 = ADAM_B2 * v + (1.0 - ADAM_B2) * (g * g)
    m_hat = m2 / (1.0 - ADAM_B1 ** ADAM_STEP)
    v_hat = v2 / (1.0 - ADAM_B2 ** ADAM_STEP)
    delta = -ADAM_LR * (m_hat / (jnp.sqrt(v_hat) + ADAM_EPS) + ADAM_WD * w)
    return delta, m2, v2


def _sum_partials(r_ref):
    g = r_ref[0].astype(F32)
    for s in range(1, N_DEV):
        g = g + r_ref[s].astype(F32)
    return g


def _adamw_shard(name, parts, w, m, v, transposed):
    def body(r_ref, w_ref, m_ref, v_ref, g_o, d_o, m_o, v_o):
        g = _sum_partials(r_ref)
        if transposed:
            g = g.T
        delta, m2, v2 = _adamw(w_ref[...], g, m_ref[...], v_ref[...])
        g_o[...] = g
        d_o[...] = delta
        m_o[...] = m2
        v_o[...] = v2

    return pl.pallas_call(
        body, name=name, out_shape=(jax.ShapeDtypeStruct(w.shape, F32),) * 4,
        in_specs=[VMEM] * 4, out_specs=(VMEM,) * 4, compiler_params=_params(40),
    )(parts, w, m, v)


def _finish_small(gws8, gbs8, gcw8, vmix8, vmlp8, vout8, small):
    names = ["sgu_ln_g", "sgu_ln_b", "w_s", "b_s", "conv_b", "conv_ln_g", "conv_ln_b", "ln1_g", "ln1_b", "ln2_g", "ln2_b"]
    flat = []
    for n in names:
        flat += list(small[n])

    def body(*refs):
        gws_ref, gbs_ref, gcw_ref, vmix_ref, vmlp_ref, vout_ref = refs[:6]
        wmv = refs[6:6 + 3 * len(names)]
        outs = refs[6 + 3 * len(names):]
        loss_o, gcw_o = outs[0], outs[1]
        outs = outs[2:]
        gws = _sum_partials(gws_ref)
        gbs = _sum_partials(gbs_ref)
        vmix = _sum_partials(vmix_ref)
        vmlp = _sum_partials(vmlp_ref)
        vout = _sum_partials(vout_ref)
        gcw_o[...] = _sum_partials(gcw_ref)
        loss_o[...] = (0.5 / D_MODEL) * jnp.sum(vout[0:1, :], axis=1, keepdims=True)
        rows = lax.broadcasted_iota(jnp.int32, (N_HEADS * CHUNK, CHUNK), 0)
        cols = lax.broadcasted_iota(jnp.int32, (N_HEADS * CHUNK, CHUNK), 1)
        gws = jnp.where((rows & (CHUNK - 1)) >= cols, gws, 0.0)
        g_bs = lax.dot_general(_head_selector(), gbs, (((1,), (1,)), ((), ())), preferred_element_type=F32,
                               precision=lax.Precision.HIGHEST)
        grads = {
            "sgu_ln_g": vmix[0:1, :], "sgu_ln_b": vmix[1:2, :], "w_s": gws, "b_s": g_bs,
            "conv_b": vmix[2:3, :], "conv_ln_g": vmix[3:4, :], "conv_ln_b": vmix[4:5, :],
            "ln1_g": vmlp[0:1, :], "ln1_b": vmlp[1:2, :], "ln2_g": vout[1:2, :], "ln2_b": vout[2:3, :],
        }
        for k, n in enumerate(names):
            w_ref, m_ref, v_ref = wmv[3 * k:3 * k + 3]
            g = grads[n]
            delta, m2, v2 = _adamw(w_ref[...], g, m_ref[...], v_ref[...])
            outs[4 * k][...] = g
            outs[4 * k + 1][...] = delta
            outs[4 * k + 2][...] = m2
            outs[4 * k + 3][...] = v2

    S = jax.ShapeDtypeStruct
    out_shape = [S((1, 1), F32), S((CONV_ROWS, D_CONV), F32)]
    for n in names:
        out_shape += [S(small[n][0].shape, F32)] * 4
    res = pl.pallas_call(
        body, name="finish_small", out_shape=tuple(out_shape),
        in_specs=[VMEM] * (6 + len(flat)), out_specs=(VMEM,) * len(out_shape), compiler_params=_params(40),
    )(gws8, gbs8, gcw8, vmix8, vmlp8, vout8, *flat)
    upd = {n: res[2 + 4 * k:6 + 4 * k] for k, n in enumerate(names)}
    return res[0], res[1], upd


def _adamw_plain(name, g, w, m, v):
    def body(g_ref, w_ref, m_ref, v_ref, d_o, m_o, v_o):
        delta, m2, v2 = _adamw(w_ref[...], g_ref[...], m_ref[...], v_ref[...])
        d_o[...] = delta
        m_o[...] = m2
        v_o[...] = v2

    return pl.pallas_call(
        body, name=name, out_shape=(jax.ShapeDtypeStruct(w.shape, F32),) * 3,
        in_specs=[VMEM] * 4, out_specs=(VMEM,) * 3,
    )(g, w, m, v)


TOKEN_TILE_MIX = 256
TOKEN_TILE_MLP = 256
TOKEN_TILE_WGRAD = 512


def kernel(x, w_in, sgu_ln_g, sgu_ln_b, w_s, b_s, conv_w, conv_b, conv_ln_g, conv_ln_b, w_out, ln1_g, ln1_b, w_gate, w_up, w_down, ln2_g, ln2_b, loss_target, m_w_in, m_sgu_ln_g, m_sgu_ln_b, m_w_s, m_b_s, m_conv_w, m_conv_b, m_conv_ln_g, m_conv_ln_b, m_w_out, m_ln1_g, m_ln1_b, m_w_gate, m_w_up, m_w_down, m_ln2_g, m_ln2_b, v_w_in, v_sgu_ln_g, v_sgu_ln_b, v_w_s, v_b_s, v_conv_w, v_conv_b, v_conv_ln_g, v_conv_ln_b, v_w_out, v_ln1_g, v_ln1_b, v_w_gate, v_w_up, v_w_down, v_ln2_g, v_ln2_b):
    xs = x[0]
    tgt = loss_target[0]

    (win_b, wout_b, wgt_b, wut_b, wd_b, cw_b, wcat, wcatt, bs_full) = _prep_weights(
        w_in[0], w_out[0], w_gate[0], w_up[0], w_down[0], conv_w[0], w_s[0], b_s[0])
    win_g, wout_g, wgt_g, wut_g, wd_g, cw_g = _exchange(
        "gather_weights", [], [win_b, wout_b, wgt_b, wut_b, wd_b, cw_b])
    wout_g = wout_g.reshape(D_MODEL, D_MODEL)
    wgt_g = wgt_g.reshape(D_FF, D_MODEL)
    wut_g = wut_g.reshape(D_FF, D_MODEL)
    wd_g = wd_g.reshape(D_FF, D_MODEL)
    cw = jnp.transpose(cw_g[:, :, :D_CONV // N_DEV], (1, 0, 2)).reshape(CONV_ROWS, D_CONV)

    proj, ycat, n1, rstd1 = _fwd_mix(xs, win_g, wout_g, sgu_ln_g, sgu_ln_b, wcat, bs_full, cw, conv_b,
                                     conv_ln_g, conv_ln_b, TOKEN_TILE_MIX)
    gate, up, x1b, dr2, vout = _fwd_mlp(n1, tgt, ln1_g, ln1_b, ln2_g, ln2_b, wgt_g, wut_g, wd_g, TOKEN_TILE_MLP)

    dgate, dup, hh, dr1, vmlp = _bwd_mlp(dr2, gate, up, n1, rstd1, ln1_g, wgt_g, wut_g, wd_g, TOKEN_TILE_MLP)
    gx, dproj, gws, gbs, gcw, vmix = _bwd_mix(dr1, proj, win_g, wout_g, sgu_ln_g, sgu_ln_b, wcat, wcatt, bs_full,
                                              cw, conv_b, conv_ln_g, conv_ln_b, TOKEN_TILE_MIX)
    tk = TOKEN_TILE_WGRAD
    g_wgt = _wgrad("wgrad_gate", dgate, x1b, N_DEV, tk)
    g_wut = _wgrad("wgrad_up", dup, x1b, N_DEV, tk)
    g_wd = _wgrad("wgrad_down", hh, dr2, N_DEV, tk)
    g_wout = _wgrad("wgrad_out", ycat, dr1, N_DEV, tk)
    g_win = _wgrad("wgrad_in", xs, dproj, -N_DEV, tk)

    r_win, r_wout, r_wgt, r_wut, r_wd, gws8, gbs8, gcw8, vmix8, vmlp8, vout8 = _exchange(
        "exchange_grads", [g_win, g_wout, g_wgt, g_wut, g_wd], [gws, gbs, gcw, vmix, vmlp, vout])

    big = {
        "w_in": _adamw_shard("adamw_in", r_win, w_in[0], m_w_in[0], v_w_in[0], False),
        "w_out": _adamw_shard("adamw_out", r_wout, w_out[0], m_w_out[0], v_w_out[0], False),
        "w_gate": _adamw_shard("adamw_gate", r_wgt, w_gate[0], m_w_gate[0], v_w_gate[0], True),
        "w_up": _adamw_shard("adamw_up", r_wut, w_up[0], m_w_up[0], v_w_up[0], True),
        "w_down": _adamw_shard("adamw_down", r_wd, w_down[0], m_w_down[0], v_w_down[0], False),
    }
    small_in = {
        "sgu_ln_g": (sgu_ln_g, m_sgu_ln_g, v_sgu_ln_g), "sgu_ln_b": (sgu_ln_b, m_sgu_ln_b, v_sgu_ln_b),
        "w_s": tuple(a.reshape(N_HEADS * CHUNK, CHUNK) for a in (w_s, m_w_s, v_w_s)),
        "b_s": (b_s[0], m_b_s[0], v_b_s[0]),
        "conv_b": (conv_b, m_conv_b, v_conv_b), "conv_ln_g": (conv_ln_g, m_conv_ln_g, v_conv_ln_g),
        "conv_ln_b": (conv_ln_b, m_conv_ln_b, v_conv_ln_b),
        "ln1_g": (ln1_g, m_ln1_g, v_ln1_g), "ln1_b": (ln1_b, m_ln1_b, v_ln1_b),
        "ln2_g": (ln2_g, m_ln2_g, v_ln2_g), "ln2_b": (ln2_b, m_ln2_b, v_ln2_b),
    }
    loss11, gcw_full, small = _finish_small(gws8, gbs8, gcw8, vmix8, vmlp8, vout8, small_in)

    me = 4 * lax.axis_index("x") + 2 * lax.axis_index("y") + lax.axis_index("c")
    g_cw = lax.dynamic_slice(gcw_full, (0, me * (D_CONV // N_DEV)), (CONV_WIDTH, D_CONV // N_DEV))
    d_cw, m_cw, v_cw = _adamw_plain("adamw_conv_w", g_cw, conv_w[0], m_conv_w[0], v_conv_w[0])

    shapes = {"w_s": w_s.shape, "b_s": b_s.shape}
    out = {}
    for n, r in big.items():
        out[n] = tuple(a[None] for a in r)
    for n, r in small.items():
        out[n] = tuple(a.reshape(shapes[n]) for a in r) if n in shapes else tuple(r)
    out["conv_w"] = tuple(a[None] for a in (g_cw, d_cw, m_cw, v_cw))

    order = ["w_in", "sgu_ln_g", "sgu_ln_b", "w_s", "b_s", "conv_w", "conv_b", "conv_ln_g", "conv_ln_b", "w_out",
             "ln1_g", "ln1_b", "w_gate", "w_up", "w_down", "ln2_g", "ln2_b"]
    loss = loss11[0, 0]
    return (loss, gx[None], *[out[n][0] for n in order], *[out[n][1] for n in order],
            *[out[n][2] for n in order], *[out[n][3] for n in order])
```

```python
import jax
import jax.numpy as jnp
from jax import lax
from jax.experimental import pallas as pl
from jax.experimental.pallas import tpu as pltpu

F32 = jnp.float32
BF16 = jnp.bfloat16

D_MODEL = 1024
D_SGU = 512
D_CONV = 512
N_HEADS = 8
CHUNK = 128
CONV_WIDTH = 31
CONV_ROWS = 32
HALO = 32
D_FF = 2816
N_DEV = 8
FF_SHARD = D_FF // N_DEV
ALPHA = (2.0 * 1) ** 0.25
LN_EPS = 1e-5
INV_SQRT2 = 0.7071067811865476
INV_SQRT_2PI = 0.3989422804014327

ADAM_LR = 0.001
ADAM_B1 = 0.9
ADAM_B2 = 0.999
ADAM_EPS = 1e-08
ADAM_WD = 0.01
ADAM_STEP = 10

MXU_COLS = 256
CONV_ROW_BLOCK = 64
MIB = 1024 * 1024

VMEM = pl.BlockSpec(memory_space=pltpu.VMEM)
ANY = pl.BlockSpec(memory_space=pl.ANY)
MESH = pl.DeviceIdType.MESH


def _params(vmem_mib, grid_dims=0):
    kw = dict(vmem_limit_bytes=vmem_mib * MIB)
    if grid_dims:
        kw["dimension_semantics"] = ("arbitrary",) * grid_dims
    return pltpu.CompilerParams(**kw)


def _full(shape):
    return pl.BlockSpec(shape, lambda i: (0,) * len(shape))


def _dot(a, b):
    return jnp.dot(a, b, preferred_element_type=F32)


def _dot_nt(a, b):
    return lax.dot_general(a, b, (((1,), (1,)), ((), ())), preferred_element_type=F32)


def _dot_tn(a, b):
    return lax.dot_general(a, b, (((0,), (0,)), ((), ())), preferred_element_type=F32)


def _gelu(x):
    return 0.5 * x * (1.0 + lax.erf(x * INV_SQRT2))


def _gelu_grad(x):
    return 0.5 * (1.0 + lax.erf(x * INV_SQRT2)) + x * jnp.exp(-0.5 * x * x) * INV_SQRT_2PI


def _ln_fwd(v):
    mu = jnp.mean(v, axis=-1, keepdims=True)
    d = v - mu
    var = jnp.mean(d * d, axis=-1, keepdims=True)
    rstd = lax.rsqrt(var + LN_EPS)
    return d * rstd, rstd


def _ln_bwd(dyhat, yhat, rstd):
    m1 = jnp.mean(dyhat, axis=-1, keepdims=True)
    m2 = jnp.mean(dyhat * yhat, axis=-1, keepdims=True)
    return rstd * (dyhat - m1 - yhat * m2)


def _colsum(v):
    return jnp.sum(v, axis=0, keepdims=True)


def _head_pair_stack(v, lo):
    return jnp.concatenate([jnp.where(lo, v, 0.0), jnp.where(lo, 0.0, v)], axis=0).astype(BF16)


def _lo_mask():
    return lax.broadcasted_iota(jnp.int32, (CHUNK, CHUNK), 1) < (CHUNK // 2)


def _head_selector():
    head = lax.broadcasted_iota(jnp.int32, (N_HEADS, D_SGU), 0)
    lane = lax.broadcasted_iota(jnp.int32, (N_HEADS, D_SGU), 1)
    width = D_SGU // N_HEADS
    return ((lane >= head * width) & (lane < (head + 1) * width)).astype(F32)


def _causal_conv(pad_ref, w_ref, out_ref, rows, offset_of_tap, bias=None):
    for r in range(0, rows, CONV_ROW_BLOCK):
        if bias is None:
            acc = jnp.zeros((CONV_ROW_BLOCK, D_CONV), F32)
        else:
            acc = jnp.broadcast_to(bias, (CONV_ROW_BLOCK, D_CONV))
        for k in range(CONV_WIDTH):
            acc = acc + w_ref[k:k + 1, :] * pad_ref[pl.ds(r + offset_of_tap(k), CONV_ROW_BLOCK), :]
        out_ref[r:r + CONV_ROW_BLOCK, :] = acc


def _prep_weights(w_in, w_out, w_gate, w_up, w_down, conv_w, w_s, b_s):
    def body(win_ref, wout_ref, wg_ref, wu_ref, wd_ref, cw_ref, ws_ref, bs_ref,
             win_o, wout_o, wgt_o, wut_o, wd_o, cw_o, wcat_o, wcatt_o, bsf_o):
        win_o[...] = win_ref[...].astype(BF16)
        wout_o[...] = wout_ref[...].astype(BF16)
        wgt_o[...] = wg_ref[...].T.astype(BF16)
        wut_o[...] = wu_ref[...].T.astype(BF16)
        wd_o[...] = wd_ref[...].astype(BF16)
        cw_o[...] = jnp.zeros(cw_o.shape, F32)
        cw_o[0:CONV_WIDTH, 0:D_CONV // N_DEV] = cw_ref[...]
        row = lax.broadcasted_iota(jnp.int32, (CHUNK, CHUNK), 0)
        col = lax.broadcasted_iota(jnp.int32, (CHUNK, CHUNK), 1)
        causal = row >= col
        for h in range(N_HEADS):
            w = jnp.where(causal, ws_ref[h], 0.0)
            p, half = h // 2, (h % 2) * CHUNK
            wcat_o[p, :, half:half + CHUNK] = w.astype(BF16)
            wcatt_o[p, :, half:half + CHUNK] = w.T.astype(BF16)
        bsf_o[...] = lax.dot_general(bs_ref[...], _head_selector(), (((0,), (0,)), ((), ())),
                                     preferred_element_type=F32, precision=lax.Precision.HIGHEST)

    S = jax.ShapeDtypeStruct
    return pl.pallas_call(
        body, name="prep_weights",
        out_shape=(S((D_MODEL, 256), BF16), S((128, D_MODEL), BF16), S((FF_SHARD, D_MODEL), BF16),
                   S((FF_SHARD, D_MODEL), BF16), S((FF_SHARD, D_MODEL), BF16), S((CONV_ROWS, 128), F32),
                   S((4, CHUNK, 2 * CHUNK), BF16), S((4, CHUNK, 2 * CHUNK), BF16), S((CHUNK, D_SGU), F32)),
        in_specs=[VMEM] * 8, out_specs=(VMEM,) * 9,
        compiler_params=_params(32),
    )(w_in, w_out, w_gate, w_up, w_down, conv_w, w_s, b_s)


def _mesh_position():
    x, y, c = lax.axis_index("x"), lax.axis_index("y"), lax.axis_index("c")
    return x, y, c


def _peers(x, y, c):
    out = []
    for k in range(1, N_DEV):
        px = 1 - x if (k >> 2) & 1 else x
        py = 1 - y if (k >> 1) & 1 else y
        pc = 1 - c if k & 1 else c
        out.append(((px, py, pc), 4 * px + 2 * py + pc))
    return out


class _Exchange:
    def __init__(self, scatter, gather):
        self.arrays = list(scatter) + list(gather)
        self.n_sc = len(scatter)
        self.n = len(self.arrays)
        self.out_shape = [jax.ShapeDtypeStruct(a.shape if k < self.n_sc else (N_DEV,) + a.shape, a.dtype)
                          for k, a in enumerate(self.arrays)]
        n_remote = self.n * (N_DEV - 1)
        self.scratch = [pltpu.SemaphoreType.DMA((n_remote,)), pltpu.SemaphoreType.DMA((n_remote,)),
                        pltpu.SemaphoreType.DMA((self.n,))] if self.n else []

    def _copies(self, src, dst, sems):
        send_sems, recv_sems, local_sems = sems
        x, y, c = _mesh_position()
        me = 4 * x + 2 * y + c
        peers = _peers(x, y, c)

        def piece(a, blk):
            return src[a].at[blk] if a < self.n_sc else src[a]

        sends, recvs, locals_ = [], [], []
        for a in range(self.n):
            locals_.append(pltpu.make_async_copy(piece(a, me), dst[a].at[me], local_sems.at[a]))
            for k, (peer, pid) in enumerate(peers):
                s = a * (N_DEV - 1) + k
                sends.append(pltpu.make_async_remote_copy(src_ref=piece(a, pid), dst_ref=dst[a].at[me],
                                                          send_sem=send_sems.at[s], recv_sem=recv_sems.at[s],
                                                          device_id=peer, device_id_type=MESH))
                recvs.append(pltpu.make_async_remote_copy(src_ref=piece(a, pid), dst_ref=dst[a].at[pid],
                                                          send_sem=send_sems.at[s], recv_sem=recv_sems.at[s],
                                                          device_id=peer, device_id_type=MESH))
        return locals_, sends, recvs

    def start(self, src, dst, sems):
        locals_, sends, _ = self._copies(src, dst, sems)
        for cp in locals_ + sends:
            cp.start()

    def wait(self, src, dst, sems):
        locals_, sends, recvs = self._copies(src, dst, sems)
        for cp in recvs:
            cp.wait_recv()
        for cp in sends:
            cp.wait_send()
        for cp in locals_:
            cp.wait()


def _exchange(name, scatter, gather):
    ex = _Exchange(scatter, gather)
    n = ex.n

    def body(*refs):
        src, dst, sems = refs[:n], refs[n:2 * n], refs[2 * n:]
        ex.start(src, dst, sems)
        ex.wait(src, dst, sems)

    return pl.pallas_call(
        body, name=name, out_shape=tuple(ex.out_shape), in_specs=[ANY] * n, out_specs=(ANY,) * n,
        scratch_shapes=ex.scratch,
    )(*ex.arrays)


def _hosted(ex, refs, n_in, n_out):
    ins, ex_src = refs[:n_in], refs[n_in:n_in + ex.n]
    rest = refs[n_in + ex.n:]
    outs, ex_dst = rest[:n_out], rest[n_out:n_out + ex.n]
    rest = rest[n_out + ex.n:]
    n_own = len(rest) - len(ex.scratch)
    return ins, outs, rest[:n_own], (ex_src, ex_dst, rest[n_own:])


def _fwd_mix(x, win_g, wout_g, sgu_g, sgu_b, wcat, bs_full, cw, cb, cg, cbeta, tm, ex):
    T = x.shape[0]
    nt = T // tm

    def body(*refs):
        ins, outs, scratch, ex_refs = _hosted(ex, refs, 11, 4)
        x_ref, win_ref, wout_ref, sg_ref, sb_ref, wcat_ref, bs_ref, cw_ref, cb_ref, cg_ref, cbeta_ref = ins
        proj_ref, ycat_ref, n1_ref, rstd1_ref = outs
        hpad, ybuf = scratch
        i = pl.program_id(0)

        @pl.when(i == 0)
        def _():
            ex.start(*ex_refs)

        xf = x_ref[...]
        xb = xf.astype(BF16)
        for j in range(N_DEV):
            proj_ref[:, 256 * j:256 * (j + 1)] = _dot(xb, win_ref[j])
        u = _gelu(proj_ref[:, 0:D_SGU])
        v = _gelu(proj_ref[:, D_SGU:2 * D_SGU])
        vhat, _ = _ln_fwd(v)
        vn = vhat * sg_ref[...] + sb_ref[...]
        lo = _lo_mask()
        for c in range(tm // CHUNK):
            rows = slice(CHUNK * c, CHUNK * (c + 1))
            for p in range(4):
                lanes = slice(CHUNK * p, CHUNK * (p + 1))
                mixed = _dot(wcat_ref[p], _head_pair_stack(vn[rows, lanes], lo)) + bs_ref[:, lanes]
                ycat_ref[rows, lanes] = (u[rows, lanes] * mixed).astype(BF16)
        base = 2 * D_SGU
        a = proj_ref[:, base:base + D_CONV]
        g = proj_ref[:, base + D_CONV:base + 2 * D_CONV]

        @pl.when(i == 0)
        def _():
            hpad[0:HALO, :] = jnp.zeros((HALO, D_CONV), F32)

        hpad[HALO:HALO + tm, :] = a * jax.nn.sigmoid(g)
        _causal_conv(hpad, cw_ref, ybuf, tm, lambda k: HALO - (CONV_WIDTH - 1) + k, bias=cb_ref[...])
        hpad[0:HALO, :] = hpad[tm:tm + HALO, :]
        yhat, _ = _ln_fwd(ybuf[...])
        yn = yhat * cg_ref[...] + cbeta_ref[...]
        ycat_ref[:, D_SGU:D_SGU + D_CONV] = (yn * jax.nn.sigmoid(yn)).astype(BF16)
        r1 = ALPHA * xf + _dot(ycat_ref[...], wout_ref[...])
        n1, rstd1 = _ln_fwd(r1)
        n1_ref[...] = n1
        rstd1_ref[...] = rstd1

        @pl.when(i == nt - 1)
        def _():
            ex.wait(*ex_refs)

    S = jax.ShapeDtypeStruct
    row = lambda w: pl.BlockSpec((tm, w), lambda i: (i, 0))
    res = pl.pallas_call(
        body, name="fwd_mix", grid=(nt,),
        in_specs=[row(D_MODEL), _full(win_g.shape), _full(wout_g.shape), _full(sgu_g.shape), _full(sgu_b.shape),
                  _full(wcat.shape), _full(bs_full.shape), _full(cw.shape), _full(cb.shape), _full(cg.shape),
                  _full(cbeta.shape)] + [ANY] * ex.n,
        out_specs=(row(2 * D_MODEL), row(D_MODEL), row(D_MODEL), row(1)) + (ANY,) * ex.n,
        out_shape=(S((T, 2 * D_MODEL), F32), S((T, D_MODEL), BF16), S((T, D_MODEL), F32), S((T, 1), F32),
                   *ex.out_shape),
        scratch_shapes=[pltpu.VMEM((tm + HALO, D_CONV), F32), pltpu.VMEM((tm, D_CONV), F32)] + ex.scratch,
        compiler_params=_params(56, 1),
    )(x, win_g, wout_g, sgu_g, sgu_b, wcat, bs_full, cw, cb, cg, cbeta, *ex.arrays)
    return res[:4], res[4:]


def _load_resident(pairs, sems):
    cps = [pltpu.make_async_copy(s, d, sems.at[k]) for k, (s, d) in enumerate(pairs)]
    for cp in cps:
        cp.start()
    for cp in cps:
        cp.wait()


def _fwd_mlp(n1, tgt, l1g, l1b, l2g, l2b, wgt, wut, wd, tm):
    T = n1.shape[0]
    nt = T // tm
    nf = D_FF // MXU_COLS

    def body(n1_ref, tgt_ref, l1g_ref, l1b_ref, l2g_ref, l2b_ref, wg_hbm, wu_hbm, wd_hbm,
             gate_ref, up_ref, x1b_ref, dr2_ref, stat_ref, wg_s, wu_s, wd_s, sems):
        i = pl.program_id(0)

        @pl.when(i == 0)
        def _():
            _load_resident([(wg_hbm, wg_s), (wu_hbm, wu_s), (wd_hbm, wd_s)], sems)
            stat_ref[...] = jnp.zeros(stat_ref.shape, F32)

        x1 = n1_ref[...] * l1g_ref[...] + l1b_ref[...]
        x1b = x1.astype(BF16)
        x1b_ref[...] = x1b
        acc = jnp.zeros((tm, D_MODEL), F32)
        for f in range(nf):
            cols = slice(MXU_COLS * f, MXU_COLS * (f + 1))
            gt = _dot_nt(x1b, wg_s[cols, :])
            ut = _dot_nt(x1b, wu_s[cols, :])
            gate_ref[:, cols] = gt.astype(BF16)
            up_ref[:, cols] = ut.astype(BF16)
            hh = (gt * jax.nn.sigmoid(gt) * ut).astype(BF16)
            acc = acc + _dot(hh, wd_s[cols, :])
        r2 = ALPHA * x1 + acc
        n2, rstd2 = _ln_fwd(r2)
        x2 = n2 * l2g_ref[...] + l2b_ref[...]
        diff = x2 - tgt_ref[...]
        dx2 = diff * (1.0 / D_MODEL)
        stat_ref[0:1, :] += _colsum(diff * diff)
        stat_ref[1:2, :] += _colsum(dx2 * n2)
        stat_ref[2:3, :] += _colsum(dx2)
        dr2_ref[...] = _ln_bwd(dx2 * l2g_ref[...], n2, rstd2)

    S = jax.ShapeDtypeStruct
    row = lambda w: pl.BlockSpec((tm, w), lambda i: (i, 0))
    vec = _full((1, D_MODEL))
    return pl.pallas_call(
        body, name="fwd_mlp", grid=(nt,),
        in_specs=[row(D_MODEL), row(D_MODEL), vec, vec, vec, vec, ANY, ANY, ANY],
        out_specs=(row(D_FF), row(D_FF), row(D_MODEL), row(D_MODEL), _full((8, D_MODEL))),
        out_shape=(S((T, D_FF), BF16), S((T, D_FF), BF16), S((T, D_MODEL), BF16), S((T, D_MODEL), F32),
                   S((8, D_MODEL), F32)),
        scratch_shapes=[pltpu.VMEM((D_FF, D_MODEL), BF16)] * 3 + [pltpu.SemaphoreType.DMA((3,))],
        compiler_params=_params(56, 1),
    )(n1, tgt, l1g, l1b, l2g, l2b, wgt, wut, wd)


def _bwd_mlp(dr2, gate, up, n1, rstd1, l1g, wgt, wut, wd, tm):
    T = n1.shape[0]
    nt = T // tm
    nf = D_FF // MXU_COLS

    def body(dr2_ref, gate_ref, up_ref, n1_ref, rstd1_ref, l1g_ref, wg_hbm, wu_hbm, wd_hbm,
             dgate_ref, dup_ref, hh_ref, dr1_ref, stat_ref, wg_s, wu_s, wd_s, sems):
        i = pl.program_id(0)

        @pl.when(i == 0)
        def _():
            _load_resident([(wg_hbm, wg_s), (wu_hbm, wu_s), (wd_hbm, wd_s)], sems)
            stat_ref[...] = jnp.zeros(stat_ref.shape, F32)

        dr2 = dr2_ref[...]
        dr2b = dr2.astype(BF16)
        acc = jnp.zeros((tm, D_MODEL), F32)
        for f in range(nf):
            cols = slice(MXU_COLS * f, MXU_COLS * (f + 1))
            dhh = _dot_nt(dr2b, wd_s[cols, :])
            gt = gate_ref[:, cols].astype(F32)
            ut = up_ref[:, cols].astype(F32)
            sg = jax.nn.sigmoid(gt)
            silu = gt * sg
            dg = (dhh * ut * (sg * (1.0 + gt * (1.0 - sg)))).astype(BF16)
            du = (dhh * silu).astype(BF16)
            dgate_ref[:, cols] = dg
            dup_ref[:, cols] = du
            hh_ref[:, cols] = (silu * ut).astype(BF16)
            acc = acc + _dot(dg, wg_s[cols, :]) + _dot(du, wu_s[cols, :])
        dx1 = ALPHA * dr2 + acc
        n1 = n1_ref[...]
        stat_ref[0:1, :] += _colsum(dx1 * n1)
        stat_ref[1:2, :] += _colsum(dx1)
        dr1_ref[...] = _ln_bwd(dx1 * l1g_ref[...], n1, rstd1_ref[...])

    S = jax.ShapeDtypeStruct
    row = lambda w: pl.BlockSpec((tm, w), lambda i: (i, 0))
    return pl.pallas_call(
        body, name="bwd_mlp", grid=(nt,),
        in_specs=[row(D_MODEL), row(D_FF), row(D_FF), row(D_MODEL), row(1), _full((1, D_MODEL)), ANY, ANY, ANY],
        out_specs=(row(D_FF), row(D_FF), row(D_FF), row(D_MODEL), _full((8, D_MODEL))),
        out_shape=(S((T, D_FF), BF16), S((T, D_FF), BF16), S((T, D_FF), BF16), S((T, D_MODEL), F32),
                   S((8, D_MODEL), F32)),
        scratch_shapes=[pltpu.VMEM((D_FF, D_MODEL), BF16)] * 3 + [pltpu.SemaphoreType.DMA((3,))],
        compiler_params=_params(56, 1),
    )(dr2, gate, up, n1, rstd1, l1g, wgt, wut, wd)


def _bwd_mix(dr1, proj, win_g, wout_g, sgu_g, sgu_b, wcat, wcatt, bs_full, cw, cb, cg, cbeta, tm, ex):
    T = dr1.shape[0]
    nt = T // tm
    halo_blocks = tm // HALO

    def body(*refs):
        ins, outs, scratch, ex_refs = _hosted(ex, refs, 14, 6)
        (dr1_ref, proj_ref, halo_ref, win_ref, wout_ref, sg_ref, sb_ref, wcat_ref, wcatt_ref, bs_ref,
         cw_ref, cb_ref, cg_ref, cbeta_ref) = ins
        gx_ref, dproj_ref, gws_ref, gbs_ref, gcw_ref, vec_ref = outs
        hpad, dypad, ybuf, dhbuf, dubuf, dvnbuf, gcw_acc = scratch
        i = pl.program_id(0)
        tile = nt - 1 - i

        @pl.when(i == 0)
        def _():
            ex.start(*ex_refs)
            gws_ref[...] = jnp.zeros(gws_ref.shape, F32)
            gbs_ref[...] = jnp.zeros(gbs_ref.shape, F32)
            gcw_ref[...] = jnp.zeros(gcw_ref.shape, F32)
            vec_ref[...] = jnp.zeros(vec_ref.shape, F32)
            gcw_acc[...] = jnp.zeros(gcw_acc.shape, F32)
            dypad[tm:tm + HALO, :] = jnp.zeros((HALO, D_CONV), F32)

        dr1 = dr1_ref[...]
        dycat = _dot_nt(dr1.astype(BF16), wout_ref[...])
        pu = proj_ref[:, 0:D_SGU]
        pv = proj_ref[:, D_SGU:2 * D_SGU]
        u = _gelu(pu)
        vhat, rstd_v = _ln_fwd(_gelu(pv))
        vn = vhat * sg_ref[...] + sb_ref[...]
        lo = _lo_mask()
        for c in range(tm // CHUNK):
            rows = slice(CHUNK * c, CHUNK * (c + 1))
            for p in range(4):
                lanes = slice(CHUNK * p, CHUNK * (p + 1))
                vstack = _head_pair_stack(vn[rows, lanes], lo)
                mixed = _dot(wcat_ref[p], vstack) + bs_ref[:, lanes]
                d_a = dycat[rows, lanes]
                dubuf[rows, lanes] = d_a * mixed
                dm = d_a * u[rows, lanes]
                gbs_ref[:, lanes] += dm
                dstack = _head_pair_stack(dm, lo)
                gws_ref[2 * CHUNK * p:2 * CHUNK * (p + 1), :] += _dot_nt(dstack, vn[rows, lanes].astype(BF16))
                dvnbuf[rows, lanes] = _dot(wcatt_ref[p], dstack)
        dvn = dvnbuf[...]
        vec_ref[0:1, :] += _colsum(dvn * vhat)
        vec_ref[1:2, :] += _colsum(dvn)
        dv = _ln_bwd(dvn * sg_ref[...], vhat, rstd_v)
        dproj_ref[:, 0:D_SGU] = (dubuf[...] * _gelu_grad(pu)).astype(BF16)
        dproj_ref[:, D_SGU:2 * D_SGU] = (dv * _gelu_grad(pv)).astype(BF16)
        base = 2 * D_SGU
        a = proj_ref[:, base:base + D_CONV]
        sgm = jax.nn.sigmoid(proj_ref[:, base + D_CONV:base + 2 * D_CONV])
        h_before = halo_ref[:, 0:D_CONV] * jax.nn.sigmoid(halo_ref[:, D_CONV:2 * D_CONV])
        hpad[0:HALO, :] = jnp.where(tile > 0, h_before, 0.0)
        hpad[HALO:HALO + tm, :] = a * sgm
        _causal_conv(hpad, cw_ref, ybuf, tm, lambda k: HALO - (CONV_WIDTH - 1) + k, bias=cb_ref[...])
        yhat, rstd_y = _ln_fwd(ybuf[...])
        yn = yhat * cg_ref[...] + cbeta_ref[...]
        s = jax.nn.sigmoid(yn)
        dyn = dycat[:, D_SGU:D_SGU + D_CONV] * (s * (1.0 + yn * (1.0 - s)))
        vec_ref[3:4, :] += _colsum(dyn * yhat)
        vec_ref[4:5, :] += _colsum(dyn)
        dy = _ln_bwd(dyn * cg_ref[...], yhat, rstd_y)
        vec_ref[2:3, :] += _colsum(dy)
        dypad[0:tm, :] = dy
        _causal_conv(dypad, cw_ref, dhbuf, tm, lambda k: (CONV_WIDTH - 1) - k)
        for r in range(0, tm, CONV_ROW_BLOCK):
            dyb = dypad[r:r + CONV_ROW_BLOCK, :]
            for k in range(CONV_WIDTH):
                off = r + HALO - (CONV_WIDTH - 1) + k
                pr = dyb * hpad[pl.ds(off, CONV_ROW_BLOCK), :]
                gcw_acc[k] += pr.reshape(CONV_ROW_BLOCK // 8, 8, D_CONV).sum(axis=0)
        dypad[tm:tm + HALO, :] = dypad[0:HALO, :]
        dh = dhbuf[...]
        dproj_ref[:, base:base + D_CONV] = (dh * sgm).astype(BF16)
        dproj_ref[:, base + D_CONV:base + 2 * D_CONV] = (dh * a * sgm * (1.0 - sgm)).astype(BF16)
        gx = ALPHA * dr1
        for j in range(N_DEV):
            gx = gx + _dot_nt(dproj_ref[:, 256 * j:256 * (j + 1)], win_ref[j])
        gx_ref[...] = gx

        @pl.when(i == nt - 1)
        def _():
            gcw_ref[...] = gcw_acc[...].sum(axis=1)
            ex.wait(*ex_refs)

    S = jax.ShapeDtypeStruct
    row = lambda w: pl.BlockSpec((tm, w), lambda i: (nt - 1 - i, 0))
    halo = pl.BlockSpec((HALO, D_MODEL), lambda i: (jnp.maximum((nt - 1 - i) * halo_blocks - 1, 0), 1))
    res = pl.pallas_call(
        body, name="bwd_mix", grid=(nt,),
        in_specs=[row(D_MODEL), row(2 * D_MODEL), halo, _full(win_g.shape), _full(wout_g.shape), _full(sgu_g.shape),
                  _full(sgu_b.shape), _full(wcat.shape), _full(wcatt.shape), _full(bs_full.shape), _full(cw.shape),
                  _full(cb.shape), _full(cg.shape), _full(cbeta.shape)] + [ANY] * ex.n,
        out_specs=(row(D_MODEL), row(2 * D_MODEL), _full((N_HEADS * CHUNK, CHUNK)), _full((CHUNK, D_SGU)),
                   _full((CONV_ROWS, D_CONV)), _full((8, D_CONV))) + (ANY,) * ex.n,
        out_shape=(S((T, D_MODEL), F32), S((T, 2 * D_MODEL), BF16), S((N_HEADS * CHUNK, CHUNK), F32),
                   S((CHUNK, D_SGU), F32), S((CONV_ROWS, D_CONV), F32), S((8, D_CONV), F32), *ex.out_shape),
        scratch_shapes=[pltpu.VMEM((tm + HALO, D_CONV), F32), pltpu.VMEM((tm + HALO, D_CONV), F32),
                        pltpu.VMEM((tm, D_CONV), F32), pltpu.VMEM((tm, D_CONV), F32), pltpu.VMEM((tm, D_SGU), F32),
                        pltpu.VMEM((tm, D_SGU), F32), pltpu.VMEM((CONV_ROWS, 8, D_CONV), F32)] + ex.scratch,
        compiler_params=_params(56, 1),
    )(dr1, proj, proj, win_g, wout_g, sgu_g, sgu_b, wcat, wcatt, bs_full, cw, cb, cg, cbeta, *ex.arrays)
    return res[:6], res[6:]


def _wgrad(name, a, b, blocks, tk, ex=None):
    T, M = a.shape
    N = b.shape[1]
    nk = T // tk
    by_rows = blocks > 0
    nb = abs(blocks)
    out_shape = (nb, M // nb, N) if by_rows else (nb, M, N // nb)
    ex = ex or _Exchange([], [])

    def body(*refs):
        (a_ref, b_ref), (o_ref,), (acc,), ex_refs = _hosted(ex, refs, 2, 1)
        i = pl.program_id(0)

        @pl.when(i == 0)
        def _():
            if ex.n:
                ex.start(*ex_refs)
            acc[...] = jnp.zeros(acc.shape, F32)

        acc[...] += _dot_tn(a_ref[...].astype(BF16), b_ref[...].astype(BF16))

        @pl.when(i == nk - 1)
        def _():
            for j in range(nb):
                if by_rows:
                    o_ref[j] = acc[(M // nb) * j:(M // nb) * (j + 1), :].astype(BF16)
                else:
                    o_ref[j] = acc[:, (N // nb) * j:(N // nb) * (j + 1)].astype(BF16)
            if ex.n:
                ex.wait(*ex_refs)

    res = pl.pallas_call(
        body, name=name, grid=(nk,),
        in_specs=[pl.BlockSpec((tk, M), lambda i: (i, 0)), pl.BlockSpec((tk, N), lambda i: (i, 0))] + [ANY] * ex.n,
        out_specs=(_full(out_shape),) + (ANY,) * ex.n,
        out_shape=(jax.ShapeDtypeStruct(out_shape, BF16), *ex.out_shape),
        scratch_shapes=[pltpu.VMEM((M, N), F32)] + ex.scratch,
        compiler_params=_params(56, 1),
    )(a, b, *ex.arrays)
    return (res[0], res[1:]) if ex.n else res[0]


def _adamw(w, g, m, v):
    m2 = ADAM_B1 * m + (1.0 - ADAM_B1) * g
    v2 = ADAM_B2 * v + (1.0 - ADAM_B2) * (g * g)
    m_hat = m2 / (1.0 - ADAM_B1 ** ADAM_STEP)
    v_hat = v2 / (1.0 - ADAM_B2 ** ADAM_STEP)
    delta = -ADAM_LR * (m_hat / (jnp.sqrt(v_hat) + ADAM_EPS) + ADAM_WD * w)
    return delta, m2, v2


def _sum_partials(r_ref):
    g = r_ref[0].astype(F32)
    for s in range(1, N_DEV):
        g = g + r_ref[s].astype(F32)
    return g


def _adamw_shard(name, parts, w, m, v, transposed):
    def body(r_ref, w_ref, m_ref, v_ref, g_o, d_o, m_o, v_o):
        g = _sum_partials(r_ref)
        if transposed:
            g = g.T
        delta, m2, v2 = _adamw(w_ref[...], g, m_ref[...], v_ref[...])
        g_o[...] = g
        d_o[...] = delta
        m_o[...] = m2
        v_o[...] = v2

    return pl.pallas_call(
        body, name=name, out_shape=(jax.ShapeDtypeStruct(w.shape, F32),) * 4,
        in_specs=[VMEM] * 4, out_specs=(VMEM,) * 4, compiler_params=_params(40),
    )(parts, w, m, v)


def _finish_small(gws8, gbs8, gcw8, vmix8, vmlp8, vout8, small):
    names = ["sgu_ln_g", "sgu_ln_b", "w_s", "b_s", "conv_b", "conv_ln_g", "conv_ln_b", "ln1_g", "ln1_b", "ln2_g", "ln2_b"]
    flat = []
    for n in names:
        flat += list(small[n])

    def body(*refs):
        gws_ref, gbs_ref, gcw_ref, vmix_ref, vmlp_ref, vout_ref = refs[:6]
        wmv = refs[6:6 + 3 * len(names)]
        outs = refs[6 + 3 * len(names):]
        loss_o, gcw_o = outs[0], outs[1]
        outs = outs[2:]
        gws = _sum_partials(gws_ref)
        gbs = _sum_partials(gbs_ref)
        vmix = _sum_partials(vmix_ref)
        vmlp = _sum_partials(vmlp_ref)
        vout = _sum_partials(vout_ref)
        gcw_o[...] = _sum_partials(gcw_ref)
        loss_o[...] = (0.5 / D_MODEL) * jnp.sum(vout[0:1, :], axis=1, keepdims=True)
        rows = lax.broadcasted_iota(jnp.int32, (N_HEADS * CHUNK, CHUNK), 0)
        cols = lax.broadcasted_iota(jnp.int32, (N_HEADS * CHUNK, CHUNK), 1)
        gws = jnp.where((rows & (CHUNK - 1)) >= cols, gws, 0.0)
        g_bs = lax.dot_general(_head_selector(), gbs, (((1,), (1,)), ((), ())), preferred_element_type=F32,
                               precision=lax.Precision.HIGHEST)
        grads = {
            "sgu_ln_g": vmix[0:1, :], "sgu_ln_b": vmix[1:2, :], "w_s": gws, "b_s": g_bs,
            "conv_b": vmix[2:3, :], "conv_ln_g": vmix[3:4, :], "conv_ln_b": vmix[4:5, :],
            "ln1_g": vmlp[0:1, :], "ln1_b": vmlp[1:2, :], "ln2_g": vout[1:2, :], "ln2_b": vout[2:3, :],
        }
        for k, n in enumerate(names):
            w_ref, m_ref, v_ref = wmv[3 * k:3 * k + 3]
            g = grads[n]
            delta, m2, v2 = _adamw(w_ref[...], g, m_ref[...], v_ref[...])
            outs[4 * k][...] = g
            outs[4 * k + 1][...] = delta
            outs[4 * k + 2][...] = m2
            outs[4 * k + 3][...] = v2

    S = jax.ShapeDtypeStruct
    out_shape = [S((1, 1), F32), S((CONV_ROWS, D_CONV), F32)]
    for n in names:
        out_shape += [S(small[n][0].shape, F32)] * 4
    res = pl.pallas_call(
        body, name="finish_small", out_shape=tuple(out_shape),
        in_specs=[VMEM] * (6 + len(flat)), out_specs=(VMEM,) * len(out_shape), compiler_params=_params(40),
    )(gws8, gbs8, gcw8, vmix8, vmlp8, vout8, *flat)
    upd = {n: res[2 + 4 * k:6 + 4 * k] for k, n in enumerate(names)}
    return res[0], res[1], upd


def _adamw_plain(name, g, w, m, v):
    def body(g_ref, w_ref, m_ref, v_ref, d_o, m_o, v_o):
        delta, m2, v2 = _adamw(w_ref[...], g_ref[...], m_ref[...], v_ref[...])
        d_o[...] = delta
        m_o[...] = m2
        v_o[...] = v2

    return pl.pallas_call(
        body, name=name, out_shape=(jax.ShapeDtypeStruct(w.shape, F32),) * 3,
        in_specs=[VMEM] * 4, out_specs=(VMEM,) * 3,
    )(g, w, m, v)


TOKEN_TILE_MIX = 256
TOKEN_TILE_MLP = 256
TOKEN_TILE_WGRAD = 512


def kernel(x, w_in, sgu_ln_g, sgu_ln_b, w_s, b_s, conv_w, conv_b, conv_ln_g, conv_ln_b, w_out, ln1_g, ln1_b, w_gate, w_up, w_down, ln2_g, ln2_b, loss_target, m_w_in, m_sgu_ln_g, m_sgu_ln_b, m_w_s, m_b_s, m_conv_w, m_conv_b, m_conv_ln_g, m_conv_ln_b, m_w_out, m_ln1_g, m_ln1_b, m_w_gate, m_w_up, m_w_down, m_ln2_g, m_ln2_b, v_w_in, v_sgu_ln_g, v_sgu_ln_b, v_w_s, v_b_s, v_conv_w, v_conv_b, v_conv_ln_g, v_conv_ln_b, v_w_out, v_ln1_g, v_ln1_b, v_w_gate, v_w_up, v_w_down, v_ln2_g, v_ln2_b):
    xs = x[0]
    tgt = loss_target[0]

    (win_b, wout_b, wgt_b, wut_b, wd_b, cw_b, wcat, wcatt, bs_full) = _prep_weights(
        w_in[0], w_out[0], w_gate[0], w_up[0], w_down[0], conv_w[0], w_s[0], b_s[0])
    win_g, wout_g, cw_g = _exchange("gather_mix_weights", [], [win_b, wout_b, cw_b])
    wout_g = wout_g.reshape(D_MODEL, D_MODEL)
    cw = jnp.transpose(cw_g[:, :, :D_CONV // N_DEV], (1, 0, 2)).reshape(CONV_ROWS, D_CONV)

    (proj, ycat, n1, rstd1), (wgt_g, wut_g, wd_g) = _fwd_mix(
        xs, win_g, wout_g, sgu_ln_g, sgu_ln_b, wcat, bs_full, cw, conv_b, conv_ln_g, conv_ln_b, TOKEN_TILE_MIX,
        _Exchange([], [wgt_b, wut_b, wd_b]))
    wgt_g = wgt_g.reshape(D_FF, D_MODEL)
    wut_g = wut_g.reshape(D_FF, D_MODEL)
    wd_g = wd_g.reshape(D_FF, D_MODEL)
    gate, up, x1b, dr2, vout = _fwd_mlp(n1, tgt, ln1_g, ln1_b, ln2_g, ln2_b, wgt_g, wut_g, wd_g, TOKEN_TILE_MLP)

    dgate, dup, hh, dr1, vmlp = _bwd_mlp(dr2, gate, up, n1, rstd1, ln1_g, wgt_g, wut_g, wd_g, TOKEN_TILE_MLP)
    tk = TOKEN_TILE_WGRAD
    g_wgt = _wgrad("wgrad_gate", dgate, x1b, N_DEV, tk)
    g_wut = _wgrad("wgrad_up", dup, x1b, N_DEV, tk)
    g_wd = _wgrad("wgrad_down", hh, dr2, N_DEV, tk)
    (gx, dproj, gws, gbs, gcw, vmix), (r_wgt, r_wut, r_wd) = _bwd_mix(
        dr1, proj, win_g, wout_g, sgu_ln_g, sgu_ln_b, wcat, wcatt, bs_full, cw, conv_b, conv_ln_g, conv_ln_b,
        TOKEN_TILE_MIX, _Exchange([g_wgt, g_wut, g_wd], []))
    g_wout = _wgrad("wgrad_out", ycat, dr1, N_DEV, tk)
    g_win, (r_wout, gws8, gbs8, gcw8, vmix8, vmlp8, vout8) = _wgrad(
        "wgrad_in", xs, dproj, -N_DEV, tk,
        _Exchange([g_wout], [gws, gbs, gcw, vmix, vmlp, vout]))
    (r_win,) = _exchange("exchange_grad_in", [g_win], [])

    big = {
        "w_in": _adamw_shard("adamw_in", r_win, w_in[0], m_w_in[0], v_w_in[0], False),
        "w_out": _adamw_shard("adamw_out", r_wout, w_out[0], m_w_out[0], v_w_out[0], False),
        "w_gate": _adamw_shard("adamw_gate", r_wgt, w_gate[0], m_w_gate[0], v_w_gate[0], True),
        "w_up": _adamw_shard("adamw_up", r_wut, w_up[0], m_w_up[0], v_w_up[0], True),
        "w_down": _adamw_shard("adamw_down", r_wd, w_down[0], m_w_down[0], v_w_down[0], False),
    }
    small_in = {
        "sgu_ln_g": (sgu_ln_g, m_sgu_ln_g, v_sgu_ln_g), "sgu_ln_b": (sgu_ln_b, m_sgu_ln_b, v_sgu_ln_b),
        "w_s": tuple(a.reshape(N_HEADS * CHUNK, CHUNK) for a in (w_s, m_w_s, v_w_s)),
        "b_s": (b_s[0], m_b_s[0], v_b_s[0]),
        "conv_b": (conv_b, m_conv_b, v_conv_b), "conv_ln_g": (conv_ln_g, m_conv_ln_g, v_conv_ln_g),
        "conv_ln_b": (conv_ln_b, m_conv_ln_b, v_conv_ln_b),
        "ln1_g": (ln1_g, m_ln1_g, v_ln1_g), "ln1_b": (ln1_b, m_ln1_b, v_ln1_b),
        "ln2_g": (ln2_g, m_ln2_g, v_ln2_g), "ln2_b": (ln2_b, m_ln2_b, v_ln2_b),
    }
    loss11, gcw_full, small = _finish_small(gws8, gbs8, gcw8, vmix8, vmlp8, vout8, small_in)

    me = 4 * lax.axis_index("x") + 2 * lax.axis_index("y") + lax.axis_index("c")
    g_cw = lax.dynamic_slice(gcw_full, (0, me * (D_CONV // N_DEV)), (CONV_WIDTH, D_CONV // N_DEV))
    d_cw, m_cw, v_cw = _adamw_plain("adamw_conv_w", g_cw, conv_w[0], m_conv_w[0], v_conv_w[0])

    shapes = {"w_s": w_s.shape, "b_s": b_s.shape}
    out = {}
    for n, r in big.items():
        out[n] = tuple(a[None] for a in r)
    for n, r in small.items():
        out[n] = tuple(a.reshape(shapes[n]) for a in r) if n in shapes else tuple(r)
    out["conv_w"] = tuple(a[None] for a in (g_cw, d_cw, m_cw, v_cw))

    order = ["w_in", "sgu_ln_g", "sgu_ln_b", "w_s", "b_s", "conv_w", "conv_b", "conv_ln_g", "conv_ln_b", "w_out",
             "ln1_g", "ln1_b", "w_gate", "w_up", "w_down", "ln2_g", "ln2_b"]
    loss = loss11[0, 0]
    return (loss, gx[None], *[out[n][0] for n in order], *[out[n][1] for n in order],
            *[out[n][2] for n in order], *[out[n][3] for n in order])
```

```python
import jax
import jax.numpy as jnp
from jax import lax
from jax.experimental import pallas as pl
from jax.experimental.pallas import tpu as pltpu

F32 = jnp.float32
BF16 = jnp.bfloat16

D_MODEL = 1024
D_SGU = 512
D_CONV = 512
N_HEADS = 8
CHUNK = 128
CONV_WIDTH = 31
CONV_ROWS = 32
HALO = 32
D_FF = 2816
N_DEV = 8
FF_SHARD = D_FF // N_DEV
ALPHA = (2.0 * 1) ** 0.25
LN_EPS = 1e-5
INV_SQRT2 = 0.7071067811865476
INV_SQRT_2PI = 0.3989422804014327

ADAM_LR = 0.001
ADAM_B1 = 0.9
ADAM_B2 = 0.999
ADAM_EPS = 1e-08
ADAM_WD = 0.01
ADAM_STEP = 10

MXU_COLS = 256
SUBLANES = 8
CONV_ROW_BLOCK = 32
WGRAD_ROW_BLOCK = 16
SHIFT_ROWS = HALO - SUBLANES
MIB = 1024 * 1024

VMEM = pl.BlockSpec(memory_space=pltpu.VMEM)
ANY = pl.BlockSpec(memory_space=pl.ANY)
MESH = pl.DeviceIdType.MESH


def _params(vmem_mib, grid_dims=0):
    kw = dict(vmem_limit_bytes=vmem_mib * MIB)
    if grid_dims:
        kw["dimension_semantics"] = ("arbitrary",) * grid_dims
    return pltpu.CompilerParams(**kw)


def _full(shape):
    return pl.BlockSpec(shape, lambda i: (0,) * len(shape))


def _dot(a, b):
    return jnp.dot(a, b, preferred_element_type=F32)


def _dot_nt(a, b):
    return lax.dot_general(a, b, (((1,), (1,)), ((), ())), preferred_element_type=F32)


def _dot_tn(a, b):
    return lax.dot_general(a, b, (((0,), (0,)), ((), ())), preferred_element_type=F32)


def _gelu(x):
    return 0.5 * x * (1.0 + lax.erf(x * INV_SQRT2))


def _gelu_grad(x):
    return 0.5 * (1.0 + lax.erf(x * INV_SQRT2)) + x * jnp.exp(-0.5 * x * x) * INV_SQRT_2PI


def _ln_fwd(v):
    mu = jnp.mean(v, axis=-1, keepdims=True)
    d = v - mu
    var = jnp.mean(d * d, axis=-1, keepdims=True)
    rstd = lax.rsqrt(var + LN_EPS)
    return d * rstd, rstd


def _ln_bwd(dyhat, yhat, rstd):
    m1 = jnp.mean(dyhat, axis=-1, keepdims=True)
    m2 = jnp.mean(dyhat * yhat, axis=-1, keepdims=True)
    return rstd * (dyhat - m1 - yhat * m2)


def _colsum(v):
    return jnp.sum(v, axis=0, keepdims=True)


def _head_pair_stack(v, lo):
    return jnp.concatenate([jnp.where(lo, v, 0.0), jnp.where(lo, 0.0, v)], axis=0).astype(BF16)


def _lo_mask():
    return lax.broadcasted_iota(jnp.int32, (CHUNK, CHUNK), 1) < (CHUNK // 2)


def _head_selector():
    head = lax.broadcasted_iota(jnp.int32, (N_HEADS, D_SGU), 0)
    lane = lax.broadcasted_iota(jnp.int32, (N_HEADS, D_SGU), 1)
    width = D_SGU // N_HEADS
    return ((lane >= head * width) & (lane < (head + 1) * width)).astype(F32)


def _shifted_copies(pad_ref, sh_ref, rows):
    for r in range(1, SUBLANES):
        sh_ref[r - 1, 0:rows, :] = pad_ref[pl.ds(r, rows), :]


def _tap_groups(offset_of_tap):
    groups = {}
    for k in range(CONV_WIDTH):
        o = offset_of_tap(k)
        groups.setdefault(o % SUBLANES, []).append((k, o // SUBLANES))
    return groups


def _tap_window(pad_ref, sh_ref, r, taps, row0, rows):
    q0 = min(q for _, q in taps)
    q1 = max(q for _, q in taps)
    src = pad_ref if r == 0 else sh_ref.at[r - 1]
    win = src[pl.ds(row0 + SUBLANES * q0, SUBLANES * (q1 - q0) + rows), :]
    return win, [(k, SUBLANES * (q - q0)) for k, q in taps]


def _causal_conv(pad_ref, sh_ref, w_ref, out_ref, rows, offset_of_tap, bias=None):
    groups = _tap_groups(offset_of_tap)

    def block(b, carry):
        row0 = pl.multiple_of(b * CONV_ROW_BLOCK, CONV_ROW_BLOCK)
        if bias is None:
            acc = jnp.zeros((CONV_ROW_BLOCK, D_CONV), F32)
        else:
            acc = jnp.broadcast_to(bias, (CONV_ROW_BLOCK, D_CONV))
        for r, taps in groups.items():
            win, starts = _tap_window(pad_ref, sh_ref, r, taps, row0, CONV_ROW_BLOCK)
            for k, s in starts:
                acc = acc + w_ref[k:k + 1, :] * win[s:s + CONV_ROW_BLOCK, :]
        out_ref[pl.ds(row0, CONV_ROW_BLOCK), :] = acc
        return carry

    lax.fori_loop(0, rows // CONV_ROW_BLOCK, block, 0)


def _conv_weight_grad(dy_ref, pad_ref, sh_ref, acc_ref, rows, offset_of_tap):
    groups = _tap_groups(offset_of_tap)
    for r, taps in groups.items():

        def block(b, parts, r=r, taps=taps):
            row0 = pl.multiple_of(b * WGRAD_ROW_BLOCK, WGRAD_ROW_BLOCK)
            dyb = dy_ref[pl.ds(row0, WGRAD_ROW_BLOCK), :]
            win, starts = _tap_window(pad_ref, sh_ref, r, taps, row0, WGRAD_ROW_BLOCK)
            out = []
            for part, (_, s) in zip(parts, starts):
                pr = dyb * win[s:s + WGRAD_ROW_BLOCK, :]
                out.append(part + pr.reshape(WGRAD_ROW_BLOCK // SUBLANES, SUBLANES, D_CONV).sum(axis=0))
            return tuple(out)

        zeros = tuple(jnp.zeros((SUBLANES, D_CONV), F32) for _ in taps)
        parts = lax.fori_loop(0, rows // WGRAD_ROW_BLOCK, block, zeros)
        for part, (k, _) in zip(parts, taps):
            acc_ref[k] += part


def _prep_weights(w_in, w_out, w_gate, w_up, w_down, conv_w, w_s, b_s):
    def body(win_ref, wout_ref, wg_ref, wu_ref, wd_ref, cw_ref, ws_ref, bs_ref,
             win_o, wout_o, wgt_o, wut_o, wd_o, cw_o, wcat_o, wcatt_o, bsf_o):
        win_o[...] = win_ref[...].astype(BF16)
        wout_o[...] = wout_ref[...].astype(BF16)
        wgt_o[...] = wg_ref[...].T.astype(BF16)
        wut_o[...] = wu_ref[...].T.astype(BF16)
        wd_o[...] = wd_ref[...].astype(BF16)
        cw_o[...] = jnp.zeros(cw_o.shape, F32)
        cw_o[0:CONV_WIDTH, 0:D_CONV // N_DEV] = cw_ref[...]
        row = lax.broadcasted_iota(jnp.int32, (CHUNK, CHUNK), 0)
        col = lax.broadcasted_iota(jnp.int32, (CHUNK, CHUNK), 1)
        causal = row >= col
        for h in range(N_HEADS):
            w = jnp.where(causal, ws_ref[h], 0.0)
            p, half = h // 2, (h % 2) * CHUNK
            wcat_o[p, :, half:half + CHUNK] = w.astype(BF16)
            wcatt_o[p, :, half:half + CHUNK] = w.T.astype(BF16)
        bsf_o[...] = lax.dot_general(bs_ref[...], _head_selector(), (((0,), (0,)), ((), ())),
                                     preferred_element_type=F32, precision=lax.Precision.HIGHEST)

    S = jax.ShapeDtypeStruct
    return pl.pallas_call(
        body, name="prep_weights",
        out_shape=(S((D_MODEL, 256), BF16), S((128, D_MODEL), BF16), S((FF_SHARD, D_MODEL), BF16),
                   S((FF_SHARD, D_MODEL), BF16), S((FF_SHARD, D_MODEL), BF16), S((CONV_ROWS, 128), F32),
                   S((4, CHUNK, 2 * CHUNK), BF16), S((4, CHUNK, 2 * CHUNK), BF16), S((CHUNK, D_SGU), F32)),
        in_specs=[VMEM] * 8, out_specs=(VMEM,) * 9,
        compiler_params=_params(32),
    )(w_in, w_out, w_gate, w_up, w_down, conv_w, w_s, b_s)


def _mesh_position():
    x, y, c = lax.axis_index("x"), lax.axis_index("y"), lax.axis_index("c")
    return x, y, c


def _peers(x, y, c):
    out = []
    for k in range(1, N_DEV):
        px = 1 - x if (k >> 2) & 1 else x
        py = 1 - y if (k >> 1) & 1 else y
        pc = 1 - c if k & 1 else c
        out.append(((px, py, pc), 4 * px + 2 * py + pc))
    return out


class _Exchange:
    def __init__(self, scatter, gather):
        self.arrays = list(scatter) + list(gather)
        self.n_sc = len(scatter)
        self.n = len(self.arrays)
        self.out_shape = [jax.ShapeDtypeStruct(a.shape if k < self.n_sc else (N_DEV,) + a.shape, a.dtype)
                          for k, a in enumerate(self.arrays)]
        n_remote = self.n * (N_DEV - 1)
        self.scratch = [pltpu.SemaphoreType.DMA((n_remote,)), pltpu.SemaphoreType.DMA((n_remote,)),
                        pltpu.SemaphoreType.DMA((self.n,))] if self.n else []

    def _copies(self, src, dst, sems):
        send_sems, recv_sems, local_sems = sems
        x, y, c = _mesh_position()
        me = 4 * x + 2 * y + c
        peers = _peers(x, y, c)

        def piece(a, blk):
            return src[a].at[blk] if a < self.n_sc else src[a]

        sends, recvs, locals_ = [], [], []
        for a in range(self.n):
            locals_.append(pltpu.make_async_copy(piece(a, me), dst[a].at[me], local_sems.at[a]))
            for k, (peer, pid) in enumerate(peers):
                s = a * (N_DEV - 1) + k
                sends.append(pltpu.make_async_remote_copy(src_ref=piece(a, pid), dst_ref=dst[a].at[me],
                                                          send_sem=send_sems.at[s], recv_sem=recv_sems.at[s],
                                                          device_id=peer, device_id_type=MESH))
                recvs.append(pltpu.make_async_remote_copy(src_ref=piece(a, pid), dst_ref=dst[a].at[pid],
                                                          send_sem=send_sems.at[s], recv_sem=recv_sems.at[s],
                                                          device_id=peer, device_id_type=MESH))
        return locals_, sends, recvs

    def start(self, src, dst, sems):
        if not self.n:
            return
        locals_, sends, _ = self._copies(src, dst, sems)
        for cp in locals_ + sends:
            cp.start()

    def wait(self, src, dst, sems):
        if not self.n:
            return
        locals_, sends, recvs = self._copies(src, dst, sems)
        for cp in recvs:
            cp.wait_recv()
        for cp in sends:
            cp.wait_send()
        for cp in locals_:
            cp.wait()


def _exchange(name, scatter, gather):
    ex = _Exchange(scatter, gather)
    n = ex.n

    def body(*refs):
        src, dst, sems = refs[:n], refs[n:2 * n], refs[2 * n:]
        ex.start(src, dst, sems)
        ex.wait(src, dst, sems)

    return pl.pallas_call(
        body, name=name, out_shape=tuple(ex.out_shape), in_specs=[ANY] * n, out_specs=(ANY,) * n,
        scratch_shapes=ex.scratch,
    )(*ex.arrays)


def _hosted(ex, refs, n_in, n_out):
    ins, ex_src = refs[:n_in], refs[n_in:n_in + ex.n]
    rest = refs[n_in + ex.n:]
    outs, ex_dst = rest[:n_out], rest[n_out:n_out + ex.n]
    rest = rest[n_out + ex.n:]
    n_own = len(rest) - len(ex.scratch)
    return ins, outs, rest[:n_own], (ex_src, ex_dst, rest[n_own:])


def _fwd_mix(x, win_g, wout_g, sgu_g, sgu_b, wcat, bs_full, cw, cb, cg, cbeta, tm, ex):
    T = x.shape[0]
    nt = T // tm

    def body(*refs):
        ins, outs, scratch, ex_refs = _hosted(ex, refs, 11, 4)
        x_ref, win_ref, wout_ref, sg_ref, sb_ref, wcat_ref, bs_ref, cw_ref, cb_ref, cg_ref, cbeta_ref = ins
        proj_ref, ycat_ref, n1_ref, rstd1_ref = outs
        hpad, hshift, ybuf = scratch
        i = pl.program_id(0)

        @pl.when(i == 0)
        def _():
            ex.start(*ex_refs)

        xf = x_ref[...]
        xb = xf.astype(BF16)
        for j in range(N_DEV):
            proj_ref[:, 256 * j:256 * (j + 1)] = _dot(xb, win_ref[j])
        u = _gelu(proj_ref[:, 0:D_SGU])
        v = _gelu(proj_ref[:, D_SGU:2 * D_SGU])
        vhat, _ = _ln_fwd(v)
        vn = vhat * sg_ref[...] + sb_ref[...]
        lo = _lo_mask()
        for c in range(tm // CHUNK):
            rows = slice(CHUNK * c, CHUNK * (c + 1))
            for p in range(4):
                lanes = slice(CHUNK * p, CHUNK * (p + 1))
                mixed = _dot(wcat_ref[p], _head_pair_stack(vn[rows, lanes], lo)) + bs_ref[:, lanes]
                ycat_ref[rows, lanes] = (u[rows, lanes] * mixed).astype(BF16)
        base = 2 * D_SGU
        a = proj_ref[:, base:base + D_CONV]
        g = proj_ref[:, base + D_CONV:base + 2 * D_CONV]

        @pl.when(i == 0)
        def _():
            hpad[0:HALO, :] = jnp.zeros((HALO, D_CONV), F32)

        hpad[HALO:HALO + tm, :] = a * jax.nn.sigmoid(g)
        _shifted_copies(hpad, hshift, tm + SHIFT_ROWS)
        _causal_conv(hpad, hshift, cw_ref, ybuf, tm, lambda k: HALO - (CONV_WIDTH - 1) + k, bias=cb_ref[...])
        hpad[0:HALO, :] = hpad[tm:tm + HALO, :]
        yhat, _ = _ln_fwd(ybuf[...])
        yn = yhat * cg_ref[...] + cbeta_ref[...]
        ycat_ref[:, D_SGU:D_SGU + D_CONV] = (yn * jax.nn.sigmoid(yn)).astype(BF16)
        r1 = ALPHA * xf + _dot(ycat_ref[...], wout_ref[...])
        n1, rstd1 = _ln_fwd(r1)
        n1_ref[...] = n1
        rstd1_ref[...] = rstd1

        @pl.when(i == nt - 1)
        def _():
            ex.wait(*ex_refs)

    S = jax.ShapeDtypeStruct
    row = lambda w: pl.BlockSpec((tm, w), lambda i: (i, 0))
    res = pl.pallas_call(
        body, name="fwd_mix", grid=(nt,),
        in_specs=[row(D_MODEL), _full(win_g.shape), _full(wout_g.shape), _full(sgu_g.shape), _full(sgu_b.shape),
                  _full(wcat.shape), _full(bs_full.shape), _full(cw.shape), _full(cb.shape), _full(cg.shape),
                  _full(cbeta.shape)] + [ANY] * ex.n,
        out_specs=(row(2 * D_MODEL), row(D_MODEL), row(D_MODEL), row(1)) + (ANY,) * ex.n,
        out_shape=(S((T, 2 * D_MODEL), F32), S((T, D_MODEL), BF16), S((T, D_MODEL), F32), S((T, 1), F32),
                   *ex.out_shape),
        scratch_shapes=[pltpu.VMEM((tm + HALO, D_CONV), F32), pltpu.VMEM((SUBLANES - 1, tm + SHIFT_ROWS, D_CONV), F32),
                        pltpu.VMEM((tm, D_CONV), F32)] + ex.scratch,
        compiler_params=_params(56, 1),
    )(x, win_g, wout_g, sgu_g, sgu_b, wcat, bs_full, cw, cb, cg, cbeta, *ex.arrays)
    return res[:4], res[4:]


def _load_resident(pairs, sems):
    cps = [pltpu.make_async_copy(s, d, sems.at[k]) for k, (s, d) in enumerate(pairs)]
    for cp in cps:
        cp.start()
    for cp in cps:
        cp.wait()


def _fwd_mlp(n1, tgt, l1g, l1b, l2g, l2b, wgt, wut, wd, tm):
    T = n1.shape[0]
    nt = T // tm
    nf = D_FF // MXU_COLS

    def body(n1_ref, tgt_ref, l1g_ref, l1b_ref, l2g_ref, l2b_ref, wg_hbm, wu_hbm, wd_hbm,
             gate_ref, up_ref, x1b_ref, dr2_ref, stat_ref, wg_s, wu_s, wd_s, sems):
        i = pl.program_id(0)

        @pl.when(i == 0)
        def _():
            _load_resident([(wg_hbm, wg_s), (wu_hbm, wu_s), (wd_hbm, wd_s)], sems)
            stat_ref[...] = jnp.zeros(stat_ref.shape, F32)

        x1 = n1_ref[...] * l1g_ref[...] + l1b_ref[...]
        x1b = x1.astype(BF16)
        x1b_ref[...] = x1b
        acc = jnp.zeros((tm, D_MODEL), F32)
        for f in range(nf):
            cols = slice(MXU_COLS * f, MXU_COLS * (f + 1))
            gt = _dot_nt(x1b, wg_s[cols, :])
            ut = _dot_nt(x1b, wu_s[cols, :])
            gate_ref[:, cols] = gt.astype(BF16)
            up_ref[:, cols] = ut.astype(BF16)
            hh = (gt * jax.nn.sigmoid(gt) * ut).astype(BF16)
            acc = acc + _dot(hh, wd_s[cols, :])
        r2 = ALPHA * x1 + acc
        n2, rstd2 = _ln_fwd(r2)
        x2 = n2 * l2g_ref[...] + l2b_ref[...]
        diff = x2 - tgt_ref[...]
        dx2 = diff * (1.0 / D_MODEL)
        stat_ref[0:1, :] += _colsum(diff * diff)
        stat_ref[1:2, :] += _colsum(dx2 * n2)
        stat_ref[2:3, :] += _colsum(dx2)
        dr2_ref[...] = _ln_bwd(dx2 * l2g_ref[...], n2, rstd2)

    S = jax.ShapeDtypeStruct
    row = lambda w: pl.BlockSpec((tm, w), lambda i: (i, 0))
    vec = _full((1, D_MODEL))
    return pl.pallas_call(
        body, name="fwd_mlp", grid=(nt,),
        in_specs=[row(D_MODEL), row(D_MODEL), vec, vec, vec, vec, ANY, ANY, ANY],
        out_specs=(row(D_FF), row(D_FF), row(D_MODEL), row(D_MODEL), _full((8, D_MODEL))),
        out_shape=(S((T, D_FF), BF16), S((T, D_FF), BF16), S((T, D_MODEL), BF16), S((T, D_MODEL), F32),
                   S((8, D_MODEL), F32)),
        scratch_shapes=[pltpu.VMEM((D_FF, D_MODEL), BF16)] * 3 + [pltpu.SemaphoreType.DMA((3,))],
        compiler_params=_params(56, 1),
    )(n1, tgt, l1g, l1b, l2g, l2b, wgt, wut, wd)


def _bwd_mlp(dr2, gate, up, n1, rstd1, l1g, wgt, wut, wd, tm):
    T = n1.shape[0]
    nt = T // tm
    nf = D_FF // MXU_COLS

    def body(dr2_ref, gate_ref, up_ref, n1_ref, rstd1_ref, l1g_ref, wg_hbm, wu_hbm, wd_hbm,
             dgate_ref, dup_ref, hh_ref, dr1_ref, stat_ref, wg_s, wu_s, wd_s, sems):
        i = pl.program_id(0)

        @pl.when(i == 0)
        def _():
            _load_resident([(wg_hbm, wg_s), (wu_hbm, wu_s), (wd_hbm, wd_s)], sems)
            stat_ref[...] = jnp.zeros(stat_ref.shape, F32)

        dr2 = dr2_ref[...]
        dr2b = dr2.astype(BF16)
        acc = jnp.zeros((tm, D_MODEL), F32)
        for f in range(nf):
            cols = slice(MXU_COLS * f, MXU_COLS * (f + 1))
            dhh = _dot_nt(dr2b, wd_s[cols, :])
            gt = gate_ref[:, cols].astype(F32)
            ut = up_ref[:, cols].astype(F32)
            sg = jax.nn.sigmoid(gt)
            silu = gt * sg
            dg = (dhh * ut * (sg * (1.0 + gt * (1.0 - sg)))).astype(BF16)
            du = (dhh * silu).astype(BF16)
            dgate_ref[:, cols] = dg
            dup_ref[:, cols] = du
            hh_ref[:, cols] = (silu * ut).astype(BF16)
            acc = acc + _dot(dg, wg_s[cols, :]) + _dot(du, wu_s[cols, :])
        dx1 = ALPHA * dr2 + acc
        n1 = n1_ref[...]
        stat_ref[0:1, :] += _colsum(dx1 * n1)
        stat_ref[1:2, :] += _colsum(dx1)
        dr1_ref[...] = _ln_bwd(dx1 * l1g_ref[...], n1, rstd1_ref[...])

    S = jax.ShapeDtypeStruct
    row = lambda w: pl.BlockSpec((tm, w), lambda i: (i, 0))
    return pl.pallas_call(
        body, name="bwd_mlp", grid=(nt,),
        in_specs=[row(D_MODEL), row(D_FF), row(D_FF), row(D_MODEL), row(1), _full((1, D_MODEL)), ANY, ANY, ANY],
        out_specs=(row(D_FF), row(D_FF), row(D_FF), row(D_MODEL), _full((8, D_MODEL))),
        out_shape=(S((T, D_FF), BF16), S((T, D_FF), BF16), S((T, D_FF), BF16), S((T, D_MODEL), F32),
                   S((8, D_MODEL), F32)),
        scratch_shapes=[pltpu.VMEM((D_FF, D_MODEL), BF16)] * 3 + [pltpu.SemaphoreType.DMA((3,))],
        compiler_params=_params(56, 1),
    )(dr2, gate, up, n1, rstd1, l1g, wgt, wut, wd)


def _bwd_mix(dr1, proj, win_g, wout_g, sgu_g, sgu_b, wcat, wcatt, bs_full, cw, cb, cg, cbeta, tm, ex):
    T = dr1.shape[0]
    nt = T // tm
    halo_blocks = tm // HALO

    def body(*refs):
        ins, outs, scratch, ex_refs = _hosted(ex, refs, 14, 6)
        (dr1_ref, proj_ref, halo_ref, win_ref, wout_ref, sg_ref, sb_ref, wcat_ref, wcatt_ref, bs_ref,
         cw_ref, cb_ref, cg_ref, cbeta_ref) = ins
        gx_ref, dproj_ref, gws_ref, gbs_ref, gcw_ref, vec_ref = outs
        hpad, hshift, dypad, dyshift, ybuf, dhbuf, dubuf, dvnbuf, gcw_acc = scratch
        i = pl.program_id(0)
        tile = nt - 1 - i

        @pl.when(i == 0)
        def _():
            ex.start(*ex_refs)
            gws_ref[...] = jnp.zeros(gws_ref.shape, F32)
            gbs_ref[...] = jnp.zeros(gbs_ref.shape, F32)
            gcw_ref[...] = jnp.zeros(gcw_ref.shape, F32)
            vec_ref[...] = jnp.zeros(vec_ref.shape, F32)
            gcw_acc[...] = jnp.zeros(gcw_acc.shape, F32)
            dypad[tm:tm + HALO, :] = jnp.zeros((HALO, D_CONV), F32)

        dr1 = dr1_ref[...]
        dycat = _dot_nt(dr1.astype(BF16), wout_ref[...])
        pu = proj_ref[:, 0:D_SGU]
        pv = proj_ref[:, D_SGU:2 * D_SGU]
        u = _gelu(pu)
        vhat, rstd_v = _ln_fwd(_gelu(pv))
        vn = vhat * sg_ref[...] + sb_ref[...]
        lo = _lo_mask()
        for c in range(tm // CHUNK):
            rows = slice(CHUNK * c, CHUNK * (c + 1))
            for p in range(4):
                lanes = slice(CHUNK * p, CHUNK * (p + 1))
                vstack = _head_pair_stack(vn[rows, lanes], lo)
                mixed = _dot(wcat_ref[p], vstack) + bs_ref[:, lanes]
                d_a = dycat[rows, lanes]
                dubuf[rows, lanes] = d_a * mixed
                dm = d_a * u[rows, lanes]
                gbs_ref[:, lanes] += dm
                dstack = _head_pair_stack(dm, lo)
                gws_ref[2 * CHUNK * p:2 * CHUNK * (p + 1), :] += _dot_nt(dstack, vn[rows, lanes].astype(BF16))
                dvnbuf[rows, lanes] = _dot(wcatt_ref[p], dstack)
        dvn = dvnbuf[...]
        vec_ref[0:1, :] += _colsum(dvn * vhat)
        vec_ref[1:2, :] += _colsum(dvn)
        dv = _ln_bwd(dvn * sg_ref[...], vhat, rstd_v)
        dproj_ref[:, 0:D_SGU] = (dubuf[...] * _gelu_grad(pu)).astype(BF16)
        dproj_ref[:, D_SGU:2 * D_SGU] = (dv * _gelu_grad(pv)).astype(BF16)
        base = 2 * D_SGU
        a = proj_ref[:, base:base + D_CONV]
        sgm = jax.nn.sigmoid(proj_ref[:, base + D_CONV:base + 2 * D_CONV])
        h_before = halo_ref[:, 0:D_CONV] * jax.nn.sigmoid(halo_ref[:, D_CONV:2 * D_CONV])
        hpad[0:HALO, :] = jnp.where(tile > 0, h_before, 0.0)
        hpad[HALO:HALO + tm, :] = a * sgm
        _shifted_copies(hpad, hshift, tm + SHIFT_ROWS)
        h_offset = lambda k: HALO - (CONV_WIDTH - 1) + k
        _causal_conv(hpad, hshift, cw_ref, ybuf, tm, h_offset, bias=cb_ref[...])
        yhat, rstd_y = _ln_fwd(ybuf[...])
        yn = yhat * cg_ref[...] + cbeta_ref[...]
        s = jax.nn.sigmoid(yn)
        dyn = dycat[:, D_SGU:D_SGU + D_CONV] * (s * (1.0 + yn * (1.0 - s)))
        vec_ref[3:4, :] += _colsum(dyn * yhat)
        vec_ref[4:5, :] += _colsum(dyn)
        dy = _ln_bwd(dyn * cg_ref[...], yhat, rstd_y)
        vec_ref[2:3, :] += _colsum(dy)
        dypad[0:tm, :] = dy
        _shifted_copies(dypad, dyshift, tm + SHIFT_ROWS)
        _causal_conv(dypad, dyshift, cw_ref, dhbuf, tm, lambda k: (CONV_WIDTH - 1) - k)
        _conv_weight_grad(dypad, hpad, hshift, gcw_acc, tm, h_offset)
        dypad[tm:tm + HALO, :] = dypad[0:HALO, :]
        dh = dhbuf[...]
        dproj_ref[:, base:base + D_CONV] = (dh * sgm).astype(BF16)
        dproj_ref[:, base + D_CONV:base + 2 * D_CONV] = (dh * a * sgm * (1.0 - sgm)).astype(BF16)
        gx = ALPHA * dr1
        for j in range(N_DEV):
            gx = gx + _dot_nt(dproj_ref[:, 256 * j:256 * (j + 1)], win_ref[j])
        gx_ref[...] = gx

        @pl.when(i == nt - 1)
        def _():
            gcw_ref[...] = gcw_acc[...].sum(axis=1)
            ex.wait(*ex_refs)

    S = jax.ShapeDtypeStruct
    row = lambda w: pl.BlockSpec((tm, w), lambda i: (nt - 1 - i, 0))
    halo = pl.BlockSpec((HALO, D_MODEL), lambda i: (jnp.maximum((nt - 1 - i) * halo_blocks - 1, 0), 1))
    res = pl.pallas_call(
        body, name="bwd_mix", grid=(nt,),
        in_specs=[row(D_MODEL), row(2 * D_MODEL), halo, _full(win_g.shape), _full(wout_g.shape), _full(sgu_g.shape),
                  _full(sgu_b.shape), _full(wcat.shape), _full(wcatt.shape), _full(bs_full.shape), _full(cw.shape),
                  _full(cb.shape), _full(cg.shape), _full(cbeta.shape)] + [ANY] * ex.n,
        out_specs=(row(D_MODEL), row(2 * D_MODEL), _full((N_HEADS * CHUNK, CHUNK)), _full((CHUNK, D_SGU)),
                   _full((CONV_ROWS, D_CONV)), _full((8, D_CONV))) + (ANY,) * ex.n,
        out_shape=(S((T, D_MODEL), F32), S((T, 2 * D_MODEL), BF16), S((N_HEADS * CHUNK, CHUNK), F32),
                   S((CHUNK, D_SGU), F32), S((CONV_ROWS, D_CONV), F32), S((8, D_CONV), F32), *ex.out_shape),
        scratch_shapes=[pltpu.VMEM((tm + HALO, D_CONV), F32), pltpu.VMEM((SUBLANES - 1, tm + SHIFT_ROWS, D_CONV), F32),
                        pltpu.VMEM((tm + HALO, D_CONV), F32), pltpu.VMEM((SUBLANES - 1, tm + SHIFT_ROWS, D_CONV), F32),
                        pltpu.VMEM((tm, D_CONV), F32), pltpu.VMEM((tm, D_CONV), F32), pltpu.VMEM((tm, D_SGU), F32),
                        pltpu.VMEM((tm, D_SGU), F32), pltpu.VMEM((CONV_ROWS, 8, D_CONV), F32)] + ex.scratch,
        compiler_params=_params(56, 1),
    )(dr1, proj, proj, win_g, wout_g, sgu_g, sgu_b, wcat, wcatt, bs_full, cw, cb, cg, cbeta, *ex.arrays)
    return res[:6], res[6:]


def _wgrad(name, a, b, blocks, tk, ex=None):
    T, M = a.shape
    N = b.shape[1]
    nk = T // tk
    by_rows = blocks > 0
    nb = abs(blocks)
    out_shape = (nb, M // nb, N) if by_rows else (nb, M, N // nb)
    ex = ex or _Exchange([], [])

    def body(*refs):
        (a_ref, b_ref), (o_ref,), (acc,), ex_refs = _hosted(ex, refs, 2, 1)
        i = pl.program_id(0)

        @pl.when(i == 0)
        def _():
            ex.start(*ex_refs)
            acc[...] = jnp.zeros(acc.shape, F32)

        acc[...] += _dot_tn(a_ref[...].astype(BF16), b_ref[...].astype(BF16))

        @pl.when(i == nk - 1)
        def _():
            for j in range(nb):
                if by_rows:
                    o_ref[j] = acc[(M // nb) * j:(M // nb) * (j + 1), :].astype(BF16)
                else:
                    o_ref[j] = acc[:, (N // nb) * j:(N // nb) * (j + 1)].astype(BF16)
            ex.wait(*ex_refs)

    res = pl.pallas_call(
        body, name=name, grid=(nk,),
        in_specs=[pl.BlockSpec((tk, M), lambda i: (i, 0)), pl.BlockSpec((tk, N), lambda i: (i, 0))] + [ANY] * ex.n,
        out_specs=(_full(out_shape),) + (ANY,) * ex.n,
        out_shape=(jax.ShapeDtypeStruct(out_shape, BF16), *ex.out_shape),
        scratch_shapes=[pltpu.VMEM((M, N), F32)] + ex.scratch,
        compiler_params=_params(56, 1),
    )(a, b, *ex.arrays)
    return (res[0], res[1:]) if ex.n else res[0]


def _adamw(w, g, m, v):
    m2 = ADAM_B1 * m + (1.0 - ADAM_B1) * g
    v2 = ADAM_B2 * v + (1.0 - ADAM_B2) * (g * g)
    m_hat = m2 / (1.0 - ADAM_B1 ** ADAM_STEP)
    v_hat = v2 / (1.0 - ADAM_B2 ** ADAM_STEP)
    delta = -ADAM_LR * (m_hat / (jnp.sqrt(v_hat) + ADAM_EPS) + ADAM_WD * w)
    return delta, m2, v2


def _sum_partials(r_ref):
    g = r_ref[0].astype(F32)
    for s in range(1, N_DEV):
        g = g + r_ref[s].astype(F32)
    return g


def _adamw_shard(name, parts, w, m, v, transposed):
    def body(r_ref, w_ref, m_ref, v_ref, g_o, d_o, m_o, v_o):
        g = _sum_partials(r_ref)
        if transposed:
            g = g.T
        delta, m2, v2 = _adamw(w_ref[...], g, m_ref[...], v_ref[...])
        g_o[...] = g
        d_o[...] = delta
        m_o[...] = m2
        v_o[...] = v2

    return pl.pallas_call(
        body, name=name, out_shape=(jax.ShapeDtypeStruct(w.shape, F32),) * 4,
        in_specs=[VMEM] * 4, out_specs=(VMEM,) * 4, compiler_params=_params(40),
    )(parts, w, m, v)


def _finish_small(gws8, gbs8, gcw8, vmix8, vmlp8, vout8, small):
    names = ["sgu_ln_g", "sgu_ln_b", "w_s", "b_s", "conv_b", "conv_ln_g", "conv_ln_b", "ln1_g", "ln1_b", "ln2_g", "ln2_b"]
    flat = []
    for n in names:
        flat += list(small[n])

    def body(*refs):
        gws_ref, gbs_ref, gcw_ref, vmix_ref, vmlp_ref, vout_ref = refs[:6]
        wmv = refs[6:6 + 3 * len(names)]
        outs = refs[6 + 3 * len(names):]
        loss_o, gcw_o = outs[0], outs[1]
        outs = outs[2:]
        gws = _sum_partials(gws_ref)
        gbs = _sum_partials(gbs_ref)
        vmix = _sum_partials(vmix_ref)
        vmlp = _sum_partials(vmlp_ref)
        vout = _sum_partials(vout_ref)
        gcw_o[...] = _sum_partials(gcw_ref)
        loss_o[...] = (0.5 / D_MODEL) * jnp.sum(vout[0:1, :], axis=1, keepdims=True)
        rows = lax.broadcasted_iota(jnp.int32, (N_HEADS * CHUNK, CHUNK), 0)
        cols = lax.broadcasted_iota(jnp.int32, (N_HEADS * CHUNK, CHUNK), 1)
        gws = jnp.where((rows & (CHUNK - 1)) >= cols, gws, 0.0)
        g_bs = lax.dot_general(_head_selector(), gbs, (((1,), (1,)), ((), ())), preferred_element_type=F32,
                               precision=lax.Precision.HIGHEST)
        grads = {
            "sgu_ln_g": vmix[0:1, :], "sgu_ln_b": vmix[1:2, :], "w_s": gws, "b_s": g_bs,
            "conv_b": vmix[2:3, :], "conv_ln_g": vmix[3:4, :], "conv_ln_b": vmix[4:5, :],
            "ln1_g": vmlp[0:1, :], "ln1_b": vmlp[1:2, :], "ln2_g": vout[1:2, :], "ln2_b": vout[2:3, :],
        }
        for k, n in enumerate(names):
            w_ref, m_ref, v_ref = wmv[3 * k:3 * k + 3]
            g = grads[n]
            delta, m2, v2 = _adamw(w_ref[...], g, m_ref[...], v_ref[...])
            outs[4 * k][...] = g
            outs[4 * k + 1][...] = delta
            outs[4 * k + 2][...] = m2
            outs[4 * k + 3][...] = v2

    S = jax.ShapeDtypeStruct
    out_shape = [S((1, 1), F32), S((CONV_ROWS, D_CONV), F32)]
    for n in names:
        out_shape += [S(small[n][0].shape, F32)] * 4
    res = pl.pallas_call(
        body, name="finish_small", out_shape=tuple(out_shape),
        in_specs=[VMEM] * (6 + len(flat)), out_specs=(VMEM,) * len(out_shape), compiler_params=_params(40),
    )(gws8, gbs8, gcw8, vmix8, vmlp8, vout8, *flat)
    upd = {n: res[2 + 4 * k:6 + 4 * k] for k, n in enumerate(names)}
    return res[0], res[1], upd


def _adamw_plain(name, g, w, m, v):
    def body(g_ref, w_ref, m_ref, v_ref, d_o, m_o, v_o):
        delta, m2, v2 = _adamw(w_ref[...], g_ref[...], m_ref[...], v_ref[...])
        d_o[...] = delta
        m_o[...] = m2
        v_o[...] = v2

    return pl.pallas_call(
        body, name=name, out_shape=(jax.ShapeDtypeStruct(w.shape, F32),) * 3,
        in_specs=[VMEM] * 4, out_specs=(VMEM,) * 3,
    )(g, w, m, v)


TOKEN_TILE_MIX = 256
TOKEN_TILE_MLP = 256
TOKEN_TILE_WGRAD = 512


def kernel(x, w_in, sgu_ln_g, sgu_ln_b, w_s, b_s, conv_w, conv_b, conv_ln_g, conv_ln_b, w_out, ln1_g, ln1_b, w_gate, w_up, w_down, ln2_g, ln2_b, loss_target, m_w_in, m_sgu_ln_g, m_sgu_ln_b, m_w_s, m_b_s, m_conv_w, m_conv_b, m_conv_ln_g, m_conv_ln_b, m_w_out, m_ln1_g, m_ln1_b, m_w_gate, m_w_up, m_w_down, m_ln2_g, m_ln2_b, v_w_in, v_sgu_ln_g, v_sgu_ln_b, v_w_s, v_b_s, v_conv_w, v_conv_b, v_conv_ln_g, v_conv_ln_b, v_w_out, v_ln1_g, v_ln1_b, v_w_gate, v_w_up, v_w_down, v_ln2_g, v_ln2_b):
    xs = x[0]
    tgt = loss_target[0]

    (win_b, wout_b, wgt_b, wut_b, wd_b, cw_b, wcat, wcatt, bs_full) = _prep_weights(
        w_in[0], w_out[0], w_gate[0], w_up[0], w_down[0], conv_w[0], w_s[0], b_s[0])
    win_g, wout_g, cw_g = _exchange("gather_mix_weights", [], [win_b, wout_b, cw_b])
    wout_g = wout_g.reshape(D_MODEL, D_MODEL)
    cw = jnp.transpose(cw_g[:, :, :D_CONV // N_DEV], (1, 0, 2)).reshape(CONV_ROWS, D_CONV)

    (proj, ycat, n1, rstd1), (wgt_g, wut_g, wd_g) = _fwd_mix(
        xs, win_g, wout_g, sgu_ln_g, sgu_ln_b, wcat, bs_full, cw, conv_b, conv_ln_g, conv_ln_b, TOKEN_TILE_MIX,
        _Exchange([], [wgt_b, wut_b, wd_b]))
    wgt_g = wgt_g.reshape(D_FF, D_MODEL)
    wut_g = wut_g.reshape(D_FF, D_MODEL)
    wd_g = wd_g.reshape(D_FF, D_MODEL)
    gate, up, x1b, dr2, vout = _fwd_mlp(n1, tgt, ln1_g, ln1_b, ln2_g, ln2_b, wgt_g, wut_g, wd_g, TOKEN_TILE_MLP)

    dgate, dup, hh, dr1, vmlp = _bwd_mlp(dr2, gate, up, n1, rstd1, ln1_g, wgt_g, wut_g, wd_g, TOKEN_TILE_MLP)
    tk = TOKEN_TILE_WGRAD
    g_wgt = _wgrad("wgrad_gate", dgate, x1b, N_DEV, tk)
    g_wut = _wgrad("wgrad_up", dup, x1b, N_DEV, tk)
    g_wd = _wgrad("wgrad_down", hh, dr2, N_DEV, tk)
    (gx, dproj, gws, gbs, gcw, vmix), (r_wgt, r_wut, r_wd) = _bwd_mix(
        dr1, proj, win_g, wout_g, sgu_ln_g, sgu_ln_b, wcat, wcatt, bs_full, cw, conv_b, conv_ln_g, conv_ln_b,
        TOKEN_TILE_MIX, _Exchange([g_wgt, g_wut, g_wd], []))
    g_wout = _wgrad("wgrad_out", ycat, dr1, N_DEV, tk)
    g_win, (r_wout, gws8, gbs8, gcw8, vmix8, vmlp8, vout8) = _wgrad(
        "wgrad_in", xs, dproj, -N_DEV, tk,
        _Exchange([g_wout], [gws, gbs, gcw, vmix, vmlp, vout]))
    (r_win,) = _exchange("exchange_grad_in", [g_win], [])

    big = {
        "w_in": _adamw_shard("adamw_in", r_win, w_in[0], m_w_in[0], v_w_in[0], False),
        "w_out": _adamw_shard("adamw_out", r_wout, w_out[0], m_w_out[0], v_w_out[0], False),
        "w_gate": _adamw_shard("adamw_gate", r_wgt, w_gate[0], m_w_gate[0], v_w_gate[0], True),
        "w_up": _adamw_shard("adamw_up", r_wut, w_up[0], m_w_up[0], v_w_up[0], True),
        "w_down": _adamw_shard("adamw_down", r_wd, w_down[0], m_w_down[0], v_w_down[0], False),
    }
    small_in = {
        "sgu_ln_g": (sgu_ln_g, m_sgu_ln_g, v_sgu_ln_g), "sgu_ln_b": (sgu_ln_b, m_sgu_ln_b, v_sgu_ln_b),
        "w_s": tuple(a.reshape(N_HEADS * CHUNK, CHUNK) for a in (w_s, m_w_s, v_w_s)),
        "b_s": (b_s[0], m_b_s[0], v_b_s[0]),
        "conv_b": (conv_b, m_conv_b, v_conv_b), "conv_ln_g": (conv_ln_g, m_conv_ln_g, v_conv_ln_g),
        "conv_ln_b": (conv_ln_b, m_conv_ln_b, v_conv_ln_b),
        "ln1_g": (ln1_g, m_ln1_g, v_ln1_g), "ln1_b": (ln1_b, m_ln1_b, v_ln1_b),
        "ln2_g": (ln2_g, m_ln2_g, v_ln2_g), "ln2_b": (ln2_b, m_ln2_b, v_ln2_b),
    }
    loss11, gcw_full, small = _finish_small(gws8, gbs8, gcw8, vmix8, vmlp8, vout8, small_in)

    me = 4 * lax.axis_index("x") + 2 * lax.axis_index("y") + lax.axis_index("c")
    g_cw = lax.dynamic_slice(gcw_full, (0, me * (D_CONV // N_DEV)), (CONV_WIDTH, D_CONV // N_DEV))
    d_cw, m_cw, v_cw = _adamw_plain("adamw_conv_w", g_cw, conv_w[0], m_conv_w[0], v_conv_w[0])

    shapes = {"w_s": w_s.shape, "b_s": b_s.shape}
    out = {}
    for n, r in big.items():
        out[n] = tuple(a[None] for a in r)
    for n, r in small.items():
        out[n] = tuple(a.reshape(shapes[n]) for a in r) if n in shapes else tuple(r)
    out["conv_w"] = tuple(a[None] for a in (g_cw, d_cw, m_cw, v_cw))

    order = ["w_in", "sgu_ln_g", "sgu_ln_b", "w_s", "b_s", "conv_w", "conv_b", "conv_ln_g", "conv_ln_b", "w_out",
             "ln1_g", "ln1_b", "w_gate", "w_up", "w_down", "ln2_g", "ln2_b"]
    loss = loss11[0, 0]
    return (loss, gx[None], *[out[n][0] for n in order], *[out[n][1] for n in order],
            *[out[n][2] for n in order], *[out[n][3] for n in order])
```

```python
import jax
import jax.numpy as jnp
from jax import lax
from jax.experimental import pallas as pl
from jax.experimental.pallas import tpu as pltpu

F32 = jnp.float32
BF16 = jnp.bfloat16

D_MODEL = 1024
D_SGU = 512
D_CONV = 512
N_HEADS = 8
CHUNK = 128
CONV_WIDTH = 31
CONV_ROWS = 32
HALO = 32
D_FF = 2816
N_DEV = 8
FF_SHARD = D_FF // N_DEV
ALPHA = (2.0 * 1) ** 0.25
LN_EPS = 1e-5
INV_SQRT2 = 0.7071067811865476
INV_SQRT_2PI = 0.3989422804014327

ADAM_LR = 0.001
ADAM_B1 = 0.9
ADAM_B2 = 0.999
ADAM_EPS = 1e-08
ADAM_WD = 0.01
ADAM_STEP = 10

MXU_COLS = 256
SUBLANES = 8
CONV_ROW_BLOCK = 32
WGRAD_ROW_BLOCK = 16
SHIFT_ROWS = HALO - SUBLANES
MIB = 1024 * 1024

VMEM = pl.BlockSpec(memory_space=pltpu.VMEM)
ANY = pl.BlockSpec(memory_space=pl.ANY)
MESH = pl.DeviceIdType.MESH


def _params(vmem_mib, grid_dims=0):
    kw = dict(vmem_limit_bytes=vmem_mib * MIB)
    if grid_dims:
        kw["dimension_semantics"] = ("arbitrary",) * grid_dims
    return pltpu.CompilerParams(**kw)


def _full(shape):
    return pl.BlockSpec(shape, lambda i: (0,) * len(shape))


def _dot(a, b):
    return jnp.dot(a, b, preferred_element_type=F32)


def _dot_nt(a, b):
    return lax.dot_general(a, b, (((1,), (1,)), ((), ())), preferred_element_type=F32)


def _dot_tn(a, b):
    return lax.dot_general(a, b, (((0,), (0,)), ((), ())), preferred_element_type=F32)


def _gelu(x):
    return 0.5 * x * (1.0 + lax.erf(x * INV_SQRT2))


def _gelu_grad(x):
    return 0.5 * (1.0 + lax.erf(x * INV_SQRT2)) + x * jnp.exp(-0.5 * x * x) * INV_SQRT_2PI


def _ln_fwd(v):
    mu = jnp.mean(v, axis=-1, keepdims=True)
    d = v - mu
    var = jnp.mean(d * d, axis=-1, keepdims=True)
    rstd = lax.rsqrt(var + LN_EPS)
    return d * rstd, rstd


def _ln_bwd(dyhat, yhat, rstd):
    m1 = jnp.mean(dyhat, axis=-1, keepdims=True)
    m2 = jnp.mean(dyhat * yhat, axis=-1, keepdims=True)
    return rstd * (dyhat - m1 - yhat * m2)


def _colsum(v):
    return jnp.sum(v, axis=0, keepdims=True)


def _head_pair_stack(v, lo):
    return jnp.concatenate([jnp.where(lo, v, 0.0), jnp.where(lo, 0.0, v)], axis=0).astype(BF16)


def _lo_mask():
    return lax.broadcasted_iota(jnp.int32, (CHUNK, CHUNK), 1) < (CHUNK // 2)


def _head_selector():
    head = lax.broadcasted_iota(jnp.int32, (N_HEADS, D_SGU), 0)
    lane = lax.broadcasted_iota(jnp.int32, (N_HEADS, D_SGU), 1)
    width = D_SGU // N_HEADS
    return ((lane >= head * width) & (lane < (head + 1) * width)).astype(F32)


def _shifted_copies(pad_ref, sh_ref, rows):
    for r in range(1, SUBLANES):
        sh_ref[r - 1, 0:rows, :] = pad_ref[pl.ds(r, rows), :]


def _tap_groups(offset_of_tap):
    groups = {}
    for k in range(CONV_WIDTH):
        o = offset_of_tap(k)
        groups.setdefault(o % SUBLANES, []).append((k, o // SUBLANES))
    return groups


def _tap_window(pad_ref, sh_ref, r, taps, row0, rows):
    q0 = min(q for _, q in taps)
    q1 = max(q for _, q in taps)
    src = pad_ref if r == 0 else sh_ref.at[r - 1]
    win = src[pl.ds(row0 + SUBLANES * q0, SUBLANES * (q1 - q0) + rows), :]
    return win, [(k, SUBLANES * (q - q0)) for k, q in taps]


def _causal_conv(pad_ref, sh_ref, w_ref, out_ref, rows, offset_of_tap, bias=None):
    groups = _tap_groups(offset_of_tap)

    def block(b, carry):
        row0 = pl.multiple_of(b * CONV_ROW_BLOCK, CONV_ROW_BLOCK)
        if bias is None:
            acc = jnp.zeros((CONV_ROW_BLOCK, D_CONV), F32)
        else:
            acc = jnp.broadcast_to(bias, (CONV_ROW_BLOCK, D_CONV))
        for r, taps in groups.items():
            win, starts = _tap_window(pad_ref, sh_ref, r, taps, row0, CONV_ROW_BLOCK)
            for k, s in starts:
                acc = acc + w_ref[k:k + 1, :] * win[s:s + CONV_ROW_BLOCK, :]
        out_ref[pl.ds(row0, CONV_ROW_BLOCK), :] = acc
        return carry

    lax.fori_loop(0, rows // CONV_ROW_BLOCK, block, 0)


def _conv_weight_grad(dy_ref, pad_ref, sh_ref, acc_ref, rows, offset_of_tap):
    groups = _tap_groups(offset_of_tap)
    for r, taps in groups.items():

        def block(b, parts, r=r, taps=taps):
            row0 = pl.multiple_of(b * WGRAD_ROW_BLOCK, WGRAD_ROW_BLOCK)
            dyb = dy_ref[pl.ds(row0, WGRAD_ROW_BLOCK), :]
            win, starts = _tap_window(pad_ref, sh_ref, r, taps, row0, WGRAD_ROW_BLOCK)
            out = []
            for part, (_, s) in zip(parts, starts):
                pr = dyb * win[s:s + WGRAD_ROW_BLOCK, :]
                out.append(part + pr.reshape(WGRAD_ROW_BLOCK // SUBLANES, SUBLANES, D_CONV).sum(axis=0))
            return tuple(out)

        zeros = tuple(jnp.zeros((SUBLANES, D_CONV), F32) for _ in taps)
        parts = lax.fori_loop(0, rows // WGRAD_ROW_BLOCK, block, zeros)
        for part, (k, _) in zip(parts, taps):
            acc_ref[k] += part


def _prep_weights(w_in, w_out, w_gate, w_up, w_down, conv_w, w_s, b_s):
    def body(win_ref, wout_ref, wg_ref, wu_ref, wd_ref, cw_ref, ws_ref, bs_ref,
             win_o, wout_o, wgt_o, wut_o, wd_o, cw_o, wcat_o, wcatt_o, bsf_o):
        win_o[...] = win_ref[...].T.astype(BF16)
        wout_o[...] = wout_ref[...].astype(BF16)
        wgt_o[...] = wg_ref[...].T.astype(BF16)
        wut_o[...] = wu_ref[...].T.astype(BF16)
        wd_o[...] = wd_ref[...].astype(BF16)
        cw_o[...] = jnp.zeros(cw_o.shape, F32)
        cw_o[0:CONV_WIDTH, 0:D_CONV // N_DEV] = cw_ref[...]
        row = lax.broadcasted_iota(jnp.int32, (CHUNK, CHUNK), 0)
        col = lax.broadcasted_iota(jnp.int32, (CHUNK, CHUNK), 1)
        causal = row >= col
        for h in range(N_HEADS):
            w = jnp.where(causal, ws_ref[h], 0.0)
            p, half = h // 2, (h % 2) * CHUNK
            wcat_o[p, :, half:half + CHUNK] = w.astype(BF16)
            wcatt_o[p, :, half:half + CHUNK] = w.T.astype(BF16)
        bsf_o[...] = lax.dot_general(bs_ref[...], _head_selector(), (((0,), (0,)), ((), ())),
                                     preferred_element_type=F32, precision=lax.Precision.HIGHEST)

    S = jax.ShapeDtypeStruct
    return pl.pallas_call(
        body, name="prep_weights",
        out_shape=(S((256, D_MODEL), BF16), S((128, D_MODEL), BF16), S((FF_SHARD, D_MODEL), BF16),
                   S((FF_SHARD, D_MODEL), BF16), S((FF_SHARD, D_MODEL), BF16), S((CONV_ROWS, 128), F32),
                   S((4, CHUNK, 2 * CHUNK), BF16), S((4, CHUNK, 2 * CHUNK), BF16), S((CHUNK, D_SGU), F32)),
        in_specs=[VMEM] * 8, out_specs=(VMEM,) * 9,
        compiler_params=_params(32),
    )(w_in, w_out, w_gate, w_up, w_down, conv_w, w_s, b_s)


def _mesh_position():
    x, y, c = lax.axis_index("x"), lax.axis_index("y"), lax.axis_index("c")
    return x, y, c


def _peers(x, y, c):
    out = []
    for k in range(1, N_DEV):
        px = 1 - x if (k >> 2) & 1 else x
        py = 1 - y if (k >> 1) & 1 else y
        pc = 1 - c if k & 1 else c
        out.append(((px, py, pc), 4 * px + 2 * py + pc))
    return out


class _Exchange:
    def __init__(self, scatter, gather):
        self.arrays = list(scatter) + list(gather)
        self.n_sc = len(scatter)
        self.n = len(self.arrays)
        self.out_shape = [jax.ShapeDtypeStruct(a.shape if k < self.n_sc else (N_DEV,) + a.shape, a.dtype)
                          for k, a in enumerate(self.arrays)]
        n_remote = self.n * (N_DEV - 1)
        self.scratch = [pltpu.SemaphoreType.DMA((n_remote,)), pltpu.SemaphoreType.DMA((n_remote,)),
                        pltpu.SemaphoreType.DMA((self.n,))] if self.n else []

    def _copies(self, src, dst, sems):
        send_sems, recv_sems, local_sems = sems
        x, y, c = _mesh_position()
        me = 4 * x + 2 * y + c
        peers = _peers(x, y, c)

        def piece(a, blk):
            return src[a].at[blk] if a < self.n_sc else src[a]

        sends, recvs, locals_ = [], [], []
        for a in range(self.n):
            locals_.append(pltpu.make_async_copy(piece(a, me), dst[a].at[me], local_sems.at[a]))
            for k, (peer, pid) in enumerate(peers):
                s = a * (N_DEV - 1) + k
                sends.append(pltpu.make_async_remote_copy(src_ref=piece(a, pid), dst_ref=dst[a].at[me],
                                                          send_sem=send_sems.at[s], recv_sem=recv_sems.at[s],
                                                          device_id=peer, device_id_type=MESH))
                recvs.append(pltpu.make_async_remote_copy(src_ref=piece(a, pid), dst_ref=dst[a].at[pid],
                                                          send_sem=send_sems.at[s], recv_sem=recv_sems.at[s],
                                                          device_id=peer, device_id_type=MESH))
        return locals_, sends, recvs

    def start(self, src, dst, sems):
        if not self.n:
            return
        locals_, sends, _ = self._copies(src, dst, sems)
        for cp in locals_ + sends:
            cp.start()

    def wait(self, src, dst, sems):
        if not self.n:
            return
        locals_, sends, recvs = self._copies(src, dst, sems)
        for cp in recvs:
            cp.wait_recv()
        for cp in sends:
            cp.wait_send()
        for cp in locals_:
            cp.wait()


def _exchange(name, scatter, gather):
    ex = _Exchange(scatter, gather)
    n = ex.n

    def body(*refs):
        src, dst, sems = refs[:n], refs[n:2 * n], refs[2 * n:]
        ex.start(src, dst, sems)
        ex.wait(src, dst, sems)

    return pl.pallas_call(
        body, name=name, out_shape=tuple(ex.out_shape), in_specs=[ANY] * n, out_specs=(ANY,) * n,
        scratch_shapes=ex.scratch,
    )(*ex.arrays)


def _hosted(ex, refs, n_in, n_out):
    ins, ex_src = refs[:n_in], refs[n_in:n_in + ex.n]
    rest = refs[n_in + ex.n:]
    outs, ex_dst = rest[:n_out], rest[n_out:n_out + ex.n]
    rest = rest[n_out + ex.n:]
    n_own = len(rest) - len(ex.scratch)
    return ins, outs, rest[:n_own], (ex_src, ex_dst, rest[n_own:])


def _fwd_mix(x, win_g, wout_g, sgu_g, sgu_b, wcat, bs_full, cw, cb, cg, cbeta, tm, ex):
    T = x.shape[0]
    nt = T // tm

    def body(*refs):
        ins, outs, scratch, ex_refs = _hosted(ex, refs, 11, 4)
        x_ref, win_ref, wout_ref, sg_ref, sb_ref, wcat_ref, bs_ref, cw_ref, cb_ref, cg_ref, cbeta_ref = ins
        proj_ref, ycat_ref, n1_ref, rstd1_ref = outs
        hpad, hshift, ybuf = scratch
        i = pl.program_id(0)

        @pl.when(i == 0)
        def _():
            ex.start(*ex_refs)

        xf = x_ref[...]
        xb = xf.astype(BF16)
        proj_ref[...] = _dot_nt(xb, win_ref[...])
        u = _gelu(proj_ref[:, 0:D_SGU])
        v = _gelu(proj_ref[:, D_SGU:2 * D_SGU])
        vhat, _ = _ln_fwd(v)
        vn = vhat * sg_ref[...] + sb_ref[...]
        lo = _lo_mask()
        for c in range(tm // CHUNK):
            rows = slice(CHUNK * c, CHUNK * (c + 1))
            for p in range(4):
                lanes = slice(CHUNK * p, CHUNK * (p + 1))
                mixed = _dot(wcat_ref[p], _head_pair_stack(vn[rows, lanes], lo)) + bs_ref[:, lanes]
                ycat_ref[rows, lanes] = (u[rows, lanes] * mixed).astype(BF16)
        base = 2 * D_SGU
        a = proj_ref[:, base:base + D_CONV]
        g = proj_ref[:, base + D_CONV:base + 2 * D_CONV]

        @pl.when(i == 0)
        def _():
            hpad[0:HALO, :] = jnp.zeros((HALO, D_CONV), F32)

        hpad[HALO:HALO + tm, :] = a * jax.nn.sigmoid(g)
        _shifted_copies(hpad, hshift, tm + SHIFT_ROWS)
        _causal_conv(hpad, hshift, cw_ref, ybuf, tm, lambda k: HALO - (CONV_WIDTH - 1) + k, bias=cb_ref[...])
        hpad[0:HALO, :] = hpad[tm:tm + HALO, :]
        yhat, _ = _ln_fwd(ybuf[...])
        yn = yhat * cg_ref[...] + cbeta_ref[...]
        ycat_ref[:, D_SGU:D_SGU + D_CONV] = (yn * jax.nn.sigmoid(yn)).astype(BF16)
        r1 = ALPHA * xf + _dot(ycat_ref[...], wout_ref[...])
        n1, rstd1 = _ln_fwd(r1)
        n1_ref[...] = n1
        rstd1_ref[...] = rstd1

        @pl.when(i == nt - 1)
        def _():
            ex.wait(*ex_refs)

    S = jax.ShapeDtypeStruct
    row = lambda w: pl.BlockSpec((tm, w), lambda i: (i, 0))
    res = pl.pallas_call(
        body, name="fwd_mix", grid=(nt,),
        in_specs=[row(D_MODEL), _full(win_g.shape), _full(wout_g.shape), _full(sgu_g.shape), _full(sgu_b.shape),
                  _full(wcat.shape), _full(bs_full.shape), _full(cw.shape), _full(cb.shape), _full(cg.shape),
                  _full(cbeta.shape)] + [ANY] * ex.n,
        out_specs=(row(2 * D_MODEL), row(D_MODEL), row(D_MODEL), row(1)) + (ANY,) * ex.n,
        out_shape=(S((T, 2 * D_MODEL), F32), S((T, D_MODEL), BF16), S((T, D_MODEL), F32), S((T, 1), F32),
                   *ex.out_shape),
        scratch_shapes=[pltpu.VMEM((tm + HALO, D_CONV), F32), pltpu.VMEM((SUBLANES - 1, tm + SHIFT_ROWS, D_CONV), F32),
                        pltpu.VMEM((tm, D_CONV), F32)] + ex.scratch,
        compiler_params=_params(56, 1),
    )(x, win_g, wout_g, sgu_g, sgu_b, wcat, bs_full, cw, cb, cg, cbeta, *ex.arrays)
    return res[:4], res[4:]


def _load_resident(pairs, sems):
    cps = [pltpu.make_async_copy(s, d, sems.at[k]) for k, (s, d) in enumerate(pairs)]
    for cp in cps:
        cp.start()
    for cp in cps:
        cp.wait()


def _fwd_mlp(n1, tgt, l1g, l1b, l2g, l2b, wgt, wut, wd, tm):
    T = n1.shape[0]
    nt = T // tm
    nf = D_FF // MXU_COLS

    def body(n1_ref, tgt_ref, l1g_ref, l1b_ref, l2g_ref, l2b_ref, wg_hbm, wu_hbm, wd_hbm,
             gate_ref, up_ref, x1b_ref, dr2_ref, stat_ref, wg_s, wu_s, wd_s, sems):
        i = pl.program_id(0)

        @pl.when(i == 0)
        def _():
            _load_resident([(wg_hbm, wg_s), (wu_hbm, wu_s), (wd_hbm, wd_s)], sems)
            stat_ref[...] = jnp.zeros(stat_ref.shape, F32)

        x1 = n1_ref[...] * l1g_ref[...] + l1b_ref[...]
        x1b = x1.astype(BF16)
        x1b_ref[...] = x1b
        acc = jnp.zeros((tm, D_MODEL), F32)
        for f in range(nf):
            cols = slice(MXU_COLS * f, MXU_COLS * (f + 1))
            gt = _dot_nt(x1b, wg_s[cols, :])
            ut = _dot_nt(x1b, wu_s[cols, :])
            gate_ref[:, cols] = gt.astype(BF16)
            up_ref[:, cols] = ut.astype(BF16)
            hh = (gt * jax.nn.sigmoid(gt) * ut).astype(BF16)
            acc = acc + _dot(hh, wd_s[cols, :])
        r2 = ALPHA * x1 + acc
        n2, rstd2 = _ln_fwd(r2)
        x2 = n2 * l2g_ref[...] + l2b_ref[...]
        diff = x2 - tgt_ref[...]
        dx2 = diff * (1.0 / D_MODEL)
        stat_ref[0:1, :] += _colsum(diff * diff)
        stat_ref[1:2, :] += _colsum(dx2 * n2)
        stat_ref[2:3, :] += _colsum(dx2)
        dr2_ref[...] = _ln_bwd(dx2 * l2g_ref[...], n2, rstd2)

    S = jax.ShapeDtypeStruct
    row = lambda w: pl.BlockSpec((tm, w), lambda i: (i, 0))
    vec = _full((1, D_MODEL))
    return pl.pallas_call(
        body, name="fwd_mlp", grid=(nt,),
        in_specs=[row(D_MODEL), row(D_MODEL), vec, vec, vec, vec, ANY, ANY, ANY],
        out_specs=(row(D_FF), row(D_FF), row(D_MODEL), row(D_MODEL), _full((8, D_MODEL))),
        out_shape=(S((T, D_FF), BF16), S((T, D_FF), BF16), S((T, D_MODEL), BF16), S((T, D_MODEL), F32),
                   S((8, D_MODEL), F32)),
        scratch_shapes=[pltpu.VMEM((D_FF, D_MODEL), BF16)] * 3 + [pltpu.SemaphoreType.DMA((3,))],
        compiler_params=_params(56, 1),
    )(n1, tgt, l1g, l1b, l2g, l2b, wgt, wut, wd)


def _bwd_mlp(dr2, gate, up, n1, rstd1, l1g, wgt, wut, wd, tm):
    T = n1.shape[0]
    nt = T // tm
    nf = D_FF // MXU_COLS

    def body(dr2_ref, gate_ref, up_ref, n1_ref, rstd1_ref, l1g_ref, wg_hbm, wu_hbm, wd_hbm,
             dgate_ref, dup_ref, dr1_ref, stat_ref, wg_s, wu_s, wd_s, sems):
        i = pl.program_id(0)

        @pl.when(i == 0)
        def _():
            _load_resident([(wg_hbm, wg_s), (wu_hbm, wu_s), (wd_hbm, wd_s)], sems)
            stat_ref[...] = jnp.zeros(stat_ref.shape, F32)

        dr2 = dr2_ref[...]
        dr2b = dr2.astype(BF16)
        for f in range(nf):
            cols = slice(MXU_COLS * f, MXU_COLS * (f + 1))
            dhh = _dot_nt(dr2b, wd_s[cols, :])
            gt = gate_ref[:, cols].astype(F32)
            ut = up_ref[:, cols].astype(F32)
            sg = jax.nn.sigmoid(gt)
            dgate_ref[:, cols] = (dhh * ut * (sg * (1.0 + gt * (1.0 - sg)))).astype(BF16)
            dup_ref[:, cols] = (dhh * (gt * sg)).astype(BF16)
        dx1 = ALPHA * dr2 + _dot(dgate_ref[...], wg_s[...]) + _dot(dup_ref[...], wu_s[...])
        n1 = n1_ref[...]
        stat_ref[0:1, :] += _colsum(dx1 * n1)
        stat_ref[1:2, :] += _colsum(dx1)
        dr1_ref[...] = _ln_bwd(dx1 * l1g_ref[...], n1, rstd1_ref[...])

    S = jax.ShapeDtypeStruct
    row = lambda w: pl.BlockSpec((tm, w), lambda i: (i, 0))
    return pl.pallas_call(
        body, name="bwd_mlp", grid=(nt,),
        in_specs=[row(D_MODEL), row(D_FF), row(D_FF), row(D_MODEL), row(1), _full((1, D_MODEL)), ANY, ANY, ANY],
        out_specs=(row(D_FF), row(D_FF), row(D_MODEL), _full((8, D_MODEL))),
        out_shape=(S((T, D_FF), BF16), S((T, D_FF), BF16), S((T, D_MODEL), F32), S((8, D_MODEL), F32)),
        scratch_shapes=[pltpu.VMEM((D_FF, D_MODEL), BF16)] * 3 + [pltpu.SemaphoreType.DMA((3,))],
        compiler_params=_params(56, 1),
    )(dr2, gate, up, n1, rstd1, l1g, wgt, wut, wd)


def _bwd_mix(dr1, proj, win_g, wout_g, sgu_g, sgu_b, wcat, wcatt, bs_full, cw, cb, cg, cbeta, tm, ex):
    T = dr1.shape[0]
    nt = T // tm
    halo_blocks = tm // HALO

    def body(*refs):
        ins, outs, scratch, ex_refs = _hosted(ex, refs, 14, 6)
        (dr1_ref, proj_ref, halo_ref, win_ref, wout_ref, sg_ref, sb_ref, wcat_ref, wcatt_ref, bs_ref,
         cw_ref, cb_ref, cg_ref, cbeta_ref) = ins
        gx_ref, dproj_ref, gws_ref, gbs_ref, gcw_ref, vec_ref = outs
        hpad, hshift, dypad, dyshift, ybuf, dhbuf, dubuf, dvnbuf, gcw_acc = scratch
        i = pl.program_id(0)
        tile = nt - 1 - i

        @pl.when(i == 0)
        def _():
            ex.start(*ex_refs)
            gws_ref[...] = jnp.zeros(gws_ref.shape, F32)
            gbs_ref[...] = jnp.zeros(gbs_ref.shape, F32)
            gcw_ref[...] = jnp.zeros(gcw_ref.shape, F32)
            vec_ref[...] = jnp.zeros(vec_ref.shape, F32)
            gcw_acc[...] = jnp.zeros(gcw_acc.shape, F32)
            dypad[tm:tm + HALO, :] = jnp.zeros((HALO, D_CONV), F32)

        dr1 = dr1_ref[...]
        dycat = _dot_nt(dr1.astype(BF16), wout_ref[...])
        pu = proj_ref[:, 0:D_SGU]
        pv = proj_ref[:, D_SGU:2 * D_SGU]
        u = _gelu(pu)
        vhat, rstd_v = _ln_fwd(_gelu(pv))
        vn = vhat * sg_ref[...] + sb_ref[...]
        lo = _lo_mask()
        for c in range(tm // CHUNK):
            rows = slice(CHUNK * c, CHUNK * (c + 1))
            for p in range(4):
                lanes = slice(CHUNK * p, CHUNK * (p + 1))
                vstack = _head_pair_stack(vn[rows, lanes], lo)
                mixed = _dot(wcat_ref[p], vstack) + bs_ref[:, lanes]
                d_a = dycat[rows, lanes]
                dubuf[rows, lanes] = d_a * mixed
                dm = d_a * u[rows, lanes]
                gbs_ref[:, lanes] += dm
                dstack = _head_pair_stack(dm, lo)
                gws_ref[2 * CHUNK * p:2 * CHUNK * (p + 1), :] += _dot_nt(dstack, vn[rows, lanes].astype(BF16))
                dvnbuf[rows, lanes] = _dot(wcatt_ref[p], dstack)
        dvn = dvnbuf[...]
        vec_ref[0:1, :] += _colsum(dvn * vhat)
        vec_ref[1:2, :] += _colsum(dvn)
        dv = _ln_bwd(dvn * sg_ref[...], vhat, rstd_v)
        dproj_ref[:, 0:D_SGU] = (dubuf[...] * _gelu_grad(pu)).astype(BF16)
        dproj_ref[:, D_SGU:2 * D_SGU] = (dv * _gelu_grad(pv)).astype(BF16)
        base = 2 * D_SGU
        a = proj_ref[:, base:base + D_CONV]
        sgm = jax.nn.sigmoid(proj_ref[:, base + D_CONV:base + 2 * D_CONV])
        h_before = halo_ref[:, 0:D_CONV] * jax.nn.sigmoid(halo_ref[:, D_CONV:2 * D_CONV])
        hpad[0:HALO, :] = jnp.where(tile > 0, h_before, 0.0)
        hpad[HALO:HALO + tm, :] = a * sgm
        _shifted_copies(hpad, hshift, tm + SHIFT_ROWS)
        h_offset = lambda k: HALO - (CONV_WIDTH - 1) + k
        _causal_conv(hpad, hshift, cw_ref, ybuf, tm, h_offset, bias=cb_ref[...])
        yhat, rstd_y = _ln_fwd(ybuf[...])
        yn = yhat * cg_ref[...] + cbeta_ref[...]
        s = jax.nn.sigmoid(yn)
        dyn = dycat[:, D_SGU:D_SGU + D_CONV] * (s * (1.0 + yn * (1.0 - s)))
        vec_ref[3:4, :] += _colsum(dyn * yhat)
        vec_ref[4:5, :] += _colsum(dyn)
        dy = _ln_bwd(dyn * cg_ref[...], yhat, rstd_y)
        vec_ref[2:3, :] += _colsum(dy)
        dypad[0:tm, :] = dy
        _shifted_copies(dypad, dyshift, tm + SHIFT_ROWS)
        _causal_conv(dypad, dyshift, cw_ref, dhbuf, tm, lambda k: (CONV_WIDTH - 1) - k)
        _conv_weight_grad(dypad, hpad, hshift, gcw_acc, tm, h_offset)
        dypad[tm:tm + HALO, :] = dypad[0:HALO, :]
        dh = dhbuf[...]
        dproj_ref[:, base:base + D_CONV] = (dh * sgm).astype(BF16)
        dproj_ref[:, base + D_CONV:base + 2 * D_CONV] = (dh * a * sgm * (1.0 - sgm)).astype(BF16)
        gx_ref[...] = ALPHA * dr1 + _dot(dproj_ref[...], win_ref[...])

        @pl.when(i == nt - 1)
        def _():
            gcw_ref[...] = gcw_acc[...].sum(axis=1)
            ex.wait(*ex_refs)

    S = jax.ShapeDtypeStruct
    row = lambda w: pl.BlockSpec((tm, w), lambda i: (nt - 1 - i, 0))
    halo = pl.BlockSpec((HALO, D_MODEL), lambda i: (jnp.maximum((nt - 1 - i) * halo_blocks - 1, 0), 1))
    res = pl.pallas_call(
        body, name="bwd_mix", grid=(nt,),
        in_specs=[row(D_MODEL), row(2 * D_MODEL), halo, _full(win_g.shape), _full(wout_g.shape), _full(sgu_g.shape),
                  _full(sgu_b.shape), _full(wcat.shape), _full(wcatt.shape), _full(bs_full.shape), _full(cw.shape),
                  _full(cb.shape), _full(cg.shape), _full(cbeta.shape)] + [ANY] * ex.n,
        out_specs=(row(D_MODEL), row(2 * D_MODEL), _full((N_HEADS * CHUNK, CHUNK)), _full((CHUNK, D_SGU)),
                   _full((CONV_ROWS, D_CONV)), _full((8, D_CONV))) + (ANY,) * ex.n,
        out_shape=(S((T, D_MODEL), F32), S((T, 2 * D_MODEL), BF16), S((N_HEADS * CHUNK, CHUNK), F32),
                   S((CHUNK, D_SGU), F32), S((CONV_ROWS, D_CONV), F32), S((8, D_CONV), F32), *ex.out_shape),
        scratch_shapes=[pltpu.VMEM((tm + HALO, D_CONV), F32), pltpu.VMEM((SUBLANES - 1, tm + SHIFT_ROWS, D_CONV), F32),
                        pltpu.VMEM((tm + HALO, D_CONV), F32), pltpu.VMEM((SUBLANES - 1, tm + SHIFT_ROWS, D_CONV), F32),
                        pltpu.VMEM((tm, D_CONV), F32), pltpu.VMEM((tm, D_CONV), F32), pltpu.VMEM((tm, D_SGU), F32),
                        pltpu.VMEM((tm, D_SGU), F32), pltpu.VMEM((CONV_ROWS, 8, D_CONV), F32)] + ex.scratch,
        compiler_params=_params(56, 1),
    )(dr1, proj, proj, win_g, wout_g, sgu_g, sgu_b, wcat, wcatt, bs_full, cw, cb, cg, cbeta, *ex.arrays)
    return res[:6], res[6:]


def _wgrad(name, a, b, blocks, tk, ex=None, swiglu_up=None):
    T, M = a.shape
    N = b.shape[1]
    nk = T // tk
    out_shape = (blocks, M // blocks, N)
    ex = ex or _Exchange([], [])
    lhs = [a] if swiglu_up is None else [a, swiglu_up]

    def body(*refs):
        ins, (o_ref,), (acc,), ex_refs = _hosted(ex, refs, len(lhs) + 1, 1)
        a_ref, b_ref = ins[0], ins[-1]
        i = pl.program_id(0)

        @pl.when(i == 0)
        def _():
            ex.start(*ex_refs)
            acc[...] = jnp.zeros(acc.shape, F32)

        if swiglu_up is None:
            left = a_ref[...].astype(BF16)
        else:
            gt = a_ref[...].astype(F32)
            left = (gt * jax.nn.sigmoid(gt) * ins[1][...].astype(F32)).astype(BF16)
        acc[...] += _dot_tn(left, b_ref[...].astype(BF16))

        @pl.when(i == nk - 1)
        def _():
            o_ref[...] = acc[...].astype(BF16)
            ex.wait(*ex_refs)

    res = pl.pallas_call(
        body, name=name, grid=(nk,),
        in_specs=[pl.BlockSpec((tk, M), lambda i: (i, 0))] * len(lhs) + [pl.BlockSpec((tk, N), lambda i: (i, 0))]
        + [ANY] * ex.n,
        out_specs=(_full((M, N)),) + (ANY,) * ex.n,
        out_shape=(jax.ShapeDtypeStruct((M, N), BF16), *ex.out_shape),
        scratch_shapes=[pltpu.VMEM((M, N), F32)] + ex.scratch,
        compiler_params=_params(56, 1),
    )(*lhs, b, *ex.arrays)
    g = res[0].reshape(out_shape)
    return (g, res[1:]) if ex.n else g


def _adamw(w, g, m, v):
    m2 = ADAM_B1 * m + (1.0 - ADAM_B1) * g
    v2 = ADAM_B2 * v + (1.0 - ADAM_B2) * (g * g)
    m_hat = m2 / (1.0 - ADAM_B1 ** ADAM_STEP)
    v_hat = v2 / (1.0 - ADAM_B2 ** ADAM_STEP)
    delta = -ADAM_LR * (m_hat / (jnp.sqrt(v_hat) + ADAM_EPS) + ADAM_WD * w)
    return delta, m2, v2


def _sum_partials(r_ref):
    g = r_ref[0].astype(F32)
    for s in range(1, N_DEV):
        g = g + r_ref[s].astype(F32)
    return g


def _adamw_shard(name, parts, w, m, v, transposed):
    def body(r_ref, w_ref, m_ref, v_ref, g_o, d_o, m_o, v_o):
        g = _sum_partials(r_ref)
        if transposed:
            g = g.T
        delta, m2, v2 = _adamw(w_ref[...], g, m_ref[...], v_ref[...])
        g_o[...] = g
        d_o[...] = delta
        m_o[...] = m2
        v_o[...] = v2

    return pl.pallas_call(
        body, name=name, out_shape=(jax.ShapeDtypeStruct(w.shape, F32),) * 4,
        in_specs=[VMEM] * 4, out_specs=(VMEM,) * 4, compiler_params=_params(40),
    )(parts, w, m, v)


def _finish_small(gws8, gbs8, gcw8, vmix8, vmlp8, vout8, small):
    names = ["sgu_ln_g", "sgu_ln_b", "w_s", "b_s", "conv_b", "conv_ln_g", "conv_ln_b", "ln1_g", "ln1_b", "ln2_g", "ln2_b"]
    flat = []
    for n in names:
        flat += list(small[n])

    def body(*refs):
        gws_ref, gbs_ref, gcw_ref, vmix_ref, vmlp_ref, vout_ref = refs[:6]
        wmv = refs[6:6 + 3 * len(names)]
        outs = refs[6 + 3 * len(names):]
        loss_o, gcw_o = outs[0], outs[1]
        outs = outs[2:]
        gws = _sum_partials(gws_ref)
        gbs = _sum_partials(gbs_ref)
        vmix = _sum_partials(vmix_ref)
        vmlp = _sum_partials(vmlp_ref)
        vout = _sum_partials(vout_ref)
        gcw_o[...] = _sum_partials(gcw_ref)
        loss_o[...] = (0.5 / D_MODEL) * jnp.sum(vout[0:1, :], axis=1, keepdims=True)
        rows = lax.broadcasted_iota(jnp.int32, (N_HEADS * CHUNK, CHUNK), 0)
        cols = lax.broadcasted_iota(jnp.int32, (N_HEADS * CHUNK, CHUNK), 1)
        gws = jnp.where((rows & (CHUNK - 1)) >= cols, gws, 0.0)
        g_bs = lax.dot_general(_head_selector(), gbs, (((1,), (1,)), ((), ())), preferred_element_type=F32,
                               precision=lax.Precision.HIGHEST)
        grads = {
            "sgu_ln_g": vmix[0:1, :], "sgu_ln_b": vmix[1:2, :], "w_s": gws, "b_s": g_bs,
            "conv_b": vmix[2:3, :], "conv_ln_g": vmix[3:4, :], "conv_ln_b": vmix[4:5, :],
            "ln1_g": vmlp[0:1, :], "ln1_b": vmlp[1:2, :], "ln2_g": vout[1:2, :], "ln2_b": vout[2:3, :],
        }
        for k, n in enumerate(names):
            w_ref, m_ref, v_ref = wmv[3 * k:3 * k + 3]
            g = grads[n]
            delta, m2, v2 = _adamw(w_ref[...], g, m_ref[...], v_ref[...])
            outs[4 * k][...] = g
            outs[4 * k + 1][...] = delta
            outs[4 * k + 2][...] = m2
            outs[4 * k + 3][...] = v2

    S = jax.ShapeDtypeStruct
    out_shape = [S((1, 1), F32), S((CONV_ROWS, D_CONV), F32)]
    for n in names:
        out_shape += [S(small[n][0].shape, F32)] * 4
    res = pl.pallas_call(
        body, name="finish_small", out_shape=tuple(out_shape),
        in_specs=[VMEM] * (6 + len(flat)), out_specs=(VMEM,) * len(out_shape), compiler_params=_params(40),
    )(gws8, gbs8, gcw8, vmix8, vmlp8, vout8, *flat)
    upd = {n: res[2 + 4 * k:6 + 4 * k] for k, n in enumerate(names)}
    return res[0], res[1], upd


def _adamw_plain(name, g, w, m, v):
    def body(g_ref, w_ref, m_ref, v_ref, d_o, m_o, v_o):
        delta, m2, v2 = _adamw(w_ref[...], g_ref[...], m_ref[...], v_ref[...])
        d_o[...] = delta
        m_o[...] = m2
        v_o[...] = v2

    return pl.pallas_call(
        body, name=name, out_shape=(jax.ShapeDtypeStruct(w.shape, F32),) * 3,
        in_specs=[VMEM] * 4, out_specs=(VMEM,) * 3,
    )(g, w, m, v)


TOKEN_TILE_MIX = 256
TOKEN_TILE_FWD_MLP = 512
TOKEN_TILE_BWD_MLP = 256
TOKEN_TILE_WGRAD = 512


def kernel(x, w_in, sgu_ln_g, sgu_ln_b, w_s, b_s, conv_w, conv_b, conv_ln_g, conv_ln_b, w_out, ln1_g, ln1_b, w_gate, w_up, w_down, ln2_g, ln2_b, loss_target, m_w_in, m_sgu_ln_g, m_sgu_ln_b, m_w_s, m_b_s, m_conv_w, m_conv_b, m_conv_ln_g, m_conv_ln_b, m_w_out, m_ln1_g, m_ln1_b, m_w_gate, m_w_up, m_w_down, m_ln2_g, m_ln2_b, v_w_in, v_sgu_ln_g, v_sgu_ln_b, v_w_s, v_b_s, v_conv_w, v_conv_b, v_conv_ln_g, v_conv_ln_b, v_w_out, v_ln1_g, v_ln1_b, v_w_gate, v_w_up, v_w_down, v_ln2_g, v_ln2_b):
    xs = x[0]
    tgt = loss_target[0]

    (win_b, wout_b, wgt_b, wut_b, wd_b, cw_b, wcat, wcatt, bs_full) = _prep_weights(
        w_in[0], w_out[0], w_gate[0], w_up[0], w_down[0], conv_w[0], w_s[0], b_s[0])
    win_g, wout_g, cw_g = _exchange("gather_mix_weights", [], [win_b, wout_b, cw_b])
    win_g = win_g.reshape(2 * D_MODEL, D_MODEL)
    wout_g = wout_g.reshape(D_MODEL, D_MODEL)
    cw = jnp.transpose(cw_g[:, :, :D_CONV // N_DEV], (1, 0, 2)).reshape(CONV_ROWS, D_CONV)

    (proj, ycat, n1, rstd1), (wgt_g, wut_g, wd_g) = _fwd_mix(
        xs, win_g, wout_g, sgu_ln_g, sgu_ln_b, wcat, bs_full, cw, conv_b, conv_ln_g, conv_ln_b, TOKEN_TILE_MIX,
        _Exchange([], [wgt_b, wut_b, wd_b]))
    wgt_g = wgt_g.reshape(D_FF, D_MODEL)
    wut_g = wut_g.reshape(D_FF, D_MODEL)
    wd_g = wd_g.reshape(D_FF, D_MODEL)
    gate, up, x1b, dr2, vout = _fwd_mlp(n1, tgt, ln1_g, ln1_b, ln2_g, ln2_b, wgt_g, wut_g, wd_g, TOKEN_TILE_FWD_MLP)

    dgate, dup, dr1, vmlp = _bwd_mlp(dr2, gate, up, n1, rstd1, ln1_g, wgt_g, wut_g, wd_g, TOKEN_TILE_BWD_MLP)
    tk = TOKEN_TILE_WGRAD
    g_wgt = _wgrad("wgrad_gate", dgate, x1b, N_DEV, tk)
    g_wut = _wgrad("wgrad_up", dup, x1b, N_DEV, tk)
    g_wd = _wgrad("wgrad_down", gate, dr2, N_DEV, tk, swiglu_up=up)
    (gx, dproj, gws, gbs, gcw, vmix), (r_wgt, r_wut, r_wd) = _bwd_mix(
        dr1, proj, win_g, wout_g, sgu_ln_g, sgu_ln_b, wcat, wcatt, bs_full, cw, conv_b, conv_ln_g, conv_ln_b,
        TOKEN_TILE_MIX, _Exchange([g_wgt, g_wut, g_wd], []))
    g_wout = _wgrad("wgrad_out", ycat, dr1, N_DEV, tk)
    g_win, (r_wout, gws8, gbs8, gcw8, vmix8, vmlp8, vout8) = _wgrad(
        "wgrad_in", dproj, xs, N_DEV, tk,
        _Exchange([g_wout], [gws, gbs, gcw, vmix, vmlp, vout]))
    (r_win,) = _exchange("exchange_grad_in", [g_win], [])

    big = {
        "w_in": _adamw_shard("adamw_in", r_win, w_in[0], m_w_in[0], v_w_in[0], True),
        "w_out": _adamw_shard("adamw_out", r_wout, w_out[0], m_w_out[0], v_w_out[0], False),
        "w_gate": _adamw_shard("adamw_gate", r_wgt, w_gate[0], m_w_gate[0], v_w_gate[0], True),
        "w_up": _adamw_shard("adamw_up", r_wut, w_up[0], m_w_up[0], v_w_up[0], True),
        "w_down": _adamw_shard("adamw_down", r_wd, w_down[0], m_w_down[0], v_w_down[0], False),
    }
    small_in = {
        "sgu_ln_g": (sgu_ln_g, m_sgu_ln_g, v_sgu_ln_g), "sgu_ln_b": (sgu_ln_b, m_sgu_ln_b, v_sgu_ln_b),
        "w_s": tuple(a.reshape(N_HEADS * CHUNK, CHUNK) for a in (w_s, m_w_s, v_w_s)),
        "b_s": (b_s[0], m_b_s[0], v_b_s[0]),
        "conv_b": (conv_b, m_conv_b, v_conv_b), "conv_ln_g": (conv_ln_g, m_conv_ln_g, v_conv_ln_g),
        "conv_ln_b": (conv_ln_b, m_conv_ln_b, v_conv_ln_b),
        "ln1_g": (ln1_g, m_ln1_g, v_ln1_g), "ln1_b": (ln1_b, m_ln1_b, v_ln1_b),
        "ln2_g": (ln2_g, m_ln2_g, v_ln2_g), "ln2_b": (ln2_b, m_ln2_b, v_ln2_b),
    }
    loss11, gcw_full, small = _finish_small(gws8, gbs8, gcw8, vmix8, vmlp8, vout8, small_in)

    me = 4 * lax.axis_index("x") + 2 * lax.axis_index("y") + lax.axis_index("c")
    g_cw = lax.dynamic_slice(gcw_full, (0, me * (D_CONV // N_DEV)), (CONV_WIDTH, D_CONV // N_DEV))
    d_cw, m_cw, v_cw = _adamw_plain("adamw_conv_w", g_cw, conv_w[0], m_conv_w[0], v_conv_w[0])

    shapes = {"w_s": w_s.shape, "b_s": b_s.shape}
    out = {}
    for n, r in big.items():
        out[n] = tuple(a[None] for a in r)
    for n, r in small.items():
        out[n] = tuple(a.reshape(shapes[n]) for a in r) if n in shapes else tuple(r)
    out["conv_w"] = tuple(a[None] for a in (g_cw, d_cw, m_cw, v_cw))

    order = ["w_in", "sgu_ln_g", "sgu_ln_b", "w_s", "b_s", "conv_w", "conv_b", "conv_ln_g", "conv_ln_b", "w_out",
             "ln1_g", "ln1_b", "w_gate", "w_up", "w_down", "ln2_g", "ln2_b"]
    loss = loss11[0, 0]
    return (loss, gx[None], *[out[n][0] for n in order], *[out[n][1] for n in order],
            *[out[n][2] for n in order], *[out[n][3] for n in order])
```

```python
import jax
import jax.numpy as jnp
from jax import lax
from jax.experimental import pallas as pl
from jax.experimental.pallas import tpu as pltpu

F32 = jnp.float32
BF16 = jnp.bfloat16

D_MODEL = 1024
D_SGU = 512
D_CONV = 512
N_HEADS = 8
CHUNK = 128
CONV_WIDTH = 31
CONV_ROWS = 32
HALO = 32
D_FF = 2816
N_DEV = 8
FF_SHARD = D_FF // N_DEV
ALPHA = (2.0 * 1) ** 0.25
LN_EPS = 1e-5
INV_SQRT2 = 0.7071067811865476
INV_SQRT_2PI = 0.3989422804014327

ADAM_LR = 0.001
ADAM_B1 = 0.9
ADAM_B2 = 0.999
ADAM_EPS = 1e-08
ADAM_WD = 0.01
ADAM_STEP = 10

MXU_COLS = 256
SUBLANES = 8
CONV_ROW_BLOCK = 32
WGRAD_ROW_BLOCK = 16
SHIFT_ROWS = HALO - SUBLANES
MIB = 1024 * 1024

VMEM = pl.BlockSpec(memory_space=pltpu.VMEM)
ANY = pl.BlockSpec(memory_space=pl.ANY)
MESH = pl.DeviceIdType.MESH


def _params(vmem_mib, grid_dims=0):
    kw = dict(vmem_limit_bytes=vmem_mib * MIB)
    if grid_dims:
        kw["dimension_semantics"] = ("arbitrary",) * grid_dims
    return pltpu.CompilerParams(**kw)


def _full(shape):
    return pl.BlockSpec(shape, lambda i: (0,) * len(shape))


def _dot(a, b):
    return jnp.dot(a, b, preferred_element_type=F32)


def _dot_nt(a, b):
    return lax.dot_general(a, b, (((1,), (1,)), ((), ())), preferred_element_type=F32)


def _dot_tn(a, b):
    return lax.dot_general(a, b, (((0,), (0,)), ((), ())), preferred_element_type=F32)


def _gelu(x):
    return 0.5 * x * (1.0 + lax.erf(x * INV_SQRT2))


def _gelu_grad(x):
    return 0.5 * (1.0 + lax.erf(x * INV_SQRT2)) + x * jnp.exp(-0.5 * x * x) * INV_SQRT_2PI


def _ln_fwd(v):
    mu = jnp.mean(v, axis=-1, keepdims=True)
    d = v - mu
    var = jnp.mean(d * d, axis=-1, keepdims=True)
    rstd = lax.rsqrt(var + LN_EPS)
    return d * rstd, rstd


def _ln_bwd(dyhat, yhat, rstd):
    m1 = jnp.mean(dyhat, axis=-1, keepdims=True)
    m2 = jnp.mean(dyhat * yhat, axis=-1, keepdims=True)
    return rstd * (dyhat - m1 - yhat * m2)


def _colsum(v):
    return jnp.sum(v, axis=0, keepdims=True)


def _head_pair_stack(v, lo):
    return jnp.concatenate([jnp.where(lo, v, 0.0), jnp.where(lo, 0.0, v)], axis=0).astype(BF16)


def _lo_mask():
    return lax.broadcasted_iota(jnp.int32, (CHUNK, CHUNK), 1) < (CHUNK // 2)


def _head_selector():
    head = lax.broadcasted_iota(jnp.int32, (N_HEADS, D_SGU), 0)
    lane = lax.broadcasted_iota(jnp.int32, (N_HEADS, D_SGU), 1)
    width = D_SGU // N_HEADS
    return ((lane >= head * width) & (lane < (head + 1) * width)).astype(F32)


def _shifted_copies(pad_ref, sh_ref, rows):
    for r in range(1, SUBLANES):
        sh_ref[r - 1, 0:rows, :] = pad_ref[pl.ds(r, rows), :]


def _tap_groups(offset_of_tap):
    groups = {}
    for k in range(CONV_WIDTH):
        o = offset_of_tap(k)
        groups.setdefault(o % SUBLANES, []).append((k, o // SUBLANES))
    return groups


def _tap_window(pad_ref, sh_ref, r, taps, row0, rows):
    q0 = min(q for _, q in taps)
    q1 = max(q for _, q in taps)
    src = pad_ref if r == 0 else sh_ref.at[r - 1]
    win = src[pl.ds(row0 + SUBLANES * q0, SUBLANES * (q1 - q0) + rows), :]
    return win, [(k, SUBLANES * (q - q0)) for k, q in taps]


def _causal_conv(pad_ref, sh_ref, w_ref, out_ref, rows, offset_of_tap, bias=None):
    groups = _tap_groups(offset_of_tap)

    def block(b, carry):
        row0 = pl.multiple_of(b * CONV_ROW_BLOCK, CONV_ROW_BLOCK)
        if bias is None:
            acc = jnp.zeros((CONV_ROW_BLOCK, D_CONV), F32)
        else:
            acc = jnp.broadcast_to(bias, (CONV_ROW_BLOCK, D_CONV))
        for r, taps in groups.items():
            win, starts = _tap_window(pad_ref, sh_ref, r, taps, row0, CONV_ROW_BLOCK)
            for k, s in starts:
                acc = acc + w_ref[k:k + 1, :] * win[s:s + CONV_ROW_BLOCK, :]
        out_ref[pl.ds(row0, CONV_ROW_BLOCK), :] = acc
        return carry

    lax.fori_loop(0, rows // CONV_ROW_BLOCK, block, 0)


def _conv_weight_grad(dy_ref, pad_ref, sh_ref, acc_ref, rows, offset_of_tap):
    groups = _tap_groups(offset_of_tap)
    for r, taps in groups.items():

        def block(b, parts, r=r, taps=taps):
            row0 = pl.multiple_of(b * WGRAD_ROW_BLOCK, WGRAD_ROW_BLOCK)
            dyb = dy_ref[pl.ds(row0, WGRAD_ROW_BLOCK), :]
            win, starts = _tap_window(pad_ref, sh_ref, r, taps, row0, WGRAD_ROW_BLOCK)
            out = []
            for part, (_, s) in zip(parts, starts):
                pr = dyb * win[s:s + WGRAD_ROW_BLOCK, :]
                out.append(part + pr.reshape(WGRAD_ROW_BLOCK // SUBLANES, SUBLANES, D_CONV).sum(axis=0))
            return tuple(out)

        zeros = tuple(jnp.zeros((SUBLANES, D_CONV), F32) for _ in taps)
        parts = lax.fori_loop(0, rows // WGRAD_ROW_BLOCK, block, zeros)
        for part, (k, _) in zip(parts, taps):
            acc_ref[k] += part


def _prep_weights(w_in, w_out, w_gate_t, w_up_t, w_down, conv_w, w_s, b_s):
    def body(win_ref, wout_ref, wgt_ref, wut_ref, wd_ref, cw_ref, ws_ref, bs_ref,
             win_o, wout_o, wgt_o, wut_o, wd_o, cw_o, wcat_o, wcatt_o, bsf_o):
        win_o[...] = win_ref[...].T.astype(BF16)
        wout_o[...] = wout_ref[...].astype(BF16)
        wgt_o[...] = wgt_ref[...].astype(BF16)
        wut_o[...] = wut_ref[...].astype(BF16)
        wd_o[...] = wd_ref[...].astype(BF16)
        cw_o[...] = jnp.zeros(cw_o.shape, F32)
        cw_o[0:CONV_WIDTH, 0:D_CONV // N_DEV] = cw_ref[...]
        row = lax.broadcasted_iota(jnp.int32, (CHUNK, CHUNK), 0)
        col = lax.broadcasted_iota(jnp.int32, (CHUNK, CHUNK), 1)
        causal = row >= col
        for h in range(N_HEADS):
            w = jnp.where(causal, ws_ref[h], 0.0)
            p, half = h // 2, (h % 2) * CHUNK
            wcat_o[p, :, half:half + CHUNK] = w.astype(BF16)
            wcatt_o[p, :, half:half + CHUNK] = w.T.astype(BF16)
        bsf_o[...] = lax.dot_general(bs_ref[...], _head_selector(), (((0,), (0,)), ((), ())),
                                     preferred_element_type=F32, precision=lax.Precision.HIGHEST)

    S = jax.ShapeDtypeStruct
    return pl.pallas_call(
        body, name="prep_weights",
        out_shape=(S((256, D_MODEL), BF16), S((128, D_MODEL), BF16), S((FF_SHARD, D_MODEL), BF16),
                   S((FF_SHARD, D_MODEL), BF16), S((FF_SHARD, D_MODEL), BF16), S((CONV_ROWS, 128), F32),
                   S((4, CHUNK, 2 * CHUNK), BF16), S((4, CHUNK, 2 * CHUNK), BF16), S((CHUNK, D_SGU), F32)),
        compiler_params=_params(32),
    )(w_in, w_out, w_gate_t, w_up_t, w_down, conv_w, w_s, b_s)


def _mesh_position():
    x, y, c = lax.axis_index("x"), lax.axis_index("y"), lax.axis_index("c")
    return x, y, c


def _peers(x, y, c):
    out = []
    for k in range(1, N_DEV):
        px = 1 - x if (k >> 2) & 1 else x
        py = 1 - y if (k >> 1) & 1 else y
        pc = 1 - c if k & 1 else c
        out.append(((px, py, pc), 4 * px + 2 * py + pc))
    return out


class _Exchange:
    def __init__(self, scatter, gather):
        self.arrays = list(scatter) + list(gather)
        self.n_sc = len(scatter)
        self.n = len(self.arrays)
        self.out_shape = [jax.ShapeDtypeStruct(a.shape if k < self.n_sc else (N_DEV,) + a.shape, a.dtype)
                          for k, a in enumerate(self.arrays)]
        n_remote = self.n * (N_DEV - 1)
        self.scratch = [pltpu.SemaphoreType.DMA((n_remote,)), pltpu.SemaphoreType.DMA((n_remote,)),
                        pltpu.SemaphoreType.DMA((self.n,))] if self.n else []

    def _copies(self, src, dst, sems):
        send_sems, recv_sems, local_sems = sems
        x, y, c = _mesh_position()
        me = 4 * x + 2 * y + c
        locals_, first, arrivals, passed, last = [], [], [], [], []

        def remote(a, k, src_ref, slot, to):
            s = a * (N_DEV - 1) + k
            return pltpu.make_async_remote_copy(src_ref=src_ref, dst_ref=dst[a].at[slot], send_sem=send_sems.at[s],
                                                recv_sem=recv_sems.at[s], device_id=to, device_id_type=MESH)

        for a in range(self.n):
            if a < self.n_sc:
                locals_.append(pltpu.make_async_copy(src[a].at[me], dst[a].at[me], local_sems.at[a]))
                for k, (peer, pid) in enumerate(_peers(x, y, c)):
                    first.append(remote(a, k, src[a].at[pid], me, peer))
                    last.append(remote(a, k, src[a].at[pid], pid, peer))
                continue
            locals_.append(pltpu.make_async_copy(src[a], dst[a].at[me], local_sems.at[a]))
            sibling, sib_id = (x, y, 1 - c), 4 * x + 2 * y + (1 - c)
            chips = [(1 - x, y), (x, 1 - y), (1 - x, 1 - y)]
            first.append(remote(a, 0, src[a], me, sibling))
            last.append(remote(a, 0, src[a], sib_id, sibling))
            for j, (px, py) in enumerate(chips):
                same, other = 4 * px + 2 * py + c, 4 * px + 2 * py + (1 - c)
                first.append(remote(a, 1 + j, src[a], me, (px, py, c)))
                arrivals.append(remote(a, 1 + j, src[a], same, (px, py, c)))
                passed.append(remote(a, 4 + j, dst[a].at[same], same, sibling))
                last.append(remote(a, 4 + j, dst[a].at[other], other, sibling))
        return locals_, first, arrivals, passed, last

    def start(self, src, dst, sems):
        if not self.n:
            return
        locals_, first, _, _, _ = self._copies(src, dst, sems)
        for cp in locals_ + first:
            cp.start()

    def forward(self, src, dst, sems):
        if self.n == self.n_sc:
            return
        _, _, arrivals, passed, _ = self._copies(src, dst, sems)
        for arrived, cp in zip(arrivals, passed):
            arrived.wait_recv()
            cp.start()

    def wait(self, src, dst, sems):
        if not self.n:
            return
        locals_, first, _, passed, last = self._copies(src, dst, sems)
        for cp in last:
            cp.wait_recv()
        for cp in first + passed:
            cp.wait_send()
        for cp in locals_:
            cp.wait()


def _exchange(name, scatter, gather):
    ex = _Exchange(scatter, gather)
    n = ex.n

    def body(*refs):
        src, dst, sems = refs[:n], refs[n:2 * n], refs[2 * n:]
        ex.start(src, dst, sems)
        ex.forward(src, dst, sems)
        ex.wait(src, dst, sems)

    return pl.pallas_call(
        body, name=name, out_shape=tuple(ex.out_shape), in_specs=[ANY] * n, out_specs=(ANY,) * n,
        scratch_shapes=ex.scratch,
    )(*ex.arrays)


def _forward_step(n_steps):
    return (11 * n_steps) // 16


def _hosted(ex, refs, n_in, n_out):
    ins, ex_src = refs[:n_in], refs[n_in:n_in + ex.n]
    rest = refs[n_in + ex.n:]
    outs, ex_dst = rest[:n_out], rest[n_out:n_out + ex.n]
    rest = rest[n_out + ex.n:]
    n_own = len(rest) - len(ex.scratch)
    return ins, outs, rest[:n_own], (ex_src, ex_dst, rest[n_own:])


def _fwd_mix(x, win_g, wout_g, sgu_g, sgu_b, wcat, bs_full, cw, cb, cg, cbeta, tm, ex):
    T = x.shape[0]
    nt = T // tm

    def body(*refs):
        ins, outs, scratch, ex_refs = _hosted(ex, refs, 11, 4)
        x_ref, win_ref, wout_ref, sg_ref, sb_ref, wcat_ref, bs_ref, cw_ref, cb_ref, cg_ref, cbeta_ref = ins
        proj_ref, ycat_ref, n1_ref, rstd1_ref = outs
        hpad, hshift, ybuf = scratch
        i = pl.program_id(0)

        @pl.when(i == 0)
        def _():
            ex.start(*ex_refs)

        xf = x_ref[...]
        xb = xf.astype(BF16)
        proj_ref[...] = _dot_nt(xb, win_ref[...])
        u = _gelu(proj_ref[:, 0:D_SGU])
        v = _gelu(proj_ref[:, D_SGU:2 * D_SGU])
        vhat, _ = _ln_fwd(v)
        vn = vhat * sg_ref[...] + sb_ref[...]
        lo = _lo_mask()
        for c in range(tm // CHUNK):
            rows = slice(CHUNK * c, CHUNK * (c + 1))
            for p in range(4):
                lanes = slice(CHUNK * p, CHUNK * (p + 1))
                mixed = _dot(wcat_ref[p], _head_pair_stack(vn[rows, lanes], lo)) + bs_ref[:, lanes]
                ycat_ref[rows, lanes] = (u[rows, lanes] * mixed).astype(BF16)
        base = 2 * D_SGU
        a = proj_ref[:, base:base + D_CONV]
        g = proj_ref[:, base + D_CONV:base + 2 * D_CONV]

        @pl.when(i == 0)
        def _():
            hpad[0:HALO, :] = jnp.zeros((HALO, D_CONV), F32)

        hpad[HALO:HALO + tm, :] = a * jax.nn.sigmoid(g)
        _shifted_copies(hpad, hshift, tm + SHIFT_ROWS)
        _causal_conv(hpad, hshift, cw_ref, ybuf, tm, lambda k: HALO - (CONV_WIDTH - 1) + k, bias=cb_ref[...])
        hpad[0:HALO, :] = hpad[tm:tm + HALO, :]
        yhat, _ = _ln_fwd(ybuf[...])
        yn = yhat * cg_ref[...] + cbeta_ref[...]
        ycat_ref[:, D_SGU:D_SGU + D_CONV] = (yn * jax.nn.sigmoid(yn)).astype(BF16)
        r1 = ALPHA * xf + _dot(ycat_ref[...], wout_ref[...])
        n1, rstd1 = _ln_fwd(r1)
        n1_ref[...] = n1
        rstd1_ref[...] = rstd1

        @pl.when(i == _forward_step(nt))
        def _():
            ex.forward(*ex_refs)

        @pl.when(i == nt - 1)
        def _():
            ex.wait(*ex_refs)

    S = jax.ShapeDtypeStruct
    row = lambda w: pl.BlockSpec((tm, w), lambda i: (i, 0))
    res = pl.pallas_call(
        body, name="fwd_mix", grid=(nt,),
        in_specs=[row(D_MODEL), _full(win_g.shape), _full(wout_g.shape), _full(sgu_g.shape), _full(sgu_b.shape),
                  _full(wcat.shape), _full(bs_full.shape), _full(cw.shape), _full(cb.shape), _full(cg.shape),
                  _full(cbeta.shape)] + [ANY] * ex.n,
        out_specs=(row(2 * D_MODEL), row(D_MODEL), row(D_MODEL), row(1)) + (ANY,) * ex.n,
        out_shape=(S((T, 2 * D_MODEL), F32), S((T, D_MODEL), BF16), S((T, D_MODEL), F32), S((T, 1), F32),
                   *ex.out_shape),
        scratch_shapes=[pltpu.VMEM((tm + HALO, D_CONV), F32), pltpu.VMEM((SUBLANES - 1, tm + SHIFT_ROWS, D_CONV), F32),
                        pltpu.VMEM((tm, D_CONV), F32)] + ex.scratch,
        compiler_params=_params(56, 1),
    )(x, win_g, wout_g, sgu_g, sgu_b, wcat, bs_full, cw, cb, cg, cbeta, *ex.arrays)
    return res[:4], res[4:]


def _load_resident(pairs, sems):
    cps = [pltpu.make_async_copy(s, d, sems.at[k]) for k, (s, d) in enumerate(pairs)]
    for cp in cps:
        cp.start()
    for cp in cps:
        cp.wait()


def _fwd_mlp(n1, tgt, l1g, l1b, l2g, l2b, wgt, wut, wd, tm):
    T = n1.shape[0]
    nt = T // tm
    nf = D_FF // MXU_COLS

    def body(n1_ref, tgt_ref, l1g_ref, l1b_ref, l2g_ref, l2b_ref, wg_hbm, wu_hbm, wd_hbm,
             gate_ref, up_ref, x1b_ref, dr2_ref, stat_ref, wg_s, wu_s, wd_s, sems):
        i = pl.program_id(0)

        @pl.when(i == 0)
        def _():
            _load_resident([(wg_hbm, wg_s), (wu_hbm, wu_s), (wd_hbm, wd_s)], sems)
            stat_ref[...] = jnp.zeros(stat_ref.shape, F32)

        x1 = n1_ref[...] * l1g_ref[...] + l1b_ref[...]
        x1b = x1.astype(BF16)
        x1b_ref[...] = x1b
        acc = jnp.zeros((tm, D_MODEL), F32)
        for f in range(nf):
            cols = slice(MXU_COLS * f, MXU_COLS * (f + 1))
            gt = _dot_nt(x1b, wg_s[cols, :])
            ut = _dot_nt(x1b, wu_s[cols, :])
            gate_ref[:, cols] = gt.astype(BF16)
            up_ref[:, cols] = ut.astype(BF16)
            hh = (gt * jax.nn.sigmoid(gt) * ut).astype(BF16)
            acc = acc + _dot(hh, wd_s[cols, :])
        r2 = ALPHA * x1 + acc
        n2, rstd2 = _ln_fwd(r2)
        x2 = n2 * l2g_ref[...] + l2b_ref[...]
        diff = x2 - tgt_ref[...]
        dx2 = diff * (1.0 / D_MODEL)
        stat_ref[0:1, :] += _colsum(diff * diff)
        stat_ref[1:2, :] += _colsum(dx2 * n2)
        stat_ref[2:3, :] += _colsum(dx2)
        dr2_ref[...] = _ln_bwd(dx2 * l2g_ref[...], n2, rstd2)

    S = jax.ShapeDtypeStruct
    row = lambda w: pl.BlockSpec((tm, w), lambda i: (i, 0))
    vec = _full((1, D_MODEL))
    return pl.pallas_call(
        body, name="fwd_mlp", grid=(nt,),
        in_specs=[row(D_MODEL), row(D_MODEL), vec, vec, vec, vec, ANY, ANY, ANY],
        out_specs=(row(D_FF), row(D_FF), row(D_MODEL), row(D_MODEL), _full((8, D_MODEL))),
        out_shape=(S((T, D_FF), BF16), S((T, D_FF), BF16), S((T, D_MODEL), BF16), S((T, D_MODEL), F32),
                   S((8, D_MODEL), F32)),
        scratch_shapes=[pltpu.VMEM((D_FF, D_MODEL), BF16)] * 3 + [pltpu.SemaphoreType.DMA((3,))],
        compiler_params=_params(56, 1),
    )(n1, tgt, l1g, l1b, l2g, l2b, wgt, wut, wd)


def _bwd_mlp(dr2, gate, up, n1, rstd1, l1g, wgt, wut, wd, tm):
    T = n1.shape[0]
    nt = T // tm
    nf = D_FF // MXU_COLS

    def body(dr2_ref, gate_ref, up_ref, n1_ref, rstd1_ref, l1g_ref, wg_hbm, wu_hbm, wd_hbm,
             dgate_ref, dup_ref, hh_ref, dr1_ref, stat_ref, wg_s, wu_s, wd_s, sems):
        i = pl.program_id(0)

        @pl.when(i == 0)
        def _():
            _load_resident([(wg_hbm, wg_s), (wu_hbm, wu_s), (wd_hbm, wd_s)], sems)
            stat_ref[...] = jnp.zeros(stat_ref.shape, F32)

        dr2 = dr2_ref[...]
        dr2b = dr2.astype(BF16)
        for f in range(nf):
            cols = slice(MXU_COLS * f, MXU_COLS * (f + 1))
            dhh = _dot_nt(dr2b, wd_s[cols, :])
            gt = gate_ref[:, cols].astype(F32)
            ut = up_ref[:, cols].astype(F32)
            sg = jax.nn.sigmoid(gt)
            silu = gt * sg
            dgate_ref[:, cols] = (dhh * ut * (sg * (1.0 + gt * (1.0 - sg)))).astype(BF16)
            dup_ref[:, cols] = (dhh * silu).astype(BF16)
            hh_ref[:, cols] = (silu * ut).astype(BF16)
        dx1 = ALPHA * dr2 + _dot(dgate_ref[...], wg_s[...]) + _dot(dup_ref[...], wu_s[...])
        n1 = n1_ref[...]
        stat_ref[0:1, :] += _colsum(dx1 * n1)
        stat_ref[1:2, :] += _colsum(dx1)
        dr1_ref[...] = _ln_bwd(dx1 * l1g_ref[...], n1, rstd1_ref[...])

    S = jax.ShapeDtypeStruct
    row = lambda w: pl.BlockSpec((tm, w), lambda i: (i, 0))
    return pl.pallas_call(
        body, name="bwd_mlp", grid=(nt,),
        in_specs=[row(D_MODEL), row(D_FF), row(D_FF), row(D_MODEL), row(1), _full((1, D_MODEL)), ANY, ANY, ANY],
        out_specs=(row(D_FF), row(D_FF), row(D_FF), row(D_MODEL), _full((8, D_MODEL))),
        out_shape=(S((T, D_FF), BF16), S((T, D_FF), BF16), S((T, D_FF), BF16), S((T, D_MODEL), F32),
                   S((8, D_MODEL), F32)),
        scratch_shapes=[pltpu.VMEM((D_FF, D_MODEL), BF16)] * 3 + [pltpu.SemaphoreType.DMA((3,))],
        compiler_params=_params(56, 1),
    )(dr2, gate, up, n1, rstd1, l1g, wgt, wut, wd)


def _bwd_mix(dr1, proj, win_g, wout_g, sgu_g, sgu_b, wcat, wcatt, bs_full, cw, cb, cg, cbeta, tm, ex):
    T = dr1.shape[0]
    nt = T // tm
    halo_blocks = tm // HALO

    def body(*refs):
        ins, outs, scratch, ex_refs = _hosted(ex, refs, 14, 6)
        (dr1_ref, proj_ref, halo_ref, win_ref, wout_ref, sg_ref, sb_ref, wcat_ref, wcatt_ref, bs_ref,
         cw_ref, cb_ref, cg_ref, cbeta_ref) = ins
        gx_ref, dproj_ref, gws_out, gbs_out, gcw_ref, vec_ref = outs
        hpad, hshift, dypad, dyshift, ybuf, dhbuf, dubuf, dvnbuf, gcw_acc, gws_ref, gbs_ref = scratch
        i = pl.program_id(0)
        tile = nt - 1 - i

        @pl.when(i == 0)
        def _():
            ex.start(*ex_refs)
            gws_ref[...] = jnp.zeros(gws_ref.shape, F32)
            gbs_ref[...] = jnp.zeros(gbs_ref.shape, F32)
            gcw_ref[...] = jnp.zeros(gcw_ref.shape, F32)
            vec_ref[...] = jnp.zeros(vec_ref.shape, F32)
            gcw_acc[...] = jnp.zeros(gcw_acc.shape, F32)
            dypad[tm:tm + HALO, :] = jnp.zeros((HALO, D_CONV), F32)

        dr1 = dr1_ref[...]
        dycat = _dot_nt(dr1.astype(BF16), wout_ref[...])
        pu = proj_ref[:, 0:D_SGU]
        pv = proj_ref[:, D_SGU:2 * D_SGU]
        u = _gelu(pu)
        vhat, rstd_v = _ln_fwd(_gelu(pv))
        vn = vhat * sg_ref[...] + sb_ref[...]
        lo = _lo_mask()
        for c in range(tm // CHUNK):
            rows = slice(CHUNK * c, CHUNK * (c + 1))
            for p in range(4):
                lanes = slice(CHUNK * p, CHUNK * (p + 1))
                vstack = _head_pair_stack(vn[rows, lanes], lo)
                mixed = _dot(wcat_ref[p], vstack) + bs_ref[:, lanes]
                d_a = dycat[rows, lanes]
                dubuf[rows, lanes] = d_a * mixed
                dm = d_a * u[rows, lanes]
                gbs_ref[:, lanes] += dm
                dstack = _head_pair_stack(dm, lo)
                gws_ref[2 * CHUNK * p:2 * CHUNK * (p + 1), :] += _dot_nt(dstack, vn[rows, lanes].astype(BF16))
                dvnbuf[rows, lanes] = _dot(wcatt_ref[p], dstack)
        dvn = dvnbuf[...]
        vec_ref[0:1, :] += _colsum(dvn * vhat)
        vec_ref[1:2, :] += _colsum(dvn)
        dv = _ln_bwd(dvn * sg_ref[...], vhat, rstd_v)
        dproj_ref[:, 0:D_SGU] = (dubuf[...] * _gelu_grad(pu)).astype(BF16)
        dproj_ref[:, D_SGU:2 * D_SGU] = (dv * _gelu_grad(pv)).astype(BF16)
        base = 2 * D_SGU
        a = proj_ref[:, base:base + D_CONV]
        sgm = jax.nn.sigmoid(proj_ref[:, base + D_CONV:base + 2 * D_CONV])
        h_before = halo_ref[:, 0:D_CONV] * jax.nn.sigmoid(halo_ref[:, D_CONV:2 * D_CONV])
        hpad[0:HALO, :] = jnp.where(tile > 0, h_before, 0.0)
        hpad[HALO:HALO + tm, :] = a * sgm
        _shifted_copies(hpad, hshift, tm + SHIFT_ROWS)
        h_offset = lambda k: HALO - (CONV_WIDTH - 1) + k
        _causal_conv(hpad, hshift, cw_ref, ybuf, tm, h_offset, bias=cb_ref[...])
        yhat, rstd_y = _ln_fwd(ybuf[...])
        yn = yhat * cg_ref[...] + cbeta_ref[...]
        s = jax.nn.sigmoid(yn)
        dyn = dycat[:, D_SGU:D_SGU + D_CONV] * (s * (1.0 + yn * (1.0 - s)))
        vec_ref[3:4, :] += _colsum(dyn * yhat)
        vec_ref[4:5, :] += _colsum(dyn)
        dy = _ln_bwd(dyn * cg_ref[...], yhat, rstd_y)
        vec_ref[2:3, :] += _colsum(dy)
        dypad[0:tm, :] = dy
        _shifted_copies(dypad, dyshift, tm + SHIFT_ROWS)
        _causal_conv(dypad, dyshift, cw_ref, dhbuf, tm, lambda k: (CONV_WIDTH - 1) - k)
        _conv_weight_grad(dypad, hpad, hshift, gcw_acc, tm, h_offset)
        dypad[tm:tm + HALO, :] = dypad[0:HALO, :]
        dh = dhbuf[...]
        dproj_ref[:, base:base + D_CONV] = (dh * sgm).astype(BF16)
        dproj_ref[:, base + D_CONV:base + 2 * D_CONV] = (dh * a * sgm * (1.0 - sgm)).astype(BF16)
        gx_ref[...] = ALPHA * dr1 + _dot(dproj_ref[...], win_ref[...])

        @pl.when(i == _forward_step(nt))
        def _():
            ex.forward(*ex_refs)

        @pl.when(i == nt - 1)
        def _():
            gcw_ref[...] = gcw_acc[...].sum(axis=1)
            gws_out[...] = gws_ref[...].astype(BF16)
            gbs_out[...] = lax.dot_general(_head_selector(), gbs_ref[...], (((1,), (1,)), ((), ())),
                                           preferred_element_type=F32, precision=lax.Precision.HIGHEST)
            ex.wait(*ex_refs)

    S = jax.ShapeDtypeStruct
    row = lambda w: pl.BlockSpec((tm, w), lambda i: (nt - 1 - i, 0))
    halo = pl.BlockSpec((HALO, D_MODEL), lambda i: (jnp.maximum((nt - 1 - i) * halo_blocks - 1, 0), 1))
    res = pl.pallas_call(
        body, name="bwd_mix", grid=(nt,),
        in_specs=[row(D_MODEL), row(2 * D_MODEL), halo, _full(win_g.shape), _full(wout_g.shape), _full(sgu_g.shape),
                  _full(sgu_b.shape), _full(wcat.shape), _full(wcatt.shape), _full(bs_full.shape), _full(cw.shape),
                  _full(cb.shape), _full(cg.shape), _full(cbeta.shape)] + [ANY] * ex.n,
        out_specs=(row(D_MODEL), row(2 * D_MODEL), _full((N_HEADS * CHUNK, CHUNK)), _full((N_HEADS, CHUNK)),
                   _full((CONV_ROWS, D_CONV)), _full((8, D_CONV))) + (ANY,) * ex.n,
        out_shape=(S((T, D_MODEL), F32), S((T, 2 * D_MODEL), BF16), S((N_HEADS * CHUNK, CHUNK), BF16),
                   S((N_HEADS, CHUNK), F32), S((CONV_ROWS, D_CONV), F32), S((8, D_CONV), F32), *ex.out_shape),
        scratch_shapes=[pltpu.VMEM((tm + HALO, D_CONV), F32), pltpu.VMEM((SUBLANES - 1, tm + SHIFT_ROWS, D_CONV), F32),
                        pltpu.VMEM((tm + HALO, D_CONV), F32), pltpu.VMEM((SUBLANES - 1, tm + SHIFT_ROWS, D_CONV), F32),
                        pltpu.VMEM((tm, D_CONV), F32), pltpu.VMEM((tm, D_CONV), F32), pltpu.VMEM((tm, D_SGU), F32),
                        pltpu.VMEM((tm, D_SGU), F32), pltpu.VMEM((CONV_ROWS, 8, D_CONV), F32),
                        pltpu.VMEM((N_HEADS * CHUNK, CHUNK), F32), pltpu.VMEM((CHUNK, D_SGU), F32)] + ex.scratch,
        compiler_params=_params(56, 1),
    )(dr1, proj, proj, win_g, wout_g, sgu_g, sgu_b, wcat, wcatt, bs_full, cw, cb, cg, cbeta, *ex.arrays)
    return res[:6], res[6:]


def _wgrad(name, a, b, blocks, tk, ex=None):
    T, M = a.shape
    N = b.shape[1]
    nk = T // tk
    out_shape = (blocks, M // blocks, N)
    ex = ex or _Exchange([], [])

    def body(*refs):
        (a_ref, b_ref), (o_ref,), (acc,), ex_refs = _hosted(ex, refs, 2, 1)
        i = pl.program_id(0)

        @pl.when(i == 0)
        def _():
            ex.start(*ex_refs)
            acc[...] = jnp.zeros(acc.shape, F32)

        acc[...] += _dot_tn(a_ref[...].astype(BF16), b_ref[...].astype(BF16))

        @pl.when(i == _forward_step(nk))
        def _():
            ex.forward(*ex_refs)

        @pl.when(i == nk - 1)
        def _():
            o_ref[...] = acc[...].astype(BF16)
            ex.wait(*ex_refs)

    res = pl.pallas_call(
        body, name=name, grid=(nk,),
        in_specs=[pl.BlockSpec((tk, M), lambda i: (i, 0)), pl.BlockSpec((tk, N), lambda i: (i, 0))] + [ANY] * ex.n,
        out_specs=(_full((M, N)),) + (ANY,) * ex.n,
        out_shape=(jax.ShapeDtypeStruct((M, N), BF16), *ex.out_shape),
        scratch_shapes=[pltpu.VMEM((M, N), F32)] + ex.scratch,
        compiler_params=_params(56, 1),
    )(a, b, *ex.arrays)
    g = res[0].reshape(out_shape)
    return (g, res[1:]) if ex.n else g


def _adamw(w, g, m, v):
    m2 = ADAM_B1 * m + (1.0 - ADAM_B1) * g
    v2 = ADAM_B2 * v + (1.0 - ADAM_B2) * (g * g)
    m_hat = m2 / (1.0 - ADAM_B1 ** ADAM_STEP)
    v_hat = v2 / (1.0 - ADAM_B2 ** ADAM_STEP)
    delta = -ADAM_LR * (m_hat / (jnp.sqrt(v_hat) + ADAM_EPS) + ADAM_WD * w)
    return delta, m2, v2


def _sum_partials(r_ref):
    g = r_ref[0].astype(F32)
    for s in range(1, N_DEV):
        g = g + r_ref[s].astype(F32)
    return g


def _adamw_shard(name, parts, w, m, v, transposed):
    def body(r_ref, w_ref, m_ref, v_ref, g_o, d_o, m_o, v_o):
        g = _sum_partials(r_ref)
        if transposed:
            g = g.T
        delta, m2, v2 = _adamw(w_ref[...], g, m_ref[...], v_ref[...])
        g_o[...] = g
        d_o[...] = delta
        m_o[...] = m2
        v_o[...] = v2

    return pl.pallas_call(
        body, name=name, out_shape=(jax.ShapeDtypeStruct(w.shape, F32),) * 4, compiler_params=_params(40),
    )(parts, w, m, v)


def _finish_small(gws8, gbs8, gcw8, vmix8, vmlp8, vout8, small):
    names = ["sgu_ln_g", "sgu_ln_b", "w_s", "b_s", "conv_b", "conv_ln_g", "conv_ln_b", "ln1_g", "ln1_b", "ln2_g", "ln2_b"]
    flat = []
    for n in names:
        flat += list(small[n])

    def body(*refs):
        gws_ref, gbs_ref, gcw_ref, vmix_ref, vmlp_ref, vout_ref = refs[:6]
        wmv = refs[6:6 + 3 * len(names)]
        outs = refs[6 + 3 * len(names):]
        loss_o, gcw_o = outs[0], outs[1]
        outs = outs[2:]
        gws = _sum_partials(gws_ref)
        gbs = _sum_partials(gbs_ref)
        vmix = _sum_partials(vmix_ref)
        vmlp = _sum_partials(vmlp_ref)
        vout = _sum_partials(vout_ref)
        gcw_o[...] = _sum_partials(gcw_ref)
        loss_o[...] = (0.5 / D_MODEL) * jnp.sum(vout[0:1, :], axis=1, keepdims=True)
        rows = lax.broadcasted_iota(jnp.int32, (N_HEADS * CHUNK, CHUNK), 0)
        cols = lax.broadcasted_iota(jnp.int32, (N_HEADS * CHUNK, CHUNK), 1)
        gws = jnp.where((rows & (CHUNK - 1)) >= cols, gws, 0.0)
        grads = {
            "sgu_ln_g": vmix[0:1, :], "sgu_ln_b": vmix[1:2, :], "w_s": gws, "b_s": gbs,
            "conv_b": vmix[2:3, :], "conv_ln_g": vmix[3:4, :], "conv_ln_b": vmix[4:5, :],
            "ln1_g": vmlp[0:1, :], "ln1_b": vmlp[1:2, :], "ln2_g": vout[1:2, :], "ln2_b": vout[2:3, :],
        }
        for k, n in enumerate(names):
            w_ref, m_ref, v_ref = wmv[3 * k:3 * k + 3]
            g = grads[n]
            delta, m2, v2 = _adamw(w_ref[...], g, m_ref[...], v_ref[...])
            outs[4 * k][...] = g
            outs[4 * k + 1][...] = delta
            outs[4 * k + 2][...] = m2
            outs[4 * k + 3][...] = v2

    S = jax.ShapeDtypeStruct
    out_shape = [S((1, 1), F32), S((CONV_ROWS, D_CONV), F32)]
    for n in names:
        out_shape += [S(small[n][0].shape, F32)] * 4
    res = pl.pallas_call(
        body, name="finish_small", out_shape=tuple(out_shape),
        compiler_params=_params(40),
    )(gws8, gbs8, gcw8, vmix8, vmlp8, vout8, *flat)
    upd = {n: res[2 + 4 * k:6 + 4 * k] for k, n in enumerate(names)}
    return res[0], res[1], upd


def _adamw_plain(name, g, w, m, v):
    def body(g_ref, w_ref, m_ref, v_ref, d_o, m_o, v_o):
        delta, m2, v2 = _adamw(w_ref[...], g_ref[...], m_ref[...], v_ref[...])
        d_o[...] = delta
        m_o[...] = m2
        v_o[...] = v2

    return pl.pallas_call(
        body, name=name, out_shape=(jax.ShapeDtypeStruct(w.shape, F32),) * 3,
    )(g, w, m, v)


TOKEN_TILE_MIX = 256
TOKEN_TILE_FWD_MLP = 512
TOKEN_TILE_BWD_MLP = 256
TOKEN_TILE_WGRAD = 512


def kernel(x, w_in, sgu_ln_g, sgu_ln_b, w_s, b_s, conv_w, conv_b, conv_ln_g, conv_ln_b, w_out, ln1_g, ln1_b, w_gate, w_up, w_down, ln2_g, ln2_b, loss_target, m_w_in, m_sgu_ln_g, m_sgu_ln_b, m_w_s, m_b_s, m_conv_w, m_conv_b, m_conv_ln_g, m_conv_ln_b, m_w_out, m_ln1_g, m_ln1_b, m_w_gate, m_w_up, m_w_down, m_ln2_g, m_ln2_b, v_w_in, v_sgu_ln_g, v_sgu_ln_b, v_w_s, v_b_s, v_conv_w, v_conv_b, v_conv_ln_g, v_conv_ln_b, v_w_out, v_ln1_g, v_ln1_b, v_w_gate, v_w_up, v_w_down, v_ln2_g, v_ln2_b):
    xs = x[0]
    tgt = loss_target[0]

    (win_b, wout_b, wgt_b, wut_b, wd_b, cw_b, wcat, wcatt, bs_full) = _prep_weights(
        w_in[0], w_out[0], w_gate[0].T, w_up[0].T, w_down[0], conv_w[0], w_s[0], b_s[0])
    win_g, wout_g, cw_g = _exchange("gather_mix_weights", [], [win_b, wout_b, cw_b])
    win_g = win_g.reshape(2 * D_MODEL, D_MODEL)
    wout_g = wout_g.reshape(D_MODEL, D_MODEL)
    cw = jnp.transpose(cw_g[:, :, :D_CONV // N_DEV], (1, 0, 2)).reshape(CONV_ROWS, D_CONV)

    (proj, ycat, n1, rstd1), (wgt_g, wut_g, wd_g) = _fwd_mix(
        xs, win_g, wout_g, sgu_ln_g, sgu_ln_b, wcat, bs_full, cw, conv_b, conv_ln_g, conv_ln_b, TOKEN_TILE_MIX,
        _Exchange([], [wgt_b, wut_b, wd_b]))
    wgt_g = wgt_g.reshape(D_FF, D_MODEL)
    wut_g = wut_g.reshape(D_FF, D_MODEL)
    wd_g = wd_g.reshape(D_FF, D_MODEL)
    gate, up, x1b, dr2, vout = _fwd_mlp(n1, tgt, ln1_g, ln1_b, ln2_g, ln2_b, wgt_g, wut_g, wd_g, TOKEN_TILE_FWD_MLP)

    dgate, dup, hh, dr1, vmlp = _bwd_mlp(dr2, gate, up, n1, rstd1, ln1_g, wgt_g, wut_g, wd_g, TOKEN_TILE_BWD_MLP)
    tk = TOKEN_TILE_WGRAD
    g_wgt = _wgrad("wgrad_gate", dgate, x1b, N_DEV, tk)
    g_wut = _wgrad("wgrad_up", dup, x1b, N_DEV, tk)
    g_wd = _wgrad("wgrad_down", hh, dr2, N_DEV, tk)
    (gx, dproj, gws, gbs, gcw, vmix), (r_wgt, r_wut, r_wd) = _bwd_mix(
        dr1, proj, win_g, wout_g, sgu_ln_g, sgu_ln_b, wcat, wcatt, bs_full, cw, conv_b, conv_ln_g, conv_ln_b,
        TOKEN_TILE_MIX, _Exchange([g_wgt, g_wut, g_wd], []))
    g_win, (gws8, gbs8, gcw8, vmix8, vmlp8, vout8) = _wgrad(
        "wgrad_in", dproj, xs, N_DEV, tk,
        _Exchange([], [gws, gbs, gcw, vmix, vmlp, vout]))
    g_wout, (r_win,) = _wgrad("wgrad_out", ycat, dr1, N_DEV, tk, _Exchange([g_win], []))
    (r_wout,) = _exchange("exchange_grad_out", [g_wout], [])

    big = {
        "w_in": _adamw_shard("adamw_in", r_win, w_in[0], m_w_in[0], v_w_in[0], True),
        "w_out": _adamw_shard("adamw_out", r_wout, w_out[0], m_w_out[0], v_w_out[0], False),
        "w_gate": _adamw_shard("adamw_gate", r_wgt, w_gate[0].T, m_w_gate[0].T, v_w_gate[0].T, False),
        "w_up": _adamw_shard("adamw_up", r_wut, w_up[0].T, m_w_up[0].T, v_w_up[0].T, False),
        "w_down": _adamw_shard("adamw_down", r_wd, w_down[0], m_w_down[0], v_w_down[0], False),
    }
    small_in = {
        "sgu_ln_g": (sgu_ln_g, m_sgu_ln_g, v_sgu_ln_g), "sgu_ln_b": (sgu_ln_b, m_sgu_ln_b, v_sgu_ln_b),
        "w_s": tuple(a.reshape(N_HEADS * CHUNK, CHUNK) for a in (w_s, m_w_s, v_w_s)),
        "b_s": (b_s[0], m_b_s[0], v_b_s[0]),
        "conv_b": (conv_b, m_conv_b, v_conv_b), "conv_ln_g": (conv_ln_g, m_conv_ln_g, v_conv_ln_g),
        "conv_ln_b": (conv_ln_b, m_conv_ln_b, v_conv_ln_b),
        "ln1_g": (ln1_g, m_ln1_g, v_ln1_g), "ln1_b": (ln1_b, m_ln1_b, v_ln1_b),
        "ln2_g": (ln2_g, m_ln2_g, v_ln2_g), "ln2_b": (ln2_b, m_ln2_b, v_ln2_b),
    }
    loss11, gcw_full, small = _finish_small(gws8, gbs8, gcw8, vmix8, vmlp8, vout8, small_in)

    me = 4 * lax.axis_index("x") + 2 * lax.axis_index("y") + lax.axis_index("c")
    g_cw = lax.dynamic_slice(gcw_full, (0, me * (D_CONV // N_DEV)), (CONV_WIDTH, D_CONV // N_DEV))
    d_cw, m_cw, v_cw = _adamw_plain("adamw_conv_w", g_cw, conv_w[0], m_conv_w[0], v_conv_w[0])

    shapes = {"w_s": w_s.shape, "b_s": b_s.shape}
    out = {}
    for n, r in big.items():
        out[n] = tuple((a.T if n in ("w_gate", "w_up") else a)[None] for a in r)
    for n, r in small.items():
        out[n] = tuple(a.reshape(shapes[n]) for a in r) if n in shapes else tuple(r)
    out["conv_w"] = tuple(a[None] for a in (g_cw, d_cw, m_cw, v_cw))

    order = ["w_in", "sgu_ln_g", "sgu_ln_b", "w_s", "b_s", "conv_w", "conv_b", "conv_ln_g", "conv_ln_b", "w_out",
             "ln1_g", "ln1_b", "w_gate", "w_up", "w_down", "ln2_g", "ln2_b"]
    loss = loss11[0, 0]
    return (loss, gx[None], *[out[n][0] for n in order], *[out[n][1] for n in order],
            *[out[n][2] for n in order], *[out[n][3] for n in order])
```

```python
import jax
import jax.numpy as jnp
from jax import lax
from jax.experimental import pallas as pl
from jax.experimental.pallas import tpu as pltpu

F32 = jnp.float32
BF16 = jnp.bfloat16

D_MODEL = 1024
D_SGU = 512
D_CONV = 512
N_HEADS = 8
CHUNK = 128
CONV_WIDTH = 31
CONV_ROWS = 32
HALO = 32
D_FF = 2816
N_DEV = 8
FF_SHARD = D_FF // N_DEV
ALPHA = (2.0 * 1) ** 0.25
LN_EPS = 1e-5
INV_SQRT2 = 0.7071067811865476
INV_SQRT_2PI = 0.3989422804014327

ADAM_LR = 0.001
ADAM_B1 = 0.9
ADAM_B2 = 0.999
ADAM_EPS = 1e-08
ADAM_WD = 0.01
ADAM_STEP = 10

MXU_COLS = 256
SUBLANES = 8
CONV_ROW_BLOCK = 32
WGRAD_ROW_BLOCK = 16
SHIFT_ROWS = HALO - SUBLANES
MIB = 1024 * 1024

VMEM = pl.BlockSpec(memory_space=pltpu.VMEM)
ANY = pl.BlockSpec(memory_space=pl.ANY)
MESH = pl.DeviceIdType.MESH


def _params(vmem_mib, grid_dims=0):
    kw = dict(vmem_limit_bytes=vmem_mib * MIB)
    if grid_dims:
        kw["dimension_semantics"] = ("arbitrary",) * grid_dims
    return pltpu.CompilerParams(**kw)


def _full(shape):
    return pl.BlockSpec(shape, lambda i: (0,) * len(shape))


def _dot(a, b):
    return jnp.dot(a, b, preferred_element_type=F32)


def _dot_nt(a, b):
    return lax.dot_general(a, b, (((1,), (1,)), ((), ())), preferred_element_type=F32)


def _dot_tn(a, b):
    return lax.dot_general(a, b, (((0,), (0,)), ((), ())), preferred_element_type=F32)


def _normal_cdf(x):
    return 0.5 * (1.0 + lax.erf(x * INV_SQRT2))


def _gelu_grad(x, cdf):
    return cdf + x * jnp.exp(-0.5 * x * x) * INV_SQRT_2PI


def _ln_fwd(v):
    mu = jnp.mean(v, axis=-1, keepdims=True)
    d = v - mu
    var = jnp.mean(d * d, axis=-1, keepdims=True)
    rstd = lax.rsqrt(var + LN_EPS)
    return d * rstd, rstd


def _ln_bwd(dyhat, yhat, rstd):
    m1 = jnp.mean(dyhat, axis=-1, keepdims=True)
    m2 = jnp.mean(dyhat * yhat, axis=-1, keepdims=True)
    return rstd * (dyhat - m1 - yhat * m2)


def _colsum(v):
    return jnp.sum(v, axis=0, keepdims=True)


def _head_pair_stack(v, lo):
    return jnp.concatenate([jnp.where(lo, v, 0.0), jnp.where(lo, 0.0, v)], axis=0).astype(BF16)


def _lo_mask():
    return lax.broadcasted_iota(jnp.int32, (CHUNK, CHUNK), 1) < (CHUNK // 2)


def _head_selector():
    head = lax.broadcasted_iota(jnp.int32, (N_HEADS, D_SGU), 0)
    lane = lax.broadcasted_iota(jnp.int32, (N_HEADS, D_SGU), 1)
    width = D_SGU // N_HEADS
    return ((lane >= head * width) & (lane < (head + 1) * width)).astype(F32)


def _shifted_copies(pad_ref, sh_ref, rows):
    for r in range(1, SUBLANES):
        sh_ref[r - 1, 0:rows, :] = pad_ref[pl.ds(r, rows), :]


def _tap_groups(offset_of_tap):
    groups = {}
    for k in range(CONV_WIDTH):
        o = offset_of_tap(k)
        groups.setdefault(o % SUBLANES, []).append((k, o // SUBLANES))
    return groups


def _tap_window(pad_ref, sh_ref, r, taps, row0, rows):
    q0 = min(q for _, q in taps)
    q1 = max(q for _, q in taps)
    src = pad_ref if r == 0 else sh_ref.at[r - 1]
    win = src[pl.ds(row0 + SUBLANES * q0, SUBLANES * (q1 - q0) + rows), :]
    return win, [(k, SUBLANES * (q - q0)) for k, q in taps]


def _causal_conv(pad_ref, sh_ref, w_ref, out_ref, rows, offset_of_tap, bias=None):
    groups = _tap_groups(offset_of_tap)

    def block(b, carry):
        row0 = pl.multiple_of(b * CONV_ROW_BLOCK, CONV_ROW_BLOCK)
        if bias is None:
            acc = jnp.zeros((CONV_ROW_BLOCK, D_CONV), F32)
        else:
            acc = jnp.broadcast_to(bias, (CONV_ROW_BLOCK, D_CONV))
        for r, taps in groups.items():
            win, starts = _tap_window(pad_ref, sh_ref, r, taps, row0, CONV_ROW_BLOCK)
            for k, s in starts:
                acc = acc + w_ref[k:k + 1, :] * win[s:s + CONV_ROW_BLOCK, :]
        out_ref[pl.ds(row0, CONV_ROW_BLOCK), :] = acc
        return carry

    lax.fori_loop(0, rows // CONV_ROW_BLOCK, block, 0)


def _conv_weight_grad(dy_ref, pad_ref, sh_ref, acc_ref, rows, offset_of_tap):
    groups = _tap_groups(offset_of_tap)
    for r, taps in groups.items():

        def block(b, parts, r=r, taps=taps):
            row0 = pl.multiple_of(b * WGRAD_ROW_BLOCK, WGRAD_ROW_BLOCK)
            dyb = dy_ref[pl.ds(row0, WGRAD_ROW_BLOCK), :]
            win, starts = _tap_window(pad_ref, sh_ref, r, taps, row0, WGRAD_ROW_BLOCK)
            out = []
            for part, (_, s) in zip(parts, starts):
                pr = dyb * win[s:s + WGRAD_ROW_BLOCK, :]
                out.append(part + pr.reshape(WGRAD_ROW_BLOCK // SUBLANES, SUBLANES, D_CONV).sum(axis=0))
            return tuple(out)

        zeros = tuple(jnp.zeros((SUBLANES, D_CONV), F32) for _ in taps)
        parts = lax.fori_loop(0, rows // WGRAD_ROW_BLOCK, block, zeros)
        for part, (k, _) in zip(parts, taps):
            acc_ref[k] += part


def _prep_weights(w_in, w_out, w_gate_t, w_up_t, w_down, conv_w, w_s, b_s):
    def body(win_ref, wout_ref, wgt_ref, wut_ref, wd_ref, cw_ref, ws_ref, bs_ref,
             win_o, wout_o, wgt_o, wut_o, wd_o, cw_o, wcat_o, wcatt_o, bsf_o):
        win_o[...] = win_ref[...].T.astype(BF16)
        wout_o[...] = wout_ref[...].astype(BF16)
        wgt_o[...] = wgt_ref[...].astype(BF16)
        wut_o[...] = wut_ref[...].astype(BF16)
        wd_o[...] = wd_ref[...].astype(BF16)
        cw_o[...] = jnp.zeros(cw_o.shape, F32)
        cw_o[0:CONV_WIDTH, 0:D_CONV // N_DEV] = cw_ref[...]
        row = lax.broadcasted_iota(jnp.int32, (CHUNK, CHUNK), 0)
        col = lax.broadcasted_iota(jnp.int32, (CHUNK, CHUNK), 1)
        causal = row >= col
        for h in range(N_HEADS):
            w = jnp.where(causal, ws_ref[h], 0.0)
            p, half = h // 2, (h % 2) * CHUNK
            wcat_o[p, :, half:half + CHUNK] = w.astype(BF16)
            wcatt_o[p, :, half:half + CHUNK] = w.T.astype(BF16)
        bsf_o[...] = lax.dot_general(bs_ref[...], _head_selector(), (((0,), (0,)), ((), ())),
                                     preferred_element_type=F32, precision=lax.Precision.HIGHEST)

    S = jax.ShapeDtypeStruct
    return pl.pallas_call(
        body, name="prep_weights",
        out_shape=(S((256, D_MODEL), BF16), S((128, D_MODEL), BF16), S((FF_SHARD, D_MODEL), BF16),
                   S((FF_SHARD, D_MODEL), BF16), S((FF_SHARD, D_MODEL), BF16), S((CONV_ROWS, 128), F32),
                   S((4, CHUNK, 2 * CHUNK), BF16), S((4, CHUNK, 2 * CHUNK), BF16), S((CHUNK, D_SGU), F32)),
        compiler_params=_params(32),
    )(w_in, w_out, w_gate_t, w_up_t, w_down, conv_w, w_s, b_s)


def _mesh_position():
    x, y, c = lax.axis_index("x"), lax.axis_index("y"), lax.axis_index("c")
    return x, y, c


def _peers(x, y, c):
    out = []
    for k in range(1, N_DEV):
        px = 1 - x if (k >> 2) & 1 else x
        py = 1 - y if (k >> 1) & 1 else y
        pc = 1 - c if k & 1 else c
        out.append(((px, py, pc), 4 * px + 2 * py + pc))
    return out


class _Exchange:
    def __init__(self, scatter, gather):
        self.arrays = list(scatter) + list(gather)
        self.n_sc = len(scatter)
        self.n = len(self.arrays)
        self.out_shape = [jax.ShapeDtypeStruct(a.shape if k < self.n_sc else (N_DEV,) + a.shape, a.dtype)
                          for k, a in enumerate(self.arrays)]
        n_remote = self.n * (N_DEV - 1)
        self.scratch = [pltpu.SemaphoreType.DMA((n_remote,)), pltpu.SemaphoreType.DMA((n_remote,)),
                        pltpu.SemaphoreType.DMA((self.n,))] if self.n else []

    def _copies(self, src, dst, sems):
        send_sems, recv_sems, local_sems = sems
        x, y, c = _mesh_position()
        me = 4 * x + 2 * y + c
        locals_, first, arrivals, passed, last = [], [], [], [], []

        def remote(a, k, src_ref, slot, to):
            s = a * (N_DEV - 1) + k
            return pltpu.make_async_remote_copy(src_ref=src_ref, dst_ref=dst[a].at[slot], send_sem=send_sems.at[s],
                                                recv_sem=recv_sems.at[s], device_id=to, device_id_type=MESH)

        for a in range(self.n):
            if a < self.n_sc:
                locals_.append(pltpu.make_async_copy(src[a].at[me], dst[a].at[me], local_sems.at[a]))
                for k, (peer, pid) in enumerate(_peers(x, y, c)):
                    first.append(remote(a, k, src[a].at[pid], me, peer))
                    last.append(remote(a, k, src[a].at[pid], pid, peer))
                continue
            locals_.append(pltpu.make_async_copy(src[a], dst[a].at[me], local_sems.at[a]))
            sibling, sib_id = (x, y, 1 - c), 4 * x + 2 * y + (1 - c)
            chips = [(1 - x, y), (x, 1 - y), (1 - x, 1 - y)]
            first.append(remote(a, 0, src[a], me, sibling))
            last.append(remote(a, 0, src[a], sib_id, sibling))
            for j, (px, py) in enumerate(chips):
                same, other = 4 * px + 2 * py + c, 4 * px + 2 * py + (1 - c)
                first.append(remote(a, 1 + j, src[a], me, (px, py, c)))
                arrivals.append(remote(a, 1 + j, src[a], same, (px, py, c)))
                passed.append(remote(a, 4 + j, dst[a].at[same], same, sibling))
                last.append(remote(a, 4 + j, dst[a].at[other], other, sibling))
        return locals_, first, arrivals, passed, last

    def start(self, src, dst, sems):
        if not self.n:
            return
        locals_, first, _, _, _ = self._copies(src, dst, sems)
        for cp in locals_ + first:
            cp.start()

    def forward(self, src, dst, sems):
        if self.n == self.n_sc:
            return
        _, _, arrivals, passed, _ = self._copies(src, dst, sems)
        for arrived, cp in zip(arrivals, passed):
            arrived.wait_recv()
            cp.start()

    def wait(self, src, dst, sems):
        if not self.n:
            return
        locals_, first, _, passed, last = self._copies(src, dst, sems)
        for cp in last:
            cp.wait_recv()
        for cp in first + passed:
            cp.wait_send()
        for cp in locals_:
            cp.wait()


def _exchange(name, scatter, gather):
    ex = _Exchange(scatter, gather)
    n = ex.n

    def body(*refs):
        src, dst, sems = refs[:n], refs[n:2 * n], refs[2 * n:]
        ex.start(src, dst, sems)
        ex.forward(src, dst, sems)
        ex.wait(src, dst, sems)

    return pl.pallas_call(
        body, name=name, out_shape=tuple(ex.out_shape), in_specs=[ANY] * n, out_specs=(ANY,) * n,
        scratch_shapes=ex.scratch,
    )(*ex.arrays)


def _forward_step(n_steps):
    return (11 * n_steps) // 16


def _hosted(ex, refs, n_in, n_out):
    ins, ex_src = refs[:n_in], refs[n_in:n_in + ex.n]
    rest = refs[n_in + ex.n:]
    outs, ex_dst = rest[:n_out], rest[n_out:n_out + ex.n]
    rest = rest[n_out + ex.n:]
    n_own = len(rest) - len(ex.scratch)
    return ins, outs, rest[:n_own], (ex_src, ex_dst, rest[n_own:])


def _fwd_mix(x, win_g, wout_g, sgu_g, sgu_b, wcat, bs_full, cw, cb, cg, cbeta, tm, ex):
    T = x.shape[0]
    nt = T // tm

    def body(*refs):
        ins, outs, scratch, ex_refs = _hosted(ex, refs, 11, 6)
        x_ref, win_ref, wout_ref, sg_ref, sb_ref, wcat_ref, bs_ref, cw_ref, cb_ref, cg_ref, cbeta_ref = ins
        proj_ref, ycat_ref, n1_ref, rstd1_ref, phi_ref, y_ref = outs
        hpad, hshift = scratch
        i = pl.program_id(0)

        @pl.when(i == 0)
        def _():
            ex.start(*ex_refs)

        xf = x_ref[...]
        xb = xf.astype(BF16)
        proj_ref[...] = _dot_nt(xb, win_ref[...])
        phi_ref[...] = _normal_cdf(proj_ref[:, 0:2 * D_SGU])
        u = proj_ref[:, 0:D_SGU] * phi_ref[:, 0:D_SGU]
        v = proj_ref[:, D_SGU:2 * D_SGU] * phi_ref[:, D_SGU:2 * D_SGU]
        vhat, _ = _ln_fwd(v)
        vn = vhat * sg_ref[...] + sb_ref[...]
        lo = _lo_mask()
        for c in range(tm // CHUNK):
            rows = slice(CHUNK * c, CHUNK * (c + 1))
            for p in range(4):
                lanes = slice(CHUNK * p, CHUNK * (p + 1))
                mixed = _dot(wcat_ref[p], _head_pair_stack(vn[rows, lanes], lo)) + bs_ref[:, lanes]
                ycat_ref[rows, lanes] = (u[rows, lanes] * mixed).astype(BF16)
        base = 2 * D_SGU
        a = proj_ref[:, base:base + D_CONV]
        g = proj_ref[:, base + D_CONV:base + 2 * D_CONV]

        @pl.when(i == 0)
        def _():
            hpad[0:HALO, :] = jnp.zeros((HALO, D_CONV), F32)

        hpad[HALO:HALO + tm, :] = a * jax.nn.sigmoid(g)
        _shifted_copies(hpad, hshift, tm + SHIFT_ROWS)
        _causal_conv(hpad, hshift, cw_ref, y_ref, tm, lambda k: HALO - (CONV_WIDTH - 1) + k, bias=cb_ref[...])
        hpad[0:HALO, :] = hpad[tm:tm + HALO, :]
        yhat, _ = _ln_fwd(y_ref[...])
        yn = yhat * cg_ref[...] + cbeta_ref[...]
        ycat_ref[:, D_SGU:D_SGU + D_CONV] = (yn * jax.nn.sigmoid(yn)).astype(BF16)
        r1 = ALPHA * xf + _dot(ycat_ref[...], wout_ref[...])
        n1, rstd1 = _ln_fwd(r1)
        n1_ref[...] = n1
        rstd1_ref[...] = rstd1

        @pl.when(i == _forward_step(nt))
        def _():
            ex.forward(*ex_refs)

        @pl.when(i == nt - 1)
        def _():
            ex.wait(*ex_refs)

    S = jax.ShapeDtypeStruct
    row = lambda w: pl.BlockSpec((tm, w), lambda i: (i, 0))
    res = pl.pallas_call(
        body, name="fwd_mix", grid=(nt,),
        in_specs=[row(D_MODEL), _full(win_g.shape), _full(wout_g.shape), _full(sgu_g.shape), _full(sgu_b.shape),
                  _full(wcat.shape), _full(bs_full.shape), _full(cw.shape), _full(cb.shape), _full(cg.shape),
                  _full(cbeta.shape)] + [ANY] * ex.n,
        out_specs=(row(2 * D_MODEL), row(D_MODEL), row(D_MODEL), row(1), row(2 * D_SGU), row(D_CONV)) + (ANY,) * ex.n,
        out_shape=(S((T, 2 * D_MODEL), F32), S((T, D_MODEL), BF16), S((T, D_MODEL), F32), S((T, 1), F32),
                   S((T, 2 * D_SGU), F32), S((T, D_CONV), F32), *ex.out_shape),
        scratch_shapes=[pltpu.VMEM((tm + HALO, D_CONV), F32),
                        pltpu.VMEM((SUBLANES - 1, tm + SHIFT_ROWS, D_CONV), F32)] + ex.scratch,
        compiler_params=_params(56, 1),
    )(x, win_g, wout_g, sgu_g, sgu_b, wcat, bs_full, cw, cb, cg, cbeta, *ex.arrays)
    return res[:6], res[6:]


def _load_resident(pairs, sems):
    cps = [pltpu.make_async_copy(s, d, sems.at[k]) for k, (s, d) in enumerate(pairs)]
    for cp in cps:
        cp.start()
    for cp in cps:
        cp.wait()


def _fwd_mlp(n1, tgt, l1g, l1b, l2g, l2b, wgt, wut, wd, tm):
    T = n1.shape[0]
    nt = T // tm
    nf = D_FF // MXU_COLS

    def body(n1_ref, tgt_ref, l1g_ref, l1b_ref, l2g_ref, l2b_ref, wg_hbm, wu_hbm, wd_hbm,
             gate_ref, up_ref, x1b_ref, dr2_ref, stat_ref, wg_s, wu_s, wd_s, sems):
        i = pl.program_id(0)

        @pl.when(i == 0)
        def _():
            _load_resident([(wg_hbm, wg_s), (wu_hbm, wu_s), (wd_hbm, wd_s)], sems)
            stat_ref[...] = jnp.zeros(stat_ref.shape, F32)

        x1 = n1_ref[...] * l1g_ref[...] + l1b_ref[...]
        x1b = x1.astype(BF16)
        x1b_ref[...] = x1b
        acc = jnp.zeros((tm, D_MODEL), F32)
        for f in range(nf):
            cols = slice(MXU_COLS * f, MXU_COLS * (f + 1))
            gt = _dot_nt(x1b, wg_s[cols, :])
            ut = _dot_nt(x1b, wu_s[cols, :])
            gate_ref[:, cols] = gt.astype(BF16)
            up_ref[:, cols] = ut.astype(BF16)
            hh = (gt * jax.nn.sigmoid(gt) * ut).astype(BF16)
            acc = acc + _dot(hh, wd_s[cols, :])
        r2 = ALPHA * x1 + acc
        n2, rstd2 = _ln_fwd(r2)
        x2 = n2 * l2g_ref[...] + l2b_ref[...]
        diff = x2 - tgt_ref[...]
        dx2 = diff * (1.0 / D_MODEL)
        stat_ref[0:1, :] += _colsum(diff * diff)
        stat_ref[1:2, :] += _colsum(dx2 * n2)
        stat_ref[2:3, :] += _colsum(dx2)
        dr2_ref[...] = _ln_bwd(dx2 * l2g_ref[...], n2, rstd2)

    S = jax.ShapeDtypeStruct
    row = lambda w: pl.BlockSpec((tm, w), lambda i: (i, 0))
    vec = _full((1, D_MODEL))
    return pl.pallas_call(
        body, name="fwd_mlp", grid=(nt,),
        in_specs=[row(D_MODEL), row(D_MODEL), vec, vec, vec, vec, ANY, ANY, ANY],
        out_specs=(row(D_FF), row(D_FF), row(D_MODEL), row(D_MODEL), _full((8, D_MODEL))),
        out_shape=(S((T, D_FF), BF16), S((T, D_FF), BF16), S((T, D_MODEL), BF16), S((T, D_MODEL), F32),
                   S((8, D_MODEL), F32)),
        scratch_shapes=[pltpu.VMEM((D_FF, D_MODEL), BF16)] * 3 + [pltpu.SemaphoreType.DMA((3,))],
        compiler_params=_params(56, 1),
    )(n1, tgt, l1g, l1b, l2g, l2b, wgt, wut, wd)


def _bwd_mlp(dr2, gate, up, n1, rstd1, l1g, wgt, wut, wd, tm):
    T = n1.shape[0]
    nt = T // tm
    nf = D_FF // MXU_COLS

    def body(dr2_ref, gate_ref, up_ref, n1_ref, rstd1_ref, l1g_ref, wg_hbm, wu_hbm, wd_hbm,
             dgate_ref, dup_ref, hh_ref, dr1_ref, stat_ref, wg_s, wu_s, wd_s, sems):
        i = pl.program_id(0)

        @pl.when(i == 0)
        def _():
            _load_resident([(wg_hbm, wg_s), (wu_hbm, wu_s), (wd_hbm, wd_s)], sems)
            stat_ref[...] = jnp.zeros(stat_ref.shape, F32)

        dr2 = dr2_ref[...]
        dr2b = dr2.astype(BF16)
        for f in range(nf):
            cols = slice(MXU_COLS * f, MXU_COLS * (f + 1))
            dhh = _dot_nt(dr2b, wd_s[cols, :])
            gt = gate_ref[:, cols].astype(F32)
            ut = up_ref[:, cols].astype(F32)
            sg = jax.nn.sigmoid(gt)
            silu = gt * sg
            dgate_ref[:, cols] = (dhh * ut * (sg * (1.0 + gt * (1.0 - sg)))).astype(BF16)
            dup_ref[:, cols] = (dhh * silu).astype(BF16)
            hh_ref[:, cols] = (silu * ut).astype(BF16)
        dx1 = ALPHA * dr2 + _dot(dgate_ref[...], wg_s[...]) + _dot(dup_ref[...], wu_s[...])
        n1 = n1_ref[...]
        stat_ref[0:1, :] += _colsum(dx1 * n1)
        stat_ref[1:2, :] += _colsum(dx1)
        dr1_ref[...] = _ln_bwd(dx1 * l1g_ref[...], n1, rstd1_ref[...])

    S = jax.ShapeDtypeStruct
    row = lambda w: pl.BlockSpec((tm, w), lambda i: (i, 0))
    return pl.pallas_call(
        body, name="bwd_mlp", grid=(nt,),
        in_specs=[row(D_MODEL), row(D_FF), row(D_FF), row(D_MODEL), row(1), _full((1, D_MODEL)), ANY, ANY, ANY],
        out_specs=(row(D_FF), row(D_FF), row(D_FF), row(D_MODEL), _full((8, D_MODEL))),
        out_shape=(S((T, D_FF), BF16), S((T, D_FF), BF16), S((T, D_FF), BF16), S((T, D_MODEL), F32),
                   S((8, D_MODEL), F32)),
        scratch_shapes=[pltpu.VMEM((D_FF, D_MODEL), BF16)] * 3 + [pltpu.SemaphoreType.DMA((3,))],
        compiler_params=_params(56, 1),
    )(dr2, gate, up, n1, rstd1, l1g, wgt, wut, wd)


def _bwd_mix(dr1, proj, phi, y, win_g, wout_g, sgu_g, sgu_b, wcat, wcatt, bs_full, cw, cg, cbeta, tm, ex):
    T = dr1.shape[0]
    nt = T // tm
    halo_blocks = tm // HALO

    def body(*refs):
        ins, outs, scratch, ex_refs = _hosted(ex, refs, 15, 6)
        (dr1_ref, proj_ref, halo_ref, phi_ref, y_ref, win_ref, wout_ref, sg_ref, sb_ref, wcat_ref, wcatt_ref, bs_ref,
         cw_ref, cg_ref, cbeta_ref) = ins
        gx_ref, dproj_ref, gws_out, gbs_out, gcw_ref, vec_ref = outs
        hpad, hshift, dypad, dyshift, dhbuf, dubuf, dvnbuf, gcw_acc, gws_ref, gbs_ref = scratch
        i = pl.program_id(0)
        tile = nt - 1 - i

        @pl.when(i == 0)
        def _():
            ex.start(*ex_refs)
            gws_ref[...] = jnp.zeros(gws_ref.shape, F32)
            gbs_ref[...] = jnp.zeros(gbs_ref.shape, F32)
            gcw_ref[...] = jnp.zeros(gcw_ref.shape, F32)
            vec_ref[...] = jnp.zeros(vec_ref.shape, F32)
            gcw_acc[...] = jnp.zeros(gcw_acc.shape, F32)
            dypad[tm:tm + HALO, :] = jnp.zeros((HALO, D_CONV), F32)

        dr1 = dr1_ref[...]
        dycat = _dot_nt(dr1.astype(BF16), wout_ref[...])
        pu = proj_ref[:, 0:D_SGU]
        pv = proj_ref[:, D_SGU:2 * D_SGU]
        u = pu * phi_ref[:, 0:D_SGU]
        vhat, rstd_v = _ln_fwd(pv * phi_ref[:, D_SGU:2 * D_SGU])
        vn = vhat * sg_ref[...] + sb_ref[...]
        lo = _lo_mask()
        for c in range(tm // CHUNK):
            rows = slice(CHUNK * c, CHUNK * (c + 1))
            for p in range(4):
                lanes = slice(CHUNK * p, CHUNK * (p + 1))
                vstack = _head_pair_stack(vn[rows, lanes], lo)
                mixed = _dot(wcat_ref[p], vstack) + bs_ref[:, lanes]
                d_a = dycat[rows, lanes]
                dubuf[rows, lanes] = d_a * mixed
                dm = d_a * u[rows, lanes]
                gbs_ref[:, lanes] += dm
                dstack = _head_pair_stack(dm, lo)
                gws_ref[2 * CHUNK * p:2 * CHUNK * (p + 1), :] += _dot_nt(dstack, vn[rows, lanes].astype(BF16))
                dvnbuf[rows, lanes] = _dot(wcatt_ref[p], dstack)
        dvn = dvnbuf[...]
        vec_ref[0:1, :] += _colsum(dvn * vhat)
        vec_ref[1:2, :] += _colsum(dvn)
        dv = _ln_bwd(dvn * sg_ref[...], vhat, rstd_v)
        dproj_ref[:, 0:D_SGU] = (dubuf[...] * _gelu_grad(pu, phi_ref[:, 0:D_SGU])).astype(BF16)
        dproj_ref[:, D_SGU:2 * D_SGU] = (dv * _gelu_grad(pv, phi_ref[:, D_SGU:2 * D_SGU])).astype(BF16)
        base = 2 * D_SGU
        a = proj_ref[:, base:base + D_CONV]
        sgm = jax.nn.sigmoid(proj_ref[:, base + D_CONV:base + 2 * D_CONV])
        h_before = halo_ref[:, 0:D_CONV] * jax.nn.sigmoid(halo_ref[:, D_CONV:2 * D_CONV])
        hpad[0:HALO, :] = jnp.where(tile > 0, h_before, 0.0)
        hpad[HALO:HALO + tm, :] = a * sgm
        _shifted_copies(hpad, hshift, tm + SHIFT_ROWS)
        h_offset = lambda k: HALO - (CONV_WIDTH - 1) + k
        yhat, rstd_y = _ln_fwd(y_ref[...])
        yn = yhat * cg_ref[...] + cbeta_ref[...]
        s = jax.nn.sigmoid(yn)
        dyn = dycat[:, D_SGU:D_SGU + D_CONV] * (s * (1.0 + yn * (1.0 - s)))
        vec_ref[3:4, :] += _colsum(dyn * yhat)
        vec_ref[4:5, :] += _colsum(dyn)
        dy = _ln_bwd(dyn * cg_ref[...], yhat, rstd_y)
        vec_ref[2:3, :] += _colsum(dy)
        dypad[0:tm, :] = dy
        _shifted_copies(dypad, dyshift, tm + SHIFT_ROWS)
        _causal_conv(dypad, dyshift, cw_ref, dhbuf, tm, lambda k: (CONV_WIDTH - 1) - k)
        _conv_weight_grad(dypad, hpad, hshift, gcw_acc, tm, h_offset)
        dypad[tm:tm + HALO, :] = dypad[0:HALO, :]
        dh = dhbuf[...]
        dproj_ref[:, base:base + D_CONV] = (dh * sgm).astype(BF16)
        dproj_ref[:, base + D_CONV:base + 2 * D_CONV] = (dh * a * sgm * (1.0 - sgm)).astype(BF16)
        gx_ref[...] = ALPHA * dr1 + _dot(dproj_ref[...], win_ref[...])

        @pl.when(i == _forward_step(nt))
        def _():
            ex.forward(*ex_refs)

        @pl.when(i == nt - 1)
        def _():
            gcw_ref[...] = gcw_acc[...].sum(axis=1)
            gws_out[...] = gws_ref[...].astype(BF16)
            gbs_out[...] = lax.dot_general(_head_selector(), gbs_ref[...], (((1,), (1,)), ((), ())),
                                           preferred_element_type=F32, precision=lax.Precision.HIGHEST)
            ex.wait(*ex_refs)

    S = jax.ShapeDtypeStruct
    row = lambda w: pl.BlockSpec((tm, w), lambda i: (nt - 1 - i, 0))
    halo = pl.BlockSpec((HALO, D_MODEL), lambda i: (jnp.maximum((nt - 1 - i) * halo_blocks - 1, 0), 1))
    res = pl.pallas_call(
        body, name="bwd_mix", grid=(nt,),
        in_specs=[row(D_MODEL), row(2 * D_MODEL), halo, row(2 * D_SGU), row(D_CONV), _full(win_g.shape),
                  _full(wout_g.shape), _full(sgu_g.shape), _full(sgu_b.shape), _full(wcat.shape), _full(wcatt.shape),
                  _full(bs_full.shape), _full(cw.shape), _full(cg.shape), _full(cbeta.shape)]
        + [ANY] * ex.n,
        out_specs=(row(D_MODEL), row(2 * D_MODEL), _full((N_HEADS * CHUNK, CHUNK)), _full((N_HEADS, CHUNK)),
                   _full((CONV_ROWS, D_CONV)), _full((8, D_CONV))) + (ANY,) * ex.n,
        out_shape=(S((T, D_MODEL), F32), S((T, 2 * D_MODEL), BF16), S((N_HEADS * CHUNK, CHUNK), BF16),
                   S((N_HEADS, CHUNK), F32), S((CONV_ROWS, D_CONV), F32), S((8, D_CONV), F32), *ex.out_shape),
        scratch_shapes=[pltpu.VMEM((tm + HALO, D_CONV), F32), pltpu.VMEM((SUBLANES - 1, tm + SHIFT_ROWS, D_CONV), F32),
                        pltpu.VMEM((tm + HALO, D_CONV), F32), pltpu.VMEM((SUBLANES - 1, tm + SHIFT_ROWS, D_CONV), F32),
                        pltpu.VMEM((tm, D_CONV), F32), pltpu.VMEM((tm, D_SGU), F32),
                        pltpu.VMEM((tm, D_SGU), F32), pltpu.VMEM((CONV_ROWS, 8, D_CONV), F32),
                        pltpu.VMEM((N_HEADS * CHUNK, CHUNK), F32), pltpu.VMEM((CHUNK, D_SGU), F32)] + ex.scratch,
        compiler_params=_params(56, 1),
    )(dr1, proj, proj, phi, y, win_g, wout_g, sgu_g, sgu_b, wcat, wcatt, bs_full, cw, cg, cbeta, *ex.arrays)
    return res[:6], res[6:]


def _wgrad(name, a, b, blocks, tk, ex=None):
    T, M = a.shape
    N = b.shape[1]
    nk = T // tk
    out_shape = (blocks, M // blocks, N)
    ex = ex or _Exchange([], [])

    def body(*refs):
        (a_ref, b_ref), (o_ref,), (acc,), ex_refs = _hosted(ex, refs, 2, 1)
        i = pl.program_id(0)

        @pl.when(i == 0)
        def _():
            ex.start(*ex_refs)
            acc[...] = jnp.zeros(acc.shape, F32)

        acc[...] += _dot_tn(a_ref[...].astype(BF16), b_ref[...].astype(BF16))

        @pl.when(i == _forward_step(nk))
        def _():
            ex.forward(*ex_refs)

        @pl.when(i == nk - 1)
        def _():
            o_ref[...] = acc[...].astype(BF16)
            ex.wait(*ex_refs)

    res = pl.pallas_call(
        body, name=name, grid=(nk,),
        in_specs=[pl.BlockSpec((tk, M), lambda i: (i, 0)), pl.BlockSpec((tk, N), lambda i: (i, 0))] + [ANY] * ex.n,
        out_specs=(_full((M, N)),) + (ANY,) * ex.n,
        out_shape=(jax.ShapeDtypeStruct((M, N), BF16), *ex.out_shape),
        scratch_shapes=[pltpu.VMEM((M, N), F32)] + ex.scratch,
        compiler_params=_params(56, 1),
    )(a, b, *ex.arrays)
    g = res[0].reshape(out_shape)
    return (g, res[1:]) if ex.n else g


def _adamw(w, g, m, v):
    m2 = ADAM_B1 * m + (1.0 - ADAM_B1) * g
    v2 = ADAM_B2 * v + (1.0 - ADAM_B2) * (g * g)
    m_hat = m2 / (1.0 - ADAM_B1 ** ADAM_STEP)
    v_hat = v2 / (1.0 - ADAM_B2 ** ADAM_STEP)
    delta = -ADAM_LR * (m_hat / (jnp.sqrt(v_hat) + ADAM_EPS) + ADAM_WD * w)
    return delta, m2, v2


def _sum_partials(r_ref):
    g = r_ref[0].astype(F32)
    for s in range(1, N_DEV):
        g = g + r_ref[s].astype(F32)
    return g


def _adamw_shard(name, parts, w, m, v, transposed):
    def body(r_ref, w_ref, m_ref, v_ref, g_o, d_o, m_o, v_o):
        g = _sum_partials(r_ref)
        if transposed:
            g = g.T
        delta, m2, v2 = _adamw(w_ref[...], g, m_ref[...], v_ref[...])
        g_o[...] = g
        d_o[...] = delta
        m_o[...] = m2
        v_o[...] = v2

    return pl.pallas_call(
        body, name=name, out_shape=(jax.ShapeDtypeStruct(w.shape, F32),) * 4, compiler_params=_params(40),
    )(parts, w, m, v)


def _finish_small(gws8, gbs8, gcw8, vmix8, vmlp8, vout8, small):
    names = ["sgu_ln_g", "sgu_ln_b", "w_s", "b_s", "conv_b", "conv_ln_g", "conv_ln_b", "ln1_g", "ln1_b", "ln2_g", "ln2_b"]
    flat = []
    for n in names:
        flat += list(small[n])

    def body(*refs):
        gws_ref, gbs_ref, gcw_ref, vmix_ref, vmlp_ref, vout_ref = refs[:6]
        wmv = refs[6:6 + 3 * len(names)]
        outs = refs[6 + 3 * len(names):]
        loss_o, gcw_o = outs[0], outs[1]
        outs = outs[2:]
        gws = _sum_partials(gws_ref)
        gbs = _sum_partials(gbs_ref)
        vmix = _sum_partials(vmix_ref)
        vmlp = _sum_partials(vmlp_ref)
        vout = _sum_partials(vout_ref)
        gcw_o[...] = _sum_partials(gcw_ref)
        loss_o[...] = (0.5 / D_MODEL) * jnp.sum(vout[0:1, :], axis=1, keepdims=True)
        rows = lax.broadcasted_iota(jnp.int32, (N_HEADS * CHUNK, CHUNK), 0)
        cols = lax.broadcasted_iota(jnp.int32, (N_HEADS * CHUNK, CHUNK), 1)
        gws = jnp.where((rows & (CHUNK - 1)) >= cols, gws, 0.0)
        grads = {
            "sgu_ln_g": vmix[0:1, :], "sgu_ln_b": vmix[1:2, :], "w_s": gws, "b_s": gbs,
            "conv_b": vmix[2:3, :], "conv_ln_g": vmix[3:4, :], "conv_ln_b": vmix[4:5, :],
            "ln1_g": vmlp[0:1, :], "ln1_b": vmlp[1:2, :], "ln2_g": vout[1:2, :], "ln2_b": vout[2:3, :],
        }
        for k, n in enumerate(names):
            w_ref, m_ref, v_ref = wmv[3 * k:3 * k + 3]
            g = grads[n]
            delta, m2, v2 = _adamw(w_ref[...], g, m_ref[...], v_ref[...])
            outs[4 * k][...] = g
            outs[4 * k + 1][...] = delta
            outs[4 * k + 2][...] = m2
            outs[4 * k + 3][...] = v2

    S = jax.ShapeDtypeStruct
    out_shape = [S((1, 1), F32), S((CONV_ROWS, D_CONV), F32)]
    for n in names:
        out_shape += [S(small[n][0].shape, F32)] * 4
    res = pl.pallas_call(
        body, name="finish_small", out_shape=tuple(out_shape),
        compiler_params=_params(40),
    )(gws8, gbs8, gcw8, vmix8, vmlp8, vout8, *flat)
    upd = {n: res[2 + 4 * k:6 + 4 * k] for k, n in enumerate(names)}
    return res[0], res[1], upd


def _adamw_plain(name, g, w, m, v):
    def body(g_ref, w_ref, m_ref, v_ref, d_o, m_o, v_o):
        delta, m2, v2 = _adamw(w_ref[...], g_ref[...], m_ref[...], v_ref[...])
        d_o[...] = delta
        m_o[...] = m2
        v_o[...] = v2

    return pl.pallas_call(
        body, name=name, out_shape=(jax.ShapeDtypeStruct(w.shape, F32),) * 3,
    )(g, w, m, v)


TOKEN_TILE_MIX = 256
TOKEN_TILE_FWD_MLP = 512
TOKEN_TILE_BWD_MLP = 256
TOKEN_TILE_WGRAD = 512


def kernel(x, w_in, sgu_ln_g, sgu_ln_b, w_s, b_s, conv_w, conv_b, conv_ln_g, conv_ln_b, w_out, ln1_g, ln1_b, w_gate, w_up, w_down, ln2_g, ln2_b, loss_target, m_w_in, m_sgu_ln_g, m_sgu_ln_b, m_w_s, m_b_s, m_conv_w, m_conv_b, m_conv_ln_g, m_conv_ln_b, m_w_out, m_ln1_g, m_ln1_b, m_w_gate, m_w_up, m_w_down, m_ln2_g, m_ln2_b, v_w_in, v_sgu_ln_g, v_sgu_ln_b, v_w_s, v_b_s, v_conv_w, v_conv_b, v_conv_ln_g, v_conv_ln_b, v_w_out, v_ln1_g, v_ln1_b, v_w_gate, v_w_up, v_w_down, v_ln2_g, v_ln2_b):
    xs = x[0]
    tgt = loss_target[0]

    (win_b, wout_b, wgt_b, wut_b, wd_b, cw_b, wcat, wcatt, bs_full) = _prep_weights(
        w_in[0], w_out[0], w_gate[0].T, w_up[0].T, w_down[0], conv_w[0], w_s[0], b_s[0])
    win_g, wout_g, cw_g = _exchange("gather_mix_weights", [], [win_b, wout_b, cw_b])
    win_g = win_g.reshape(2 * D_MODEL, D_MODEL)
    wout_g = wout_g.reshape(D_MODEL, D_MODEL)
    cw = jnp.transpose(cw_g[:, :, :D_CONV // N_DEV], (1, 0, 2)).reshape(CONV_ROWS, D_CONV)

    (proj, ycat, n1, rstd1, phi, y_conv), (wgt_g, wut_g, wd_g) = _fwd_mix(
        xs, win_g, wout_g, sgu_ln_g, sgu_ln_b, wcat, bs_full, cw, conv_b, conv_ln_g, conv_ln_b, TOKEN_TILE_MIX,
        _Exchange([], [wgt_b, wut_b, wd_b]))
    wgt_g = wgt_g.reshape(D_FF, D_MODEL)
    wut_g = wut_g.reshape(D_FF, D_MODEL)
    wd_g = wd_g.reshape(D_FF, D_MODEL)
    gate, up, x1b, dr2, vout = _fwd_mlp(n1, tgt, ln1_g, ln1_b, ln2_g, ln2_b, wgt_g, wut_g, wd_g, TOKEN_TILE_FWD_MLP)

    dgate, dup, hh, dr1, vmlp = _bwd_mlp(dr2, gate, up, n1, rstd1, ln1_g, wgt_g, wut_g, wd_g, TOKEN_TILE_BWD_MLP)
    tk = TOKEN_TILE_WGRAD
    g_wgt = _wgrad("wgrad_gate", dgate, x1b, N_DEV, tk)
    g_wut = _wgrad("wgrad_up", dup, x1b, N_DEV, tk)
    g_wd = _wgrad("wgrad_down", hh, dr2, N_DEV, tk)
    (gx, dproj, gws, gbs, gcw, vmix), (r_wgt, r_wut, r_wd) = _bwd_mix(
        dr1, proj, phi, y_conv, win_g, wout_g, sgu_ln_g, sgu_ln_b, wcat, wcatt, bs_full, cw, conv_ln_g, conv_ln_b,
        TOKEN_TILE_MIX, _Exchange([g_wgt, g_wut, g_wd], []))
    g_win, (gws8, gbs8, gcw8, vmix8, vmlp8, vout8) = _wgrad(
        "wgrad_in", dproj, xs, N_DEV, tk,
        _Exchange([], [gws, gbs, gcw, vmix, vmlp, vout]))
    g_wout, (r_win,) = _wgrad("wgrad_out", ycat, dr1, N_DEV, tk, _Exchange([g_win], []))
    (r_wout,) = _exchange("exchange_grad_out", [g_wout], [])

    big = {
        "w_in": _adamw_shard("adamw_in", r_win, w_in[0], m_w_in[0], v_w_in[0], True),
        "w_out": _adamw_shard("adamw_out", r_wout, w_out[0], m_w_out[0], v_w_out[0], False),
        "w_gate": _adamw_shard("adamw_gate", r_wgt, w_gate[0].T, m_w_gate[0].T, v_w_gate[0].T, False),
        "w_up": _adamw_shard("adamw_up", r_wut, w_up[0].T, m_w_up[0].T, v_w_up[0].T, False),
        "w_down": _adamw_shard("adamw_down", r_wd, w_down[0], m_w_down[0], v_w_down[0], False),
    }
    small_in = {
        "sgu_ln_g": (sgu_ln_g, m_sgu_ln_g, v_sgu_ln_g), "sgu_ln_b": (sgu_ln_b, m_sgu_ln_b, v_sgu_ln_b),
        "w_s": tuple(a.reshape(N_HEADS * CHUNK, CHUNK) for a in (w_s, m_w_s, v_w_s)),
        "b_s": (b_s[0], m_b_s[0], v_b_s[0]),
        "conv_b": (conv_b, m_conv_b, v_conv_b), "conv_ln_g": (conv_ln_g, m_conv_ln_g, v_conv_ln_g),
        "conv_ln_b": (conv_ln_b, m_conv_ln_b, v_conv_ln_b),
        "ln1_g": (ln1_g, m_ln1_g, v_ln1_g), "ln1_b": (ln1_b, m_ln1_b, v_ln1_b),
        "ln2_g": (ln2_g, m_ln2_g, v_ln2_g), "ln2_b": (ln2_b, m_ln2_b, v_ln2_b),
    }
    loss11, gcw_full, small = _finish_small(gws8, gbs8, gcw8, vmix8, vmlp8, vout8, small_in)

    me = 4 * lax.axis_index("x") + 2 * lax.axis_index("y") + lax.axis_index("c")
    g_cw = lax.dynamic_slice(gcw_full, (0, me * (D_CONV // N_DEV)), (CONV_WIDTH, D_CONV // N_DEV))
    d_cw, m_cw, v_cw = _adamw_plain("adamw_conv_w", g_cw, conv_w[0], m_conv_w[0], v_conv_w[0])

    shapes = {"w_s": w_s.shape, "b_s": b_s.shape}
    out = {}
    for n, r in big.items():
        out[n] = tuple((a.T if n in ("w_gate", "w_up") else a)[None] for a in r)
    for n, r in small.items():
        out[n] = tuple(a.reshape(shapes[n]) for a in r) if n in shapes else tuple(r)
    out["conv_w"] = tuple(a[None] for a in (g_cw, d_cw, m_cw, v_cw))

    order = ["w_in", "sgu_ln_g", "sgu_ln_b", "w_s", "b_s", "conv_w", "conv_b", "conv_ln_g", "conv_ln_b", "w_out",
             "ln1_g", "ln1_b", "w_gate", "w_up", "w_down", "ln2_g", "ln2_b"]
    loss = loss11[0, 0]
    return (loss, gx[None], *[out[n][0] for n in order], *[out[n][1] for n in order],
            *[out[n][2] for n in order], *[out[n][3] for n in order])
```

```python
import jax
import jax.numpy as jnp
from jax import lax
from jax.experimental import pallas as pl
from jax.experimental.pallas import tpu as pltpu

F32 = jnp.float32
BF16 = jnp.bfloat16

D_MODEL = 1024
D_SGU = 512
D_CONV = 512
N_HEADS = 8
CHUNK = 128
CONV_WIDTH = 31
CONV_ROWS = 32
HALO = 32
D_FF = 2816
N_DEV = 8
FF_SHARD = D_FF // N_DEV
ALPHA = (2.0 * 1) ** 0.25
LN_EPS = 1e-5
INV_SQRT2 = 0.7071067811865476
INV_SQRT_2PI = 0.3989422804014327

ADAM_LR = 0.001
ADAM_B1 = 0.9
ADAM_B2 = 0.999
ADAM_EPS = 1e-08
ADAM_WD = 0.01
ADAM_STEP = 10

MXU_COLS = 256
SUBLANES = 8
CONV_ROW_BLOCK = 32
WGRAD_ROW_BLOCK = 32
SHIFT_ROWS = HALO - SUBLANES
MIB = 1024 * 1024

VMEM = pl.BlockSpec(memory_space=pltpu.VMEM)
ANY = pl.BlockSpec(memory_space=pl.ANY)
MESH = pl.DeviceIdType.MESH


def _params(vmem_mib, grid_dims=0):
    kw = dict(vmem_limit_bytes=vmem_mib * MIB)
    if grid_dims:
        kw["dimension_semantics"] = ("arbitrary",) * grid_dims
    return pltpu.CompilerParams(**kw)


def _full(shape):
    return pl.BlockSpec(shape, lambda i: (0,) * len(shape))


def _dot(a, b):
    return jnp.dot(a, b, preferred_element_type=F32)


def _dot_nt(a, b):
    return lax.dot_general(a, b, (((1,), (1,)), ((), ())), preferred_element_type=F32)


def _dot_tn(a, b):
    return lax.dot_general(a, b, (((0,), (0,)), ((), ())), preferred_element_type=F32)


def _normal_cdf(x):
    return 0.5 * (1.0 + lax.erf(x * INV_SQRT2))


def _gelu_grad(x, cdf):
    return cdf + x * jnp.exp(-0.5 * x * x) * INV_SQRT_2PI


def _ln_fwd(v):
    mu = jnp.mean(v, axis=-1, keepdims=True)
    d = v - mu
    var = jnp.mean(d * d, axis=-1, keepdims=True)
    rstd = lax.rsqrt(var + LN_EPS)
    return d * rstd, rstd


def _ln_bwd(dyhat, yhat, rstd):
    m1 = jnp.mean(dyhat, axis=-1, keepdims=True)
    m2 = jnp.mean(dyhat * yhat, axis=-1, keepdims=True)
    return rstd * (dyhat - m1 - yhat * m2)


def _colsum(v):
    return jnp.sum(v, axis=0, keepdims=True)


def _head_pair_stack(v, lo):
    return jnp.concatenate([jnp.where(lo, v, 0.0), jnp.where(lo, 0.0, v)], axis=0).astype(BF16)


def _lo_mask():
    return lax.broadcasted_iota(jnp.int32, (CHUNK, CHUNK), 1) < (CHUNK // 2)


def _head_selector():
    head = lax.broadcasted_iota(jnp.int32, (N_HEADS, D_SGU), 0)
    lane = lax.broadcasted_iota(jnp.int32, (N_HEADS, D_SGU), 1)
    width = D_SGU // N_HEADS
    return ((lane >= head * width) & (lane < (head + 1) * width)).astype(F32)


def _shifted_copies(pad_ref, sh_ref, rows):
    for r in range(1, SUBLANES):
        sh_ref[r - 1, 0:rows, :] = pad_ref[pl.ds(r, rows), :]


def _tap_groups(offset_of_tap):
    groups = {}
    for k in range(CONV_WIDTH):
        o = offset_of_tap(k)
        groups.setdefault(o % SUBLANES, []).append((k, o // SUBLANES))
    return groups


def _tap_window(pad_ref, sh_ref, r, taps, row0, rows):
    q0 = min(q for _, q in taps)
    q1 = max(q for _, q in taps)
    src = pad_ref if r == 0 else sh_ref.at[r - 1]
    win = src[pl.ds(row0 + SUBLANES * q0, SUBLANES * (q1 - q0) + rows), :]
    return win, [(k, SUBLANES * (q - q0)) for k, q in taps]


def _causal_conv(pad_ref, sh_ref, w_ref, out_ref, rows, offset_of_tap, bias=None):
    groups = _tap_groups(offset_of_tap)

    def block(b, carry):
        row0 = pl.multiple_of(b * CONV_ROW_BLOCK, CONV_ROW_BLOCK)
        if bias is None:
            acc = jnp.zeros((CONV_ROW_BLOCK, D_CONV), F32)
        else:
            acc = jnp.broadcast_to(bias, (CONV_ROW_BLOCK, D_CONV))
        for r, taps in groups.items():
            win, starts = _tap_window(pad_ref, sh_ref, r, taps, row0, CONV_ROW_BLOCK)
            for k, s in starts:
                acc = acc + w_ref[k:k + 1, :] * win[s:s + CONV_ROW_BLOCK, :]
        out_ref[pl.ds(row0, CONV_ROW_BLOCK), :] = acc
        return carry

    lax.fori_loop(0, rows // CONV_ROW_BLOCK, block, 0)


def _conv_weight_grad(dy_ref, pad_ref, sh_ref, acc_ref, rows, offset_of_tap):
    groups = _tap_groups(offset_of_tap)
    for r, taps in groups.items():

        def block(b, parts, r=r, taps=taps):
            row0 = pl.multiple_of(b * WGRAD_ROW_BLOCK, WGRAD_ROW_BLOCK)
            dyb = dy_ref[pl.ds(row0, WGRAD_ROW_BLOCK), :]
            win, starts = _tap_window(pad_ref, sh_ref, r, taps, row0, WGRAD_ROW_BLOCK)
            out = []
            for part, (_, s) in zip(parts, starts):
                pr = dyb * win[s:s + WGRAD_ROW_BLOCK, :]
                out.append(part + pr.reshape(WGRAD_ROW_BLOCK // SUBLANES, SUBLANES, D_CONV).sum(axis=0))
            return tuple(out)

        zeros = tuple(jnp.zeros((SUBLANES, D_CONV), F32) for _ in taps)
        parts = lax.fori_loop(0, rows // WGRAD_ROW_BLOCK, block, zeros)
        for part, (k, _) in zip(parts, taps):
            acc_ref[k] += part


def _prep_weights(w_in, w_out, w_gate_t, w_up_t, w_down, conv_w, w_s, b_s):
    def body(win_ref, wout_ref, wgt_ref, wut_ref, wd_ref, cw_ref, ws_ref, bs_ref,
             win_o, wout_o, wgt_o, wut_o, wd_o, cw_o, wcat_o, wcatt_o, bsf_o):
        win_o[...] = win_ref[...].T.astype(BF16)
        wout_o[...] = wout_ref[...].astype(BF16)
        wgt_o[...] = wgt_ref[...].astype(BF16)
        wut_o[...] = wut_ref[...].astype(BF16)
        wd_o[...] = wd_ref[...].astype(BF16)
        cw_o[...] = jnp.zeros(cw_o.shape, F32)
        cw_o[0:CONV_WIDTH, 0:D_CONV // N_DEV] = cw_ref[...]
        row = lax.broadcasted_iota(jnp.int32, (CHUNK, CHUNK), 0)
        col = lax.broadcasted_iota(jnp.int32, (CHUNK, CHUNK), 1)
        causal = row >= col
        for h in range(N_HEADS):
            w = jnp.where(causal, ws_ref[h], 0.0)
            p, half = h // 2, (h % 2) * CHUNK
            wcat_o[p, :, half:half + CHUNK] = w.astype(BF16)
            wcatt_o[p, :, half:half + CHUNK] = w.T.astype(BF16)
        bsf_o[...] = lax.dot_general(bs_ref[...], _head_selector(), (((0,), (0,)), ((), ())),
                                     preferred_element_type=F32, precision=lax.Precision.HIGHEST)

    S = jax.ShapeDtypeStruct
    return pl.pallas_call(
        body, name="prep_weights",
        out_shape=(S((256, D_MODEL), BF16), S((128, D_MODEL), BF16), S((FF_SHARD, D_MODEL), BF16),
                   S((FF_SHARD, D_MODEL), BF16), S((FF_SHARD, D_MODEL), BF16), S((CONV_ROWS, 128), F32),
                   S((4, CHUNK, 2 * CHUNK), BF16), S((4, CHUNK, 2 * CHUNK), BF16), S((CHUNK, D_SGU), F32)),
        compiler_params=_params(32),
    )(w_in, w_out, w_gate_t, w_up_t, w_down, conv_w, w_s, b_s)


def _mesh_position():
    x, y, c = lax.axis_index("x"), lax.axis_index("y"), lax.axis_index("c")
    return x, y, c


def _peers(x, y, c):
    out = []
    for k in range(1, N_DEV):
        px = 1 - x if (k >> 2) & 1 else x
        py = 1 - y if (k >> 1) & 1 else y
        pc = 1 - c if k & 1 else c
        out.append(((px, py, pc), 4 * px + 2 * py + pc))
    return out


class _Exchange:
    def __init__(self, scatter, gather):
        self.arrays = list(scatter) + list(gather)
        self.n_sc = len(scatter)
        self.n = len(self.arrays)
        self.out_shape = [jax.ShapeDtypeStruct(a.shape if k < self.n_sc else (N_DEV,) + a.shape, a.dtype)
                          for k, a in enumerate(self.arrays)]
        n_remote = self.n * (N_DEV - 1)
        self.scratch = [pltpu.SemaphoreType.DMA((n_remote,)), pltpu.SemaphoreType.DMA((n_remote,)),
                        pltpu.SemaphoreType.DMA((self.n,))] if self.n else []

    def _copies(self, src, dst, sems):
        send_sems, recv_sems, local_sems = sems
        x, y, c = _mesh_position()
        me = 4 * x + 2 * y + c
        locals_, first, arrivals, passed, last = [], [], [], [], []

        def remote(a, k, src_ref, slot, to):
            s = a * (N_DEV - 1) + k
            return pltpu.make_async_remote_copy(src_ref=src_ref, dst_ref=dst[a].at[slot], send_sem=send_sems.at[s],
                                                recv_sem=recv_sems.at[s], device_id=to, device_id_type=MESH)

        for a in range(self.n):
            if a < self.n_sc:
                locals_.append(pltpu.make_async_copy(src[a].at[me], dst[a].at[me], local_sems.at[a]))
                for k, (peer, pid) in enumerate(_peers(x, y, c)):
                    first.append(remote(a, k, src[a].at[pid], me, peer))
                    last.append(remote(a, k, src[a].at[pid], pid, peer))
                continue
            locals_.append(pltpu.make_async_copy(src[a], dst[a].at[me], local_sems.at[a]))
            sibling, sib_id = (x, y, 1 - c), 4 * x + 2 * y + (1 - c)
            chips = [(1 - x, y), (x, 1 - y), (1 - x, 1 - y)]
            first.append(remote(a, 0, src[a], me, sibling))
            last.append(remote(a, 0, src[a], sib_id, sibling))
            for j, (px, py) in enumerate(chips):
                same, other = 4 * px + 2 * py + c, 4 * px + 2 * py + (1 - c)
                first.append(remote(a, 1 + j, src[a], me, (px, py, c)))
                arrivals.append(remote(a, 1 + j, src[a], same, (px, py, c)))
                passed.append(remote(a, 4 + j, dst[a].at[same], same, sibling))
                last.append(remote(a, 4 + j, dst[a].at[other], other, sibling))
        return locals_, first, arrivals, passed, last

    def start(self, src, dst, sems):
        if not self.n:
            return
        locals_, first, _, _, _ = self._copies(src, dst, sems)
        for cp in locals_ + first:
            cp.start()

    def forward(self, src, dst, sems):
        if self.n == self.n_sc:
            return
        _, _, arrivals, passed, _ = self._copies(src, dst, sems)
        for arrived, cp in zip(arrivals, passed):
            arrived.wait_recv()
            cp.start()

    def wait(self, src, dst, sems):
        if not self.n:
            return
        locals_, first, _, passed, last = self._copies(src, dst, sems)
        for cp in last:
            cp.wait_recv()
        for cp in first + passed:
            cp.wait_send()
        for cp in locals_:
            cp.wait()


def _exchange(name, scatter, gather):
    ex = _Exchange(scatter, gather)
    n = ex.n

    def body(*refs):
        src, dst, sems = refs[:n], refs[n:2 * n], refs[2 * n:]
        ex.start(src, dst, sems)
        ex.forward(src, dst, sems)
        ex.wait(src, dst, sems)

    return pl.pallas_call(
        body, name=name, out_shape=tuple(ex.out_shape), in_specs=[ANY] * n, out_specs=(ANY,) * n,
        scratch_shapes=ex.scratch,
    )(*ex.arrays)


def _forward_step(n_steps):
    return (11 * n_steps) // 16


def _hosted(ex, refs, n_in, n_out):
    ins, ex_src = refs[:n_in], refs[n_in:n_in + ex.n]
    rest = refs[n_in + ex.n:]
    outs, ex_dst = rest[:n_out], rest[n_out:n_out + ex.n]
    rest = rest[n_out + ex.n:]
    n_own = len(rest) - len(ex.scratch)
    return ins, outs, rest[:n_own], (ex_src, ex_dst, rest[n_own:])


def _fwd_mix(x, win_g, wout_g, sgu_g, sgu_b, wcat, bs_full, cw, cb, cg, cbeta, tm, ex):
    T = x.shape[0]
    nt = T // tm

    def body(*refs):
        ins, outs, scratch, ex_refs = _hosted(ex, refs, 11, 6)
        x_ref, win_ref, wout_ref, sg_ref, sb_ref, wcat_ref, bs_ref, cw_ref, cb_ref, cg_ref, cbeta_ref = ins
        proj_ref, ycat_ref, n1_ref, rstd1_ref, phi_ref, y_ref = outs
        hpad, hshift = scratch
        i = pl.program_id(0)

        @pl.when(i == 0)
        def _():
            ex.start(*ex_refs)

        xf = x_ref[...]
        xb = xf.astype(BF16)
        proj_ref[...] = _dot_nt(xb, win_ref[...])
        phi_ref[...] = _normal_cdf(proj_ref[:, 0:2 * D_SGU])
        u = proj_ref[:, 0:D_SGU] * phi_ref[:, 0:D_SGU]
        v = proj_ref[:, D_SGU:2 * D_SGU] * phi_ref[:, D_SGU:2 * D_SGU]
        vhat, _ = _ln_fwd(v)
        vn = vhat * sg_ref[...] + sb_ref[...]
        lo = _lo_mask()
        for c in range(tm // CHUNK):
            rows = slice(CHUNK * c, CHUNK * (c + 1))
            for p in range(4):
                lanes = slice(CHUNK * p, CHUNK * (p + 1))
                mixed = _dot(wcat_ref[p], _head_pair_stack(vn[rows, lanes], lo)) + bs_ref[:, lanes]
                ycat_ref[rows, lanes] = (u[rows, lanes] * mixed).astype(BF16)
        base = 2 * D_SGU
        a = proj_ref[:, base:base + D_CONV]
        g = proj_ref[:, base + D_CONV:base + 2 * D_CONV]

        @pl.when(i == 0)
        def _():
            hpad[0:HALO, :] = jnp.zeros((HALO, D_CONV), F32)

        hpad[HALO:HALO + tm, :] = a * jax.nn.sigmoid(g)
        _shifted_copies(hpad, hshift, tm + SHIFT_ROWS)
        _causal_conv(hpad, hshift, cw_ref, y_ref, tm, lambda k: HALO - (CONV_WIDTH - 1) + k, bias=cb_ref[...])
        hpad[0:HALO, :] = hpad[tm:tm + HALO, :]
        yhat, _ = _ln_fwd(y_ref[...])
        yn = yhat * cg_ref[...] + cbeta_ref[...]
        ycat_ref[:, D_SGU:D_SGU + D_CONV] = (yn * jax.nn.sigmoid(yn)).astype(BF16)
        r1 = ALPHA * xf + _dot(ycat_ref[...], wout_ref[...])
        n1, rstd1 = _ln_fwd(r1)
        n1_ref[...] = n1
        rstd1_ref[...] = rstd1

        @pl.when(i == _forward_step(nt))
        def _():
            ex.forward(*ex_refs)

        @pl.when(i == nt - 1)
        def _():
            ex.wait(*ex_refs)

    S = jax.ShapeDtypeStruct
    row = lambda w: pl.BlockSpec((tm, w), lambda i: (i, 0))
    res = pl.pallas_call(
        body, name="fwd_mix", grid=(nt,),
        in_specs=[row(D_MODEL), _full(win_g.shape), _full(wout_g.shape), _full(sgu_g.shape), _full(sgu_b.shape),
                  _full(wcat.shape), _full(bs_full.shape), _full(cw.shape), _full(cb.shape), _full(cg.shape),
                  _full(cbeta.shape)] + [ANY] * ex.n,
        out_specs=(row(2 * D_MODEL), row(D_MODEL), row(D_MODEL), row(1), row(2 * D_SGU), row(D_CONV)) + (ANY,) * ex.n,
        out_shape=(S((T, 2 * D_MODEL), F32), S((T, D_MODEL), BF16), S((T, D_MODEL), F32), S((T, 1), F32),
                   S((T, 2 * D_SGU), F32), S((T, D_CONV), F32), *ex.out_shape),
        scratch_shapes=[pltpu.VMEM((tm + HALO, D_CONV), F32),
                        pltpu.VMEM((SUBLANES - 1, tm + SHIFT_ROWS, D_CONV), F32)] + ex.scratch,
        compiler_params=_params(56, 1),
    )(x, win_g, wout_g, sgu_g, sgu_b, wcat, bs_full, cw, cb, cg, cbeta, *ex.arrays)
    return res[:6], res[6:]


def _load_resident(pairs, sems):
    cps = [pltpu.make_async_copy(s, d, sems.at[k]) for k, (s, d) in enumerate(pairs)]
    for cp in cps:
        cp.start()
    for cp in cps:
        cp.wait()


def _fwd_mlp(n1, tgt, l1g, l1b, l2g, l2b, wgt, wut, wd, tm):
    T = n1.shape[0]
    nt = T // tm
    nf = D_FF // MXU_COLS

    def body(n1_ref, tgt_ref, l1g_ref, l1b_ref, l2g_ref, l2b_ref, wg_hbm, wu_hbm, wd_hbm,
             gate_ref, up_ref, hh_ref, x1b_ref, dr2_ref, stat_ref, wg_s, wu_s, wd_s, sems):
        i = pl.program_id(0)

        @pl.when(i == 0)
        def _():
            _load_resident([(wg_hbm, wg_s), (wu_hbm, wu_s), (wd_hbm, wd_s)], sems)
            stat_ref[...] = jnp.zeros(stat_ref.shape, F32)

        x1 = n1_ref[...] * l1g_ref[...] + l1b_ref[...]
        x1b = x1.astype(BF16)
        x1b_ref[...] = x1b
        for f in range(nf):
            cols = slice(MXU_COLS * f, MXU_COLS * (f + 1))
            gt = _dot_nt(x1b, wg_s[cols, :])
            ut = _dot_nt(x1b, wu_s[cols, :])
            gate_ref[:, cols] = gt.astype(BF16)
            up_ref[:, cols] = ut.astype(BF16)
            hh_ref[:, cols] = (gt * jax.nn.sigmoid(gt) * ut).astype(BF16)
        r2 = ALPHA * x1 + _dot(hh_ref[...], wd_s[...])
        n2, rstd2 = _ln_fwd(r2)
        x2 = n2 * l2g_ref[...] + l2b_ref[...]
        diff = x2 - tgt_ref[...]
        dx2 = diff * (1.0 / D_MODEL)
        stat_ref[0:1, :] += _colsum(diff * diff)
        stat_ref[1:2, :] += _colsum(dx2 * n2)
        stat_ref[2:3, :] += _colsum(dx2)
        dr2_ref[...] = _ln_bwd(dx2 * l2g_ref[...], n2, rstd2)

    S = jax.ShapeDtypeStruct
    row = lambda w: pl.BlockSpec((tm, w), lambda i: (i, 0))
    vec = _full((1, D_MODEL))
    return pl.pallas_call(
        body, name="fwd_mlp", grid=(nt,),
        in_specs=[row(D_MODEL), row(D_MODEL), vec, vec, vec, vec, ANY, ANY, ANY],
        out_specs=(row(D_FF), row(D_FF), row(D_FF), row(D_MODEL), row(D_MODEL), _full((8, D_MODEL))),
        out_shape=(S((T, D_FF), BF16), S((T, D_FF), BF16), S((T, D_FF), BF16), S((T, D_MODEL), BF16),
                   S((T, D_MODEL), F32), S((8, D_MODEL), F32)),
        scratch_shapes=[pltpu.VMEM((D_FF, D_MODEL), BF16)] * 3 + [pltpu.SemaphoreType.DMA((3,))],
        compiler_params=_params(56, 1),
    )(n1, tgt, l1g, l1b, l2g, l2b, wgt, wut, wd)


def _bwd_mlp(dr2, gate, up, n1, rstd1, l1g, wgt, wut, wd, tm):
    T = n1.shape[0]
    nt = T // tm
    nf = D_FF // MXU_COLS

    def body(dr2_ref, gate_ref, up_ref, n1_ref, rstd1_ref, l1g_ref, wg_hbm, wu_hbm, wd_hbm,
             dgate_ref, dup_ref, dr1_ref, stat_ref, wg_s, wu_s, wd_s, sems):
        i = pl.program_id(0)

        @pl.when(i == 0)
        def _():
            _load_resident([(wg_hbm, wg_s), (wu_hbm, wu_s), (wd_hbm, wd_s)], sems)
            stat_ref[...] = jnp.zeros(stat_ref.shape, F32)

        dr2 = dr2_ref[...]
        dr2b = dr2.astype(BF16)
        for f in range(nf):
            cols = slice(MXU_COLS * f, MXU_COLS * (f + 1))
            dhh = _dot_nt(dr2b, wd_s[cols, :])
            gt = gate_ref[:, cols].astype(F32)
            ut = up_ref[:, cols].astype(F32)
            sg = jax.nn.sigmoid(gt)
            dgate_ref[:, cols] = (dhh * ut * (sg * (1.0 + gt * (1.0 - sg)))).astype(BF16)
            dup_ref[:, cols] = (dhh * (gt * sg)).astype(BF16)
        dx1 = ALPHA * dr2 + _dot(dgate_ref[...], wg_s[...]) + _dot(dup_ref[...], wu_s[...])
        n1 = n1_ref[...]
        stat_ref[0:1, :] += _colsum(dx1 * n1)
        stat_ref[1:2, :] += _colsum(dx1)
        dr1_ref[...] = _ln_bwd(dx1 * l1g_ref[...], n1, rstd1_ref[...])

    S = jax.ShapeDtypeStruct
    row = lambda w: pl.BlockSpec((tm, w), lambda i: (i, 0))
    return pl.pallas_call(
        body, name="bwd_mlp", grid=(nt,),
        in_specs=[row(D_MODEL), row(D_FF), row(D_FF), row(D_MODEL), row(1), _full((1, D_MODEL)), ANY, ANY, ANY],
        out_specs=(row(D_FF), row(D_FF), row(D_MODEL), _full((8, D_MODEL))),
        out_shape=(S((T, D_FF), BF16), S((T, D_FF), BF16), S((T, D_MODEL), F32), S((8, D_MODEL), F32)),
        scratch_shapes=[pltpu.VMEM((D_FF, D_MODEL), BF16)] * 3 + [pltpu.SemaphoreType.DMA((3,))],
        compiler_params=_params(56, 1),
    )(dr2, gate, up, n1, rstd1, l1g, wgt, wut, wd)


def _bwd_mix(dr1, proj, phi, y, win_g, wout_g, sgu_g, sgu_b, wcat, wcatt, bs_full, cw, cg, cbeta, tm, ex):
    T = dr1.shape[0]
    nt = T // tm
    halo_blocks = tm // HALO

    def body(*refs):
        ins, outs, scratch, ex_refs = _hosted(ex, refs, 15, 6)
        (dr1_ref, proj_ref, halo_ref, phi_ref, y_ref, win_ref, wout_ref, sg_ref, sb_ref, wcat_ref, wcatt_ref, bs_ref,
         cw_ref, cg_ref, cbeta_ref) = ins
        gx_ref, dproj_ref, gws_out, gbs_out, gcw_ref, vec_ref = outs
        hpad, hshift, dypad, dyshift, dhbuf, dubuf, dvnbuf, gcw_acc, gws_ref, gbs_ref = scratch
        i = pl.program_id(0)
        tile = nt - 1 - i

        @pl.when(i == 0)
        def _():
            ex.start(*ex_refs)
            gws_ref[...] = jnp.zeros(gws_ref.shape, F32)
            gbs_ref[...] = jnp.zeros(gbs_ref.shape, F32)
            gcw_ref[...] = jnp.zeros(gcw_ref.shape, F32)
            vec_ref[...] = jnp.zeros(vec_ref.shape, F32)
            gcw_acc[...] = jnp.zeros(gcw_acc.shape, F32)
            dypad[tm:tm + HALO, :] = jnp.zeros((HALO, D_CONV), F32)

        dr1 = dr1_ref[...]
        dycat = _dot_nt(dr1.astype(BF16), wout_ref[...])
        pu = proj_ref[:, 0:D_SGU]
        pv = proj_ref[:, D_SGU:2 * D_SGU]
        u = pu * phi_ref[:, 0:D_SGU]
        vhat, rstd_v = _ln_fwd(pv * phi_ref[:, D_SGU:2 * D_SGU])
        vn = vhat * sg_ref[...] + sb_ref[...]
        lo = _lo_mask()
        for c in range(tm // CHUNK):
            rows = slice(CHUNK * c, CHUNK * (c + 1))
            for p in range(4):
                lanes = slice(CHUNK * p, CHUNK * (p + 1))
                vstack = _head_pair_stack(vn[rows, lanes], lo)
                mixed = _dot(wcat_ref[p], vstack) + bs_ref[:, lanes]
                d_a = dycat[rows, lanes]
                dubuf[rows, lanes] = d_a * mixed
                dm = d_a * u[rows, lanes]
                gbs_ref[:, lanes] += dm
                dstack = _head_pair_stack(dm, lo)
                gws_ref[2 * CHUNK * p:2 * CHUNK * (p + 1), :] += _dot_nt(dstack, vn[rows, lanes].astype(BF16))
                dvnbuf[rows, lanes] = _dot(wcatt_ref[p], dstack)
        dvn = dvnbuf[...]
        vec_ref[0:1, :] += _colsum(dvn * vhat)
        vec_ref[1:2, :] += _colsum(dvn)
        dv = _ln_bwd(dvn * sg_ref[...], vhat, rstd_v)
        dproj_ref[:, 0:D_SGU] = (dubuf[...] * _gelu_grad(pu, phi_ref[:, 0:D_SGU])).astype(BF16)
        dproj_ref[:, D_SGU:2 * D_SGU] = (dv * _gelu_grad(pv, phi_ref[:, D_SGU:2 * D_SGU])).astype(BF16)
        base = 2 * D_SGU
        a = proj_ref[:, base:base + D_CONV]
        sgm = jax.nn.sigmoid(proj_ref[:, base + D_CONV:base + 2 * D_CONV])
        h_before = halo_ref[:, 0:D_CONV] * jax.nn.sigmoid(halo_ref[:, D_CONV:2 * D_CONV])
        hpad[0:HALO, :] = jnp.where(tile > 0, h_before, 0.0)
        hpad[HALO:HALO + tm, :] = a * sgm
        _shifted_copies(hpad, hshift, tm + SHIFT_ROWS)
        h_offset = lambda k: HALO - (CONV_WIDTH - 1) + k
        yhat, rstd_y = _ln_fwd(y_ref[...])
        yn = yhat * cg_ref[...] + cbeta_ref[...]
        s = jax.nn.sigmoid(yn)
        dyn = dycat[:, D_SGU:D_SGU + D_CONV] * (s * (1.0 + yn * (1.0 - s)))
        vec_ref[3:4, :] += _colsum(dyn * yhat)
        vec_ref[4:5, :] += _colsum(dyn)
        dy = _ln_bwd(dyn * cg_ref[...], yhat, rstd_y)
        vec_ref[2:3, :] += _colsum(dy)
        dypad[0:tm, :] = dy
        _shifted_copies(dypad, dyshift, tm + SHIFT_ROWS)
        _causal_conv(dypad, dyshift, cw_ref, dhbuf, tm, lambda k: (CONV_WIDTH - 1) - k)
        _conv_weight_grad(dypad, hpad, hshift, gcw_acc, tm, h_offset)
        dypad[tm:tm + HALO, :] = dypad[0:HALO, :]
        dh = dhbuf[...]
        dproj_ref[:, base:base + D_CONV] = (dh * sgm).astype(BF16)
        dproj_ref[:, base + D_CONV:base + 2 * D_CONV] = (dh * a * sgm * (1.0 - sgm)).astype(BF16)
        gx_ref[...] = ALPHA * dr1 + _dot(dproj_ref[...], win_ref[...])

        @pl.when(i == _forward_step(nt))
        def _():
            ex.forward(*ex_refs)

        @pl.when(i == nt - 1)
        def _():
            gcw_ref[...] = gcw_acc[...].sum(axis=1)
            gws_out[...] = gws_ref[...].astype(BF16)
            gbs_out[...] = lax.dot_general(_head_selector(), gbs_ref[...], (((1,), (1,)), ((), ())),
                                           preferred_element_type=F32, precision=lax.Precision.HIGHEST)
            ex.wait(*ex_refs)

    S = jax.ShapeDtypeStruct
    row = lambda w: pl.BlockSpec((tm, w), lambda i: (nt - 1 - i, 0))
    halo = pl.BlockSpec((HALO, D_MODEL), lambda i: (jnp.maximum((nt - 1 - i) * halo_blocks - 1, 0), 1))
    res = pl.pallas_call(
        body, name="bwd_mix", grid=(nt,),
        in_specs=[row(D_MODEL), row(2 * D_MODEL), halo, row(2 * D_SGU), row(D_CONV), _full(win_g.shape),
                  _full(wout_g.shape), _full(sgu_g.shape), _full(sgu_b.shape), _full(wcat.shape), _full(wcatt.shape),
                  _full(bs_full.shape), _full(cw.shape), _full(cg.shape), _full(cbeta.shape)]
        + [ANY] * ex.n,
        out_specs=(row(D_MODEL), row(2 * D_MODEL), _full((N_HEADS * CHUNK, CHUNK)), _full((N_HEADS, CHUNK)),
                   _full((CONV_ROWS, D_CONV)), _full((8, D_CONV))) + (ANY,) * ex.n,
        out_shape=(S((T, D_MODEL), F32), S((T, 2 * D_MODEL), BF16), S((N_HEADS * CHUNK, CHUNK), BF16),
                   S((N_HEADS, CHUNK), F32), S((CONV_ROWS, D_CONV), F32), S((8, D_CONV), F32), *ex.out_shape),
        scratch_shapes=[pltpu.VMEM((tm + HALO, D_CONV), F32), pltpu.VMEM((SUBLANES - 1, tm + SHIFT_ROWS, D_CONV), F32),
                        pltpu.VMEM((tm + HALO, D_CONV), F32), pltpu.VMEM((SUBLANES - 1, tm + SHIFT_ROWS, D_CONV), F32),
                        pltpu.VMEM((tm, D_CONV), F32), pltpu.VMEM((tm, D_SGU), F32),
                        pltpu.VMEM((tm, D_SGU), F32), pltpu.VMEM((CONV_ROWS, 8, D_CONV), F32),
                        pltpu.VMEM((N_HEADS * CHUNK, CHUNK), F32), pltpu.VMEM((CHUNK, D_SGU), F32)] + ex.scratch,
        compiler_params=_params(56, 1),
    )(dr1, proj, proj, phi, y, win_g, wout_g, sgu_g, sgu_b, wcat, wcatt, bs_full, cw, cg, cbeta, *ex.arrays)
    return res[:6], res[6:]


def _wgrad(name, a, b, blocks, tk, ex=None):
    T, M = a.shape
    N = b.shape[1]
    nk = T // tk
    out_shape = (blocks, M // blocks, N)
    ex = ex or _Exchange([], [])

    def body(*refs):
        (a_ref, b_ref), (o_ref,), (acc,), ex_refs = _hosted(ex, refs, 2, 1)
        i = pl.program_id(0)

        @pl.when(i == 0)
        def _():
            ex.start(*ex_refs)
            acc[...] = jnp.zeros(acc.shape, F32)

        acc[...] += _dot_tn(a_ref[...].astype(BF16), b_ref[...].astype(BF16))

        @pl.when(i == _forward_step(nk))
        def _():
            ex.forward(*ex_refs)

        @pl.when(i == nk - 1)
        def _():
            o_ref[...] = acc[...].astype(BF16)
            ex.wait(*ex_refs)

    res = pl.pallas_call(
        body, name=name, grid=(nk,),
        in_specs=[pl.BlockSpec((tk, M), lambda i: (i, 0)), pl.BlockSpec((tk, N), lambda i: (i, 0))] + [ANY] * ex.n,
        out_specs=(_full((M, N)),) + (ANY,) * ex.n,
        out_shape=(jax.ShapeDtypeStruct((M, N), BF16), *ex.out_shape),
        scratch_shapes=[pltpu.VMEM((M, N), F32)] + ex.scratch,
        compiler_params=_params(56, 1),
    )(a, b, *ex.arrays)
    g = res[0].reshape(out_shape)
    return (g, res[1:]) if ex.n else g


def _adamw(w, g, m, v):
    m2 = ADAM_B1 * m + (1.0 - ADAM_B1) * g
    v2 = ADAM_B2 * v + (1.0 - ADAM_B2) * (g * g)
    m_hat = m2 / (1.0 - ADAM_B1 ** ADAM_STEP)
    v_hat = v2 / (1.0 - ADAM_B2 ** ADAM_STEP)
    delta = -ADAM_LR * (m_hat / (jnp.sqrt(v_hat) + ADAM_EPS) + ADAM_WD * w)
    return delta, m2, v2


def _sum_partials(r_ref):
    g = r_ref[0].astype(F32)
    for s in range(1, N_DEV):
        g = g + r_ref[s].astype(F32)
    return g


def _adamw_shard(name, parts, w, m, v, transposed):
    def body(r_ref, w_ref, m_ref, v_ref, g_o, d_o, m_o, v_o):
        g = _sum_partials(r_ref)
        if transposed:
            g = g.T
        delta, m2, v2 = _adamw(w_ref[...], g, m_ref[...], v_ref[...])
        g_o[...] = g
        d_o[...] = delta
        m_o[...] = m2
        v_o[...] = v2

    return pl.pallas_call(
        body, name=name, out_shape=(jax.ShapeDtypeStruct(w.shape, F32),) * 4, compiler_params=_params(40),
    )(parts, w, m, v)


def _finish_small(gws8, gbs8, gcw8, vmix8, vmlp8, vout8, small):
    names = ["sgu_ln_g", "sgu_ln_b", "w_s", "b_s", "conv_b", "conv_ln_g", "conv_ln_b", "ln1_g", "ln1_b", "ln2_g", "ln2_b"]
    flat = []
    for n in names:
        flat += list(small[n])

    def body(*refs):
        gws_ref, gbs_ref, gcw_ref, vmix_ref, vmlp_ref, vout_ref = refs[:6]
        wmv = refs[6:6 + 3 * len(names)]
        outs = refs[6 + 3 * len(names):]
        loss_o, gcw_o = outs[0], outs[1]
        outs = outs[2:]
        gws = _sum_partials(gws_ref)
        gbs = _sum_partials(gbs_ref)
        vmix = _sum_partials(vmix_ref)
        vmlp = _sum_partials(vmlp_ref)
        vout = _sum_partials(vout_ref)
        gcw_o[...] = _sum_partials(gcw_ref)
        loss_o[...] = (0.5 / D_MODEL) * jnp.sum(vout[0:1, :], axis=1, keepdims=True)
        rows = lax.broadcasted_iota(jnp.int32, (N_HEADS * CHUNK, CHUNK), 0)
        cols = lax.broadcasted_iota(jnp.int32, (N_HEADS * CHUNK, CHUNK), 1)
        gws = jnp.where((rows & (CHUNK - 1)) >= cols, gws, 0.0)
        grads = {
            "sgu_ln_g": vmix[0:1, :], "sgu_ln_b": vmix[1:2, :], "w_s": gws, "b_s": gbs,
            "conv_b": vmix[2:3, :], "conv_ln_g": vmix[3:4, :], "conv_ln_b": vmix[4:5, :],
            "ln1_g": vmlp[0:1, :], "ln1_b": vmlp[1:2, :], "ln2_g": vout[1:2, :], "ln2_b": vout[2:3, :],
        }
        for k, n in enumerate(names):
            w_ref, m_ref, v_ref = wmv[3 * k:3 * k + 3]
            g = grads[n]
            delta, m2, v2 = _adamw(w_ref[...], g, m_ref[...], v_ref[...])
            outs[4 * k][...] = g
            outs[4 * k + 1][...] = delta
            outs[4 * k + 2][...] = m2
            outs[4 * k + 3][...] = v2

    S = jax.ShapeDtypeStruct
    out_shape = [S((1, 1), F32), S((CONV_ROWS, D_CONV), F32)]
    for n in names:
        out_shape += [S(small[n][0].shape, F32)] * 4
    res = pl.pallas_call(
        body, name="finish_small", out_shape=tuple(out_shape),
        compiler_params=_params(40),
    )(gws8, gbs8, gcw8, vmix8, vmlp8, vout8, *flat)
    upd = {n: res[2 + 4 * k:6 + 4 * k] for k, n in enumerate(names)}
    return res[0], res[1], upd


def _adamw_plain(name, g, w, m, v):
    def body(g_ref, w_ref, m_ref, v_ref, d_o, m_o, v_o):
        delta, m2, v2 = _adamw(w_ref[...], g_ref[...], m_ref[...], v_ref[...])
        d_o[...] = delta
        m_o[...] = m2
        v_o[...] = v2

    return pl.pallas_call(
        body, name=name, out_shape=(jax.ShapeDtypeStruct(w.shape, F32),) * 3,
    )(g, w, m, v)


TOKEN_TILE_MIX = 256
TOKEN_TILE_FWD_MLP = 512
TOKEN_TILE_BWD_MLP = 512
TOKEN_TILE_WGRAD = 1024


def kernel(x, w_in, sgu_ln_g, sgu_ln_b, w_s, b_s, conv_w, conv_b, conv_ln_g, conv_ln_b, w_out, ln1_g, ln1_b, w_gate, w_up, w_down, ln2_g, ln2_b, loss_target, m_w_in, m_sgu_ln_g, m_sgu_ln_b, m_w_s, m_b_s, m_conv_w, m_conv_b, m_conv_ln_g, m_conv_ln_b, m_w_out, m_ln1_g, m_ln1_b, m_w_gate, m_w_up, m_w_down, m_ln2_g, m_ln2_b, v_w_in, v_sgu_ln_g, v_sgu_ln_b, v_w_s, v_b_s, v_conv_w, v_conv_b, v_conv_ln_g, v_conv_ln_b, v_w_out, v_ln1_g, v_ln1_b, v_w_gate, v_w_up, v_w_down, v_ln2_g, v_ln2_b):
    xs = x[0]
    tgt = loss_target[0]

    (win_b, wout_b, wgt_b, wut_b, wd_b, cw_b, wcat, wcatt, bs_full) = _prep_weights(
        w_in[0], w_out[0], w_gate[0].T, w_up[0].T, w_down[0], conv_w[0], w_s[0], b_s[0])
    win_g, wout_g, cw_g = _exchange("gather_mix_weights", [], [win_b, wout_b, cw_b])
    win_g = win_g.reshape(2 * D_MODEL, D_MODEL)
    wout_g = wout_g.reshape(D_MODEL, D_MODEL)
    cw = jnp.transpose(cw_g[:, :, :D_CONV // N_DEV], (1, 0, 2)).reshape(CONV_ROWS, D_CONV)

    (proj, ycat, n1, rstd1, phi, y_conv), (wgt_g, wut_g, wd_g) = _fwd_mix(
        xs, win_g, wout_g, sgu_ln_g, sgu_ln_b, wcat, bs_full, cw, conv_b, conv_ln_g, conv_ln_b, TOKEN_TILE_MIX,
        _Exchange([], [wgt_b, wut_b, wd_b]))
    wgt_g = wgt_g.reshape(D_FF, D_MODEL)
    wut_g = wut_g.reshape(D_FF, D_MODEL)
    wd_g = wd_g.reshape(D_FF, D_MODEL)
    gate, up, hh, x1b, dr2, vout = _fwd_mlp(n1, tgt, ln1_g, ln1_b, ln2_g, ln2_b, wgt_g, wut_g, wd_g, TOKEN_TILE_FWD_MLP)

    dgate, dup, dr1, vmlp = _bwd_mlp(dr2, gate, up, n1, rstd1, ln1_g, wgt_g, wut_g, wd_g, TOKEN_TILE_BWD_MLP)
    tk = TOKEN_TILE_WGRAD
    g_wgt = _wgrad("wgrad_gate", dgate, x1b, N_DEV, tk)
    g_wut = _wgrad("wgrad_up", dup, x1b, N_DEV, tk)
    g_wd = _wgrad("wgrad_down", hh, dr2, N_DEV, tk)
    (gx, dproj, gws, gbs, gcw, vmix), (r_wgt, r_wut, r_wd) = _bwd_mix(
        dr1, proj, phi, y_conv, win_g, wout_g, sgu_ln_g, sgu_ln_b, wcat, wcatt, bs_full, cw, conv_ln_g, conv_ln_b,
        TOKEN_TILE_MIX, _Exchange([g_wgt, g_wut, g_wd], []))
    g_win, (gws8, gbs8, gcw8, vmix8, vmlp8, vout8) = _wgrad(
        "wgrad_in", dproj, xs, N_DEV, tk,
        _Exchange([], [gws, gbs, gcw, vmix, vmlp, vout]))
    g_wout, (r_win,) = _wgrad("wgrad_out", ycat, dr1, N_DEV, tk, _Exchange([g_win], []))
    (r_wout,) = _exchange("exchange_grad_out", [g_wout], [])

    big = {
        "w_in": _adamw_shard("adamw_in", r_win, w_in[0], m_w_in[0], v_w_in[0], True),
        "w_out": _adamw_shard("adamw_out", r_wout, w_out[0], m_w_out[0], v_w_out[0], False),
        "w_gate": _adamw_shard("adamw_gate", r_wgt, w_gate[0].T, m_w_gate[0].T, v_w_gate[0].T, False),
        "w_up": _adamw_shard("adamw_up", r_wut, w_up[0].T, m_w_up[0].T, v_w_up[0].T, False),
        "w_down": _adamw_shard("adamw_down", r_wd, w_down[0], m_w_down[0], v_w_down[0], False),
    }
    small_in = {
        "sgu_ln_g": (sgu_ln_g, m_sgu_ln_g, v_sgu_ln_g), "sgu_ln_b": (sgu_ln_b, m_sgu_ln_b, v_sgu_ln_b),
        "w_s": tuple(a.reshape(N_HEADS * CHUNK, CHUNK) for a in (w_s, m_w_s, v_w_s)),
        "b_s": (b_s[0], m_b_s[0], v_b_s[0]),
        "conv_b": (conv_b, m_conv_b, v_conv_b), "conv_ln_g": (conv_ln_g, m_conv_ln_g, v_conv_ln_g),
        "conv_ln_b": (conv_ln_b, m_conv_ln_b, v_conv_ln_b),
        "ln1_g": (ln1_g, m_ln1_g, v_ln1_g), "ln1_b": (ln1_b, m_ln1_b, v_ln1_b),
        "ln2_g": (ln2_g, m_ln2_g, v_ln2_g), "ln2_b": (ln2_b, m_ln2_b, v_ln2_b),
    }
    loss11, gcw_full, small = _finish_small(gws8, gbs8, gcw8, vmix8, vmlp8, vout8, small_in)

    me = 4 * lax.axis_index("x") + 2 * lax.axis_index("y") + lax.axis_index("c")
    g_cw = lax.dynamic_slice(gcw_full, (0, me * (D_CONV // N_DEV)), (CONV_WIDTH, D_CONV // N_DEV))
    d_cw, m_cw, v_cw = _adamw_plain("adamw_conv_w", g_cw, conv_w[0], m_conv_w[0], v_conv_w[0])

    shapes = {"w_s": w_s.shape, "b_s": b_s.shape}
    out = {}
    for n, r in big.items():
        out[n] = tuple((a.T if n in ("w_gate", "w_up") else a)[None] for a in r)
    for n, r in small.items():
        out[n] = tuple(a.reshape(shapes[n]) for a in r) if n in shapes else tuple(r)
    out["conv_w"] = tuple(a[None] for a in (g_cw, d_cw, m_cw, v_cw))

    order = ["w_in", "sgu_ln_g", "sgu_ln_b", "w_s", "b_s", "conv_w", "conv_b", "conv_ln_g", "conv_ln_b", "w_out",
             "ln1_g", "ln1_b", "w_gate", "w_up", "w_down", "ln2_g", "ln2_b"]
    loss = loss11[0, 0]
    return (loss, gx[None], *[out[n][0] for n in order], *[out[n][1] for n in order],
            *[out[n][2] for n in order], *[out[n][3] for n in order])
```

```python
import jax
import jax.numpy as jnp
from jax import lax
from jax.experimental import pallas as pl
from jax.experimental.pallas import tpu as pltpu

F32 = jnp.float32
BF16 = jnp.bfloat16

D_MODEL = 1024
D_SGU = 512
D_CONV = 512
N_HEADS = 8
CHUNK = 128
CONV_WIDTH = 31
CONV_ROWS = 32
HALO = 32
D_FF = 2816
N_DEV = 8
FF_SHARD = D_FF // N_DEV
ALPHA = (2.0 * 1) ** 0.25
LN_EPS = 1e-5
INV_SQRT2 = 0.7071067811865476
INV_SQRT_2PI = 0.3989422804014327

ADAM_LR = 0.001
ADAM_B1 = 0.9
ADAM_B2 = 0.999
ADAM_EPS = 1e-08
ADAM_WD = 0.01
ADAM_STEP = 10

MXU_COLS = 256
SUBLANES = 8
CONV_ROW_BLOCK = 32
WGRAD_ROW_BLOCK = 32
SHIFT_ROWS = HALO - SUBLANES
MIB = 1024 * 1024

HBM = pl.BlockSpec(memory_space=pltpu.HBM)
ANY = pl.BlockSpec(memory_space=pl.ANY)
MESH = pl.DeviceIdType.MESH


def _params(vmem_mib, grid_dims=0):
    kw = dict(vmem_limit_bytes=vmem_mib * MIB)
    if grid_dims:
        kw["dimension_semantics"] = ("arbitrary",) * grid_dims
    return pltpu.CompilerParams(**kw)


def _full(shape):
    return pl.BlockSpec(shape, lambda i: (0,) * len(shape))


def _dot(a, b):
    return jnp.dot(a, b, preferred_element_type=F32)


def _dot_nt(a, b):
    return lax.dot_general(a, b, (((1,), (1,)), ((), ())), preferred_element_type=F32)


def _dot_tn(a, b):
    return lax.dot_general(a, b, (((0,), (0,)), ((), ())), preferred_element_type=F32)


def _normal_cdf(x):
    return 0.5 * (1.0 + lax.erf(x * INV_SQRT2))


def _gelu_grad(x, cdf):
    return cdf + x * jnp.exp(-0.5 * x * x) * INV_SQRT_2PI


def _ln_fwd(v):
    mu = jnp.mean(v, axis=-1, keepdims=True)
    d = v - mu
    var = jnp.mean(d * d, axis=-1, keepdims=True)
    rstd = lax.rsqrt(var + LN_EPS)
    return d * rstd, rstd


def _ln_bwd(dyhat, yhat, rstd):
    m1 = jnp.mean(dyhat, axis=-1, keepdims=True)
    m2 = jnp.mean(dyhat * yhat, axis=-1, keepdims=True)
    return rstd * (dyhat - m1 - yhat * m2)


def _colsum(v):
    return jnp.sum(v, axis=0, keepdims=True)


def _head_pair_stack(v, lo):
    return jnp.concatenate([jnp.where(lo, v, 0.0), jnp.where(lo, 0.0, v)], axis=0).astype(BF16)


def _lo_mask():
    return lax.broadcasted_iota(jnp.int32, (CHUNK, CHUNK), 1) < (CHUNK // 2)


def _head_selector():
    head = lax.broadcasted_iota(jnp.int32, (N_HEADS, D_SGU), 0)
    lane = lax.broadcasted_iota(jnp.int32, (N_HEADS, D_SGU), 1)
    width = D_SGU // N_HEADS
    return ((lane >= head * width) & (lane < (head + 1) * width)).astype(F32)


def _shifted_copies(pad_ref, sh_ref, rows):
    for r in range(1, SUBLANES):
        sh_ref[r - 1, 0:rows, :] = pad_ref[pl.ds(r, rows), :]


def _tap_groups(offset_of_tap):
    groups = {}
    for k in range(CONV_WIDTH):
        o = offset_of_tap(k)
        groups.setdefault(o % SUBLANES, []).append((k, o // SUBLANES))
    return groups


def _tap_window(pad_ref, sh_ref, r, taps, row0, rows):
    q0 = min(q for _, q in taps)
    q1 = max(q for _, q in taps)
    src = pad_ref if r == 0 else sh_ref.at[r - 1]
    win = src[pl.ds(row0 + SUBLANES * q0, SUBLANES * (q1 - q0) + rows), :]
    return win, [(k, SUBLANES * (q - q0)) for k, q in taps]


def _causal_conv(pad_ref, sh_ref, w_ref, out_ref, rows, offset_of_tap, bias=None):
    groups = _tap_groups(offset_of_tap)

    def block(b, carry):
        row0 = pl.multiple_of(b * CONV_ROW_BLOCK, CONV_ROW_BLOCK)
        if bias is None:
            acc = jnp.zeros((CONV_ROW_BLOCK, D_CONV), F32)
        else:
            acc = jnp.broadcast_to(bias, (CONV_ROW_BLOCK, D_CONV))
        for r, taps in groups.items():
            win, starts = _tap_window(pad_ref, sh_ref, r, taps, row0, CONV_ROW_BLOCK)
            for k, s in starts:
                acc = acc + w_ref[k:k + 1, :] * win[s:s + CONV_ROW_BLOCK, :]
        out_ref[pl.ds(row0, CONV_ROW_BLOCK), :] = acc
        return carry

    lax.fori_loop(0, rows // CONV_ROW_BLOCK, block, 0)


def _conv_weight_grad(dy_ref, pad_ref, sh_ref, acc_ref, rows, offset_of_tap):
    groups = _tap_groups(offset_of_tap)
    for r, taps in groups.items():

        def block(b, parts, r=r, taps=taps):
            row0 = pl.multiple_of(b * WGRAD_ROW_BLOCK, WGRAD_ROW_BLOCK)
            dyb = dy_ref[pl.ds(row0, WGRAD_ROW_BLOCK), :]
            win, starts = _tap_window(pad_ref, sh_ref, r, taps, row0, WGRAD_ROW_BLOCK)
            out = []
            for part, (_, s) in zip(parts, starts):
                pr = dyb * win[s:s + WGRAD_ROW_BLOCK, :]
                out.append(part + pr.reshape(WGRAD_ROW_BLOCK // SUBLANES, SUBLANES, D_CONV).sum(axis=0))
            return tuple(out)

        zeros = tuple(jnp.zeros((SUBLANES, D_CONV), F32) for _ in taps)
        parts = lax.fori_loop(0, rows // WGRAD_ROW_BLOCK, block, zeros)
        for part, (k, _) in zip(parts, taps):
            acc_ref[k] += part


def _prep_weights(w_in, w_out, w_gate_t, w_up_t, w_down, conv_w, w_s, b_s):
    def compute(ins, outs):
        win_ref, wout_ref, wgt_ref, wut_ref, wd_ref, cw_ref, ws_ref, bs_ref = ins
        win_o, wout_o, wgt_o, wut_o, wd_o, cw_o, wcat_o, wcatt_o, bsf_o = outs
        win_o[...] = win_ref[...].T.astype(BF16)
        wout_o[...] = wout_ref[...].astype(BF16)
        wgt_o[...] = wgt_ref[...].astype(BF16)
        wut_o[...] = wut_ref[...].astype(BF16)
        wd_o[...] = wd_ref[...].astype(BF16)
        cw_o[...] = jnp.zeros(cw_o.shape, F32)
        cw_o[0:CONV_WIDTH, 0:D_CONV // N_DEV] = cw_ref[...]
        row = lax.broadcasted_iota(jnp.int32, (CHUNK, CHUNK), 0)
        col = lax.broadcasted_iota(jnp.int32, (CHUNK, CHUNK), 1)
        causal = row >= col
        for h in range(N_HEADS):
            w = jnp.where(causal, ws_ref[h], 0.0)
            p, half = h // 2, (h % 2) * CHUNK
            wcat_o[p, :, half:half + CHUNK] = w.astype(BF16)
            wcatt_o[p, :, half:half + CHUNK] = w.T.astype(BF16)
        bsf_o[...] = lax.dot_general(bs_ref[...], _head_selector(), (((0,), (0,)), ((), ())),
                                     preferred_element_type=F32, precision=lax.Precision.HIGHEST)

    S = jax.ShapeDtypeStruct
    out_shapes = [S((256, D_MODEL), BF16), S((128, D_MODEL), BF16), S((FF_SHARD, D_MODEL), BF16),
                  S((FF_SHARD, D_MODEL), BF16), S((FF_SHARD, D_MODEL), BF16), S((CONV_ROWS, 128), F32),
                  S((4, CHUNK, 2 * CHUNK), BF16), S((4, CHUNK, 2 * CHUNK), BF16), S((CHUNK, D_SGU), F32)]
    (res,), _ = _staged_call(
        "prep_weights", [([w_in, w_out, w_gate_t, w_up_t, w_down, conv_w, w_s, b_s], out_shapes, compute)], 32)
    return res


def _mesh_position():
    x, y, c = lax.axis_index("x"), lax.axis_index("y"), lax.axis_index("c")
    return x, y, c


def _peers(x, y, c):
    out = []
    for k in range(1, N_DEV):
        px = 1 - x if (k >> 2) & 1 else x
        py = 1 - y if (k >> 1) & 1 else y
        pc = 1 - c if k & 1 else c
        out.append(((px, py, pc), 4 * px + 2 * py + pc))
    return out


class _Exchange:
    def __init__(self, scatter, gather):
        self.arrays = list(scatter) + list(gather)
        self.n_sc = len(scatter)
        self.n = len(self.arrays)
        self.out_shape = [jax.ShapeDtypeStruct(a.shape if k < self.n_sc else (N_DEV,) + a.shape, a.dtype)
                          for k, a in enumerate(self.arrays)]
        n_remote = self.n * (N_DEV - 1)
        self.scratch = [pltpu.SemaphoreType.DMA((n_remote,)), pltpu.SemaphoreType.DMA((n_remote,)),
                        pltpu.SemaphoreType.DMA((self.n,))] if self.n else []

    def _copies(self, src, dst, sems):
        send_sems, recv_sems, local_sems = sems
        x, y, c = _mesh_position()
        me = 4 * x + 2 * y + c
        locals_, first, arrivals, passed, last = [], [], [], [], []

        def remote(a, k, src_ref, slot, to):
            s = a * (N_DEV - 1) + k
            return pltpu.make_async_remote_copy(src_ref=src_ref, dst_ref=dst[a].at[slot], send_sem=send_sems.at[s],
                                                recv_sem=recv_sems.at[s], device_id=to, device_id_type=MESH)

        for a in range(self.n):
            if a < self.n_sc:
                locals_.append(pltpu.make_async_copy(src[a].at[me], dst[a].at[me], local_sems.at[a]))
                for k, (peer, pid) in enumerate(_peers(x, y, c)):
                    first.append(remote(a, k, src[a].at[pid], me, peer))
                    last.append(remote(a, k, src[a].at[pid], pid, peer))
                continue
            locals_.append(pltpu.make_async_copy(src[a], dst[a].at[me], local_sems.at[a]))
            sibling, sib_id = (x, y, 1 - c), 4 * x + 2 * y + (1 - c)
            chips = [(1 - x, y), (x, 1 - y), (1 - x, 1 - y)]
            first.append(remote(a, 0, src[a], me, sibling))
            last.append(remote(a, 0, src[a], sib_id, sibling))
            for j, (px, py) in enumerate(chips):
                same, other = 4 * px + 2 * py + c, 4 * px + 2 * py + (1 - c)
                first.append(remote(a, 1 + j, src[a], me, (px, py, c)))
                arrivals.append(remote(a, 1 + j, src[a], same, (px, py, c)))
                passed.append(remote(a, 4 + j, dst[a].at[same], same, sibling))
                last.append(remote(a, 4 + j, dst[a].at[other], other, sibling))
        return locals_, first, arrivals, passed, last

    def start(self, src, dst, sems):
        if not self.n:
            return
        locals_, first, _, _, _ = self._copies(src, dst, sems)
        for cp in locals_ + first:
            cp.start()

    def forward(self, src, dst, sems):
        if self.n == self.n_sc:
            return
        _, _, arrivals, passed, _ = self._copies(src, dst, sems)
        for arrived, cp in zip(arrivals, passed):
            arrived.wait_recv()
            cp.start()

    def wait(self, src, dst, sems):
        if not self.n:
            return
        locals_, first, _, passed, last = self._copies(src, dst, sems)
        for cp in last:
            cp.wait_recv()
        for cp in first + passed:
            cp.wait_send()
        for cp in locals_:
            cp.wait()


def _exchange(name, scatter, gather):
    ex = _Exchange(scatter, gather)
    n = ex.n

    def body(*refs):
        src, dst, sems = refs[:n], refs[n:2 * n], refs[2 * n:]
        ex.start(src, dst, sems)
        ex.forward(src, dst, sems)
        ex.wait(src, dst, sems)

    return pl.pallas_call(
        body, name=name, out_shape=tuple(ex.out_shape), in_specs=[ANY] * n, out_specs=(ANY,) * n,
        scratch_shapes=ex.scratch,
    )(*ex.arrays)


def _forward_step(n_steps):
    return (11 * n_steps) // 16


def _hosted(ex, refs, n_in, n_out):
    ins, ex_src = refs[:n_in], refs[n_in:n_in + ex.n]
    rest = refs[n_in + ex.n:]
    outs, ex_dst = rest[:n_out], rest[n_out:n_out + ex.n]
    rest = rest[n_out + ex.n:]
    n_own = len(rest) - len(ex.scratch)
    return ins, outs, rest[:n_own], (ex_src, ex_dst, rest[n_own:])


def _fwd_mix(x, win_g, wout_g, sgu_g, sgu_b, wcat, bs_full, cw, cb, cg, cbeta, tm, ex):
    T = x.shape[0]
    nt = T // tm

    def body(*refs):
        ins, outs, scratch, ex_refs = _hosted(ex, refs, 11, 6)
        x_ref, win_ref, wout_ref, sg_ref, sb_ref, wcat_ref, bs_ref, cw_ref, cb_ref, cg_ref, cbeta_ref = ins
        proj_ref, ycat_ref, n1_ref, rstd1_ref, phi_ref, y_ref = outs
        hpad, hshift = scratch
        i = pl.program_id(0)

        @pl.when(i == 0)
        def _():
            ex.start(*ex_refs)

        xf = x_ref[...]
        xb = xf.astype(BF16)
        proj_ref[...] = _dot_nt(xb, win_ref[...])
        phi_ref[...] = _normal_cdf(proj_ref[:, 0:2 * D_SGU])
        u = proj_ref[:, 0:D_SGU] * phi_ref[:, 0:D_SGU]
        v = proj_ref[:, D_SGU:2 * D_SGU] * phi_ref[:, D_SGU:2 * D_SGU]
        vhat, _ = _ln_fwd(v)
        vn = vhat * sg_ref[...] + sb_ref[...]
        lo = _lo_mask()
        for c in range(tm // CHUNK):
            rows = slice(CHUNK * c, CHUNK * (c + 1))
            for p in range(4):
                lanes = slice(CHUNK * p, CHUNK * (p + 1))
                mixed = _dot(wcat_ref[p], _head_pair_stack(vn[rows, lanes], lo)) + bs_ref[:, lanes]
                ycat_ref[rows, lanes] = (u[rows, lanes] * mixed).astype(BF16)
        base = 2 * D_SGU
        a = proj_ref[:, base:base + D_CONV]
        g = proj_ref[:, base + D_CONV:base + 2 * D_CONV]

        @pl.when(i == 0)
        def _():
            hpad[0:HALO, :] = jnp.zeros((HALO, D_CONV), F32)

        hpad[HALO:HALO + tm, :] = a * jax.nn.sigmoid(g)
        _shifted_copies(hpad, hshift, tm + SHIFT_ROWS)
        _causal_conv(hpad, hshift, cw_ref, y_ref, tm, lambda k: HALO - (CONV_WIDTH - 1) + k, bias=cb_ref[...])
        hpad[0:HALO, :] = hpad[tm:tm + HALO, :]
        yhat, _ = _ln_fwd(y_ref[...])
        yn = yhat * cg_ref[...] + cbeta_ref[...]
        ycat_ref[:, D_SGU:D_SGU + D_CONV] = (yn * jax.nn.sigmoid(yn)).astype(BF16)
        r1 = ALPHA * xf + _dot(ycat_ref[...], wout_ref[...])
        n1, rstd1 = _ln_fwd(r1)
        n1_ref[...] = n1
        rstd1_ref[...] = rstd1

        @pl.when(i == _forward_step(nt))
        def _():
            ex.forward(*ex_refs)

        @pl.when(i == nt - 1)
        def _():
            ex.wait(*ex_refs)

    S = jax.ShapeDtypeStruct
    row = lambda w: pl.BlockSpec((tm, w), lambda i: (i, 0))
    res = pl.pallas_call(
        body, name="fwd_mix", grid=(nt,),
        in_specs=[row(D_MODEL), _full(win_g.shape), _full(wout_g.shape), _full(sgu_g.shape), _full(sgu_b.shape),
                  _full(wcat.shape), _full(bs_full.shape), _full(cw.shape), _full(cb.shape), _full(cg.shape),
                  _full(cbeta.shape)] + [ANY] * ex.n,
        out_specs=(row(2 * D_MODEL), row(D_MODEL), row(D_MODEL), row(1), row(2 * D_SGU), row(D_CONV)) + (ANY,) * ex.n,
        out_shape=(S((T, 2 * D_MODEL), F32), S((T, D_MODEL), BF16), S((T, D_MODEL), F32), S((T, 1), F32),
                   S((T, 2 * D_SGU), F32), S((T, D_CONV), F32), *ex.out_shape),
        scratch_shapes=[pltpu.VMEM((tm + HALO, D_CONV), F32),
                        pltpu.VMEM((SUBLANES - 1, tm + SHIFT_ROWS, D_CONV), F32)] + ex.scratch,
        compiler_params=_params(56, 1),
    )(x, win_g, wout_g, sgu_g, sgu_b, wcat, bs_full, cw, cb, cg, cbeta, *ex.arrays)
    return res[:6], res[6:]


def _load_resident(pairs, sems):
    cps = [pltpu.make_async_copy(s, d, sems.at[k]) for k, (s, d) in enumerate(pairs)]
    for cp in cps:
        cp.start()
    for cp in cps:
        cp.wait()


def _fwd_mlp(n1, tgt, l1g, l1b, l2g, l2b, wgt, wut, wd, tm):
    T = n1.shape[0]
    nt = T // tm
    nf = D_FF // MXU_COLS

    def body(n1_ref, tgt_ref, l1g_ref, l1b_ref, l2g_ref, l2b_ref, wg_hbm, wu_hbm, wd_hbm,
             gate_ref, up_ref, hh_ref, x1b_ref, dr2_ref, stat_ref, wg_s, wu_s, wd_s, sems):
        i = pl.program_id(0)

        @pl.when(i == 0)
        def _():
            _load_resident([(wg_hbm, wg_s), (wu_hbm, wu_s), (wd_hbm, wd_s)], sems)
            stat_ref[...] = jnp.zeros(stat_ref.shape, F32)

        x1 = n1_ref[...] * l1g_ref[...] + l1b_ref[...]
        x1b = x1.astype(BF16)
        x1b_ref[...] = x1b
        for f in range(nf):
            cols = slice(MXU_COLS * f, MXU_COLS * (f + 1))
            gt = _dot_nt(x1b, wg_s[cols, :])
            ut = _dot_nt(x1b, wu_s[cols, :])
            gate_ref[:, cols] = gt.astype(BF16)
            up_ref[:, cols] = ut.astype(BF16)
            hh_ref[:, cols] = (gt * jax.nn.sigmoid(gt) * ut).astype(BF16)
        r2 = ALPHA * x1 + _dot(hh_ref[...], wd_s[...])
        n2, rstd2 = _ln_fwd(r2)
        x2 = n2 * l2g_ref[...] + l2b_ref[...]
        diff = x2 - tgt_ref[...]
        dx2 = diff * (1.0 / D_MODEL)
        stat_ref[0:1, :] += _colsum(diff * diff)
        stat_ref[1:2, :] += _colsum(dx2 * n2)
        stat_ref[2:3, :] += _colsum(dx2)
        dr2_ref[...] = _ln_bwd(dx2 * l2g_ref[...], n2, rstd2)

    S = jax.ShapeDtypeStruct
    row = lambda w: pl.BlockSpec((tm, w), lambda i: (i, 0))
    vec = _full((1, D_MODEL))
    return pl.pallas_call(
        body, name="fwd_mlp", grid=(nt,),
        in_specs=[row(D_MODEL), row(D_MODEL), vec, vec, vec, vec, ANY, ANY, ANY],
        out_specs=(row(D_FF), row(D_FF), row(D_FF), row(D_MODEL), row(D_MODEL), _full((8, D_MODEL))),
        out_shape=(S((T, D_FF), BF16), S((T, D_FF), BF16), S((T, D_FF), BF16), S((T, D_MODEL), BF16),
                   S((T, D_MODEL), F32), S((8, D_MODEL), F32)),
        scratch_shapes=[pltpu.VMEM((D_FF, D_MODEL), BF16)] * 3 + [pltpu.SemaphoreType.DMA((3,))],
        compiler_params=_params(56, 1),
    )(n1, tgt, l1g, l1b, l2g, l2b, wgt, wut, wd)


def _bwd_mlp(dr2, gate, up, n1, rstd1, l1g, wgt, wut, wd, tm):
    T = n1.shape[0]
    nt = T // tm
    nf = D_FF // MXU_COLS

    def body(dr2_ref, gate_ref, up_ref, n1_ref, rstd1_ref, l1g_ref, wg_hbm, wu_hbm, wd_hbm,
             dgate_ref, dup_ref, dr1_ref, stat_ref, wg_s, wu_s, wd_s, sems):
        i = pl.program_id(0)

        @pl.when(i == 0)
        def _():
            _load_resident([(wg_hbm, wg_s), (wu_hbm, wu_s), (wd_hbm, wd_s)], sems)
            stat_ref[...] = jnp.zeros(stat_ref.shape, F32)

        dr2 = dr2_ref[...]
        dr2b = dr2.astype(BF16)
        for f in range(nf):
            cols = slice(MXU_COLS * f, MXU_COLS * (f + 1))
            dhh = _dot_nt(dr2b, wd_s[cols, :])
            gt = gate_ref[:, cols].astype(F32)
            ut = up_ref[:, cols].astype(F32)
            sg = jax.nn.sigmoid(gt)
            dgate_ref[:, cols] = (dhh * ut * (sg * (1.0 + gt * (1.0 - sg)))).astype(BF16)
            dup_ref[:, cols] = (dhh * (gt * sg)).astype(BF16)
        dx1 = ALPHA * dr2 + _dot(dgate_ref[...], wg_s[...]) + _dot(dup_ref[...], wu_s[...])
        n1 = n1_ref[...]
        stat_ref[0:1, :] += _colsum(dx1 * n1)
        stat_ref[1:2, :] += _colsum(dx1)
        dr1_ref[...] = _ln_bwd(dx1 * l1g_ref[...], n1, rstd1_ref[...])

    S = jax.ShapeDtypeStruct
    row = lambda w: pl.BlockSpec((tm, w), lambda i: (i, 0))
    return pl.pallas_call(
        body, name="bwd_mlp", grid=(nt,),
        in_specs=[row(D_MODEL), row(D_FF), row(D_FF), row(D_MODEL), row(1), _full((1, D_MODEL)), ANY, ANY, ANY],
        out_specs=(row(D_FF), row(D_FF), row(D_MODEL), _full((8, D_MODEL))),
        out_shape=(S((T, D_FF), BF16), S((T, D_FF), BF16), S((T, D_MODEL), F32), S((8, D_MODEL), F32)),
        scratch_shapes=[pltpu.VMEM((D_FF, D_MODEL), BF16)] * 3 + [pltpu.SemaphoreType.DMA((3,))],
        compiler_params=_params(56, 1),
    )(dr2, gate, up, n1, rstd1, l1g, wgt, wut, wd)


def _bwd_mix(dr1, proj, phi, y, win_g, wout_g, sgu_g, sgu_b, wcat, wcatt, bs_full, cw, cg, cbeta, tm, ex):
    T = dr1.shape[0]
    nt = T // tm
    halo_blocks = tm // HALO

    def body(*refs):
        ins, outs, scratch, ex_refs = _hosted(ex, refs, 15, 6)
        (dr1_ref, proj_ref, halo_ref, phi_ref, y_ref, win_ref, wout_ref, sg_ref, sb_ref, wcat_ref, wcatt_ref, bs_ref,
         cw_ref, cg_ref, cbeta_ref) = ins
        gx_ref, dproj_ref, gws_out, gbs_out, gcw_ref, vec_ref = outs
        hpad, hshift, dypad, dyshift, dhbuf, dubuf, dvnbuf, gcw_acc, gws_ref, gbs_ref = scratch
        i = pl.program_id(0)
        tile = nt - 1 - i

        @pl.when(i == 0)
        def _():
            ex.start(*ex_refs)
            gws_ref[...] = jnp.zeros(gws_ref.shape, F32)
            gbs_ref[...] = jnp.zeros(gbs_ref.shape, F32)
            gcw_ref[...] = jnp.zeros(gcw_ref.shape, F32)
            vec_ref[...] = jnp.zeros(vec_ref.shape, F32)
            gcw_acc[...] = jnp.zeros(gcw_acc.shape, F32)
            dypad[tm:tm + HALO, :] = jnp.zeros((HALO, D_CONV), F32)

        dr1 = dr1_ref[...]
        dycat = _dot_nt(dr1.astype(BF16), wout_ref[...])
        pu = proj_ref[:, 0:D_SGU]
        pv = proj_ref[:, D_SGU:2 * D_SGU]
        u = pu * phi_ref[:, 0:D_SGU]
        vhat, rstd_v = _ln_fwd(pv * phi_ref[:, D_SGU:2 * D_SGU])
        vn = vhat * sg_ref[...] + sb_ref[...]
        lo = _lo_mask()
        for c in range(tm // CHUNK):
            rows = slice(CHUNK * c, CHUNK * (c + 1))
            for p in range(4):
                lanes = slice(CHUNK * p, CHUNK * (p + 1))
                vstack = _head_pair_stack(vn[rows, lanes], lo)
                mixed = _dot(wcat_ref[p], vstack) + bs_ref[:, lanes]
                d_a = dycat[rows, lanes]
                dubuf[rows, lanes] = d_a * mixed
                dm = d_a * u[rows, lanes]
                gbs_ref[:, lanes] += dm
                dstack = _head_pair_stack(dm, lo)
                gws_ref[2 * CHUNK * p:2 * CHUNK * (p + 1), :] += _dot_nt(dstack, vn[rows, lanes].astype(BF16))
                dvnbuf[rows, lanes] = _dot(wcatt_ref[p], dstack)
        dvn = dvnbuf[...]
        vec_ref[0:1, :] += _colsum(dvn * vhat)
        vec_ref[1:2, :] += _colsum(dvn)
        dv = _ln_bwd(dvn * sg_ref[...], vhat, rstd_v)
        dproj_ref[:, 0:D_SGU] = (dubuf[...] * _gelu_grad(pu, phi_ref[:, 0:D_SGU])).astype(BF16)
        dproj_ref[:, D_SGU:2 * D_SGU] = (dv * _gelu_grad(pv, phi_ref[:, D_SGU:2 * D_SGU])).astype(BF16)
        base = 2 * D_SGU
        a = proj_ref[:, base:base + D_CONV]
        sgm = jax.nn.sigmoid(proj_ref[:, base + D_CONV:base + 2 * D_CONV])
        h_before = halo_ref[:, 0:D_CONV] * jax.nn.sigmoid(halo_ref[:, D_CONV:2 * D_CONV])
        hpad[0:HALO, :] = jnp.where(tile > 0, h_before, 0.0)
        hpad[HALO:HALO + tm, :] = a * sgm
        _shifted_copies(hpad, hshift, tm + SHIFT_ROWS)
        h_offset = lambda k: HALO - (CONV_WIDTH - 1) + k
        yhat, rstd_y = _ln_fwd(y_ref[...])
        yn = yhat * cg_ref[...] + cbeta_ref[...]
        s = jax.nn.sigmoid(yn)
        dyn = dycat[:, D_SGU:D_SGU + D_CONV] * (s * (1.0 + yn * (1.0 - s)))
        vec_ref[3:4, :] += _colsum(dyn * yhat)
        vec_ref[4:5, :] += _colsum(dyn)
        dy = _ln_bwd(dyn * cg_ref[...], yhat, rstd_y)
        vec_ref[2:3, :] += _colsum(dy)
        dypad[0:tm, :] = dy
        _shifted_copies(dypad, dyshift, tm + SHIFT_ROWS)
        _causal_conv(dypad, dyshift, cw_ref, dhbuf, tm, lambda k: (CONV_WIDTH - 1) - k)
        _conv_weight_grad(dypad, hpad, hshift, gcw_acc, tm, h_offset)
        dypad[tm:tm + HALO, :] = dypad[0:HALO, :]
        dh = dhbuf[...]
        dproj_ref[:, base:base + D_CONV] = (dh * sgm).astype(BF16)
        dproj_ref[:, base + D_CONV:base + 2 * D_CONV] = (dh * a * sgm * (1.0 - sgm)).astype(BF16)
        gx_ref[...] = ALPHA * dr1 + _dot(dproj_ref[...], win_ref[...])

        @pl.when(i == _forward_step(nt))
        def _():
            ex.forward(*ex_refs)

        @pl.when(i == nt - 1)
        def _():
            gcw_ref[...] = gcw_acc[...].sum(axis=1)
            gws_out[...] = gws_ref[...].astype(BF16)
            gbs_out[...] = lax.dot_general(_head_selector(), gbs_ref[...], (((1,), (1,)), ((), ())),
                                           preferred_element_type=F32, precision=lax.Precision.HIGHEST)
            ex.wait(*ex_refs)

    S = jax.ShapeDtypeStruct
    row = lambda w: pl.BlockSpec((tm, w), lambda i: (nt - 1 - i, 0))
    halo = pl.BlockSpec((HALO, D_MODEL), lambda i: (jnp.maximum((nt - 1 - i) * halo_blocks - 1, 0), 1))
    res = pl.pallas_call(
        body, name="bwd_mix", grid=(nt,),
        in_specs=[row(D_MODEL), row(2 * D_MODEL), halo, row(2 * D_SGU), row(D_CONV), _full(win_g.shape),
                  _full(wout_g.shape), _full(sgu_g.shape), _full(sgu_b.shape), _full(wcat.shape), _full(wcatt.shape),
                  _full(bs_full.shape), _full(cw.shape), _full(cg.shape), _full(cbeta.shape)]
        + [ANY] * ex.n,
        out_specs=(row(D_MODEL), row(2 * D_MODEL), _full((N_HEADS * CHUNK, CHUNK)), _full((N_HEADS, CHUNK)),
                   _full((CONV_ROWS, D_CONV)), _full((8, D_CONV))) + (ANY,) * ex.n,
        out_shape=(S((T, D_MODEL), F32), S((T, 2 * D_MODEL), BF16), S((N_HEADS * CHUNK, CHUNK), BF16),
                   S((N_HEADS, CHUNK), F32), S((CONV_ROWS, D_CONV), F32), S((8, D_CONV), F32), *ex.out_shape),
        scratch_shapes=[pltpu.VMEM((tm + HALO, D_CONV), F32), pltpu.VMEM((SUBLANES - 1, tm + SHIFT_ROWS, D_CONV), F32),
                        pltpu.VMEM((tm + HALO, D_CONV), F32), pltpu.VMEM((SUBLANES - 1, tm + SHIFT_ROWS, D_CONV), F32),
                        pltpu.VMEM((tm, D_CONV), F32), pltpu.VMEM((tm, D_SGU), F32),
                        pltpu.VMEM((tm, D_SGU), F32), pltpu.VMEM((CONV_ROWS, 8, D_CONV), F32),
                        pltpu.VMEM((N_HEADS * CHUNK, CHUNK), F32), pltpu.VMEM((CHUNK, D_SGU), F32)] + ex.scratch,
        compiler_params=_params(56, 1),
    )(dr1, proj, proj, phi, y, win_g, wout_g, sgu_g, sgu_b, wcat, wcatt, bs_full, cw, cg, cbeta, *ex.arrays)
    return res[:6], res[6:]


def _wgrad(name, a, b, blocks, tk, ex=None):
    T, M = a.shape
    N = b.shape[1]
    nk = T // tk
    out_shape = (blocks, M // blocks, N)
    ex = ex or _Exchange([], [])

    def body(*refs):
        (a_ref, b_ref), (o_ref,), (acc,), ex_refs = _hosted(ex, refs, 2, 1)
        i = pl.program_id(0)

        @pl.when(i == 0)
        def _():
            ex.start(*ex_refs)
            acc[...] = jnp.zeros(acc.shape, F32)

        acc[...] += _dot_tn(a_ref[...].astype(BF16), b_ref[...].astype(BF16))

        @pl.when(i == _forward_step(nk))
        def _():
            ex.forward(*ex_refs)

        @pl.when(i == nk - 1)
        def _():
            o_ref[...] = acc[...].astype(BF16)
            ex.wait(*ex_refs)

    res = pl.pallas_call(
        body, name=name, grid=(nk,),
        in_specs=[pl.BlockSpec((tk, M), lambda i: (i, 0)), pl.BlockSpec((tk, N), lambda i: (i, 0))] + [ANY] * ex.n,
        out_specs=(_full((M, N)),) + (ANY,) * ex.n,
        out_shape=(jax.ShapeDtypeStruct((M, N), BF16), *ex.out_shape),
        scratch_shapes=[pltpu.VMEM((M, N), F32)] + ex.scratch,
        compiler_params=_params(56, 1),
    )(a, b, *ex.arrays)
    g = res[0].reshape(out_shape)
    return (g, res[1:]) if ex.n else g


def _adamw(w, g, m, v):
    m2 = ADAM_B1 * m + (1.0 - ADAM_B1) * g
    v2 = ADAM_B2 * v + (1.0 - ADAM_B2) * (g * g)
    m_hat = m2 / (1.0 - ADAM_B1 ** ADAM_STEP)
    v_hat = v2 / (1.0 - ADAM_B2 ** ADAM_STEP)
    delta = -ADAM_LR * (m_hat / (jnp.sqrt(v_hat) + ADAM_EPS) + ADAM_WD * w)
    return delta, m2, v2


def _sum_partials(r_ref):
    g = r_ref[0].astype(F32)
    for s in range(1, N_DEV):
        g = g + r_ref[s].astype(F32)
    return g


def _staged_call(name, groups, vmem_mib, ex=None):
    ex = ex or _Exchange([], [])
    inputs = [a for ins, _, _ in groups for a in ins]
    out_shapes = [s for _, outs, _ in groups for s in outs]
    n_in, n_out = len(inputs), len(out_shapes)

    def body(*refs):
        ins, outs, scratch, ex_refs = _hosted(ex, refs, n_in, n_out)
        in_bufs, out_bufs, sems = scratch[:n_in], scratch[n_in:n_in + n_out], scratch[n_in + n_out]
        ex.start(*ex_refs)
        loads = [pltpu.make_async_copy(ins[k], in_bufs[k], sems.at[k]) for k in range(n_in)]
        stores = [pltpu.make_async_copy(out_bufs[k], outs[k], sems.at[n_in + k]) for k in range(n_out)]
        for cp in loads:
            cp.start()
        i0 = o0 = 0
        for g_ins, g_outs, compute in groups:
            i1, o1 = i0 + len(g_ins), o0 + len(g_outs)
            for cp in loads[i0:i1]:
                cp.wait()
            compute(in_bufs[i0:i1], out_bufs[o0:o1])
            for cp in stores[o0:o1]:
                cp.start()
            i0, o0 = i1, o1
        for cp in stores:
            cp.wait()
        ex.forward(*ex_refs)
        ex.wait(*ex_refs)

    scratch = ([pltpu.VMEM(a.shape, a.dtype) for a in inputs] + [pltpu.VMEM(s.shape, s.dtype) for s in out_shapes]
               + [pltpu.SemaphoreType.DMA((n_in + n_out,))] + ex.scratch)
    res = pl.pallas_call(
        body, name=name, out_shape=(*out_shapes, *ex.out_shape),
        in_specs=[HBM] * n_in + [ANY] * ex.n, out_specs=(HBM,) * n_out + (ANY,) * ex.n,
        scratch_shapes=scratch, compiler_params=_params(vmem_mib),
    )(*inputs, *ex.arrays)
    per_group, o0 = [], 0
    for _, g_outs, _ in groups:
        per_group.append(list(res[o0:o0 + len(g_outs)]))
        o0 += len(g_outs)
    return per_group, res[n_out:]


def _adamw_shard_group(parts, w, m, v, transposed):
    def compute(ins, outs):
        r_ref, w_ref, m_ref, v_ref = ins
        g = _sum_partials(r_ref)
        if transposed:
            g = g.T
        delta, m2, v2 = _adamw(w_ref[...], g, m_ref[...], v_ref[...])
        for o, val in zip(outs, (g, delta, m2, v2)):
            o[...] = val

    return [parts, w, m, v], [jax.ShapeDtypeStruct(w.shape, F32)] * 4, compute


def _finish_small(gws8, gbs8, gcw8, vmix8, vmlp8, vout8, small):
    names = ["sgu_ln_g", "sgu_ln_b", "w_s", "b_s", "conv_b", "conv_ln_g", "conv_ln_b", "ln1_g", "ln1_b", "ln2_g", "ln2_b"]
    flat = []
    for n in names:
        flat += list(small[n])

    def compute(ins, outs):
        gws_ref, gbs_ref, gcw_ref, vmix_ref, vmlp_ref, vout_ref = ins[:6]
        wmv = ins[6:]
        loss_o, gcw_o = outs[0], outs[1]
        outs = outs[2:]
        gws = _sum_partials(gws_ref)
        gbs = _sum_partials(gbs_ref)
        vmix = _sum_partials(vmix_ref)
        vmlp = _sum_partials(vmlp_ref)
        vout = _sum_partials(vout_ref)
        gcw_o[...] = _sum_partials(gcw_ref)
        loss = (0.5 / D_MODEL) * jnp.sum(vout[0:1, :], axis=1, keepdims=True)
        loss_o[...] = jnp.broadcast_to(loss, loss_o.shape)
        rows = lax.broadcasted_iota(jnp.int32, (N_HEADS * CHUNK, CHUNK), 0)
        cols = lax.broadcasted_iota(jnp.int32, (N_HEADS * CHUNK, CHUNK), 1)
        gws = jnp.where((rows & (CHUNK - 1)) >= cols, gws, 0.0)
        grads = {
            "sgu_ln_g": vmix[0:1, :], "sgu_ln_b": vmix[1:2, :], "w_s": gws, "b_s": gbs,
            "conv_b": vmix[2:3, :], "conv_ln_g": vmix[3:4, :], "conv_ln_b": vmix[4:5, :],
            "ln1_g": vmlp[0:1, :], "ln1_b": vmlp[1:2, :], "ln2_g": vout[1:2, :], "ln2_b": vout[2:3, :],
        }
        for k, n in enumerate(names):
            w_ref, m_ref, v_ref = wmv[3 * k:3 * k + 3]
            g = grads[n]
            delta, m2, v2 = _adamw(w_ref[...], g, m_ref[...], v_ref[...])
            outs[4 * k][...] = g
            outs[4 * k + 1][...] = delta
            outs[4 * k + 2][...] = m2
            outs[4 * k + 3][...] = v2

    S = jax.ShapeDtypeStruct
    out_shape = [S((SUBLANES, 128), F32), S((CONV_ROWS, D_CONV), F32)]
    for n in names:
        out_shape += [S(small[n][0].shape, F32)] * 4
    (res,), _ = _staged_call(
        "finish_small", [([gws8, gbs8, gcw8, vmix8, vmlp8, vout8, *flat], out_shape, compute)], 40)
    upd = {n: res[2 + 4 * k:6 + 4 * k] for k, n in enumerate(names)}
    return res[0], res[1], upd


def _adamw_plain(name, g, w, m, v):
    def compute(ins, outs):
        g_ref, w_ref, m_ref, v_ref = ins
        for o, val in zip(outs, _adamw(w_ref[...], g_ref[...], m_ref[...], v_ref[...])):
            o[...] = val

    (res,), _ = _staged_call(name, [([g, w, m, v], [jax.ShapeDtypeStruct(w.shape, F32)] * 3, compute)], 16)
    return res


TOKEN_TILE_FWD_MIX = 512
TOKEN_TILE_BWD_MIX = 256
TOKEN_TILE_FWD_MLP = 512
TOKEN_TILE_BWD_MLP = 512
TOKEN_TILE_WGRAD = 1024


def kernel(x, w_in, sgu_ln_g, sgu_ln_b, w_s, b_s, conv_w, conv_b, conv_ln_g, conv_ln_b, w_out, ln1_g, ln1_b, w_gate, w_up, w_down, ln2_g, ln2_b, loss_target, m_w_in, m_sgu_ln_g, m_sgu_ln_b, m_w_s, m_b_s, m_conv_w, m_conv_b, m_conv_ln_g, m_conv_ln_b, m_w_out, m_ln1_g, m_ln1_b, m_w_gate, m_w_up, m_w_down, m_ln2_g, m_ln2_b, v_w_in, v_sgu_ln_g, v_sgu_ln_b, v_w_s, v_b_s, v_conv_w, v_conv_b, v_conv_ln_g, v_conv_ln_b, v_w_out, v_ln1_g, v_ln1_b, v_w_gate, v_w_up, v_w_down, v_ln2_g, v_ln2_b):
    xs = x[0]
    tgt = loss_target[0]

    (win_b, wout_b, wgt_b, wut_b, wd_b, cw_b, wcat, wcatt, bs_full) = _prep_weights(
        w_in[0], w_out[0], w_gate[0].T, w_up[0].T, w_down[0], conv_w[0], w_s[0], b_s[0])
    win_g, wout_g, cw_g = _exchange("gather_mix_weights", [], [win_b, wout_b, cw_b])
    win_g = win_g.reshape(2 * D_MODEL, D_MODEL)
    wout_g = wout_g.reshape(D_MODEL, D_MODEL)
    cw = jnp.transpose(cw_g[:, :, :D_CONV // N_DEV], (1, 0, 2)).reshape(CONV_ROWS, D_CONV)

    (proj, ycat, n1, rstd1, phi, y_conv), (wgt_g, wut_g, wd_g) = _fwd_mix(
        xs, win_g, wout_g, sgu_ln_g, sgu_ln_b, wcat, bs_full, cw, conv_b, conv_ln_g, conv_ln_b, TOKEN_TILE_FWD_MIX,
        _Exchange([], [wgt_b, wut_b, wd_b]))
    wgt_g = wgt_g.reshape(D_FF, D_MODEL)
    wut_g = wut_g.reshape(D_FF, D_MODEL)
    wd_g = wd_g.reshape(D_FF, D_MODEL)
    gate, up, hh, x1b, dr2, vout = _fwd_mlp(n1, tgt, ln1_g, ln1_b, ln2_g, ln2_b, wgt_g, wut_g, wd_g, TOKEN_TILE_FWD_MLP)

    dgate, dup, dr1, vmlp = _bwd_mlp(dr2, gate, up, n1, rstd1, ln1_g, wgt_g, wut_g, wd_g, TOKEN_TILE_BWD_MLP)
    tk = TOKEN_TILE_WGRAD
    g_wgt = _wgrad("wgrad_gate", dgate, x1b, N_DEV, tk)
    g_wut = _wgrad("wgrad_up", dup, x1b, N_DEV, tk)
    g_wd = _wgrad("wgrad_down", hh, dr2, N_DEV, tk)
    (gx, dproj, gws, gbs, gcw, vmix), (r_wgt, r_wut, r_wd) = _bwd_mix(
        dr1, proj, phi, y_conv, win_g, wout_g, sgu_ln_g, sgu_ln_b, wcat, wcatt, bs_full, cw, conv_ln_g, conv_ln_b,
        TOKEN_TILE_BWD_MIX, _Exchange([g_wgt, g_wut, g_wd], []))
    g_win, (gws8, gbs8, gcw8, vmix8, vmlp8, vout8) = _wgrad(
        "wgrad_in", dproj, xs, N_DEV, tk,
        _Exchange([], [gws, gbs, gcw, vmix, vmlp, vout]))
    g_wout, (r_win,) = _wgrad("wgrad_out", ycat, dr1, N_DEV, tk, _Exchange([g_win], []))
    (u_gate, u_up, u_down), (r_wout,) = _staged_call(
        "adamw_mlp",
        [_adamw_shard_group(r_wgt, w_gate[0].T, m_w_gate[0].T, v_w_gate[0].T, False),
         _adamw_shard_group(r_wut, w_up[0].T, m_w_up[0].T, v_w_up[0].T, False),
         _adamw_shard_group(r_wd, w_down[0], m_w_down[0], v_w_down[0], False)],
        56, _Exchange([g_wout], []))
    (u_in, u_out), _ = _staged_call(
        "adamw_mix",
        [_adamw_shard_group(r_win, w_in[0], m_w_in[0], v_w_in[0], True),
         _adamw_shard_group(r_wout, w_out[0], m_w_out[0], v_w_out[0], False)], 32)
    big = {"w_in": u_in, "w_out": u_out, "w_gate": u_gate, "w_up": u_up, "w_down": u_down}
    small_in = {
        "sgu_ln_g": (sgu_ln_g, m_sgu_ln_g, v_sgu_ln_g), "sgu_ln_b": (sgu_ln_b, m_sgu_ln_b, v_sgu_ln_b),
        "w_s": tuple(a.reshape(N_HEADS * CHUNK, CHUNK) for a in (w_s, m_w_s, v_w_s)),
        "b_s": (b_s[0], m_b_s[0], v_b_s[0]),
        "conv_b": (conv_b, m_conv_b, v_conv_b), "conv_ln_g": (conv_ln_g, m_conv_ln_g, v_conv_ln_g),
        "conv_ln_b": (conv_ln_b, m_conv_ln_b, v_conv_ln_b),
        "ln1_g": (ln1_g, m_ln1_g, v_ln1_g), "ln1_b": (ln1_b, m_ln1_b, v_ln1_b),
        "ln2_g": (ln2_g, m_ln2_g, v_ln2_g), "ln2_b": (ln2_b, m_ln2_b, v_ln2_b),
    }
    loss11, gcw_full, small = _finish_small(gws8, gbs8, gcw8, vmix8, vmlp8, vout8, small_in)

    me = 4 * lax.axis_index("x") + 2 * lax.axis_index("y") + lax.axis_index("c")
    g_cw = lax.dynamic_slice(gcw_full, (0, me * (D_CONV // N_DEV)), (CONV_WIDTH, D_CONV // N_DEV))
    d_cw, m_cw, v_cw = _adamw_plain("adamw_conv_w", g_cw, conv_w[0], m_conv_w[0], v_conv_w[0])

    shapes = {"w_s": w_s.shape, "b_s": b_s.shape}
    out = {}
    for n, r in big.items():
        out[n] = tuple((a.T if n in ("w_gate", "w_up") else a)[None] for a in r)
    for n, r in small.items():
        out[n] = tuple(a.reshape(shapes[n]) for a in r) if n in shapes else tuple(r)
    out["conv_w"] = tuple(a[None] for a in (g_cw, d_cw, m_cw, v_cw))

    order = ["w_in", "sgu_ln_g", "sgu_ln_b", "w_s", "b_s", "conv_w", "conv_b", "conv_ln_g", "conv_ln_b", "w_out",
             "ln1_g", "ln1_b", "w_gate", "w_up", "w_down", "ln2_g", "ln2_b"]
    loss = loss11[0, 0]
    return (loss, gx[None], *[out[n][0] for n in order], *[out[n][1] for n in order],
            *[out[n][2] for n in order], *[out[n][3] for n in order])
```

```python
import jax
import jax.numpy as jnp
from jax import lax
from jax.experimental import pallas as pl
from jax.experimental.pallas import tpu as pltpu

F32 = jnp.float32
BF16 = jnp.bfloat16

D_MODEL = 1024
D_SGU = 512
D_CONV = 512
N_HEADS = 8
CHUNK = 128
CONV_WIDTH = 31
CONV_ROWS = 32
HALO = 32
D_FF = 2816
N_DEV = 8
FF_SHARD = D_FF // N_DEV
ALPHA = (2.0 * 1) ** 0.25
LN_EPS = 1e-5
INV_SQRT2 = 0.7071067811865476
INV_SQRT_2PI = 0.3989422804014327

ADAM_LR = 0.001
ADAM_B1 = 0.9
ADAM_B2 = 0.999
ADAM_EPS = 1e-08
ADAM_WD = 0.01
ADAM_STEP = 10

MXU_COLS = 256
SUBLANES = 8
CONV_ROW_BLOCK = 32
WGRAD_ROW_BLOCK = 32
SHIFT_ROWS = HALO - SUBLANES
MIB = 1024 * 1024

HBM = pl.BlockSpec(memory_space=pltpu.HBM)
ANY = pl.BlockSpec(memory_space=pl.ANY)
MESH = pl.DeviceIdType.MESH


def _params(vmem_mib, grid_dims=0):
    kw = dict(vmem_limit_bytes=vmem_mib * MIB)
    if grid_dims:
        kw["dimension_semantics"] = ("arbitrary",) * grid_dims
    return pltpu.CompilerParams(**kw)


def _full(shape):
    return pl.BlockSpec(shape, lambda i: (0,) * len(shape))


def _dot(a, b):
    return jnp.dot(a, b, preferred_element_type=F32)


def _dot_nt(a, b):
    return lax.dot_general(a, b, (((1,), (1,)), ((), ())), preferred_element_type=F32)


def _dot_tn(a, b):
    return lax.dot_general(a, b, (((0,), (0,)), ((), ())), preferred_element_type=F32)


def _normal_cdf(x):
    return 0.5 * (1.0 + lax.erf(x * INV_SQRT2))


def _gelu_grad(x, cdf):
    return cdf + x * jnp.exp(-0.5 * x * x) * INV_SQRT_2PI


def _ln_fwd(v):
    mu = jnp.mean(v, axis=-1, keepdims=True)
    d = v - mu
    var = jnp.mean(d * d, axis=-1, keepdims=True)
    rstd = lax.rsqrt(var + LN_EPS)
    return d * rstd, rstd


def _ln_bwd(dyhat, yhat, rstd):
    m1 = jnp.mean(dyhat, axis=-1, keepdims=True)
    m2 = jnp.mean(dyhat * yhat, axis=-1, keepdims=True)
    return rstd * (dyhat - m1 - yhat * m2)


def _colsum(v):
    return jnp.sum(v, axis=0, keepdims=True)


def _head_pair_stack(v, lo):
    return jnp.concatenate([jnp.where(lo, v, 0.0), jnp.where(lo, 0.0, v)], axis=0).astype(BF16)


def _lo_mask():
    return lax.broadcasted_iota(jnp.int32, (CHUNK, CHUNK), 1) < (CHUNK // 2)


def _head_selector():
    head = lax.broadcasted_iota(jnp.int32, (N_HEADS, D_SGU), 0)
    lane = lax.broadcasted_iota(jnp.int32, (N_HEADS, D_SGU), 1)
    width = D_SGU // N_HEADS
    return ((lane >= head * width) & (lane < (head + 1) * width)).astype(F32)


def _shifted_copies(pad_ref, sh_ref, rows):
    for r in range(1, SUBLANES):
        sh_ref[r - 1, 0:rows, :] = pad_ref[pl.ds(r, rows), :]


def _tap_groups(offset_of_tap):
    groups = {}
    for k in range(CONV_WIDTH):
        o = offset_of_tap(k)
        groups.setdefault(o % SUBLANES, []).append((k, o // SUBLANES))
    return groups


def _tap_window(pad_ref, sh_ref, r, taps, row0, rows):
    q0 = min(q for _, q in taps)
    q1 = max(q for _, q in taps)
    src = pad_ref if r == 0 else sh_ref.at[r - 1]
    win = src[pl.ds(row0 + SUBLANES * q0, SUBLANES * (q1 - q0) + rows), :]
    return win, [(k, SUBLANES * (q - q0)) for k, q in taps]


def _causal_conv(pad_ref, sh_ref, w_ref, out_ref, rows, offset_of_tap, bias=None):
    groups = _tap_groups(offset_of_tap)

    def block(b, carry):
        row0 = pl.multiple_of(b * CONV_ROW_BLOCK, CONV_ROW_BLOCK)
        if bias is None:
            acc = jnp.zeros((CONV_ROW_BLOCK, D_CONV), F32)
        else:
            acc = jnp.broadcast_to(bias, (CONV_ROW_BLOCK, D_CONV))
        for r, taps in groups.items():
            win, starts = _tap_window(pad_ref, sh_ref, r, taps, row0, CONV_ROW_BLOCK)
            for k, s in starts:
                acc = acc + w_ref[k:k + 1, :] * win[s:s + CONV_ROW_BLOCK, :]
        out_ref[pl.ds(row0, CONV_ROW_BLOCK), :] = acc
        return carry

    lax.fori_loop(0, rows // CONV_ROW_BLOCK, block, 0)


def _conv_weight_grad(dy_ref, pad_ref, sh_ref, acc_ref, rows, offset_of_tap):
    groups = _tap_groups(offset_of_tap)
    for r, taps in groups.items():

        def block(b, parts, r=r, taps=taps):
            row0 = pl.multiple_of(b * WGRAD_ROW_BLOCK, WGRAD_ROW_BLOCK)
            dyb = dy_ref[pl.ds(row0, WGRAD_ROW_BLOCK), :]
            win, starts = _tap_window(pad_ref, sh_ref, r, taps, row0, WGRAD_ROW_BLOCK)
            out = []
            for part, (_, s) in zip(parts, starts):
                pr = dyb * win[s:s + WGRAD_ROW_BLOCK, :]
                out.append(part + pr.reshape(WGRAD_ROW_BLOCK // SUBLANES, SUBLANES, D_CONV).sum(axis=0))
            return tuple(out)

        zeros = tuple(jnp.zeros((SUBLANES, D_CONV), F32) for _ in taps)
        parts = lax.fori_loop(0, rows // WGRAD_ROW_BLOCK, block, zeros)
        for part, (k, _) in zip(parts, taps):
            acc_ref[k] += part


def _prep_weights(w_in, w_out, w_gate_t, w_up_t, w_down, conv_w, w_s, b_s):
    def compute(ins, outs):
        win_ref, wout_ref, wgt_ref, wut_ref, wd_ref, cw_ref, ws_ref, bs_ref = ins
        win_o, wout_o, wgt_o, wut_o, wd_o, cw_o, wcat_o, wcatt_o, bsf_o = outs
        win_o[...] = win_ref[...].T.astype(BF16)
        wout_o[...] = wout_ref[...].astype(BF16)
        wgt_o[...] = wgt_ref[...].astype(BF16)
        wut_o[...] = wut_ref[...].astype(BF16)
        wd_o[...] = wd_ref[...].astype(BF16)
        cw_o[...] = jnp.zeros(cw_o.shape, F32)
        cw_o[0:CONV_WIDTH, 0:D_CONV // N_DEV] = cw_ref[...]
        row = lax.broadcasted_iota(jnp.int32, (CHUNK, CHUNK), 0)
        col = lax.broadcasted_iota(jnp.int32, (CHUNK, CHUNK), 1)
        causal = row >= col
        for h in range(N_HEADS):
            w = jnp.where(causal, ws_ref[h], 0.0)
            p, half = h // 2, (h % 2) * CHUNK
            wcat_o[p, :, half:half + CHUNK] = w.astype(BF16)
            wcatt_o[p, :, half:half + CHUNK] = w.T.astype(BF16)
        bsf_o[...] = lax.dot_general(bs_ref[...], _head_selector(), (((0,), (0,)), ((), ())),
                                     preferred_element_type=F32, precision=lax.Precision.HIGHEST)

    S = jax.ShapeDtypeStruct
    out_shapes = [S((256, D_MODEL), BF16), S((128, D_MODEL), BF16), S((FF_SHARD, D_MODEL), BF16),
                  S((FF_SHARD, D_MODEL), BF16), S((FF_SHARD, D_MODEL), BF16), S((CONV_ROWS, 128), F32),
                  S((4, CHUNK, 2 * CHUNK), BF16), S((4, CHUNK, 2 * CHUNK), BF16), S((CHUNK, D_SGU), F32)]
    (res,), _ = _staged_call(
        "prep_weights", [([w_in, w_out, w_gate_t, w_up_t, w_down, conv_w, w_s, b_s], out_shapes, compute)], 32)
    return res


def _mesh_position():
    x, y, c = lax.axis_index("x"), lax.axis_index("y"), lax.axis_index("c")
    return x, y, c


def _peers(x, y, c):
    out = []
    for k in range(1, N_DEV):
        px = 1 - x if (k >> 2) & 1 else x
        py = 1 - y if (k >> 1) & 1 else y
        pc = 1 - c if k & 1 else c
        out.append(((px, py, pc), 4 * px + 2 * py + pc))
    return out


class _Exchange:
    def __init__(self, scatter, gather):
        self.arrays = list(scatter) + list(gather)
        self.n_sc = len(scatter)
        self.n = len(self.arrays)
        self.out_shape = [jax.ShapeDtypeStruct(a.shape if k < self.n_sc else (N_DEV,) + a.shape, a.dtype)
                          for k, a in enumerate(self.arrays)]
        n_remote = self.n * (N_DEV - 1)
        self.scratch = [pltpu.SemaphoreType.DMA((n_remote,)), pltpu.SemaphoreType.DMA((n_remote,)),
                        pltpu.SemaphoreType.DMA((self.n,))] if self.n else []

    def _copies(self, src, dst, sems):
        send_sems, recv_sems, local_sems = sems
        x, y, c = _mesh_position()
        me = 4 * x + 2 * y + c
        locals_, first, arrivals, passed, last = [], [], [], [], []

        def remote(a, k, src_ref, slot, to):
            s = a * (N_DEV - 1) + k
            return pltpu.make_async_remote_copy(src_ref=src_ref, dst_ref=dst[a].at[slot], send_sem=send_sems.at[s],
                                                recv_sem=recv_sems.at[s], device_id=to, device_id_type=MESH)

        for a in range(self.n):
            if a < self.n_sc:
                locals_.append(pltpu.make_async_copy(src[a].at[me], dst[a].at[me], local_sems.at[a]))
                for k, (peer, pid) in enumerate(_peers(x, y, c)):
                    first.append(remote(a, k, src[a].at[pid], me, peer))
                    last.append(remote(a, k, src[a].at[pid], pid, peer))
                continue
            locals_.append(pltpu.make_async_copy(src[a], dst[a].at[me], local_sems.at[a]))
            sibling, sib_id = (x, y, 1 - c), 4 * x + 2 * y + (1 - c)
            chips = [(1 - x, y), (x, 1 - y), (1 - x, 1 - y)]
            first.append(remote(a, 0, src[a], me, sibling))
            last.append(remote(a, 0, src[a], sib_id, sibling))
            for j, (px, py) in enumerate(chips):
                same, other = 4 * px + 2 * py + c, 4 * px + 2 * py + (1 - c)
                first.append(remote(a, 1 + j, src[a], me, (px, py, c)))
                arrivals.append(remote(a, 1 + j, src[a], same, (px, py, c)))
                passed.append(remote(a, 4 + j, dst[a].at[same], same, sibling))
                last.append(remote(a, 4 + j, dst[a].at[other], other, sibling))
        return locals_, first, arrivals, passed, last

    def start(self, src, dst, sems):
        if not self.n:
            return
        locals_, first, _, _, _ = self._copies(src, dst, sems)
        for cp in locals_ + first:
            cp.start()

    def forward(self, src, dst, sems):
        if self.n == self.n_sc:
            return
        _, _, arrivals, passed, _ = self._copies(src, dst, sems)
        for arrived, cp in zip(arrivals, passed):
            arrived.wait_recv()
            cp.start()

    def wait(self, src, dst, sems):
        if not self.n:
            return
        locals_, first, _, passed, last = self._copies(src, dst, sems)
        for cp in last:
            cp.wait_recv()
        for cp in first + passed:
            cp.wait_send()
        for cp in locals_:
            cp.wait()


def _exchange(name, scatter, gather):
    ex = _Exchange(scatter, gather)
    n = ex.n

    def body(*refs):
        src, dst, sems = refs[:n], refs[n:2 * n], refs[2 * n:]
        ex.start(src, dst, sems)
        ex.forward(src, dst, sems)
        ex.wait(src, dst, sems)

    return pl.pallas_call(
        body, name=name, out_shape=tuple(ex.out_shape), in_specs=[ANY] * n, out_specs=(ANY,) * n,
        scratch_shapes=ex.scratch,
    )(*ex.arrays)


def _forward_step(n_steps):
    return (11 * n_steps) // 16


def _hosted(ex, refs, n_in, n_out):
    ins, ex_src = refs[:n_in], refs[n_in:n_in + ex.n]
    rest = refs[n_in + ex.n:]
    outs, ex_dst = rest[:n_out], rest[n_out:n_out + ex.n]
    rest = rest[n_out + ex.n:]
    n_own = len(rest) - len(ex.scratch)
    return ins, outs, rest[:n_own], (ex_src, ex_dst, rest[n_own:])


def _fwd_mix(x, win_g, wout_g, sgu_g, sgu_b, wcat, bs_full, cw, cb, cg, cbeta, tm, ex):
    T = x.shape[0]
    nt = T // tm

    def body(*refs):
        ins, outs, scratch, ex_refs = _hosted(ex, refs, 11, 6)
        x_ref, win_ref, wout_ref, sg_ref, sb_ref, wcat_ref, bs_ref, cw_ref, cb_ref, cg_ref, cbeta_ref = ins
        proj_ref, ycat_ref, n1_ref, rstd1_ref, phi_ref, y_ref = outs
        hpad, hshift = scratch
        i = pl.program_id(0)

        @pl.when(i == 0)
        def _():
            ex.start(*ex_refs)

        xf = x_ref[...]
        xb = xf.astype(BF16)
        proj_ref[...] = _dot_nt(xb, win_ref[...])
        phi_ref[...] = _normal_cdf(proj_ref[:, 0:2 * D_SGU])
        u = proj_ref[:, 0:D_SGU] * phi_ref[:, 0:D_SGU]
        v = proj_ref[:, D_SGU:2 * D_SGU] * phi_ref[:, D_SGU:2 * D_SGU]
        vhat, _ = _ln_fwd(v)
        vn = vhat * sg_ref[...] + sb_ref[...]
        lo = _lo_mask()
        for c in range(tm // CHUNK):
            rows = slice(CHUNK * c, CHUNK * (c + 1))
            for p in range(4):
                lanes = slice(CHUNK * p, CHUNK * (p + 1))
                mixed = _dot(wcat_ref[p], _head_pair_stack(vn[rows, lanes], lo)) + bs_ref[:, lanes]
                ycat_ref[rows, lanes] = (u[rows, lanes] * mixed).astype(BF16)
        base = 2 * D_SGU
        a = proj_ref[:, base:base + D_CONV]
        g = proj_ref[:, base + D_CONV:base + 2 * D_CONV]

        @pl.when(i == 0)
        def _():
            hpad[0:HALO, :] = jnp.zeros((HALO, D_CONV), F32)

        hpad[HALO:HALO + tm, :] = a * jax.nn.sigmoid(g)
        _shifted_copies(hpad, hshift, tm + SHIFT_ROWS)
        _causal_conv(hpad, hshift, cw_ref, y_ref, tm, lambda k: HALO - (CONV_WIDTH - 1) + k, bias=cb_ref[...])
        hpad[0:HALO, :] = hpad[tm:tm + HALO, :]
        yhat, _ = _ln_fwd(y_ref[...])
        yn = yhat * cg_ref[...] + cbeta_ref[...]
        ycat_ref[:, D_SGU:D_SGU + D_CONV] = (yn * jax.nn.sigmoid(yn)).astype(BF16)
        r1 = ALPHA * xf + _dot(ycat_ref[...], wout_ref[...])
        n1, rstd1 = _ln_fwd(r1)
        n1_ref[...] = n1
        rstd1_ref[...] = rstd1

        @pl.when(i == _forward_step(nt))
        def _():
            ex.forward(*ex_refs)

        @pl.when(i == nt - 1)
        def _():
            ex.wait(*ex_refs)

    S = jax.ShapeDtypeStruct
    row = lambda w: pl.BlockSpec((tm, w), lambda i: (i, 0))
    res = pl.pallas_call(
        body, name="fwd_mix", grid=(nt,),
        in_specs=[row(D_MODEL), _full(win_g.shape), _full(wout_g.shape), _full(sgu_g.shape), _full(sgu_b.shape),
                  _full(wcat.shape), _full(bs_full.shape), _full(cw.shape), _full(cb.shape), _full(cg.shape),
                  _full(cbeta.shape)] + [ANY] * ex.n,
        out_specs=(row(2 * D_MODEL), row(D_MODEL), row(D_MODEL), row(1), row(2 * D_SGU), row(D_CONV)) + (ANY,) * ex.n,
        out_shape=(S((T, 2 * D_MODEL), F32), S((T, D_MODEL), BF16), S((T, D_MODEL), F32), S((T, 1), F32),
                   S((T, 2 * D_SGU), F32), S((T, D_CONV), F32), *ex.out_shape),
        scratch_shapes=[pltpu.VMEM((tm + HALO, D_CONV), F32),
                        pltpu.VMEM((SUBLANES - 1, tm + SHIFT_ROWS, D_CONV), F32)] + ex.scratch,
        compiler_params=_params(56, 1),
    )(x, win_g, wout_g, sgu_g, sgu_b, wcat, bs_full, cw, cb, cg, cbeta, *ex.arrays)
    return res[:6], res[6:]


def _load_resident(pairs, sems):
    cps = [pltpu.make_async_copy(s, d, sems.at[k]) for k, (s, d) in enumerate(pairs)]
    for cp in cps:
        cp.start()
    for cp in cps:
        cp.wait()


def _fwd_mlp(n1, tgt, l1g, l1b, l2g, l2b, wgt, wut, wd, tm):
    T = n1.shape[0]
    nt = T // tm
    nf = D_FF // MXU_COLS

    def body(n1_ref, tgt_ref, l1g_ref, l1b_ref, l2g_ref, l2b_ref, wg_hbm, wu_hbm, wd_hbm,
             gate_ref, up_ref, hh_ref, x1b_ref, dr2_ref, stat_ref, wg_s, wu_s, wd_s, sems):
        i = pl.program_id(0)

        @pl.when(i == 0)
        def _():
            _load_resident([(wg_hbm, wg_s), (wu_hbm, wu_s), (wd_hbm, wd_s)], sems)
            stat_ref[...] = jnp.zeros(stat_ref.shape, F32)

        x1 = n1_ref[...] * l1g_ref[...] + l1b_ref[...]
        x1b = x1.astype(BF16)
        x1b_ref[...] = x1b
        for f in range(nf):
            cols = slice(MXU_COLS * f, MXU_COLS * (f + 1))
            gt = _dot_nt(x1b, wg_s[cols, :])
            ut = _dot_nt(x1b, wu_s[cols, :])
            gate_ref[:, cols] = gt.astype(BF16)
            up_ref[:, cols] = ut.astype(BF16)
            hh_ref[:, cols] = (gt * jax.nn.sigmoid(gt) * ut).astype(BF16)
        r2 = ALPHA * x1 + _dot(hh_ref[...], wd_s[...])
        n2, rstd2 = _ln_fwd(r2)
        x2 = n2 * l2g_ref[...] + l2b_ref[...]
        diff = x2 - tgt_ref[...]
        dx2 = diff * (1.0 / D_MODEL)
        stat_ref[0:1, :] += _colsum(diff * diff)
        stat_ref[1:2, :] += _colsum(dx2 * n2)
        stat_ref[2:3, :] += _colsum(dx2)
        dr2_ref[...] = _ln_bwd(dx2 * l2g_ref[...], n2, rstd2)

    S = jax.ShapeDtypeStruct
    row = lambda w: pl.BlockSpec((tm, w), lambda i: (i, 0))
    vec = _full((1, D_MODEL))
    return pl.pallas_call(
        body, name="fwd_mlp", grid=(nt,),
        in_specs=[row(D_MODEL), row(D_MODEL), vec, vec, vec, vec, ANY, ANY, ANY],
        out_specs=(row(D_FF), row(D_FF), row(D_FF), row(D_MODEL), row(D_MODEL), _full((8, D_MODEL))),
        out_shape=(S((T, D_FF), BF16), S((T, D_FF), BF16), S((T, D_FF), BF16), S((T, D_MODEL), BF16),
                   S((T, D_MODEL), F32), S((8, D_MODEL), F32)),
        scratch_shapes=[pltpu.VMEM((D_FF, D_MODEL), BF16)] * 3 + [pltpu.SemaphoreType.DMA((3,))],
        compiler_params=_params(56, 1),
    )(n1, tgt, l1g, l1b, l2g, l2b, wgt, wut, wd)


def _bwd_mlp(dr2, gate, up, n1, rstd1, l1g, wgt, wut, wd, tm):
    T = n1.shape[0]
    nt = T // tm
    nf = D_FF // MXU_COLS

    def body(dr2_ref, gate_ref, up_ref, n1_ref, rstd1_ref, l1g_ref, wg_hbm, wu_hbm, wd_hbm,
             dgate_ref, dup_ref, dr1_ref, stat_ref, wg_s, wu_s, wd_s, sems):
        i = pl.program_id(0)

        @pl.when(i == 0)
        def _():
            _load_resident([(wg_hbm, wg_s), (wu_hbm, wu_s), (wd_hbm, wd_s)], sems)
            stat_ref[...] = jnp.zeros(stat_ref.shape, F32)

        dr2 = dr2_ref[...]
        dr2b = dr2.astype(BF16)
        for f in range(nf):
            cols = slice(MXU_COLS * f, MXU_COLS * (f + 1))
            dhh = _dot_nt(dr2b, wd_s[cols, :])
            gt = gate_ref[:, cols].astype(F32)
            ut = up_ref[:, cols].astype(F32)
            sg = jax.nn.sigmoid(gt)
            dgate_ref[:, cols] = (dhh * ut * (sg * (1.0 + gt * (1.0 - sg)))).astype(BF16)
            dup_ref[:, cols] = (dhh * (gt * sg)).astype(BF16)
        dx1 = ALPHA * dr2 + _dot(dgate_ref[...], wg_s[...]) + _dot(dup_ref[...], wu_s[...])
        n1 = n1_ref[...]
        stat_ref[0:1, :] += _colsum(dx1 * n1)
        stat_ref[1:2, :] += _colsum(dx1)
        dr1_ref[...] = _ln_bwd(dx1 * l1g_ref[...], n1, rstd1_ref[...])

    S = jax.ShapeDtypeStruct
    row = lambda w: pl.BlockSpec((tm, w), lambda i: (i, 0))
    return pl.pallas_call(
        body, name="bwd_mlp", grid=(nt,),
        in_specs=[row(D_MODEL), row(D_FF), row(D_FF), row(D_MODEL), row(1), _full((1, D_MODEL)), ANY, ANY, ANY],
        out_specs=(row(D_FF), row(D_FF), row(D_MODEL), _full((8, D_MODEL))),
        out_shape=(S((T, D_FF), BF16), S((T, D_FF), BF16), S((T, D_MODEL), F32), S((8, D_MODEL), F32)),
        scratch_shapes=[pltpu.VMEM((D_FF, D_MODEL), BF16)] * 3 + [pltpu.SemaphoreType.DMA((3,))],
        compiler_params=_params(56, 1),
    )(dr2, gate, up, n1, rstd1, l1g, wgt, wut, wd)


def _bwd_mix(dr1, proj, phi, y, win_g, wout_g, sgu_g, sgu_b, wcat, wcatt, bs_full, cw, cg, cbeta, tm, ex):
    T = dr1.shape[0]
    nt = T // tm
    halo_blocks = tm // HALO

    def body(*refs):
        ins, outs, scratch, ex_refs = _hosted(ex, refs, 15, 6)
        (dr1_ref, proj_ref, halo_ref, phi_ref, y_ref, win_ref, wout_ref, sg_ref, sb_ref, wcat_ref, wcatt_ref, bs_ref,
         cw_ref, cg_ref, cbeta_ref) = ins
        gx_ref, dproj_ref, gws_out, gbs_out, gcw_ref, vec_ref = outs
        hpad, hshift, dypad, dyshift, dhbuf, dubuf, dvnbuf, gcw_acc, gws_ref, gbs_ref = scratch
        i = pl.program_id(0)
        tile = nt - 1 - i

        @pl.when(i == 0)
        def _():
            ex.start(*ex_refs)
            gws_ref[...] = jnp.zeros(gws_ref.shape, F32)
            gbs_ref[...] = jnp.zeros(gbs_ref.shape, F32)
            gcw_ref[...] = jnp.zeros(gcw_ref.shape, F32)
            vec_ref[...] = jnp.zeros(vec_ref.shape, F32)
            gcw_acc[...] = jnp.zeros(gcw_acc.shape, F32)
            dypad[tm:tm + HALO, :] = jnp.zeros((HALO, D_CONV), F32)

        dr1 = dr1_ref[...]
        dycat = _dot_nt(dr1.astype(BF16), wout_ref[...])
        pu = proj_ref[:, 0:D_SGU]
        pv = proj_ref[:, D_SGU:2 * D_SGU]
        u = pu * phi_ref[:, 0:D_SGU]
        vhat, rstd_v = _ln_fwd(pv * phi_ref[:, D_SGU:2 * D_SGU])
        vn = vhat * sg_ref[...] + sb_ref[...]
        lo = _lo_mask()
        for c in range(tm // CHUNK):
            rows = slice(CHUNK * c, CHUNK * (c + 1))
            for p in range(4):
                lanes = slice(CHUNK * p, CHUNK * (p + 1))
                vstack = _head_pair_stack(vn[rows, lanes], lo)
                mixed = _dot(wcat_ref[p], vstack) + bs_ref[:, lanes]
                d_a = dycat[rows, lanes]
                dubuf[rows, lanes] = d_a * mixed
                dm = d_a * u[rows, lanes]
                gbs_ref[:, lanes] += dm
                dstack = _head_pair_stack(dm, lo)
                gws_ref[2 * CHUNK * p:2 * CHUNK * (p + 1), :] += _dot_nt(dstack, vn[rows, lanes].astype(BF16))
                dvnbuf[rows, lanes] = _dot(wcatt_ref[p], dstack)
        dvn = dvnbuf[...]
        vec_ref[0:1, :] += _colsum(dvn * vhat)
        vec_ref[1:2, :] += _colsum(dvn)
        dv = _ln_bwd(dvn * sg_ref[...], vhat, rstd_v)
        dproj_ref[:, 0:D_SGU] = (dubuf[...] * _gelu_grad(pu, phi_ref[:, 0:D_SGU])).astype(BF16)
        dproj_ref[:, D_SGU:2 * D_SGU] = (dv * _gelu_grad(pv, phi_ref[:, D_SGU:2 * D_SGU])).astype(BF16)
        base = 2 * D_SGU
        a = proj_ref[:, base:base + D_CONV]
        sgm = jax.nn.sigmoid(proj_ref[:, base + D_CONV:base + 2 * D_CONV])
        h_before = halo_ref[:, 0:D_CONV] * jax.nn.sigmoid(halo_ref[:, D_CONV:2 * D_CONV])
        hpad[0:HALO, :] = jnp.where(tile > 0, h_before, 0.0)
        hpad[HALO:HALO + tm, :] = a * sgm
        _shifted_copies(hpad, hshift, tm + SHIFT_ROWS)
        h_offset = lambda k: HALO - (CONV_WIDTH - 1) + k
        yhat, rstd_y = _ln_fwd(y_ref[...])
        yn = yhat * cg_ref[...] + cbeta_ref[...]
        s = jax.nn.sigmoid(yn)
        dyn = dycat[:, D_SGU:D_SGU + D_CONV] * (s * (1.0 + yn * (1.0 - s)))
        vec_ref[3:4, :] += _colsum(dyn * yhat)
        vec_ref[4:5, :] += _colsum(dyn)
        dy = _ln_bwd(dyn * cg_ref[...], yhat, rstd_y)
        vec_ref[2:3, :] += _colsum(dy)
        dypad[0:tm, :] = dy
        _shifted_copies(dypad, dyshift, tm + SHIFT_ROWS)
        _causal_conv(dypad, dyshift, cw_ref, dhbuf, tm, lambda k: (CONV_WIDTH - 1) - k)
        _conv_weight_grad(dypad, hpad, hshift, gcw_acc, tm, h_offset)
        dypad[tm:tm + HALO, :] = dypad[0:HALO, :]
        dh = dhbuf[...]
        dproj_ref[:, base:base + D_CONV] = (dh * sgm).astype(BF16)
        dproj_ref[:, base + D_CONV:base + 2 * D_CONV] = (dh * a * sgm * (1.0 - sgm)).astype(BF16)
        gx_ref[...] = ALPHA * dr1 + _dot(dproj_ref[...], win_ref[...])

        @pl.when(i == _forward_step(nt))
        def _():
            ex.forward(*ex_refs)

        @pl.when(i == nt - 1)
        def _():
            gcw_ref[...] = gcw_acc[...].sum(axis=1)
            gws_out[...] = gws_ref[...].astype(BF16)
            gbs_out[...] = lax.dot_general(_head_selector(), gbs_ref[...], (((1,), (1,)), ((), ())),
                                           preferred_element_type=F32, precision=lax.Precision.HIGHEST)
            ex.wait(*ex_refs)

    S = jax.ShapeDtypeStruct
    row = lambda w: pl.BlockSpec((tm, w), lambda i: (nt - 1 - i, 0))
    halo = pl.BlockSpec((HALO, D_MODEL), lambda i: (jnp.maximum((nt - 1 - i) * halo_blocks - 1, 0), 1))
    res = pl.pallas_call(
        body, name="bwd_mix", grid=(nt,),
        in_specs=[row(D_MODEL), row(2 * D_MODEL), halo, row(2 * D_SGU), row(D_CONV), _full(win_g.shape),
                  _full(wout_g.shape), _full(sgu_g.shape), _full(sgu_b.shape), _full(wcat.shape), _full(wcatt.shape),
                  _full(bs_full.shape), _full(cw.shape), _full(cg.shape), _full(cbeta.shape)]
        + [ANY] * ex.n,
        out_specs=(row(D_MODEL), row(2 * D_MODEL), _full((N_HEADS * CHUNK, CHUNK)), _full((N_HEADS, CHUNK)),
                   _full((CONV_ROWS, D_CONV)), _full((8, D_CONV))) + (ANY,) * ex.n,
        out_shape=(S((T, D_MODEL), F32), S((T, 2 * D_MODEL), BF16), S((N_HEADS * CHUNK, CHUNK), BF16),
                   S((N_HEADS, CHUNK), F32), S((CONV_ROWS, D_CONV), F32), S((8, D_CONV), F32), *ex.out_shape),
        scratch_shapes=[pltpu.VMEM((tm + HALO, D_CONV), F32), pltpu.VMEM((SUBLANES - 1, tm + SHIFT_ROWS, D_CONV), F32),
                        pltpu.VMEM((tm + HALO, D_CONV), F32), pltpu.VMEM((SUBLANES - 1, tm + SHIFT_ROWS, D_CONV), F32),
                        pltpu.VMEM((tm, D_CONV), F32), pltpu.VMEM((tm, D_SGU), F32),
                        pltpu.VMEM((tm, D_SGU), F32), pltpu.VMEM((CONV_ROWS, 8, D_CONV), F32),
                        pltpu.VMEM((N_HEADS * CHUNK, CHUNK), F32), pltpu.VMEM((CHUNK, D_SGU), F32)] + ex.scratch,
        compiler_params=_params(56, 1),
    )(dr1, proj, proj, phi, y, win_g, wout_g, sgu_g, sgu_b, wcat, wcatt, bs_full, cw, cg, cbeta, *ex.arrays)
    return res[:6], res[6:]


def _wgrad(name, a, b, blocks, tk, ex=None, b_cols=None):
    T, M = a.shape
    col, N = b_cols or (0, b.shape[1])
    nk = T // tk
    out_shape = (blocks, M // blocks, N)
    ex = ex or _Exchange([], [])

    def body(*refs):
        (a_ref, b_ref), (o_ref,), (acc,), ex_refs = _hosted(ex, refs, 2, 1)
        i = pl.program_id(0)

        @pl.when(i == 0)
        def _():
            ex.start(*ex_refs)
            acc[...] = jnp.zeros(acc.shape, F32)

        acc[...] += _dot_tn(a_ref[...].astype(BF16), b_ref[...].astype(BF16))

        @pl.when(i == _forward_step(nk))
        def _():
            ex.forward(*ex_refs)

        @pl.when(i == nk - 1)
        def _():
            o_ref[...] = acc[...].astype(BF16)
            ex.wait(*ex_refs)

    res = pl.pallas_call(
        body, name=name, grid=(nk,),
        in_specs=[pl.BlockSpec((tk, M), lambda i: (i, 0)), pl.BlockSpec((tk, N), lambda i: (i, col))] + [ANY] * ex.n,
        out_specs=(_full((M, N)),) + (ANY,) * ex.n,
        out_shape=(jax.ShapeDtypeStruct((M, N), BF16), *ex.out_shape),
        scratch_shapes=[pltpu.VMEM((M, N), F32)] + ex.scratch,
        compiler_params=_params(56, 1),
    )(a, b, *ex.arrays)
    g = res[0].reshape(out_shape)
    return (g, res[1:]) if ex.n else g


def _adamw(w, g, m, v):
    m2 = ADAM_B1 * m + (1.0 - ADAM_B1) * g
    v2 = ADAM_B2 * v + (1.0 - ADAM_B2) * (g * g)
    m_hat = m2 / (1.0 - ADAM_B1 ** ADAM_STEP)
    v_hat = v2 / (1.0 - ADAM_B2 ** ADAM_STEP)
    delta = -ADAM_LR * (m_hat / (jnp.sqrt(v_hat) + ADAM_EPS) + ADAM_WD * w)
    return delta, m2, v2


def _sum_partials(r_ref):
    g = r_ref[0].astype(F32)
    for s in range(1, N_DEV):
        g = g + r_ref[s].astype(F32)
    return g


def _staged_call(name, groups, vmem_mib, ex=None):
    ex = ex or _Exchange([], [])
    inputs = [a for ins, _, _ in groups for a in ins]
    out_shapes = [s for _, outs, _ in groups for s in outs]
    n_in, n_out = len(inputs), len(out_shapes)

    def body(*refs):
        ins, outs, scratch, ex_refs = _hosted(ex, refs, n_in, n_out)
        in_bufs, out_bufs, sems = scratch[:n_in], scratch[n_in:n_in + n_out], scratch[n_in + n_out]
        ex.start(*ex_refs)
        loads = [pltpu.make_async_copy(ins[k], in_bufs[k], sems.at[k]) for k in range(n_in)]
        stores = [pltpu.make_async_copy(out_bufs[k], outs[k], sems.at[n_in + k]) for k in range(n_out)]
        for cp in loads:
            cp.start()
        i0 = o0 = 0
        for g_ins, g_outs, compute in groups:
            i1, o1 = i0 + len(g_ins), o0 + len(g_outs)
            for cp in loads[i0:i1]:
                cp.wait()
            compute(in_bufs[i0:i1], out_bufs[o0:o1])
            for cp in stores[o0:o1]:
                cp.start()
            i0, o0 = i1, o1
        for cp in stores:
            cp.wait()
        ex.forward(*ex_refs)
        ex.wait(*ex_refs)

    scratch = ([pltpu.VMEM(a.shape, a.dtype) for a in inputs] + [pltpu.VMEM(s.shape, s.dtype) for s in out_shapes]
               + [pltpu.SemaphoreType.DMA((n_in + n_out,))] + ex.scratch)
    res = pl.pallas_call(
        body, name=name, out_shape=(*out_shapes, *ex.out_shape),
        in_specs=[HBM] * n_in + [ANY] * ex.n, out_specs=(HBM,) * n_out + (ANY,) * ex.n,
        scratch_shapes=scratch, compiler_params=_params(vmem_mib),
    )(*inputs, *ex.arrays)
    per_group, o0 = [], 0
    for _, g_outs, _ in groups:
        per_group.append(list(res[o0:o0 + len(g_outs)]))
        o0 += len(g_outs)
    return per_group, res[n_out:]


def _adamw_shard_group(parts, w, m, v, transposed):
    n = len(parts)

    def compute(ins, outs):
        w_ref, m_ref, v_ref = ins[n:]
        lo = 0
        for r_ref in ins[:n]:
            g = _sum_partials(r_ref)
            cols = g.shape[1]
            if transposed:
                g, at = g.T, (slice(lo, lo + cols), slice(None))
            else:
                at = (slice(None), slice(lo, lo + cols))
            delta, m2, v2 = _adamw(w_ref[at], g, m_ref[at], v_ref[at])
            for o, val in zip(outs, (g, delta, m2, v2)):
                o[at] = val
            lo += cols

    return [*parts, w, m, v], [jax.ShapeDtypeStruct(w.shape, F32)] * 4, compute


def _finish_small(gws8, gbs8, gcw8, vmix8, vmlp8, vout8, small):
    names = ["sgu_ln_g", "sgu_ln_b", "w_s", "b_s", "conv_b", "conv_ln_g", "conv_ln_b", "ln1_g", "ln1_b", "ln2_g", "ln2_b"]
    flat = []
    for n in names:
        flat += list(small[n])

    def compute(ins, outs):
        gws_ref, gbs_ref, gcw_ref, vmix_ref, vmlp_ref, vout_ref = ins[:6]
        wmv = ins[6:]
        loss_o, gcw_o = outs[0], outs[1]
        outs = outs[2:]
        gws = _sum_partials(gws_ref)
        gbs = _sum_partials(gbs_ref)
        vmix = _sum_partials(vmix_ref)
        vmlp = _sum_partials(vmlp_ref)
        vout = _sum_partials(vout_ref)
        gcw_o[...] = _sum_partials(gcw_ref)
        loss = (0.5 / D_MODEL) * jnp.sum(vout[0:1, :], axis=1, keepdims=True)
        loss_o[...] = jnp.broadcast_to(loss, loss_o.shape)
        rows = lax.broadcasted_iota(jnp.int32, (N_HEADS * CHUNK, CHUNK), 0)
        cols = lax.broadcasted_iota(jnp.int32, (N_HEADS * CHUNK, CHUNK), 1)
        gws = jnp.where((rows & (CHUNK - 1)) >= cols, gws, 0.0)
        grads = {
            "sgu_ln_g": vmix[0:1, :], "sgu_ln_b": vmix[1:2, :], "w_s": gws, "b_s": gbs,
            "conv_b": vmix[2:3, :], "conv_ln_g": vmix[3:4, :], "conv_ln_b": vmix[4:5, :],
            "ln1_g": vmlp[0:1, :], "ln1_b": vmlp[1:2, :], "ln2_g": vout[1:2, :], "ln2_b": vout[2:3, :],
        }
        for k, n in enumerate(names):
            w_ref, m_ref, v_ref = wmv[3 * k:3 * k + 3]
            g = grads[n]
            delta, m2, v2 = _adamw(w_ref[...], g, m_ref[...], v_ref[...])
            outs[4 * k][...] = g
            outs[4 * k + 1][...] = delta
            outs[4 * k + 2][...] = m2
            outs[4 * k + 3][...] = v2

    S = jax.ShapeDtypeStruct
    out_shape = [S((SUBLANES, 128), F32), S((CONV_ROWS, D_CONV), F32)]
    for n in names:
        out_shape += [S(small[n][0].shape, F32)] * 4
    (res,), _ = _staged_call(
        "finish_small", [([gws8, gbs8, gcw8, vmix8, vmlp8, vout8, *flat], out_shape, compute)], 40)
    upd = {n: res[2 + 4 * k:6 + 4 * k] for k, n in enumerate(names)}
    return res[0], res[1], upd


def _adamw_plain(name, g, w, m, v):
    def compute(ins, outs):
        g_ref, w_ref, m_ref, v_ref = ins
        for o, val in zip(outs, _adamw(w_ref[...], g_ref[...], m_ref[...], v_ref[...])):
            o[...] = val

    (res,), _ = _staged_call(name, [([g, w, m, v], [jax.ShapeDtypeStruct(w.shape, F32)] * 3, compute)], 16)
    return res


TOKEN_TILE_FWD_MIX = 512
TOKEN_TILE_BWD_MIX = 256
TOKEN_TILE_FWD_MLP = 512
TOKEN_TILE_BWD_MLP = 512
TOKEN_TILE_WGRAD = 1024


def kernel(x, w_in, sgu_ln_g, sgu_ln_b, w_s, b_s, conv_w, conv_b, conv_ln_g, conv_ln_b, w_out, ln1_g, ln1_b, w_gate, w_up, w_down, ln2_g, ln2_b, loss_target, m_w_in, m_sgu_ln_g, m_sgu_ln_b, m_w_s, m_b_s, m_conv_w, m_conv_b, m_conv_ln_g, m_conv_ln_b, m_w_out, m_ln1_g, m_ln1_b, m_w_gate, m_w_up, m_w_down, m_ln2_g, m_ln2_b, v_w_in, v_sgu_ln_g, v_sgu_ln_b, v_w_s, v_b_s, v_conv_w, v_conv_b, v_conv_ln_g, v_conv_ln_b, v_w_out, v_ln1_g, v_ln1_b, v_w_gate, v_w_up, v_w_down, v_ln2_g, v_ln2_b):
    xs = x[0]
    tgt = loss_target[0]

    (win_b, wout_b, wgt_b, wut_b, wd_b, cw_b, wcat, wcatt, bs_full) = _prep_weights(
        w_in[0], w_out[0], w_gate[0].T, w_up[0].T, w_down[0], conv_w[0], w_s[0], b_s[0])
    win_g, wout_g, cw_g = _exchange("gather_mix_weights", [], [win_b, wout_b, cw_b])
    win_g = win_g.reshape(2 * D_MODEL, D_MODEL)
    wout_g = wout_g.reshape(D_MODEL, D_MODEL)
    cw = jnp.transpose(cw_g[:, :, :D_CONV // N_DEV], (1, 0, 2)).reshape(CONV_ROWS, D_CONV)

    (proj, ycat, n1, rstd1, phi, y_conv), (wgt_g, wut_g, wd_g) = _fwd_mix(
        xs, win_g, wout_g, sgu_ln_g, sgu_ln_b, wcat, bs_full, cw, conv_b, conv_ln_g, conv_ln_b, TOKEN_TILE_FWD_MIX,
        _Exchange([], [wgt_b, wut_b, wd_b]))
    wgt_g = wgt_g.reshape(D_FF, D_MODEL)
    wut_g = wut_g.reshape(D_FF, D_MODEL)
    wd_g = wd_g.reshape(D_FF, D_MODEL)
    gate, up, hh, x1b, dr2, vout = _fwd_mlp(n1, tgt, ln1_g, ln1_b, ln2_g, ln2_b, wgt_g, wut_g, wd_g, TOKEN_TILE_FWD_MLP)

    dgate, dup, dr1, vmlp = _bwd_mlp(dr2, gate, up, n1, rstd1, ln1_g, wgt_g, wut_g, wd_g, TOKEN_TILE_BWD_MLP)
    tk = TOKEN_TILE_WGRAD
    g_wgt = _wgrad("wgrad_gate", dgate, x1b, N_DEV, tk)
    g_wut = _wgrad("wgrad_up", dup, x1b, N_DEV, tk)
    g_wd = _wgrad("wgrad_down", hh, dr2, N_DEV, tk)
    g_wout = _wgrad("wgrad_out", ycat, dr1, N_DEV, tk)
    (gx, dproj, gws, gbs, gcw, vmix), (r_wgt, r_wut, r_wd, r_wout) = _bwd_mix(
        dr1, proj, phi, y_conv, win_g, wout_g, sgu_ln_g, sgu_ln_b, wcat, wcatt, bs_full, cw, conv_ln_g, conv_ln_b,
        TOKEN_TILE_BWD_MIX, _Exchange([g_wgt, g_wut, g_wd, g_wout], []))
    half = D_MODEL // 2
    g_win_a, (gws8, gbs8, gcw8, vmix8, vmlp8, vout8) = _wgrad(
        "wgrad_in_a", dproj, xs, N_DEV, tk, _Exchange([], [gws, gbs, gcw, vmix, vmlp, vout]), b_cols=(0, half))
    g_win_b, (r_win_a,) = _wgrad("wgrad_in_b", dproj, xs, N_DEV, tk, _Exchange([g_win_a], []), b_cols=(1, half))
    (r_win_b,) = _exchange("exchange_grad_in", [g_win_b], [])
    (u_gate, u_up, u_down), _ = _staged_call(
        "adamw_mlp",
        [_adamw_shard_group([r_wgt], w_gate[0].T, m_w_gate[0].T, v_w_gate[0].T, False),
         _adamw_shard_group([r_wut], w_up[0].T, m_w_up[0].T, v_w_up[0].T, False),
         _adamw_shard_group([r_wd], w_down[0], m_w_down[0], v_w_down[0], False)], 56)
    (u_in, u_out), _ = _staged_call(
        "adamw_mix",
        [_adamw_shard_group([r_win_a, r_win_b], w_in[0], m_w_in[0], v_w_in[0], True),
         _adamw_shard_group([r_wout], w_out[0], m_w_out[0], v_w_out[0], False)], 32)
    big = {"w_in": u_in, "w_out": u_out, "w_gate": u_gate, "w_up": u_up, "w_down": u_down}
    small_in = {
        "sgu_ln_g": (sgu_ln_g, m_sgu_ln_g, v_sgu_ln_g), "sgu_ln_b": (sgu_ln_b, m_sgu_ln_b, v_sgu_ln_b),
        "w_s": tuple(a.reshape(N_HEADS * CHUNK, CHUNK) for a in (w_s, m_w_s, v_w_s)),
        "b_s": (b_s[0], m_b_s[0], v_b_s[0]),
        "conv_b": (conv_b, m_conv_b, v_conv_b), "conv_ln_g": (conv_ln_g, m_conv_ln_g, v_conv_ln_g),
        "conv_ln_b": (conv_ln_b, m_conv_ln_b, v_conv_ln_b),
        "ln1_g": (ln1_g, m_ln1_g, v_ln1_g), "ln1_b": (ln1_b, m_ln1_b, v_ln1_b),
        "ln2_g": (ln2_g, m_ln2_g, v_ln2_g), "ln2_b": (ln2_b, m_ln2_b, v_ln2_b),
    }
    loss11, gcw_full, small = _finish_small(gws8, gbs8, gcw8, vmix8, vmlp8, vout8, small_in)

    me = 4 * lax.axis_index("x") + 2 * lax.axis_index("y") + lax.axis_index("c")
    g_cw = lax.dynamic_slice(gcw_full, (0, me * (D_CONV // N_DEV)), (CONV_WIDTH, D_CONV // N_DEV))
    d_cw, m_cw, v_cw = _adamw_plain("adamw_conv_w", g_cw, conv_w[0], m_conv_w[0], v_conv_w[0])

    shapes = {"w_s": w_s.shape, "b_s": b_s.shape}
    out = {}
    for n, r in big.items():
        out[n] = tuple((a.T if n in ("w_gate", "w_up") else a)[None] for a in r)
    for n, r in small.items():
        out[n] = tuple(a.reshape(shapes[n]) for a in r) if n in shapes else tuple(r)
    out["conv_w"] = tuple(a[None] for a in (g_cw, d_cw, m_cw, v_cw))

    order = ["w_in", "sgu_ln_g", "sgu_ln_b", "w_s", "b_s", "conv_w", "conv_b", "conv_ln_g", "conv_ln_b", "w_out",
             "ln1_g", "ln1_b", "w_gate", "w_up", "w_down", "ln2_g", "ln2_b"]
    loss = loss11[0, 0]
    return (loss, gx[None], *[out[n][0] for n in order], *[out[n][1] for n in order],
            *[out[n][2] for n in order], *[out[n][3] for n in order])
```

```python
import jax
import jax.numpy as jnp
from jax import lax
from jax.experimental import pallas as pl
from jax.experimental.pallas import tpu as pltpu

F32 = jnp.float32
BF16 = jnp.bfloat16

D_MODEL = 1024
D_SGU = 512
D_CONV = 512
N_HEADS = 8
CHUNK = 128
CONV_WIDTH = 31
CONV_ROWS = 32
HALO = 32
D_FF = 2816
N_DEV = 8
FF_SHARD = D_FF // N_DEV
ALPHA = (2.0 * 1) ** 0.25
LN_EPS = 1e-5
INV_SQRT2 = 0.7071067811865476
INV_SQRT_2PI = 0.3989422804014327

ADAM_LR = 0.001
ADAM_B1 = 0.9
ADAM_B2 = 0.999
ADAM_EPS = 1e-08
ADAM_WD = 0.01
ADAM_STEP = 10

MXU_COLS = 256
SUBLANES = 8
CONV_ROW_BLOCK = 32
WGRAD_ROW_BLOCK = 32
SHIFT_ROWS = HALO - SUBLANES
MIB = 1024 * 1024

HBM = pl.BlockSpec(memory_space=pltpu.HBM)
ANY = pl.BlockSpec(memory_space=pl.ANY)
MESH = pl.DeviceIdType.MESH


def _params(vmem_mib, grid_dims=0):
    kw = dict(vmem_limit_bytes=vmem_mib * MIB)
    if grid_dims:
        kw["dimension_semantics"] = ("arbitrary",) * grid_dims
    return pltpu.CompilerParams(**kw)


def _full(shape):
    return pl.BlockSpec(shape, lambda i: (0,) * len(shape))


def _dot(a, b):
    return jnp.dot(a, b, preferred_element_type=F32)


def _dot_nt(a, b):
    return lax.dot_general(a, b, (((1,), (1,)), ((), ())), preferred_element_type=F32)


def _dot_tn(a, b):
    return lax.dot_general(a, b, (((0,), (0,)), ((), ())), preferred_element_type=F32)


def _normal_cdf(x):
    return 0.5 * (1.0 + lax.erf(x * INV_SQRT2))


def _gelu_grad(x, cdf):
    return cdf + x * jnp.exp(-0.5 * x * x) * INV_SQRT_2PI


def _ln_fwd(v):
    mu = jnp.mean(v, axis=-1, keepdims=True)
    d = v - mu
    var = jnp.mean(d * d, axis=-1, keepdims=True)
    rstd = lax.rsqrt(var + LN_EPS)
    return d * rstd, rstd


def _ln_bwd(dyhat, yhat, rstd):
    m1 = jnp.mean(dyhat, axis=-1, keepdims=True)
    m2 = jnp.mean(dyhat * yhat, axis=-1, keepdims=True)
    return rstd * (dyhat - m1 - yhat * m2)


def _colsum(v):
    return jnp.sum(v, axis=0, keepdims=True)


def _head_pair_stack(v, lo):
    return jnp.concatenate([jnp.where(lo, v, 0.0), jnp.where(lo, 0.0, v)], axis=0).astype(BF16)


def _lo_mask():
    return lax.broadcasted_iota(jnp.int32, (CHUNK, CHUNK), 1) < (CHUNK // 2)


def _head_selector():
    head = lax.broadcasted_iota(jnp.int32, (N_HEADS, D_SGU), 0)
    lane = lax.broadcasted_iota(jnp.int32, (N_HEADS, D_SGU), 1)
    width = D_SGU // N_HEADS
    return ((lane >= head * width) & (lane < (head + 1) * width)).astype(F32)


def _shifted_copies(pad_ref, sh_ref, rows):
    for r in range(1, SUBLANES):
        sh_ref[r - 1, 0:rows, :] = pad_ref[pl.ds(r, rows), :]


def _tap_groups(offset_of_tap):
    groups = {}
    for k in range(CONV_WIDTH):
        o = offset_of_tap(k)
        groups.setdefault(o % SUBLANES, []).append((k, o // SUBLANES))
    return groups


def _tap_window(pad_ref, sh_ref, r, taps, row0, rows):
    q0 = min(q for _, q in taps)
    q1 = max(q for _, q in taps)
    src = pad_ref if r == 0 else sh_ref.at[r - 1]
    win = src[pl.ds(row0 + SUBLANES * q0, SUBLANES * (q1 - q0) + rows), :]
    return win, [(k, SUBLANES * (q - q0)) for k, q in taps]


def _causal_conv(pad_ref, sh_ref, w_ref, out_ref, rows, offset_of_tap, bias=None):
    groups = _tap_groups(offset_of_tap)

    def block(b, carry):
        row0 = pl.multiple_of(b * CONV_ROW_BLOCK, CONV_ROW_BLOCK)
        if bias is None:
            acc = jnp.zeros((CONV_ROW_BLOCK, D_CONV), F32)
        else:
            acc = jnp.broadcast_to(bias, (CONV_ROW_BLOCK, D_CONV))
        for r, taps in groups.items():
            win, starts = _tap_window(pad_ref, sh_ref, r, taps, row0, CONV_ROW_BLOCK)
            for k, s in starts:
                acc = acc + w_ref[k:k + 1, :] * win[s:s + CONV_ROW_BLOCK, :]
        out_ref[pl.ds(row0, CONV_ROW_BLOCK), :] = acc
        return carry

    lax.fori_loop(0, rows // CONV_ROW_BLOCK, block, 0)


def _conv_weight_grad(dy_ref, pad_ref, sh_ref, acc_ref, rows, offset_of_tap):
    groups = _tap_groups(offset_of_tap)
    for r, taps in groups.items():

        def block(b, parts, r=r, taps=taps):
            row0 = pl.multiple_of(b * WGRAD_ROW_BLOCK, WGRAD_ROW_BLOCK)
            dyb = dy_ref[pl.ds(row0, WGRAD_ROW_BLOCK), :]
            win, starts = _tap_window(pad_ref, sh_ref, r, taps, row0, WGRAD_ROW_BLOCK)
            out = []
            for part, (_, s) in zip(parts, starts):
                pr = dyb * win[s:s + WGRAD_ROW_BLOCK, :]
                out.append(part + pr.reshape(WGRAD_ROW_BLOCK // SUBLANES, SUBLANES, D_CONV).sum(axis=0))
            return tuple(out)

        zeros = tuple(jnp.zeros((SUBLANES, D_CONV), F32) for _ in taps)
        parts = lax.fori_loop(0, rows // WGRAD_ROW_BLOCK, block, zeros)
        for part, (k, _) in zip(parts, taps):
            acc_ref[k] += part


def _prep_weights(w_in, w_out, w_gate_t, w_up_t, w_down, conv_w, w_s, b_s):
    def compute(ins, outs):
        win_ref, wout_ref, wgt_ref, wut_ref, wd_ref, cw_ref, ws_ref, bs_ref = ins
        win_o, wout_o, wgt_o, wut_o, wd_o, cw_o, wcat_o, wcatt_o, bsf_o = outs
        win_o[...] = win_ref[...].T.astype(BF16)
        wout_o[...] = wout_ref[...].astype(BF16)
        wgt_o[...] = wgt_ref[...].astype(BF16)
        wut_o[...] = wut_ref[...].astype(BF16)
        wd_o[...] = wd_ref[...].astype(BF16)
        cw_o[...] = jnp.zeros(cw_o.shape, F32)
        cw_o[0:CONV_WIDTH, 0:D_CONV // N_DEV] = cw_ref[...]
        row = lax.broadcasted_iota(jnp.int32, (CHUNK, CHUNK), 0)
        col = lax.broadcasted_iota(jnp.int32, (CHUNK, CHUNK), 1)
        causal = row >= col
        for h in range(N_HEADS):
            w = jnp.where(causal, ws_ref[h], 0.0)
            p, half = h // 2, (h % 2) * CHUNK
            wcat_o[p, :, half:half + CHUNK] = w.astype(BF16)
            wcatt_o[p, :, half:half + CHUNK] = w.T.astype(BF16)
        bsf_o[...] = lax.dot_general(bs_ref[...], _head_selector(), (((0,), (0,)), ((), ())),
                                     preferred_element_type=F32, precision=lax.Precision.HIGHEST)

    S = jax.ShapeDtypeStruct
    out_shapes = [S((256, D_MODEL), BF16), S((128, D_MODEL), BF16), S((FF_SHARD, D_MODEL), BF16),
                  S((FF_SHARD, D_MODEL), BF16), S((FF_SHARD, D_MODEL), BF16), S((CONV_ROWS, 128), F32),
                  S((4, CHUNK, 2 * CHUNK), BF16), S((4, CHUNK, 2 * CHUNK), BF16), S((CHUNK, D_SGU), F32)]
    (res,), _ = _staged_call(
        "prep_weights", [([w_in, w_out, w_gate_t, w_up_t, w_down, conv_w, w_s, b_s], out_shapes, compute)], 32)
    return res


def _mesh_position():
    x, y, c = lax.axis_index("x"), lax.axis_index("y"), lax.axis_index("c")
    return x, y, c


def _peers(x, y, c):
    out = []
    for k in range(1, N_DEV):
        px = 1 - x if (k >> 2) & 1 else x
        py = 1 - y if (k >> 1) & 1 else y
        pc = 1 - c if k & 1 else c
        out.append(((px, py, pc), 4 * px + 2 * py + pc))
    return out


class _Exchange:
    def __init__(self, scatter, gather):
        self.arrays = list(scatter) + list(gather)
        self.n_sc = len(scatter)
        self.n = len(self.arrays)
        self.out_shape = [jax.ShapeDtypeStruct(a.shape if k < self.n_sc else (N_DEV,) + a.shape, a.dtype)
                          for k, a in enumerate(self.arrays)]
        n_remote = self.n * (N_DEV - 1)
        self.scratch = [pltpu.SemaphoreType.DMA((n_remote,)), pltpu.SemaphoreType.DMA((n_remote,)),
                        pltpu.SemaphoreType.DMA((self.n,))] if self.n else []

    def _copies(self, src, dst, sems):
        send_sems, recv_sems, local_sems = sems
        x, y, c = _mesh_position()
        me = 4 * x + 2 * y + c
        locals_, first, arrivals, passed, last = [], [], [], [], []

        def remote(a, k, src_ref, slot, to):
            s = a * (N_DEV - 1) + k
            return pltpu.make_async_remote_copy(src_ref=src_ref, dst_ref=dst[a].at[slot], send_sem=send_sems.at[s],
                                                recv_sem=recv_sems.at[s], device_id=to, device_id_type=MESH)

        for a in range(self.n):
            if a < self.n_sc:
                locals_.append(pltpu.make_async_copy(src[a].at[me], dst[a].at[me], local_sems.at[a]))
                for k, (peer, pid) in enumerate(_peers(x, y, c)):
                    first.append(remote(a, k, src[a].at[pid], me, peer))
                    last.append(remote(a, k, src[a].at[pid], pid, peer))
                continue
            locals_.append(pltpu.make_async_copy(src[a], dst[a].at[me], local_sems.at[a]))
            sibling, sib_id = (x, y, 1 - c), 4 * x + 2 * y + (1 - c)
            chips = [(1 - x, y), (x, 1 - y), (1 - x, 1 - y)]
            first.append(remote(a, 0, src[a], me, sibling))
            last.append(remote(a, 0, src[a], sib_id, sibling))
            for j, (px, py) in enumerate(chips):
                same, other = 4 * px + 2 * py + c, 4 * px + 2 * py + (1 - c)
                first.append(remote(a, 1 + j, src[a], me, (px, py, c)))
                arrivals.append(remote(a, 1 + j, src[a], same, (px, py, c)))
                passed.append(remote(a, 4 + j, dst[a].at[same], same, sibling))
                last.append(remote(a, 4 + j, dst[a].at[other], other, sibling))
        return locals_, first, arrivals, passed, last

    def start(self, src, dst, sems):
        if not self.n:
            return
        locals_, first, _, _, _ = self._copies(src, dst, sems)
        for cp in locals_ + first:
            cp.start()

    def forward(self, src, dst, sems):
        if self.n == self.n_sc:
            return
        _, _, arrivals, passed, _ = self._copies(src, dst, sems)
        for arrived, cp in zip(arrivals, passed):
            arrived.wait_recv()
            cp.start()

    def wait(self, src, dst, sems):
        if not self.n:
            return
        locals_, first, _, passed, last = self._copies(src, dst, sems)
        for cp in last:
            cp.wait_recv()
        for cp in first + passed:
            cp.wait_send()
        for cp in locals_:
            cp.wait()


def _exchange(name, scatter, gather):
    ex = _Exchange(scatter, gather)
    n = ex.n

    def body(*refs):
        src, dst, sems = refs[:n], refs[n:2 * n], refs[2 * n:]
        ex.start(src, dst, sems)
        ex.forward(src, dst, sems)
        ex.wait(src, dst, sems)

    return pl.pallas_call(
        body, name=name, out_shape=tuple(ex.out_shape), in_specs=[ANY] * n, out_specs=(ANY,) * n,
        scratch_shapes=ex.scratch,
    )(*ex.arrays)


def _forward_step(n_steps):
    return (11 * n_steps) // 16


def _hosted(ex, refs, n_in, n_out):
    ins, ex_src = refs[:n_in], refs[n_in:n_in + ex.n]
    rest = refs[n_in + ex.n:]
    outs, ex_dst = rest[:n_out], rest[n_out:n_out + ex.n]
    rest = rest[n_out + ex.n:]
    n_own = len(rest) - len(ex.scratch)
    return ins, outs, rest[:n_own], (ex_src, ex_dst, rest[n_own:])


def _fwd_mix(x, win_g, wout_g, sgu_g, sgu_b, wcat, bs_full, cw, cb, cg, cbeta, tm, ex):
    T = x.shape[0]
    nt = T // tm

    def body(*refs):
        ins, outs, scratch, ex_refs = _hosted(ex, refs, 11, 6)
        x_ref, win_ref, wout_ref, sg_ref, sb_ref, wcat_ref, bs_ref, cw_ref, cb_ref, cg_ref, cbeta_ref = ins
        proj_ref, ycat_ref, n1_ref, rstd1_ref, phi_ref, y_ref = outs
        hpad, hshift = scratch
        i = pl.program_id(0)

        @pl.when(i == 0)
        def _():
            ex.start(*ex_refs)

        xf = x_ref[...]
        xb = xf.astype(BF16)
        proj_ref[...] = _dot_nt(xb, win_ref[...])
        phi_ref[...] = _normal_cdf(proj_ref[:, 0:2 * D_SGU])
        u = proj_ref[:, 0:D_SGU] * phi_ref[:, 0:D_SGU]
        v = proj_ref[:, D_SGU:2 * D_SGU] * phi_ref[:, D_SGU:2 * D_SGU]
        vhat, _ = _ln_fwd(v)
        vn = vhat * sg_ref[...] + sb_ref[...]
        lo = _lo_mask()
        for c in range(tm // CHUNK):
            rows = slice(CHUNK * c, CHUNK * (c + 1))
            for p in range(4):
                lanes = slice(CHUNK * p, CHUNK * (p + 1))
                mixed = _dot(wcat_ref[p], _head_pair_stack(vn[rows, lanes], lo)) + bs_ref[:, lanes]
                ycat_ref[rows, lanes] = (u[rows, lanes] * mixed).astype(BF16)
        base = 2 * D_SGU
        a = proj_ref[:, base:base + D_CONV]
        g = proj_ref[:, base + D_CONV:base + 2 * D_CONV]

        @pl.when(i == 0)
        def _():
            hpad[0:HALO, :] = jnp.zeros((HALO, D_CONV), F32)

        hpad[HALO:HALO + tm, :] = a * jax.nn.sigmoid(g)
        _shifted_copies(hpad, hshift, tm + SHIFT_ROWS)
        _causal_conv(hpad, hshift, cw_ref, y_ref, tm, lambda k: HALO - (CONV_WIDTH - 1) + k, bias=cb_ref[...])
        hpad[0:HALO, :] = hpad[tm:tm + HALO, :]
        yhat, _ = _ln_fwd(y_ref[...])
        yn = yhat * cg_ref[...] + cbeta_ref[...]
        ycat_ref[:, D_SGU:D_SGU + D_CONV] = (yn * jax.nn.sigmoid(yn)).astype(BF16)
        r1 = ALPHA * xf + _dot(ycat_ref[...], wout_ref[...])
        n1, rstd1 = _ln_fwd(r1)
        n1_ref[...] = n1
        rstd1_ref[...] = rstd1

        @pl.when(i == _forward_step(nt))
        def _():
            ex.forward(*ex_refs)

        @pl.when(i == nt - 1)
        def _():
            ex.wait(*ex_refs)

    S = jax.ShapeDtypeStruct
    row = lambda w: pl.BlockSpec((tm, w), lambda i: (i, 0))
    res = pl.pallas_call(
        body, name="fwd_mix", grid=(nt,),
        in_specs=[row(D_MODEL), _full(win_g.shape), _full(wout_g.shape), _full(sgu_g.shape), _full(sgu_b.shape),
                  _full(wcat.shape), _full(bs_full.shape), _full(cw.shape), _full(cb.shape), _full(cg.shape),
                  _full(cbeta.shape)] + [ANY] * ex.n,
        out_specs=(row(2 * D_MODEL), row(D_MODEL), row(D_MODEL), row(1), row(2 * D_SGU), row(D_CONV)) + (ANY,) * ex.n,
        out_shape=(S((T, 2 * D_MODEL), F32), S((T, D_MODEL), BF16), S((T, D_MODEL), F32), S((T, 1), F32),
                   S((T, 2 * D_SGU), F32), S((T, D_CONV), F32), *ex.out_shape),
        scratch_shapes=[pltpu.VMEM((tm + HALO, D_CONV), F32),
                        pltpu.VMEM((SUBLANES - 1, tm + SHIFT_ROWS, D_CONV), F32)] + ex.scratch,
        compiler_params=_params(56, 1),
    )(x, win_g, wout_g, sgu_g, sgu_b, wcat, bs_full, cw, cb, cg, cbeta, *ex.arrays)
    return res[:6], res[6:]


def _load_resident(pairs, sems):
    cps = [pltpu.make_async_copy(s, d, sems.at[k]) for k, (s, d) in enumerate(pairs)]
    for cp in cps:
        cp.start()
    for cp in cps:
        cp.wait()


def _fwd_mlp(n1, tgt, l1g, l1b, l2g, l2b, wgt, wut, wd, tm):
    T = n1.shape[0]
    nt = T // tm
    nf = D_FF // MXU_COLS

    def body(n1_ref, tgt_ref, l1g_ref, l1b_ref, l2g_ref, l2b_ref, wg_hbm, wu_hbm, wd_hbm,
             gate_ref, up_ref, hh_ref, x1b_ref, dr2_ref, stat_ref, wg_s, wu_s, wd_s, sems):
        i = pl.program_id(0)

        @pl.when(i == 0)
        def _():
            _load_resident([(wg_hbm, wg_s), (wu_hbm, wu_s), (wd_hbm, wd_s)], sems)
            stat_ref[...] = jnp.zeros(stat_ref.shape, F32)

        x1 = n1_ref[...] * l1g_ref[...] + l1b_ref[...]
        x1b = x1.astype(BF16)
        x1b_ref[...] = x1b
        for f in range(nf):
            cols = slice(MXU_COLS * f, MXU_COLS * (f + 1))
            gt = _dot_nt(x1b, wg_s[cols, :])
            ut = _dot_nt(x1b, wu_s[cols, :])
            gate_ref[:, cols] = gt.astype(BF16)
            up_ref[:, cols] = ut.astype(BF16)
            hh_ref[:, cols] = (gt * jax.nn.sigmoid(gt) * ut).astype(BF16)
        r2 = ALPHA * x1 + _dot(hh_ref[...], wd_s[...])
        n2, rstd2 = _ln_fwd(r2)
        x2 = n2 * l2g_ref[...] + l2b_ref[...]
        diff = x2 - tgt_ref[...]
        dx2 = diff * (1.0 / D_MODEL)
        stat_ref[0:1, :] += _colsum(diff * diff)
        stat_ref[1:2, :] += _colsum(dx2 * n2)
        stat_ref[2:3, :] += _colsum(dx2)
        dr2_ref[...] = _ln_bwd(dx2 * l2g_ref[...], n2, rstd2)

    S = jax.ShapeDtypeStruct
    row = lambda w: pl.BlockSpec((tm, w), lambda i: (i, 0))
    vec = _full((1, D_MODEL))
    return pl.pallas_call(
        body, name="fwd_mlp", grid=(nt,),
        in_specs=[row(D_MODEL), row(D_MODEL), vec, vec, vec, vec, ANY, ANY, ANY],
        out_specs=(row(D_FF), row(D_FF), row(D_FF), row(D_MODEL), row(D_MODEL), _full((8, D_MODEL))),
        out_shape=(S((T, D_FF), BF16), S((T, D_FF), BF16), S((T, D_FF), BF16), S((T, D_MODEL), BF16),
                   S((T, D_MODEL), F32), S((8, D_MODEL), F32)),
        scratch_shapes=[pltpu.VMEM((D_FF, D_MODEL), BF16)] * 3 + [pltpu.SemaphoreType.DMA((3,))],
        compiler_params=_params(56, 1),
    )(n1, tgt, l1g, l1b, l2g, l2b, wgt, wut, wd)


def _bwd_mlp(dr2, gate, up, n1, rstd1, l1g, wgt, wut, wd, tm):
    T = n1.shape[0]
    nt = T // tm
    nf = D_FF // MXU_COLS

    def body(dr2_ref, gate_ref, up_ref, n1_ref, rstd1_ref, l1g_ref, wg_hbm, wu_hbm, wd_hbm,
             dgate_ref, dup_ref, dr1_ref, stat_ref, wg_s, wu_s, wd_s, sems):
        i = pl.program_id(0)

        @pl.when(i == 0)
        def _():
            _load_resident([(wg_hbm, wg_s), (wu_hbm, wu_s), (wd_hbm, wd_s)], sems)
            stat_ref[...] = jnp.zeros(stat_ref.shape, F32)

        dr2 = dr2_ref[...]
        dr2b = dr2.astype(BF16)
        for f in range(nf):
            cols = slice(MXU_COLS * f, MXU_COLS * (f + 1))
            dhh = _dot_nt(dr2b, wd_s[cols, :])
            gt = gate_ref[:, cols].astype(F32)
            ut = up_ref[:, cols].astype(F32)
            sg = jax.nn.sigmoid(gt)
            dgate_ref[:, cols] = (dhh * ut * (sg * (1.0 + gt * (1.0 - sg)))).astype(BF16)
            dup_ref[:, cols] = (dhh * (gt * sg)).astype(BF16)
        dx1 = ALPHA * dr2 + _dot(dgate_ref[...], wg_s[...]) + _dot(dup_ref[...], wu_s[...])
        n1 = n1_ref[...]
        stat_ref[0:1, :] += _colsum(dx1 * n1)
        stat_ref[1:2, :] += _colsum(dx1)
        dr1_ref[...] = _ln_bwd(dx1 * l1g_ref[...], n1, rstd1_ref[...])

    S = jax.ShapeDtypeStruct
    row = lambda w: pl.BlockSpec((tm, w), lambda i: (i, 0))
    return pl.pallas_call(
        body, name="bwd_mlp", grid=(nt,),
        in_specs=[row(D_MODEL), row(D_FF), row(D_FF), row(D_MODEL), row(1), _full((1, D_MODEL)), ANY, ANY, ANY],
        out_specs=(row(D_FF), row(D_FF), row(D_MODEL), _full((8, D_MODEL))),
        out_shape=(S((T, D_FF), BF16), S((T, D_FF), BF16), S((T, D_MODEL), F32), S((8, D_MODEL), F32)),
        scratch_shapes=[pltpu.VMEM((D_FF, D_MODEL), BF16)] * 3 + [pltpu.SemaphoreType.DMA((3,))],
        compiler_params=_params(56, 1),
    )(dr2, gate, up, n1, rstd1, l1g, wgt, wut, wd)


def _bwd_mix(dr1, proj, phi, y, win_g, wout_g, sgu_g, sgu_b, wcat, wcatt, bs_full, cw, cg, cbeta, tm, ex):
    T = dr1.shape[0]
    nt = T // tm
    halo_blocks = tm // HALO

    def body(*refs):
        ins, outs, scratch, ex_refs = _hosted(ex, refs, 15, 6)
        (dr1_ref, proj_ref, halo_ref, phi_ref, y_ref, win_ref, wout_ref, sg_ref, sb_ref, wcat_ref, wcatt_ref, bs_ref,
         cw_ref, cg_ref, cbeta_ref) = ins
        gx_ref, dproj_ref, gws_out, gbs_out, gcw_ref, vec_ref = outs
        hpad, hshift, dypad, dyshift, dhbuf, dubuf, dvnbuf, gcw_acc, gws_ref, gbs_ref = scratch
        i = pl.program_id(0)
        tile = nt - 1 - i

        @pl.when(i == 0)
        def _():
            ex.start(*ex_refs)
            gws_ref[...] = jnp.zeros(gws_ref.shape, F32)
            gbs_ref[...] = jnp.zeros(gbs_ref.shape, F32)
            gcw_ref[...] = jnp.zeros(gcw_ref.shape, F32)
            vec_ref[...] = jnp.zeros(vec_ref.shape, F32)
            gcw_acc[...] = jnp.zeros(gcw_acc.shape, F32)
            dypad[tm:tm + HALO, :] = jnp.zeros((HALO, D_CONV), F32)

        dr1 = dr1_ref[...]
        dycat = _dot_nt(dr1.astype(BF16), wout_ref[...])
        pu = proj_ref[:, 0:D_SGU]
        pv = proj_ref[:, D_SGU:2 * D_SGU]
        u = pu * phi_ref[:, 0:D_SGU]
        vhat, rstd_v = _ln_fwd(pv * phi_ref[:, D_SGU:2 * D_SGU])
        vn = vhat * sg_ref[...] + sb_ref[...]
        lo = _lo_mask()
        for c in range(tm // CHUNK):
            rows = slice(CHUNK * c, CHUNK * (c + 1))
            for p in range(4):
                lanes = slice(CHUNK * p, CHUNK * (p + 1))
                vstack = _head_pair_stack(vn[rows, lanes], lo)
                mixed = _dot(wcat_ref[p], vstack) + bs_ref[:, lanes]
                d_a = dycat[rows, lanes]
                dubuf[rows, lanes] = d_a * mixed
                dm = d_a * u[rows, lanes]
                gbs_ref[:, lanes] += dm
                dstack = _head_pair_stack(dm, lo)
                gws_ref[2 * CHUNK * p:2 * CHUNK * (p + 1), :] += _dot_nt(dstack, vn[rows, lanes].astype(BF16))
                dvnbuf[rows, lanes] = _dot(wcatt_ref[p], dstack)
        dvn = dvnbuf[...]
        vec_ref[0:1, :] += _colsum(dvn * vhat)
        vec_ref[1:2, :] += _colsum(dvn)
        dv = _ln_bwd(dvn * sg_ref[...], vhat, rstd_v)
        dproj_ref[:, 0:D_SGU] = (dubuf[...] * _gelu_grad(pu, phi_ref[:, 0:D_SGU])).astype(BF16)
        dproj_ref[:, D_SGU:2 * D_SGU] = (dv * _gelu_grad(pv, phi_ref[:, D_SGU:2 * D_SGU])).astype(BF16)
        base = 2 * D_SGU
        a = proj_ref[:, base:base + D_CONV]
        sgm = jax.nn.sigmoid(proj_ref[:, base + D_CONV:base + 2 * D_CONV])
        h_before = halo_ref[:, 0:D_CONV] * jax.nn.sigmoid(halo_ref[:, D_CONV:2 * D_CONV])
        hpad[0:HALO, :] = jnp.where(tile > 0, h_before, 0.0)
        hpad[HALO:HALO + tm, :] = a * sgm
        _shifted_copies(hpad, hshift, tm + SHIFT_ROWS)
        h_offset = lambda k: HALO - (CONV_WIDTH - 1) + k
        yhat, rstd_y = _ln_fwd(y_ref[...])
        yn = yhat * cg_ref[...] + cbeta_ref[...]
        s = jax.nn.sigmoid(yn)
        dyn = dycat[:, D_SGU:D_SGU + D_CONV] * (s * (1.0 + yn * (1.0 - s)))
        vec_ref[3:4, :] += _colsum(dyn * yhat)
        vec_ref[4:5, :] += _colsum(dyn)
        dy = _ln_bwd(dyn * cg_ref[...], yhat, rstd_y)
        vec_ref[2:3, :] += _colsum(dy)
        dypad[0:tm, :] = dy
        _shifted_copies(dypad, dyshift, tm + SHIFT_ROWS)
        _causal_conv(dypad, dyshift, cw_ref, dhbuf, tm, lambda k: (CONV_WIDTH - 1) - k)
        _conv_weight_grad(dypad, hpad, hshift, gcw_acc, tm, h_offset)
        dypad[tm:tm + HALO, :] = dypad[0:HALO, :]
        dh = dhbuf[...]
        dproj_ref[:, base:base + D_CONV] = (dh * sgm).astype(BF16)
        dproj_ref[:, base + D_CONV:base + 2 * D_CONV] = (dh * a * sgm * (1.0 - sgm)).astype(BF16)
        gx_ref[...] = ALPHA * dr1 + _dot(dproj_ref[...], win_ref[...])

        @pl.when(i == _forward_step(nt))
        def _():
            ex.forward(*ex_refs)

        @pl.when(i == nt - 1)
        def _():
            gcw_ref[...] = gcw_acc[...].sum(axis=1)
            gws_out[...] = gws_ref[...].astype(BF16)
            gbs_out[...] = lax.dot_general(_head_selector(), gbs_ref[...], (((1,), (1,)), ((), ())),
                                           preferred_element_type=F32, precision=lax.Precision.HIGHEST)
            ex.wait(*ex_refs)

    S = jax.ShapeDtypeStruct
    row = lambda w: pl.BlockSpec((tm, w), lambda i: (nt - 1 - i, 0))
    halo = pl.BlockSpec((HALO, D_MODEL), lambda i: (jnp.maximum((nt - 1 - i) * halo_blocks - 1, 0), 1))
    res = pl.pallas_call(
        body, name="bwd_mix", grid=(nt,),
        in_specs=[row(D_MODEL), row(2 * D_MODEL), halo, row(2 * D_SGU), row(D_CONV), _full(win_g.shape),
                  _full(wout_g.shape), _full(sgu_g.shape), _full(sgu_b.shape), _full(wcat.shape), _full(wcatt.shape),
                  _full(bs_full.shape), _full(cw.shape), _full(cg.shape), _full(cbeta.shape)]
        + [ANY] * ex.n,
        out_specs=(row(D_MODEL), row(2 * D_MODEL), _full((N_HEADS * CHUNK, CHUNK)), _full((N_HEADS, CHUNK)),
                   _full((CONV_ROWS, D_CONV)), _full((8, D_CONV))) + (ANY,) * ex.n,
        out_shape=(S((T, D_MODEL), F32), S((T, 2 * D_MODEL), BF16), S((N_HEADS * CHUNK, CHUNK), BF16),
                   S((N_HEADS, CHUNK), F32), S((CONV_ROWS, D_CONV), F32), S((8, D_CONV), F32), *ex.out_shape),
        scratch_shapes=[pltpu.VMEM((tm + HALO, D_CONV), F32), pltpu.VMEM((SUBLANES - 1, tm + SHIFT_ROWS, D_CONV), F32),
                        pltpu.VMEM((tm + HALO, D_CONV), F32), pltpu.VMEM((SUBLANES - 1, tm + SHIFT_ROWS, D_CONV), F32),
                        pltpu.VMEM((tm, D_CONV), F32), pltpu.VMEM((tm, D_SGU), F32),
                        pltpu.VMEM((tm, D_SGU), F32), pltpu.VMEM((CONV_ROWS, 8, D_CONV), F32),
                        pltpu.VMEM((N_HEADS * CHUNK, CHUNK), F32), pltpu.VMEM((CHUNK, D_SGU), F32)] + ex.scratch,
        compiler_params=_params(56, 1),
    )(dr1, proj, proj, phi, y, win_g, wout_g, sgu_g, sgu_b, wcat, wcatt, bs_full, cw, cg, cbeta, *ex.arrays)
    return res[:6], res[6:]


def _wgrad(name, a, b, blocks, tk, ex=None, b_cols=None):
    T, M = a.shape
    col, N = b_cols or (0, b.shape[1])
    nk = T // tk
    out_shape = (blocks, M // blocks, N)
    ex = ex or _Exchange([], [])

    def body(*refs):
        (a_ref, b_ref), (o_ref,), (acc,), ex_refs = _hosted(ex, refs, 2, 1)
        i = pl.program_id(0)

        @pl.when(i == 0)
        def _():
            ex.start(*ex_refs)
            acc[...] = jnp.zeros(acc.shape, F32)

        acc[...] += _dot_tn(a_ref[...].astype(BF16), b_ref[...].astype(BF16))

        @pl.when(i == _forward_step(nk))
        def _():
            ex.forward(*ex_refs)

        @pl.when(i == nk - 1)
        def _():
            o_ref[...] = acc[...].astype(BF16)
            ex.wait(*ex_refs)

    res = pl.pallas_call(
        body, name=name, grid=(nk,),
        in_specs=[pl.BlockSpec((tk, M), lambda i: (i, 0)), pl.BlockSpec((tk, N), lambda i: (i, col))] + [ANY] * ex.n,
        out_specs=(_full((M, N)),) + (ANY,) * ex.n,
        out_shape=(jax.ShapeDtypeStruct((M, N), BF16), *ex.out_shape),
        scratch_shapes=[pltpu.VMEM((M, N), F32)] + ex.scratch,
        compiler_params=_params(56, 1),
    )(a, b, *ex.arrays)
    g = res[0].reshape(out_shape)
    return (g, res[1:]) if ex.n else g


def _adamw(w, g, m, v):
    m2 = ADAM_B1 * m + (1.0 - ADAM_B1) * g
    v2 = ADAM_B2 * v + (1.0 - ADAM_B2) * (g * g)
    m_hat = m2 / (1.0 - ADAM_B1 ** ADAM_STEP)
    v_hat = v2 / (1.0 - ADAM_B2 ** ADAM_STEP)
    delta = -ADAM_LR * (m_hat / (jnp.sqrt(v_hat) + ADAM_EPS) + ADAM_WD * w)
    return delta, m2, v2


def _sum_partials(r_ref):
    g = r_ref[0].astype(F32)
    for s in range(1, N_DEV):
        g = g + r_ref[s].astype(F32)
    return g


def _staged_call(name, groups, vmem_mib, ex=None):
    ex = ex or _Exchange([], [])
    inputs = [a for ins, _, _ in groups for a in ins]
    out_shapes = [s for _, outs, _ in groups for s in outs]
    n_in, n_out = len(inputs), len(out_shapes)

    def body(*refs):
        ins, outs, scratch, ex_refs = _hosted(ex, refs, n_in, n_out)
        in_bufs, out_bufs, sems = scratch[:n_in], scratch[n_in:n_in + n_out], scratch[n_in + n_out]
        ex.start(*ex_refs)
        loads = [pltpu.make_async_copy(ins[k], in_bufs[k], sems.at[k]) for k in range(n_in)]
        stores = [pltpu.make_async_copy(out_bufs[k], outs[k], sems.at[n_in + k]) for k in range(n_out)]
        for cp in loads:
            cp.start()
        i0 = o0 = 0
        for g_ins, g_outs, compute in groups:
            i1, o1 = i0 + len(g_ins), o0 + len(g_outs)
            for cp in loads[i0:i1]:
                cp.wait()
            compute(in_bufs[i0:i1], out_bufs[o0:o1])
            for cp in stores[o0:o1]:
                cp.start()
            i0, o0 = i1, o1
        for cp in stores:
            cp.wait()
        ex.forward(*ex_refs)
        ex.wait(*ex_refs)

    scratch = ([pltpu.VMEM(a.shape, a.dtype) for a in inputs] + [pltpu.VMEM(s.shape, s.dtype) for s in out_shapes]
               + [pltpu.SemaphoreType.DMA((n_in + n_out,))] + ex.scratch)
    res = pl.pallas_call(
        body, name=name, out_shape=(*[pltpu.HBM(s.shape, s.dtype) for s in out_shapes], *ex.out_shape),
        in_specs=[HBM] * n_in + [ANY] * ex.n, out_specs=(HBM,) * n_out + (ANY,) * ex.n,
        scratch_shapes=scratch, compiler_params=_params(vmem_mib),
    )(*[pltpu.with_memory_space_constraint(a, pltpu.HBM) for a in inputs], *ex.arrays)
    per_group, o0 = [], 0
    for _, g_outs, _ in groups:
        per_group.append(list(res[o0:o0 + len(g_outs)]))
        o0 += len(g_outs)
    return per_group, res[n_out:]


def _adamw_shard_group(parts, w, m, v, transposed):
    n = len(parts)

    def compute(ins, outs):
        w_ref, m_ref, v_ref = ins[n:]
        lo = 0
        for r_ref in ins[:n]:
            g = _sum_partials(r_ref)
            cols = g.shape[1]
            if transposed:
                g, at = g.T, (slice(lo, lo + cols), slice(None))
            else:
                at = (slice(None), slice(lo, lo + cols))
            delta, m2, v2 = _adamw(w_ref[at], g, m_ref[at], v_ref[at])
            for o, val in zip(outs, (g, delta, m2, v2)):
                o[at] = val
            lo += cols

    return [*parts, w, m, v], [jax.ShapeDtypeStruct(w.shape, F32)] * 4, compute


def _finish_small(gws8, gbs8, gcw8, vmix8, vmlp8, vout8, small):
    names = ["sgu_ln_g", "sgu_ln_b", "w_s", "b_s", "conv_b", "conv_ln_g", "conv_ln_b", "ln1_g", "ln1_b", "ln2_g", "ln2_b"]
    flat = []
    for n in names:
        flat += list(small[n])

    def compute(ins, outs):
        gws_ref, gbs_ref, gcw_ref, vmix_ref, vmlp_ref, vout_ref = ins[:6]
        wmv = ins[6:]
        loss_o, gcw_o = outs[0], outs[1]
        outs = outs[2:]
        gws = _sum_partials(gws_ref)
        gbs = _sum_partials(gbs_ref)
        vmix = _sum_partials(vmix_ref)
        vmlp = _sum_partials(vmlp_ref)
        vout = _sum_partials(vout_ref)
        gcw_o[...] = _sum_partials(gcw_ref)
        loss = (0.5 / D_MODEL) * jnp.sum(vout[0:1, :], axis=1, keepdims=True)
        loss_o[...] = jnp.broadcast_to(loss, loss_o.shape)
        rows = lax.broadcasted_iota(jnp.int32, (N_HEADS * CHUNK, CHUNK), 0)
        cols = lax.broadcasted_iota(jnp.int32, (N_HEADS * CHUNK, CHUNK), 1)
        gws = jnp.where((rows & (CHUNK - 1)) >= cols, gws, 0.0)
        grads = {
            "sgu_ln_g": vmix[0:1, :], "sgu_ln_b": vmix[1:2, :], "w_s": gws, "b_s": gbs,
            "conv_b": vmix[2:3, :], "conv_ln_g": vmix[3:4, :], "conv_ln_b": vmix[4:5, :],
            "ln1_g": vmlp[0:1, :], "ln1_b": vmlp[1:2, :], "ln2_g": vout[1:2, :], "ln2_b": vout[2:3, :],
        }
        for k, n in enumerate(names):
            w_ref, m_ref, v_ref = wmv[3 * k:3 * k + 3]
            g = grads[n]
            delta, m2, v2 = _adamw(w_ref[...], g, m_ref[...], v_ref[...])
            outs[4 * k][...] = g
            outs[4 * k + 1][...] = delta
            outs[4 * k + 2][...] = m2
            outs[4 * k + 3][...] = v2

    S = jax.ShapeDtypeStruct
    out_shape = [S((SUBLANES, 128), F32), S((CONV_ROWS, D_CONV), F32)]
    for n in names:
        out_shape += [S(small[n][0].shape, F32)] * 4
    (res,), _ = _staged_call(
        "finish_small", [([gws8, gbs8, gcw8, vmix8, vmlp8, vout8, *flat], out_shape, compute)], 40)
    upd = {n: res[2 + 4 * k:6 + 4 * k] for k, n in enumerate(names)}
    return res[0], res[1], upd


def _adamw_plain(name, g, w, m, v):
    def compute(ins, outs):
        g_ref, w_ref, m_ref, v_ref = ins
        for o, val in zip(outs, _adamw(w_ref[...], g_ref[...], m_ref[...], v_ref[...])):
            o[...] = val

    (res,), _ = _staged_call(name, [([g, w, m, v], [jax.ShapeDtypeStruct(w.shape, F32)] * 3, compute)], 16)
    return res


TOKEN_TILE_FWD_MIX = 512
TOKEN_TILE_BWD_MIX = 256
TOKEN_TILE_FWD_MLP = 512
TOKEN_TILE_BWD_MLP = 512
TOKEN_TILE_WGRAD = 1024


def kernel(x, w_in, sgu_ln_g, sgu_ln_b, w_s, b_s, conv_w, conv_b, conv_ln_g, conv_ln_b, w_out, ln1_g, ln1_b, w_gate, w_up, w_down, ln2_g, ln2_b, loss_target, m_w_in, m_sgu_ln_g, m_sgu_ln_b, m_w_s, m_b_s, m_conv_w, m_conv_b, m_conv_ln_g, m_conv_ln_b, m_w_out, m_ln1_g, m_ln1_b, m_w_gate, m_w_up, m_w_down, m_ln2_g, m_ln2_b, v_w_in, v_sgu_ln_g, v_sgu_ln_b, v_w_s, v_b_s, v_conv_w, v_conv_b, v_conv_ln_g, v_conv_ln_b, v_w_out, v_ln1_g, v_ln1_b, v_w_gate, v_w_up, v_w_down, v_ln2_g, v_ln2_b):
    xs = x[0]
    tgt = loss_target[0]

    (win_b, wout_b, wgt_b, wut_b, wd_b, cw_b, wcat, wcatt, bs_full) = _prep_weights(
        w_in[0], w_out[0], w_gate[0].T, w_up[0].T, w_down[0], conv_w[0], w_s[0], b_s[0])
    win_g, wout_g, cw_g = _exchange("gather_mix_weights", [], [win_b, wout_b, cw_b])
    win_g = win_g.reshape(2 * D_MODEL, D_MODEL)
    wout_g = wout_g.reshape(D_MODEL, D_MODEL)
    cw = jnp.transpose(cw_g[:, :, :D_CONV // N_DEV], (1, 0, 2)).reshape(CONV_ROWS, D_CONV)

    (proj, ycat, n1, rstd1, phi, y_conv), (wgt_g, wut_g, wd_g) = _fwd_mix(
        xs, win_g, wout_g, sgu_ln_g, sgu_ln_b, wcat, bs_full, cw, conv_b, conv_ln_g, conv_ln_b, TOKEN_TILE_FWD_MIX,
        _Exchange([], [wgt_b, wut_b, wd_b]))
    wgt_g = wgt_g.reshape(D_FF, D_MODEL)
    wut_g = wut_g.reshape(D_FF, D_MODEL)
    wd_g = wd_g.reshape(D_FF, D_MODEL)
    gate, up, hh, x1b, dr2, vout = _fwd_mlp(n1, tgt, ln1_g, ln1_b, ln2_g, ln2_b, wgt_g, wut_g, wd_g, TOKEN_TILE_FWD_MLP)

    dgate, dup, dr1, vmlp = _bwd_mlp(dr2, gate, up, n1, rstd1, ln1_g, wgt_g, wut_g, wd_g, TOKEN_TILE_BWD_MLP)
    tk = TOKEN_TILE_WGRAD
    g_wgt = _wgrad("wgrad_gate", dgate, x1b, N_DEV, tk)
    g_wut = _wgrad("wgrad_up", dup, x1b, N_DEV, tk)
    g_wd = _wgrad("wgrad_down", hh, dr2, N_DEV, tk)
    g_wout = _wgrad("wgrad_out", ycat, dr1, N_DEV, tk)
    (gx, dproj, gws, gbs, gcw, vmix), (r_wgt, r_wut, r_wd, r_wout) = _bwd_mix(
        dr1, proj, phi, y_conv, win_g, wout_g, sgu_ln_g, sgu_ln_b, wcat, wcatt, bs_full, cw, conv_ln_g, conv_ln_b,
        TOKEN_TILE_BWD_MIX, _Exchange([g_wgt, g_wut, g_wd, g_wout], []))
    half = D_MODEL // 2
    g_win_a, (gws8, gbs8, gcw8, vmix8, vmlp8, vout8) = _wgrad(
        "wgrad_in_a", dproj, xs, N_DEV, tk, _Exchange([], [gws, gbs, gcw, vmix, vmlp, vout]), b_cols=(0, half))
    g_win_b, (r_win_a,) = _wgrad("wgrad_in_b", dproj, xs, N_DEV, tk, _Exchange([g_win_a], []), b_cols=(1, half))
    (r_win_b,) = _exchange("exchange_grad_in", [g_win_b], [])
    (u_gate, u_up, u_down), _ = _staged_call(
        "adamw_mlp",
        [_adamw_shard_group([r_wgt], w_gate[0].T, m_w_gate[0].T, v_w_gate[0].T, False),
         _adamw_shard_group([r_wut], w_up[0].T, m_w_up[0].T, v_w_up[0].T, False),
         _adamw_shard_group([r_wd], w_down[0], m_w_down[0], v_w_down[0], False)], 56)
    (u_in, u_out), _ = _staged_call(
        "adamw_mix",
        [_adamw_shard_group([r_win_a, r_win_b], w_in[0], m_w_in[0], v_w_in[0], True),
         _adamw_shard_group([r_wout], w_out[0], m_w_out[0], v_w_out[0], False)], 32)
    big = {"w_in": u_in, "w_out": u_out, "w_gate": u_gate, "w_up": u_up, "w_down": u_down}
    small_in = {
        "sgu_ln_g": (sgu_ln_g, m_sgu_ln_g, v_sgu_ln_g), "sgu_ln_b": (sgu_ln_b, m_sgu_ln_b, v_sgu_ln_b),
        "w_s": tuple(a.reshape(N_HEADS * CHUNK, CHUNK) for a in (w_s, m_w_s, v_w_s)),
        "b_s": (b_s[0], m_b_s[0], v_b_s[0]),
        "conv_b": (conv_b, m_conv_b, v_conv_b), "conv_ln_g": (conv_ln_g, m_conv_ln_g, v_conv_ln_g),
        "conv_ln_b": (conv_ln_b, m_conv_ln_b, v_conv_ln_b),
        "ln1_g": (ln1_g, m_ln1_g, v_ln1_g), "ln1_b": (ln1_b, m_ln1_b, v_ln1_b),
        "ln2_g": (ln2_g, m_ln2_g, v_ln2_g), "ln2_b": (ln2_b, m_ln2_b, v_ln2_b),
    }
    loss11, gcw_full, small = _finish_small(gws8, gbs8, gcw8, vmix8, vmlp8, vout8, small_in)

    me = 4 * lax.axis_index("x") + 2 * lax.axis_index("y") + lax.axis_index("c")
    g_cw = lax.dynamic_slice(gcw_full, (0, me * (D_CONV // N_DEV)), (CONV_WIDTH, D_CONV // N_DEV))
    d_cw, m_cw, v_cw = _adamw_plain("adamw_conv_w", g_cw, conv_w[0], m_conv_w[0], v_conv_w[0])

    shapes = {"w_s": w_s.shape, "b_s": b_s.shape}
    out = {}
    for n, r in big.items():
        out[n] = tuple((a.T if n in ("w_gate", "w_up") else a)[None] for a in r)
    for n, r in small.items():
        out[n] = tuple(a.reshape(shapes[n]) for a in r) if n in shapes else tuple(r)
    out["conv_w"] = tuple(a[None] for a in (g_cw, d_cw, m_cw, v_cw))

    order = ["w_in", "sgu_ln_g", "sgu_ln_b", "w_s", "b_s", "conv_w", "conv_b", "conv_ln_g", "conv_ln_b", "w_out",
             "ln1_g", "ln1_b", "w_gate", "w_up", "w_down", "ln2_g", "ln2_b"]
    loss = loss11[0, 0]
    return (loss, gx[None], *[out[n][0] for n in order], *[out[n][1] for n in order],
            *[out[n][2] for n in order], *[out[n][3] for n in order])
```

```python
import jax
import jax.numpy as jnp
from jax import lax
from jax.experimental import pallas as pl
from jax.experimental.pallas import tpu as pltpu

F32 = jnp.float32
BF16 = jnp.bfloat16

D_MODEL = 1024
D_SGU = 512
D_CONV = 512
N_HEADS = 8
CHUNK = 128
CONV_WIDTH = 31
CONV_ROWS = 32
HALO = 32
D_FF = 2816
N_DEV = 8
FF_SHARD = D_FF // N_DEV
ALPHA = (2.0 * 1) ** 0.25
LN_EPS = 1e-5
INV_SQRT2 = 0.7071067811865476
INV_SQRT_2PI = 0.3989422804014327

ADAM_LR = 0.001
ADAM_B1 = 0.9
ADAM_B2 = 0.999
ADAM_EPS = 1e-08
ADAM_WD = 0.01
ADAM_STEP = 10

MXU_COLS = 256
SUBLANES = 8
CONV_ROW_BLOCK = 32
WGRAD_ROW_BLOCK = 32
SHIFT_ROWS = HALO - SUBLANES
MIB = 1024 * 1024

HBM = pl.BlockSpec(memory_space=pltpu.HBM)
ANY = pl.BlockSpec(memory_space=pl.ANY)
MESH = pl.DeviceIdType.MESH


def _params(vmem_mib, grid_dims=0):
    kw = dict(vmem_limit_bytes=vmem_mib * MIB)
    if grid_dims:
        kw["dimension_semantics"] = ("arbitrary",) * grid_dims
    return pltpu.CompilerParams(**kw)


def _full(shape):
    return pl.BlockSpec(shape, lambda i: (0,) * len(shape))


def _dot(a, b):
    return jnp.dot(a, b, preferred_element_type=F32)


def _dot_nt(a, b):
    return lax.dot_general(a, b, (((1,), (1,)), ((), ())), preferred_element_type=F32)


def _dot_tn(a, b):
    return lax.dot_general(a, b, (((0,), (0,)), ((), ())), preferred_element_type=F32)


def _normal_cdf(x):
    return 0.5 * (1.0 + lax.erf(x * INV_SQRT2))


def _gelu_grad(x, cdf):
    return cdf + x * jnp.exp(-0.5 * x * x) * INV_SQRT_2PI


def _ln_fwd(v):
    mu = jnp.mean(v, axis=-1, keepdims=True)
    d = v - mu
    var = jnp.mean(d * d, axis=-1, keepdims=True)
    rstd = lax.rsqrt(var + LN_EPS)
    return d * rstd, rstd


def _ln_bwd(dyhat, yhat, rstd):
    m1 = jnp.mean(dyhat, axis=-1, keepdims=True)
    m2 = jnp.mean(dyhat * yhat, axis=-1, keepdims=True)
    return rstd * (dyhat - m1 - yhat * m2)


def _colsum(v):
    return jnp.sum(v, axis=0, keepdims=True)


def _head_pair_stack(v, lo):
    return jnp.concatenate([jnp.where(lo, v, 0.0), jnp.where(lo, 0.0, v)], axis=0).astype(BF16)


def _lo_mask():
    return lax.broadcasted_iota(jnp.int32, (CHUNK, CHUNK), 1) < (CHUNK // 2)


def _head_selector():
    head = lax.broadcasted_iota(jnp.int32, (N_HEADS, D_SGU), 0)
    lane = lax.broadcasted_iota(jnp.int32, (N_HEADS, D_SGU), 1)
    width = D_SGU // N_HEADS
    return ((lane >= head * width) & (lane < (head + 1) * width)).astype(F32)


def _shifted_copies(pad_ref, sh_ref, rows):
    for r in range(1, SUBLANES):
        sh_ref[r - 1, 0:rows, :] = pad_ref[pl.ds(r, rows), :]


def _tap_groups(offset_of_tap):
    groups = {}
    for k in range(CONV_WIDTH):
        o = offset_of_tap(k)
        groups.setdefault(o % SUBLANES, []).append((k, o // SUBLANES))
    return groups


def _tap_window(pad_ref, sh_ref, r, taps, row0, rows):
    q0 = min(q for _, q in taps)
    q1 = max(q for _, q in taps)
    src = pad_ref if r == 0 else sh_ref.at[r - 1]
    win = src[pl.ds(row0 + SUBLANES * q0, SUBLANES * (q1 - q0) + rows), :]
    return win, [(k, SUBLANES * (q - q0)) for k, q in taps]


def _causal_conv(pad_ref, sh_ref, w_ref, out_ref, rows, offset_of_tap, bias=None):
    groups = _tap_groups(offset_of_tap)

    def block(b, carry):
        row0 = pl.multiple_of(b * CONV_ROW_BLOCK, CONV_ROW_BLOCK)
        if bias is None:
            acc = jnp.zeros((CONV_ROW_BLOCK, D_CONV), F32)
        else:
            acc = jnp.broadcast_to(bias, (CONV_ROW_BLOCK, D_CONV))
        for r, taps in groups.items():
            win, starts = _tap_window(pad_ref, sh_ref, r, taps, row0, CONV_ROW_BLOCK)
            for k, s in starts:
                acc = acc + w_ref[k:k + 1, :] * win[s:s + CONV_ROW_BLOCK, :]
        out_ref[pl.ds(row0, CONV_ROW_BLOCK), :] = acc
        return carry

    lax.fori_loop(0, rows // CONV_ROW_BLOCK, block, 0)


def _conv_weight_grad(dy_ref, pad_ref, sh_ref, acc_ref, rows, offset_of_tap):
    groups = _tap_groups(offset_of_tap)
    for r, taps in groups.items():

        def block(b, parts, r=r, taps=taps):
            row0 = pl.multiple_of(b * WGRAD_ROW_BLOCK, WGRAD_ROW_BLOCK)
            dyb = dy_ref[pl.ds(row0, WGRAD_ROW_BLOCK), :]
            win, starts = _tap_window(pad_ref, sh_ref, r, taps, row0, WGRAD_ROW_BLOCK)
            out = []
            for part, (_, s) in zip(parts, starts):
                pr = dyb * win[s:s + WGRAD_ROW_BLOCK, :]
                out.append(part + pr.reshape(WGRAD_ROW_BLOCK // SUBLANES, SUBLANES, D_CONV).sum(axis=0))
            return tuple(out)

        zeros = tuple(jnp.zeros((SUBLANES, D_CONV), F32) for _ in taps)
        parts = lax.fori_loop(0, rows // WGRAD_ROW_BLOCK, block, zeros)
        for part, (k, _) in zip(parts, taps):
            acc_ref[k] += part


def _prep_weights(w_in, w_out, w_gate_t, w_up_t, w_down, conv_w, w_s, b_s):
    def compute(ins, outs):
        win_ref, wout_ref, wgt_ref, wut_ref, wd_ref, cw_ref, ws_ref, bs_ref = ins
        win_o, wout_o, wgt_o, wut_o, wd_o, cw_o, wcat_o, wcatt_o, bsf_o = outs
        win_o[...] = win_ref[...].T.astype(BF16)
        wout_o[...] = wout_ref[...].astype(BF16)
        wgt_o[...] = wgt_ref[...].astype(BF16)
        wut_o[...] = wut_ref[...].astype(BF16)
        wd_o[...] = wd_ref[...].astype(BF16)
        cw_o[...] = jnp.zeros(cw_o.shape, F32)
        cw_o[0:CONV_WIDTH, 0:D_CONV // N_DEV] = cw_ref[...]
        row = lax.broadcasted_iota(jnp.int32, (CHUNK, CHUNK), 0)
        col = lax.broadcasted_iota(jnp.int32, (CHUNK, CHUNK), 1)
        causal = row >= col
        for h in range(N_HEADS):
            w = jnp.where(causal, ws_ref[h], 0.0)
            p, half = h // 2, (h % 2) * CHUNK
            wcat_o[p, :, half:half + CHUNK] = w.astype(BF16)
            wcatt_o[p, :, half:half + CHUNK] = w.T.astype(BF16)
        bsf_o[...] = lax.dot_general(bs_ref[...], _head_selector(), (((0,), (0,)), ((), ())),
                                     preferred_element_type=F32, precision=lax.Precision.HIGHEST)

    S = jax.ShapeDtypeStruct
    out_shapes = [S((256, D_MODEL), BF16), S((128, D_MODEL), BF16), S((FF_SHARD, D_MODEL), BF16),
                  S((FF_SHARD, D_MODEL), BF16), S((FF_SHARD, D_MODEL), BF16), S((CONV_ROWS, 128), F32),
                  S((4, CHUNK, 2 * CHUNK), BF16), S((4, CHUNK, 2 * CHUNK), BF16), S((CHUNK, D_SGU), F32)]
    (res,), _ = _staged_call(
        "prep_weights", [([w_in, w_out, w_gate_t, w_up_t, w_down, conv_w, w_s, b_s], out_shapes, compute)], 32)
    return res


def _mesh_position():
    x, y, c = lax.axis_index("x"), lax.axis_index("y"), lax.axis_index("c")
    return x, y, c


def _peers(x, y, c):
    out = []
    for k in range(1, N_DEV):
        px = 1 - x if (k >> 2) & 1 else x
        py = 1 - y if (k >> 1) & 1 else y
        pc = 1 - c if k & 1 else c
        out.append(((px, py, pc), 4 * px + 2 * py + pc))
    return out


class _Exchange:
    def __init__(self, scatter, gather):
        self.arrays = list(scatter) + list(gather)
        self.n_sc = len(scatter)
        self.n = len(self.arrays)
        self.out_shape = [jax.ShapeDtypeStruct(a.shape if k < self.n_sc else (N_DEV,) + a.shape, a.dtype)
                          for k, a in enumerate(self.arrays)]
        n_remote = self.n * (N_DEV - 1)
        self.scratch = [pltpu.SemaphoreType.DMA((n_remote,)), pltpu.SemaphoreType.DMA((n_remote,)),
                        pltpu.SemaphoreType.DMA((self.n,))] if self.n else []

    def _copies(self, src, dst, sems):
        send_sems, recv_sems, local_sems = sems
        x, y, c = _mesh_position()
        me = 4 * x + 2 * y + c
        locals_, first, arrivals, passed, last = [], [], [], [], []

        def remote(a, k, src_ref, slot, to):
            s = a * (N_DEV - 1) + k
            return pltpu.make_async_remote_copy(src_ref=src_ref, dst_ref=dst[a].at[slot], send_sem=send_sems.at[s],
                                                recv_sem=recv_sems.at[s], device_id=to, device_id_type=MESH)

        for a in range(self.n):
            if a < self.n_sc:
                locals_.append(pltpu.make_async_copy(src[a].at[me], dst[a].at[me], local_sems.at[a]))
                for k, (peer, pid) in enumerate(_peers(x, y, c)):
                    first.append(remote(a, k, src[a].at[pid], me, peer))
                    last.append(remote(a, k, src[a].at[pid], pid, peer))
                continue
            locals_.append(pltpu.make_async_copy(src[a], dst[a].at[me], local_sems.at[a]))
            sibling, sib_id = (x, y, 1 - c), 4 * x + 2 * y + (1 - c)
            chips = [(1 - x, y), (x, 1 - y), (1 - x, 1 - y)]
            first.append(remote(a, 0, src[a], me, sibling))
            last.append(remote(a, 0, src[a], sib_id, sibling))
            for j, (px, py) in enumerate(chips):
                same, other = 4 * px + 2 * py + c, 4 * px + 2 * py + (1 - c)
                first.append(remote(a, 1 + j, src[a], me, (px, py, c)))
                arrivals.append(remote(a, 1 + j, src[a], same, (px, py, c)))
                passed.append(remote(a, 4 + j, dst[a].at[same], same, sibling))
                last.append(remote(a, 4 + j, dst[a].at[other], other, sibling))
        return locals_, first, arrivals, passed, last

    def start(self, src, dst, sems):
        if not self.n:
            return
        locals_, first, _, _, _ = self._copies(src, dst, sems)
        for cp in locals_ + first:
            cp.start()

    def forward(self, src, dst, sems):
        if self.n == self.n_sc:
            return
        _, _, arrivals, passed, _ = self._copies(src, dst, sems)
        for arrived, cp in zip(arrivals, passed):
            arrived.wait_recv()
            cp.start()

    def wait(self, src, dst, sems):
        if not self.n:
            return
        locals_, first, _, passed, last = self._copies(src, dst, sems)
        for cp in last:
            cp.wait_recv()
        for cp in first + passed:
            cp.wait_send()
        for cp in locals_:
            cp.wait()


def _exchange(name, scatter, gather):
    ex = _Exchange(scatter, gather)
    n = ex.n

    def body(*refs):
        src, dst, sems = refs[:n], refs[n:2 * n], refs[2 * n:]
        ex.start(src, dst, sems)
        ex.forward(src, dst, sems)
        ex.wait(src, dst, sems)

    return pl.pallas_call(
        body, name=name, out_shape=tuple(ex.out_shape), in_specs=[ANY] * n, out_specs=(ANY,) * n,
        scratch_shapes=ex.scratch,
    )(*ex.arrays)


def _forward_step(n_steps):
    return (11 * n_steps) // 16


def _hosted(ex, refs, n_in, n_out):
    ins, ex_src = refs[:n_in], refs[n_in:n_in + ex.n]
    rest = refs[n_in + ex.n:]
    outs, ex_dst = rest[:n_out], rest[n_out:n_out + ex.n]
    rest = rest[n_out + ex.n:]
    n_own = len(rest) - len(ex.scratch)
    return ins, outs, rest[:n_own], (ex_src, ex_dst, rest[n_own:])


def _fwd_mix(x, win_g, wout_g, sgu_g, sgu_b, wcat, bs_full, cw, cb, cg, cbeta, tm, ex):
    T = x.shape[0]
    nt = T // tm

    def body(*refs):
        ins, outs, scratch, ex_refs = _hosted(ex, refs, 11, 6)
        x_ref, win_ref, wout_ref, sg_ref, sb_ref, wcat_ref, bs_ref, cw_ref, cb_ref, cg_ref, cbeta_ref = ins
        proj_ref, ycat_ref, n1_ref, rstd1_ref, phi_ref, y_ref = outs
        hpad, hshift = scratch
        i = pl.program_id(0)

        @pl.when(i == 0)
        def _():
            ex.start(*ex_refs)

        xf = x_ref[...]
        xb = xf.astype(BF16)
        proj_ref[...] = _dot_nt(xb, win_ref[...])
        phi_ref[...] = _normal_cdf(proj_ref[:, 0:2 * D_SGU])
        u = proj_ref[:, 0:D_SGU] * phi_ref[:, 0:D_SGU]
        v = proj_ref[:, D_SGU:2 * D_SGU] * phi_ref[:, D_SGU:2 * D_SGU]
        vhat, _ = _ln_fwd(v)
        vn = vhat * sg_ref[...] + sb_ref[...]
        lo = _lo_mask()
        for c in range(tm // CHUNK):
            rows = slice(CHUNK * c, CHUNK * (c + 1))
            for p in range(4):
                lanes = slice(CHUNK * p, CHUNK * (p + 1))
                mixed = _dot(wcat_ref[p], _head_pair_stack(vn[rows, lanes], lo)) + bs_ref[:, lanes]
                ycat_ref[rows, lanes] = (u[rows, lanes] * mixed).astype(BF16)
        base = 2 * D_SGU
        a = proj_ref[:, base:base + D_CONV]
        g = proj_ref[:, base + D_CONV:base + 2 * D_CONV]

        @pl.when(i == 0)
        def _():
            hpad[0:HALO, :] = jnp.zeros((HALO, D_CONV), F32)

        hpad[HALO:HALO + tm, :] = a * jax.nn.sigmoid(g)
        _shifted_copies(hpad, hshift, tm + SHIFT_ROWS)
        _causal_conv(hpad, hshift, cw_ref, y_ref, tm, lambda k: HALO - (CONV_WIDTH - 1) + k, bias=cb_ref[...])
        hpad[0:HALO, :] = hpad[tm:tm + HALO, :]
        yhat, _ = _ln_fwd(y_ref[...])
        yn = yhat * cg_ref[...] + cbeta_ref[...]
        ycat_ref[:, D_SGU:D_SGU + D_CONV] = (yn * jax.nn.sigmoid(yn)).astype(BF16)
        r1 = ALPHA * xf + _dot(ycat_ref[...], wout_ref[...])
        n1, rstd1 = _ln_fwd(r1)
        n1_ref[...] = n1
        rstd1_ref[...] = rstd1

        @pl.when(i == _forward_step(nt))
        def _():
            ex.forward(*ex_refs)

        @pl.when(i == nt - 1)
        def _():
            ex.wait(*ex_refs)

    S = jax.ShapeDtypeStruct
    row = lambda w: pl.BlockSpec((tm, w), lambda i: (i, 0))
    res = pl.pallas_call(
        body, name="fwd_mix", grid=(nt,),
        in_specs=[row(D_MODEL), _full(win_g.shape), _full(wout_g.shape), _full(sgu_g.shape), _full(sgu_b.shape),
                  _full(wcat.shape), _full(bs_full.shape), _full(cw.shape), _full(cb.shape), _full(cg.shape),
                  _full(cbeta.shape)] + [ANY] * ex.n,
        out_specs=(row(2 * D_MODEL), row(D_MODEL), row(D_MODEL), row(1), row(2 * D_SGU), row(D_CONV)) + (ANY,) * ex.n,
        out_shape=(S((T, 2 * D_MODEL), F32), S((T, D_MODEL), BF16), S((T, D_MODEL), F32), S((T, 1), F32),
                   S((T, 2 * D_SGU), F32), S((T, D_CONV), F32), *ex.out_shape),
        scratch_shapes=[pltpu.VMEM((tm + HALO, D_CONV), F32),
                        pltpu.VMEM((SUBLANES - 1, tm + SHIFT_ROWS, D_CONV), F32)] + ex.scratch,
        compiler_params=_params(56, 1),
    )(x, win_g, wout_g, sgu_g, sgu_b, wcat, bs_full, cw, cb, cg, cbeta, *ex.arrays)
    return res[:6], res[6:]


class _ResidentWeights:
    def __init__(self, pairs, sems, chunk):
        self.pairs, self.sems, self.chunk = pairs, sems, chunk
        self.n_chunks = pairs[0][0].shape[0] // chunk

    def _copy(self, k, f):
        src, dst = self.pairs[k]
        rows = pl.ds(self.chunk * f, self.chunk)
        return pltpu.make_async_copy(src.at[rows], dst.at[rows], self.sems.at[k * self.n_chunks + f])

    def start(self, order):
        for k in order:
            for f in range(self.n_chunks):
                self._copy(k, f).start()

    def wait(self, k, f=None):
        for c in (range(self.n_chunks) if f is None else [f]):
            self._copy(k, c).wait()


def _fwd_mlp(n1, tgt, l1g, l1b, l2g, l2b, wgt, wut, wd, tm):
    T = n1.shape[0]
    nt = T // tm
    nf = D_FF // MXU_COLS

    def body(n1_ref, tgt_ref, l1g_ref, l1b_ref, l2g_ref, l2b_ref, wg_hbm, wu_hbm, wd_hbm,
             gate_ref, up_ref, hh_ref, x1b_ref, dr2_ref, stat_ref, wg_s, wu_s, wd_s, sems):
        weights = _ResidentWeights([(wg_hbm, wg_s), (wu_hbm, wu_s), (wd_hbm, wd_s)], sems, MXU_COLS)

        def step(first):
            if first:
                weights.start([0, 1, 2])
                stat_ref[...] = jnp.zeros(stat_ref.shape, F32)
            x1 = n1_ref[...] * l1g_ref[...] + l1b_ref[...]
            x1b = x1.astype(BF16)
            x1b_ref[...] = x1b
            for f in range(nf):
                cols = slice(MXU_COLS * f, MXU_COLS * (f + 1))
                if first:
                    weights.wait(0, f)
                    weights.wait(1, f)
                gt = _dot_nt(x1b, wg_s[cols, :])
                ut = _dot_nt(x1b, wu_s[cols, :])
                gate_ref[:, cols] = gt.astype(BF16)
                up_ref[:, cols] = ut.astype(BF16)
                hh_ref[:, cols] = (gt * jax.nn.sigmoid(gt) * ut).astype(BF16)
            if first:
                weights.wait(2)
            r2 = ALPHA * x1 + _dot(hh_ref[...], wd_s[...])
            n2, rstd2 = _ln_fwd(r2)
            x2 = n2 * l2g_ref[...] + l2b_ref[...]
            diff = x2 - tgt_ref[...]
            dx2 = diff * (1.0 / D_MODEL)
            stat_ref[0:1, :] += _colsum(diff * diff)
            stat_ref[1:2, :] += _colsum(dx2 * n2)
            stat_ref[2:3, :] += _colsum(dx2)
            dr2_ref[...] = _ln_bwd(dx2 * l2g_ref[...], n2, rstd2)

        pl.when(pl.program_id(0) == 0)(lambda: step(True))
        pl.when(pl.program_id(0) > 0)(lambda: step(False))

    S = jax.ShapeDtypeStruct
    row = lambda w: pl.BlockSpec((tm, w), lambda i: (i, 0))
    vec = _full((1, D_MODEL))
    return pl.pallas_call(
        body, name="fwd_mlp", grid=(nt,),
        in_specs=[row(D_MODEL), row(D_MODEL), vec, vec, vec, vec, ANY, ANY, ANY],
        out_specs=(row(D_FF), row(D_FF), row(D_FF), row(D_MODEL), row(D_MODEL), _full((8, D_MODEL))),
        out_shape=(S((T, D_FF), BF16), S((T, D_FF), BF16), S((T, D_FF), BF16), S((T, D_MODEL), BF16),
                   S((T, D_MODEL), F32), S((8, D_MODEL), F32)),
        scratch_shapes=[pltpu.VMEM((D_FF, D_MODEL), BF16)] * 3 + [pltpu.SemaphoreType.DMA((3 * nf,))],
        compiler_params=_params(56, 1),
    )(n1, tgt, l1g, l1b, l2g, l2b, wgt, wut, wd)


def _bwd_mlp(dr2, gate, up, n1, rstd1, l1g, wgt, wut, wd, tm):
    T = n1.shape[0]
    nt = T // tm
    nf = D_FF // MXU_COLS

    def body(dr2_ref, gate_ref, up_ref, n1_ref, rstd1_ref, l1g_ref, wg_hbm, wu_hbm, wd_hbm,
             dgate_ref, dup_ref, dr1_ref, stat_ref, wg_s, wu_s, wd_s, sems):
        weights = _ResidentWeights([(wg_hbm, wg_s), (wu_hbm, wu_s), (wd_hbm, wd_s)], sems, MXU_COLS)

        def step(first):
            if first:
                weights.start([2, 0, 1])
                stat_ref[...] = jnp.zeros(stat_ref.shape, F32)
            dr2 = dr2_ref[...]
            dr2b = dr2.astype(BF16)
            for f in range(nf):
                cols = slice(MXU_COLS * f, MXU_COLS * (f + 1))
                if first:
                    weights.wait(2, f)
                dhh = _dot_nt(dr2b, wd_s[cols, :])
                gt = gate_ref[:, cols].astype(F32)
                ut = up_ref[:, cols].astype(F32)
                sg = jax.nn.sigmoid(gt)
                dgate_ref[:, cols] = (dhh * ut * (sg * (1.0 + gt * (1.0 - sg)))).astype(BF16)
                dup_ref[:, cols] = (dhh * (gt * sg)).astype(BF16)
            if first:
                weights.wait(0)
                weights.wait(1)
            dx1 = ALPHA * dr2 + _dot(dgate_ref[...], wg_s[...]) + _dot(dup_ref[...], wu_s[...])
            n1 = n1_ref[...]
            stat_ref[0:1, :] += _colsum(dx1 * n1)
            stat_ref[1:2, :] += _colsum(dx1)
            dr1_ref[...] = _ln_bwd(dx1 * l1g_ref[...], n1, rstd1_ref[...])

        pl.when(pl.program_id(0) == 0)(lambda: step(True))
        pl.when(pl.program_id(0) > 0)(lambda: step(False))

    S = jax.ShapeDtypeStruct
    row = lambda w: pl.BlockSpec((tm, w), lambda i: (i, 0))
    return pl.pallas_call(
        body, name="bwd_mlp", grid=(nt,),
        in_specs=[row(D_MODEL), row(D_FF), row(D_FF), row(D_MODEL), row(1), _full((1, D_MODEL)), ANY, ANY, ANY],
        out_specs=(row(D_FF), row(D_FF), row(D_MODEL), _full((8, D_MODEL))),
        out_shape=(S((T, D_FF), BF16), S((T, D_FF), BF16), S((T, D_MODEL), F32), S((8, D_MODEL), F32)),
        scratch_shapes=[pltpu.VMEM((D_FF, D_MODEL), BF16)] * 3 + [pltpu.SemaphoreType.DMA((3 * nf,))],
        compiler_params=_params(56, 1),
    )(dr2, gate, up, n1, rstd1, l1g, wgt, wut, wd)


def _bwd_mix(dr1, proj, phi, y, win_g, wout_g, sgu_g, sgu_b, wcat, wcatt, bs_full, cw, cg, cbeta, tm, ex):
    T = dr1.shape[0]
    nt = T // tm
    halo_blocks = tm // HALO

    def body(*refs):
        ins, outs, scratch, ex_refs = _hosted(ex, refs, 15, 6)
        (dr1_ref, proj_ref, halo_ref, phi_ref, y_ref, win_ref, wout_ref, sg_ref, sb_ref, wcat_ref, wcatt_ref, bs_ref,
         cw_ref, cg_ref, cbeta_ref) = ins
        gx_ref, dproj_ref, gws_out, gbs_out, gcw_ref, vec_ref = outs
        hpad, hshift, dypad, dyshift, dhbuf, dubuf, dvnbuf, gcw_acc, gws_ref, gbs_ref = scratch
        i = pl.program_id(0)
        tile = nt - 1 - i

        @pl.when(i == 0)
        def _():
            ex.start(*ex_refs)
            gws_ref[...] = jnp.zeros(gws_ref.shape, F32)
            gbs_ref[...] = jnp.zeros(gbs_ref.shape, F32)
            gcw_ref[...] = jnp.zeros(gcw_ref.shape, F32)
            vec_ref[...] = jnp.zeros(vec_ref.shape, F32)
            gcw_acc[...] = jnp.zeros(gcw_acc.shape, F32)
            dypad[tm:tm + HALO, :] = jnp.zeros((HALO, D_CONV), F32)

        dr1 = dr1_ref[...]
        dycat = _dot_nt(dr1.astype(BF16), wout_ref[...])
        pu = proj_ref[:, 0:D_SGU]
        pv = proj_ref[:, D_SGU:2 * D_SGU]
        u = pu * phi_ref[:, 0:D_SGU]
        vhat, rstd_v = _ln_fwd(pv * phi_ref[:, D_SGU:2 * D_SGU])
        vn = vhat * sg_ref[...] + sb_ref[...]
        lo = _lo_mask()
        for c in range(tm // CHUNK):
            rows = slice(CHUNK * c, CHUNK * (c + 1))
            for p in range(4):
                lanes = slice(CHUNK * p, CHUNK * (p + 1))
                vstack = _head_pair_stack(vn[rows, lanes], lo)
                mixed = _dot(wcat_ref[p], vstack) + bs_ref[:, lanes]
                d_a = dycat[rows, lanes]
                dubuf[rows, lanes] = d_a * mixed
                dm = d_a * u[rows, lanes]
                gbs_ref[:, lanes] += dm
                dstack = _head_pair_stack(dm, lo)
                gws_ref[2 * CHUNK * p:2 * CHUNK * (p + 1), :] += _dot_nt(dstack, vn[rows, lanes].astype(BF16))
                dvnbuf[rows, lanes] = _dot(wcatt_ref[p], dstack)
        dvn = dvnbuf[...]
        vec_ref[0:1, :] += _colsum(dvn * vhat)
        vec_ref[1:2, :] += _colsum(dvn)
        dv = _ln_bwd(dvn * sg_ref[...], vhat, rstd_v)
        dproj_ref[:, 0:D_SGU] = (dubuf[...] * _gelu_grad(pu, phi_ref[:, 0:D_SGU])).astype(BF16)
        dproj_ref[:, D_SGU:2 * D_SGU] = (dv * _gelu_grad(pv, phi_ref[:, D_SGU:2 * D_SGU])).astype(BF16)
        base = 2 * D_SGU
        a = proj_ref[:, base:base + D_CONV]
        sgm = jax.nn.sigmoid(proj_ref[:, base + D_CONV:base + 2 * D_CONV])
        h_before = halo_ref[:, 0:D_CONV] * jax.nn.sigmoid(halo_ref[:, D_CONV:2 * D_CONV])
        hpad[0:HALO, :] = jnp.where(tile > 0, h_before, 0.0)
        hpad[HALO:HALO + tm, :] = a * sgm
        _shifted_copies(hpad, hshift, tm + SHIFT_ROWS)
        h_offset = lambda k: HALO - (CONV_WIDTH - 1) + k
        yhat, rstd_y = _ln_fwd(y_ref[...])
        yn = yhat * cg_ref[...] + cbeta_ref[...]
        s = jax.nn.sigmoid(yn)
        dyn = dycat[:, D_SGU:D_SGU + D_CONV] * (s * (1.0 + yn * (1.0 - s)))
        vec_ref[3:4, :] += _colsum(dyn * yhat)
        vec_ref[4:5, :] += _colsum(dyn)
        dy = _ln_bwd(dyn * cg_ref[...], yhat, rstd_y)
        vec_ref[2:3, :] += _colsum(dy)
        dypad[0:tm, :] = dy
        _shifted_copies(dypad, dyshift, tm + SHIFT_ROWS)
        _causal_conv(dypad, dyshift, cw_ref, dhbuf, tm, lambda k: (CONV_WIDTH - 1) - k)
        _conv_weight_grad(dypad, hpad, hshift, gcw_acc, tm, h_offset)
        dypad[tm:tm + HALO, :] = dypad[0:HALO, :]
        dh = dhbuf[...]
        dproj_ref[:, base:base + D_CONV] = (dh * sgm).astype(BF16)
        dproj_ref[:, base + D_CONV:base + 2 * D_CONV] = (dh * a * sgm * (1.0 - sgm)).astype(BF16)
        gx_ref[...] = ALPHA * dr1 + _dot(dproj_ref[...], win_ref[...])

        @pl.when(i == _forward_step(nt))
        def _():
            ex.forward(*ex_refs)

        @pl.when(i == nt - 1)
        def _():
            gcw_ref[...] = gcw_acc[...].sum(axis=1)
            gws_out[...] = gws_ref[...].astype(BF16)
            gbs_out[...] = lax.dot_general(_head_selector(), gbs_ref[...], (((1,), (1,)), ((), ())),
                                           preferred_element_type=F32, precision=lax.Precision.HIGHEST)
            ex.wait(*ex_refs)

    S = jax.ShapeDtypeStruct
    row = lambda w: pl.BlockSpec((tm, w), lambda i: (nt - 1 - i, 0))
    halo = pl.BlockSpec((HALO, D_MODEL), lambda i: (jnp.maximum((nt - 1 - i) * halo_blocks - 1, 0), 1))
    res = pl.pallas_call(
        body, name="bwd_mix", grid=(nt,),
        in_specs=[row(D_MODEL), row(2 * D_MODEL), halo, row(2 * D_SGU), row(D_CONV), _full(win_g.shape),
                  _full(wout_g.shape), _full(sgu_g.shape), _full(sgu_b.shape), _full(wcat.shape), _full(wcatt.shape),
                  _full(bs_full.shape), _full(cw.shape), _full(cg.shape), _full(cbeta.shape)]
        + [ANY] * ex.n,
        out_specs=(row(D_MODEL), row(2 * D_MODEL), _full((N_HEADS * CHUNK, CHUNK)), _full((N_HEADS, CHUNK)),
                   _full((CONV_ROWS, D_CONV)), _full((8, D_CONV))) + (ANY,) * ex.n,
        out_shape=(S((T, D_MODEL), F32), S((T, 2 * D_MODEL), BF16), S((N_HEADS * CHUNK, CHUNK), BF16),
                   S((N_HEADS, CHUNK), F32), S((CONV_ROWS, D_CONV), F32), S((8, D_CONV), F32), *ex.out_shape),
        scratch_shapes=[pltpu.VMEM((tm + HALO, D_CONV), F32), pltpu.VMEM((SUBLANES - 1, tm + SHIFT_ROWS, D_CONV), F32),
                        pltpu.VMEM((tm + HALO, D_CONV), F32), pltpu.VMEM((SUBLANES - 1, tm + SHIFT_ROWS, D_CONV), F32),
                        pltpu.VMEM((tm, D_CONV), F32), pltpu.VMEM((tm, D_SGU), F32),
                        pltpu.VMEM((tm, D_SGU), F32), pltpu.VMEM((CONV_ROWS, 8, D_CONV), F32),
                        pltpu.VMEM((N_HEADS * CHUNK, CHUNK), F32), pltpu.VMEM((CHUNK, D_SGU), F32)] + ex.scratch,
        compiler_params=_params(56, 1),
    )(dr1, proj, proj, phi, y, win_g, wout_g, sgu_g, sgu_b, wcat, wcatt, bs_full, cw, cg, cbeta, *ex.arrays)
    return res[:6], res[6:]


def _wgrad(name, a, b, blocks, tk, ex=None, b_cols=None):
    T, M = a.shape
    col, N = b_cols or (0, b.shape[1])
    nk = T // tk
    out_shape = (blocks, M // blocks, N)
    ex = ex or _Exchange([], [])

    def body(*refs):
        (a_ref, b_ref), (o_ref,), (acc,), ex_refs = _hosted(ex, refs, 2, 1)
        i = pl.program_id(0)

        @pl.when(i == 0)
        def _():
            ex.start(*ex_refs)
            acc[...] = jnp.zeros(acc.shape, F32)

        acc[...] += _dot_tn(a_ref[...].astype(BF16), b_ref[...].astype(BF16))

        @pl.when(i == _forward_step(nk))
        def _():
            ex.forward(*ex_refs)

        @pl.when(i == nk - 1)
        def _():
            o_ref[...] = acc[...].astype(BF16)
            ex.wait(*ex_refs)

    res = pl.pallas_call(
        body, name=name, grid=(nk,),
        in_specs=[pl.BlockSpec((tk, M), lambda i: (i, 0)), pl.BlockSpec((tk, N), lambda i: (i, col))] + [ANY] * ex.n,
        out_specs=(_full((M, N)),) + (ANY,) * ex.n,
        out_shape=(jax.ShapeDtypeStruct((M, N), BF16), *ex.out_shape),
        scratch_shapes=[pltpu.VMEM((M, N), F32)] + ex.scratch,
        compiler_params=_params(56, 1),
    )(a, b, *ex.arrays)
    g = res[0].reshape(out_shape)
    return (g, res[1:]) if ex.n else g


def _adamw(w, g, m, v):
    m2 = ADAM_B1 * m + (1.0 - ADAM_B1) * g
    v2 = ADAM_B2 * v + (1.0 - ADAM_B2) * (g * g)
    m_hat = m2 / (1.0 - ADAM_B1 ** ADAM_STEP)
    v_hat = v2 / (1.0 - ADAM_B2 ** ADAM_STEP)
    delta = -ADAM_LR * (m_hat / (jnp.sqrt(v_hat) + ADAM_EPS) + ADAM_WD * w)
    return delta, m2, v2


def _sum_partials(r_ref):
    g = r_ref[0].astype(F32)
    for s in range(1, N_DEV):
        g = g + r_ref[s].astype(F32)
    return g


def _staged_call(name, groups, vmem_mib, ex=None):
    ex = ex or _Exchange([], [])
    inputs = [a for ins, _, _ in groups for a in ins]
    out_shapes = [s for _, outs, _ in groups for s in outs]
    n_in, n_out = len(inputs), len(out_shapes)

    def body(*refs):
        ins, outs, scratch, ex_refs = _hosted(ex, refs, n_in, n_out)
        in_bufs, out_bufs, sems = scratch[:n_in], scratch[n_in:n_in + n_out], scratch[n_in + n_out]
        ex.start(*ex_refs)
        loads = [pltpu.make_async_copy(ins[k], in_bufs[k], sems.at[k]) for k in range(n_in)]
        stores = [pltpu.make_async_copy(out_bufs[k], outs[k], sems.at[n_in + k]) for k in range(n_out)]
        for cp in loads:
            cp.start()
        i0 = o0 = 0
        for g_ins, g_outs, compute in groups:
            i1, o1 = i0 + len(g_ins), o0 + len(g_outs)
            for cp in loads[i0:i1]:
                cp.wait()
            compute(in_bufs[i0:i1], out_bufs[o0:o1])
            for cp in stores[o0:o1]:
                cp.start()
            i0, o0 = i1, o1
        for cp in stores:
            cp.wait()
        ex.forward(*ex_refs)
        ex.wait(*ex_refs)

    scratch = ([pltpu.VMEM(a.shape, a.dtype) for a in inputs] + [pltpu.VMEM(s.shape, s.dtype) for s in out_shapes]
               + [pltpu.SemaphoreType.DMA((n_in + n_out,))] + ex.scratch)
    res = pl.pallas_call(
        body, name=name, out_shape=(*[pltpu.HBM(s.shape, s.dtype) for s in out_shapes], *ex.out_shape),
        in_specs=[HBM] * n_in + [ANY] * ex.n, out_specs=(HBM,) * n_out + (ANY,) * ex.n,
        scratch_shapes=scratch, compiler_params=_params(vmem_mib),
    )(*[pltpu.with_memory_space_constraint(a, pltpu.HBM) for a in inputs], *ex.arrays)
    per_group, o0 = [], 0
    for _, g_outs, _ in groups:
        per_group.append(list(res[o0:o0 + len(g_outs)]))
        o0 += len(g_outs)
    return per_group, res[n_out:]


def _adamw_shard_group(parts, w, m, v, transposed):
    n = len(parts)

    def compute(ins, outs):
        w_ref, m_ref, v_ref = ins[n:]
        lo = 0
        for r_ref in ins[:n]:
            g = _sum_partials(r_ref)
            cols = g.shape[1]
            if transposed:
                g, at = g.T, (slice(lo, lo + cols), slice(None))
            else:
                at = (slice(None), slice(lo, lo + cols))
            delta, m2, v2 = _adamw(w_ref[at], g, m_ref[at], v_ref[at])
            for o, val in zip(outs, (g, delta, m2, v2)):
                o[at] = val
            lo += cols

    return [*parts, w, m, v], [jax.ShapeDtypeStruct(w.shape, F32)] * 4, compute


def _finish_small(gws8, gbs8, gcw8, vmix8, vmlp8, vout8, small):
    names = ["sgu_ln_g", "sgu_ln_b", "w_s", "b_s", "conv_b", "conv_ln_g", "conv_ln_b", "ln1_g", "ln1_b", "ln2_g", "ln2_b"]
    flat = []
    for n in names:
        flat += list(small[n])

    def compute(ins, outs):
        gws_ref, gbs_ref, gcw_ref, vmix_ref, vmlp_ref, vout_ref = ins[:6]
        wmv = ins[6:]
        loss_o, gcw_o = outs[0], outs[1]
        outs = outs[2:]
        gws = _sum_partials(gws_ref)
        gbs = _sum_partials(gbs_ref)
        vmix = _sum_partials(vmix_ref)
        vmlp = _sum_partials(vmlp_ref)
        vout = _sum_partials(vout_ref)
        gcw_o[...] = _sum_partials(gcw_ref)
        loss = (0.5 / D_MODEL) * jnp.sum(vout[0:1, :], axis=1, keepdims=True)
        loss_o[...] = jnp.broadcast_to(loss, loss_o.shape)
        rows = lax.broadcasted_iota(jnp.int32, (N_HEADS * CHUNK, CHUNK), 0)
        cols = lax.broadcasted_iota(jnp.int32, (N_HEADS * CHUNK, CHUNK), 1)
        gws = jnp.where((rows & (CHUNK - 1)) >= cols, gws, 0.0)
        grads = {
            "sgu_ln_g": vmix[0:1, :], "sgu_ln_b": vmix[1:2, :], "w_s": gws, "b_s": gbs,
            "conv_b": vmix[2:3, :], "conv_ln_g": vmix[3:4, :], "conv_ln_b": vmix[4:5, :],
            "ln1_g": vmlp[0:1, :], "ln1_b": vmlp[1:2, :], "ln2_g": vout[1:2, :], "ln2_b": vout[2:3, :],
        }
        for k, n in enumerate(names):
            w_ref, m_ref, v_ref = wmv[3 * k:3 * k + 3]
            g = grads[n]
            delta, m2, v2 = _adamw(w_ref[...], g, m_ref[...], v_ref[...])
            outs[4 * k][...] = g
            outs[4 * k + 1][...] = delta
            outs[4 * k + 2][...] = m2
            outs[4 * k + 3][...] = v2

    S = jax.ShapeDtypeStruct
    out_shape = [S((SUBLANES, 128), F32), S((CONV_ROWS, D_CONV), F32)]
    for n in names:
        out_shape += [S(small[n][0].shape, F32)] * 4
    (res,), _ = _staged_call(
        "finish_small", [([gws8, gbs8, gcw8, vmix8, vmlp8, vout8, *flat], out_shape, compute)], 40)
    upd = {n: res[2 + 4 * k:6 + 4 * k] for k, n in enumerate(names)}
    return res[0], res[1], upd


def _adamw_plain(name, g, w, m, v):
    def compute(ins, outs):
        g_ref, w_ref, m_ref, v_ref = ins
        for o, val in zip(outs, _adamw(w_ref[...], g_ref[...], m_ref[...], v_ref[...])):
            o[...] = val

    (res,), _ = _staged_call(name, [([g, w, m, v], [jax.ShapeDtypeStruct(w.shape, F32)] * 3, compute)], 16)
    return res


TOKEN_TILE_FWD_MIX = 512
TOKEN_TILE_BWD_MIX = 256
TOKEN_TILE_FWD_MLP = 512
TOKEN_TILE_BWD_MLP = 512
TOKEN_TILE_WGRAD = 1024


def kernel(x, w_in, sgu_ln_g, sgu_ln_b, w_s, b_s, conv_w, conv_b, conv_ln_g, conv_ln_b, w_out, ln1_g, ln1_b, w_gate, w_up, w_down, ln2_g, ln2_b, loss_target, m_w_in, m_sgu_ln_g, m_sgu_ln_b, m_w_s, m_b_s, m_conv_w, m_conv_b, m_conv_ln_g, m_conv_ln_b, m_w_out, m_ln1_g, m_ln1_b, m_w_gate, m_w_up, m_w_down, m_ln2_g, m_ln2_b, v_w_in, v_sgu_ln_g, v_sgu_ln_b, v_w_s, v_b_s, v_conv_w, v_conv_b, v_conv_ln_g, v_conv_ln_b, v_w_out, v_ln1_g, v_ln1_b, v_w_gate, v_w_up, v_w_down, v_ln2_g, v_ln2_b):
    xs = x[0]
    tgt = loss_target[0]

    (win_b, wout_b, wgt_b, wut_b, wd_b, cw_b, wcat, wcatt, bs_full) = _prep_weights(
        w_in[0], w_out[0], w_gate[0].T, w_up[0].T, w_down[0], conv_w[0], w_s[0], b_s[0])
    win_g, wout_g, cw_g = _exchange("gather_mix_weights", [], [win_b, wout_b, cw_b])
    win_g = win_g.reshape(2 * D_MODEL, D_MODEL)
    wout_g = wout_g.reshape(D_MODEL, D_MODEL)
    cw = jnp.transpose(cw_g[:, :, :D_CONV // N_DEV], (1, 0, 2)).reshape(CONV_ROWS, D_CONV)

    (proj, ycat, n1, rstd1, phi, y_conv), (wgt_g, wut_g, wd_g) = _fwd_mix(
        xs, win_g, wout_g, sgu_ln_g, sgu_ln_b, wcat, bs_full, cw, conv_b, conv_ln_g, conv_ln_b, TOKEN_TILE_FWD_MIX,
        _Exchange([], [wgt_b, wut_b, wd_b]))
    wgt_g = wgt_g.reshape(D_FF, D_MODEL)
    wut_g = wut_g.reshape(D_FF, D_MODEL)
    wd_g = wd_g.reshape(D_FF, D_MODEL)
    gate, up, hh, x1b, dr2, vout = _fwd_mlp(n1, tgt, ln1_g, ln1_b, ln2_g, ln2_b, wgt_g, wut_g, wd_g, TOKEN_TILE_FWD_MLP)

    dgate, dup, dr1, vmlp = _bwd_mlp(dr2, gate, up, n1, rstd1, ln1_g, wgt_g, wut_g, wd_g, TOKEN_TILE_BWD_MLP)
    tk = TOKEN_TILE_WGRAD
    g_wgt = _wgrad("wgrad_gate", dgate, x1b, N_DEV, tk)
    g_wut = _wgrad("wgrad_up", dup, x1b, N_DEV, tk)
    g_wd = _wgrad("wgrad_down", hh, dr2, N_DEV, tk)
    g_wout = _wgrad("wgrad_out", ycat, dr1, N_DEV, tk)
    (gx, dproj, gws, gbs, gcw, vmix), (r_wgt, r_wut, r_wd, r_wout) = _bwd_mix(
        dr1, proj, phi, y_conv, win_g, wout_g, sgu_ln_g, sgu_ln_b, wcat, wcatt, bs_full, cw, conv_ln_g, conv_ln_b,
        TOKEN_TILE_BWD_MIX, _Exchange([g_wgt, g_wut, g_wd, g_wout], []))
    half = D_MODEL // 2
    g_win_a, (gws8, gbs8, gcw8, vmix8, vmlp8, vout8) = _wgrad(
        "wgrad_in_a", dproj, xs, N_DEV, tk, _Exchange([], [gws, gbs, gcw, vmix, vmlp, vout]), b_cols=(0, half))
    g_win_b, (r_win_a,) = _wgrad("wgrad_in_b", dproj, xs, N_DEV, tk, _Exchange([g_win_a], []), b_cols=(1, half))
    (r_win_b,) = _exchange("exchange_grad_in", [g_win_b], [])
    (u_gate, u_up, u_down), _ = _staged_call(
        "adamw_mlp",
        [_adamw_shard_group([r_wgt], w_gate[0].T, m_w_gate[0].T, v_w_gate[0].T, False),
         _adamw_shard_group([r_wut], w_up[0].T, m_w_up[0].T, v_w_up[0].T, False),
         _adamw_shard_group([r_wd], w_down[0], m_w_down[0], v_w_down[0], False)], 56)
    (u_in, u_out), _ = _staged_call(
        "adamw_mix",
        [_adamw_shard_group([r_win_a, r_win_b], w_in[0], m_w_in[0], v_w_in[0], True),
         _adamw_shard_group([r_wout], w_out[0], m_w_out[0], v_w_out[0], False)], 32)
    big = {"w_in": u_in, "w_out": u_out, "w_gate": u_gate, "w_up": u_up, "w_down": u_down}
    small_in = {
        "sgu_ln_g": (sgu_ln_g, m_sgu_ln_g, v_sgu_ln_g), "sgu_ln_b": (sgu_ln_b, m_sgu_ln_b, v_sgu_ln_b),
        "w_s": tuple(a.reshape(N_HEADS * CHUNK, CHUNK) for a in (w_s, m_w_s, v_w_s)),
        "b_s": (b_s[0], m_b_s[0], v_b_s[0]),
        "conv_b": (conv_b, m_conv_b, v_conv_b), "conv_ln_g": (conv_ln_g, m_conv_ln_g, v_conv_ln_g),
        "conv_ln_b": (conv_ln_b, m_conv_ln_b, v_conv_ln_b),
        "ln1_g": (ln1_g, m_ln1_g, v_ln1_g), "ln1_b": (ln1_b, m_ln1_b, v_ln1_b),
        "ln2_g": (ln2_g, m_ln2_g, v_ln2_g), "ln2_b": (ln2_b, m_ln2_b, v_ln2_b),
    }
    loss11, gcw_full, small = _finish_small(gws8, gbs8, gcw8, vmix8, vmlp8, vout8, small_in)

    me = 4 * lax.axis_index("x") + 2 * lax.axis_index("y") + lax.axis_index("c")
    g_cw = lax.dynamic_slice(gcw_full, (0, me * (D_CONV // N_DEV)), (CONV_WIDTH, D_CONV // N_DEV))
    d_cw, m_cw, v_cw = _adamw_plain("adamw_conv_w", g_cw, conv_w[0], m_conv_w[0], v_conv_w[0])

    shapes = {"w_s": w_s.shape, "b_s": b_s.shape}
    out = {}
    for n, r in big.items():
        out[n] = tuple((a.T if n in ("w_gate", "w_up") else a)[None] for a in r)
    for n, r in small.items():
        out[n] = tuple(a.reshape(shapes[n]) for a in r) if n in shapes else tuple(r)
    out["conv_w"] = tuple(a[None] for a in (g_cw, d_cw, m_cw, v_cw))

    order = ["w_in", "sgu_ln_g", "sgu_ln_b", "w_s", "b_s", "conv_w", "conv_b", "conv_ln_g", "conv_ln_b", "w_out",
             "ln1_g", "ln1_b", "w_gate", "w_up", "w_down", "ln2_g", "ln2_b"]
    loss = loss11[0, 0]
    return (loss, gx[None], *[out[n][0] for n in order], *[out[n][1] for n in order],
            *[out[n][2] for n in order], *[out[n][3] for n in order])
```

```python
import jax
import jax.numpy as jnp
from jax import lax
from jax.experimental import pallas as pl
from jax.experimental.pallas import tpu as pltpu

F32 = jnp.float32
BF16 = jnp.bfloat16

D_MODEL = 1024
D_SGU = 512
D_CONV = 512
N_HEADS = 8
CHUNK = 128
CONV_WIDTH = 31
CONV_ROWS = 32
HALO = 32
D_FF = 2816
N_DEV = 8
FF_SHARD = D_FF // N_DEV
ALPHA = (2.0 * 1) ** 0.25
LN_EPS = 1e-5
INV_SQRT2 = 0.7071067811865476
INV_SQRT_2PI = 0.3989422804014327

ADAM_LR = 0.001
ADAM_B1 = 0.9
ADAM_B2 = 0.999
ADAM_EPS = 1e-08
ADAM_WD = 0.01
ADAM_STEP = 10

MXU_COLS = 256
SUBLANES = 8
CONV_ROW_BLOCK = 32
WGRAD_ROW_BLOCK = 32
SHIFT_ROWS = HALO - SUBLANES
MIB = 1024 * 1024

HBM = pl.BlockSpec(memory_space=pltpu.HBM)
ANY = pl.BlockSpec(memory_space=pl.ANY)
MESH = pl.DeviceIdType.MESH


def _params(vmem_mib, grid_dims=0):
    kw = dict(vmem_limit_bytes=vmem_mib * MIB)
    if grid_dims:
        kw["dimension_semantics"] = ("arbitrary",) * grid_dims
    return pltpu.CompilerParams(**kw)


def _full(shape):
    return pl.BlockSpec(shape, lambda i: (0,) * len(shape))


def _dot(a, b):
    return jnp.dot(a, b, preferred_element_type=F32)


def _dot_nt(a, b):
    return lax.dot_general(a, b, (((1,), (1,)), ((), ())), preferred_element_type=F32)


def _dot_tn(a, b):
    return lax.dot_general(a, b, (((0,), (0,)), ((), ())), preferred_element_type=F32)


def _normal_cdf(x):
    return 0.5 * (1.0 + lax.erf(x * INV_SQRT2))


def _gelu_grad(x, cdf):
    return cdf + x * jnp.exp(-0.5 * x * x) * INV_SQRT_2PI


def _ln_fwd(v):
    mu = jnp.mean(v, axis=-1, keepdims=True)
    d = v - mu
    var = jnp.mean(d * d, axis=-1, keepdims=True)
    rstd = lax.rsqrt(var + LN_EPS)
    return d * rstd, rstd


def _ln_bwd(dyhat, yhat, rstd):
    m1 = jnp.mean(dyhat, axis=-1, keepdims=True)
    m2 = jnp.mean(dyhat * yhat, axis=-1, keepdims=True)
    return rstd * (dyhat - m1 - yhat * m2)


def _colsum(v):
    return jnp.sum(v, axis=0, keepdims=True)


def _head_pair_stack(v, lo):
    return jnp.concatenate([jnp.where(lo, v, 0.0), jnp.where(lo, 0.0, v)], axis=0).astype(BF16)


def _lo_mask():
    return lax.broadcasted_iota(jnp.int32, (CHUNK, CHUNK), 1) < (CHUNK // 2)


def _head_selector():
    head = lax.broadcasted_iota(jnp.int32, (N_HEADS, D_SGU), 0)
    lane = lax.broadcasted_iota(jnp.int32, (N_HEADS, D_SGU), 1)
    width = D_SGU // N_HEADS
    return ((lane >= head * width) & (lane < (head + 1) * width)).astype(F32)


def _shifted_copies(pad_ref, sh_ref, rows):
    for r in range(1, SUBLANES):
        sh_ref[r - 1, 0:rows, :] = pad_ref[pl.ds(r, rows), :]


def _tap_groups(offset_of_tap):
    groups = {}
    for k in range(CONV_WIDTH):
        o = offset_of_tap(k)
        groups.setdefault(o % SUBLANES, []).append((k, o // SUBLANES))
    return groups


def _tap_window(pad_ref, sh_ref, r, taps, row0, rows):
    q0 = min(q for _, q in taps)
    q1 = max(q for _, q in taps)
    src = pad_ref if r == 0 else sh_ref.at[r - 1]
    win = src[pl.ds(row0 + SUBLANES * q0, SUBLANES * (q1 - q0) + rows), :]
    return win, [(k, SUBLANES * (q - q0)) for k, q in taps]


def _causal_conv(pad_ref, sh_ref, w_ref, out_ref, rows, offset_of_tap, bias=None):
    groups = _tap_groups(offset_of_tap)

    def block(b, carry):
        row0 = pl.multiple_of(b * CONV_ROW_BLOCK, CONV_ROW_BLOCK)
        if bias is None:
            acc = jnp.zeros((CONV_ROW_BLOCK, D_CONV), F32)
        else:
            acc = jnp.broadcast_to(bias, (CONV_ROW_BLOCK, D_CONV))
        for r, taps in groups.items():
            win, starts = _tap_window(pad_ref, sh_ref, r, taps, row0, CONV_ROW_BLOCK)
            for k, s in starts:
                acc = acc + w_ref[k:k + 1, :] * win[s:s + CONV_ROW_BLOCK, :]
        out_ref[pl.ds(row0, CONV_ROW_BLOCK), :] = acc
        return carry

    lax.fori_loop(0, rows // CONV_ROW_BLOCK, block, 0)


def _conv_weight_grad(dy_ref, pad_ref, sh_ref, acc_ref, rows, offset_of_tap):
    groups = _tap_groups(offset_of_tap)
    for r, taps in groups.items():

        def block(b, parts, r=r, taps=taps):
            row0 = pl.multiple_of(b * WGRAD_ROW_BLOCK, WGRAD_ROW_BLOCK)
            dyb = dy_ref[pl.ds(row0, WGRAD_ROW_BLOCK), :]
            win, starts = _tap_window(pad_ref, sh_ref, r, taps, row0, WGRAD_ROW_BLOCK)
            out = []
            for part, (_, s) in zip(parts, starts):
                pr = dyb * win[s:s + WGRAD_ROW_BLOCK, :]
                out.append(part + pr.reshape(WGRAD_ROW_BLOCK // SUBLANES, SUBLANES, D_CONV).sum(axis=0))
            return tuple(out)

        zeros = tuple(jnp.zeros((SUBLANES, D_CONV), F32) for _ in taps)
        parts = lax.fori_loop(0, rows // WGRAD_ROW_BLOCK, block, zeros)
        for part, (k, _) in zip(parts, taps):
            acc_ref[k] += part


def _prep_weights(w_in, w_out, w_gate_t, w_up_t, w_down, conv_w, w_s, b_s):
    def compute(ins, outs):
        win_ref, wout_ref, wgt_ref, wut_ref, wd_ref, cw_ref, ws_ref, bs_ref = ins
        win_o, wout_o, wgt_o, wut_o, wd_o, cw_o, wcat_o, wcatt_o, bsf_o = outs
        win_o[...] = win_ref[...].T.astype(BF16)
        wout_o[...] = wout_ref[...].astype(BF16)
        wgt_o[...] = wgt_ref[...].astype(BF16)
        wut_o[...] = wut_ref[...].astype(BF16)
        wd_o[...] = wd_ref[...].astype(BF16)
        cw_o[...] = jnp.zeros(cw_o.shape, F32)
        cw_o[0:CONV_WIDTH, 0:D_CONV // N_DEV] = cw_ref[...]
        row = lax.broadcasted_iota(jnp.int32, (CHUNK, CHUNK), 0)
        col = lax.broadcasted_iota(jnp.int32, (CHUNK, CHUNK), 1)
        causal = row >= col
        for h in range(N_HEADS):
            w = jnp.where(causal, ws_ref[h], 0.0)
            p, half = h // 2, (h % 2) * CHUNK
            wcat_o[p, :, half:half + CHUNK] = w.astype(BF16)
            wcatt_o[p, :, half:half + CHUNK] = w.T.astype(BF16)
        bsf_o[...] = lax.dot_general(bs_ref[...], _head_selector(), (((0,), (0,)), ((), ())),
                                     preferred_element_type=F32, precision=lax.Precision.HIGHEST)

    S = jax.ShapeDtypeStruct
    out_shapes = [S((256, D_MODEL), BF16), S((128, D_MODEL), BF16), S((FF_SHARD, D_MODEL), BF16),
                  S((FF_SHARD, D_MODEL), BF16), S((FF_SHARD, D_MODEL), BF16), S((CONV_ROWS, 128), F32),
                  S((4, CHUNK, 2 * CHUNK), BF16), S((4, CHUNK, 2 * CHUNK), BF16), S((CHUNK, D_SGU), F32)]
    (res,), _ = _staged_call(
        "prep_weights", [([w_in, w_out, w_gate_t, w_up_t, w_down, conv_w, w_s, b_s], out_shapes, compute)], 32)
    return res


def _mesh_position():
    x, y, c = lax.axis_index("x"), lax.axis_index("y"), lax.axis_index("c")
    return x, y, c


def _peers(x, y, c):
    out = []
    for k in range(1, N_DEV):
        px = 1 - x if (k >> 2) & 1 else x
        py = 1 - y if (k >> 1) & 1 else y
        pc = 1 - c if k & 1 else c
        out.append(((px, py, pc), 4 * px + 2 * py + pc))
    return out


class _Exchange:
    def __init__(self, scatter, gather):
        self.arrays = list(scatter) + list(gather)
        self.n_sc = len(scatter)
        self.n = len(self.arrays)
        self.out_shape = [jax.ShapeDtypeStruct(a.shape if k < self.n_sc else (N_DEV,) + a.shape, a.dtype)
                          for k, a in enumerate(self.arrays)]
        n_remote = self.n * (N_DEV - 1)
        self.scratch = [pltpu.SemaphoreType.DMA((n_remote,)), pltpu.SemaphoreType.DMA((n_remote,)),
                        pltpu.SemaphoreType.DMA((self.n,))] if self.n else []

    def _copies(self, src, dst, sems):
        send_sems, recv_sems, local_sems = sems
        x, y, c = _mesh_position()
        me = 4 * x + 2 * y + c
        locals_, first, arrivals, passed, last = [], [], [], [], []

        def remote(a, k, src_ref, slot, to):
            s = a * (N_DEV - 1) + k
            return pltpu.make_async_remote_copy(src_ref=src_ref, dst_ref=dst[a].at[slot], send_sem=send_sems.at[s],
                                                recv_sem=recv_sems.at[s], device_id=to, device_id_type=MESH)

        for a in range(self.n):
            if a < self.n_sc:
                locals_.append(pltpu.make_async_copy(src[a].at[me], dst[a].at[me], local_sems.at[a]))
                for k, (peer, pid) in enumerate(_peers(x, y, c)):
                    first.append(remote(a, k, src[a].at[pid], me, peer))
                    last.append(remote(a, k, src[a].at[pid], pid, peer))
                continue
            locals_.append(pltpu.make_async_copy(src[a], dst[a].at[me], local_sems.at[a]))
            sibling, sib_id = (x, y, 1 - c), 4 * x + 2 * y + (1 - c)
            chips = [(1 - x, y), (x, 1 - y), (1 - x, 1 - y)]
            first.append(remote(a, 0, src[a], me, sibling))
            last.append(remote(a, 0, src[a], sib_id, sibling))
            for j, (px, py) in enumerate(chips):
                same, other = 4 * px + 2 * py + c, 4 * px + 2 * py + (1 - c)
                first.append(remote(a, 1 + j, src[a], me, (px, py, c)))
                arrivals.append(remote(a, 1 + j, src[a], same, (px, py, c)))
                passed.append(remote(a, 4 + j, dst[a].at[same], same, sibling))
                last.append(remote(a, 4 + j, dst[a].at[other], other, sibling))
        return locals_, first, arrivals, passed, last

    def start(self, src, dst, sems):
        if not self.n:
            return
        locals_, first, _, _, _ = self._copies(src, dst, sems)
        for cp in locals_ + first:
            cp.start()

    def forward(self, src, dst, sems):
        if self.n == self.n_sc:
            return
        _, _, arrivals, passed, _ = self._copies(src, dst, sems)
        for arrived, cp in zip(arrivals, passed):
            arrived.wait_recv()
            cp.start()

    def wait(self, src, dst, sems):
        if not self.n:
            return
        locals_, first, _, passed, last = self._copies(src, dst, sems)
        for cp in last:
            cp.wait_recv()
        for cp in first + passed:
            cp.wait_send()
        for cp in locals_:
            cp.wait()


def _exchange(name, scatter, gather):
    ex = _Exchange(scatter, gather)
    n = ex.n

    def body(*refs):
        src, dst, sems = refs[:n], refs[n:2 * n], refs[2 * n:]
        ex.start(src, dst, sems)
        ex.forward(src, dst, sems)
        ex.wait(src, dst, sems)

    return pl.pallas_call(
        body, name=name, out_shape=tuple(ex.out_shape), in_specs=[ANY] * n, out_specs=(ANY,) * n,
        scratch_shapes=ex.scratch,
    )(*ex.arrays)


def _forward_step(n_steps):
    return (11 * n_steps) // 16


def _hosted(ex, refs, n_in, n_out):
    ins, ex_src = refs[:n_in], refs[n_in:n_in + ex.n]
    rest = refs[n_in + ex.n:]
    outs, ex_dst = rest[:n_out], rest[n_out:n_out + ex.n]
    rest = rest[n_out + ex.n:]
    n_own = len(rest) - len(ex.scratch)
    return ins, outs, rest[:n_own], (ex_src, ex_dst, rest[n_own:])


def _fwd_mix(x, win_g, wout_g, sgu_g, sgu_b, wcat, bs_full, cw, cb, cg, cbeta, tm, ex):
    T = x.shape[0]
    nt = T // tm

    def body(*refs):
        ins, outs, scratch, ex_refs = _hosted(ex, refs, 11, 6)
        x_ref, win_ref, wout_ref, sg_ref, sb_ref, wcat_ref, bs_ref, cw_ref, cb_ref, cg_ref, cbeta_ref = ins
        proj_ref, ycat_ref, n1_ref, rstd1_ref, phi_ref, y_ref = outs
        hpad, hshift = scratch
        i = pl.program_id(0)

        @pl.when(i == 0)
        def _():
            ex.start(*ex_refs)

        xf = x_ref[...]
        xb = xf.astype(BF16)
        proj_ref[...] = _dot_nt(xb, win_ref[...])
        cdf = _normal_cdf(proj_ref[:, 0:2 * D_SGU])
        phi_ref[...] = cdf.astype(BF16)
        u = proj_ref[:, 0:D_SGU] * cdf[:, 0:D_SGU]
        v = proj_ref[:, D_SGU:2 * D_SGU] * cdf[:, D_SGU:2 * D_SGU]
        vhat, _ = _ln_fwd(v)
        vn = vhat * sg_ref[...] + sb_ref[...]
        lo = _lo_mask()
        for c in range(tm // CHUNK):
            rows = slice(CHUNK * c, CHUNK * (c + 1))
            for p in range(4):
                lanes = slice(CHUNK * p, CHUNK * (p + 1))
                mixed = _dot(wcat_ref[p], _head_pair_stack(vn[rows, lanes], lo)) + bs_ref[:, lanes]
                ycat_ref[rows, lanes] = (u[rows, lanes] * mixed).astype(BF16)
        base = 2 * D_SGU
        a = proj_ref[:, base:base + D_CONV]
        g = proj_ref[:, base + D_CONV:base + 2 * D_CONV]

        @pl.when(i == 0)
        def _():
            hpad[0:HALO, :] = jnp.zeros((HALO, D_CONV), F32)

        hpad[HALO:HALO + tm, :] = a * jax.nn.sigmoid(g)
        _shifted_copies(hpad, hshift, tm + SHIFT_ROWS)
        _causal_conv(hpad, hshift, cw_ref, y_ref, tm, lambda k: HALO - (CONV_WIDTH - 1) + k, bias=cb_ref[...])
        hpad[0:HALO, :] = hpad[tm:tm + HALO, :]
        yhat, _ = _ln_fwd(y_ref[...])
        yn = yhat * cg_ref[...] + cbeta_ref[...]
        ycat_ref[:, D_SGU:D_SGU + D_CONV] = (yn * jax.nn.sigmoid(yn)).astype(BF16)
        r1 = ALPHA * xf + _dot(ycat_ref[...], wout_ref[...])
        n1, rstd1 = _ln_fwd(r1)
        n1_ref[...] = n1
        rstd1_ref[...] = rstd1

        @pl.when(i == _forward_step(nt))
        def _():
            ex.forward(*ex_refs)

        @pl.when(i == nt - 1)
        def _():
            ex.wait(*ex_refs)

    S = jax.ShapeDtypeStruct
    row = lambda w: pl.BlockSpec((tm, w), lambda i: (i, 0))
    res = pl.pallas_call(
        body, name="fwd_mix", grid=(nt,),
        in_specs=[row(D_MODEL), _full(win_g.shape), _full(wout_g.shape), _full(sgu_g.shape), _full(sgu_b.shape),
                  _full(wcat.shape), _full(bs_full.shape), _full(cw.shape), _full(cb.shape), _full(cg.shape),
                  _full(cbeta.shape)] + [ANY] * ex.n,
        out_specs=(row(2 * D_MODEL), row(D_MODEL), row(D_MODEL), row(1), row(2 * D_SGU), row(D_CONV)) + (ANY,) * ex.n,
        out_shape=(S((T, 2 * D_MODEL), F32), S((T, D_MODEL), BF16), S((T, D_MODEL), F32), S((T, 1), F32),
                   S((T, 2 * D_SGU), BF16), S((T, D_CONV), F32), *ex.out_shape),
        scratch_shapes=[pltpu.VMEM((tm + HALO, D_CONV), F32),
                        pltpu.VMEM((SUBLANES - 1, tm + SHIFT_ROWS, D_CONV), F32)] + ex.scratch,
        compiler_params=_params(56, 1),
    )(x, win_g, wout_g, sgu_g, sgu_b, wcat, bs_full, cw, cb, cg, cbeta, *ex.arrays)
    return res[:6], res[6:]


def _load_resident(pairs, sems):
    cps = [pltpu.make_async_copy(s, d, sems.at[k]) for k, (s, d) in enumerate(pairs)]
    for cp in cps:
        cp.start()
    for cp in cps:
        cp.wait()


def _fwd_mlp(n1, tgt, l1g, l1b, l2g, l2b, wgt, wut, wd, tm):
    T = n1.shape[0]
    nt = T // tm
    nf = D_FF // MXU_COLS

    def body(n1_ref, tgt_ref, l1g_ref, l1b_ref, l2g_ref, l2b_ref, wg_hbm, wu_hbm, wd_hbm,
             gate_ref, up_ref, hh_ref, dr2_ref, stat_ref, wg_s, wu_s, wd_s, sems):
        i = pl.program_id(0)

        @pl.when(i == 0)
        def _():
            _load_resident([(wg_hbm, wg_s), (wu_hbm, wu_s), (wd_hbm, wd_s)], sems)
            stat_ref[...] = jnp.zeros(stat_ref.shape, F32)

        x1 = n1_ref[...] * l1g_ref[...] + l1b_ref[...]
        x1b = x1.astype(BF16)
        for f in range(nf):
            cols = slice(MXU_COLS * f, MXU_COLS * (f + 1))
            gt = _dot_nt(x1b, wg_s[cols, :])
            ut = _dot_nt(x1b, wu_s[cols, :])
            gate_ref[:, cols] = gt.astype(BF16)
            up_ref[:, cols] = ut.astype(BF16)
            hh_ref[:, cols] = (gt * jax.nn.sigmoid(gt) * ut).astype(BF16)
        r2 = ALPHA * x1 + _dot(hh_ref[...], wd_s[...])
        n2, rstd2 = _ln_fwd(r2)
        x2 = n2 * l2g_ref[...] + l2b_ref[...]
        diff = x2 - tgt_ref[...]
        dx2 = diff * (1.0 / D_MODEL)
        stat_ref[0:1, :] += _colsum(diff * diff)
        stat_ref[1:2, :] += _colsum(dx2 * n2)
        stat_ref[2:3, :] += _colsum(dx2)
        dr2_ref[...] = _ln_bwd(dx2 * l2g_ref[...], n2, rstd2)

    S = jax.ShapeDtypeStruct
    row = lambda w: pl.BlockSpec((tm, w), lambda i: (i, 0))
    vec = _full((1, D_MODEL))
    return pl.pallas_call(
        body, name="fwd_mlp", grid=(nt,),
        in_specs=[row(D_MODEL), row(D_MODEL), vec, vec, vec, vec, ANY, ANY, ANY],
        out_specs=(row(D_FF), row(D_FF), row(D_FF), row(D_MODEL), _full((8, D_MODEL))),
        out_shape=(S((T, D_FF), BF16), S((T, D_FF), BF16), S((T, D_FF), BF16), S((T, D_MODEL), F32),
                   S((8, D_MODEL), F32)),
        scratch_shapes=[pltpu.VMEM((D_FF, D_MODEL), BF16)] * 3 + [pltpu.SemaphoreType.DMA((3,))],
        compiler_params=_params(56, 1),
    )(n1, tgt, l1g, l1b, l2g, l2b, wgt, wut, wd)


def _bwd_mlp(dr2, gate, up, n1, rstd1, l1g, wgt, wut, wd, tm):
    T = n1.shape[0]
    nt = T // tm
    nf = D_FF // MXU_COLS

    def body(dr2_ref, gate_ref, up_ref, n1_ref, rstd1_ref, l1g_ref, wg_hbm, wu_hbm, wd_hbm,
             dgate_ref, dup_ref, dr1_ref, stat_ref, wg_s, wu_s, wd_s, sems):
        i = pl.program_id(0)

        @pl.when(i == 0)
        def _():
            _load_resident([(wg_hbm, wg_s), (wu_hbm, wu_s), (wd_hbm, wd_s)], sems)
            stat_ref[...] = jnp.zeros(stat_ref.shape, F32)

        dr2 = dr2_ref[...]
        dr2b = dr2.astype(BF16)
        for f in range(nf):
            cols = slice(MXU_COLS * f, MXU_COLS * (f + 1))
            dhh = _dot_nt(dr2b, wd_s[cols, :])
            gt = gate_ref[:, cols].astype(F32)
            ut = up_ref[:, cols].astype(F32)
            sg = jax.nn.sigmoid(gt)
            dgate_ref[:, cols] = (dhh * ut * (sg * (1.0 + gt * (1.0 - sg)))).astype(BF16)
            dup_ref[:, cols] = (dhh * (gt * sg)).astype(BF16)
        dx1 = ALPHA * dr2 + _dot(dgate_ref[...], wg_s[...]) + _dot(dup_ref[...], wu_s[...])
        n1 = n1_ref[...]
        stat_ref[0:1, :] += _colsum(dx1 * n1)
        stat_ref[1:2, :] += _colsum(dx1)
        dr1_ref[...] = _ln_bwd(dx1 * l1g_ref[...], n1, rstd1_ref[...])

    S = jax.ShapeDtypeStruct
    row = lambda w: pl.BlockSpec((tm, w), lambda i: (i, 0))
    return pl.pallas_call(
        body, name="bwd_mlp", grid=(nt,),
        in_specs=[row(D_MODEL), row(D_FF), row(D_FF), row(D_MODEL), row(1), _full((1, D_MODEL)), ANY, ANY, ANY],
        out_specs=(row(D_FF), row(D_FF), row(D_MODEL), _full((8, D_MODEL))),
        out_shape=(S((T, D_FF), BF16), S((T, D_FF), BF16), S((T, D_MODEL), F32), S((8, D_MODEL), F32)),
        scratch_shapes=[pltpu.VMEM((D_FF, D_MODEL), BF16)] * 3 + [pltpu.SemaphoreType.DMA((3,))],
        compiler_params=_params(56, 1),
    )(dr2, gate, up, n1, rstd1, l1g, wgt, wut, wd)


def _bwd_mix(dr1, proj, phi, y, win_g, wout_g, sgu_g, sgu_b, wcat, wcatt, bs_full, cw, cg, cbeta, tm, ex):
    T = dr1.shape[0]
    nt = T // tm
    halo_blocks = tm // HALO

    def body(*refs):
        ins, outs, scratch, ex_refs = _hosted(ex, refs, 15, 6)
        (dr1_ref, proj_ref, halo_ref, phi_ref, y_ref, win_ref, wout_ref, sg_ref, sb_ref, wcat_ref, wcatt_ref, bs_ref,
         cw_ref, cg_ref, cbeta_ref) = ins
        gx_ref, dproj_ref, gws_out, gbs_out, gcw_ref, vec_ref = outs
        hpad, hshift, dypad, dyshift, dhbuf, dubuf, dvnbuf, gcw_acc, gws_ref, gbs_ref = scratch
        i = pl.program_id(0)
        tile = nt - 1 - i

        @pl.when(i == 0)
        def _():
            ex.start(*ex_refs)
            gws_ref[...] = jnp.zeros(gws_ref.shape, F32)
            gbs_ref[...] = jnp.zeros(gbs_ref.shape, F32)
            gcw_ref[...] = jnp.zeros(gcw_ref.shape, F32)
            vec_ref[...] = jnp.zeros(vec_ref.shape, F32)
            gcw_acc[...] = jnp.zeros(gcw_acc.shape, F32)
            dypad[tm:tm + HALO, :] = jnp.zeros((HALO, D_CONV), F32)

        dr1 = dr1_ref[...]
        dycat = _dot_nt(dr1.astype(BF16), wout_ref[...])
        pu = proj_ref[:, 0:D_SGU]
        pv = proj_ref[:, D_SGU:2 * D_SGU]
        cdf_u = phi_ref[:, 0:D_SGU].astype(F32)
        cdf_v = phi_ref[:, D_SGU:2 * D_SGU].astype(F32)
        u = pu * cdf_u
        vhat, rstd_v = _ln_fwd(pv * cdf_v)
        vn = vhat * sg_ref[...] + sb_ref[...]
        lo = _lo_mask()
        for c in range(tm // CHUNK):
            rows = slice(CHUNK * c, CHUNK * (c + 1))
            for p in range(4):
                lanes = slice(CHUNK * p, CHUNK * (p + 1))
                vstack = _head_pair_stack(vn[rows, lanes], lo)
                mixed = _dot(wcat_ref[p], vstack) + bs_ref[:, lanes]
                d_a = dycat[rows, lanes]
                dubuf[rows, lanes] = d_a * mixed
                dm = d_a * u[rows, lanes]
                gbs_ref[:, lanes] += dm
                dstack = _head_pair_stack(dm, lo)
                gws_ref[2 * CHUNK * p:2 * CHUNK * (p + 1), :] += _dot_nt(dstack, vn[rows, lanes].astype(BF16))
                dvnbuf[rows, lanes] = _dot(wcatt_ref[p], dstack)
        dvn = dvnbuf[...]
        vec_ref[0:1, :] += _colsum(dvn * vhat)
        vec_ref[1:2, :] += _colsum(dvn)
        dv = _ln_bwd(dvn * sg_ref[...], vhat, rstd_v)
        dproj_ref[:, 0:D_SGU] = (dubuf[...] * _gelu_grad(pu, cdf_u)).astype(BF16)
        dproj_ref[:, D_SGU:2 * D_SGU] = (dv * _gelu_grad(pv, cdf_v)).astype(BF16)
        base = 2 * D_SGU
        a = proj_ref[:, base:base + D_CONV]
        sgm = jax.nn.sigmoid(proj_ref[:, base + D_CONV:base + 2 * D_CONV])
        h_before = halo_ref[:, 0:D_CONV] * jax.nn.sigmoid(halo_ref[:, D_CONV:2 * D_CONV])
        hpad[0:HALO, :] = jnp.where(tile > 0, h_before, 0.0)
        hpad[HALO:HALO + tm, :] = a * sgm
        _shifted_copies(hpad, hshift, tm + SHIFT_ROWS)
        h_offset = lambda k: HALO - (CONV_WIDTH - 1) + k
        yhat, rstd_y = _ln_fwd(y_ref[...])
        yn = yhat * cg_ref[...] + cbeta_ref[...]
        s = jax.nn.sigmoid(yn)
        dyn = dycat[:, D_SGU:D_SGU + D_CONV] * (s * (1.0 + yn * (1.0 - s)))
        vec_ref[3:4, :] += _colsum(dyn * yhat)
        vec_ref[4:5, :] += _colsum(dyn)
        dy = _ln_bwd(dyn * cg_ref[...], yhat, rstd_y)
        vec_ref[2:3, :] += _colsum(dy)
        dypad[0:tm, :] = dy
        _shifted_copies(dypad, dyshift, tm + SHIFT_ROWS)
        _causal_conv(dypad, dyshift, cw_ref, dhbuf, tm, lambda k: (CONV_WIDTH - 1) - k)
        _conv_weight_grad(dypad, hpad, hshift, gcw_acc, tm, h_offset)
        dypad[tm:tm + HALO, :] = dypad[0:HALO, :]
        dh = dhbuf[...]
        dproj_ref[:, base:base + D_CONV] = (dh * sgm).astype(BF16)
        dproj_ref[:, base + D_CONV:base + 2 * D_CONV] = (dh * a * sgm * (1.0 - sgm)).astype(BF16)
        gx_ref[...] = ALPHA * dr1 + _dot(dproj_ref[...], win_ref[...])

        @pl.when(i == _forward_step(nt))
        def _():
            ex.forward(*ex_refs)

        @pl.when(i == nt - 1)
        def _():
            gcw_ref[...] = gcw_acc[...].sum(axis=1)
            gws_out[...] = gws_ref[...].astype(BF16)
            gbs_out[...] = lax.dot_general(_head_selector(), gbs_ref[...], (((1,), (1,)), ((), ())),
                                           preferred_element_type=F32, precision=lax.Precision.HIGHEST)
            ex.wait(*ex_refs)

    S = jax.ShapeDtypeStruct
    row = lambda w: pl.BlockSpec((tm, w), lambda i: (nt - 1 - i, 0))
    halo = pl.BlockSpec((HALO, D_MODEL), lambda i: (jnp.maximum((nt - 1 - i) * halo_blocks - 1, 0), 1))
    res = pl.pallas_call(
        body, name="bwd_mix", grid=(nt,),
        in_specs=[row(D_MODEL), row(2 * D_MODEL), halo, row(2 * D_SGU), row(D_CONV), _full(win_g.shape),
                  _full(wout_g.shape), _full(sgu_g.shape), _full(sgu_b.shape), _full(wcat.shape), _full(wcatt.shape),
                  _full(bs_full.shape), _full(cw.shape), _full(cg.shape), _full(cbeta.shape)]
        + [ANY] * ex.n,
        out_specs=(row(D_MODEL), row(2 * D_MODEL), _full((N_HEADS * CHUNK, CHUNK)), _full((N_HEADS, CHUNK)),
                   _full((CONV_ROWS, D_CONV)), _full((8, D_CONV))) + (ANY,) * ex.n,
        out_shape=(S((T, D_MODEL), F32), S((T, 2 * D_MODEL), BF16), S((N_HEADS * CHUNK, CHUNK), BF16),
                   S((N_HEADS, CHUNK), F32), S((CONV_ROWS, D_CONV), F32), S((8, D_CONV), F32), *ex.out_shape),
        scratch_shapes=[pltpu.VMEM((tm + HALO, D_CONV), F32), pltpu.VMEM((SUBLANES - 1, tm + SHIFT_ROWS, D_CONV), F32),
                        pltpu.VMEM((tm + HALO, D_CONV), F32), pltpu.VMEM((SUBLANES - 1, tm + SHIFT_ROWS, D_CONV), F32),
                        pltpu.VMEM((tm, D_CONV), F32), pltpu.VMEM((tm, D_SGU), F32),
                        pltpu.VMEM((tm, D_SGU), F32), pltpu.VMEM((CONV_ROWS, 8, D_CONV), F32),
                        pltpu.VMEM((N_HEADS * CHUNK, CHUNK), F32), pltpu.VMEM((CHUNK, D_SGU), F32)] + ex.scratch,
        compiler_params=_params(56, 1),
    )(dr1, proj, proj, phi, y, win_g, wout_g, sgu_g, sgu_b, wcat, wcatt, bs_full, cw, cg, cbeta, *ex.arrays)
    return res[:6], res[6:]


def _wgrad(name, a, b, blocks, tk, ex=None, b_cols=None, b_affine=None):
    T, M = a.shape
    col, N = b_cols or (0, b.shape[1])
    nk = T // tk
    out_shape = (blocks, M // blocks, N)
    ex = ex or _Exchange([], [])
    affine = list(b_affine or [])

    def body(*refs):
        ins, (o_ref,), (acc,), ex_refs = _hosted(ex, refs, 2 + len(affine), 1)
        a_ref, b_ref = ins[:2]
        i = pl.program_id(0)

        @pl.when(i == 0)
        def _():
            ex.start(*ex_refs)
            acc[...] = jnp.zeros(acc.shape, F32)

        right = b_ref[...]
        if affine:
            right = right * ins[2][...] + ins[3][...]
        acc[...] += _dot_tn(a_ref[...].astype(BF16), right.astype(BF16))

        @pl.when(i == _forward_step(nk))
        def _():
            ex.forward(*ex_refs)

        @pl.when(i == nk - 1)
        def _():
            o_ref[...] = acc[...].astype(BF16)
            ex.wait(*ex_refs)

    res = pl.pallas_call(
        body, name=name, grid=(nk,),
        in_specs=[pl.BlockSpec((tk, M), lambda i: (i, 0)), pl.BlockSpec((tk, N), lambda i: (i, col))]
        + [_full((1, N))] * len(affine) + [ANY] * ex.n,
        out_specs=(_full((M, N)),) + (ANY,) * ex.n,
        out_shape=(jax.ShapeDtypeStruct((M, N), BF16), *ex.out_shape),
        scratch_shapes=[pltpu.VMEM((M, N), F32)] + ex.scratch,
        compiler_params=_params(56, 1),
    )(a, b, *affine, *ex.arrays)
    g = res[0].reshape(out_shape)
    return (g, res[1:]) if ex.n else g


def _adamw(w, g, m, v):
    m2 = ADAM_B1 * m + (1.0 - ADAM_B1) * g
    v2 = ADAM_B2 * v + (1.0 - ADAM_B2) * (g * g)
    m_hat = m2 / (1.0 - ADAM_B1 ** ADAM_STEP)
    v_hat = v2 / (1.0 - ADAM_B2 ** ADAM_STEP)
    delta = -ADAM_LR * (m_hat / (jnp.sqrt(v_hat) + ADAM_EPS) + ADAM_WD * w)
    return delta, m2, v2


def _sum_partials(r_ref):
    g = r_ref[0].astype(F32)
    for s in range(1, N_DEV):
        g = g + r_ref[s].astype(F32)
    return g


def _staged_call(name, groups, vmem_mib, ex=None):
    ex = ex or _Exchange([], [])
    inputs = [a for ins, _, _ in groups for a in ins]
    out_shapes = [s for _, outs, _ in groups for s in outs]
    n_in, n_out = len(inputs), len(out_shapes)

    def body(*refs):
        ins, outs, scratch, ex_refs = _hosted(ex, refs, n_in, n_out)
        in_bufs, out_bufs, sems = scratch[:n_in], scratch[n_in:n_in + n_out], scratch[n_in + n_out]
        ex.start(*ex_refs)
        loads = [pltpu.make_async_copy(ins[k], in_bufs[k], sems.at[k]) for k in range(n_in)]
        stores = [pltpu.make_async_copy(out_bufs[k], outs[k], sems.at[n_in + k]) for k in range(n_out)]
        for cp in loads:
            cp.start()
        i0 = o0 = 0
        for g_ins, g_outs, compute in groups:
            i1, o1 = i0 + len(g_ins), o0 + len(g_outs)
            for cp in loads[i0:i1]:
                cp.wait()
            compute(in_bufs[i0:i1], out_bufs[o0:o1])
            for cp in stores[o0:o1]:
                cp.start()
            i0, o0 = i1, o1
        for cp in stores:
            cp.wait()
        ex.forward(*ex_refs)
        ex.wait(*ex_refs)

    scratch = ([pltpu.VMEM(a.shape, a.dtype) for a in inputs] + [pltpu.VMEM(s.shape, s.dtype) for s in out_shapes]
               + [pltpu.SemaphoreType.DMA((n_in + n_out,))] + ex.scratch)
    res = pl.pallas_call(
        body, name=name, out_shape=(*[pltpu.HBM(s.shape, s.dtype) for s in out_shapes], *ex.out_shape),
        in_specs=[HBM] * n_in + [ANY] * ex.n, out_specs=(HBM,) * n_out + (ANY,) * ex.n,
        scratch_shapes=scratch, compiler_params=_params(vmem_mib),
    )(*[pltpu.with_memory_space_constraint(a, pltpu.HBM) for a in inputs], *ex.arrays)
    per_group, o0 = [], 0
    for _, g_outs, _ in groups:
        per_group.append(list(res[o0:o0 + len(g_outs)]))
        o0 += len(g_outs)
    return per_group, res[n_out:]


def _adamw_shard_group(parts, w, m, v, transposed):
    n = len(parts)

    def compute(ins, outs):
        w_ref, m_ref, v_ref = ins[n:]
        lo = 0
        for r_ref in ins[:n]:
            g = _sum_partials(r_ref)
            cols = g.shape[1]
            if transposed:
                g, at = g.T, (slice(lo, lo + cols), slice(None))
            else:
                at = (slice(None), slice(lo, lo + cols))
            delta, m2, v2 = _adamw(w_ref[at], g, m_ref[at], v_ref[at])
            for o, val in zip(outs, (g, delta, m2, v2)):
                o[at] = val
            lo += cols

    return [*parts, w, m, v], [jax.ShapeDtypeStruct(w.shape, F32)] * 4, compute


def _finish_small(gws8, gbs8, gcw8, vmix8, vmlp8, vout8, small):
    names = ["sgu_ln_g", "sgu_ln_b", "w_s", "b_s", "conv_b", "conv_ln_g", "conv_ln_b", "ln1_g", "ln1_b", "ln2_g", "ln2_b"]
    flat = []
    for n in names:
        flat += list(small[n])

    def compute(ins, outs):
        gws_ref, gbs_ref, gcw_ref, vmix_ref, vmlp_ref, vout_ref = ins[:6]
        wmv = ins[6:]
        loss_o, gcw_o = outs[0], outs[1]
        outs = outs[2:]
        gws = _sum_partials(gws_ref)
        gbs = _sum_partials(gbs_ref)
        vmix = _sum_partials(vmix_ref)
        vmlp = _sum_partials(vmlp_ref)
        vout = _sum_partials(vout_ref)
        gcw_o[...] = _sum_partials(gcw_ref)
        loss = (0.5 / D_MODEL) * jnp.sum(vout[0:1, :], axis=1, keepdims=True)
        loss_o[...] = jnp.broadcast_to(loss, loss_o.shape)
        rows = lax.broadcasted_iota(jnp.int32, (N_HEADS * CHUNK, CHUNK), 0)
        cols = lax.broadcasted_iota(jnp.int32, (N_HEADS * CHUNK, CHUNK), 1)
        gws = jnp.where((rows & (CHUNK - 1)) >= cols, gws, 0.0)
        grads = {
            "sgu_ln_g": vmix[0:1, :], "sgu_ln_b": vmix[1:2, :], "w_s": gws, "b_s": gbs,
            "conv_b": vmix[2:3, :], "conv_ln_g": vmix[3:4, :], "conv_ln_b": vmix[4:5, :],
            "ln1_g": vmlp[0:1, :], "ln1_b": vmlp[1:2, :], "ln2_g": vout[1:2, :], "ln2_b": vout[2:3, :],
        }
        for k, n in enumerate(names):
            w_ref, m_ref, v_ref = wmv[3 * k:3 * k + 3]
            g = grads[n]
            delta, m2, v2 = _adamw(w_ref[...], g, m_ref[...], v_ref[...])
            outs[4 * k][...] = g
            outs[4 * k + 1][...] = delta
            outs[4 * k + 2][...] = m2
            outs[4 * k + 3][...] = v2

    S = jax.ShapeDtypeStruct
    out_shape = [S((SUBLANES, 128), F32), S((CONV_ROWS, D_CONV), F32)]
    for n in names:
        out_shape += [S(small[n][0].shape, F32)] * 4
    (res,), _ = _staged_call(
        "finish_small", [([gws8, gbs8, gcw8, vmix8, vmlp8, vout8, *flat], out_shape, compute)], 40)
    upd = {n: res[2 + 4 * k:6 + 4 * k] for k, n in enumerate(names)}
    return res[0], res[1], upd


def _adamw_plain(name, g, w, m, v):
    def compute(ins, outs):
        g_ref, w_ref, m_ref, v_ref = ins
        for o, val in zip(outs, _adamw(w_ref[...], g_ref[...], m_ref[...], v_ref[...])):
            o[...] = val

    (res,), _ = _staged_call(name, [([g, w, m, v], [jax.ShapeDtypeStruct(w.shape, F32)] * 3, compute)], 16)
    return res


TOKEN_TILE_FWD_MIX = 512
TOKEN_TILE_BWD_MIX = 256
TOKEN_TILE_FWD_MLP = 512
TOKEN_TILE_BWD_MLP = 512
TOKEN_TILE_WGRAD = 1024


def kernel(x, w_in, sgu_ln_g, sgu_ln_b, w_s, b_s, conv_w, conv_b, conv_ln_g, conv_ln_b, w_out, ln1_g, ln1_b, w_gate, w_up, w_down, ln2_g, ln2_b, loss_target, m_w_in, m_sgu_ln_g, m_sgu_ln_b, m_w_s, m_b_s, m_conv_w, m_conv_b, m_conv_ln_g, m_conv_ln_b, m_w_out, m_ln1_g, m_ln1_b, m_w_gate, m_w_up, m_w_down, m_ln2_g, m_ln2_b, v_w_in, v_sgu_ln_g, v_sgu_ln_b, v_w_s, v_b_s, v_conv_w, v_conv_b, v_conv_ln_g, v_conv_ln_b, v_w_out, v_ln1_g, v_ln1_b, v_w_gate, v_w_up, v_w_down, v_ln2_g, v_ln2_b):
    xs = x[0]
    tgt = loss_target[0]

    (win_b, wout_b, wgt_b, wut_b, wd_b, cw_b, wcat, wcatt, bs_full) = _prep_weights(
        w_in[0], w_out[0], w_gate[0].T, w_up[0].T, w_down[0], conv_w[0], w_s[0], b_s[0])
    win_g, wout_g, cw_g = _exchange("gather_mix_weights", [], [win_b, wout_b, cw_b])
    win_g = win_g.reshape(2 * D_MODEL, D_MODEL)
    wout_g = wout_g.reshape(D_MODEL, D_MODEL)
    cw = jnp.transpose(cw_g[:, :, :D_CONV // N_DEV], (1, 0, 2)).reshape(CONV_ROWS, D_CONV)

    (proj, ycat, n1, rstd1, phi, y_conv), (wgt_g, wut_g, wd_g) = _fwd_mix(
        xs, win_g, wout_g, sgu_ln_g, sgu_ln_b, wcat, bs_full, cw, conv_b, conv_ln_g, conv_ln_b, TOKEN_TILE_FWD_MIX,
        _Exchange([], [wgt_b, wut_b, wd_b]))
    wgt_g = wgt_g.reshape(D_FF, D_MODEL)
    wut_g = wut_g.reshape(D_FF, D_MODEL)
    wd_g = wd_g.reshape(D_FF, D_MODEL)
    gate, up, hh, dr2, vout = _fwd_mlp(n1, tgt, ln1_g, ln1_b, ln2_g, ln2_b, wgt_g, wut_g, wd_g, TOKEN_TILE_FWD_MLP)

    dgate, dup, dr1, vmlp = _bwd_mlp(dr2, gate, up, n1, rstd1, ln1_g, wgt_g, wut_g, wd_g, TOKEN_TILE_BWD_MLP)
    tk = TOKEN_TILE_WGRAD
    x1 = dict(b_affine=(ln1_g, ln1_b))
    g_wgt = _wgrad("wgrad_gate", dgate, n1, N_DEV, tk, **x1)
    g_wut = _wgrad("wgrad_up", dup, n1, N_DEV, tk, **x1)
    g_wd = _wgrad("wgrad_down", hh, dr2, N_DEV, tk)
    g_wout = _wgrad("wgrad_out", ycat, dr1, N_DEV, tk)
    (gx, dproj, gws, gbs, gcw, vmix), (r_wgt, r_wut, r_wd, r_wout) = _bwd_mix(
        dr1, proj, phi, y_conv, win_g, wout_g, sgu_ln_g, sgu_ln_b, wcat, wcatt, bs_full, cw, conv_ln_g, conv_ln_b,
        TOKEN_TILE_BWD_MIX, _Exchange([g_wgt, g_wut, g_wd, g_wout], []))
    half = D_MODEL // 2
    g_win_a, (gws8, gbs8, gcw8, vmix8, vmlp8, vout8) = _wgrad(
        "wgrad_in_a", dproj, xs, N_DEV, tk, _Exchange([], [gws, gbs, gcw, vmix, vmlp, vout]), b_cols=(0, half))
    g_win_b, (r_win_a,) = _wgrad("wgrad_in_b", dproj, xs, N_DEV, tk, _Exchange([g_win_a], []), b_cols=(1, half))
    (r_win_b,) = _exchange("exchange_grad_in", [g_win_b], [])
    (u_gate, u_up, u_down), _ = _staged_call(
        "adamw_mlp",
        [_adamw_shard_group([r_wgt], w_gate[0].T, m_w_gate[0].T, v_w_gate[0].T, False),
         _adamw_shard_group([r_wut], w_up[0].T, m_w_up[0].T, v_w_up[0].T, False),
         _adamw_shard_group([r_wd], w_down[0], m_w_down[0], v_w_down[0], False)], 56)
    (u_in, u_out), _ = _staged_call(
        "adamw_mix",
        [_adamw_shard_group([r_win_a, r_win_b], w_in[0], m_w_in[0], v_w_in[0], True),
         _adamw_shard_group([r_wout], w_out[0], m_w_out[0], v_w_out[0], False)], 32)
    big = {"w_in": u_in, "w_out": u_out, "w_gate": u_gate, "w_up": u_up, "w_down": u_down}
    small_in = {
        "sgu_ln_g": (sgu_ln_g, m_sgu_ln_g, v_sgu_ln_g), "sgu_ln_b": (sgu_ln_b, m_sgu_ln_b, v_sgu_ln_b),
        "w_s": tuple(a.reshape(N_HEADS * CHUNK, CHUNK) for a in (w_s, m_w_s, v_w_s)),
        "b_s": (b_s[0], m_b_s[0], v_b_s[0]),
        "conv_b": (conv_b, m_conv_b, v_conv_b), "conv_ln_g": (conv_ln_g, m_conv_ln_g, v_conv_ln_g),
        "conv_ln_b": (conv_ln_b, m_conv_ln_b, v_conv_ln_b),
        "ln1_g": (ln1_g, m_ln1_g, v_ln1_g), "ln1_b": (ln1_b, m_ln1_b, v_ln1_b),
        "ln2_g": (ln2_g, m_ln2_g, v_ln2_g), "ln2_b": (ln2_b, m_ln2_b, v_ln2_b),
    }
    loss11, gcw_full, small = _finish_small(gws8, gbs8, gcw8, vmix8, vmlp8, vout8, small_in)

    me = 4 * lax.axis_index("x") + 2 * lax.axis_index("y") + lax.axis_index("c")
    g_cw = lax.dynamic_slice(gcw_full, (0, me * (D_CONV // N_DEV)), (CONV_WIDTH, D_CONV // N_DEV))
    d_cw, m_cw, v_cw = _adamw_plain("adamw_conv_w", g_cw, conv_w[0], m_conv_w[0], v_conv_w[0])

    shapes = {"w_s": w_s.shape, "b_s": b_s.shape}
    out = {}
    for n, r in big.items():
        out[n] = tuple((a.T if n in ("w_gate", "w_up") else a)[None] for a in r)
    for n, r in small.items():
        out[n] = tuple(a.reshape(shapes[n]) for a in r) if n in shapes else tuple(r)
    out["conv_w"] = tuple(a[None] for a in (g_cw, d_cw, m_cw, v_cw))

    order = ["w_in", "sgu_ln_g", "sgu_ln_b", "w_s", "b_s", "conv_w", "conv_b", "conv_ln_g", "conv_ln_b", "w_out",
             "ln1_g", "ln1_b", "w_gate", "w_up", "w_down", "ln2_g", "ln2_b"]
    loss = loss11[0, 0]
    return (loss, gx[None], *[out[n][0] for n in order], *[out[n][1] for n in order],
            *[out[n][2] for n in order], *[out[n][3] for n in order])
```

```python
import jax
import jax.numpy as jnp
from jax import lax
from jax.experimental import pallas as pl
from jax.experimental.pallas import tpu as pltpu

F32 = jnp.float32
BF16 = jnp.bfloat16

D_MODEL = 1024
D_SGU = 512
D_CONV = 512
N_HEADS = 8
CHUNK = 128
CONV_WIDTH = 31
CONV_ROWS = 32
HALO = 32
D_FF = 2816
N_DEV = 8
FF_SHARD = D_FF // N_DEV
ALPHA = (2.0 * 1) ** 0.25
LN_EPS = 1e-5
INV_SQRT2 = 0.7071067811865476
INV_SQRT_2PI = 0.3989422804014327

ADAM_LR = 0.001
ADAM_B1 = 0.9
ADAM_B2 = 0.999
ADAM_EPS = 1e-08
ADAM_WD = 0.01
ADAM_STEP = 10

MXU_COLS = 256
SUBLANES = 8
CONV_ROW_BLOCK = 32
WGRAD_ROW_BLOCK = 32
SHIFT_ROWS = HALO - SUBLANES
MIB = 1024 * 1024

HBM = pl.BlockSpec(memory_space=pltpu.HBM)
ANY = pl.BlockSpec(memory_space=pl.ANY)
MESH = pl.DeviceIdType.MESH


def _params(vmem_mib, grid_dims=0):
    kw = dict(vmem_limit_bytes=vmem_mib * MIB)
    if grid_dims:
        kw["dimension_semantics"] = ("arbitrary",) * grid_dims
    return pltpu.CompilerParams(**kw)


def _full(shape):
    return pl.BlockSpec(shape, lambda i: (0,) * len(shape))


def _dot(a, b):
    return jnp.dot(a, b, preferred_element_type=F32)


def _dot_nt(a, b):
    return lax.dot_general(a, b, (((1,), (1,)), ((), ())), preferred_element_type=F32)


def _dot_tn(a, b):
    return lax.dot_general(a, b, (((0,), (0,)), ((), ())), preferred_element_type=F32)


def _normal_cdf(x):
    return 0.5 * (1.0 + lax.erf(x * INV_SQRT2))


def _gelu_grad(x, cdf):
    return cdf + x * jnp.exp(-0.5 * x * x) * INV_SQRT_2PI


def _ln_fwd(v):
    mu = jnp.mean(v, axis=-1, keepdims=True)
    d = v - mu
    var = jnp.mean(d * d, axis=-1, keepdims=True)
    rstd = lax.rsqrt(var + LN_EPS)
    return d * rstd, rstd


def _ln_bwd(dyhat, yhat, rstd):
    m1 = jnp.mean(dyhat, axis=-1, keepdims=True)
    m2 = jnp.mean(dyhat * yhat, axis=-1, keepdims=True)
    return rstd * (dyhat - m1 - yhat * m2)


def _colsum(v):
    return jnp.sum(v, axis=0, keepdims=True)


def _head_pair_stack(v, lo):
    return jnp.concatenate([jnp.where(lo, v, 0.0), jnp.where(lo, 0.0, v)], axis=0).astype(BF16)


def _lo_mask():
    return lax.broadcasted_iota(jnp.int32, (CHUNK, CHUNK), 1) < (CHUNK // 2)


def _head_selector():
    head = lax.broadcasted_iota(jnp.int32, (N_HEADS, D_SGU), 0)
    lane = lax.broadcasted_iota(jnp.int32, (N_HEADS, D_SGU), 1)
    width = D_SGU // N_HEADS
    return ((lane >= head * width) & (lane < (head + 1) * width)).astype(F32)


def _shifted_copies(pad_ref, sh_ref, rows):
    for r in range(1, SUBLANES):
        sh_ref[r - 1, 0:rows, :] = pad_ref[pl.ds(r, rows), :]


def _tap_groups(offset_of_tap):
    groups = {}
    for k in range(CONV_WIDTH):
        o = offset_of_tap(k)
        groups.setdefault(o % SUBLANES, []).append((k, o // SUBLANES))
    return groups


def _tap_window(pad_ref, sh_ref, r, taps, row0, rows):
    q0 = min(q for _, q in taps)
    q1 = max(q for _, q in taps)
    src = pad_ref if r == 0 else sh_ref.at[r - 1]
    win = src[pl.ds(row0 + SUBLANES * q0, SUBLANES * (q1 - q0) + rows), :]
    return win, [(k, SUBLANES * (q - q0)) for k, q in taps]


def _causal_conv(pad_ref, sh_ref, w_ref, out_ref, rows, offset_of_tap, bias=None):
    groups = _tap_groups(offset_of_tap)

    def block(b, carry):
        row0 = pl.multiple_of(b * CONV_ROW_BLOCK, CONV_ROW_BLOCK)
        if bias is None:
            acc = jnp.zeros((CONV_ROW_BLOCK, D_CONV), F32)
        else:
            acc = jnp.broadcast_to(bias, (CONV_ROW_BLOCK, D_CONV))
        for r, taps in groups.items():
            win, starts = _tap_window(pad_ref, sh_ref, r, taps, row0, CONV_ROW_BLOCK)
            for k, s in starts:
                acc = acc + w_ref[k:k + 1, :] * win[s:s + CONV_ROW_BLOCK, :]
        out_ref[pl.ds(row0, CONV_ROW_BLOCK), :] = acc
        return carry

    lax.fori_loop(0, rows // CONV_ROW_BLOCK, block, 0)


def _conv_weight_grad(dy_ref, pad_ref, sh_ref, acc_ref, rows, offset_of_tap):
    groups = _tap_groups(offset_of_tap)
    for r, taps in groups.items():

        def block(b, parts, r=r, taps=taps):
            row0 = pl.multiple_of(b * WGRAD_ROW_BLOCK, WGRAD_ROW_BLOCK)
            dyb = dy_ref[pl.ds(row0, WGRAD_ROW_BLOCK), :]
            win, starts = _tap_window(pad_ref, sh_ref, r, taps, row0, WGRAD_ROW_BLOCK)
            out = []
            for part, (_, s) in zip(parts, starts):
                pr = dyb * win[s:s + WGRAD_ROW_BLOCK, :]
                out.append(part + pr.reshape(WGRAD_ROW_BLOCK // SUBLANES, SUBLANES, D_CONV).sum(axis=0))
            return tuple(out)

        zeros = tuple(jnp.zeros((SUBLANES, D_CONV), F32) for _ in taps)
        parts = lax.fori_loop(0, rows // WGRAD_ROW_BLOCK, block, zeros)
        for part, (k, _) in zip(parts, taps):
            acc_ref[k] += part


def _prep_weights(w_in, w_out, w_gate_t, w_up_t, w_down, conv_w, w_s, b_s):
    def compute(ins, outs):
        win_ref, wout_ref, wgt_ref, wut_ref, wd_ref, cw_ref, ws_ref, bs_ref = ins
        win_o, wout_o, wgt_o, wut_o, wd_o, cw_o, wcat_o, wcatt_o, bsf_o = outs
        win_o[...] = win_ref[...].T.astype(BF16)
        wout_o[...] = wout_ref[...].astype(BF16)
        wgt_o[...] = wgt_ref[...].astype(BF16)
        wut_o[...] = wut_ref[...].astype(BF16)
        wd_o[...] = wd_ref[...].astype(BF16)
        cw_o[...] = jnp.zeros(cw_o.shape, F32)
        cw_o[0:CONV_WIDTH, 0:D_CONV // N_DEV] = cw_ref[...]
        row = lax.broadcasted_iota(jnp.int32, (CHUNK, CHUNK), 0)
        col = lax.broadcasted_iota(jnp.int32, (CHUNK, CHUNK), 1)
        causal = row >= col
        for h in range(N_HEADS):
            w = jnp.where(causal, ws_ref[h], 0.0)
            p, half = h // 2, (h % 2) * CHUNK
            wcat_o[p, :, half:half + CHUNK] = w.astype(BF16)
            wcatt_o[p, :, half:half + CHUNK] = w.T.astype(BF16)
        bsf_o[...] = lax.dot_general(bs_ref[...], _head_selector(), (((0,), (0,)), ((), ())),
                                     preferred_element_type=F32, precision=lax.Precision.HIGHEST)

    S = jax.ShapeDtypeStruct
    out_shapes = [S((256, D_MODEL), BF16), S((128, D_MODEL), BF16), S((FF_SHARD, D_MODEL), BF16),
                  S((FF_SHARD, D_MODEL), BF16), S((FF_SHARD, D_MODEL), BF16), S((CONV_ROWS, 128), F32),
                  S((4, CHUNK, 2 * CHUNK), BF16), S((4, CHUNK, 2 * CHUNK), BF16), S((CHUNK, D_SGU), F32)]
    (res,), _ = _staged_call(
        "prep_weights", [([w_in, w_out, w_gate_t, w_up_t, w_down, conv_w, w_s, b_s], out_shapes, compute)], 32)
    return res


def _mesh_position():
    x, y, c = lax.axis_index("x"), lax.axis_index("y"), lax.axis_index("c")
    return x, y, c


def _peers(x, y, c):
    out = []
    for k in range(1, N_DEV):
        px = 1 - x if (k >> 2) & 1 else x
        py = 1 - y if (k >> 1) & 1 else y
        pc = 1 - c if k & 1 else c
        out.append(((px, py, pc), 4 * px + 2 * py + pc))
    return out


class _Exchange:
    def __init__(self, scatter, gather):
        self.arrays = list(scatter) + list(gather)
        self.n_sc = len(scatter)
        self.n = len(self.arrays)
        self.out_shape = [jax.ShapeDtypeStruct(a.shape if k < self.n_sc else (N_DEV,) + a.shape, a.dtype)
                          for k, a in enumerate(self.arrays)]
        n_remote = self.n * (N_DEV - 1)
        self.scratch = [pltpu.SemaphoreType.DMA((n_remote,)), pltpu.SemaphoreType.DMA((n_remote,)),
                        pltpu.SemaphoreType.DMA((self.n,))] if self.n else []

    def _copies(self, src, dst, sems):
        send_sems, recv_sems, local_sems = sems
        x, y, c = _mesh_position()
        me = 4 * x + 2 * y + c
        locals_, first, arrivals, passed, last = [], [], [], [], []

        def remote(a, k, src_ref, slot, to):
            s = a * (N_DEV - 1) + k
            return pltpu.make_async_remote_copy(src_ref=src_ref, dst_ref=dst[a].at[slot], send_sem=send_sems.at[s],
                                                recv_sem=recv_sems.at[s], device_id=to, device_id_type=MESH)

        for a in range(self.n):
            if a < self.n_sc:
                locals_.append(pltpu.make_async_copy(src[a].at[me], dst[a].at[me], local_sems.at[a]))
                for k, (peer, pid) in enumerate(_peers(x, y, c)):
                    first.append(remote(a, k, src[a].at[pid], me, peer))
                    last.append(remote(a, k, src[a].at[pid], pid, peer))
                continue
            locals_.append(pltpu.make_async_copy(src[a], dst[a].at[me], local_sems.at[a]))
            sibling, sib_id = (x, y, 1 - c), 4 * x + 2 * y + (1 - c)
            chips = [(1 - x, y), (x, 1 - y), (1 - x, 1 - y)]
            first.append(remote(a, 0, src[a], me, sibling))
            last.append(remote(a, 0, src[a], sib_id, sibling))
            for j, (px, py) in enumerate(chips):
                same, other = 4 * px + 2 * py + c, 4 * px + 2 * py + (1 - c)
                first.append(remote(a, 1 + j, src[a], me, (px, py, c)))
                arrivals.append(remote(a, 1 + j, src[a], same, (px, py, c)))
                passed.append(remote(a, 4 + j, dst[a].at[same], same, sibling))
                last.append(remote(a, 4 + j, dst[a].at[other], other, sibling))
        return locals_, first, arrivals, passed, last

    def start(self, src, dst, sems):
        if not self.n:
            return
        locals_, first, _, _, _ = self._copies(src, dst, sems)
        for cp in locals_ + first:
            cp.start()

    def forward(self, src, dst, sems):
        if self.n == self.n_sc:
            return
        _, _, arrivals, passed, _ = self._copies(src, dst, sems)
        for arrived, cp in zip(arrivals, passed):
            arrived.wait_recv()
            cp.start()

    def wait(self, src, dst, sems):
        if not self.n:
            return
        locals_, first, _, passed, last = self._copies(src, dst, sems)
        for cp in last:
            cp.wait_recv()
        for cp in first + passed:
            cp.wait_send()
        for cp in locals_:
            cp.wait()


def _exchange(name, scatter, gather):
    ex = _Exchange(scatter, gather)
    n = ex.n

    def body(*refs):
        src, dst, sems = refs[:n], refs[n:2 * n], refs[2 * n:]
        ex.start(src, dst, sems)
        ex.forward(src, dst, sems)
        ex.wait(src, dst, sems)

    return pl.pallas_call(
        body, name=name, out_shape=tuple(ex.out_shape), in_specs=[ANY] * n, out_specs=(ANY,) * n,
        scratch_shapes=ex.scratch,
    )(*ex.arrays)


def _forward_step(n_steps):
    return (11 * n_steps) // 16


def _hosted(ex, refs, n_in, n_out):
    ins, ex_src = refs[:n_in], refs[n_in:n_in + ex.n]
    rest = refs[n_in + ex.n:]
    outs, ex_dst = rest[:n_out], rest[n_out:n_out + ex.n]
    rest = rest[n_out + ex.n:]
    n_own = len(rest) - len(ex.scratch)
    return ins, outs, rest[:n_own], (ex_src, ex_dst, rest[n_own:])


def _fwd_mix(x, win_g, wout_g, sgu_g, sgu_b, wcat, bs_full, cw, cb, cg, cbeta, tm, ex):
    T = x.shape[0]
    nt = T // tm

    def body(*refs):
        ins, outs, scratch, ex_refs = _hosted(ex, refs, 11, 6)
        x_ref, win_ref, wout_ref, sg_ref, sb_ref, wcat_ref, bs_ref, cw_ref, cb_ref, cg_ref, cbeta_ref = ins
        proj_ref, ycat_ref, n1_ref, rstd1_ref, phi_ref, y_ref = outs
        hpad, hshift = scratch
        i = pl.program_id(0)

        @pl.when(i == 0)
        def _():
            ex.start(*ex_refs)

        xf = x_ref[...]
        xb = xf.astype(BF16)
        proj_ref[...] = _dot_nt(xb, win_ref[...])
        cdf = _normal_cdf(proj_ref[:, 0:2 * D_SGU])
        phi_ref[...] = cdf.astype(BF16)
        u = proj_ref[:, 0:D_SGU] * cdf[:, 0:D_SGU]
        v = proj_ref[:, D_SGU:2 * D_SGU] * cdf[:, D_SGU:2 * D_SGU]
        vhat, _ = _ln_fwd(v)
        vn = vhat * sg_ref[...] + sb_ref[...]
        lo = _lo_mask()
        for c in range(tm // CHUNK):
            rows = slice(CHUNK * c, CHUNK * (c + 1))
            for p in range(4):
                lanes = slice(CHUNK * p, CHUNK * (p + 1))
                mixed = _dot(wcat_ref[p], _head_pair_stack(vn[rows, lanes], lo)) + bs_ref[:, lanes]
                ycat_ref[rows, lanes] = (u[rows, lanes] * mixed).astype(BF16)
        base = 2 * D_SGU
        a = proj_ref[:, base:base + D_CONV]
        g = proj_ref[:, base + D_CONV:base + 2 * D_CONV]

        @pl.when(i == 0)
        def _():
            hpad[0:HALO, :] = jnp.zeros((HALO, D_CONV), F32)

        hpad[HALO:HALO + tm, :] = a * jax.nn.sigmoid(g)
        _shifted_copies(hpad, hshift, tm + SHIFT_ROWS)
        _causal_conv(hpad, hshift, cw_ref, y_ref, tm, lambda k: HALO - (CONV_WIDTH - 1) + k, bias=cb_ref[...])
        hpad[0:HALO, :] = hpad[tm:tm + HALO, :]
        yhat, _ = _ln_fwd(y_ref[...])
        yn = yhat * cg_ref[...] + cbeta_ref[...]
        ycat_ref[:, D_SGU:D_SGU + D_CONV] = (yn * jax.nn.sigmoid(yn)).astype(BF16)
        r1 = ALPHA * xf + _dot(ycat_ref[...], wout_ref[...])
        n1, rstd1 = _ln_fwd(r1)
        n1_ref[...] = n1
        rstd1_ref[...] = rstd1

        @pl.when(i == _forward_step(nt))
        def _():
            ex.forward(*ex_refs)

        @pl.when(i == nt - 1)
        def _():
            ex.wait(*ex_refs)

    S = jax.ShapeDtypeStruct
    row = lambda w: pl.BlockSpec((tm, w), lambda i: (i, 0))
    res = pl.pallas_call(
        body, name="fwd_mix", grid=(nt,),
        in_specs=[row(D_MODEL), _full(win_g.shape), _full(wout_g.shape), _full(sgu_g.shape), _full(sgu_b.shape),
                  _full(wcat.shape), _full(bs_full.shape), _full(cw.shape), _full(cb.shape), _full(cg.shape),
                  _full(cbeta.shape)] + [ANY] * ex.n,
        out_specs=(row(2 * D_MODEL), row(D_MODEL), row(D_MODEL), row(1), row(2 * D_SGU), row(D_CONV)) + (ANY,) * ex.n,
        out_shape=(S((T, 2 * D_MODEL), F32), S((T, D_MODEL), BF16), S((T, D_MODEL), F32), S((T, 1), F32),
                   S((T, 2 * D_SGU), BF16), S((T, D_CONV), F32), *ex.out_shape),
        scratch_shapes=[pltpu.VMEM((tm + HALO, D_CONV), F32),
                        pltpu.VMEM((SUBLANES - 1, tm + SHIFT_ROWS, D_CONV), F32)] + ex.scratch,
        compiler_params=_params(56, 1),
    )(x, win_g, wout_g, sgu_g, sgu_b, wcat, bs_full, cw, cb, cg, cbeta, *ex.arrays)
    return res[:6], res[6:]


def _load_resident(pairs, sems):
    cps = [pltpu.make_async_copy(s, d, sems.at[k]) for k, (s, d) in enumerate(pairs)]
    for cp in cps:
        cp.start()
    for cp in cps:
        cp.wait()


def _fwd_mlp(n1, tgt, l1g, l1b, l2g, l2b, wgt, wut, wd, tm):
    T = n1.shape[0]
    nt = T // tm
    nf = D_FF // MXU_COLS

    def body(n1_ref, tgt_ref, l1g_ref, l1b_ref, l2g_ref, l2b_ref, wg_hbm, wu_hbm, wd_hbm,
             gate_ref, up_ref, hh_ref, dr2_ref, stat_ref, wg_s, wu_s, wd_s, sems):
        i = pl.program_id(0)

        @pl.when(i == 0)
        def _():
            _load_resident([(wg_hbm, wg_s), (wu_hbm, wu_s), (wd_hbm, wd_s)], sems)
            stat_ref[...] = jnp.zeros(stat_ref.shape, F32)

        x1 = n1_ref[...] * l1g_ref[...] + l1b_ref[...]
        x1b = x1.astype(BF16)
        for f in range(nf):
            cols = slice(MXU_COLS * f, MXU_COLS * (f + 1))
            gt = _dot_nt(x1b, wg_s[cols, :])
            ut = _dot_nt(x1b, wu_s[cols, :])
            gate_ref[:, cols] = gt.astype(BF16)
            up_ref[:, cols] = ut.astype(BF16)
            hh_ref[:, cols] = (gt * jax.nn.sigmoid(gt) * ut).astype(BF16)
        r2 = ALPHA * x1 + _dot(hh_ref[...], wd_s[...])
        n2, rstd2 = _ln_fwd(r2)
        x2 = n2 * l2g_ref[...] + l2b_ref[...]
        diff = x2 - tgt_ref[...]
        dx2 = diff * (1.0 / D_MODEL)
        stat_ref[0:1, :] += _colsum(diff * diff)
        stat_ref[1:2, :] += _colsum(dx2 * n2)
        stat_ref[2:3, :] += _colsum(dx2)
        dr2_ref[...] = _ln_bwd(dx2 * l2g_ref[...], n2, rstd2)

    S = jax.ShapeDtypeStruct
    row = lambda w: pl.BlockSpec((tm, w), lambda i: (i, 0))
    vec = _full((1, D_MODEL))
    return pl.pallas_call(
        body, name="fwd_mlp", grid=(nt,),
        in_specs=[row(D_MODEL), row(D_MODEL), vec, vec, vec, vec, ANY, ANY, ANY],
        out_specs=(row(D_FF), row(D_FF), row(D_FF), row(D_MODEL), _full((8, D_MODEL))),
        out_shape=(S((T, D_FF), BF16), S((T, D_FF), BF16), S((T, D_FF), BF16), S((T, D_MODEL), F32),
                   S((8, D_MODEL), F32)),
        scratch_shapes=[pltpu.VMEM((D_FF, D_MODEL), BF16)] * 3 + [pltpu.SemaphoreType.DMA((3,))],
        compiler_params=_params(56, 1),
    )(n1, tgt, l1g, l1b, l2g, l2b, wgt, wut, wd)


def _bwd_mlp(dr2, gate, up, n1, rstd1, l1g, wgt, wut, wd, tm):
    T = n1.shape[0]
    nt = T // tm
    nf = D_FF // MXU_COLS

    def body(dr2_ref, gate_ref, up_ref, n1_ref, rstd1_ref, l1g_ref, wg_hbm, wu_hbm, wd_hbm,
             dgate_ref, dup_ref, dr1_ref, stat_ref, wg_s, wu_s, wd_s, sems):
        i = pl.program_id(0)

        @pl.when(i == 0)
        def _():
            _load_resident([(wg_hbm, wg_s), (wu_hbm, wu_s), (wd_hbm, wd_s)], sems)
            stat_ref[...] = jnp.zeros(stat_ref.shape, F32)

        dr2 = dr2_ref[...]
        dr2b = dr2.astype(BF16)
        for f in range(nf):
            cols = slice(MXU_COLS * f, MXU_COLS * (f + 1))
            dhh = _dot_nt(dr2b, wd_s[cols, :])
            gt = gate_ref[:, cols].astype(F32)
            ut = up_ref[:, cols].astype(F32)
            sg = jax.nn.sigmoid(gt)
            dgate_ref[:, cols] = (dhh * ut * (sg * (1.0 + gt * (1.0 - sg)))).astype(BF16)
            dup_ref[:, cols] = (dhh * (gt * sg)).astype(BF16)
        dx1 = ALPHA * dr2 + _dot(dgate_ref[...], wg_s[...]) + _dot(dup_ref[...], wu_s[...])
        n1 = n1_ref[...]
        stat_ref[0:1, :] += _colsum(dx1 * n1)
        stat_ref[1:2, :] += _colsum(dx1)
        dr1_ref[...] = _ln_bwd(dx1 * l1g_ref[...], n1, rstd1_ref[...])

    S = jax.ShapeDtypeStruct
    row = lambda w: pl.BlockSpec((tm, w), lambda i: (i, 0))
    return pl.pallas_call(
        body, name="bwd_mlp", grid=(nt,),
        in_specs=[row(D_MODEL), row(D_FF), row(D_FF), row(D_MODEL), row(1), _full((1, D_MODEL)), ANY, ANY, ANY],
        out_specs=(row(D_FF), row(D_FF), row(D_MODEL), _full((8, D_MODEL))),
        out_shape=(S((T, D_FF), BF16), S((T, D_FF), BF16), S((T, D_MODEL), F32), S((8, D_MODEL), F32)),
        scratch_shapes=[pltpu.VMEM((D_FF, D_MODEL), BF16)] * 3 + [pltpu.SemaphoreType.DMA((3,))],
        compiler_params=_params(56, 1),
    )(dr2, gate, up, n1, rstd1, l1g, wgt, wut, wd)


def _bwd_mix(dr1, proj, phi, y, win_g, wout_g, sgu_g, sgu_b, wcat, wcatt, bs_full, cw, cg, cbeta, tm, ex):
    T = dr1.shape[0]
    nt = T // tm
    halo_blocks = tm // HALO

    def body(*refs):
        ins, outs, scratch, ex_refs = _hosted(ex, refs, 15, 6)
        (dr1_ref, proj_ref, halo_ref, phi_ref, y_ref, win_ref, wout_ref, sg_ref, sb_ref, wcat_ref, wcatt_ref, bs_ref,
         cw_ref, cg_ref, cbeta_ref) = ins
        gx_ref, dproj_ref, gws_out, gbs_out, gcw_ref, vec_ref = outs
        hpad, shift, dypad, dhbuf, dubuf, dvnbuf, gcw_acc, gws_ref, gbs_ref = scratch
        i = pl.program_id(0)
        tile = nt - 1 - i

        @pl.when(i == 0)
        def _():
            ex.start(*ex_refs)
            gws_ref[...] = jnp.zeros(gws_ref.shape, F32)
            gbs_ref[...] = jnp.zeros(gbs_ref.shape, F32)
            gcw_ref[...] = jnp.zeros(gcw_ref.shape, F32)
            vec_ref[...] = jnp.zeros(vec_ref.shape, F32)
            gcw_acc[...] = jnp.zeros(gcw_acc.shape, F32)
            dypad[tm:tm + HALO, :] = jnp.zeros((HALO, D_CONV), F32)

        dr1 = dr1_ref[...]
        dycat = _dot_nt(dr1.astype(BF16), wout_ref[...])
        pu = proj_ref[:, 0:D_SGU]
        pv = proj_ref[:, D_SGU:2 * D_SGU]
        cdf_u = phi_ref[:, 0:D_SGU].astype(F32)
        cdf_v = phi_ref[:, D_SGU:2 * D_SGU].astype(F32)
        u = pu * cdf_u
        vhat, rstd_v = _ln_fwd(pv * cdf_v)
        vn = vhat * sg_ref[...] + sb_ref[...]
        lo = _lo_mask()
        for c in range(tm // CHUNK):
            rows = slice(CHUNK * c, CHUNK * (c + 1))
            for p in range(4):
                lanes = slice(CHUNK * p, CHUNK * (p + 1))
                vstack = _head_pair_stack(vn[rows, lanes], lo)
                mixed = _dot(wcat_ref[p], vstack) + bs_ref[:, lanes]
                d_a = dycat[rows, lanes]
                dubuf[rows, lanes] = d_a * mixed
                dm = d_a * u[rows, lanes]
                gbs_ref[:, lanes] += dm
                dstack = _head_pair_stack(dm, lo)
                gws_ref[2 * CHUNK * p:2 * CHUNK * (p + 1), :] += _dot_nt(dstack, vn[rows, lanes].astype(BF16))
                dvnbuf[rows, lanes] = _dot(wcatt_ref[p], dstack)
        dvn = dvnbuf[...]
        vec_ref[0:1, :] += _colsum(dvn * vhat)
        vec_ref[1:2, :] += _colsum(dvn)
        dv = _ln_bwd(dvn * sg_ref[...], vhat, rstd_v)
        dproj_ref[:, 0:D_SGU] = (dubuf[...] * _gelu_grad(pu, cdf_u)).astype(BF16)
        dproj_ref[:, D_SGU:2 * D_SGU] = (dv * _gelu_grad(pv, cdf_v)).astype(BF16)
        base = 2 * D_SGU
        a = proj_ref[:, base:base + D_CONV]
        sgm = jax.nn.sigmoid(proj_ref[:, base + D_CONV:base + 2 * D_CONV])
        h_before = halo_ref[:, 0:D_CONV] * jax.nn.sigmoid(halo_ref[:, D_CONV:2 * D_CONV])
        hpad[0:HALO, :] = jnp.where(tile > 0, h_before, 0.0)
        hpad[HALO:HALO + tm, :] = a * sgm
        _shifted_copies(hpad, shift, tm + SHIFT_ROWS)
        h_offset = lambda k: HALO - (CONV_WIDTH - 1) + k
        yhat, rstd_y = _ln_fwd(y_ref[...])
        yn = yhat * cg_ref[...] + cbeta_ref[...]
        s = jax.nn.sigmoid(yn)
        dyn = dycat[:, D_SGU:D_SGU + D_CONV] * (s * (1.0 + yn * (1.0 - s)))
        vec_ref[3:4, :] += _colsum(dyn * yhat)
        vec_ref[4:5, :] += _colsum(dyn)
        dy = _ln_bwd(dyn * cg_ref[...], yhat, rstd_y)
        vec_ref[2:3, :] += _colsum(dy)
        dypad[0:tm, :] = dy
        _conv_weight_grad(dypad, hpad, shift, gcw_acc, tm, h_offset)
        _shifted_copies(dypad, shift, tm + SHIFT_ROWS)
        _causal_conv(dypad, shift, cw_ref, dhbuf, tm, lambda k: (CONV_WIDTH - 1) - k)
        dypad[tm:tm + HALO, :] = dypad[0:HALO, :]
        dh = dhbuf[...]
        dproj_ref[:, base:base + D_CONV] = (dh * sgm).astype(BF16)
        dproj_ref[:, base + D_CONV:base + 2 * D_CONV] = (dh * a * sgm * (1.0 - sgm)).astype(BF16)
        gx_ref[...] = ALPHA * dr1 + _dot(dproj_ref[...], win_ref[...])

        @pl.when(i == _forward_step(nt))
        def _():
            ex.forward(*ex_refs)

        @pl.when(i == nt - 1)
        def _():
            gcw_ref[...] = gcw_acc[...].sum(axis=1)
            gws_out[...] = gws_ref[...].astype(BF16)
            gbs_out[...] = lax.dot_general(_head_selector(), gbs_ref[...], (((1,), (1,)), ((), ())),
                                           preferred_element_type=F32, precision=lax.Precision.HIGHEST)
            ex.wait(*ex_refs)

    S = jax.ShapeDtypeStruct
    row = lambda w: pl.BlockSpec((tm, w), lambda i: (nt - 1 - i, 0))
    halo = pl.BlockSpec((HALO, D_MODEL), lambda i: (jnp.maximum((nt - 1 - i) * halo_blocks - 1, 0), 1))
    res = pl.pallas_call(
        body, name="bwd_mix", grid=(nt,),
        in_specs=[row(D_MODEL), row(2 * D_MODEL), halo, row(2 * D_SGU), row(D_CONV), _full(win_g.shape),
                  _full(wout_g.shape), _full(sgu_g.shape), _full(sgu_b.shape), _full(wcat.shape), _full(wcatt.shape),
                  _full(bs_full.shape), _full(cw.shape), _full(cg.shape), _full(cbeta.shape)]
        + [ANY] * ex.n,
        out_specs=(row(D_MODEL), row(2 * D_MODEL), _full((N_HEADS * CHUNK, CHUNK)), _full((N_HEADS, CHUNK)),
                   _full((CONV_ROWS, D_CONV)), _full((8, D_CONV))) + (ANY,) * ex.n,
        out_shape=(S((T, D_MODEL), F32), S((T, 2 * D_MODEL), BF16), S((N_HEADS * CHUNK, CHUNK), BF16),
                   S((N_HEADS, CHUNK), F32), S((CONV_ROWS, D_CONV), F32), S((8, D_CONV), F32), *ex.out_shape),
        scratch_shapes=[pltpu.VMEM((tm + HALO, D_CONV), F32), pltpu.VMEM((SUBLANES - 1, tm + SHIFT_ROWS, D_CONV), F32),
                        pltpu.VMEM((tm + HALO, D_CONV), F32),
                        pltpu.VMEM((tm, D_CONV), F32), pltpu.VMEM((tm, D_SGU), F32),
                        pltpu.VMEM((tm, D_SGU), F32), pltpu.VMEM((CONV_ROWS, 8, D_CONV), F32),
                        pltpu.VMEM((N_HEADS * CHUNK, CHUNK), F32), pltpu.VMEM((CHUNK, D_SGU), F32)] + ex.scratch,
        compiler_params=_params(56, 1),
    )(dr1, proj, proj, phi, y, win_g, wout_g, sgu_g, sgu_b, wcat, wcatt, bs_full, cw, cg, cbeta, *ex.arrays)
    return res[:6], res[6:]


def _wgrad(name, a, b, blocks, tk, ex=None, b_cols=None, b_affine=None):
    T, M = a.shape
    col, N = b_cols or (0, b.shape[1])
    nk = T // tk
    out_shape = (blocks, M // blocks, N)
    ex = ex or _Exchange([], [])
    affine = list(b_affine or [])

    def body(*refs):
        ins, (o_ref,), (acc,), ex_refs = _hosted(ex, refs, 2 + len(affine), 1)
        a_ref, b_ref = ins[:2]
        i = pl.program_id(0)

        @pl.when(i == 0)
        def _():
            ex.start(*ex_refs)
            acc[...] = jnp.zeros(acc.shape, F32)

        right = b_ref[...]
        if affine:
            right = right * ins[2][...] + ins[3][...]
        acc[...] += _dot_tn(a_ref[...].astype(BF16), right.astype(BF16))

        @pl.when(i == _forward_step(nk))
        def _():
            ex.forward(*ex_refs)

        @pl.when(i == nk - 1)
        def _():
            o_ref[...] = acc[...].astype(BF16)
            ex.wait(*ex_refs)

    res = pl.pallas_call(
        body, name=name, grid=(nk,),
        in_specs=[pl.BlockSpec((tk, M), lambda i: (i, 0)), pl.BlockSpec((tk, N), lambda i: (i, col))]
        + [_full((1, N))] * len(affine) + [ANY] * ex.n,
        out_specs=(_full((M, N)),) + (ANY,) * ex.n,
        out_shape=(jax.ShapeDtypeStruct((M, N), BF16), *ex.out_shape),
        scratch_shapes=[pltpu.VMEM((M, N), F32)] + ex.scratch,
        compiler_params=_params(56, 1),
    )(a, b, *affine, *ex.arrays)
    g = res[0].reshape(out_shape)
    return (g, res[1:]) if ex.n else g


def _adamw(w, g, m, v):
    m2 = ADAM_B1 * m + (1.0 - ADAM_B1) * g
    v2 = ADAM_B2 * v + (1.0 - ADAM_B2) * (g * g)
    m_hat = m2 / (1.0 - ADAM_B1 ** ADAM_STEP)
    v_hat = v2 / (1.0 - ADAM_B2 ** ADAM_STEP)
    delta = -ADAM_LR * (m_hat / (jnp.sqrt(v_hat) + ADAM_EPS) + ADAM_WD * w)
    return delta, m2, v2


def _sum_partials(r_ref):
    g = r_ref[0].astype(F32)
    for s in range(1, N_DEV):
        g = g + r_ref[s].astype(F32)
    return g


def _staged_call(name, groups, vmem_mib, ex=None):
    ex = ex or _Exchange([], [])
    inputs = [a for ins, _, _ in groups for a in ins]
    out_shapes = [s for _, outs, _ in groups for s in outs]
    n_in, n_out = len(inputs), len(out_shapes)

    def body(*refs):
        ins, outs, scratch, ex_refs = _hosted(ex, refs, n_in, n_out)
        in_bufs, out_bufs, sems = scratch[:n_in], scratch[n_in:n_in + n_out], scratch[n_in + n_out]
        ex.start(*ex_refs)
        loads = [pltpu.make_async_copy(ins[k], in_bufs[k], sems.at[k]) for k in range(n_in)]
        stores = [pltpu.make_async_copy(out_bufs[k], outs[k], sems.at[n_in + k]) for k in range(n_out)]
        for cp in loads:
            cp.start()
        i0 = o0 = 0
        for g_ins, g_outs, compute in groups:
            i1, o1 = i0 + len(g_ins), o0 + len(g_outs)
            for cp in loads[i0:i1]:
                cp.wait()
            compute(in_bufs[i0:i1], out_bufs[o0:o1])
            for cp in stores[o0:o1]:
                cp.start()
            i0, o0 = i1, o1
        for cp in stores:
            cp.wait()
        ex.forward(*ex_refs)
        ex.wait(*ex_refs)

    scratch = ([pltpu.VMEM(a.shape, a.dtype) for a in inputs] + [pltpu.VMEM(s.shape, s.dtype) for s in out_shapes]
               + [pltpu.SemaphoreType.DMA((n_in + n_out,))] + ex.scratch)
    res = pl.pallas_call(
        body, name=name, out_shape=(*[pltpu.HBM(s.shape, s.dtype) for s in out_shapes], *ex.out_shape),
        in_specs=[HBM] * n_in + [ANY] * ex.n, out_specs=(HBM,) * n_out + (ANY,) * ex.n,
        scratch_shapes=scratch, compiler_params=_params(vmem_mib),
    )(*[pltpu.with_memory_space_constraint(a, pltpu.HBM) for a in inputs], *ex.arrays)
    per_group, o0 = [], 0
    for _, g_outs, _ in groups:
        per_group.append(list(res[o0:o0 + len(g_outs)]))
        o0 += len(g_outs)
    return per_group, res[n_out:]


def _adamw_shard_group(parts, w, m, v, transposed):
    n = len(parts)

    def compute(ins, outs):
        w_ref, m_ref, v_ref = ins[n:]
        lo = 0
        for r_ref in ins[:n]:
            g = _sum_partials(r_ref)
            cols = g.shape[1]
            if transposed:
                g, at = g.T, (slice(lo, lo + cols), slice(None))
            else:
                at = (slice(None), slice(lo, lo + cols))
            delta, m2, v2 = _adamw(w_ref[at], g, m_ref[at], v_ref[at])
            for o, val in zip(outs, (g, delta, m2, v2)):
                o[at] = val
            lo += cols

    return [*parts, w, m, v], [jax.ShapeDtypeStruct(w.shape, F32)] * 4, compute


def _finish_small(gws8, gbs8, gcw8, vmix8, vmlp8, vout8, small):
    names = ["sgu_ln_g", "sgu_ln_b", "w_s", "b_s", "conv_b", "conv_ln_g", "conv_ln_b", "ln1_g", "ln1_b", "ln2_g", "ln2_b"]
    flat = []
    for n in names:
        flat += list(small[n])

    def compute(ins, outs):
        gws_ref, gbs_ref, gcw_ref, vmix_ref, vmlp_ref, vout_ref = ins[:6]
        wmv = ins[6:]
        loss_o, gcw_o = outs[0], outs[1]
        outs = outs[2:]
        gws = _sum_partials(gws_ref)
        gbs = _sum_partials(gbs_ref)
        vmix = _sum_partials(vmix_ref)
        vmlp = _sum_partials(vmlp_ref)
        vout = _sum_partials(vout_ref)
        gcw_o[...] = _sum_partials(gcw_ref)
        loss = (0.5 / D_MODEL) * jnp.sum(vout[0:1, :], axis=1, keepdims=True)
        loss_o[...] = jnp.broadcast_to(loss, loss_o.shape)
        rows = lax.broadcasted_iota(jnp.int32, (N_HEADS * CHUNK, CHUNK), 0)
        cols = lax.broadcasted_iota(jnp.int32, (N_HEADS * CHUNK, CHUNK), 1)
        gws = jnp.where((rows & (CHUNK - 1)) >= cols, gws, 0.0)
        grads = {
            "sgu_ln_g": vmix[0:1, :], "sgu_ln_b": vmix[1:2, :], "w_s": gws, "b_s": gbs,
            "conv_b": vmix[2:3, :], "conv_ln_g": vmix[3:4, :], "conv_ln_b": vmix[4:5, :],
            "ln1_g": vmlp[0:1, :], "ln1_b": vmlp[1:2, :], "ln2_g": vout[1:2, :], "ln2_b": vout[2:3, :],
        }
        for k, n in enumerate(names):
            w_ref, m_ref, v_ref = wmv[3 * k:3 * k + 3]
            g = grads[n]
            delta, m2, v2 = _adamw(w_ref[...], g, m_ref[...], v_ref[...])
            outs[4 * k][...] = g
            outs[4 * k + 1][...] = delta
            outs[4 * k + 2][...] = m2
            outs[4 * k + 3][...] = v2

    S = jax.ShapeDtypeStruct
    out_shape = [S((SUBLANES, 128), F32), S((CONV_ROWS, D_CONV), F32)]
    for n in names:
        out_shape += [S(small[n][0].shape, F32)] * 4
    (res,), _ = _staged_call(
        "finish_small", [([gws8, gbs8, gcw8, vmix8, vmlp8, vout8, *flat], out_shape, compute)], 40)
    upd = {n: res[2 + 4 * k:6 + 4 * k] for k, n in enumerate(names)}
    return res[0], res[1], upd


def _adamw_plain(name, g, w, m, v):
    def compute(ins, outs):
        g_ref, w_ref, m_ref, v_ref = ins
        for o, val in zip(outs, _adamw(w_ref[...], g_ref[...], m_ref[...], v_ref[...])):
            o[...] = val

    (res,), _ = _staged_call(name, [([g, w, m, v], [jax.ShapeDtypeStruct(w.shape, F32)] * 3, compute)], 16)
    return res


TOKEN_TILE_FWD_MIX = 512
TOKEN_TILE_BWD_MIX = 512
TOKEN_TILE_FWD_MLP = 512
TOKEN_TILE_BWD_MLP = 512
TOKEN_TILE_WGRAD = 1024


def kernel(x, w_in, sgu_ln_g, sgu_ln_b, w_s, b_s, conv_w, conv_b, conv_ln_g, conv_ln_b, w_out, ln1_g, ln1_b, w_gate, w_up, w_down, ln2_g, ln2_b, loss_target, m_w_in, m_sgu_ln_g, m_sgu_ln_b, m_w_s, m_b_s, m_conv_w, m_conv_b, m_conv_ln_g, m_conv_ln_b, m_w_out, m_ln1_g, m_ln1_b, m_w_gate, m_w_up, m_w_down, m_ln2_g, m_ln2_b, v_w_in, v_sgu_ln_g, v_sgu_ln_b, v_w_s, v_b_s, v_conv_w, v_conv_b, v_conv_ln_g, v_conv_ln_b, v_w_out, v_ln1_g, v_ln1_b, v_w_gate, v_w_up, v_w_down, v_ln2_g, v_ln2_b):
    xs = x[0]
    tgt = loss_target[0]

    (win_b, wout_b, wgt_b, wut_b, wd_b, cw_b, wcat, wcatt, bs_full) = _prep_weights(
        w_in[0], w_out[0], w_gate[0].T, w_up[0].T, w_down[0], conv_w[0], w_s[0], b_s[0])
    win_g, wout_g, cw_g = _exchange("gather_mix_weights", [], [win_b, wout_b, cw_b])
    win_g = win_g.reshape(2 * D_MODEL, D_MODEL)
    wout_g = wout_g.reshape(D_MODEL, D_MODEL)
    cw = jnp.transpose(cw_g[:, :, :D_CONV // N_DEV], (1, 0, 2)).reshape(CONV_ROWS, D_CONV)

    (proj, ycat, n1, rstd1, phi, y_conv), (wgt_g, wut_g, wd_g) = _fwd_mix(
        xs, win_g, wout_g, sgu_ln_g, sgu_ln_b, wcat, bs_full, cw, conv_b, conv_ln_g, conv_ln_b, TOKEN_TILE_FWD_MIX,
        _Exchange([], [wgt_b, wut_b, wd_b]))
    wgt_g = wgt_g.reshape(D_FF, D_MODEL)
    wut_g = wut_g.reshape(D_FF, D_MODEL)
    wd_g = wd_g.reshape(D_FF, D_MODEL)
    gate, up, hh, dr2, vout = _fwd_mlp(n1, tgt, ln1_g, ln1_b, ln2_g, ln2_b, wgt_g, wut_g, wd_g, TOKEN_TILE_FWD_MLP)

    dgate, dup, dr1, vmlp = _bwd_mlp(dr2, gate, up, n1, rstd1, ln1_g, wgt_g, wut_g, wd_g, TOKEN_TILE_BWD_MLP)
    tk = TOKEN_TILE_WGRAD
    x1 = dict(b_affine=(ln1_g, ln1_b))
    g_wgt = _wgrad("wgrad_gate", dgate, n1, N_DEV, tk, **x1)
    g_wut = _wgrad("wgrad_up", dup, n1, N_DEV, tk, **x1)
    g_wd = _wgrad("wgrad_down", hh, dr2, N_DEV, tk)
    g_wout = _wgrad("wgrad_out", ycat, dr1, N_DEV, tk)
    (gx, dproj, gws, gbs, gcw, vmix), (r_wgt, r_wut, r_wd, r_wout) = _bwd_mix(
        dr1, proj, phi, y_conv, win_g, wout_g, sgu_ln_g, sgu_ln_b, wcat, wcatt, bs_full, cw, conv_ln_g, conv_ln_b,
        TOKEN_TILE_BWD_MIX, _Exchange([g_wgt, g_wut, g_wd, g_wout], []))
    half = D_MODEL // 2
    g_win_a, (gws8, gbs8, gcw8, vmix8, vmlp8, vout8) = _wgrad(
        "wgrad_in_a", dproj, xs, N_DEV, tk, _Exchange([], [gws, gbs, gcw, vmix, vmlp, vout]), b_cols=(0, half))
    g_win_b, (r_win_a,) = _wgrad("wgrad_in_b", dproj, xs, N_DEV, tk, _Exchange([g_win_a], []), b_cols=(1, half))
    (r_win_b,) = _exchange("exchange_grad_in", [g_win_b], [])
    (u_gate, u_up, u_down), _ = _staged_call(
        "adamw_mlp",
        [_adamw_shard_group([r_wgt], w_gate[0].T, m_w_gate[0].T, v_w_gate[0].T, False),
         _adamw_shard_group([r_wut], w_up[0].T, m_w_up[0].T, v_w_up[0].T, False),
         _adamw_shard_group([r_wd], w_down[0], m_w_down[0], v_w_down[0], False)], 56)
    (u_in, u_out), _ = _staged_call(
        "adamw_mix",
        [_adamw_shard_group([r_win_a, r_win_b], w_in[0], m_w_in[0], v_w_in[0], True),
         _adamw_shard_group([r_wout], w_out[0], m_w_out[0], v_w_out[0], False)], 32)
    big = {"w_in": u_in, "w_out": u_out, "w_gate": u_gate, "w_up": u_up, "w_down": u_down}
    small_in = {
        "sgu_ln_g": (sgu_ln_g, m_sgu_ln_g, v_sgu_ln_g), "sgu_ln_b": (sgu_ln_b, m_sgu_ln_b, v_sgu_ln_b),
        "w_s": tuple(a.reshape(N_HEADS * CHUNK, CHUNK) for a in (w_s, m_w_s, v_w_s)),
        "b_s": (b_s[0], m_b_s[0], v_b_s[0]),
        "conv_b": (conv_b, m_conv_b, v_conv_b), "conv_ln_g": (conv_ln_g, m_conv_ln_g, v_conv_ln_g),
        "conv_ln_b": (conv_ln_b, m_conv_ln_b, v_conv_ln_b),
        "ln1_g": (ln1_g, m_ln1_g, v_ln1_g), "ln1_b": (ln1_b, m_ln1_b, v_ln1_b),
        "ln2_g": (ln2_g, m_ln2_g, v_ln2_g), "ln2_b": (ln2_b, m_ln2_b, v_ln2_b),
    }
    loss11, gcw_full, small = _finish_small(gws8, gbs8, gcw8, vmix8, vmlp8, vout8, small_in)

    me = 4 * lax.axis_index("x") + 2 * lax.axis_index("y") + lax.axis_index("c")
    g_cw = lax.dynamic_slice(gcw_full, (0, me * (D_CONV // N_DEV)), (CONV_WIDTH, D_CONV // N_DEV))
    d_cw, m_cw, v_cw = _adamw_plain("adamw_conv_w", g_cw, conv_w[0], m_conv_w[0], v_conv_w[0])

    shapes = {"w_s": w_s.shape, "b_s": b_s.shape}
    out = {}
    for n, r in big.items():
        out[n] = tuple((a.T if n in ("w_gate", "w_up") else a)[None] for a in r)
    for n, r in small.items():
        out[n] = tuple(a.reshape(shapes[n]) for a in r) if n in shapes else tuple(r)
    out["conv_w"] = tuple(a[None] for a in (g_cw, d_cw, m_cw, v_cw))

    order = ["w_in", "sgu_ln_g", "sgu_ln_b", "w_s", "b_s", "conv_w", "conv_b", "conv_ln_g", "conv_ln_b", "w_out",
             "ln1_g", "ln1_b", "w_gate", "w_up", "w_down", "ln2_g", "ln2_b"]
    loss = loss11[0, 0]
    return (loss, gx[None], *[out[n][0] for n in order], *[out[n][1] for n in order],
            *[out[n][2] for n in order], *[out[n][3] for n in order])
```

```python
import jax
import jax.numpy as jnp
from jax import lax
from jax.experimental import pallas as pl
from jax.experimental.pallas import tpu as pltpu

F32 = jnp.float32
BF16 = jnp.bfloat16

D_MODEL = 1024
D_SGU = 512
D_CONV = 512
N_HEADS = 8
CHUNK = 128
CONV_WIDTH = 31
CONV_ROWS = 32
HALO = 32
D_FF = 2816
N_DEV = 8
FF_SHARD = D_FF // N_DEV
ALPHA = (2.0 * 1) ** 0.25
LN_EPS = 1e-5
INV_SQRT2 = 0.7071067811865476
INV_SQRT_2PI = 0.3989422804014327

ADAM_LR = 0.001
ADAM_B1 = 0.9
ADAM_B2 = 0.999
ADAM_EPS = 1e-08
ADAM_WD = 0.01
ADAM_STEP = 10

MXU_COLS = 256
SUBLANES = 8
CONV_ROW_BLOCK = 32
WGRAD_ROW_BLOCK = 32
SHIFT_ROWS = HALO - SUBLANES
MIB = 1024 * 1024

HBM = pl.BlockSpec(memory_space=pltpu.HBM)
ANY = pl.BlockSpec(memory_space=pl.ANY)
MESH = pl.DeviceIdType.MESH


def _params(vmem_mib, grid_dims=0):
    kw = dict(vmem_limit_bytes=vmem_mib * MIB)
    if grid_dims:
        kw["dimension_semantics"] = ("arbitrary",) * grid_dims
    return pltpu.CompilerParams(**kw)


def _full(shape):
    return pl.BlockSpec(shape, lambda i: (0,) * len(shape))


def _dot(a, b):
    return jnp.dot(a, b, preferred_element_type=F32)


def _dot_nt(a, b):
    return lax.dot_general(a, b, (((1,), (1,)), ((), ())), preferred_element_type=F32)


def _dot_tn(a, b):
    return lax.dot_general(a, b, (((0,), (0,)), ((), ())), preferred_element_type=F32)


def _normal_cdf(x):
    return 0.5 * (1.0 + lax.erf(x * INV_SQRT2))


def _gelu_grad(x, cdf):
    return cdf + x * jnp.exp(-0.5 * x * x) * INV_SQRT_2PI


def _ln_fwd(v):
    mu = jnp.mean(v, axis=-1, keepdims=True)
    d = v - mu
    var = jnp.mean(d * d, axis=-1, keepdims=True)
    rstd = lax.rsqrt(var + LN_EPS)
    return d * rstd, rstd


def _ln_bwd(dyhat, yhat, rstd):
    m1 = jnp.mean(dyhat, axis=-1, keepdims=True)
    m2 = jnp.mean(dyhat * yhat, axis=-1, keepdims=True)
    return rstd * (dyhat - m1 - yhat * m2)


def _colsum(v):
    return jnp.sum(v, axis=0, keepdims=True)


def _head_pair_stack(v, lo):
    return jnp.concatenate([jnp.where(lo, v, 0.0), jnp.where(lo, 0.0, v)], axis=0).astype(BF16)


def _lo_mask():
    return lax.broadcasted_iota(jnp.int32, (CHUNK, CHUNK), 1) < (CHUNK // 2)


def _head_selector():
    head = lax.broadcasted_iota(jnp.int32, (N_HEADS, D_SGU), 0)
    lane = lax.broadcasted_iota(jnp.int32, (N_HEADS, D_SGU), 1)
    width = D_SGU // N_HEADS
    return ((lane >= head * width) & (lane < (head + 1) * width)).astype(F32)


def _shifted_copies(pad_ref, sh_ref, rows):
    for r in range(1, SUBLANES):
        sh_ref[r - 1, 0:rows, :] = pad_ref[pl.ds(r, rows), :]


def _tap_groups(offset_of_tap):
    groups = {}
    for k in range(CONV_WIDTH):
        o = offset_of_tap(k)
        groups.setdefault(o % SUBLANES, []).append((k, o // SUBLANES))
    return groups


def _tap_window(pad_ref, sh_ref, r, taps, row0, rows):
    q0 = min(q for _, q in taps)
    q1 = max(q for _, q in taps)
    src = pad_ref if r == 0 else sh_ref.at[r - 1]
    win = src[pl.ds(row0 + SUBLANES * q0, SUBLANES * (q1 - q0) + rows), :]
    return win, [(k, SUBLANES * (q - q0)) for k, q in taps]


def _causal_conv(pad_ref, sh_ref, w_ref, out_ref, rows, offset_of_tap, bias=None):
    groups = _tap_groups(offset_of_tap)

    def block(b, carry):
        row0 = pl.multiple_of(b * CONV_ROW_BLOCK, CONV_ROW_BLOCK)
        if bias is None:
            acc = jnp.zeros((CONV_ROW_BLOCK, D_CONV), F32)
        else:
            acc = jnp.broadcast_to(bias, (CONV_ROW_BLOCK, D_CONV))
        for r, taps in groups.items():
            win, starts = _tap_window(pad_ref, sh_ref, r, taps, row0, CONV_ROW_BLOCK)
            for k, s in starts:
                acc = acc + w_ref[k:k + 1, :] * win[s:s + CONV_ROW_BLOCK, :]
        out_ref[pl.ds(row0, CONV_ROW_BLOCK), :] = acc
        return carry

    lax.fori_loop(0, rows // CONV_ROW_BLOCK, block, 0)


def _conv_weight_grad(dy_ref, pad_ref, sh_ref, acc_ref, rows, offset_of_tap):
    groups = _tap_groups(offset_of_tap)
    for r, taps in groups.items():

        def block(b, parts, r=r, taps=taps):
            row0 = pl.multiple_of(b * WGRAD_ROW_BLOCK, WGRAD_ROW_BLOCK)
            dyb = dy_ref[pl.ds(row0, WGRAD_ROW_BLOCK), :]
            win, starts = _tap_window(pad_ref, sh_ref, r, taps, row0, WGRAD_ROW_BLOCK)
            out = []
            for part, (_, s) in zip(parts, starts):
                pr = dyb * win[s:s + WGRAD_ROW_BLOCK, :]
                out.append(part + pr.reshape(WGRAD_ROW_BLOCK // SUBLANES, SUBLANES, D_CONV).sum(axis=0))
            return tuple(out)

        zeros = tuple(jnp.zeros((SUBLANES, D_CONV), F32) for _ in taps)
        parts = lax.fori_loop(0, rows // WGRAD_ROW_BLOCK, block, zeros)
        for part, (k, _) in zip(parts, taps):
            acc_ref[k] += part


def _prep_weights(w_in, w_out, w_gate_t, w_up_t, w_down, conv_w, w_s, b_s):
    def compute(ins, outs):
        win_ref, wout_ref, wgt_ref, wut_ref, wd_ref, cw_ref, ws_ref, bs_ref = ins
        win_o, wout_o, wgt_o, wut_o, wd_o, cw_o, wcat_o, wcatt_o, bsf_o = outs
        win_o[...] = win_ref[...].T.astype(BF16)
        wout_o[...] = wout_ref[...].astype(BF16)
        wgt_o[...] = wgt_ref[...].astype(BF16)
        wut_o[...] = wut_ref[...].astype(BF16)
        wd_o[...] = wd_ref[...].astype(BF16)
        cw_o[...] = jnp.zeros(cw_o.shape, F32)
        cw_o[0:CONV_WIDTH, 0:D_CONV // N_DEV] = cw_ref[...]
        row = lax.broadcasted_iota(jnp.int32, (CHUNK, CHUNK), 0)
        col = lax.broadcasted_iota(jnp.int32, (CHUNK, CHUNK), 1)
        causal = row >= col
        for h in range(N_HEADS):
            w = jnp.where(causal, ws_ref[h], 0.0)
            p, half = h // 2, (h % 2) * CHUNK
            wcat_o[p, :, half:half + CHUNK] = w.astype(BF16)
            wcatt_o[p, :, half:half + CHUNK] = w.T.astype(BF16)
        bsf_o[...] = lax.dot_general(bs_ref[...], _head_selector(), (((0,), (0,)), ((), ())),
                                     preferred_element_type=F32, precision=lax.Precision.HIGHEST)

    S = jax.ShapeDtypeStruct
    out_shapes = [S((256, D_MODEL), BF16), S((128, D_MODEL), BF16), S((FF_SHARD, D_MODEL), BF16),
                  S((FF_SHARD, D_MODEL), BF16), S((FF_SHARD, D_MODEL), BF16), S((CONV_ROWS, 128), F32),
                  S((4, CHUNK, 2 * CHUNK), BF16), S((4, CHUNK, 2 * CHUNK), BF16), S((CHUNK, D_SGU), F32)]
    (res,), _ = _staged_call(
        "prep_weights", [([w_in, w_out, w_gate_t, w_up_t, w_down, conv_w, w_s, b_s], out_shapes, compute)], 32)
    return res


def _mesh_position():
    x, y, c = lax.axis_index("x"), lax.axis_index("y"), lax.axis_index("c")
    return x, y, c


def _peers(x, y, c):
    out = []
    for k in range(1, N_DEV):
        px = 1 - x if (k >> 2) & 1 else x
        py = 1 - y if (k >> 1) & 1 else y
        pc = 1 - c if k & 1 else c
        out.append(((px, py, pc), 4 * px + 2 * py + pc))
    return out


class _Exchange:
    def __init__(self, scatter, gather):
        self.arrays = list(scatter) + list(gather)
        self.n_sc = len(scatter)
        self.n = len(self.arrays)
        self.out_shape = [jax.ShapeDtypeStruct(a.shape if k < self.n_sc else (N_DEV,) + a.shape, a.dtype)
                          for k, a in enumerate(self.arrays)]
        n_remote = self.n * (N_DEV - 1)
        self.scratch = [pltpu.SemaphoreType.DMA((n_remote,)), pltpu.SemaphoreType.DMA((n_remote,)),
                        pltpu.SemaphoreType.DMA((self.n,))] if self.n else []

    def _copies(self, src, dst, sems):
        send_sems, recv_sems, local_sems = sems
        x, y, c = _mesh_position()
        me = 4 * x + 2 * y + c
        locals_, first, arrivals, passed, last = [], [], [], [], []

        def remote(a, k, src_ref, slot, to):
            s = a * (N_DEV - 1) + k
            return pltpu.make_async_remote_copy(src_ref=src_ref, dst_ref=dst[a].at[slot], send_sem=send_sems.at[s],
                                                recv_sem=recv_sems.at[s], device_id=to, device_id_type=MESH)

        for a in range(self.n):
            if a < self.n_sc:
                locals_.append(pltpu.make_async_copy(src[a].at[me], dst[a].at[me], local_sems.at[a]))
                for k, (peer, pid) in enumerate(_peers(x, y, c)):
                    first.append(remote(a, k, src[a].at[pid], me, peer))
                    last.append(remote(a, k, src[a].at[pid], pid, peer))
                continue
            locals_.append(pltpu.make_async_copy(src[a], dst[a].at[me], local_sems.at[a]))
            sibling, sib_id = (x, y, 1 - c), 4 * x + 2 * y + (1 - c)
            chips = [(1 - x, y), (x, 1 - y), (1 - x, 1 - y)]
            first.append(remote(a, 0, src[a], me, sibling))
            last.append(remote(a, 0, src[a], sib_id, sibling))
            for j, (px, py) in enumerate(chips):
                same, other = 4 * px + 2 * py + c, 4 * px + 2 * py + (1 - c)
                first.append(remote(a, 1 + j, src[a], me, (px, py, c)))
                arrivals.append(remote(a, 1 + j, src[a], same, (px, py, c)))
                passed.append(remote(a, 4 + j, dst[a].at[same], same, sibling))
                last.append(remote(a, 4 + j, dst[a].at[other], other, sibling))
        return locals_, first, arrivals, passed, last

    def start(self, src, dst, sems):
        if not self.n:
            return
        locals_, first, _, _, _ = self._copies(src, dst, sems)
        for cp in locals_ + first:
            cp.start()

    def forward(self, src, dst, sems):
        if self.n == self.n_sc:
            return
        _, _, arrivals, passed, _ = self._copies(src, dst, sems)
        for arrived, cp in zip(arrivals, passed):
            arrived.wait_recv()
            cp.start()

    def wait(self, src, dst, sems):
        if not self.n:
            return
        locals_, first, _, passed, last = self._copies(src, dst, sems)
        for cp in last:
            cp.wait_recv()
        for cp in first + passed:
            cp.wait_send()
        for cp in locals_:
            cp.wait()


def _exchange(name, scatter, gather):
    ex = _Exchange(scatter, gather)
    n = ex.n

    def body(*refs):
        src, dst, sems = refs[:n], refs[n:2 * n], refs[2 * n:]
        ex.start(src, dst, sems)
        ex.forward(src, dst, sems)
        ex.wait(src, dst, sems)

    return pl.pallas_call(
        body, name=name, out_shape=tuple(ex.out_shape), in_specs=[ANY] * n, out_specs=(ANY,) * n,
        scratch_shapes=ex.scratch,
    )(*ex.arrays)


def _forward_step(n_steps):
    return (11 * n_steps) // 16


def _hosted(ex, refs, n_in, n_out):
    ins, ex_src = refs[:n_in], refs[n_in:n_in + ex.n]
    rest = refs[n_in + ex.n:]
    outs, ex_dst = rest[:n_out], rest[n_out:n_out + ex.n]
    rest = rest[n_out + ex.n:]
    n_own = len(rest) - len(ex.scratch)
    return ins, outs, rest[:n_own], (ex_src, ex_dst, rest[n_own:])


def _fwd_mix(x, win_g, wout_g, sgu_g, sgu_b, wcat, bs_full, cw, cb, cg, cbeta, tm, ex):
    T = x.shape[0]
    nt = T // tm

    def body(*refs):
        ins, outs, scratch, ex_refs = _hosted(ex, refs, 11, 6)
        x_ref, win_ref, wout_ref, sg_ref, sb_ref, wcat_ref, bs_ref, cw_ref, cb_ref, cg_ref, cbeta_ref = ins
        proj_ref, ycat_ref, n1_ref, rstd1_ref, phi_ref, y_ref = outs
        hpad, hshift = scratch
        i = pl.program_id(0)

        @pl.when(i == 0)
        def _():
            ex.start(*ex_refs)

        xf = x_ref[...]
        xb = xf.astype(BF16)
        proj_ref[...] = _dot_nt(xb, win_ref[...])
        cdf = _normal_cdf(proj_ref[:, 0:2 * D_SGU])
        phi_ref[...] = cdf.astype(BF16)
        u = proj_ref[:, 0:D_SGU] * cdf[:, 0:D_SGU]
        v = proj_ref[:, D_SGU:2 * D_SGU] * cdf[:, D_SGU:2 * D_SGU]
        vhat, _ = _ln_fwd(v)
        vn = vhat * sg_ref[...] + sb_ref[...]
        lo = _lo_mask()
        for c in range(tm // CHUNK):
            rows = slice(CHUNK * c, CHUNK * (c + 1))
            for p in range(4):
                lanes = slice(CHUNK * p, CHUNK * (p + 1))
                mixed = _dot(wcat_ref[p], _head_pair_stack(vn[rows, lanes], lo)) + bs_ref[:, lanes]
                ycat_ref[rows, lanes] = (u[rows, lanes] * mixed).astype(BF16)
        base = 2 * D_SGU
        a = proj_ref[:, base:base + D_CONV]
        g = proj_ref[:, base + D_CONV:base + 2 * D_CONV]

        @pl.when(i == 0)
        def _():
            hpad[0:HALO, :] = jnp.zeros((HALO, D_CONV), F32)

        hpad[HALO:HALO + tm, :] = a * jax.nn.sigmoid(g)
        _shifted_copies(hpad, hshift, tm + SHIFT_ROWS)
        _causal_conv(hpad, hshift, cw_ref, y_ref, tm, lambda k: HALO - (CONV_WIDTH - 1) + k, bias=cb_ref[...])
        hpad[0:HALO, :] = hpad[tm:tm + HALO, :]
        yhat, _ = _ln_fwd(y_ref[...])
        yn = yhat * cg_ref[...] + cbeta_ref[...]
        ycat_ref[:, D_SGU:D_SGU + D_CONV] = (yn * jax.nn.sigmoid(yn)).astype(BF16)
        r1 = ALPHA * xf + _dot(ycat_ref[...], wout_ref[...])
        n1, rstd1 = _ln_fwd(r1)
        n1_ref[...] = n1
        rstd1_ref[...] = rstd1

        @pl.when(i == _forward_step(nt))
        def _():
            ex.forward(*ex_refs)

        @pl.when(i == nt - 1)
        def _():
            ex.wait(*ex_refs)

    S = jax.ShapeDtypeStruct
    row = lambda w: pl.BlockSpec((tm, w), lambda i: (i, 0))
    res = pl.pallas_call(
        body, name="fwd_mix", grid=(nt,),
        in_specs=[row(D_MODEL), _full(win_g.shape), _full(wout_g.shape), _full(sgu_g.shape), _full(sgu_b.shape),
                  _full(wcat.shape), _full(bs_full.shape), _full(cw.shape), _full(cb.shape), _full(cg.shape),
                  _full(cbeta.shape)] + [ANY] * ex.n,
        out_specs=(row(2 * D_MODEL), row(D_MODEL), row(D_MODEL), row(1), row(2 * D_SGU), row(D_CONV)) + (ANY,) * ex.n,
        out_shape=(S((T, 2 * D_MODEL), F32), S((T, D_MODEL), BF16), S((T, D_MODEL), F32), S((T, 1), F32),
                   S((T, 2 * D_SGU), BF16), S((T, D_CONV), F32), *ex.out_shape),
        scratch_shapes=[pltpu.VMEM((tm + HALO, D_CONV), F32),
                        pltpu.VMEM((SUBLANES - 1, tm + SHIFT_ROWS, D_CONV), F32)] + ex.scratch,
        compiler_params=_params(56, 1),
    )(x, win_g, wout_g, sgu_g, sgu_b, wcat, bs_full, cw, cb, cg, cbeta, *ex.arrays)
    return res[:6], res[6:]


def _load_resident(pairs, sems):
    cps = [pltpu.make_async_copy(s, d, sems.at[k]) for k, (s, d) in enumerate(pairs)]
    for cp in cps:
        cp.start()
    for cp in cps:
        cp.wait()


def _fwd_mlp(n1, tgt, l1g, l1b, l2g, l2b, wgt, wut, wd, tm):
    T = n1.shape[0]
    nt = T // tm
    nf = D_FF // MXU_COLS

    def body(n1_ref, tgt_ref, l1g_ref, l1b_ref, l2g_ref, l2b_ref, wg_hbm, wu_hbm, wd_hbm,
             gate_ref, up_ref, hh_ref, dr2_ref, stat_ref, wg_s, wu_s, wd_s, sems):
        i = pl.program_id(0)

        @pl.when(i == 0)
        def _():
            _load_resident([(wg_hbm, wg_s), (wu_hbm, wu_s), (wd_hbm, wd_s)], sems)
            stat_ref[...] = jnp.zeros(stat_ref.shape, F32)

        x1 = n1_ref[...] * l1g_ref[...] + l1b_ref[...]
        x1b = x1.astype(BF16)
        for f in range(nf):
            cols = slice(MXU_COLS * f, MXU_COLS * (f + 1))
            gt = _dot_nt(x1b, wg_s[cols, :])
            ut = _dot_nt(x1b, wu_s[cols, :])
            gate_ref[:, cols] = gt.astype(BF16)
            up_ref[:, cols] = ut.astype(BF16)
            hh_ref[:, cols] = (gt * jax.nn.sigmoid(gt) * ut).astype(BF16)
        r2 = ALPHA * x1 + _dot(hh_ref[...], wd_s[...])
        n2, rstd2 = _ln_fwd(r2)
        x2 = n2 * l2g_ref[...] + l2b_ref[...]
        diff = x2 - tgt_ref[...]
        dx2 = diff * (1.0 / D_MODEL)
        stat_ref[0:1, :] += _colsum(diff * diff)
        stat_ref[1:2, :] += _colsum(dx2 * n2)
        stat_ref[2:3, :] += _colsum(dx2)
        dr2_ref[...] = _ln_bwd(dx2 * l2g_ref[...], n2, rstd2)

    S = jax.ShapeDtypeStruct
    row = lambda w: pl.BlockSpec((tm, w), lambda i: (i, 0))
    vec = _full((1, D_MODEL))
    return pl.pallas_call(
        body, name="fwd_mlp", grid=(nt,),
        in_specs=[row(D_MODEL), row(D_MODEL), vec, vec, vec, vec, ANY, ANY, ANY],
        out_specs=(row(D_FF), row(D_FF), row(D_FF), row(D_MODEL), _full((8, D_MODEL))),
        out_shape=(S((T, D_FF), BF16), S((T, D_FF), BF16), S((T, D_FF), BF16), S((T, D_MODEL), F32),
                   S((8, D_MODEL), F32)),
        scratch_shapes=[pltpu.VMEM((D_FF, D_MODEL), BF16)] * 3 + [pltpu.SemaphoreType.DMA((3,))],
        compiler_params=_params(56, 1),
    )(n1, tgt, l1g, l1b, l2g, l2b, wgt, wut, wd)


def _bwd_mlp(dr2, gate, up, n1, rstd1, l1g, wgt, wut, wd, tm):
    T = n1.shape[0]
    nt = T // tm
    nf = D_FF // MXU_COLS

    def body(dr2_ref, gate_ref, up_ref, n1_ref, rstd1_ref, l1g_ref, wg_hbm, wu_hbm, wd_hbm,
             dgate_ref, dup_ref, dr1_ref, stat_ref, wg_s, wu_s, wd_s, sems):
        i = pl.program_id(0)

        @pl.when(i == 0)
        def _():
            _load_resident([(wg_hbm, wg_s), (wu_hbm, wu_s), (wd_hbm, wd_s)], sems)
            stat_ref[...] = jnp.zeros(stat_ref.shape, F32)

        dr2 = dr2_ref[...]
        dr2b = dr2.astype(BF16)
        for f in range(nf):
            cols = slice(MXU_COLS * f, MXU_COLS * (f + 1))
            dhh = _dot_nt(dr2b, wd_s[cols, :])
            gt = gate_ref[:, cols].astype(F32)
            ut = up_ref[:, cols].astype(F32)
            sg = jax.nn.sigmoid(gt)
            dgate_ref[:, cols] = (dhh * ut * (sg * (1.0 + gt * (1.0 - sg)))).astype(BF16)
            dup_ref[:, cols] = (dhh * (gt * sg)).astype(BF16)
        dx1 = ALPHA * dr2 + _dot(dgate_ref[...], wg_s[...]) + _dot(dup_ref[...], wu_s[...])
        n1 = n1_ref[...]
        stat_ref[0:1, :] += _colsum(dx1 * n1)
        stat_ref[1:2, :] += _colsum(dx1)
        dr1_ref[...] = _ln_bwd(dx1 * l1g_ref[...], n1, rstd1_ref[...])

    S = jax.ShapeDtypeStruct
    row = lambda w: pl.BlockSpec((tm, w), lambda i: (i, 0))
    return pl.pallas_call(
        body, name="bwd_mlp", grid=(nt,),
        in_specs=[row(D_MODEL), row(D_FF), row(D_FF), row(D_MODEL), row(1), _full((1, D_MODEL)), ANY, ANY, ANY],
        out_specs=(row(D_FF), row(D_FF), row(D_MODEL), _full((8, D_MODEL))),
        out_shape=(S((T, D_FF), BF16), S((T, D_FF), BF16), S((T, D_MODEL), F32), S((8, D_MODEL), F32)),
        scratch_shapes=[pltpu.VMEM((D_FF, D_MODEL), BF16)] * 3 + [pltpu.SemaphoreType.DMA((3,))],
        compiler_params=_params(56, 1),
    )(dr2, gate, up, n1, rstd1, l1g, wgt, wut, wd)


def _bwd_mix(dr1, proj, phi, y, win_g, wout_g, sgu_g, sgu_b, wcat, wcatt, bs_full, cw, cg, cbeta, tm, ex):
    T = dr1.shape[0]
    nt = T // tm
    halo_blocks = tm // HALO

    def body(*refs):
        ins, outs, scratch, ex_refs = _hosted(ex, refs, 15, 6)
        (dr1_ref, proj_ref, halo_ref, phi_ref, y_ref, win_ref, wout_ref, sg_ref, sb_ref, wcat_ref, wcatt_ref, bs_ref,
         cw_ref, cg_ref, cbeta_ref) = ins
        gx_ref, dproj_ref, gws_out, gbs_out, gcw_ref, vec_ref = outs
        hpad, shift, dypad, dhbuf, dubuf, dvnbuf, gcw_acc, gws_ref, gbs_ref = scratch
        i = pl.program_id(0)
        tile = nt - 1 - i

        @pl.when(i == 0)
        def _():
            ex.start(*ex_refs)
            gws_ref[...] = jnp.zeros(gws_ref.shape, F32)
            gbs_ref[...] = jnp.zeros(gbs_ref.shape, F32)
            gcw_ref[...] = jnp.zeros(gcw_ref.shape, F32)
            vec_ref[...] = jnp.zeros(vec_ref.shape, F32)
            gcw_acc[...] = jnp.zeros(gcw_acc.shape, F32)
            dypad[tm:tm + HALO, :] = jnp.zeros((HALO, D_CONV), F32)

        dr1 = dr1_ref[...]
        dycat = _dot_nt(dr1.astype(BF16), wout_ref[...])
        pu = proj_ref[:, 0:D_SGU]
        pv = proj_ref[:, D_SGU:2 * D_SGU]
        cdf_u = phi_ref[:, 0:D_SGU].astype(F32)
        cdf_v = phi_ref[:, D_SGU:2 * D_SGU].astype(F32)
        u = pu * cdf_u
        vhat, rstd_v = _ln_fwd(pv * cdf_v)
        vn = vhat * sg_ref[...] + sb_ref[...]
        lo = _lo_mask()
        for c in range(tm // CHUNK):
            rows = slice(CHUNK * c, CHUNK * (c + 1))
            for p in range(4):
                lanes = slice(CHUNK * p, CHUNK * (p + 1))
                vstack = _head_pair_stack(vn[rows, lanes], lo)
                mixed = _dot(wcat_ref[p], vstack) + bs_ref[:, lanes]
                d_a = dycat[rows, lanes]
                dubuf[rows, lanes] = d_a * mixed
                dm = d_a * u[rows, lanes]
                gbs_ref[:, lanes] += dm
                dstack = _head_pair_stack(dm, lo)
                gws_ref[2 * CHUNK * p:2 * CHUNK * (p + 1), :] += _dot_nt(dstack, vn[rows, lanes].astype(BF16))
                dvnbuf[rows, lanes] = _dot(wcatt_ref[p], dstack)
        dvn = dvnbuf[...]
        vec_ref[0:1, :] += _colsum(dvn * vhat)
        vec_ref[1:2, :] += _colsum(dvn)
        dv = _ln_bwd(dvn * sg_ref[...], vhat, rstd_v)
        dproj_ref[:, 0:D_SGU] = (dubuf[...] * _gelu_grad(pu, cdf_u)).astype(BF16)
        dproj_ref[:, D_SGU:2 * D_SGU] = (dv * _gelu_grad(pv, cdf_v)).astype(BF16)
        base = 2 * D_SGU
        a = proj_ref[:, base:base + D_CONV]
        sgm = jax.nn.sigmoid(proj_ref[:, base + D_CONV:base + 2 * D_CONV])
        h_before = halo_ref[:, 0:D_CONV] * jax.nn.sigmoid(halo_ref[:, D_CONV:2 * D_CONV])
        hpad[0:HALO, :] = jnp.where(tile > 0, h_before, 0.0)
        hpad[HALO:HALO + tm, :] = a * sgm
        _shifted_copies(hpad, shift, tm + SHIFT_ROWS)
        h_offset = lambda k: HALO - (CONV_WIDTH - 1) + k
        yhat, rstd_y = _ln_fwd(y_ref[...])
        yn = yhat * cg_ref[...] + cbeta_ref[...]
        s = jax.nn.sigmoid(yn)
        dyn = dycat[:, D_SGU:D_SGU + D_CONV] * (s * (1.0 + yn * (1.0 - s)))
        vec_ref[3:4, :] += _colsum(dyn * yhat)
        vec_ref[4:5, :] += _colsum(dyn)
        dy = _ln_bwd(dyn * cg_ref[...], yhat, rstd_y)
        vec_ref[2:3, :] += _colsum(dy)
        dypad[0:tm, :] = dy
        _conv_weight_grad(dypad, hpad, shift, gcw_acc, tm, h_offset)
        _shifted_copies(dypad, shift, tm + SHIFT_ROWS)
        _causal_conv(dypad, shift, cw_ref, dhbuf, tm, lambda k: (CONV_WIDTH - 1) - k)
        dypad[tm:tm + HALO, :] = dypad[0:HALO, :]
        dh = dhbuf[...]
        dproj_ref[:, base:base + D_CONV] = (dh * sgm).astype(BF16)
        dproj_ref[:, base + D_CONV:base + 2 * D_CONV] = (dh * a * sgm * (1.0 - sgm)).astype(BF16)
        gx_ref[...] = ALPHA * dr1 + _dot(dproj_ref[...], win_ref[...])

        @pl.when(i == _forward_step(nt))
        def _():
            ex.forward(*ex_refs)

        @pl.when(i == nt - 1)
        def _():
            gcw_ref[...] = gcw_acc[...].sum(axis=1)
            gws_out[...] = gws_ref[...].astype(BF16)
            gbs_out[...] = lax.dot_general(_head_selector(), gbs_ref[...], (((1,), (1,)), ((), ())),
                                           preferred_element_type=F32, precision=lax.Precision.HIGHEST)
            ex.wait(*ex_refs)

    S = jax.ShapeDtypeStruct
    row = lambda w: pl.BlockSpec((tm, w), lambda i: (nt - 1 - i, 0))
    halo = pl.BlockSpec((HALO, D_MODEL), lambda i: (jnp.maximum((nt - 1 - i) * halo_blocks - 1, 0), 1))
    res = pl.pallas_call(
        body, name="bwd_mix", grid=(nt,),
        in_specs=[row(D_MODEL), row(2 * D_MODEL), halo, row(2 * D_SGU), row(D_CONV), _full(win_g.shape),
                  _full(wout_g.shape), _full(sgu_g.shape), _full(sgu_b.shape), _full(wcat.shape), _full(wcatt.shape),
                  _full(bs_full.shape), _full(cw.shape), _full(cg.shape), _full(cbeta.shape)]
        + [ANY] * ex.n,
        out_specs=(row(D_MODEL), row(2 * D_MODEL), _full((N_HEADS * CHUNK, CHUNK)), _full((N_HEADS, CHUNK)),
                   _full((CONV_ROWS, D_CONV)), _full((8, D_CONV))) + (ANY,) * ex.n,
        out_shape=(S((T, D_MODEL), F32), S((T, 2 * D_MODEL), BF16), S((N_HEADS * CHUNK, CHUNK), BF16),
                   S((N_HEADS, CHUNK), F32), S((CONV_ROWS, D_CONV), F32), S((8, D_CONV), F32), *ex.out_shape),
        scratch_shapes=[pltpu.VMEM((tm + HALO, D_CONV), F32), pltpu.VMEM((SUBLANES - 1, tm + SHIFT_ROWS, D_CONV), F32),
                        pltpu.VMEM((tm + HALO, D_CONV), F32),
                        pltpu.VMEM((tm, D_CONV), F32), pltpu.VMEM((tm, D_SGU), F32),
                        pltpu.VMEM((tm, D_SGU), F32), pltpu.VMEM((CONV_ROWS, 8, D_CONV), F32),
                        pltpu.VMEM((N_HEADS * CHUNK, CHUNK), F32), pltpu.VMEM((CHUNK, D_SGU), F32)] + ex.scratch,
        compiler_params=_params(56, 1),
    )(dr1, proj, proj, phi, y, win_g, wout_g, sgu_g, sgu_b, wcat, wcatt, bs_full, cw, cg, cbeta, *ex.arrays)
    return res[:6], res[6:]


def _wgrad(name, a, b, blocks, tk, ex=None, b_cols=None, b_affine=None):
    T, M = a.shape
    col, N = b_cols or (0, b.shape[1])
    nk = T // tk
    out_shape = (blocks, M // blocks, N)
    ex = ex or _Exchange([], [])
    affine = list(b_affine or [])

    def body(*refs):
        ins, (o_ref,), (acc,), ex_refs = _hosted(ex, refs, 2 + len(affine), 1)
        a_ref, b_ref = ins[:2]
        i = pl.program_id(0)

        @pl.when(i == 0)
        def _():
            ex.start(*ex_refs)
            acc[...] = jnp.zeros(acc.shape, F32)

        right = b_ref[...]
        if affine:
            right = right * ins[2][...] + ins[3][...]
        acc[...] += _dot_tn(a_ref[...].astype(BF16), right.astype(BF16))

        @pl.when(i == _forward_step(nk))
        def _():
            ex.forward(*ex_refs)

        @pl.when(i == nk - 1)
        def _():
            o_ref[...] = acc[...].astype(BF16)
            ex.wait(*ex_refs)

    res = pl.pallas_call(
        body, name=name, grid=(nk,),
        in_specs=[pl.BlockSpec((tk, M), lambda i: (i, 0)), pl.BlockSpec((tk, N), lambda i: (i, col))]
        + [_full((1, N))] * len(affine) + [ANY] * ex.n,
        out_specs=(_full((M, N)),) + (ANY,) * ex.n,
        out_shape=(jax.ShapeDtypeStruct((M, N), BF16), *ex.out_shape),
        scratch_shapes=[pltpu.VMEM((M, N), F32)] + ex.scratch,
        compiler_params=_params(56, 1),
    )(a, b, *affine, *ex.arrays)
    g = res[0].reshape(out_shape)
    return (g, res[1:]) if ex.n else g


def _adamw(w, g, m, v):
    m2 = ADAM_B1 * m + (1.0 - ADAM_B1) * g
    v2 = ADAM_B2 * v + (1.0 - ADAM_B2) * (g * g)
    m_hat = m2 / (1.0 - ADAM_B1 ** ADAM_STEP)
    v_hat = v2 / (1.0 - ADAM_B2 ** ADAM_STEP)
    delta = -ADAM_LR * (m_hat / (jnp.sqrt(v_hat) + ADAM_EPS) + ADAM_WD * w)
    return delta, m2, v2


def _sum_partials(r_ref):
    g = r_ref[0].astype(F32)
    for s in range(1, N_DEV):
        g = g + r_ref[s].astype(F32)
    return g


def _staged_call(name, groups, vmem_mib, ex=None):
    ex = ex or _Exchange([], [])
    inputs = [a for ins, _, _ in groups for a in ins]
    out_shapes = [s for _, outs, _ in groups for s in outs]
    n_in, n_out = len(inputs), len(out_shapes)

    def body(*refs):
        ins, outs, scratch, ex_refs = _hosted(ex, refs, n_in, n_out)
        in_bufs, out_bufs, sems = scratch[:n_in], scratch[n_in:n_in + n_out], scratch[n_in + n_out]
        ex.start(*ex_refs)
        loads = [pltpu.make_async_copy(ins[k], in_bufs[k], sems.at[k]) for k in range(n_in)]
        stores = [pltpu.make_async_copy(out_bufs[k], outs[k], sems.at[n_in + k]) for k in range(n_out)]
        for cp in loads:
            cp.start()
        i0 = o0 = 0
        for g_ins, g_outs, compute in groups:
            i1, o1 = i0 + len(g_ins), o0 + len(g_outs)
            for cp in loads[i0:i1]:
                cp.wait()
            compute(in_bufs[i0:i1], out_bufs[o0:o1])
            for cp in stores[o0:o1]:
                cp.start()
            i0, o0 = i1, o1
        for cp in stores:
            cp.wait()
        ex.forward(*ex_refs)
        ex.wait(*ex_refs)

    scratch = ([pltpu.VMEM(a.shape, a.dtype) for a in inputs] + [pltpu.VMEM(s.shape, s.dtype) for s in out_shapes]
               + [pltpu.SemaphoreType.DMA((n_in + n_out,))] + ex.scratch)
    res = pl.pallas_call(
        body, name=name, out_shape=(*[pltpu.HBM(s.shape, s.dtype) for s in out_shapes], *ex.out_shape),
        in_specs=[HBM] * n_in + [ANY] * ex.n, out_specs=(HBM,) * n_out + (ANY,) * ex.n,
        scratch_shapes=scratch, compiler_params=_params(vmem_mib),
    )(*[pltpu.with_memory_space_constraint(a, pltpu.HBM) for a in inputs], *ex.arrays)
    per_group, o0 = [], 0
    for _, g_outs, _ in groups:
        per_group.append(list(res[o0:o0 + len(g_outs)]))
        o0 += len(g_outs)
    return per_group, res[n_out:]


def _adamw_shard_group(parts, w, m, v, transposed):
    n = len(parts)

    def compute(ins, outs):
        w_ref, m_ref, v_ref = ins[n:]
        lo = 0
        for r_ref in ins[:n]:
            g = _sum_partials(r_ref)
            cols = g.shape[1]
            if transposed:
                g, at = g.T, (slice(lo, lo + cols), slice(None))
            else:
                at = (slice(None), slice(lo, lo + cols))
            delta, m2, v2 = _adamw(w_ref[at], g, m_ref[at], v_ref[at])
            for o, val in zip(outs, (g, delta, m2, v2)):
                o[at] = val
            lo += cols

    return [*parts, w, m, v], [jax.ShapeDtypeStruct(w.shape, F32)] * 4, compute


SMALL_NAMES = ["sgu_ln_g", "sgu_ln_b", "w_s", "b_s", "conv_b", "conv_ln_g", "conv_ln_b", "ln1_g", "ln1_b", "ln2_g", "ln2_b",
               "conv_w"]


def _finish_small_group(gws8, gbs8, gcw8, vmix8, vmlp8, vout8, small):
    names = SMALL_NAMES
    flat = []
    for n in names:
        flat += list(small[n])
    cw_block = D_CONV // N_DEV

    def compute(ins, outs):
        gws_ref, gbs_ref, gcw_ref, vmix_ref, vmlp_ref, vout_ref = ins[:6]
        wmv = ins[6:]
        loss_o = outs[0]
        outs = outs[1:]
        gws = _sum_partials(gws_ref)
        gbs = _sum_partials(gbs_ref)
        vmix = _sum_partials(vmix_ref)
        vmlp = _sum_partials(vmlp_ref)
        vout = _sum_partials(vout_ref)
        x, y, c = _mesh_position()
        first = (4 * x + 2 * y + c) * cw_block
        pick = (lax.broadcasted_iota(jnp.int32, (D_CONV, cw_block), 0)
                == first + lax.broadcasted_iota(jnp.int32, (D_CONV, cw_block), 1)).astype(F32)
        gcw = jnp.dot(_sum_partials(gcw_ref), pick, preferred_element_type=F32,
                      precision=lax.Precision.HIGHEST)[0:CONV_WIDTH, :]
        loss = (0.5 / D_MODEL) * jnp.sum(vout[0:1, :], axis=1, keepdims=True)
        loss_o[...] = jnp.broadcast_to(loss, loss_o.shape)
        rows = lax.broadcasted_iota(jnp.int32, (N_HEADS * CHUNK, CHUNK), 0)
        cols = lax.broadcasted_iota(jnp.int32, (N_HEADS * CHUNK, CHUNK), 1)
        gws = jnp.where((rows & (CHUNK - 1)) >= cols, gws, 0.0)
        grads = {
            "sgu_ln_g": vmix[0:1, :], "sgu_ln_b": vmix[1:2, :], "w_s": gws, "b_s": gbs,
            "conv_b": vmix[2:3, :], "conv_ln_g": vmix[3:4, :], "conv_ln_b": vmix[4:5, :],
            "ln1_g": vmlp[0:1, :], "ln1_b": vmlp[1:2, :], "ln2_g": vout[1:2, :], "ln2_b": vout[2:3, :],
            "conv_w": gcw,
        }
        for k, n in enumerate(names):
            w_ref, m_ref, v_ref = wmv[3 * k:3 * k + 3]
            g = grads[n]
            delta, m2, v2 = _adamw(w_ref[...], g, m_ref[...], v_ref[...])
            outs[4 * k][...] = g
            outs[4 * k + 1][...] = delta
            outs[4 * k + 2][...] = m2
            outs[4 * k + 3][...] = v2

    S = jax.ShapeDtypeStruct
    out_shape = [S((SUBLANES, 128), F32)]
    for n in names:
        out_shape += [S(small[n][0].shape, F32)] * 4
    return [gws8, gbs8, gcw8, vmix8, vmlp8, vout8, *flat], out_shape, compute


TOKEN_TILE_FWD_MIX = 512
TOKEN_TILE_BWD_MIX = 512
TOKEN_TILE_FWD_MLP = 512
TOKEN_TILE_BWD_MLP = 512
TOKEN_TILE_WGRAD = 1024


def kernel(x, w_in, sgu_ln_g, sgu_ln_b, w_s, b_s, conv_w, conv_b, conv_ln_g, conv_ln_b, w_out, ln1_g, ln1_b, w_gate, w_up, w_down, ln2_g, ln2_b, loss_target, m_w_in, m_sgu_ln_g, m_sgu_ln_b, m_w_s, m_b_s, m_conv_w, m_conv_b, m_conv_ln_g, m_conv_ln_b, m_w_out, m_ln1_g, m_ln1_b, m_w_gate, m_w_up, m_w_down, m_ln2_g, m_ln2_b, v_w_in, v_sgu_ln_g, v_sgu_ln_b, v_w_s, v_b_s, v_conv_w, v_conv_b, v_conv_ln_g, v_conv_ln_b, v_w_out, v_ln1_g, v_ln1_b, v_w_gate, v_w_up, v_w_down, v_ln2_g, v_ln2_b):
    xs = x[0]
    tgt = loss_target[0]

    (win_b, wout_b, wgt_b, wut_b, wd_b, cw_b, wcat, wcatt, bs_full) = _prep_weights(
        w_in[0], w_out[0], w_gate[0].T, w_up[0].T, w_down[0], conv_w[0], w_s[0], b_s[0])
    win_g, wout_g, cw_g = _exchange("gather_mix_weights", [], [win_b, wout_b, cw_b])
    win_g = win_g.reshape(2 * D_MODEL, D_MODEL)
    wout_g = wout_g.reshape(D_MODEL, D_MODEL)
    cw = jnp.transpose(cw_g[:, :, :D_CONV // N_DEV], (1, 0, 2)).reshape(CONV_ROWS, D_CONV)

    (proj, ycat, n1, rstd1, phi, y_conv), (wgt_g, wut_g, wd_g) = _fwd_mix(
        xs, win_g, wout_g, sgu_ln_g, sgu_ln_b, wcat, bs_full, cw, conv_b, conv_ln_g, conv_ln_b, TOKEN_TILE_FWD_MIX,
        _Exchange([], [wgt_b, wut_b, wd_b]))
    wgt_g = wgt_g.reshape(D_FF, D_MODEL)
    wut_g = wut_g.reshape(D_FF, D_MODEL)
    wd_g = wd_g.reshape(D_FF, D_MODEL)
    gate, up, hh, dr2, vout = _fwd_mlp(n1, tgt, ln1_g, ln1_b, ln2_g, ln2_b, wgt_g, wut_g, wd_g, TOKEN_TILE_FWD_MLP)

    dgate, dup, dr1, vmlp = _bwd_mlp(dr2, gate, up, n1, rstd1, ln1_g, wgt_g, wut_g, wd_g, TOKEN_TILE_BWD_MLP)
    tk = TOKEN_TILE_WGRAD
    x1 = dict(b_affine=(ln1_g, ln1_b))
    g_wgt = _wgrad("wgrad_gate", dgate, n1, N_DEV, tk, **x1)
    g_wut = _wgrad("wgrad_up", dup, n1, N_DEV, tk, **x1)
    g_wd = _wgrad("wgrad_down", hh, dr2, N_DEV, tk)
    g_wout = _wgrad("wgrad_out", ycat, dr1, N_DEV, tk)
    (gx, dproj, gws, gbs, gcw, vmix), (r_wgt, r_wut, r_wd, r_wout) = _bwd_mix(
        dr1, proj, phi, y_conv, win_g, wout_g, sgu_ln_g, sgu_ln_b, wcat, wcatt, bs_full, cw, conv_ln_g, conv_ln_b,
        TOKEN_TILE_BWD_MIX, _Exchange([g_wgt, g_wut, g_wd, g_wout], []))
    half = D_MODEL // 2
    g_win_a, (gws8, gbs8, gcw8, vmix8, vmlp8, vout8) = _wgrad(
        "wgrad_in_a", dproj, xs, N_DEV, tk, _Exchange([], [gws, gbs, gcw, vmix, vmlp, vout]), b_cols=(0, half))
    g_win_b, (r_win_a,) = _wgrad("wgrad_in_b", dproj, xs, N_DEV, tk, _Exchange([g_win_a], []), b_cols=(1, half))
    (r_win_b,) = _exchange("exchange_grad_in", [g_win_b], [])
    (u_gate, u_up, u_down), _ = _staged_call(
        "adamw_mlp",
        [_adamw_shard_group([r_wgt], w_gate[0].T, m_w_gate[0].T, v_w_gate[0].T, False),
         _adamw_shard_group([r_wut], w_up[0].T, m_w_up[0].T, v_w_up[0].T, False),
         _adamw_shard_group([r_wd], w_down[0], m_w_down[0], v_w_down[0], False)], 56)
    small_in = {
        "conv_w": (conv_w[0], m_conv_w[0], v_conv_w[0]),
        "sgu_ln_g": (sgu_ln_g, m_sgu_ln_g, v_sgu_ln_g), "sgu_ln_b": (sgu_ln_b, m_sgu_ln_b, v_sgu_ln_b),
        "w_s": tuple(a.reshape(N_HEADS * CHUNK, CHUNK) for a in (w_s, m_w_s, v_w_s)),
        "b_s": (b_s[0], m_b_s[0], v_b_s[0]),
        "conv_b": (conv_b, m_conv_b, v_conv_b), "conv_ln_g": (conv_ln_g, m_conv_ln_g, v_conv_ln_g),
        "conv_ln_b": (conv_ln_b, m_conv_ln_b, v_conv_ln_b),
        "ln1_g": (ln1_g, m_ln1_g, v_ln1_g), "ln1_b": (ln1_b, m_ln1_b, v_ln1_b),
        "ln2_g": (ln2_g, m_ln2_g, v_ln2_g), "ln2_b": (ln2_b, m_ln2_b, v_ln2_b),
    }
    (u_in, u_out, fin), _ = _staged_call(
        "adamw_mix_small",
        [_adamw_shard_group([r_win_a, r_win_b], w_in[0], m_w_in[0], v_w_in[0], True),
         _adamw_shard_group([r_wout], w_out[0], m_w_out[0], v_w_out[0], False),
         _finish_small_group(gws8, gbs8, gcw8, vmix8, vmlp8, vout8, small_in)], 40)
    big = {"w_in": u_in, "w_out": u_out, "w_gate": u_gate, "w_up": u_up, "w_down": u_down}
    loss11 = fin[0]
    small = {n: fin[1 + 4 * k:5 + 4 * k] for k, n in enumerate(SMALL_NAMES)}

    shapes = {"w_s": w_s.shape, "b_s": b_s.shape, "conv_w": conv_w.shape}
    out = {}
    for n, r in big.items():
        out[n] = tuple((a.T if n in ("w_gate", "w_up") else a)[None] for a in r)
    for n, r in small.items():
        out[n] = tuple(a.reshape(shapes[n]) for a in r) if n in shapes else tuple(r)

    order = ["w_in", "sgu_ln_g", "sgu_ln_b", "w_s", "b_s", "conv_w", "conv_b", "conv_ln_g", "conv_ln_b", "w_out",
             "ln1_g", "ln1_b", "w_gate", "w_up", "w_down", "ln2_g", "ln2_b"]
    loss = loss11[0, 0]
    return (loss, gx[None], *[out[n][0] for n in order], *[out[n][1] for n in order],
            *[out[n][2] for n in order], *[out[n][3] for n in order])
```

```python
import jax
import jax.numpy as jnp
from jax import lax
from jax.experimental import pallas as pl
from jax.experimental.pallas import tpu as pltpu

F32 = jnp.float32
BF16 = jnp.bfloat16

D_MODEL = 1024
D_SGU = 512
D_CONV = 512
N_HEADS = 8
CHUNK = 128
CONV_WIDTH = 31
CONV_ROWS = 32
HALO = 32
D_FF = 2816
N_DEV = 8
FF_SHARD = D_FF // N_DEV
ALPHA = (2.0 * 1) ** 0.25
LN_EPS = 1e-5
INV_SQRT2 = 0.7071067811865476
INV_SQRT_2PI = 0.3989422804014327

ADAM_LR = 0.001
ADAM_B1 = 0.9
ADAM_B2 = 0.999
ADAM_EPS = 1e-08
ADAM_WD = 0.01
ADAM_STEP = 10

MXU_COLS = 256
SUBLANES = 8
CONV_ROW_BLOCK = 32
WGRAD_ROW_BLOCK = 32
SHIFT_ROWS = HALO - SUBLANES
MIB = 1024 * 1024

HBM = pl.BlockSpec(memory_space=pltpu.HBM)
ANY = pl.BlockSpec(memory_space=pl.ANY)
MESH = pl.DeviceIdType.MESH


def _params(vmem_mib, grid_dims=0):
    kw = dict(vmem_limit_bytes=vmem_mib * MIB)
    if grid_dims:
        kw["dimension_semantics"] = ("arbitrary",) * grid_dims
    return pltpu.CompilerParams(**kw)


def _full(shape):
    return pl.BlockSpec(shape, lambda i: (0,) * len(shape))


def _dot(a, b):
    return jnp.dot(a, b, preferred_element_type=F32)


def _dot_nt(a, b):
    return lax.dot_general(a, b, (((1,), (1,)), ((), ())), preferred_element_type=F32)


def _dot_tn(a, b):
    return lax.dot_general(a, b, (((0,), (0,)), ((), ())), preferred_element_type=F32)


def _normal_cdf(x):
    return 0.5 * (1.0 + lax.erf(x * INV_SQRT2))


def _gelu_grad(x, cdf):
    return cdf + x * jnp.exp(-0.5 * x * x) * INV_SQRT_2PI


def _ln_fwd(v):
    mu = jnp.mean(v, axis=-1, keepdims=True)
    d = v - mu
    var = jnp.mean(d * d, axis=-1, keepdims=True)
    rstd = lax.rsqrt(var + LN_EPS)
    return d * rstd, rstd


def _ln_bwd(dyhat, yhat, rstd):
    m1 = jnp.mean(dyhat, axis=-1, keepdims=True)
    m2 = jnp.mean(dyhat * yhat, axis=-1, keepdims=True)
    return rstd * (dyhat - m1 - yhat * m2)


def _colsum(v):
    return jnp.sum(v, axis=0, keepdims=True)


def _head_pair_stack(v, lo):
    return jnp.concatenate([jnp.where(lo, v, 0.0), jnp.where(lo, 0.0, v)], axis=0).astype(BF16)


def _lo_mask():
    return lax.broadcasted_iota(jnp.int32, (CHUNK, CHUNK), 1) < (CHUNK // 2)


def _head_selector():
    head = lax.broadcasted_iota(jnp.int32, (N_HEADS, D_SGU), 0)
    lane = lax.broadcasted_iota(jnp.int32, (N_HEADS, D_SGU), 1)
    width = D_SGU // N_HEADS
    return ((lane >= head * width) & (lane < (head + 1) * width)).astype(F32)


def _shifted_copies(pad_ref, sh_ref, rows):
    for r in range(1, SUBLANES):
        sh_ref[r - 1, 0:rows, :] = pad_ref[pl.ds(r, rows), :]


def _tap_groups(offset_of_tap):
    groups = {}
    for k in range(CONV_WIDTH):
        o = offset_of_tap(k)
        groups.setdefault(o % SUBLANES, []).append((k, o // SUBLANES))
    return groups


def _tap_window(pad_ref, sh_ref, r, taps, row0, rows):
    q0 = min(q for _, q in taps)
    q1 = max(q for _, q in taps)
    src = pad_ref if r == 0 else sh_ref.at[r - 1]
    win = src[pl.ds(row0 + SUBLANES * q0, SUBLANES * (q1 - q0) + rows), :]
    return win, [(k, SUBLANES * (q - q0)) for k, q in taps]


def _causal_conv(pad_ref, sh_ref, w_ref, out_ref, rows, offset_of_tap, bias=None):
    groups = _tap_groups(offset_of_tap)

    def block(b, carry):
        row0 = pl.multiple_of(b * CONV_ROW_BLOCK, CONV_ROW_BLOCK)
        if bias is None:
            acc = jnp.zeros((CONV_ROW_BLOCK, D_CONV), F32)
        else:
            acc = jnp.broadcast_to(bias, (CONV_ROW_BLOCK, D_CONV))
        for r, taps in groups.items():
            win, starts = _tap_window(pad_ref, sh_ref, r, taps, row0, CONV_ROW_BLOCK)
            for k, s in starts:
                acc = acc + w_ref[k:k + 1, :] * win[s:s + CONV_ROW_BLOCK, :]
        out_ref[pl.ds(row0, CONV_ROW_BLOCK), :] = acc
        return carry

    lax.fori_loop(0, rows // CONV_ROW_BLOCK, block, 0)


def _conv_weight_grad(dy_ref, pad_ref, sh_ref, acc_ref, rows, offset_of_tap):
    groups = _tap_groups(offset_of_tap)
    for r, taps in groups.items():

        def block(b, parts, r=r, taps=taps):
            row0 = pl.multiple_of(b * WGRAD_ROW_BLOCK, WGRAD_ROW_BLOCK)
            dyb = dy_ref[pl.ds(row0, WGRAD_ROW_BLOCK), :]
            win, starts = _tap_window(pad_ref, sh_ref, r, taps, row0, WGRAD_ROW_BLOCK)
            out = []
            for part, (_, s) in zip(parts, starts):
                pr = dyb * win[s:s + WGRAD_ROW_BLOCK, :]
                out.append(part + pr.reshape(WGRAD_ROW_BLOCK // SUBLANES, SUBLANES, D_CONV).sum(axis=0))
            return tuple(out)

        zeros = tuple(jnp.zeros((SUBLANES, D_CONV), F32) for _ in taps)
        parts = lax.fori_loop(0, rows // WGRAD_ROW_BLOCK, block, zeros)
        for part, (k, _) in zip(parts, taps):
            acc_ref[k] += part


def _prep_weights(w_in, w_out, w_gate_t, w_up_t, w_down, conv_w, w_s, b_s):
    def compute(ins, outs):
        win_ref, wout_ref, wgt_ref, wut_ref, wd_ref, cw_ref, ws_ref, bs_ref = ins
        win_o, wout_o, wgt_o, wut_o, wd_o, cw_o, wcat_o, wcatt_o, bsf_o = outs
        win_o[...] = win_ref[...].T.astype(BF16)
        wout_o[...] = wout_ref[...].astype(BF16)
        wgt_o[...] = wgt_ref[...].astype(BF16)
        wut_o[...] = wut_ref[...].astype(BF16)
        wd_o[...] = wd_ref[...].astype(BF16)
        cw_o[...] = jnp.zeros(cw_o.shape, F32)
        cw_o[0:CONV_WIDTH, 0:D_CONV // N_DEV] = cw_ref[...]
        row = lax.broadcasted_iota(jnp.int32, (CHUNK, CHUNK), 0)
        col = lax.broadcasted_iota(jnp.int32, (CHUNK, CHUNK), 1)
        causal = row >= col
        for h in range(N_HEADS):
            w = jnp.where(causal, ws_ref[h], 0.0)
            p, half = h // 2, (h % 2) * CHUNK
            wcat_o[p, :, half:half + CHUNK] = w.astype(BF16)
            wcatt_o[p, :, half:half + CHUNK] = w.T.astype(BF16)
        bsf_o[...] = lax.dot_general(bs_ref[...], _head_selector(), (((0,), (0,)), ((), ())),
                                     preferred_element_type=F32, precision=lax.Precision.HIGHEST)

    S = jax.ShapeDtypeStruct
    out_shapes = [S((256, D_MODEL), BF16), S((128, D_MODEL), BF16), S((FF_SHARD, D_MODEL), BF16),
                  S((FF_SHARD, D_MODEL), BF16), S((FF_SHARD, D_MODEL), BF16), S((CONV_ROWS, 128), F32),
                  S((4, CHUNK, 2 * CHUNK), BF16), S((4, CHUNK, 2 * CHUNK), BF16), S((CHUNK, D_SGU), F32)]
    (res,), _ = _staged_call(
        "prep_weights", [([w_in, w_out, w_gate_t, w_up_t, w_down, conv_w, w_s, b_s], out_shapes, compute)], 32)
    return res


def _mesh_position():
    x, y, c = lax.axis_index("x"), lax.axis_index("y"), lax.axis_index("c")
    return x, y, c


def _peers(x, y, c):
    out = []
    for k in range(1, N_DEV):
        px = 1 - x if (k >> 2) & 1 else x
        py = 1 - y if (k >> 1) & 1 else y
        pc = 1 - c if k & 1 else c
        out.append(((px, py, pc), 4 * px + 2 * py + pc))
    return out


class _Exchange:
    def __init__(self, scatter, gather):
        self.arrays = list(scatter) + list(gather)
        self.n_sc = len(scatter)
        self.n = len(self.arrays)
        self.out_shape = [jax.ShapeDtypeStruct(a.shape if k < self.n_sc else (N_DEV,) + a.shape, a.dtype)
                          for k, a in enumerate(self.arrays)]
        n_remote = self.n * (N_DEV - 1)
        self.scratch = [pltpu.SemaphoreType.DMA((n_remote,)), pltpu.SemaphoreType.DMA((n_remote,)),
                        pltpu.SemaphoreType.DMA((self.n,))] if self.n else []

    def _copies(self, src, dst, sems):
        send_sems, recv_sems, local_sems = sems
        x, y, c = _mesh_position()
        me = 4 * x + 2 * y + c
        locals_, first, arrivals, passed, last = [], [], [], [], []

        def remote(a, k, src_ref, slot, to):
            s = a * (N_DEV - 1) + k
            return pltpu.make_async_remote_copy(src_ref=src_ref, dst_ref=dst[a].at[slot], send_sem=send_sems.at[s],
                                                recv_sem=recv_sems.at[s], device_id=to, device_id_type=MESH)

        for a in range(self.n):
            if a < self.n_sc:
                locals_.append(pltpu.make_async_copy(src[a].at[me], dst[a].at[me], local_sems.at[a]))
                for k, (peer, pid) in enumerate(_peers(x, y, c)):
                    first.append(remote(a, k, src[a].at[pid], me, peer))
                    last.append(remote(a, k, src[a].at[pid], pid, peer))
                continue
            locals_.append(pltpu.make_async_copy(src[a], dst[a].at[me], local_sems.at[a]))
            sibling, sib_id = (x, y, 1 - c), 4 * x + 2 * y + (1 - c)
            chips = [(1 - x, y), (x, 1 - y), (1 - x, 1 - y)]
            first.append(remote(a, 0, src[a], me, sibling))
            last.append(remote(a, 0, src[a], sib_id, sibling))
            for j, (px, py) in enumerate(chips):
                same, other = 4 * px + 2 * py + c, 4 * px + 2 * py + (1 - c)
                first.append(remote(a, 1 + j, src[a], me, (px, py, c)))
                arrivals.append(remote(a, 1 + j, src[a], same, (px, py, c)))
                passed.append(remote(a, 4 + j, dst[a].at[same], same, sibling))
                last.append(remote(a, 4 + j, dst[a].at[other], other, sibling))
        return locals_, first, arrivals, passed, last

    def start(self, src, dst, sems):
        if not self.n:
            return
        locals_, first, _, _, _ = self._copies(src, dst, sems)
        for cp in locals_ + first:
            cp.start()

    def forward(self, src, dst, sems):
        if self.n == self.n_sc:
            return
        _, _, arrivals, passed, _ = self._copies(src, dst, sems)
        for arrived, cp in zip(arrivals, passed):
            arrived.wait_recv()
            cp.start()

    def wait(self, src, dst, sems):
        if not self.n:
            return
        locals_, first, _, passed, last = self._copies(src, dst, sems)
        for cp in last:
            cp.wait_recv()
        for cp in first + passed:
            cp.wait_send()
        for cp in locals_:
            cp.wait()


def _exchange(name, scatter, gather):
    ex = _Exchange(scatter, gather)
    n = ex.n

    def body(*refs):
        src, dst, sems = refs[:n], refs[n:2 * n], refs[2 * n:]
        ex.start(src, dst, sems)
        ex.forward(src, dst, sems)
        ex.wait(src, dst, sems)

    return pl.pallas_call(
        body, name=name, out_shape=tuple(ex.out_shape), in_specs=[ANY] * n, out_specs=(ANY,) * n,
        scratch_shapes=ex.scratch,
    )(*ex.arrays)


def _forward_step(n_steps):
    return (5 * n_steps) // 8


def _hosted(ex, refs, n_in, n_out):
    ins, ex_src = refs[:n_in], refs[n_in:n_in + ex.n]
    rest = refs[n_in + ex.n:]
    outs, ex_dst = rest[:n_out], rest[n_out:n_out + ex.n]
    rest = rest[n_out + ex.n:]
    n_own = len(rest) - len(ex.scratch)
    return ins, outs, rest[:n_own], (ex_src, ex_dst, rest[n_own:])


def _fwd_mix(x, win_g, wout_g, sgu_g, sgu_b, wcat, bs_full, cw, cb, cg, cbeta, tm, ex):
    T = x.shape[0]
    nt = T // tm

    def body(*refs):
        ins, outs, scratch, ex_refs = _hosted(ex, refs, 11, 6)
        x_ref, win_ref, wout_ref, sg_ref, sb_ref, wcat_ref, bs_ref, cw_ref, cb_ref, cg_ref, cbeta_ref = ins
        proj_ref, ycat_ref, n1_ref, rstd1_ref, phi_ref, y_ref = outs
        hpad, hshift = scratch
        i = pl.program_id(0)

        @pl.when(i == 0)
        def _():
            ex.start(*ex_refs)

        xf = x_ref[...]
        xb = xf.astype(BF16)
        proj_ref[...] = _dot_nt(xb, win_ref[...])
        cdf = _normal_cdf(proj_ref[:, 0:2 * D_SGU])
        phi_ref[...] = cdf.astype(BF16)
        u = proj_ref[:, 0:D_SGU] * cdf[:, 0:D_SGU]
        v = proj_ref[:, D_SGU:2 * D_SGU] * cdf[:, D_SGU:2 * D_SGU]
        vhat, _ = _ln_fwd(v)
        vn = vhat * sg_ref[...] + sb_ref[...]
        lo = _lo_mask()
        for c in range(tm // CHUNK):
            rows = slice(CHUNK * c, CHUNK * (c + 1))
            for p in range(4):
                lanes = slice(CHUNK * p, CHUNK * (p + 1))
                mixed = _dot(wcat_ref[p], _head_pair_stack(vn[rows, lanes], lo)) + bs_ref[:, lanes]
                ycat_ref[rows, lanes] = (u[rows, lanes] * mixed).astype(BF16)
        base = 2 * D_SGU
        a = proj_ref[:, base:base + D_CONV]
        g = proj_ref[:, base + D_CONV:base + 2 * D_CONV]

        @pl.when(i == 0)
        def _():
            hpad[0:HALO, :] = jnp.zeros((HALO, D_CONV), F32)

        hpad[HALO:HALO + tm, :] = a * jax.nn.sigmoid(g)
        _shifted_copies(hpad, hshift, tm + SHIFT_ROWS)
        _causal_conv(hpad, hshift, cw_ref, y_ref, tm, lambda k: HALO - (CONV_WIDTH - 1) + k, bias=cb_ref[...])
        hpad[0:HALO, :] = hpad[tm:tm + HALO, :]
        yhat, _ = _ln_fwd(y_ref[...])
        yn = yhat * cg_ref[...] + cbeta_ref[...]
        ycat_ref[:, D_SGU:D_SGU + D_CONV] = (yn * jax.nn.sigmoid(yn)).astype(BF16)
        r1 = ALPHA * xf + _dot(ycat_ref[...], wout_ref[...])
        n1, rstd1 = _ln_fwd(r1)
        n1_ref[...] = n1
        rstd1_ref[...] = rstd1

        @pl.when(i == _forward_step(nt))
        def _():
            ex.forward(*ex_refs)

        @pl.when(i == nt - 1)
        def _():
            ex.wait(*ex_refs)

    S = jax.ShapeDtypeStruct
    row = lambda w: pl.BlockSpec((tm, w), lambda i: (i, 0))
    res = pl.pallas_call(
        body, name="fwd_mix", grid=(nt,),
        in_specs=[row(D_MODEL), _full(win_g.shape), _full(wout_g.shape), _full(sgu_g.shape), _full(sgu_b.shape),
                  _full(wcat.shape), _full(bs_full.shape), _full(cw.shape), _full(cb.shape), _full(cg.shape),
                  _full(cbeta.shape)] + [ANY] * ex.n,
        out_specs=(row(2 * D_MODEL), row(D_MODEL), row(D_MODEL), row(1), row(2 * D_SGU), row(D_CONV)) + (ANY,) * ex.n,
        out_shape=(S((T, 2 * D_MODEL), F32), S((T, D_MODEL), BF16), S((T, D_MODEL), F32), S((T, 1), F32),
                   S((T, 2 * D_SGU), BF16), S((T, D_CONV), F32), *ex.out_shape),
        scratch_shapes=[pltpu.VMEM((tm + HALO, D_CONV), F32),
                        pltpu.VMEM((SUBLANES - 1, tm + SHIFT_ROWS, D_CONV), F32)] + ex.scratch,
        compiler_params=_params(56, 1),
    )(x, win_g, wout_g, sgu_g, sgu_b, wcat, bs_full, cw, cb, cg, cbeta, *ex.arrays)
    return res[:6], res[6:]


def _load_resident(pairs, sems):
    cps = [pltpu.make_async_copy(s, d, sems.at[k]) for k, (s, d) in enumerate(pairs)]
    for cp in cps:
        cp.start()
    for cp in cps:
        cp.wait()


def _fwd_mlp(n1, tgt, l1g, l1b, l2g, l2b, wgt, wut, wd, tm):
    T = n1.shape[0]
    nt = T // tm
    nf = D_FF // MXU_COLS

    def body(n1_ref, tgt_ref, l1g_ref, l1b_ref, l2g_ref, l2b_ref, wg_hbm, wu_hbm, wd_hbm,
             gate_ref, up_ref, hh_ref, dr2_ref, stat_ref, wg_s, wu_s, wd_s, sems):
        i = pl.program_id(0)

        @pl.when(i == 0)
        def _():
            _load_resident([(wg_hbm, wg_s), (wu_hbm, wu_s), (wd_hbm, wd_s)], sems)
            stat_ref[...] = jnp.zeros(stat_ref.shape, F32)

        x1 = n1_ref[...] * l1g_ref[...] + l1b_ref[...]
        x1b = x1.astype(BF16)
        for f in range(nf):
            cols = slice(MXU_COLS * f, MXU_COLS * (f + 1))
            gt = _dot_nt(x1b, wg_s[cols, :])
            ut = _dot_nt(x1b, wu_s[cols, :])
            gate_ref[:, cols] = gt.astype(BF16)
            up_ref[:, cols] = ut.astype(BF16)
            hh_ref[:, cols] = (gt * jax.nn.sigmoid(gt) * ut).astype(BF16)
        r2 = ALPHA * x1 + _dot(hh_ref[...], wd_s[...])
        n2, rstd2 = _ln_fwd(r2)
        x2 = n2 * l2g_ref[...] + l2b_ref[...]
        diff = x2 - tgt_ref[...]
        dx2 = diff * (1.0 / D_MODEL)
        stat_ref[0:1, :] += _colsum(diff * diff)
        stat_ref[1:2, :] += _colsum(dx2 * n2)
        stat_ref[2:3, :] += _colsum(dx2)
        dr2_ref[...] = _ln_bwd(dx2 * l2g_ref[...], n2, rstd2)

    S = jax.ShapeDtypeStruct
    row = lambda w: pl.BlockSpec((tm, w), lambda i: (i, 0))
    vec = _full((1, D_MODEL))
    return pl.pallas_call(
        body, name="fwd_mlp", grid=(nt,),
        in_specs=[row(D_MODEL), row(D_MODEL), vec, vec, vec, vec, ANY, ANY, ANY],
        out_specs=(row(D_FF), row(D_FF), row(D_FF), row(D_MODEL), _full((8, D_MODEL))),
        out_shape=(S((T, D_FF), BF16), S((T, D_FF), BF16), S((T, D_FF), BF16), S((T, D_MODEL), F32),
                   S((8, D_MODEL), F32)),
        scratch_shapes=[pltpu.VMEM((D_FF, D_MODEL), BF16)] * 3 + [pltpu.SemaphoreType.DMA((3,))],
        compiler_params=_params(56, 1),
    )(n1, tgt, l1g, l1b, l2g, l2b, wgt, wut, wd)


def _bwd_mlp(dr2, gate, up, n1, rstd1, l1g, wgt, wut, wd, tm):
    T = n1.shape[0]
    nt = T // tm
    nf = D_FF // MXU_COLS

    def body(dr2_ref, gate_ref, up_ref, n1_ref, rstd1_ref, l1g_ref, wg_hbm, wu_hbm, wd_hbm,
             dgate_ref, dup_ref, dr1_ref, stat_ref, wg_s, wu_s, wd_s, sems):
        i = pl.program_id(0)

        @pl.when(i == 0)
        def _():
            _load_resident([(wg_hbm, wg_s), (wu_hbm, wu_s), (wd_hbm, wd_s)], sems)
            stat_ref[...] = jnp.zeros(stat_ref.shape, F32)

        dr2 = dr2_ref[...]
        dr2b = dr2.astype(BF16)
        for f in range(nf):
            cols = slice(MXU_COLS * f, MXU_COLS * (f + 1))
            dhh = _dot_nt(dr2b, wd_s[cols, :])
            gt = gate_ref[:, cols].astype(F32)
            ut = up_ref[:, cols].astype(F32)
            sg = jax.nn.sigmoid(gt)
            dgate_ref[:, cols] = (dhh * ut * (sg * (1.0 + gt * (1.0 - sg)))).astype(BF16)
            dup_ref[:, cols] = (dhh * (gt * sg)).astype(BF16)
        dx1 = ALPHA * dr2 + _dot(dgate_ref[...], wg_s[...]) + _dot(dup_ref[...], wu_s[...])
        n1 = n1_ref[...]
        stat_ref[0:1, :] += _colsum(dx1 * n1)
        stat_ref[1:2, :] += _colsum(dx1)
        dr1_ref[...] = _ln_bwd(dx1 * l1g_ref[...], n1, rstd1_ref[...])

    S = jax.ShapeDtypeStruct
    row = lambda w: pl.BlockSpec((tm, w), lambda i: (i, 0))
    return pl.pallas_call(
        body, name="bwd_mlp", grid=(nt,),
        in_specs=[row(D_MODEL), row(D_FF), row(D_FF), row(D_MODEL), row(1), _full((1, D_MODEL)), ANY, ANY, ANY],
        out_specs=(row(D_FF), row(D_FF), row(D_MODEL), _full((8, D_MODEL))),
        out_shape=(S((T, D_FF), BF16), S((T, D_FF), BF16), S((T, D_MODEL), F32), S((8, D_MODEL), F32)),
        scratch_shapes=[pltpu.VMEM((D_FF, D_MODEL), BF16)] * 3 + [pltpu.SemaphoreType.DMA((3,))],
        compiler_params=_params(56, 1),
    )(dr2, gate, up, n1, rstd1, l1g, wgt, wut, wd)


def _bwd_mix(dr1, proj, phi, y, win_g, wout_g, sgu_g, sgu_b, wcat, wcatt, bs_full, cw, cg, cbeta, tm, ex):
    T = dr1.shape[0]
    nt = T // tm
    halo_blocks = tm // HALO

    def body(*refs):
        ins, outs, scratch, ex_refs = _hosted(ex, refs, 15, 6)
        (dr1_ref, proj_ref, halo_ref, phi_ref, y_ref, win_ref, wout_ref, sg_ref, sb_ref, wcat_ref, wcatt_ref, bs_ref,
         cw_ref, cg_ref, cbeta_ref) = ins
        gx_ref, dproj_ref, gws_out, gbs_out, gcw_ref, vec_ref = outs
        hpad, shift, dypad, dhbuf, dubuf, dvnbuf, gcw_acc, gws_ref, gbs_ref = scratch
        i = pl.program_id(0)
        tile = nt - 1 - i

        @pl.when(i == 0)
        def _():
            ex.start(*ex_refs)
            gws_ref[...] = jnp.zeros(gws_ref.shape, F32)
            gbs_ref[...] = jnp.zeros(gbs_ref.shape, F32)
            gcw_ref[...] = jnp.zeros(gcw_ref.shape, F32)
            vec_ref[...] = jnp.zeros(vec_ref.shape, F32)
            gcw_acc[...] = jnp.zeros(gcw_acc.shape, F32)
            dypad[tm:tm + HALO, :] = jnp.zeros((HALO, D_CONV), F32)

        dr1 = dr1_ref[...]
        dycat = _dot_nt(dr1.astype(BF16), wout_ref[...])
        pu = proj_ref[:, 0:D_SGU]
        pv = proj_ref[:, D_SGU:2 * D_SGU]
        cdf_u = phi_ref[:, 0:D_SGU].astype(F32)
        cdf_v = phi_ref[:, D_SGU:2 * D_SGU].astype(F32)
        u = pu * cdf_u
        vhat, rstd_v = _ln_fwd(pv * cdf_v)
        vn = vhat * sg_ref[...] + sb_ref[...]
        lo = _lo_mask()
        for c in range(tm // CHUNK):
            rows = slice(CHUNK * c, CHUNK * (c + 1))
            for p in range(4):
                lanes = slice(CHUNK * p, CHUNK * (p + 1))
                vstack = _head_pair_stack(vn[rows, lanes], lo)
                mixed = _dot(wcat_ref[p], vstack) + bs_ref[:, lanes]
                d_a = dycat[rows, lanes]
                dubuf[rows, lanes] = d_a * mixed
                dm = d_a * u[rows, lanes]
                gbs_ref[:, lanes] += dm
                dstack = _head_pair_stack(dm, lo)
                gws_ref[2 * CHUNK * p:2 * CHUNK * (p + 1), :] += _dot_nt(dstack, vn[rows, lanes].astype(BF16))
                dvnbuf[rows, lanes] = _dot(wcatt_ref[p], dstack)
        dvn = dvnbuf[...]
        vec_ref[0:1, :] += _colsum(dvn * vhat)
        vec_ref[1:2, :] += _colsum(dvn)
        dv = _ln_bwd(dvn * sg_ref[...], vhat, rstd_v)
        dproj_ref[:, 0:D_SGU] = (dubuf[...] * _gelu_grad(pu, cdf_u)).astype(BF16)
        dproj_ref[:, D_SGU:2 * D_SGU] = (dv * _gelu_grad(pv, cdf_v)).astype(BF16)
        base = 2 * D_SGU
        a = proj_ref[:, base:base + D_CONV]
        sgm = jax.nn.sigmoid(proj_ref[:, base + D_CONV:base + 2 * D_CONV])
        h_before = halo_ref[:, 0:D_CONV] * jax.nn.sigmoid(halo_ref[:, D_CONV:2 * D_CONV])
        hpad[0:HALO, :] = jnp.where(tile > 0, h_before, 0.0)
        hpad[HALO:HALO + tm, :] = a * sgm
        _shifted_copies(hpad, shift, tm + SHIFT_ROWS)
        h_offset = lambda k: HALO - (CONV_WIDTH - 1) + k
        yhat, rstd_y = _ln_fwd(y_ref[...])
        yn = yhat * cg_ref[...] + cbeta_ref[...]
        s = jax.nn.sigmoid(yn)
        dyn = dycat[:, D_SGU:D_SGU + D_CONV] * (s * (1.0 + yn * (1.0 - s)))
        vec_ref[3:4, :] += _colsum(dyn * yhat)
        vec_ref[4:5, :] += _colsum(dyn)
        dy = _ln_bwd(dyn * cg_ref[...], yhat, rstd_y)
        vec_ref[2:3, :] += _colsum(dy)
        dypad[0:tm, :] = dy
        _conv_weight_grad(dypad, hpad, shift, gcw_acc, tm, h_offset)
        _shifted_copies(dypad, shift, tm + SHIFT_ROWS)
        _causal_conv(dypad, shift, cw_ref, dhbuf, tm, lambda k: (CONV_WIDTH - 1) - k)
        dypad[tm:tm + HALO, :] = dypad[0:HALO, :]
        dh = dhbuf[...]
        dproj_ref[:, base:base + D_CONV] = (dh * sgm).astype(BF16)
        dproj_ref[:, base + D_CONV:base + 2 * D_CONV] = (dh * a * sgm * (1.0 - sgm)).astype(BF16)
        gx_ref[...] = ALPHA * dr1 + _dot(dproj_ref[...], win_ref[...])

        @pl.when(i == _forward_step(nt))
        def _():
            ex.forward(*ex_refs)

        @pl.when(i == nt - 1)
        def _():
            gcw_ref[...] = gcw_acc[...].sum(axis=1)
            gws_out[...] = gws_ref[...].astype(BF16)
            gbs_out[...] = lax.dot_general(_head_selector(), gbs_ref[...], (((1,), (1,)), ((), ())),
                                           preferred_element_type=F32, precision=lax.Precision.HIGHEST)
            ex.wait(*ex_refs)

    S = jax.ShapeDtypeStruct
    row = lambda w: pl.BlockSpec((tm, w), lambda i: (nt - 1 - i, 0))
    halo = pl.BlockSpec((HALO, D_MODEL), lambda i: (jnp.maximum((nt - 1 - i) * halo_blocks - 1, 0), 1))
    res = pl.pallas_call(
        body, name="bwd_mix", grid=(nt,),
        in_specs=[row(D_MODEL), row(2 * D_MODEL), halo, row(2 * D_SGU), row(D_CONV), _full(win_g.shape),
                  _full(wout_g.shape), _full(sgu_g.shape), _full(sgu_b.shape), _full(wcat.shape), _full(wcatt.shape),
                  _full(bs_full.shape), _full(cw.shape), _full(cg.shape), _full(cbeta.shape)]
        + [ANY] * ex.n,
        out_specs=(row(D_MODEL), row(2 * D_MODEL), _full((N_HEADS * CHUNK, CHUNK)), _full((N_HEADS, CHUNK)),
                   _full((CONV_ROWS, D_CONV)), _full((8, D_CONV))) + (ANY,) * ex.n,
        out_shape=(S((T, D_MODEL), F32), S((T, 2 * D_MODEL), BF16), S((N_HEADS * CHUNK, CHUNK), BF16),
                   S((N_HEADS, CHUNK), F32), S((CONV_ROWS, D_CONV), F32), S((8, D_CONV), F32), *ex.out_shape),
        scratch_shapes=[pltpu.VMEM((tm + HALO, D_CONV), F32), pltpu.VMEM((SUBLANES - 1, tm + SHIFT_ROWS, D_CONV), F32),
                        pltpu.VMEM((tm + HALO, D_CONV), F32),
                        pltpu.VMEM((tm, D_CONV), F32), pltpu.VMEM((tm, D_SGU), F32),
                        pltpu.VMEM((tm, D_SGU), F32), pltpu.VMEM((CONV_ROWS, 8, D_CONV), F32),
                        pltpu.VMEM((N_HEADS * CHUNK, CHUNK), F32), pltpu.VMEM((CHUNK, D_SGU), F32)] + ex.scratch,
        compiler_params=_params(56, 1),
    )(dr1, proj, proj, phi, y, win_g, wout_g, sgu_g, sgu_b, wcat, wcatt, bs_full, cw, cg, cbeta, *ex.arrays)
    return res[:6], res[6:]


def _wgrad(name, a, b, blocks, tk, ex=None, b_cols=None, b_affine=None):
    T, M = a.shape
    col, N = b_cols or (0, b.shape[1])
    nk = T // tk
    out_shape = (blocks, M // blocks, N)
    ex = ex or _Exchange([], [])
    affine = list(b_affine or [])

    def body(*refs):
        ins, (o_ref,), (acc,), ex_refs = _hosted(ex, refs, 2 + len(affine), 1)
        a_ref, b_ref = ins[:2]
        i = pl.program_id(0)

        @pl.when(i == 0)
        def _():
            ex.start(*ex_refs)
            acc[...] = jnp.zeros(acc.shape, F32)

        right = b_ref[...]
        if affine:
            right = right * ins[2][...] + ins[3][...]
        acc[...] += _dot_tn(a_ref[...].astype(BF16), right.astype(BF16))

        @pl.when(i == _forward_step(nk))
        def _():
            ex.forward(*ex_refs)

        @pl.when(i == nk - 1)
        def _():
            o_ref[...] = acc[...].astype(BF16)
            ex.wait(*ex_refs)

    res = pl.pallas_call(
        body, name=name, grid=(nk,),
        in_specs=[pl.BlockSpec((tk, M), lambda i: (i, 0)), pl.BlockSpec((tk, N), lambda i: (i, col))]
        + [_full((1, N))] * len(affine) + [ANY] * ex.n,
        out_specs=(_full((M, N)),) + (ANY,) * ex.n,
        out_shape=(jax.ShapeDtypeStruct((M, N), BF16), *ex.out_shape),
        scratch_shapes=[pltpu.VMEM((M, N), F32)] + ex.scratch,
        compiler_params=_params(56, 1),
    )(a, b, *affine, *ex.arrays)
    g = res[0].reshape(out_shape)
    return (g, res[1:]) if ex.n else g


def _adamw(w, g, m, v):
    m2 = ADAM_B1 * m + (1.0 - ADAM_B1) * g
    v2 = ADAM_B2 * v + (1.0 - ADAM_B2) * (g * g)
    m_hat = m2 / (1.0 - ADAM_B1 ** ADAM_STEP)
    v_hat = v2 / (1.0 - ADAM_B2 ** ADAM_STEP)
    delta = -ADAM_LR * (m_hat / (jnp.sqrt(v_hat) + ADAM_EPS) + ADAM_WD * w)
    return delta, m2, v2


def _sum_partials(r_ref):
    g = r_ref[0].astype(F32)
    for s in range(1, N_DEV):
        g = g + r_ref[s].astype(F32)
    return g


def _staged_call(name, groups, vmem_mib, ex=None):
    ex = ex or _Exchange([], [])
    inputs = [a for ins, _, _ in groups for a in ins]
    out_shapes = [s for _, outs, _ in groups for s in outs]
    n_in, n_out = len(inputs), len(out_shapes)

    def body(*refs):
        ins, outs, scratch, ex_refs = _hosted(ex, refs, n_in, n_out)
        in_bufs, out_bufs, sems = scratch[:n_in], scratch[n_in:n_in + n_out], scratch[n_in + n_out]
        ex.start(*ex_refs)
        loads = [pltpu.make_async_copy(ins[k], in_bufs[k], sems.at[k]) for k in range(n_in)]
        stores = [pltpu.make_async_copy(out_bufs[k], outs[k], sems.at[n_in + k]) for k in range(n_out)]
        for cp in loads:
            cp.start()
        i0 = o0 = 0
        for g_ins, g_outs, compute in groups:
            i1, o1 = i0 + len(g_ins), o0 + len(g_outs)
            for cp in loads[i0:i1]:
                cp.wait()
            compute(in_bufs[i0:i1], out_bufs[o0:o1])
            for cp in stores[o0:o1]:
                cp.start()
            i0, o0 = i1, o1
        for cp in stores:
            cp.wait()
        ex.forward(*ex_refs)
        ex.wait(*ex_refs)

    scratch = ([pltpu.VMEM(a.shape, a.dtype) for a in inputs] + [pltpu.VMEM(s.shape, s.dtype) for s in out_shapes]
               + [pltpu.SemaphoreType.DMA((n_in + n_out,))] + ex.scratch)
    res = pl.pallas_call(
        body, name=name, out_shape=(*[pltpu.HBM(s.shape, s.dtype) for s in out_shapes], *ex.out_shape),
        in_specs=[HBM] * n_in + [ANY] * ex.n, out_specs=(HBM,) * n_out + (ANY,) * ex.n,
        scratch_shapes=scratch, compiler_params=_params(vmem_mib),
    )(*[pltpu.with_memory_space_constraint(a, pltpu.HBM) for a in inputs], *ex.arrays)
    per_group, o0 = [], 0
    for _, g_outs, _ in groups:
        per_group.append(list(res[o0:o0 + len(g_outs)]))
        o0 += len(g_outs)
    return per_group, res[n_out:]


def _adamw_shard_group(parts, w, m, v, transposed):
    n = len(parts)

    def compute(ins, outs):
        w_ref, m_ref, v_ref = ins[n:]
        lo = 0
        for r_ref in ins[:n]:
            g = _sum_partials(r_ref)
            cols = g.shape[1]
            if transposed:
                g, at = g.T, (slice(lo, lo + cols), slice(None))
            else:
                at = (slice(None), slice(lo, lo + cols))
            delta, m2, v2 = _adamw(w_ref[at], g, m_ref[at], v_ref[at])
            for o, val in zip(outs, (g, delta, m2, v2)):
                o[at] = val
            lo += cols

    return [*parts, w, m, v], [jax.ShapeDtypeStruct(w.shape, F32)] * 4, compute


SMALL_NAMES = ["sgu_ln_g", "sgu_ln_b", "w_s", "b_s", "conv_b", "conv_ln_g", "conv_ln_b", "ln1_g", "ln1_b", "ln2_g", "ln2_b",
               "conv_w"]


def _finish_small_group(gws8, gbs8, gcw8, vmix8, vmlp8, vout8, small):
    names = SMALL_NAMES
    flat = []
    for n in names:
        flat += list(small[n])
    cw_block = D_CONV // N_DEV

    def compute(ins, outs):
        gws_ref, gbs_ref, gcw_ref, vmix_ref, vmlp_ref, vout_ref = ins[:6]
        wmv = ins[6:]
        loss_o = outs[0]
        outs = outs[1:]
        gws = _sum_partials(gws_ref)
        gbs = _sum_partials(gbs_ref)
        vmix = _sum_partials(vmix_ref)
        vmlp = _sum_partials(vmlp_ref)
        vout = _sum_partials(vout_ref)
        x, y, c = _mesh_position()
        first = (4 * x + 2 * y + c) * cw_block
        pick = (lax.broadcasted_iota(jnp.int32, (D_CONV, cw_block), 0)
                == first + lax.broadcasted_iota(jnp.int32, (D_CONV, cw_block), 1)).astype(F32)
        gcw = jnp.dot(_sum_partials(gcw_ref), pick, preferred_element_type=F32,
                      precision=lax.Precision.HIGHEST)[0:CONV_WIDTH, :]
        loss = (0.5 / D_MODEL) * jnp.sum(vout[0:1, :], axis=1, keepdims=True)
        loss_o[...] = jnp.broadcast_to(loss, loss_o.shape)
        rows = lax.broadcasted_iota(jnp.int32, (N_HEADS * CHUNK, CHUNK), 0)
        cols = lax.broadcasted_iota(jnp.int32, (N_HEADS * CHUNK, CHUNK), 1)
        gws = jnp.where((rows & (CHUNK - 1)) >= cols, gws, 0.0)
        grads = {
            "sgu_ln_g": vmix[0:1, :], "sgu_ln_b": vmix[1:2, :], "w_s": gws, "b_s": gbs,
            "conv_b": vmix[2:3, :], "conv_ln_g": vmix[3:4, :], "conv_ln_b": vmix[4:5, :],
            "ln1_g": vmlp[0:1, :], "ln1_b": vmlp[1:2, :], "ln2_g": vout[1:2, :], "ln2_b": vout[2:3, :],
            "conv_w": gcw,
        }
        for k, n in enumerate(names):
            w_ref, m_ref, v_ref = wmv[3 * k:3 * k + 3]
            g = grads[n]
            delta, m2, v2 = _adamw(w_ref[...], g, m_ref[...], v_ref[...])
            outs[4 * k][...] = g
            outs[4 * k + 1][...] = delta
            outs[4 * k + 2][...] = m2
            outs[4 * k + 3][...] = v2

    S = jax.ShapeDtypeStruct
    out_shape = [S((SUBLANES, 128), F32)]
    for n in names:
        out_shape += [S(small[n][0].shape, F32)] * 4
    return [gws8, gbs8, gcw8, vmix8, vmlp8, vout8, *flat], out_shape, compute


TOKEN_TILE_FWD_MIX = 512
TOKEN_TILE_BWD_MIX = 512
TOKEN_TILE_FWD_MLP = 512
TOKEN_TILE_BWD_MLP = 512
TOKEN_TILE_WGRAD = 1024


def kernel(x, w_in, sgu_ln_g, sgu_ln_b, w_s, b_s, conv_w, conv_b, conv_ln_g, conv_ln_b, w_out, ln1_g, ln1_b, w_gate, w_up, w_down, ln2_g, ln2_b, loss_target, m_w_in, m_sgu_ln_g, m_sgu_ln_b, m_w_s, m_b_s, m_conv_w, m_conv_b, m_conv_ln_g, m_conv_ln_b, m_w_out, m_ln1_g, m_ln1_b, m_w_gate, m_w_up, m_w_down, m_ln2_g, m_ln2_b, v_w_in, v_sgu_ln_g, v_sgu_ln_b, v_w_s, v_b_s, v_conv_w, v_conv_b, v_conv_ln_g, v_conv_ln_b, v_w_out, v_ln1_g, v_ln1_b, v_w_gate, v_w_up, v_w_down, v_ln2_g, v_ln2_b):
    xs = x[0]
    tgt = loss_target[0]

    (win_b, wout_b, wgt_b, wut_b, wd_b, cw_b, wcat, wcatt, bs_full) = _prep_weights(
        w_in[0], w_out[0], w_gate[0].T, w_up[0].T, w_down[0], conv_w[0], w_s[0], b_s[0])
    win_g, wout_g, cw_g = _exchange("gather_mix_weights", [], [win_b, wout_b, cw_b])
    win_g = win_g.reshape(2 * D_MODEL, D_MODEL)
    wout_g = wout_g.reshape(D_MODEL, D_MODEL)
    cw = jnp.transpose(cw_g[:, :, :D_CONV // N_DEV], (1, 0, 2)).reshape(CONV_ROWS, D_CONV)

    (proj, ycat, n1, rstd1, phi, y_conv), (wgt_g, wut_g, wd_g) = _fwd_mix(
        xs, win_g, wout_g, sgu_ln_g, sgu_ln_b, wcat, bs_full, cw, conv_b, conv_ln_g, conv_ln_b, TOKEN_TILE_FWD_MIX,
        _Exchange([], [wgt_b, wut_b, wd_b]))
    wgt_g = wgt_g.reshape(D_FF, D_MODEL)
    wut_g = wut_g.reshape(D_FF, D_MODEL)
    wd_g = wd_g.reshape(D_FF, D_MODEL)
    gate, up, hh, dr2, vout = _fwd_mlp(n1, tgt, ln1_g, ln1_b, ln2_g, ln2_b, wgt_g, wut_g, wd_g, TOKEN_TILE_FWD_MLP)

    dgate, dup, dr1, vmlp = _bwd_mlp(dr2, gate, up, n1, rstd1, ln1_g, wgt_g, wut_g, wd_g, TOKEN_TILE_BWD_MLP)
    tk = TOKEN_TILE_WGRAD
    x1 = dict(b_affine=(ln1_g, ln1_b))
    g_wgt = _wgrad("wgrad_gate", dgate, n1, N_DEV, tk, **x1)
    g_wut = _wgrad("wgrad_up", dup, n1, N_DEV, tk, **x1)
    g_wd = _wgrad("wgrad_down", hh, dr2, N_DEV, tk)
    g_wout = _wgrad("wgrad_out", ycat, dr1, N_DEV, tk)
    (gx, dproj, gws, gbs, gcw, vmix), (r_wgt, r_wut, r_wd, r_wout) = _bwd_mix(
        dr1, proj, phi, y_conv, win_g, wout_g, sgu_ln_g, sgu_ln_b, wcat, wcatt, bs_full, cw, conv_ln_g, conv_ln_b,
        TOKEN_TILE_BWD_MIX, _Exchange([g_wgt, g_wut, g_wd, g_wout], []))
    half = D_MODEL // 2
    g_win_a, (gws8, gbs8, gcw8, vmix8, vmlp8, vout8) = _wgrad(
        "wgrad_in_a", dproj, xs, N_DEV, tk, _Exchange([], [gws, gbs, gcw, vmix, vmlp, vout]), b_cols=(0, half))
    g_win_b, (r_win_a,) = _wgrad("wgrad_in_b", dproj, xs, N_DEV, tk, _Exchange([g_win_a], []), b_cols=(1, half))
    (r_win_b,) = _exchange("exchange_grad_in", [g_win_b], [])
    (u_gate, u_up, u_down), _ = _staged_call(
        "adamw_mlp",
        [_adamw_shard_group([r_wgt], w_gate[0].T, m_w_gate[0].T, v_w_gate[0].T, False),
         _adamw_shard_group([r_wut], w_up[0].T, m_w_up[0].T, v_w_up[0].T, False),
         _adamw_shard_group([r_wd], w_down[0], m_w_down[0], v_w_down[0], False)], 56)
    small_in = {
        "conv_w": (conv_w[0], m_conv_w[0], v_conv_w[0]),
        "sgu_ln_g": (sgu_ln_g, m_sgu_ln_g, v_sgu_ln_g), "sgu_ln_b": (sgu_ln_b, m_sgu_ln_b, v_sgu_ln_b),
        "w_s": tuple(a.reshape(N_HEADS * CHUNK, CHUNK) for a in (w_s, m_w_s, v_w_s)),
        "b_s": (b_s[0], m_b_s[0], v_b_s[0]),
        "conv_b": (conv_b, m_conv_b, v_conv_b), "conv_ln_g": (conv_ln_g, m_conv_ln_g, v_conv_ln_g),
        "conv_ln_b": (conv_ln_b, m_conv_ln_b, v_conv_ln_b),
        "ln1_g": (ln1_g, m_ln1_g, v_ln1_g), "ln1_b": (ln1_b, m_ln1_b, v_ln1_b),
        "ln2_g": (ln2_g, m_ln2_g, v_ln2_g), "ln2_b": (ln2_b, m_ln2_b, v_ln2_b),
    }
    (u_in, u_out, fin), _ = _staged_call(
        "adamw_mix_small",
        [_adamw_shard_group([r_win_a, r_win_b], w_in[0], m_w_in[0], v_w_in[0], True),
         _adamw_shard_group([r_wout], w_out[0], m_w_out[0], v_w_out[0], False),
         _finish_small_group(gws8, gbs8, gcw8, vmix8, vmlp8, vout8, small_in)], 40)
    big = {"w_in": u_in, "w_out": u_out, "w_gate": u_gate, "w_up": u_up, "w_down": u_down}
    loss11 = fin[0]
    small = {n: fin[1 + 4 * k:5 + 4 * k] for k, n in enumerate(SMALL_NAMES)}

    shapes = {"w_s": w_s.shape, "b_s": b_s.shape, "conv_w": conv_w.shape}
    out = {}
    for n, r in big.items():
        out[n] = tuple((a.T if n in ("w_gate", "w_up") else a)[None] for a in r)
    for n, r in small.items():
        out[n] = tuple(a.reshape(shapes[n]) for a in r) if n in shapes else tuple(r)

    order = ["w_in", "sgu_ln_g", "sgu_ln_b", "w_s", "b_s", "conv_w", "conv_b", "conv_ln_g", "conv_ln_b", "w_out",
             "ln1_g", "ln1_b", "w_gate", "w_up", "w_down", "ln2_g", "ln2_b"]
    loss = loss11[0, 0]
    return (loss, gx[None], *[out[n][0] for n in order], *[out[n][1] for n in order],
            *[out[n][2] for n in order], *[out[n][3] for n in order])
```

```python
import jax
import jax.numpy as jnp
from jax import lax
from jax.experimental import pallas as pl
from jax.experimental.pallas import tpu as pltpu

F32 = jnp.float32
BF16 = jnp.bfloat16

D_MODEL = 1024
D_SGU = 512
D_CONV = 512
N_HEADS = 8
CHUNK = 128
CONV_WIDTH = 31
CONV_ROWS = 32
HALO = 32
D_FF = 2816
N_DEV = 8
FF_SHARD = D_FF // N_DEV
ALPHA = (2.0 * 1) ** 0.25
LN_EPS = 1e-5
INV_SQRT2 = 0.7071067811865476
INV_SQRT_2PI = 0.3989422804014327

ADAM_LR = 0.001
ADAM_B1 = 0.9
ADAM_B2 = 0.999
ADAM_EPS = 1e-08
ADAM_WD = 0.01
ADAM_STEP = 10

MXU_COLS = 256
SUBLANES = 8
CONV_ROW_BLOCK = 32
WGRAD_ROW_BLOCK = 32
SHIFT_ROWS = HALO - SUBLANES
MIB = 1024 * 1024

HBM = pl.BlockSpec(memory_space=pltpu.HBM)
ANY = pl.BlockSpec(memory_space=pl.ANY)
MESH = pl.DeviceIdType.MESH


def _params(vmem_mib, grid_dims=0):
    kw = dict(vmem_limit_bytes=vmem_mib * MIB)
    if grid_dims:
        kw["dimension_semantics"] = ("arbitrary",) * grid_dims
    return pltpu.CompilerParams(**kw)


def _full(shape):
    return pl.BlockSpec(shape, lambda i: (0,) * len(shape))


def _dot(a, b):
    return jnp.dot(a, b, preferred_element_type=F32)


def _dot_nt(a, b):
    return lax.dot_general(a, b, (((1,), (1,)), ((), ())), preferred_element_type=F32)


def _dot_tn(a, b):
    return lax.dot_general(a, b, (((0,), (0,)), ((), ())), preferred_element_type=F32)


def _normal_cdf(x):
    return 0.5 * (1.0 + lax.erf(x * INV_SQRT2))


def _gelu_grad(x, cdf):
    return cdf + x * jnp.exp(-0.5 * x * x) * INV_SQRT_2PI


def _ln_fwd(v):
    mu = jnp.mean(v, axis=-1, keepdims=True)
    d = v - mu
    var = jnp.mean(d * d, axis=-1, keepdims=True)
    rstd = lax.rsqrt(var + LN_EPS)
    return d * rstd, rstd


def _ln_bwd(dyhat, yhat, rstd):
    m1 = jnp.mean(dyhat, axis=-1, keepdims=True)
    m2 = jnp.mean(dyhat * yhat, axis=-1, keepdims=True)
    return rstd * (dyhat - m1 - yhat * m2)


def _colsum(v):
    return jnp.sum(v, axis=0, keepdims=True)


def _head_pair_stack(v, lo):
    return jnp.concatenate([jnp.where(lo, v, 0.0), jnp.where(lo, 0.0, v)], axis=0).astype(BF16)


def _lo_mask():
    return lax.broadcasted_iota(jnp.int32, (CHUNK, CHUNK), 1) < (CHUNK // 2)


def _head_selector():
    head = lax.broadcasted_iota(jnp.int32, (N_HEADS, D_SGU), 0)
    lane = lax.broadcasted_iota(jnp.int32, (N_HEADS, D_SGU), 1)
    width = D_SGU // N_HEADS
    return ((lane >= head * width) & (lane < (head + 1) * width)).astype(F32)


def _shifted_copies(pad_ref, sh_ref, rows):
    for r in range(1, SUBLANES):
        sh_ref[r - 1, 0:rows, :] = pad_ref[pl.ds(r, rows), :]


def _tap_groups(offset_of_tap):
    groups = {}
    for k in range(CONV_WIDTH):
        o = offset_of_tap(k)
        groups.setdefault(o % SUBLANES, []).append((k, o // SUBLANES))
    return groups


def _tap_window(pad_ref, sh_ref, r, taps, row0, rows):
    q0 = min(q for _, q in taps)
    q1 = max(q for _, q in taps)
    src = pad_ref if r == 0 else sh_ref.at[r - 1]
    win = src[pl.ds(row0 + SUBLANES * q0, SUBLANES * (q1 - q0) + rows), :]
    return win, [(k, SUBLANES * (q - q0)) for k, q in taps]


def _causal_conv(pad_ref, sh_ref, w_ref, out_ref, rows, offset_of_tap, bias=None):
    groups = _tap_groups(offset_of_tap)

    def block(b, carry):
        row0 = pl.multiple_of(b * CONV_ROW_BLOCK, CONV_ROW_BLOCK)
        if bias is None:
            acc = jnp.zeros((CONV_ROW_BLOCK, D_CONV), F32)
        else:
            acc = jnp.broadcast_to(bias, (CONV_ROW_BLOCK, D_CONV))
        for r, taps in groups.items():
            win, starts = _tap_window(pad_ref, sh_ref, r, taps, row0, CONV_ROW_BLOCK)
            for k, s in starts:
                acc = acc + w_ref[k:k + 1, :] * win[s:s + CONV_ROW_BLOCK, :]
        out_ref[pl.ds(row0, CONV_ROW_BLOCK), :] = acc
        return carry

    lax.fori_loop(0, rows // CONV_ROW_BLOCK, block, 0)


def _conv_weight_grad(dy_ref, pad_ref, sh_ref, acc_ref, rows, offset_of_tap):
    groups = _tap_groups(offset_of_tap)
    for r, taps in groups.items():

        def block(b, parts, r=r, taps=taps):
            row0 = pl.multiple_of(b * WGRAD_ROW_BLOCK, WGRAD_ROW_BLOCK)
            dyb = dy_ref[pl.ds(row0, WGRAD_ROW_BLOCK), :]
            win, starts = _tap_window(pad_ref, sh_ref, r, taps, row0, WGRAD_ROW_BLOCK)
            out = []
            for part, (_, s) in zip(parts, starts):
                pr = dyb * win[s:s + WGRAD_ROW_BLOCK, :]
                out.append(part + pr.reshape(WGRAD_ROW_BLOCK // SUBLANES, SUBLANES, D_CONV).sum(axis=0))
            return tuple(out)

        zeros = tuple(jnp.zeros((SUBLANES, D_CONV), F32) for _ in taps)
        parts = lax.fori_loop(0, rows // WGRAD_ROW_BLOCK, block, zeros)
        for part, (k, _) in zip(parts, taps):
            acc_ref[k] += part


def _prep_weights(w_in, w_out, w_gate_t, w_up_t, w_down, conv_w, w_s, b_s):
    def compute(ins, outs):
        win_ref, wout_ref, wgt_ref, wut_ref, wd_ref, cw_ref, ws_ref, bs_ref = ins
        win_o, wout_o, wgt_o, wut_o, wd_o, cw_o, wcat_o, wcatt_o, bsf_o = outs
        win_o[...] = win_ref[...].T.astype(BF16)
        wout_o[...] = wout_ref[...].astype(BF16)
        wgt_o[...] = wgt_ref[...].astype(BF16)
        wut_o[...] = wut_ref[...].astype(BF16)
        wd_o[...] = wd_ref[...].astype(BF16)
        cw_o[...] = jnp.zeros(cw_o.shape, F32)
        cw_o[0:CONV_WIDTH, 0:D_CONV // N_DEV] = cw_ref[...]
        row = lax.broadcasted_iota(jnp.int32, (CHUNK, CHUNK), 0)
        col = lax.broadcasted_iota(jnp.int32, (CHUNK, CHUNK), 1)
        causal = row >= col
        for h in range(N_HEADS):
            w = jnp.where(causal, ws_ref[h], 0.0)
            p, half = h // 2, (h % 2) * CHUNK
            wcat_o[p, :, half:half + CHUNK] = w.astype(BF16)
            wcatt_o[p, :, half:half + CHUNK] = w.T.astype(BF16)
        bsf_o[...] = lax.dot_general(bs_ref[...], _head_selector(), (((0,), (0,)), ((), ())),
                                     preferred_element_type=F32, precision=lax.Precision.HIGHEST)

    S = jax.ShapeDtypeStruct
    out_shapes = [S((256, D_MODEL), BF16), S((128, D_MODEL), BF16), S((FF_SHARD, D_MODEL), BF16),
                  S((FF_SHARD, D_MODEL), BF16), S((FF_SHARD, D_MODEL), BF16), S((CONV_ROWS, 128), F32),
                  S((4, CHUNK, 2 * CHUNK), BF16), S((4, CHUNK, 2 * CHUNK), BF16), S((CHUNK, D_SGU), F32)]
    (res,), _ = _staged_call(
        "prep_weights", [([w_in, w_out, w_gate_t, w_up_t, w_down, conv_w, w_s, b_s], out_shapes, compute)], 32)
    return res


def _mesh_position():
    x, y, c = lax.axis_index("x"), lax.axis_index("y"), lax.axis_index("c")
    return x, y, c


def _peers(x, y, c):
    out = []
    for k in range(1, N_DEV):
        px = 1 - x if (k >> 2) & 1 else x
        py = 1 - y if (k >> 1) & 1 else y
        pc = 1 - c if k & 1 else c
        out.append(((px, py, pc), 4 * px + 2 * py + pc))
    return out


class _Exchange:
    def __init__(self, scatter, gather):
        self.arrays = list(scatter) + list(gather)
        self.n_sc = len(scatter)
        self.n = len(self.arrays)
        self.out_shape = [jax.ShapeDtypeStruct(a.shape if k < self.n_sc else (N_DEV,) + a.shape, a.dtype)
                          for k, a in enumerate(self.arrays)]
        n_remote = self.n * (N_DEV - 1)
        self.scratch = [pltpu.SemaphoreType.DMA((n_remote,)), pltpu.SemaphoreType.DMA((n_remote,)),
                        pltpu.SemaphoreType.DMA((self.n,))] if self.n else []

    def _copies(self, src, dst, sems):
        send_sems, recv_sems, local_sems = sems
        x, y, c = _mesh_position()
        me = 4 * x + 2 * y + c
        locals_, first, arrivals, passed, last = [], [], [], [], []

        def remote(a, k, src_ref, slot, to):
            s = a * (N_DEV - 1) + k
            return pltpu.make_async_remote_copy(src_ref=src_ref, dst_ref=dst[a].at[slot], send_sem=send_sems.at[s],
                                                recv_sem=recv_sems.at[s], device_id=to, device_id_type=MESH)

        for a in range(self.n):
            if a < self.n_sc:
                locals_.append(pltpu.make_async_copy(src[a].at[me], dst[a].at[me], local_sems.at[a]))
                for k, (peer, pid) in enumerate(_peers(x, y, c)):
                    first.append(remote(a, k, src[a].at[pid], me, peer))
                    last.append(remote(a, k, src[a].at[pid], pid, peer))
                continue
            locals_.append(pltpu.make_async_copy(src[a], dst[a].at[me], local_sems.at[a]))
            sibling, sib_id = (x, y, 1 - c), 4 * x + 2 * y + (1 - c)
            chips = [(1 - x, y), (x, 1 - y), (1 - x, 1 - y)]
            first.append(remote(a, 0, src[a], me, sibling))
            last.append(remote(a, 0, src[a], sib_id, sibling))
            for j, (px, py) in enumerate(chips):
                same, other = 4 * px + 2 * py + c, 4 * px + 2 * py + (1 - c)
                first.append(remote(a, 1 + j, src[a], me, (px, py, c)))
                arrivals.append(remote(a, 1 + j, src[a], same, (px, py, c)))
                passed.append(remote(a, 4 + j, dst[a].at[same], same, sibling))
                last.append(remote(a, 4 + j, dst[a].at[other], other, sibling))
        return locals_, first, arrivals, passed, last

    def start(self, src, dst, sems):
        if not self.n:
            return
        locals_, first, _, _, _ = self._copies(src, dst, sems)
        for cp in locals_ + first:
            cp.start()

    def forward(self, src, dst, sems):
        if self.n == self.n_sc:
            return
        _, _, arrivals, passed, _ = self._copies(src, dst, sems)
        for arrived, cp in zip(arrivals, passed):
            arrived.wait_recv()
            cp.start()

    def wait(self, src, dst, sems):
        if not self.n:
            return
        locals_, first, _, passed, last = self._copies(src, dst, sems)
        for cp in last:
            cp.wait_recv()
        for cp in first + passed:
            cp.wait_send()
        for cp in locals_:
            cp.wait()


def _exchange(name, scatter, gather):
    ex = _Exchange(scatter, gather)
    n = ex.n

    def body(*refs):
        src, dst, sems = refs[:n], refs[n:2 * n], refs[2 * n:]
        ex.start(src, dst, sems)
        ex.forward(src, dst, sems)
        ex.wait(src, dst, sems)

    return pl.pallas_call(
        body, name=name, out_shape=tuple(ex.out_shape), in_specs=[ANY] * n, out_specs=(ANY,) * n,
        scratch_shapes=ex.scratch,
    )(*ex.arrays)


PROJ_ROWS = 2048


def _gather_proj(x, win_b, wout_b, cw_b):
    T = x.shape[0]
    chunk = min(PROJ_ROWS, T)
    nq = T // chunk
    ex = _Exchange([], [win_b, wout_b, cw_b])

    def body(x_hbm, win_hbm, wout_hbm, cw_hbm, proj_hbm, win_g, wout_g, cw_g,
             xb, xstage, w_s, out_s, xsem, wsem, osem, send_sems, recv_sems, local_sems):
        locals_, first, arrivals, passed, last = ex._copies(
            (win_hbm, wout_hbm, cw_hbm), (win_g, wout_g, cw_g), (send_sems, recv_sems, local_sems))
        for cp in locals_ + first:
            cp.start()
        x_, y_, c_ = _mesh_position()
        chips = [(1 - x_, y_), (x_, 1 - y_), (1 - x_, 1 - y_)]
        order = ([4 * x_ + 2 * y_ + c_, 4 * x_ + 2 * y_ + (1 - c_)] + [4 * px + 2 * py + c_ for px, py in chips]
                 + [4 * px + 2 * py + (1 - c_) for px, py in chips])

        def x_copy(q, slot):
            return pltpu.make_async_copy(x_hbm.at[pl.ds(q * chunk, chunk)], xstage.at[slot], xsem.at[slot])

        x_copy(0, 0).start()
        writes = [None, None]
        n_out = 0
        for r, blk in enumerate(order):
            if r == 1:
                for a in range(ex.n):
                    last[4 * a].wait_recv()
            elif 2 <= r <= 4:
                for a in range(ex.n):
                    arrivals[3 * a + r - 2].wait_recv()
                    passed[3 * a + r - 2].start()
            elif r >= 5:
                for a in range(ex.n):
                    last[4 * a + r - 4].wait_recv()
            w_copy = pltpu.make_async_copy(win_hbm if r == 0 else win_g.at[blk], w_s, wsem.at[0])
            w_copy.start()
            w_copy.wait()
            for q in range(nq):
                rows = pl.ds(q * chunk, chunk)
                if r == 0:
                    x_copy(q, q % 2).wait()
                    if q + 1 < nq:
                        x_copy(q + 1, (q + 1) % 2).start()
                    xb[rows, :] = xstage[q % 2].astype(BF16)
                slot = n_out % 2
                if writes[slot] is not None:
                    writes[slot].wait()
                out_s[slot] = _dot_nt(xb[rows, :], w_s[...])
                writes[slot] = pltpu.make_async_copy(out_s.at[slot], proj_hbm.at[blk, rows], osem.at[slot])
                writes[slot].start()
                n_out += 1
        for cp in writes:
            cp.wait()
        for cp in first + passed:
            cp.wait_send()
        for cp in locals_:
            cp.wait()

    S = jax.ShapeDtypeStruct
    res = pl.pallas_call(
        body, name="gather_proj",
        out_shape=(S((N_DEV, T, MXU_COLS), F32), *ex.out_shape),
        in_specs=[ANY] * 4, out_specs=(ANY,) * 4,
        scratch_shapes=[pltpu.VMEM((T, D_MODEL), BF16), pltpu.VMEM((2, chunk, D_MODEL), F32),
                        pltpu.VMEM(win_b.shape, BF16), pltpu.VMEM((2, chunk, MXU_COLS), F32),
                        pltpu.SemaphoreType.DMA((2,)), pltpu.SemaphoreType.DMA((1,)), pltpu.SemaphoreType.DMA((2,))]
        + ex.scratch,
        compiler_params=_params(48),
    )(x, win_b, wout_b, cw_b)
    return res[0], res[1:]


def _forward_step(n_steps):
    return (5 * n_steps) // 8


def _hosted(ex, refs, n_in, n_out):
    ins, ex_src = refs[:n_in], refs[n_in:n_in + ex.n]
    rest = refs[n_in + ex.n:]
    outs, ex_dst = rest[:n_out], rest[n_out:n_out + ex.n]
    rest = rest[n_out + ex.n:]
    n_own = len(rest) - len(ex.scratch)
    return ins, outs, rest[:n_own], (ex_src, ex_dst, rest[n_own:])


def _proj_cols(proj_ref, first, count):
    return jnp.concatenate([proj_ref[j] for j in range(first, first + count)], axis=1)


def _fwd_mix(x, proj, wout_g, sgu_g, sgu_b, wcat, bs_full, cw, cb, cg, cbeta, tm, ex):
    T = x.shape[0]
    nt = T // tm

    def body(*refs):
        ins, outs, scratch, ex_refs = _hosted(ex, refs, 11, 5)
        x_ref, proj_ref, wout_ref, sg_ref, sb_ref, wcat_ref, bs_ref, cw_ref, cb_ref, cg_ref, cbeta_ref = ins
        ycat_ref, n1_ref, rstd1_ref, phi_ref, y_ref = outs
        hpad, hshift = scratch
        i = pl.program_id(0)

        @pl.when(i == 0)
        def _():
            ex.start(*ex_refs)

        xf = x_ref[...]
        pre = _proj_cols(proj_ref, 0, 4)
        cdf = _normal_cdf(pre)
        phi_ref[...] = cdf.astype(BF16)
        u = pre[:, 0:D_SGU] * cdf[:, 0:D_SGU]
        v = pre[:, D_SGU:2 * D_SGU] * cdf[:, D_SGU:2 * D_SGU]
        vhat, _ = _ln_fwd(v)
        vn = vhat * sg_ref[...] + sb_ref[...]
        lo = _lo_mask()
        for c in range(tm // CHUNK):
            rows = slice(CHUNK * c, CHUNK * (c + 1))
            for p in range(4):
                lanes = slice(CHUNK * p, CHUNK * (p + 1))
                mixed = _dot(wcat_ref[p], _head_pair_stack(vn[rows, lanes], lo)) + bs_ref[:, lanes]
                ycat_ref[rows, lanes] = (u[rows, lanes] * mixed).astype(BF16)
        a = _proj_cols(proj_ref, 4, 2)
        g = _proj_cols(proj_ref, 6, 2)

        @pl.when(i == 0)
        def _():
            hpad[0:HALO, :] = jnp.zeros((HALO, D_CONV), F32)

        hpad[HALO:HALO + tm, :] = a * jax.nn.sigmoid(g)
        _shifted_copies(hpad, hshift, tm + SHIFT_ROWS)
        _causal_conv(hpad, hshift, cw_ref, y_ref, tm, lambda k: HALO - (CONV_WIDTH - 1) + k, bias=cb_ref[...])
        hpad[0:HALO, :] = hpad[tm:tm + HALO, :]
        yhat, _ = _ln_fwd(y_ref[...])
        yn = yhat * cg_ref[...] + cbeta_ref[...]
        ycat_ref[:, D_SGU:D_SGU + D_CONV] = (yn * jax.nn.sigmoid(yn)).astype(BF16)
        r1 = ALPHA * xf + _dot(ycat_ref[...], wout_ref[...])
        n1, rstd1 = _ln_fwd(r1)
        n1_ref[...] = n1
        rstd1_ref[...] = rstd1

        @pl.when(i == _forward_step(nt))
        def _():
            ex.forward(*ex_refs)

        @pl.when(i == nt - 1)
        def _():
            ex.wait(*ex_refs)

    S = jax.ShapeDtypeStruct
    row = lambda w: pl.BlockSpec((tm, w), lambda i: (i, 0))
    res = pl.pallas_call(
        body, name="fwd_mix", grid=(nt,),
        in_specs=[row(D_MODEL), pl.BlockSpec((N_DEV, tm, MXU_COLS), lambda i: (0, i, 0)), _full(wout_g.shape),
                  _full(sgu_g.shape), _full(sgu_b.shape), _full(wcat.shape), _full(bs_full.shape), _full(cw.shape),
                  _full(cb.shape), _full(cg.shape), _full(cbeta.shape)] + [ANY] * ex.n,
        out_specs=(row(D_MODEL), row(D_MODEL), row(1), row(2 * D_SGU), row(D_CONV)) + (ANY,) * ex.n,
        out_shape=(S((T, D_MODEL), BF16), S((T, D_MODEL), F32), S((T, 1), F32),
                   S((T, 2 * D_SGU), BF16), S((T, D_CONV), F32), *ex.out_shape),
        scratch_shapes=[pltpu.VMEM((tm + HALO, D_CONV), F32),
                        pltpu.VMEM((SUBLANES - 1, tm + SHIFT_ROWS, D_CONV), F32)] + ex.scratch,
        compiler_params=_params(56, 1),
    )(x, proj, wout_g, sgu_g, sgu_b, wcat, bs_full, cw, cb, cg, cbeta, *ex.arrays)
    return res[:5], res[5:]


def _load_resident(pairs, sems):
    cps = [pltpu.make_async_copy(s, d, sems.at[k]) for k, (s, d) in enumerate(pairs)]
    for cp in cps:
        cp.start()
    for cp in cps:
        cp.wait()


def _fwd_mlp(n1, tgt, l1g, l1b, l2g, l2b, wgt, wut, wd, tm):
    T = n1.shape[0]
    nt = T // tm
    nf = D_FF // MXU_COLS

    def body(n1_ref, tgt_ref, l1g_ref, l1b_ref, l2g_ref, l2b_ref, wg_hbm, wu_hbm, wd_hbm,
             gate_ref, up_ref, hh_ref, dr2_ref, stat_ref, wg_s, wu_s, wd_s, sems):
        i = pl.program_id(0)

        @pl.when(i == 0)
        def _():
            _load_resident([(wg_hbm, wg_s), (wu_hbm, wu_s), (wd_hbm, wd_s)], sems)
            stat_ref[...] = jnp.zeros(stat_ref.shape, F32)

        x1 = n1_ref[...] * l1g_ref[...] + l1b_ref[...]
        x1b = x1.astype(BF16)
        for f in range(nf):
            cols = slice(MXU_COLS * f, MXU_COLS * (f + 1))
            gt = _dot_nt(x1b, wg_s[cols, :])
            ut = _dot_nt(x1b, wu_s[cols, :])
            gate_ref[:, cols] = gt.astype(BF16)
            up_ref[:, cols] = ut.astype(BF16)
            hh_ref[:, cols] = (gt * jax.nn.sigmoid(gt) * ut).astype(BF16)
        r2 = ALPHA * x1 + _dot(hh_ref[...], wd_s[...])
        n2, rstd2 = _ln_fwd(r2)
        x2 = n2 * l2g_ref[...] + l2b_ref[...]
        diff = x2 - tgt_ref[...]
        dx2 = diff * (1.0 / D_MODEL)
        stat_ref[0:1, :] += _colsum(diff * diff)
        stat_ref[1:2, :] += _colsum(dx2 * n2)
        stat_ref[2:3, :] += _colsum(dx2)
        dr2_ref[...] = _ln_bwd(dx2 * l2g_ref[...], n2, rstd2)

    S = jax.ShapeDtypeStruct
    row = lambda w: pl.BlockSpec((tm, w), lambda i: (i, 0))
    vec = _full((1, D_MODEL))
    return pl.pallas_call(
        body, name="fwd_mlp", grid=(nt,),
        in_specs=[row(D_MODEL), row(D_MODEL), vec, vec, vec, vec, ANY, ANY, ANY],
        out_specs=(row(D_FF), row(D_FF), row(D_FF), row(D_MODEL), _full((8, D_MODEL))),
        out_shape=(S((T, D_FF), BF16), S((T, D_FF), BF16), S((T, D_FF), BF16), S((T, D_MODEL), F32),
                   S((8, D_MODEL), F32)),
        scratch_shapes=[pltpu.VMEM((D_FF, D_MODEL), BF16)] * 3 + [pltpu.SemaphoreType.DMA((3,))],
        compiler_params=_params(56, 1),
    )(n1, tgt, l1g, l1b, l2g, l2b, wgt, wut, wd)


def _bwd_mlp(dr2, gate, up, n1, rstd1, l1g, wgt, wut, wd, tm):
    T = n1.shape[0]
    nt = T // tm
    nf = D_FF // MXU_COLS

    def body(dr2_ref, gate_ref, up_ref, n1_ref, rstd1_ref, l1g_ref, wg_hbm, wu_hbm, wd_hbm,
             dgate_ref, dup_ref, dr1_ref, stat_ref, wg_s, wu_s, wd_s, sems):
        i = pl.program_id(0)

        @pl.when(i == 0)
        def _():
            _load_resident([(wg_hbm, wg_s), (wu_hbm, wu_s), (wd_hbm, wd_s)], sems)
            stat_ref[...] = jnp.zeros(stat_ref.shape, F32)

        dr2 = dr2_ref[...]
        dr2b = dr2.astype(BF16)
        for f in range(nf):
            cols = slice(MXU_COLS * f, MXU_COLS * (f + 1))
            dhh = _dot_nt(dr2b, wd_s[cols, :])
            gt = gate_ref[:, cols].astype(F32)
            ut = up_ref[:, cols].astype(F32)
            sg = jax.nn.sigmoid(gt)
            dgate_ref[:, cols] = (dhh * ut * (sg * (1.0 + gt * (1.0 - sg)))).astype(BF16)
            dup_ref[:, cols] = (dhh * (gt * sg)).astype(BF16)
        dx1 = ALPHA * dr2 + _dot(dgate_ref[...], wg_s[...]) + _dot(dup_ref[...], wu_s[...])
        n1 = n1_ref[...]
        stat_ref[0:1, :] += _colsum(dx1 * n1)
        stat_ref[1:2, :] += _colsum(dx1)
        dr1_ref[...] = _ln_bwd(dx1 * l1g_ref[...], n1, rstd1_ref[...])

    S = jax.ShapeDtypeStruct
    row = lambda w: pl.BlockSpec((tm, w), lambda i: (i, 0))
    return pl.pallas_call(
        body, name="bwd_mlp", grid=(nt,),
        in_specs=[row(D_MODEL), row(D_FF), row(D_FF), row(D_MODEL), row(1), _full((1, D_MODEL)), ANY, ANY, ANY],
        out_specs=(row(D_FF), row(D_FF), row(D_MODEL), _full((8, D_MODEL))),
        out_shape=(S((T, D_FF), BF16), S((T, D_FF), BF16), S((T, D_MODEL), F32), S((8, D_MODEL), F32)),
        scratch_shapes=[pltpu.VMEM((D_FF, D_MODEL), BF16)] * 3 + [pltpu.SemaphoreType.DMA((3,))],
        compiler_params=_params(56, 1),
    )(dr2, gate, up, n1, rstd1, l1g, wgt, wut, wd)


def _bwd_mix(dr1, proj, phi, y, win_g, wout_g, sgu_g, sgu_b, wcat, wcatt, bs_full, cw, cg, cbeta, tm, ex):
    T = dr1.shape[0]
    nt = T // tm
    halo_blocks = tm // HALO

    def body(*refs):
        ins, outs, scratch, ex_refs = _hosted(ex, refs, 15, 6)
        (dr1_ref, proj_ref, halo_ref, phi_ref, y_ref, win_ref, wout_ref, sg_ref, sb_ref, wcat_ref, wcatt_ref, bs_ref,
         cw_ref, cg_ref, cbeta_ref) = ins
        gx_ref, dproj_ref, gws_out, gbs_out, gcw_ref, vec_ref = outs
        hpad, shift, dypad, dhbuf, dubuf, dvnbuf, gcw_acc, gws_ref, gbs_ref = scratch
        i = pl.program_id(0)
        tile = nt - 1 - i

        @pl.when(i == 0)
        def _():
            ex.start(*ex_refs)
            gws_ref[...] = jnp.zeros(gws_ref.shape, F32)
            gbs_ref[...] = jnp.zeros(gbs_ref.shape, F32)
            gcw_ref[...] = jnp.zeros(gcw_ref.shape, F32)
            vec_ref[...] = jnp.zeros(vec_ref.shape, F32)
            gcw_acc[...] = jnp.zeros(gcw_acc.shape, F32)
            dypad[tm:tm + HALO, :] = jnp.zeros((HALO, D_CONV), F32)

        dr1 = dr1_ref[...]
        dycat = _dot_nt(dr1.astype(BF16), wout_ref[...])
        pu = _proj_cols(proj_ref, 0, 2)
        pv = _proj_cols(proj_ref, 2, 2)
        cdf_u = phi_ref[:, 0:D_SGU].astype(F32)
        cdf_v = phi_ref[:, D_SGU:2 * D_SGU].astype(F32)
        u = pu * cdf_u
        vhat, rstd_v = _ln_fwd(pv * cdf_v)
        vn = vhat * sg_ref[...] + sb_ref[...]
        lo = _lo_mask()
        for c in range(tm // CHUNK):
            rows = slice(CHUNK * c, CHUNK * (c + 1))
            for p in range(4):
                lanes = slice(CHUNK * p, CHUNK * (p + 1))
                vstack = _head_pair_stack(vn[rows, lanes], lo)
                mixed = _dot(wcat_ref[p], vstack) + bs_ref[:, lanes]
                d_a = dycat[rows, lanes]
                dubuf[rows, lanes] = d_a * mixed
                dm = d_a * u[rows, lanes]
                gbs_ref[:, lanes] += dm
                dstack = _head_pair_stack(dm, lo)
                gws_ref[2 * CHUNK * p:2 * CHUNK * (p + 1), :] += _dot_nt(dstack, vn[rows, lanes].astype(BF16))
                dvnbuf[rows, lanes] = _dot(wcatt_ref[p], dstack)
        dvn = dvnbuf[...]
        vec_ref[0:1, :] += _colsum(dvn * vhat)
        vec_ref[1:2, :] += _colsum(dvn)
        dv = _ln_bwd(dvn * sg_ref[...], vhat, rstd_v)
        dproj_ref[:, 0:D_SGU] = (dubuf[...] * _gelu_grad(pu, cdf_u)).astype(BF16)
        dproj_ref[:, D_SGU:2 * D_SGU] = (dv * _gelu_grad(pv, cdf_v)).astype(BF16)
        base = 2 * D_SGU
        a = _proj_cols(proj_ref, 4, 2)
        sgm = jax.nn.sigmoid(_proj_cols(proj_ref, 6, 2))
        h_before = _proj_cols(halo_ref, 0, 2) * jax.nn.sigmoid(_proj_cols(halo_ref, 2, 2))
        hpad[0:HALO, :] = jnp.where(tile > 0, h_before, 0.0)
        hpad[HALO:HALO + tm, :] = a * sgm
        _shifted_copies(hpad, shift, tm + SHIFT_ROWS)
        h_offset = lambda k: HALO - (CONV_WIDTH - 1) + k
        yhat, rstd_y = _ln_fwd(y_ref[...])
        yn = yhat * cg_ref[...] + cbeta_ref[...]
        s = jax.nn.sigmoid(yn)
        dyn = dycat[:, D_SGU:D_SGU + D_CONV] * (s * (1.0 + yn * (1.0 - s)))
        vec_ref[3:4, :] += _colsum(dyn * yhat)
        vec_ref[4:5, :] += _colsum(dyn)
        dy = _ln_bwd(dyn * cg_ref[...], yhat, rstd_y)
        vec_ref[2:3, :] += _colsum(dy)
        dypad[0:tm, :] = dy
        _conv_weight_grad(dypad, hpad, shift, gcw_acc, tm, h_offset)
        _shifted_copies(dypad, shift, tm + SHIFT_ROWS)
        _causal_conv(dypad, shift, cw_ref, dhbuf, tm, lambda k: (CONV_WIDTH - 1) - k)
        dypad[tm:tm + HALO, :] = dypad[0:HALO, :]
        dh = dhbuf[...]
        dproj_ref[:, base:base + D_CONV] = (dh * sgm).astype(BF16)
        dproj_ref[:, base + D_CONV:base + 2 * D_CONV] = (dh * a * sgm * (1.0 - sgm)).astype(BF16)
        gx_ref[...] = ALPHA * dr1 + _dot(dproj_ref[...], win_ref[...])

        @pl.when(i == _forward_step(nt))
        def _():
            ex.forward(*ex_refs)

        @pl.when(i == nt - 1)
        def _():
            gcw_ref[...] = gcw_acc[...].sum(axis=1)
            gws_out[...] = gws_ref[...].astype(BF16)
            gbs_out[...] = lax.dot_general(_head_selector(), gbs_ref[...], (((1,), (1,)), ((), ())),
                                           preferred_element_type=F32, precision=lax.Precision.HIGHEST)
            ex.wait(*ex_refs)

    S = jax.ShapeDtypeStruct
    row = lambda w: pl.BlockSpec((tm, w), lambda i: (nt - 1 - i, 0))
    proj_tile = pl.BlockSpec((N_DEV, tm, MXU_COLS), lambda i: (0, nt - 1 - i, 0))
    halo = pl.BlockSpec((N_DEV // 2, HALO, MXU_COLS),
                        lambda i: (1, jnp.maximum((nt - 1 - i) * halo_blocks - 1, 0), 0))
    res = pl.pallas_call(
        body, name="bwd_mix", grid=(nt,),
        in_specs=[row(D_MODEL), proj_tile, halo, row(2 * D_SGU), row(D_CONV), _full(win_g.shape),
                  _full(wout_g.shape), _full(sgu_g.shape), _full(sgu_b.shape), _full(wcat.shape), _full(wcatt.shape),
                  _full(bs_full.shape), _full(cw.shape), _full(cg.shape), _full(cbeta.shape)]
        + [ANY] * ex.n,
        out_specs=(row(D_MODEL), row(2 * D_MODEL), _full((N_HEADS * CHUNK, CHUNK)), _full((N_HEADS, CHUNK)),
                   _full((CONV_ROWS, D_CONV)), _full((8, D_CONV))) + (ANY,) * ex.n,
        out_shape=(S((T, D_MODEL), F32), S((T, 2 * D_MODEL), BF16), S((N_HEADS * CHUNK, CHUNK), BF16),
                   S((N_HEADS, CHUNK), F32), S((CONV_ROWS, D_CONV), F32), S((8, D_CONV), F32), *ex.out_shape),
        scratch_shapes=[pltpu.VMEM((tm + HALO, D_CONV), F32), pltpu.VMEM((SUBLANES - 1, tm + SHIFT_ROWS, D_CONV), F32),
                        pltpu.VMEM((tm + HALO, D_CONV), F32),
                        pltpu.VMEM((tm, D_CONV), F32), pltpu.VMEM((tm, D_SGU), F32),
                        pltpu.VMEM((tm, D_SGU), F32), pltpu.VMEM((CONV_ROWS, 8, D_CONV), F32),
                        pltpu.VMEM((N_HEADS * CHUNK, CHUNK), F32), pltpu.VMEM((CHUNK, D_SGU), F32)] + ex.scratch,
        compiler_params=_params(56, 1),
    )(dr1, proj, proj, phi, y, win_g, wout_g, sgu_g, sgu_b, wcat, wcatt, bs_full, cw, cg, cbeta, *ex.arrays)
    return res[:6], res[6:]


def _wgrad(name, a, b, blocks, tk, ex=None, b_cols=None, b_affine=None):
    T, M = a.shape
    col, N = b_cols or (0, b.shape[1])
    nk = T // tk
    out_shape = (blocks, M // blocks, N)
    ex = ex or _Exchange([], [])
    affine = list(b_affine or [])

    def body(*refs):
        ins, (o_ref,), (acc,), ex_refs = _hosted(ex, refs, 2 + len(affine), 1)
        a_ref, b_ref = ins[:2]
        i = pl.program_id(0)

        @pl.when(i == 0)
        def _():
            ex.start(*ex_refs)
            acc[...] = jnp.zeros(acc.shape, F32)

        right = b_ref[...]
        if affine:
            right = right * ins[2][...] + ins[3][...]
        acc[...] += _dot_tn(a_ref[...].astype(BF16), right.astype(BF16))

        @pl.when(i == _forward_step(nk))
        def _():
            ex.forward(*ex_refs)

        @pl.when(i == nk - 1)
        def _():
            o_ref[...] = acc[...].astype(BF16)
            ex.wait(*ex_refs)

    res = pl.pallas_call(
        body, name=name, grid=(nk,),
        in_specs=[pl.BlockSpec((tk, M), lambda i: (i, 0)), pl.BlockSpec((tk, N), lambda i: (i, col))]
        + [_full((1, N))] * len(affine) + [ANY] * ex.n,
        out_specs=(_full((M, N)),) + (ANY,) * ex.n,
        out_shape=(jax.ShapeDtypeStruct((M, N), BF16), *ex.out_shape),
        scratch_shapes=[pltpu.VMEM((M, N), F32)] + ex.scratch,
        compiler_params=_params(56, 1),
    )(a, b, *affine, *ex.arrays)
    g = res[0].reshape(out_shape)
    return (g, res[1:]) if ex.n else g


def _adamw(w, g, m, v):
    m2 = ADAM_B1 * m + (1.0 - ADAM_B1) * g
    v2 = ADAM_B2 * v + (1.0 - ADAM_B2) * (g * g)
    m_hat = m2 / (1.0 - ADAM_B1 ** ADAM_STEP)
    v_hat = v2 / (1.0 - ADAM_B2 ** ADAM_STEP)
    delta = -ADAM_LR * (m_hat / (jnp.sqrt(v_hat) + ADAM_EPS) + ADAM_WD * w)
    return delta, m2, v2


def _sum_partials(r_ref):
    g = r_ref[0].astype(F32)
    for s in range(1, N_DEV):
        g = g + r_ref[s].astype(F32)
    return g


def _staged_call(name, groups, vmem_mib, ex=None):
    ex = ex or _Exchange([], [])
    inputs = [a for ins, _, _ in groups for a in ins]
    out_shapes = [s for _, outs, _ in groups for s in outs]
    n_in, n_out = len(inputs), len(out_shapes)

    def body(*refs):
        ins, outs, scratch, ex_refs = _hosted(ex, refs, n_in, n_out)
        in_bufs, out_bufs, sems = scratch[:n_in], scratch[n_in:n_in + n_out], scratch[n_in + n_out]
        ex.start(*ex_refs)
        loads = [pltpu.make_async_copy(ins[k], in_bufs[k], sems.at[k]) for k in range(n_in)]
        stores = [pltpu.make_async_copy(out_bufs[k], outs[k], sems.at[n_in + k]) for k in range(n_out)]
        for cp in loads:
            cp.start()
        i0 = o0 = 0
        for g_ins, g_outs, compute in groups:
            i1, o1 = i0 + len(g_ins), o0 + len(g_outs)
            for cp in loads[i0:i1]:
                cp.wait()
            compute(in_bufs[i0:i1], out_bufs[o0:o1])
            for cp in stores[o0:o1]:
                cp.start()
            i0, o0 = i1, o1
        for cp in stores:
            cp.wait()
        ex.forward(*ex_refs)
        ex.wait(*ex_refs)

    scratch = ([pltpu.VMEM(a.shape, a.dtype) for a in inputs] + [pltpu.VMEM(s.shape, s.dtype) for s in out_shapes]
               + [pltpu.SemaphoreType.DMA((n_in + n_out,))] + ex.scratch)
    res = pl.pallas_call(
        body, name=name, out_shape=(*[pltpu.HBM(s.shape, s.dtype) for s in out_shapes], *ex.out_shape),
        in_specs=[HBM] * n_in + [ANY] * ex.n, out_specs=(HBM,) * n_out + (ANY,) * ex.n,
        scratch_shapes=scratch, compiler_params=_params(vmem_mib),
    )(*[pltpu.with_memory_space_constraint(a, pltpu.HBM) for a in inputs], *ex.arrays)
    per_group, o0 = [], 0
    for _, g_outs, _ in groups:
        per_group.append(list(res[o0:o0 + len(g_outs)]))
        o0 += len(g_outs)
    return per_group, res[n_out:]


def _adamw_shard_group(parts, w, m, v, transposed):
    n = len(parts)

    def compute(ins, outs):
        w_ref, m_ref, v_ref = ins[n:]
        lo = 0
        for r_ref in ins[:n]:
            g = _sum_partials(r_ref)
            cols = g.shape[1]
            if transposed:
                g, at = g.T, (slice(lo, lo + cols), slice(None))
            else:
                at = (slice(None), slice(lo, lo + cols))
            delta, m2, v2 = _adamw(w_ref[at], g, m_ref[at], v_ref[at])
            for o, val in zip(outs, (g, delta, m2, v2)):
                o[at] = val
            lo += cols

    return [*parts, w, m, v], [jax.ShapeDtypeStruct(w.shape, F32)] * 4, compute


SMALL_NAMES = ["sgu_ln_g", "sgu_ln_b", "w_s", "b_s", "conv_b", "conv_ln_g", "conv_ln_b", "ln1_g", "ln1_b", "ln2_g", "ln2_b",
               "conv_w"]


def _finish_small_group(gws8, gbs8, gcw8, vmix8, vmlp8, vout8, small):
    names = SMALL_NAMES
    flat = []
    for n in names:
        flat += list(small[n])
    cw_block = D_CONV // N_DEV

    def compute(ins, outs):
        gws_ref, gbs_ref, gcw_ref, vmix_ref, vmlp_ref, vout_ref = ins[:6]
        wmv = ins[6:]
        loss_o = outs[0]
        outs = outs[1:]
        gws = _sum_partials(gws_ref)
        gbs = _sum_partials(gbs_ref)
        vmix = _sum_partials(vmix_ref)
        vmlp = _sum_partials(vmlp_ref)
        vout = _sum_partials(vout_ref)
        x, y, c = _mesh_position()
        first = (4 * x + 2 * y + c) * cw_block
        pick = (lax.broadcasted_iota(jnp.int32, (D_CONV, cw_block), 0)
                == first + lax.broadcasted_iota(jnp.int32, (D_CONV, cw_block), 1)).astype(F32)
        gcw = jnp.dot(_sum_partials(gcw_ref), pick, preferred_element_type=F32,
                      precision=lax.Precision.HIGHEST)[0:CONV_WIDTH, :]
        loss = (0.5 / D_MODEL) * jnp.sum(vout[0:1, :], axis=1, keepdims=True)
        loss_o[...] = jnp.broadcast_to(loss, loss_o.shape)
        rows = lax.broadcasted_iota(jnp.int32, (N_HEADS * CHUNK, CHUNK), 0)
        cols = lax.broadcasted_iota(jnp.int32, (N_HEADS * CHUNK, CHUNK), 1)
        gws = jnp.where((rows & (CHUNK - 1)) >= cols, gws, 0.0)
        grads = {
            "sgu_ln_g": vmix[0:1, :], "sgu_ln_b": vmix[1:2, :], "w_s": gws, "b_s": gbs,
            "conv_b": vmix[2:3, :], "conv_ln_g": vmix[3:4, :], "conv_ln_b": vmix[4:5, :],
            "ln1_g": vmlp[0:1, :], "ln1_b": vmlp[1:2, :], "ln2_g": vout[1:2, :], "ln2_b": vout[2:3, :],
            "conv_w": gcw,
        }
        for k, n in enumerate(names):
            w_ref, m_ref, v_ref = wmv[3 * k:3 * k + 3]
            g = grads[n]
            delta, m2, v2 = _adamw(w_ref[...], g, m_ref[...], v_ref[...])
            outs[4 * k][...] = g
            outs[4 * k + 1][...] = delta
            outs[4 * k + 2][...] = m2
            outs[4 * k + 3][...] = v2

    S = jax.ShapeDtypeStruct
    out_shape = [S((SUBLANES, 128), F32)]
    for n in names:
        out_shape += [S(small[n][0].shape, F32)] * 4
    return [gws8, gbs8, gcw8, vmix8, vmlp8, vout8, *flat], out_shape, compute


TOKEN_TILE_FWD_MIX = 512
TOKEN_TILE_BWD_MIX = 512
TOKEN_TILE_FWD_MLP = 512
TOKEN_TILE_BWD_MLP = 512
TOKEN_TILE_WGRAD = 1024


def kernel(x, w_in, sgu_ln_g, sgu_ln_b, w_s, b_s, conv_w, conv_b, conv_ln_g, conv_ln_b, w_out, ln1_g, ln1_b, w_gate, w_up, w_down, ln2_g, ln2_b, loss_target, m_w_in, m_sgu_ln_g, m_sgu_ln_b, m_w_s, m_b_s, m_conv_w, m_conv_b, m_conv_ln_g, m_conv_ln_b, m_w_out, m_ln1_g, m_ln1_b, m_w_gate, m_w_up, m_w_down, m_ln2_g, m_ln2_b, v_w_in, v_sgu_ln_g, v_sgu_ln_b, v_w_s, v_b_s, v_conv_w, v_conv_b, v_conv_ln_g, v_conv_ln_b, v_w_out, v_ln1_g, v_ln1_b, v_w_gate, v_w_up, v_w_down, v_ln2_g, v_ln2_b):
    xs = x[0]
    tgt = loss_target[0]

    (win_b, wout_b, wgt_b, wut_b, wd_b, cw_b, wcat, wcatt, bs_full) = _prep_weights(
        w_in[0], w_out[0], w_gate[0].T, w_up[0].T, w_down[0], conv_w[0], w_s[0], b_s[0])
    proj, (win_g, wout_g, cw_g) = _gather_proj(xs, win_b, wout_b, cw_b)
    win_g = win_g.reshape(2 * D_MODEL, D_MODEL)
    wout_g = wout_g.reshape(D_MODEL, D_MODEL)
    cw = jnp.transpose(cw_g[:, :, :D_CONV // N_DEV], (1, 0, 2)).reshape(CONV_ROWS, D_CONV)

    (ycat, n1, rstd1, phi, y_conv), (wgt_g, wut_g, wd_g) = _fwd_mix(
        xs, proj, wout_g, sgu_ln_g, sgu_ln_b, wcat, bs_full, cw, conv_b, conv_ln_g, conv_ln_b, TOKEN_TILE_FWD_MIX,
        _Exchange([], [wgt_b, wut_b, wd_b]))
    wgt_g = wgt_g.reshape(D_FF, D_MODEL)
    wut_g = wut_g.reshape(D_FF, D_MODEL)
    wd_g = wd_g.reshape(D_FF, D_MODEL)
    gate, up, hh, dr2, vout = _fwd_mlp(n1, tgt, ln1_g, ln1_b, ln2_g, ln2_b, wgt_g, wut_g, wd_g, TOKEN_TILE_FWD_MLP)

    dgate, dup, dr1, vmlp = _bwd_mlp(dr2, gate, up, n1, rstd1, ln1_g, wgt_g, wut_g, wd_g, TOKEN_TILE_BWD_MLP)
    tk = TOKEN_TILE_WGRAD
    x1 = dict(b_affine=(ln1_g, ln1_b))
    g_wgt = _wgrad("wgrad_gate", dgate, n1, N_DEV, tk, **x1)
    g_wut = _wgrad("wgrad_up", dup, n1, N_DEV, tk, **x1)
    g_wd = _wgrad("wgrad_down", hh, dr2, N_DEV, tk)
    g_wout = _wgrad("wgrad_out", ycat, dr1, N_DEV, tk)
    (gx, dproj, gws, gbs, gcw, vmix), (r_wgt, r_wut, r_wd, r_wout) = _bwd_mix(
        dr1, proj, phi, y_conv, win_g, wout_g, sgu_ln_g, sgu_ln_b, wcat, wcatt, bs_full, cw, conv_ln_g, conv_ln_b,
        TOKEN_TILE_BWD_MIX, _Exchange([g_wgt, g_wut, g_wd, g_wout], []))
    half = D_MODEL // 2
    g_win_a, (gws8, gbs8, gcw8, vmix8, vmlp8, vout8) = _wgrad(
        "wgrad_in_a", dproj, xs, N_DEV, tk, _Exchange([], [gws, gbs, gcw, vmix, vmlp, vout]), b_cols=(0, half))
    g_win_b, (r_win_a,) = _wgrad("wgrad_in_b", dproj, xs, N_DEV, tk, _Exchange([g_win_a], []), b_cols=(1, half))
    (r_win_b,) = _exchange("exchange_grad_in", [g_win_b], [])
    (u_gate, u_up, u_down), _ = _staged_call(
        "adamw_mlp",
        [_adamw_shard_group([r_wgt], w_gate[0].T, m_w_gate[0].T, v_w_gate[0].T, False),
         _adamw_shard_group([r_wut], w_up[0].T, m_w_up[0].T, v_w_up[0].T, False),
         _adamw_shard_group([r_wd], w_down[0], m_w_down[0], v_w_down[0], False)], 56)
    small_in = {
        "conv_w": (conv_w[0], m_conv_w[0], v_conv_w[0]),
        "sgu_ln_g": (sgu_ln_g, m_sgu_ln_g, v_sgu_ln_g), "sgu_ln_b": (sgu_ln_b, m_sgu_ln_b, v_sgu_ln_b),
        "w_s": tuple(a.reshape(N_HEADS * CHUNK, CHUNK) for a in (w_s, m_w_s, v_w_s)),
        "b_s": (b_s[0], m_b_s[0], v_b_s[0]),
        "conv_b": (conv_b, m_conv_b, v_conv_b), "conv_ln_g": (conv_ln_g, m_conv_ln_g, v_conv_ln_g),
        "conv_ln_b": (conv_ln_b, m_conv_ln_b, v_conv_ln_b),
        "ln1_g": (ln1_g, m_ln1_g, v_ln1_g), "ln1_b": (ln1_b, m_ln1_b, v_ln1_b),
        "ln2_g": (ln2_g, m_ln2_g, v_ln2_g), "ln2_b": (ln2_b, m_ln2_b, v_ln2_b),
    }
    (u_in, u_out, fin), _ = _staged_call(
        "adamw_mix_small",
        [_adamw_shard_group([r_win_a, r_win_b], w_in[0], m_w_in[0], v_w_in[0], True),
         _adamw_shard_group([r_wout], w_out[0], m_w_out[0], v_w_out[0], False),
         _finish_small_group(gws8, gbs8, gcw8, vmix8, vmlp8, vout8, small_in)], 40)
    big = {"w_in": u_in, "w_out": u_out, "w_gate": u_gate, "w_up": u_up, "w_down": u_down}
    loss11 = fin[0]
    small = {n: fin[1 + 4 * k:5 + 4 * k] for k, n in enumerate(SMALL_NAMES)}

    shapes = {"w_s": w_s.shape, "b_s": b_s.shape, "conv_w": conv_w.shape}
    out = {}
    for n, r in big.items():
        out[n] = tuple((a.T if n in ("w_gate", "w_up") else a)[None] for a in r)
    for n, r in small.items():
        out[n] = tuple(a.reshape(shapes[n]) for a in r) if n in shapes else tuple(r)

    order = ["w_in", "sgu_ln_g", "sgu_ln_b", "w_s", "b_s", "conv_w", "conv_b", "conv_ln_g", "conv_ln_b", "w_out",
             "ln1_g", "ln1_b", "w_gate", "w_up", "w_down", "ln2_g", "ln2_b"]
    loss = loss11[0, 0]
    return (loss, gx[None], *[out[n][0] for n in order], *[out[n][1] for n in order],
            *[out[n][2] for n in order], *[out[n][3] for n in order])
```

```python
import jax
import jax.numpy as jnp
from jax import lax
from jax.experimental import pallas as pl
from jax.experimental.pallas import tpu as pltpu

F32 = jnp.float32
BF16 = jnp.bfloat16

D_MODEL = 1024
D_SGU = 512
D_CONV = 512
N_HEADS = 8
CHUNK = 128
CONV_WIDTH = 31
CONV_ROWS = 32
HALO = 32
D_FF = 2816
N_DEV = 8
FF_SHARD = D_FF // N_DEV
ALPHA = (2.0 * 1) ** 0.25
LN_EPS = 1e-5
INV_SQRT2 = 0.7071067811865476
INV_SQRT_2PI = 0.3989422804014327

ADAM_LR = 0.001
ADAM_B1 = 0.9
ADAM_B2 = 0.999
ADAM_EPS = 1e-08
ADAM_WD = 0.01
ADAM_STEP = 10

MXU_COLS = 256
SUBLANES = 8
CONV_ROW_BLOCK = 32
WGRAD_ROW_BLOCK = 32
SHIFT_ROWS = HALO - SUBLANES
MIB = 1024 * 1024

HBM = pl.BlockSpec(memory_space=pltpu.HBM)
ANY = pl.BlockSpec(memory_space=pl.ANY)
MESH = pl.DeviceIdType.MESH


def _params(vmem_mib, grid_dims=0):
    kw = dict(vmem_limit_bytes=vmem_mib * MIB)
    if grid_dims:
        kw["dimension_semantics"] = ("arbitrary",) * grid_dims
    return pltpu.CompilerParams(**kw)


def _full(shape):
    return pl.BlockSpec(shape, lambda i: (0,) * len(shape))


def _dot(a, b):
    return jnp.dot(a, b, preferred_element_type=F32)


def _dot_nt(a, b):
    return lax.dot_general(a, b, (((1,), (1,)), ((), ())), preferred_element_type=F32)


def _dot_tn(a, b):
    return lax.dot_general(a, b, (((0,), (0,)), ((), ())), preferred_element_type=F32)


def _normal_cdf(x):
    return 0.5 * (1.0 + lax.erf(x * INV_SQRT2))


def _gelu_grad(x, cdf):
    return cdf + x * jnp.exp(-0.5 * x * x) * INV_SQRT_2PI


def _ln_fwd(v):
    mu = jnp.mean(v, axis=-1, keepdims=True)
    d = v - mu
    var = jnp.mean(d * d, axis=-1, keepdims=True)
    rstd = lax.rsqrt(var + LN_EPS)
    return d * rstd, rstd


def _ln_bwd(dyhat, yhat, rstd):
    m1 = jnp.mean(dyhat, axis=-1, keepdims=True)
    m2 = jnp.mean(dyhat * yhat, axis=-1, keepdims=True)
    return rstd * (dyhat - m1 - yhat * m2)


def _colsum(v):
    return jnp.sum(v, axis=0, keepdims=True)


def _head_pair_stack(v, lo):
    return jnp.concatenate([jnp.where(lo, v, 0.0), jnp.where(lo, 0.0, v)], axis=0).astype(BF16)


def _lo_mask():
    return lax.broadcasted_iota(jnp.int32, (CHUNK, CHUNK), 1) < (CHUNK // 2)


def _head_selector():
    head = lax.broadcasted_iota(jnp.int32, (N_HEADS, D_SGU), 0)
    lane = lax.broadcasted_iota(jnp.int32, (N_HEADS, D_SGU), 1)
    width = D_SGU // N_HEADS
    return ((lane >= head * width) & (lane < (head + 1) * width)).astype(F32)


def _shifted_copies(pad_ref, sh_ref, rows):
    for r in range(1, SUBLANES):
        sh_ref[r - 1, 0:rows, :] = pad_ref[pl.ds(r, rows), :]


def _tap_groups(offset_of_tap):
    groups = {}
    for k in range(CONV_WIDTH):
        o = offset_of_tap(k)
        groups.setdefault(o % SUBLANES, []).append((k, o // SUBLANES))
    return groups


def _tap_window(pad_ref, sh_ref, r, taps, row0, rows):
    q0 = min(q for _, q in taps)
    q1 = max(q for _, q in taps)
    src = pad_ref if r == 0 else sh_ref.at[r - 1]
    win = src[pl.ds(row0 + SUBLANES * q0, SUBLANES * (q1 - q0) + rows), :]
    return win, [(k, SUBLANES * (q - q0)) for k, q in taps]


def _causal_conv(pad_ref, sh_ref, w_ref, out_ref, rows, offset_of_tap, bias=None):
    groups = _tap_groups(offset_of_tap)

    def block(b, carry):
        row0 = pl.multiple_of(b * CONV_ROW_BLOCK, CONV_ROW_BLOCK)
        if bias is None:
            acc = jnp.zeros((CONV_ROW_BLOCK, D_CONV), F32)
        else:
            acc = jnp.broadcast_to(bias, (CONV_ROW_BLOCK, D_CONV))
        for r, taps in groups.items():
            win, starts = _tap_window(pad_ref, sh_ref, r, taps, row0, CONV_ROW_BLOCK)
            for k, s in starts:
                acc = acc + w_ref[k:k + 1, :] * win[s:s + CONV_ROW_BLOCK, :]
        out_ref[pl.ds(row0, CONV_ROW_BLOCK), :] = acc
        return carry

    lax.fori_loop(0, rows // CONV_ROW_BLOCK, block, 0)


def _conv_weight_grad(dy_ref, pad_ref, sh_ref, acc_ref, rows, offset_of_tap):
    groups = _tap_groups(offset_of_tap)
    for r, taps in groups.items():

        def block(b, parts, r=r, taps=taps):
            row0 = pl.multiple_of(b * WGRAD_ROW_BLOCK, WGRAD_ROW_BLOCK)
            dyb = dy_ref[pl.ds(row0, WGRAD_ROW_BLOCK), :]
            win, starts = _tap_window(pad_ref, sh_ref, r, taps, row0, WGRAD_ROW_BLOCK)
            out = []
            for part, (_, s) in zip(parts, starts):
                pr = dyb * win[s:s + WGRAD_ROW_BLOCK, :]
                out.append(part + pr.reshape(WGRAD_ROW_BLOCK // SUBLANES, SUBLANES, D_CONV).sum(axis=0))
            return tuple(out)

        zeros = tuple(jnp.zeros((SUBLANES, D_CONV), F32) for _ in taps)
        parts = lax.fori_loop(0, rows // WGRAD_ROW_BLOCK, block, zeros)
        for part, (k, _) in zip(parts, taps):
            acc_ref[k] += part


def _prep_weights(w_in, w_out, w_gate_t, w_up_t, w_down, conv_w, w_s, b_s):
    def compute(ins, outs):
        win_ref, wout_ref, wgt_ref, wut_ref, wd_ref, cw_ref, ws_ref, bs_ref = ins
        win_o, wout_o, wgt_o, wut_o, wd_o, cw_o, wcat_o, wcatt_o, bsf_o = outs
        win_o[...] = win_ref[...].T.astype(BF16)
        wout_o[...] = wout_ref[...].astype(BF16)
        wgt_o[...] = wgt_ref[...].astype(BF16)
        wut_o[...] = wut_ref[...].astype(BF16)
        wd_o[...] = wd_ref[...].astype(BF16)
        cw_o[...] = jnp.zeros(cw_o.shape, F32)
        cw_o[0:CONV_WIDTH, 0:D_CONV // N_DEV] = cw_ref[...]
        row = lax.broadcasted_iota(jnp.int32, (CHUNK, CHUNK), 0)
        col = lax.broadcasted_iota(jnp.int32, (CHUNK, CHUNK), 1)
        causal = row >= col
        for h in range(N_HEADS):
            w = jnp.where(causal, ws_ref[h], 0.0)
            p, half = h // 2, (h % 2) * CHUNK
            wcat_o[p, :, half:half + CHUNK] = w.astype(BF16)
            wcatt_o[p, :, half:half + CHUNK] = w.T.astype(BF16)
        bsf_o[...] = lax.dot_general(bs_ref[...], _head_selector(), (((0,), (0,)), ((), ())),
                                     preferred_element_type=F32, precision=lax.Precision.HIGHEST)

    S = jax.ShapeDtypeStruct
    out_shapes = [S((256, D_MODEL), BF16), S((128, D_MODEL), BF16), S((FF_SHARD, D_MODEL), BF16),
                  S((FF_SHARD, D_MODEL), BF16), S((FF_SHARD, D_MODEL), BF16), S((CONV_ROWS, 128), F32),
                  S((4, CHUNK, 2 * CHUNK), BF16), S((4, CHUNK, 2 * CHUNK), BF16), S((CHUNK, D_SGU), F32)]
    (res,), _ = _staged_call(
        "prep_weights", [([w_in, w_out, w_gate_t, w_up_t, w_down, conv_w, w_s, b_s], out_shapes, compute)], 32)
    return res


def _mesh_position():
    x, y, c = lax.axis_index("x"), lax.axis_index("y"), lax.axis_index("c")
    return x, y, c


def _peers(x, y, c):
    out = []
    for k in range(1, N_DEV):
        px = 1 - x if (k >> 2) & 1 else x
        py = 1 - y if (k >> 1) & 1 else y
        pc = 1 - c if k & 1 else c
        out.append(((px, py, pc), 4 * px + 2 * py + pc))
    return out


class _Exchange:
    def __init__(self, scatter, gather):
        self.arrays = list(scatter) + list(gather)
        self.n_sc = len(scatter)
        self.n = len(self.arrays)
        self.out_shape = [jax.ShapeDtypeStruct(a.shape if k < self.n_sc else (N_DEV,) + a.shape, a.dtype)
                          for k, a in enumerate(self.arrays)]
        n_remote = self.n * (N_DEV - 1)
        self.scratch = [pltpu.SemaphoreType.DMA((n_remote,)), pltpu.SemaphoreType.DMA((n_remote,)),
                        pltpu.SemaphoreType.DMA((self.n,))] if self.n else []

    def _copies(self, src, dst, sems):
        send_sems, recv_sems, local_sems = sems
        x, y, c = _mesh_position()
        me = 4 * x + 2 * y + c
        locals_, first, arrivals, passed, last = [], [], [], [], []

        def remote(a, k, src_ref, slot, to):
            s = a * (N_DEV - 1) + k
            return pltpu.make_async_remote_copy(src_ref=src_ref, dst_ref=dst[a].at[slot], send_sem=send_sems.at[s],
                                                recv_sem=recv_sems.at[s], device_id=to, device_id_type=MESH)

        for a in range(self.n):
            if a < self.n_sc:
                locals_.append(pltpu.make_async_copy(src[a].at[me], dst[a].at[me], local_sems.at[a]))
                for k, (peer, pid) in enumerate(_peers(x, y, c)):
                    first.append(remote(a, k, src[a].at[pid], me, peer))
                    last.append(remote(a, k, src[a].at[pid], pid, peer))
                continue
            locals_.append(pltpu.make_async_copy(src[a], dst[a].at[me], local_sems.at[a]))
            sibling, sib_id = (x, y, 1 - c), 4 * x + 2 * y + (1 - c)
            chips = [(1 - x, y), (x, 1 - y), (1 - x, 1 - y)]
            first.append(remote(a, 0, src[a], me, sibling))
            last.append(remote(a, 0, src[a], sib_id, sibling))
            for j, (px, py) in enumerate(chips):
                same, other = 4 * px + 2 * py + c, 4 * px + 2 * py + (1 - c)
                first.append(remote(a, 1 + j, src[a], me, (px, py, c)))
                arrivals.append(remote(a, 1 + j, src[a], same, (px, py, c)))
                passed.append(remote(a, 4 + j, dst[a].at[same], same, sibling))
                last.append(remote(a, 4 + j, dst[a].at[other], other, sibling))
        return locals_, first, arrivals, passed, last

    def start(self, src, dst, sems):
        if not self.n:
            return
        locals_, first, _, _, _ = self._copies(src, dst, sems)
        for cp in locals_ + first:
            cp.start()

    def forward(self, src, dst, sems):
        if self.n == self.n_sc:
            return
        _, _, arrivals, passed, _ = self._copies(src, dst, sems)
        for arrived, cp in zip(arrivals, passed):
            arrived.wait_recv()
            cp.start()

    def wait(self, src, dst, sems):
        if not self.n:
            return
        locals_, first, _, passed, last = self._copies(src, dst, sems)
        for cp in last:
            cp.wait_recv()
        for cp in first + passed:
            cp.wait_send()
        for cp in locals_:
            cp.wait()


def _exchange(name, scatter, gather):
    ex = _Exchange(scatter, gather)
    n = ex.n

    def body(*refs):
        src, dst, sems = refs[:n], refs[n:2 * n], refs[2 * n:]
        ex.start(src, dst, sems)
        ex.forward(src, dst, sems)
        ex.wait(src, dst, sems)

    return pl.pallas_call(
        body, name=name, out_shape=tuple(ex.out_shape), in_specs=[ANY] * n, out_specs=(ANY,) * n,
        scratch_shapes=ex.scratch,
    )(*ex.arrays)


PROJ_ROWS = 512


def _gather_proj(x, win_b, wout_b, cw_b):
    T = x.shape[0]
    chunk = min(PROJ_ROWS, T)
    nq = T // chunk
    ex = _Exchange([], [win_b, wout_b, cw_b])

    def body(x_hbm, win_hbm, wout_hbm, cw_hbm, proj_hbm, win_g, wout_g, cw_g,
             xb, xstage, w_s, out_s, xsem, wsem, osem, send_sems, recv_sems, local_sems):
        locals_, first, arrivals, passed, last = ex._copies(
            (win_hbm, wout_hbm, cw_hbm), (win_g, wout_g, cw_g), (send_sems, recv_sems, local_sems))
        for cp in locals_ + first:
            cp.start()
        x_, y_, c_ = _mesh_position()
        chips = [(1 - x_, y_), (x_, 1 - y_), (1 - x_, 1 - y_)]
        order = ([4 * x_ + 2 * y_ + c_, 4 * x_ + 2 * y_ + (1 - c_)] + [4 * px + 2 * py + c_ for px, py in chips]
                 + [4 * px + 2 * py + (1 - c_) for px, py in chips])

        def x_copy(q, slot):
            return pltpu.make_async_copy(x_hbm.at[pl.ds(q * chunk, chunk)], xstage.at[slot], xsem.at[slot])

        x_copy(0, 0).start()
        writes = [None, None]
        n_out = 0
        for r, blk in enumerate(order):
            if r == 1:
                for a in range(ex.n):
                    last[4 * a].wait_recv()
            elif 2 <= r <= 4:
                for a in range(ex.n):
                    arrivals[3 * a + r - 2].wait_recv()
                    passed[3 * a + r - 2].start()
            elif r >= 5:
                for a in range(ex.n):
                    last[4 * a + r - 4].wait_recv()
            w_copy = pltpu.make_async_copy(win_hbm if r == 0 else win_g.at[blk], w_s, wsem.at[0])
            w_copy.start()
            w_copy.wait()
            for q in range(nq):
                rows = pl.ds(q * chunk, chunk)
                if r == 0:
                    x_copy(q, q % 2).wait()
                    if q + 1 < nq:
                        x_copy(q + 1, (q + 1) % 2).start()
                    xb[rows, :] = xstage[q % 2].astype(BF16)
                slot = n_out % 2
                if writes[slot] is not None:
                    writes[slot].wait()
                out_s[slot] = _dot_nt(xb[rows, :], w_s[...])
                writes[slot] = pltpu.make_async_copy(out_s.at[slot], proj_hbm.at[blk, rows], osem.at[slot])
                writes[slot].start()
                n_out += 1
        for cp in writes:
            cp.wait()
        for cp in first + passed:
            cp.wait_send()
        for cp in locals_:
            cp.wait()

    S = jax.ShapeDtypeStruct
    res = pl.pallas_call(
        body, name="gather_proj",
        out_shape=(S((N_DEV, T, MXU_COLS), F32), *ex.out_shape),
        in_specs=[ANY] * 4, out_specs=(ANY,) * 4,
        scratch_shapes=[pltpu.VMEM((T, D_MODEL), BF16), pltpu.VMEM((2, chunk, D_MODEL), F32),
                        pltpu.VMEM(win_b.shape, BF16), pltpu.VMEM((2, chunk, MXU_COLS), F32),
                        pltpu.SemaphoreType.DMA((2,)), pltpu.SemaphoreType.DMA((1,)), pltpu.SemaphoreType.DMA((2,))]
        + ex.scratch,
        compiler_params=_params(48),
    )(x, win_b, wout_b, cw_b)
    return res[0], res[1:]


def _forward_step(n_steps):
    return (5 * n_steps) // 8


def _hosted(ex, refs, n_in, n_out):
    ins, ex_src = refs[:n_in], refs[n_in:n_in + ex.n]
    rest = refs[n_in + ex.n:]
    outs, ex_dst = rest[:n_out], rest[n_out:n_out + ex.n]
    rest = rest[n_out + ex.n:]
    n_own = len(rest) - len(ex.scratch)
    return ins, outs, rest[:n_own], (ex_src, ex_dst, rest[n_own:])


def _proj_cols(proj_ref, first, count):
    return jnp.concatenate([proj_ref[j] for j in range(first, first + count)], axis=1)


def _fwd_mix(x, proj, wout_g, sgu_g, sgu_b, wcat, bs_full, cw, cb, cg, cbeta, tm, ex):
    T = x.shape[0]
    nt = T // tm

    def body(*refs):
        ins, outs, scratch, ex_refs = _hosted(ex, refs, 11, 5)
        x_ref, proj_ref, wout_ref, sg_ref, sb_ref, wcat_ref, bs_ref, cw_ref, cb_ref, cg_ref, cbeta_ref = ins
        ycat_ref, n1_ref, rstd1_ref, phi_ref, y_ref = outs
        hpad, hshift = scratch
        i = pl.program_id(0)

        @pl.when(i == 0)
        def _():
            ex.start(*ex_refs)

        xf = x_ref[...]
        pre = _proj_cols(proj_ref, 0, 4)
        cdf = _normal_cdf(pre)
        phi_ref[...] = cdf.astype(BF16)
        u = pre[:, 0:D_SGU] * cdf[:, 0:D_SGU]
        v = pre[:, D_SGU:2 * D_SGU] * cdf[:, D_SGU:2 * D_SGU]
        vhat, _ = _ln_fwd(v)
        vn = vhat * sg_ref[...] + sb_ref[...]
        lo = _lo_mask()
        for c in range(tm // CHUNK):
            rows = slice(CHUNK * c, CHUNK * (c + 1))
            for p in range(4):
                lanes = slice(CHUNK * p, CHUNK * (p + 1))
                mixed = _dot(wcat_ref[p], _head_pair_stack(vn[rows, lanes], lo)) + bs_ref[:, lanes]
                ycat_ref[rows, lanes] = (u[rows, lanes] * mixed).astype(BF16)
        a = _proj_cols(proj_ref, 4, 2)
        g = _proj_cols(proj_ref, 6, 2)

        @pl.when(i == 0)
        def _():
            hpad[0:HALO, :] = jnp.zeros((HALO, D_CONV), F32)

        hpad[HALO:HALO + tm, :] = a * jax.nn.sigmoid(g)
        _shifted_copies(hpad, hshift, tm + SHIFT_ROWS)
        _causal_conv(hpad, hshift, cw_ref, y_ref, tm, lambda k: HALO - (CONV_WIDTH - 1) + k, bias=cb_ref[...])
        hpad[0:HALO, :] = hpad[tm:tm + HALO, :]
        yhat, _ = _ln_fwd(y_ref[...])
        yn = yhat * cg_ref[...] + cbeta_ref[...]
        ycat_ref[:, D_SGU:D_SGU + D_CONV] = (yn * jax.nn.sigmoid(yn)).astype(BF16)
        r1 = ALPHA * xf + _dot(ycat_ref[...], wout_ref[...])
        n1, rstd1 = _ln_fwd(r1)
        n1_ref[...] = n1
        rstd1_ref[...] = rstd1

        @pl.when(i == _forward_step(nt))
        def _():
            ex.forward(*ex_refs)

        @pl.when(i == nt - 1)
        def _():
            ex.wait(*ex_refs)

    S = jax.ShapeDtypeStruct
    row = lambda w: pl.BlockSpec((tm, w), lambda i: (i, 0))
    res = pl.pallas_call(
        body, name="fwd_mix", grid=(nt,),
        in_specs=[row(D_MODEL), pl.BlockSpec((N_DEV, tm, MXU_COLS), lambda i: (0, i, 0)), _full(wout_g.shape),
                  _full(sgu_g.shape), _full(sgu_b.shape), _full(wcat.shape), _full(bs_full.shape), _full(cw.shape),
                  _full(cb.shape), _full(cg.shape), _full(cbeta.shape)] + [ANY] * ex.n,
        out_specs=(row(D_MODEL), row(D_MODEL), row(1), row(2 * D_SGU), row(D_CONV)) + (ANY,) * ex.n,
        out_shape=(S((T, D_MODEL), BF16), S((T, D_MODEL), F32), S((T, 1), F32),
                   S((T, 2 * D_SGU), BF16), S((T, D_CONV), F32), *ex.out_shape),
        scratch_shapes=[pltpu.VMEM((tm + HALO, D_CONV), F32),
                        pltpu.VMEM((SUBLANES - 1, tm + SHIFT_ROWS, D_CONV), F32)] + ex.scratch,
        compiler_params=_params(56, 1),
    )(x, proj, wout_g, sgu_g, sgu_b, wcat, bs_full, cw, cb, cg, cbeta, *ex.arrays)
    return res[:5], res[5:]


def _load_resident(pairs, sems):
    cps = [pltpu.make_async_copy(s, d, sems.at[k]) for k, (s, d) in enumerate(pairs)]
    for cp in cps:
        cp.start()
    for cp in cps:
        cp.wait()


def _fwd_mlp(n1, tgt, l1g, l1b, l2g, l2b, wgt, wut, wd, tm):
    T = n1.shape[0]
    nt = T // tm
    nf = D_FF // MXU_COLS

    def body(n1_ref, tgt_ref, l1g_ref, l1b_ref, l2g_ref, l2b_ref, wg_hbm, wu_hbm, wd_hbm,
             gate_ref, up_ref, hh_ref, dr2_ref, stat_ref, wg_s, wu_s, wd_s, sems):
        i = pl.program_id(0)

        @pl.when(i == 0)
        def _():
            _load_resident([(wg_hbm, wg_s), (wu_hbm, wu_s), (wd_hbm, wd_s)], sems)
            stat_ref[...] = jnp.zeros(stat_ref.shape, F32)

        x1 = n1_ref[...] * l1g_ref[...] + l1b_ref[...]
        x1b = x1.astype(BF16)
        for f in range(nf):
            cols = slice(MXU_COLS * f, MXU_COLS * (f + 1))
            gt = _dot_nt(x1b, wg_s[cols, :])
            ut = _dot_nt(x1b, wu_s[cols, :])
            gate_ref[:, cols] = gt.astype(BF16)
            up_ref[:, cols] = ut.astype(BF16)
            hh_ref[:, cols] = (gt * jax.nn.sigmoid(gt) * ut).astype(BF16)
        r2 = ALPHA * x1 + _dot(hh_ref[...], wd_s[...])
        n2, rstd2 = _ln_fwd(r2)
        x2 = n2 * l2g_ref[...] + l2b_ref[...]
        diff = x2 - tgt_ref[...]
        dx2 = diff * (1.0 / D_MODEL)
        stat_ref[0:1, :] += _colsum(diff * diff)
        stat_ref[1:2, :] += _colsum(dx2 * n2)
        stat_ref[2:3, :] += _colsum(dx2)
        dr2_ref[...] = _ln_bwd(dx2 * l2g_ref[...], n2, rstd2)

    S = jax.ShapeDtypeStruct
    row = lambda w: pl.BlockSpec((tm, w), lambda i: (i, 0))
    vec = _full((1, D_MODEL))
    return pl.pallas_call(
        body, name="fwd_mlp", grid=(nt,),
        in_specs=[row(D_MODEL), row(D_MODEL), vec, vec, vec, vec, ANY, ANY, ANY],
        out_specs=(row(D_FF), row(D_FF), row(D_FF), row(D_MODEL), _full((8, D_MODEL))),
        out_shape=(S((T, D_FF), BF16), S((T, D_FF), BF16), S((T, D_FF), BF16), S((T, D_MODEL), F32),
                   S((8, D_MODEL), F32)),
        scratch_shapes=[pltpu.VMEM((D_FF, D_MODEL), BF16)] * 3 + [pltpu.SemaphoreType.DMA((3,))],
        compiler_params=_params(56, 1),
    )(n1, tgt, l1g, l1b, l2g, l2b, wgt, wut, wd)


def _bwd_mlp(dr2, gate, up, n1, rstd1, l1g, wgt, wut, wd, tm):
    T = n1.shape[0]
    nt = T // tm
    nf = D_FF // MXU_COLS

    def body(dr2_ref, gate_ref, up_ref, n1_ref, rstd1_ref, l1g_ref, wg_hbm, wu_hbm, wd_hbm,
             dgate_ref, dup_ref, dr1_ref, stat_ref, wg_s, wu_s, wd_s, sems):
        i = pl.program_id(0)

        @pl.when(i == 0)
        def _():
            _load_resident([(wg_hbm, wg_s), (wu_hbm, wu_s), (wd_hbm, wd_s)], sems)
            stat_ref[...] = jnp.zeros(stat_ref.shape, F32)

        dr2 = dr2_ref[...]
        dr2b = dr2.astype(BF16)
        for f in range(nf):
            cols = slice(MXU_COLS * f, MXU_COLS * (f + 1))
            dhh = _dot_nt(dr2b, wd_s[cols, :])
            gt = gate_ref[:, cols].astype(F32)
            ut = up_ref[:, cols].astype(F32)
            sg = jax.nn.sigmoid(gt)
            dgate_ref[:, cols] = (dhh * ut * (sg * (1.0 + gt * (1.0 - sg)))).astype(BF16)
            dup_ref[:, cols] = (dhh * (gt * sg)).astype(BF16)
        dx1 = ALPHA * dr2 + _dot(dgate_ref[...], wg_s[...]) + _dot(dup_ref[...], wu_s[...])
        n1 = n1_ref[...]
        stat_ref[0:1, :] += _colsum(dx1 * n1)
        stat_ref[1:2, :] += _colsum(dx1)
        dr1_ref[...] = _ln_bwd(dx1 * l1g_ref[...], n1, rstd1_ref[...])

    S = jax.ShapeDtypeStruct
    row = lambda w: pl.BlockSpec((tm, w), lambda i: (i, 0))
    return pl.pallas_call(
        body, name="bwd_mlp", grid=(nt,),
        in_specs=[row(D_MODEL), row(D_FF), row(D_FF), row(D_MODEL), row(1), _full((1, D_MODEL)), ANY, ANY, ANY],
        out_specs=(row(D_FF), row(D_FF), row(D_MODEL), _full((8, D_MODEL))),
        out_shape=(S((T, D_FF), BF16), S((T, D_FF), BF16), S((T, D_MODEL), F32), S((8, D_MODEL), F32)),
        scratch_shapes=[pltpu.VMEM((D_FF, D_MODEL), BF16)] * 3 + [pltpu.SemaphoreType.DMA((3,))],
        compiler_params=_params(56, 1),
    )(dr2, gate, up, n1, rstd1, l1g, wgt, wut, wd)


def _bwd_mix(dr1, proj, phi, y, win_g, wout_g, sgu_g, sgu_b, wcat, wcatt, bs_full, cw, cg, cbeta, tm, ex):
    T = dr1.shape[0]
    nt = T // tm
    halo_blocks = tm // HALO

    def body(*refs):
        ins, outs, scratch, ex_refs = _hosted(ex, refs, 15, 6)
        (dr1_ref, proj_ref, halo_ref, phi_ref, y_ref, win_ref, wout_ref, sg_ref, sb_ref, wcat_ref, wcatt_ref, bs_ref,
         cw_ref, cg_ref, cbeta_ref) = ins
        gx_ref, dproj_ref, gws_out, gbs_out, gcw_ref, vec_ref = outs
        hpad, shift, dypad, dhbuf, dubuf, dvnbuf, gcw_acc, gws_ref, gbs_ref = scratch
        i = pl.program_id(0)
        tile = nt - 1 - i

        @pl.when(i == 0)
        def _():
            ex.start(*ex_refs)
            gws_ref[...] = jnp.zeros(gws_ref.shape, F32)
            gbs_ref[...] = jnp.zeros(gbs_ref.shape, F32)
            gcw_ref[...] = jnp.zeros(gcw_ref.shape, F32)
            vec_ref[...] = jnp.zeros(vec_ref.shape, F32)
            gcw_acc[...] = jnp.zeros(gcw_acc.shape, F32)
            dypad[tm:tm + HALO, :] = jnp.zeros((HALO, D_CONV), F32)

        dr1 = dr1_ref[...]
        dycat = _dot_nt(dr1.astype(BF16), wout_ref[...])
        pu = _proj_cols(proj_ref, 0, 2)
        pv = _proj_cols(proj_ref, 2, 2)
        cdf_u = phi_ref[:, 0:D_SGU].astype(F32)
        cdf_v = phi_ref[:, D_SGU:2 * D_SGU].astype(F32)
        u = pu * cdf_u
        vhat, rstd_v = _ln_fwd(pv * cdf_v)
        vn = vhat * sg_ref[...] + sb_ref[...]
        lo = _lo_mask()
        for c in range(tm // CHUNK):
            rows = slice(CHUNK * c, CHUNK * (c + 1))
            for p in range(4):
                lanes = slice(CHUNK * p, CHUNK * (p + 1))
                vstack = _head_pair_stack(vn[rows, lanes], lo)
                mixed = _dot(wcat_ref[p], vstack) + bs_ref[:, lanes]
                d_a = dycat[rows, lanes]
                dubuf[rows, lanes] = d_a * mixed
                dm = d_a * u[rows, lanes]
                gbs_ref[:, lanes] += dm
                dstack = _head_pair_stack(dm, lo)
                gws_ref[2 * CHUNK * p:2 * CHUNK * (p + 1), :] += _dot_nt(dstack, vn[rows, lanes].astype(BF16))
                dvnbuf[rows, lanes] = _dot(wcatt_ref[p], dstack)
        dvn = dvnbuf[...]
        vec_ref[0:1, :] += _colsum(dvn * vhat)
        vec_ref[1:2, :] += _colsum(dvn)
        dv = _ln_bwd(dvn * sg_ref[...], vhat, rstd_v)
        dproj_ref[:, 0:D_SGU] = (dubuf[...] * _gelu_grad(pu, cdf_u)).astype(BF16)
        dproj_ref[:, D_SGU:2 * D_SGU] = (dv * _gelu_grad(pv, cdf_v)).astype(BF16)
        base = 2 * D_SGU
        a = _proj_cols(proj_ref, 4, 2)
        sgm = jax.nn.sigmoid(_proj_cols(proj_ref, 6, 2))
        h_before = _proj_cols(halo_ref, 0, 2) * jax.nn.sigmoid(_proj_cols(halo_ref, 2, 2))
        hpad[0:HALO, :] = jnp.where(tile > 0, h_before, 0.0)
        hpad[HALO:HALO + tm, :] = a * sgm
        _shifted_copies(hpad, shift, tm + SHIFT_ROWS)
        h_offset = lambda k: HALO - (CONV_WIDTH - 1) + k
        yhat, rstd_y = _ln_fwd(y_ref[...])
        yn = yhat * cg_ref[...] + cbeta_ref[...]
        s = jax.nn.sigmoid(yn)
        dyn = dycat[:, D_SGU:D_SGU + D_CONV] * (s * (1.0 + yn * (1.0 - s)))
        vec_ref[3:4, :] += _colsum(dyn * yhat)
        vec_ref[4:5, :] += _colsum(dyn)
        dy = _ln_bwd(dyn * cg_ref[...], yhat, rstd_y)
        vec_ref[2:3, :] += _colsum(dy)
        dypad[0:tm, :] = dy
        _conv_weight_grad(dypad, hpad, shift, gcw_acc, tm, h_offset)
        _shifted_copies(dypad, shift, tm + SHIFT_ROWS)
        _causal_conv(dypad, shift, cw_ref, dhbuf, tm, lambda k: (CONV_WIDTH - 1) - k)
        dypad[tm:tm + HALO, :] = dypad[0:HALO, :]
        dh = dhbuf[...]
        dproj_ref[:, base:base + D_CONV] = (dh * sgm).astype(BF16)
        dproj_ref[:, base + D_CONV:base + 2 * D_CONV] = (dh * a * sgm * (1.0 - sgm)).astype(BF16)
        gx_ref[...] = ALPHA * dr1 + _dot(dproj_ref[...], win_ref[...])

        @pl.when(i == _forward_step(nt))
        def _():
            ex.forward(*ex_refs)

        @pl.when(i == nt - 1)
        def _():
            gcw_ref[...] = gcw_acc[...].sum(axis=1)
            gws_out[...] = gws_ref[...].astype(BF16)
            gbs_out[...] = lax.dot_general(_head_selector(), gbs_ref[...], (((1,), (1,)), ((), ())),
                                           preferred_element_type=F32, precision=lax.Precision.HIGHEST)
            ex.wait(*ex_refs)

    S = jax.ShapeDtypeStruct
    row = lambda w: pl.BlockSpec((tm, w), lambda i: (nt - 1 - i, 0))
    proj_tile = pl.BlockSpec((N_DEV, tm, MXU_COLS), lambda i: (0, nt - 1 - i, 0))
    halo = pl.BlockSpec((N_DEV // 2, HALO, MXU_COLS),
                        lambda i: (1, jnp.maximum((nt - 1 - i) * halo_blocks - 1, 0), 0))
    res = pl.pallas_call(
        body, name="bwd_mix", grid=(nt,),
        in_specs=[row(D_MODEL), proj_tile, halo, row(2 * D_SGU), row(D_CONV), _full(win_g.shape),
                  _full(wout_g.shape), _full(sgu_g.shape), _full(sgu_b.shape), _full(wcat.shape), _full(wcatt.shape),
                  _full(bs_full.shape), _full(cw.shape), _full(cg.shape), _full(cbeta.shape)]
        + [ANY] * ex.n,
        out_specs=(row(D_MODEL), row(2 * D_MODEL), _full((N_HEADS * CHUNK, CHUNK)), _full((N_HEADS, CHUNK)),
                   _full((CONV_ROWS, D_CONV)), _full((8, D_CONV))) + (ANY,) * ex.n,
        out_shape=(S((T, D_MODEL), F32), S((T, 2 * D_MODEL), BF16), S((N_HEADS * CHUNK, CHUNK), BF16),
                   S((N_HEADS, CHUNK), F32), S((CONV_ROWS, D_CONV), F32), S((8, D_CONV), F32), *ex.out_shape),
        scratch_shapes=[pltpu.VMEM((tm + HALO, D_CONV), F32), pltpu.VMEM((SUBLANES - 1, tm + SHIFT_ROWS, D_CONV), F32),
                        pltpu.VMEM((tm + HALO, D_CONV), F32),
                        pltpu.VMEM((tm, D_CONV), F32), pltpu.VMEM((tm, D_SGU), F32),
                        pltpu.VMEM((tm, D_SGU), F32), pltpu.VMEM((CONV_ROWS, 8, D_CONV), F32),
                        pltpu.VMEM((N_HEADS * CHUNK, CHUNK), F32), pltpu.VMEM((CHUNK, D_SGU), F32)] + ex.scratch,
        compiler_params=_params(56, 1),
    )(dr1, proj, proj, phi, y, win_g, wout_g, sgu_g, sgu_b, wcat, wcatt, bs_full, cw, cg, cbeta, *ex.arrays)
    return res[:6], res[6:]


def _wgrad(name, a, b, blocks, tk, ex=None, b_cols=None, b_affine=None):
    T, M = a.shape
    col, N = b_cols or (0, b.shape[1])
    nk = T // tk
    out_shape = (blocks, M // blocks, N)
    ex = ex or _Exchange([], [])
    affine = list(b_affine or [])

    def body(*refs):
        ins, (o_ref,), (acc,), ex_refs = _hosted(ex, refs, 2 + len(affine), 1)
        a_ref, b_ref = ins[:2]
        i = pl.program_id(0)

        @pl.when(i == 0)
        def _():
            ex.start(*ex_refs)
            acc[...] = jnp.zeros(acc.shape, F32)

        right = b_ref[...]
        if affine:
            right = right * ins[2][...] + ins[3][...]
        acc[...] += _dot_tn(a_ref[...].astype(BF16), right.astype(BF16))

        @pl.when(i == _forward_step(nk))
        def _():
            ex.forward(*ex_refs)

        @pl.when(i == nk - 1)
        def _():
            o_ref[...] = acc[...].astype(BF16)
            ex.wait(*ex_refs)

    res = pl.pallas_call(
        body, name=name, grid=(nk,),
        in_specs=[pl.BlockSpec((tk, M), lambda i: (i, 0)), pl.BlockSpec((tk, N), lambda i: (i, col))]
        + [_full((1, N))] * len(affine) + [ANY] * ex.n,
        out_specs=(_full((M, N)),) + (ANY,) * ex.n,
        out_shape=(jax.ShapeDtypeStruct((M, N), BF16), *ex.out_shape),
        scratch_shapes=[pltpu.VMEM((M, N), F32)] + ex.scratch,
        compiler_params=_params(56, 1),
    )(a, b, *affine, *ex.arrays)
    g = res[0].reshape(out_shape)
    return (g, res[1:]) if ex.n else g


def _adamw(w, g, m, v):
    m2 = ADAM_B1 * m + (1.0 - ADAM_B1) * g
    v2 = ADAM_B2 * v + (1.0 - ADAM_B2) * (g * g)
    m_hat = m2 / (1.0 - ADAM_B1 ** ADAM_STEP)
    v_hat = v2 / (1.0 - ADAM_B2 ** ADAM_STEP)
    delta = -ADAM_LR * (m_hat / (jnp.sqrt(v_hat) + ADAM_EPS) + ADAM_WD * w)
    return delta, m2, v2


def _sum_partials(r_ref):
    g = r_ref[0].astype(F32)
    for s in range(1, N_DEV):
        g = g + r_ref[s].astype(F32)
    return g


def _staged_call(name, groups, vmem_mib, ex=None):
    ex = ex or _Exchange([], [])
    inputs = [a for ins, _, _ in groups for a in ins]
    out_shapes = [s for _, outs, _ in groups for s in outs]
    n_in, n_out = len(inputs), len(out_shapes)

    def body(*refs):
        ins, outs, scratch, ex_refs = _hosted(ex, refs, n_in, n_out)
        in_bufs, out_bufs, sems = scratch[:n_in], scratch[n_in:n_in + n_out], scratch[n_in + n_out]
        ex.start(*ex_refs)
        loads = [pltpu.make_async_copy(ins[k], in_bufs[k], sems.at[k]) for k in range(n_in)]
        stores = [pltpu.make_async_copy(out_bufs[k], outs[k], sems.at[n_in + k]) for k in range(n_out)]
        for cp in loads:
            cp.start()
        i0 = o0 = 0
        for g_ins, g_outs, compute in groups:
            i1, o1 = i0 + len(g_ins), o0 + len(g_outs)
            for cp in loads[i0:i1]:
                cp.wait()
            compute(in_bufs[i0:i1], out_bufs[o0:o1])
            for cp in stores[o0:o1]:
                cp.start()
            i0, o0 = i1, o1
        for cp in stores:
            cp.wait()
        ex.forward(*ex_refs)
        ex.wait(*ex_refs)

    scratch = ([pltpu.VMEM(a.shape, a.dtype) for a in inputs] + [pltpu.VMEM(s.shape, s.dtype) for s in out_shapes]
               + [pltpu.SemaphoreType.DMA((n_in + n_out,))] + ex.scratch)
    res = pl.pallas_call(
        body, name=name, out_shape=(*[pltpu.HBM(s.shape, s.dtype) for s in out_shapes], *ex.out_shape),
        in_specs=[HBM] * n_in + [ANY] * ex.n, out_specs=(HBM,) * n_out + (ANY,) * ex.n,
        scratch_shapes=scratch, compiler_params=_params(vmem_mib),
    )(*[pltpu.with_memory_space_constraint(a, pltpu.HBM) for a in inputs], *ex.arrays)
    per_group, o0 = [], 0
    for _, g_outs, _ in groups:
        per_group.append(list(res[o0:o0 + len(g_outs)]))
        o0 += len(g_outs)
    return per_group, res[n_out:]


def _adamw_shard_group(parts, w, m, v, transposed):
    n = len(parts)

    def compute(ins, outs):
        w_ref, m_ref, v_ref = ins[n:]
        lo = 0
        for r_ref in ins[:n]:
            g = _sum_partials(r_ref)
            cols = g.shape[1]
            if transposed:
                g, at = g.T, (slice(lo, lo + cols), slice(None))
            else:
                at = (slice(None), slice(lo, lo + cols))
            delta, m2, v2 = _adamw(w_ref[at], g, m_ref[at], v_ref[at])
            for o, val in zip(outs, (g, delta, m2, v2)):
                o[at] = val
            lo += cols

    return [*parts, w, m, v], [jax.ShapeDtypeStruct(w.shape, F32)] * 4, compute


SMALL_NAMES = ["sgu_ln_g", "sgu_ln_b", "w_s", "b_s", "conv_b", "conv_ln_g", "conv_ln_b", "ln1_g", "ln1_b", "ln2_g", "ln2_b",
               "conv_w"]


def _finish_small_group(gws8, gbs8, gcw8, vmix8, vmlp8, vout8, small):
    names = SMALL_NAMES
    flat = []
    for n in names:
        flat += list(small[n])
    cw_block = D_CONV // N_DEV

    def compute(ins, outs):
        gws_ref, gbs_ref, gcw_ref, vmix_ref, vmlp_ref, vout_ref = ins[:6]
        wmv = ins[6:]
        loss_o = outs[0]
        outs = outs[1:]
        gws = _sum_partials(gws_ref)
        gbs = _sum_partials(gbs_ref)
        vmix = _sum_partials(vmix_ref)
        vmlp = _sum_partials(vmlp_ref)
        vout = _sum_partials(vout_ref)
        x, y, c = _mesh_position()
        first = (4 * x + 2 * y + c) * cw_block
        pick = (lax.broadcasted_iota(jnp.int32, (D_CONV, cw_block), 0)
                == first + lax.broadcasted_iota(jnp.int32, (D_CONV, cw_block), 1)).astype(F32)
        gcw = jnp.dot(_sum_partials(gcw_ref), pick, preferred_element_type=F32,
                      precision=lax.Precision.HIGHEST)[0:CONV_WIDTH, :]
        loss = (0.5 / D_MODEL) * jnp.sum(vout[0:1, :], axis=1, keepdims=True)
        loss_o[...] = jnp.broadcast_to(loss, loss_o.shape)
        rows = lax.broadcasted_iota(jnp.int32, (N_HEADS * CHUNK, CHUNK), 0)
        cols = lax.broadcasted_iota(jnp.int32, (N_HEADS * CHUNK, CHUNK), 1)
        gws = jnp.where((rows & (CHUNK - 1)) >= cols, gws, 0.0)
        grads = {
            "sgu_ln_g": vmix[0:1, :], "sgu_ln_b": vmix[1:2, :], "w_s": gws, "b_s": gbs,
            "conv_b": vmix[2:3, :], "conv_ln_g": vmix[3:4, :], "conv_ln_b": vmix[4:5, :],
            "ln1_g": vmlp[0:1, :], "ln1_b": vmlp[1:2, :], "ln2_g": vout[1:2, :], "ln2_b": vout[2:3, :],
            "conv_w": gcw,
        }
        for k, n in enumerate(names):
            w_ref, m_ref, v_ref = wmv[3 * k:3 * k + 3]
            g = grads[n]
            delta, m2, v2 = _adamw(w_ref[...], g, m_ref[...], v_ref[...])
            outs[4 * k][...] = g
            outs[4 * k + 1][...] = delta
            outs[4 * k + 2][...] = m2
            outs[4 * k + 3][...] = v2

    S = jax.ShapeDtypeStruct
    out_shape = [S((SUBLANES, 128), F32)]
    for n in names:
        out_shape += [S(small[n][0].shape, F32)] * 4
    return [gws8, gbs8, gcw8, vmix8, vmlp8, vout8, *flat], out_shape, compute


TOKEN_TILE_FWD_MIX = 512
TOKEN_TILE_BWD_MIX = 512
TOKEN_TILE_FWD_MLP = 512
TOKEN_TILE_BWD_MLP = 512
TOKEN_TILE_WGRAD = 1024


def kernel(x, w_in, sgu_ln_g, sgu_ln_b, w_s, b_s, conv_w, conv_b, conv_ln_g, conv_ln_b, w_out, ln1_g, ln1_b, w_gate, w_up, w_down, ln2_g, ln2_b, loss_target, m_w_in, m_sgu_ln_g, m_sgu_ln_b, m_w_s, m_b_s, m_conv_w, m_conv_b, m_conv_ln_g, m_conv_ln_b, m_w_out, m_ln1_g, m_ln1_b, m_w_gate, m_w_up, m_w_down, m_ln2_g, m_ln2_b, v_w_in, v_sgu_ln_g, v_sgu_ln_b, v_w_s, v_b_s, v_conv_w, v_conv_b, v_conv_ln_g, v_conv_ln_b, v_w_out, v_ln1_g, v_ln1_b, v_w_gate, v_w_up, v_w_down, v_ln2_g, v_ln2_b):
    xs = x[0]
    tgt = loss_target[0]

    (win_b, wout_b, wgt_b, wut_b, wd_b, cw_b, wcat, wcatt, bs_full) = _prep_weights(
        w_in[0], w_out[0], w_gate[0].T, w_up[0].T, w_down[0], conv_w[0], w_s[0], b_s[0])
    proj, (win_g, wout_g, cw_g) = _gather_proj(xs, win_b, wout_b, cw_b)
    win_g = win_g.reshape(2 * D_MODEL, D_MODEL)
    wout_g = wout_g.reshape(D_MODEL, D_MODEL)
    cw = jnp.transpose(cw_g[:, :, :D_CONV // N_DEV], (1, 0, 2)).reshape(CONV_ROWS, D_CONV)

    (ycat, n1, rstd1, phi, y_conv), (wgt_g, wut_g, wd_g) = _fwd_mix(
        xs, proj, wout_g, sgu_ln_g, sgu_ln_b, wcat, bs_full, cw, conv_b, conv_ln_g, conv_ln_b, TOKEN_TILE_FWD_MIX,
        _Exchange([], [wgt_b, wut_b, wd_b]))
    wgt_g = wgt_g.reshape(D_FF, D_MODEL)
    wut_g = wut_g.reshape(D_FF, D_MODEL)
    wd_g = wd_g.reshape(D_FF, D_MODEL)
    gate, up, hh, dr2, vout = _fwd_mlp(n1, tgt, ln1_g, ln1_b, ln2_g, ln2_b, wgt_g, wut_g, wd_g, TOKEN_TILE_FWD_MLP)

    dgate, dup, dr1, vmlp = _bwd_mlp(dr2, gate, up, n1, rstd1, ln1_g, wgt_g, wut_g, wd_g, TOKEN_TILE_BWD_MLP)
    tk = TOKEN_TILE_WGRAD
    x1 = dict(b_affine=(ln1_g, ln1_b))
    g_wgt = _wgrad("wgrad_gate", dgate, n1, N_DEV, tk, **x1)
    g_wut = _wgrad("wgrad_up", dup, n1, N_DEV, tk, **x1)
    g_wd = _wgrad("wgrad_down", hh, dr2, N_DEV, tk)
    g_wout = _wgrad("wgrad_out", ycat, dr1, N_DEV, tk)
    (gx, dproj, gws, gbs, gcw, vmix), (r_wgt, r_wut, r_wd, r_wout) = _bwd_mix(
        dr1, proj, phi, y_conv, win_g, wout_g, sgu_ln_g, sgu_ln_b, wcat, wcatt, bs_full, cw, conv_ln_g, conv_ln_b,
        TOKEN_TILE_BWD_MIX, _Exchange([g_wgt, g_wut, g_wd, g_wout], []))
    half = D_MODEL // 2
    g_win_a, (gws8, gbs8, gcw8, vmix8, vmlp8, vout8) = _wgrad(
        "wgrad_in_a", dproj, xs, N_DEV, tk, _Exchange([], [gws, gbs, gcw, vmix, vmlp, vout]), b_cols=(0, half))
    g_win_b, (r_win_a,) = _wgrad("wgrad_in_b", dproj, xs, N_DEV, tk, _Exchange([g_win_a], []), b_cols=(1, half))
    (r_win_b,) = _exchange("exchange_grad_in", [g_win_b], [])
    (u_gate, u_up, u_down), _ = _staged_call(
        "adamw_mlp",
        [_adamw_shard_group([r_wgt], w_gate[0].T, m_w_gate[0].T, v_w_gate[0].T, False),
         _adamw_shard_group([r_wut], w_up[0].T, m_w_up[0].T, v_w_up[0].T, False),
         _adamw_shard_group([r_wd], w_down[0], m_w_down[0], v_w_down[0], False)], 56)
    small_in = {
        "conv_w": (conv_w[0], m_conv_w[0], v_conv_w[0]),
        "sgu_ln_g": (sgu_ln_g, m_sgu_ln_g, v_sgu_ln_g), "sgu_ln_b": (sgu_ln_b, m_sgu_ln_b, v_sgu_ln_b),
        "w_s": tuple(a.reshape(N_HEADS * CHUNK, CHUNK) for a in (w_s, m_w_s, v_w_s)),
        "b_s": (b_s[0], m_b_s[0], v_b_s[0]),
        "conv_b": (conv_b, m_conv_b, v_conv_b), "conv_ln_g": (conv_ln_g, m_conv_ln_g, v_conv_ln_g),
        "conv_ln_b": (conv_ln_b, m_conv_ln_b, v_conv_ln_b),
        "ln1_g": (ln1_g, m_ln1_g, v_ln1_g), "ln1_b": (ln1_b, m_ln1_b, v_ln1_b),
        "ln2_g": (ln2_g, m_ln2_g, v_ln2_g), "ln2_b": (ln2_b, m_ln2_b, v_ln2_b),
    }
    (u_in, u_out, fin), _ = _staged_call(
        "adamw_mix_small",
        [_adamw_shard_group([r_win_a, r_win_b], w_in[0], m_w_in[0], v_w_in[0], True),
         _adamw_shard_group([r_wout], w_out[0], m_w_out[0], v_w_out[0], False),
         _finish_small_group(gws8, gbs8, gcw8, vmix8, vmlp8, vout8, small_in)], 40)
    big = {"w_in": u_in, "w_out": u_out, "w_gate": u_gate, "w_up": u_up, "w_down": u_down}
    loss11 = fin[0]
    small = {n: fin[1 + 4 * k:5 + 4 * k] for k, n in enumerate(SMALL_NAMES)}

    shapes = {"w_s": w_s.shape, "b_s": b_s.shape, "conv_w": conv_w.shape}
    out = {}
    for n, r in big.items():
        out[n] = tuple((a.T if n in ("w_gate", "w_up") else a)[None] for a in r)
    for n, r in small.items():
        out[n] = tuple(a.reshape(shapes[n]) for a in r) if n in shapes else tuple(r)

    order = ["w_in", "sgu_ln_g", "sgu_ln_b", "w_s", "b_s", "conv_w", "conv_b", "conv_ln_g", "conv_ln_b", "w_out",
             "ln1_g", "ln1_b", "w_gate", "w_up", "w_down", "ln2_g", "ln2_b"]
    loss = loss11[0, 0]
    return (loss, gx[None], *[out[n][0] for n in order], *[out[n][1] for n in order],
            *[out[n][2] for n in order], *[out[n][3] for n in order])
```

```python
import jax
import jax.numpy as jnp
from jax import lax
from jax.experimental import pallas as pl
from jax.experimental.pallas import tpu as pltpu

F32 = jnp.float32
BF16 = jnp.bfloat16

D_MODEL = 1024
D_SGU = 512
D_CONV = 512
N_HEADS = 8
CHUNK = 128
CONV_WIDTH = 31
CONV_ROWS = 32
HALO = 32
D_FF = 2816
N_DEV = 8
FF_SHARD = D_FF // N_DEV
ALPHA = (2.0 * 1) ** 0.25
LN_EPS = 1e-5
INV_SQRT2 = 0.7071067811865476
INV_SQRT_2PI = 0.3989422804014327

ADAM_LR = 0.001
ADAM_B1 = 0.9
ADAM_B2 = 0.999
ADAM_EPS = 1e-08
ADAM_WD = 0.01
ADAM_STEP = 10

MXU_COLS = 256
SUBLANES = 8
CONV_ROW_BLOCK = 32
WGRAD_ROW_BLOCK = 32
SHIFT_ROWS = HALO - SUBLANES
MIB = 1024 * 1024

HBM = pl.BlockSpec(memory_space=pltpu.HBM)
ANY = pl.BlockSpec(memory_space=pl.ANY)
MESH = pl.DeviceIdType.MESH


def _params(vmem_mib, grid_dims=0):
    kw = dict(vmem_limit_bytes=vmem_mib * MIB)
    if grid_dims:
        kw["dimension_semantics"] = ("arbitrary",) * grid_dims
    return pltpu.CompilerParams(**kw)


def _full(shape):
    return pl.BlockSpec(shape, lambda i: (0,) * len(shape))


def _dot(a, b):
    return jnp.dot(a, b, preferred_element_type=F32)


def _dot_nt(a, b):
    return lax.dot_general(a, b, (((1,), (1,)), ((), ())), preferred_element_type=F32)


def _dot_tn(a, b):
    return lax.dot_general(a, b, (((0,), (0,)), ((), ())), preferred_element_type=F32)


def _normal_cdf(x):
    return 0.5 * (1.0 + lax.erf(x * INV_SQRT2))


def _gelu_grad(x, cdf):
    return cdf + x * jnp.exp(-0.5 * x * x) * INV_SQRT_2PI


def _ln_fwd(v):
    mu = jnp.mean(v, axis=-1, keepdims=True)
    d = v - mu
    var = jnp.mean(d * d, axis=-1, keepdims=True)
    rstd = lax.rsqrt(var + LN_EPS)
    return d * rstd, rstd


def _ln_bwd(dyhat, yhat, rstd):
    m1 = jnp.mean(dyhat, axis=-1, keepdims=True)
    m2 = jnp.mean(dyhat * yhat, axis=-1, keepdims=True)
    return rstd * (dyhat - m1 - yhat * m2)


def _colsum(v):
    return jnp.sum(v, axis=0, keepdims=True)


def _head_pair_stack(v, lo):
    return jnp.concatenate([jnp.where(lo, v, 0.0), jnp.where(lo, 0.0, v)], axis=0).astype(BF16)


def _lo_mask():
    return lax.broadcasted_iota(jnp.int32, (CHUNK, CHUNK), 1) < (CHUNK // 2)


def _head_selector():
    head = lax.broadcasted_iota(jnp.int32, (N_HEADS, D_SGU), 0)
    lane = lax.broadcasted_iota(jnp.int32, (N_HEADS, D_SGU), 1)
    width = D_SGU // N_HEADS
    return ((lane >= head * width) & (lane < (head + 1) * width)).astype(F32)


def _shifted_copies(pad_ref, sh_ref, rows):
    for r in range(1, SUBLANES):
        sh_ref[r - 1, 0:rows, :] = pad_ref[pl.ds(r, rows), :]


def _tap_groups(offset_of_tap):
    groups = {}
    for k in range(CONV_WIDTH):
        o = offset_of_tap(k)
        groups.setdefault(o % SUBLANES, []).append((k, o // SUBLANES))
    return groups


def _tap_window(pad_ref, sh_ref, r, taps, row0, rows):
    q0 = min(q for _, q in taps)
    q1 = max(q for _, q in taps)
    src = pad_ref if r == 0 else sh_ref.at[r - 1]
    win = src[pl.ds(row0 + SUBLANES * q0, SUBLANES * (q1 - q0) + rows), :]
    return win, [(k, SUBLANES * (q - q0)) for k, q in taps]


def _causal_conv(pad_ref, sh_ref, w_ref, out_ref, rows, offset_of_tap, bias=None):
    groups = _tap_groups(offset_of_tap)

    def block(b, carry):
        row0 = pl.multiple_of(b * CONV_ROW_BLOCK, CONV_ROW_BLOCK)
        if bias is None:
            acc = jnp.zeros((CONV_ROW_BLOCK, D_CONV), F32)
        else:
            acc = jnp.broadcast_to(bias, (CONV_ROW_BLOCK, D_CONV))
        for r, taps in groups.items():
            win, starts = _tap_window(pad_ref, sh_ref, r, taps, row0, CONV_ROW_BLOCK)
            for k, s in starts:
                acc = acc + w_ref[k:k + 1, :] * win[s:s + CONV_ROW_BLOCK, :]
        out_ref[pl.ds(row0, CONV_ROW_BLOCK), :] = acc
        return carry

    lax.fori_loop(0, rows // CONV_ROW_BLOCK, block, 0)


def _conv_weight_grad(dy_ref, pad_ref, sh_ref, acc_ref, rows, offset_of_tap):
    groups = _tap_groups(offset_of_tap)
    for r, taps in groups.items():

        def block(b, parts, r=r, taps=taps):
            row0 = pl.multiple_of(b * WGRAD_ROW_BLOCK, WGRAD_ROW_BLOCK)
            dyb = dy_ref[pl.ds(row0, WGRAD_ROW_BLOCK), :]
            win, starts = _tap_window(pad_ref, sh_ref, r, taps, row0, WGRAD_ROW_BLOCK)
            out = []
            for part, (_, s) in zip(parts, starts):
                pr = dyb * win[s:s + WGRAD_ROW_BLOCK, :]
                out.append(part + pr.reshape(WGRAD_ROW_BLOCK // SUBLANES, SUBLANES, D_CONV).sum(axis=0))
            return tuple(out)

        zeros = tuple(jnp.zeros((SUBLANES, D_CONV), F32) for _ in taps)
        parts = lax.fori_loop(0, rows // WGRAD_ROW_BLOCK, block, zeros)
        for part, (k, _) in zip(parts, taps):
            acc_ref[k] += part


def _prep_weights(w_in, w_out, w_gate_t, w_up_t, w_down, conv_w, w_s, b_s):
    def compute(ins, outs):
        win_ref, wout_ref, wgt_ref, wut_ref, wd_ref, cw_ref, ws_ref, bs_ref = ins
        win_o, wout_o, wgt_o, wut_o, wd_o, cw_o, wcat_o, wcatt_o, bsf_o = outs
        win_o[...] = win_ref[...].T.astype(BF16)
        wout_o[...] = wout_ref[...].astype(BF16)
        wgt_o[...] = wgt_ref[...].astype(BF16)
        wut_o[...] = wut_ref[...].astype(BF16)
        wd_o[...] = wd_ref[...].astype(BF16)
        cw_o[...] = jnp.zeros(cw_o.shape, F32)
        cw_o[0:CONV_WIDTH, 0:D_CONV // N_DEV] = cw_ref[...]
        row = lax.broadcasted_iota(jnp.int32, (CHUNK, CHUNK), 0)
        col = lax.broadcasted_iota(jnp.int32, (CHUNK, CHUNK), 1)
        causal = row >= col
        for h in range(N_HEADS):
            w = jnp.where(causal, ws_ref[h], 0.0)
            p, half = h // 2, (h % 2) * CHUNK
            wcat_o[p, :, half:half + CHUNK] = w.astype(BF16)
            wcatt_o[p, :, half:half + CHUNK] = w.T.astype(BF16)
        bsf_o[...] = lax.dot_general(bs_ref[...], _head_selector(), (((0,), (0,)), ((), ())),
                                     preferred_element_type=F32, precision=lax.Precision.HIGHEST)

    S = jax.ShapeDtypeStruct
    out_shapes = [S((256, D_MODEL), BF16), S((128, D_MODEL), BF16), S((FF_SHARD, D_MODEL), BF16),
                  S((FF_SHARD, D_MODEL), BF16), S((FF_SHARD, D_MODEL), BF16), S((CONV_ROWS, 128), F32),
                  S((4, CHUNK, 2 * CHUNK), BF16), S((4, CHUNK, 2 * CHUNK), BF16), S((CHUNK, D_SGU), F32)]
    (res,), _ = _staged_call(
        "prep_weights", [([w_in, w_out, w_gate_t, w_up_t, w_down, conv_w, w_s, b_s], out_shapes, compute)], 32)
    return res


def _mesh_position():
    x, y, c = lax.axis_index("x"), lax.axis_index("y"), lax.axis_index("c")
    return x, y, c


def _peers(x, y, c):
    out = []
    for k in range(1, N_DEV):
        px = 1 - x if (k >> 2) & 1 else x
        py = 1 - y if (k >> 1) & 1 else y
        pc = 1 - c if k & 1 else c
        out.append(((px, py, pc), 4 * px + 2 * py + pc))
    return out


class _Exchange:
    def __init__(self, scatter, gather):
        self.arrays = list(scatter) + list(gather)
        self.n_sc = len(scatter)
        self.n = len(self.arrays)
        self.out_shape = [jax.ShapeDtypeStruct(a.shape if k < self.n_sc else (N_DEV,) + a.shape, a.dtype)
                          for k, a in enumerate(self.arrays)]
        n_remote = self.n * (N_DEV - 1)
        self.scratch = [pltpu.SemaphoreType.DMA((n_remote,)), pltpu.SemaphoreType.DMA((n_remote,)),
                        pltpu.SemaphoreType.DMA((self.n,))] if self.n else []

    def _copies(self, src, dst, sems):
        send_sems, recv_sems, local_sems = sems
        x, y, c = _mesh_position()
        me = 4 * x + 2 * y + c
        locals_, first, arrivals, passed, last = [], [], [], [], []

        def remote(a, k, src_ref, slot, to):
            s = a * (N_DEV - 1) + k
            return pltpu.make_async_remote_copy(src_ref=src_ref, dst_ref=dst[a].at[slot], send_sem=send_sems.at[s],
                                                recv_sem=recv_sems.at[s], device_id=to, device_id_type=MESH)

        for a in range(self.n):
            if a < self.n_sc:
                locals_.append(pltpu.make_async_copy(src[a].at[me], dst[a].at[me], local_sems.at[a]))
                for k, (peer, pid) in enumerate(_peers(x, y, c)):
                    first.append(remote(a, k, src[a].at[pid], me, peer))
                    last.append(remote(a, k, src[a].at[pid], pid, peer))
                continue
            locals_.append(pltpu.make_async_copy(src[a], dst[a].at[me], local_sems.at[a]))
            sibling, sib_id = (x, y, 1 - c), 4 * x + 2 * y + (1 - c)
            chips = [(1 - x, y), (x, 1 - y), (1 - x, 1 - y)]
            first.append(remote(a, 0, src[a], me, sibling))
            last.append(remote(a, 0, src[a], sib_id, sibling))
            for j, (px, py) in enumerate(chips):
                same, other = 4 * px + 2 * py + c, 4 * px + 2 * py + (1 - c)
                first.append(remote(a, 1 + j, src[a], me, (px, py, c)))
                arrivals.append(remote(a, 1 + j, src[a], same, (px, py, c)))
                passed.append(remote(a, 4 + j, dst[a].at[same], same, sibling))
                last.append(remote(a, 4 + j, dst[a].at[other], other, sibling))
        return locals_, first, arrivals, passed, last

    def start(self, src, dst, sems):
        if not self.n:
            return
        locals_, first, _, _, _ = self._copies(src, dst, sems)
        for cp in locals_ + first:
            cp.start()

    def forward(self, src, dst, sems):
        if self.n == self.n_sc:
            return
        _, _, arrivals, passed, _ = self._copies(src, dst, sems)
        for arrived, cp in zip(arrivals, passed):
            arrived.wait_recv()
            cp.start()

    def wait(self, src, dst, sems):
        if not self.n:
            return
        locals_, first, _, passed, last = self._copies(src, dst, sems)
        for cp in last:
            cp.wait_recv()
        for cp in first + passed:
            cp.wait_send()
        for cp in locals_:
            cp.wait()


def _exchange(name, scatter, gather):
    ex = _Exchange(scatter, gather)
    n = ex.n

    def body(*refs):
        src, dst, sems = refs[:n], refs[n:2 * n], refs[2 * n:]
        ex.start(src, dst, sems)
        ex.forward(src, dst, sems)
        ex.wait(src, dst, sems)

    return pl.pallas_call(
        body, name=name, out_shape=tuple(ex.out_shape), in_specs=[ANY] * n, out_specs=(ANY,) * n,
        scratch_shapes=ex.scratch,
    )(*ex.arrays)


def _forward_step(n_steps):
    return (5 * n_steps) // 8


def _hosted(ex, refs, n_in, n_out):
    ins, ex_src = refs[:n_in], refs[n_in:n_in + ex.n]
    rest = refs[n_in + ex.n:]
    outs, ex_dst = rest[:n_out], rest[n_out:n_out + ex.n]
    rest = rest[n_out + ex.n:]
    n_own = len(rest) - len(ex.scratch)
    return ins, outs, rest[:n_own], (ex_src, ex_dst, rest[n_own:])


def _fwd_mix(x, win_g, wout_g, sgu_g, sgu_b, wcat, bs_full, cw, cb, cg, cbeta, tm, ex):
    T = x.shape[0]
    nt = T // tm

    def body(*refs):
        ins, outs, scratch, ex_refs = _hosted(ex, refs, 11, 6)
        x_ref, win_ref, wout_ref, sg_ref, sb_ref, wcat_ref, bs_ref, cw_ref, cb_ref, cg_ref, cbeta_ref = ins
        proj_ref, ycat_ref, n1_ref, rstd1_ref, phi_ref, y_ref = outs
        hpad, hshift = scratch
        i = pl.program_id(0)

        @pl.when(i == 0)
        def _():
            ex.start(*ex_refs)

        xf = x_ref[...]
        xb = xf.astype(BF16)
        proj_ref[...] = _dot_nt(xb, win_ref[...])
        cdf = _normal_cdf(proj_ref[:, 0:2 * D_SGU])
        phi_ref[...] = cdf.astype(BF16)
        u = proj_ref[:, 0:D_SGU] * cdf[:, 0:D_SGU]
        v = proj_ref[:, D_SGU:2 * D_SGU] * cdf[:, D_SGU:2 * D_SGU]
        vhat, _ = _ln_fwd(v)
        vn = vhat * sg_ref[...] + sb_ref[...]
        lo = _lo_mask()
        for c in range(tm // CHUNK):
            rows = slice(CHUNK * c, CHUNK * (c + 1))
            for p in range(4):
                lanes = slice(CHUNK * p, CHUNK * (p + 1))
                mixed = _dot(wcat_ref[p], _head_pair_stack(vn[rows, lanes], lo)) + bs_ref[:, lanes]
                ycat_ref[rows, lanes] = (u[rows, lanes] * mixed).astype(BF16)
        base = 2 * D_SGU
        a = proj_ref[:, base:base + D_CONV]
        g = proj_ref[:, base + D_CONV:base + 2 * D_CONV]

        @pl.when(i == 0)
        def _():
            hpad[0:HALO, :] = jnp.zeros((HALO, D_CONV), F32)

        hpad[HALO:HALO + tm, :] = a * jax.nn.sigmoid(g)
        _shifted_copies(hpad, hshift, tm + SHIFT_ROWS)
        _causal_conv(hpad, hshift, cw_ref, y_ref, tm, lambda k: HALO - (CONV_WIDTH - 1) + k, bias=cb_ref[...])
        hpad[0:HALO, :] = hpad[tm:tm + HALO, :]
        yhat, _ = _ln_fwd(y_ref[...])
        yn = yhat * cg_ref[...] + cbeta_ref[...]
        ycat_ref[:, D_SGU:D_SGU + D_CONV] = (yn * jax.nn.sigmoid(yn)).astype(BF16)
        r1 = ALPHA * xf + _dot(ycat_ref[...], wout_ref[...])
        n1, rstd1 = _ln_fwd(r1)
        n1_ref[...] = n1
        rstd1_ref[...] = rstd1

        @pl.when(i == _forward_step(nt))
        def _():
            ex.forward(*ex_refs)

        @pl.when(i == nt - 1)
        def _():
            ex.wait(*ex_refs)

    S = jax.ShapeDtypeStruct
    row = lambda w: pl.BlockSpec((tm, w), lambda i: (i, 0))
    res = pl.pallas_call(
        body, name="fwd_mix", grid=(nt,),
        in_specs=[row(D_MODEL), _full(win_g.shape), _full(wout_g.shape), _full(sgu_g.shape), _full(sgu_b.shape),
                  _full(wcat.shape), _full(bs_full.shape), _full(cw.shape), _full(cb.shape), _full(cg.shape),
                  _full(cbeta.shape)] + [ANY] * ex.n,
        out_specs=(row(2 * D_MODEL), row(D_MODEL), row(D_MODEL), row(1), row(2 * D_SGU), row(D_CONV)) + (ANY,) * ex.n,
        out_shape=(S((T, 2 * D_MODEL), F32), S((T, D_MODEL), BF16), S((T, D_MODEL), F32), S((T, 1), F32),
                   S((T, 2 * D_SGU), BF16), S((T, D_CONV), F32), *ex.out_shape),
        scratch_shapes=[pltpu.VMEM((tm + HALO, D_CONV), F32),
                        pltpu.VMEM((SUBLANES - 1, tm + SHIFT_ROWS, D_CONV), F32)] + ex.scratch,
        compiler_params=_params(56, 1),
    )(x, win_g, wout_g, sgu_g, sgu_b, wcat, bs_full, cw, cb, cg, cbeta, *ex.arrays)
    return res[:6], res[6:]


def _load_resident(pairs, sems):
    cps = [pltpu.make_async_copy(s, d, sems.at[k]) for k, (s, d) in enumerate(pairs)]
    for cp in cps:
        cp.start()
    for cp in cps:
        cp.wait()


def _fwd_mlp(n1, tgt, l1g, l1b, l2g, l2b, wgt, wut, wd, tm):
    T = n1.shape[0]
    nt = T // tm
    nf = D_FF // MXU_COLS

    def body(n1_ref, tgt_ref, l1g_ref, l1b_ref, l2g_ref, l2b_ref, wg_hbm, wu_hbm, wd_hbm,
             gate_ref, up_ref, hh_ref, dr2_ref, stat_ref, wg_s, wu_s, wd_s, sems):
        i = pl.program_id(0)

        @pl.when(i == 0)
        def _():
            _load_resident([(wg_hbm, wg_s), (wu_hbm, wu_s), (wd_hbm, wd_s)], sems)
            stat_ref[...] = jnp.zeros(stat_ref.shape, F32)

        x1 = n1_ref[...] * l1g_ref[...] + l1b_ref[...]
        x1b = x1.astype(BF16)
        for f in range(nf):
            cols = slice(MXU_COLS * f, MXU_COLS * (f + 1))
            gt = _dot_nt(x1b, wg_s[cols, :])
            ut = _dot_nt(x1b, wu_s[cols, :])
            gate_ref[:, cols] = gt.astype(BF16)
            up_ref[:, cols] = ut.astype(BF16)
            hh_ref[:, cols] = (gt * jax.nn.sigmoid(gt) * ut).astype(BF16)
        r2 = ALPHA * x1 + _dot(hh_ref[...], wd_s[...])
        n2, rstd2 = _ln_fwd(r2)
        x2 = n2 * l2g_ref[...] + l2b_ref[...]
        diff = x2 - tgt_ref[...]
        dx2 = diff * (1.0 / D_MODEL)
        stat_ref[0:1, :] += _colsum(diff * diff)
        stat_ref[1:2, :] += _colsum(dx2 * n2)
        stat_ref[2:3, :] += _colsum(dx2)
        dr2_ref[...] = _ln_bwd(dx2 * l2g_ref[...], n2, rstd2)

    S = jax.ShapeDtypeStruct
    row = lambda w: pl.BlockSpec((tm, w), lambda i: (i, 0))
    vec = _full((1, D_MODEL))
    return pl.pallas_call(
        body, name="fwd_mlp", grid=(nt,),
        in_specs=[row(D_MODEL), row(D_MODEL), vec, vec, vec, vec, ANY, ANY, ANY],
        out_specs=(row(D_FF), row(D_FF), row(D_FF), row(D_MODEL), _full((8, D_MODEL))),
        out_shape=(S((T, D_FF), BF16), S((T, D_FF), BF16), S((T, D_FF), BF16), S((T, D_MODEL), F32),
                   S((8, D_MODEL), F32)),
        scratch_shapes=[pltpu.VMEM((D_FF, D_MODEL), BF16)] * 3 + [pltpu.SemaphoreType.DMA((3,))],
        compiler_params=_params(56, 1),
    )(n1, tgt, l1g, l1b, l2g, l2b, wgt, wut, wd)


def _bwd_mlp(dr2, gate, up, n1, rstd1, l1g, wgt, wut, wd, tm):
    T = n1.shape[0]
    nt = T // tm
    nf = D_FF // MXU_COLS

    def body(dr2_ref, gate_hbm, up_hbm, n1_ref, rstd1_ref, l1g_ref, wg_hbm, wu_hbm, wd_hbm,
             dgate_ref, dup_ref, dr1_ref, stat_ref, wg_s, wu_s, wd_s, sems, gbuf, ubuf, ahead_sems):
        i = pl.program_id(0)

        def fetch(step, slot):
            rows = pl.ds(pl.multiple_of(step * tm, tm), tm)
            return [pltpu.make_async_copy(gate_hbm.at[rows], gbuf.at[slot], ahead_sems.at[slot]),
                    pltpu.make_async_copy(up_hbm.at[rows], ubuf.at[slot], ahead_sems.at[MLP_RING + slot])]

        @pl.when(i == 0)
        def _():
            for s in range(MLP_TILES_AHEAD):
                for cp in fetch(s, s):
                    cp.start()
            _load_resident([(wg_hbm, wg_s), (wu_hbm, wu_s), (wd_hbm, wd_s)], sems)
            stat_ref[...] = jnp.zeros(stat_ref.shape, F32)

        @pl.when(i + MLP_TILES_AHEAD < nt)
        def _():
            for cp in fetch(i + MLP_TILES_AHEAD, (i + MLP_TILES_AHEAD) % MLP_RING):
                cp.start()

        slot = i % MLP_RING
        for cp in fetch(i, slot):
            cp.wait()
        dr2 = dr2_ref[...]
        dr2b = dr2.astype(BF16)
        for f in range(nf):
            cols = slice(MXU_COLS * f, MXU_COLS * (f + 1))
            dhh = _dot_nt(dr2b, wd_s[cols, :])
            gt = gbuf[slot, :, cols].astype(F32)
            ut = ubuf[slot, :, cols].astype(F32)
            sg = jax.nn.sigmoid(gt)
            dgate_ref[:, cols] = (dhh * ut * (sg * (1.0 + gt * (1.0 - sg)))).astype(BF16)
            dup_ref[:, cols] = (dhh * (gt * sg)).astype(BF16)
        dx1 = ALPHA * dr2 + _dot(dgate_ref[...], wg_s[...]) + _dot(dup_ref[...], wu_s[...])
        n1 = n1_ref[...]
        stat_ref[0:1, :] += _colsum(dx1 * n1)
        stat_ref[1:2, :] += _colsum(dx1)
        dr1_ref[...] = _ln_bwd(dx1 * l1g_ref[...], n1, rstd1_ref[...])

    S = jax.ShapeDtypeStruct
    row = lambda w: pl.BlockSpec((tm, w), lambda i: (i, 0))
    return pl.pallas_call(
        body, name="bwd_mlp", grid=(nt,),
        in_specs=[row(D_MODEL), ANY, ANY, row(D_MODEL), row(1), _full((1, D_MODEL)), ANY, ANY, ANY],
        out_specs=(row(D_FF), row(D_FF), row(D_MODEL), _full((8, D_MODEL))),
        out_shape=(S((T, D_FF), BF16), S((T, D_FF), BF16), S((T, D_MODEL), F32), S((8, D_MODEL), F32)),
        scratch_shapes=[pltpu.VMEM((D_FF, D_MODEL), BF16)] * 3 + [pltpu.SemaphoreType.DMA((3,))]
        + [pltpu.VMEM((MLP_RING, tm, D_FF), BF16)] * 2 + [pltpu.SemaphoreType.DMA((2 * MLP_RING,))],
        compiler_params=_params(56, 1),
    )(dr2, gate, up, n1, rstd1, l1g, wgt, wut, wd)


def _bwd_mix(dr1, proj, phi, y, win_g, wout_g, sgu_g, sgu_b, wcat, wcatt, bs_full, cw, cg, cbeta, tm, ex):
    T = dr1.shape[0]
    nt = T // tm
    halo_blocks = tm // HALO

    def body(*refs):
        ins, outs, scratch, ex_refs = _hosted(ex, refs, 15, 6)
        (dr1_ref, proj_ref, halo_ref, phi_ref, y_ref, win_ref, wout_ref, sg_ref, sb_ref, wcat_ref, wcatt_ref, bs_ref,
         cw_ref, cg_ref, cbeta_ref) = ins
        gx_ref, dproj_ref, gws_out, gbs_out, gcw_ref, vec_ref = outs
        hpad, shift, dypad, dhbuf, dubuf, dvnbuf, gcw_acc, gws_ref, gbs_ref = scratch
        i = pl.program_id(0)
        tile = nt - 1 - i

        @pl.when(i == 0)
        def _():
            ex.start(*ex_refs)
            gws_ref[...] = jnp.zeros(gws_ref.shape, F32)
            gbs_ref[...] = jnp.zeros(gbs_ref.shape, F32)
            gcw_ref[...] = jnp.zeros(gcw_ref.shape, F32)
            vec_ref[...] = jnp.zeros(vec_ref.shape, F32)
            gcw_acc[...] = jnp.zeros(gcw_acc.shape, F32)
            dypad[tm:tm + HALO, :] = jnp.zeros((HALO, D_CONV), F32)

        dr1 = dr1_ref[...]
        dycat = _dot_nt(dr1.astype(BF16), wout_ref[...])
        pu = proj_ref[:, 0:D_SGU]
        pv = proj_ref[:, D_SGU:2 * D_SGU]
        cdf_u = phi_ref[:, 0:D_SGU].astype(F32)
        cdf_v = phi_ref[:, D_SGU:2 * D_SGU].astype(F32)
        u = pu * cdf_u
        vhat, rstd_v = _ln_fwd(pv * cdf_v)
        vn = vhat * sg_ref[...] + sb_ref[...]
        lo = _lo_mask()
        for c in range(tm // CHUNK):
            rows = slice(CHUNK * c, CHUNK * (c + 1))
            for p in range(4):
                lanes = slice(CHUNK * p, CHUNK * (p + 1))
                vstack = _head_pair_stack(vn[rows, lanes], lo)
                mixed = _dot(wcat_ref[p], vstack) + bs_ref[:, lanes]
                d_a = dycat[rows, lanes]
                dubuf[rows, lanes] = d_a * mixed
                dm = d_a * u[rows, lanes]
                gbs_ref[:, lanes] += dm
                dstack = _head_pair_stack(dm, lo)
                gws_ref[2 * CHUNK * p:2 * CHUNK * (p + 1), :] += _dot_nt(dstack, vn[rows, lanes].astype(BF16))
                dvnbuf[rows, lanes] = _dot(wcatt_ref[p], dstack)
        dvn = dvnbuf[...]
        vec_ref[0:1, :] += _colsum(dvn * vhat)
        vec_ref[1:2, :] += _colsum(dvn)
        dv = _ln_bwd(dvn * sg_ref[...], vhat, rstd_v)
        dproj_ref[:, 0:D_SGU] = (dubuf[...] * _gelu_grad(pu, cdf_u)).astype(BF16)
        dproj_ref[:, D_SGU:2 * D_SGU] = (dv * _gelu_grad(pv, cdf_v)).astype(BF16)
        base = 2 * D_SGU
        a = proj_ref[:, base:base + D_CONV]
        sgm = jax.nn.sigmoid(proj_ref[:, base + D_CONV:base + 2 * D_CONV])
        h_before = halo_ref[:, 0:D_CONV] * jax.nn.sigmoid(halo_ref[:, D_CONV:2 * D_CONV])
        hpad[0:HALO, :] = jnp.where(tile > 0, h_before, 0.0)
        hpad[HALO:HALO + tm, :] = a * sgm
        _shifted_copies(hpad, shift, tm + SHIFT_ROWS)
        h_offset = lambda k: HALO - (CONV_WIDTH - 1) + k
        yhat, rstd_y = _ln_fwd(y_ref[...])
        yn = yhat * cg_ref[...] + cbeta_ref[...]
        s = jax.nn.sigmoid(yn)
        dyn = dycat[:, D_SGU:D_SGU + D_CONV] * (s * (1.0 + yn * (1.0 - s)))
        vec_ref[3:4, :] += _colsum(dyn * yhat)
        vec_ref[4:5, :] += _colsum(dyn)
        dy = _ln_bwd(dyn * cg_ref[...], yhat, rstd_y)
        vec_ref[2:3, :] += _colsum(dy)
        dypad[0:tm, :] = dy
        _conv_weight_grad(dypad, hpad, shift, gcw_acc, tm, h_offset)
        _shifted_copies(dypad, shift, tm + SHIFT_ROWS)
        _causal_conv(dypad, shift, cw_ref, dhbuf, tm, lambda k: (CONV_WIDTH - 1) - k)
        dypad[tm:tm + HALO, :] = dypad[0:HALO, :]
        dh = dhbuf[...]
        dproj_ref[:, base:base + D_CONV] = (dh * sgm).astype(BF16)
        dproj_ref[:, base + D_CONV:base + 2 * D_CONV] = (dh * a * sgm * (1.0 - sgm)).astype(BF16)
        gx_ref[...] = ALPHA * dr1 + _dot(dproj_ref[...], win_ref[...])

        @pl.when(i == _forward_step(nt))
        def _():
            ex.forward(*ex_refs)

        @pl.when(i == nt - 1)
        def _():
            gcw_ref[...] = gcw_acc[...].sum(axis=1)
            gws_out[...] = gws_ref[...].astype(BF16)
            gbs_out[...] = lax.dot_general(_head_selector(), gbs_ref[...], (((1,), (1,)), ((), ())),
                                           preferred_element_type=F32, precision=lax.Precision.HIGHEST)
            ex.wait(*ex_refs)

    S = jax.ShapeDtypeStruct
    row = lambda w: pl.BlockSpec((tm, w), lambda i: (nt - 1 - i, 0))
    halo = pl.BlockSpec((HALO, D_MODEL), lambda i: (jnp.maximum((nt - 1 - i) * halo_blocks - 1, 0), 1))
    res = pl.pallas_call(
        body, name="bwd_mix", grid=(nt,),
        in_specs=[row(D_MODEL), row(2 * D_MODEL), halo, row(2 * D_SGU), row(D_CONV), _full(win_g.shape),
                  _full(wout_g.shape), _full(sgu_g.shape), _full(sgu_b.shape), _full(wcat.shape), _full(wcatt.shape),
                  _full(bs_full.shape), _full(cw.shape), _full(cg.shape), _full(cbeta.shape)]
        + [ANY] * ex.n,
        out_specs=(row(D_MODEL), row(2 * D_MODEL), _full((N_HEADS * CHUNK, CHUNK)), _full((N_HEADS, CHUNK)),
                   _full((CONV_ROWS, D_CONV)), _full((8, D_CONV))) + (ANY,) * ex.n,
        out_shape=(S((T, D_MODEL), F32), S((T, 2 * D_MODEL), BF16), S((N_HEADS * CHUNK, CHUNK), BF16),
                   S((N_HEADS, CHUNK), F32), S((CONV_ROWS, D_CONV), F32), S((8, D_CONV), F32), *ex.out_shape),
        scratch_shapes=[pltpu.VMEM((tm + HALO, D_CONV), F32), pltpu.VMEM((SUBLANES - 1, tm + SHIFT_ROWS, D_CONV), F32),
                        pltpu.VMEM((tm + HALO, D_CONV), F32),
                        pltpu.VMEM((tm, D_CONV), F32), pltpu.VMEM((tm, D_SGU), F32),
                        pltpu.VMEM((tm, D_SGU), F32), pltpu.VMEM((CONV_ROWS, 8, D_CONV), F32),
                        pltpu.VMEM((N_HEADS * CHUNK, CHUNK), F32), pltpu.VMEM((CHUNK, D_SGU), F32)] + ex.scratch,
        compiler_params=_params(56, 1),
    )(dr1, proj, proj, phi, y, win_g, wout_g, sgu_g, sgu_b, wcat, wcatt, bs_full, cw, cg, cbeta, *ex.arrays)
    return res[:6], res[6:]


def _wgrad(name, a, b, blocks, tk, ex=None, b_cols=None, b_affine=None):
    T, M = a.shape
    col, N = b_cols or (0, b.shape[1])
    nk = T // tk
    out_shape = (blocks, M // blocks, N)
    ex = ex or _Exchange([], [])
    affine = list(b_affine or [])

    def body(*refs):
        ins, (o_ref,), (acc,), ex_refs = _hosted(ex, refs, 2 + len(affine), 1)
        a_ref, b_ref = ins[:2]
        i = pl.program_id(0)

        @pl.when(i == 0)
        def _():
            ex.start(*ex_refs)
            acc[...] = jnp.zeros(acc.shape, F32)

        right = b_ref[...]
        if affine:
            right = right * ins[2][...] + ins[3][...]
        acc[...] += _dot_tn(a_ref[...].astype(BF16), right.astype(BF16))

        @pl.when(i == _forward_step(nk))
        def _():
            ex.forward(*ex_refs)

        @pl.when(i == nk - 1)
        def _():
            o_ref[...] = acc[...].astype(BF16)
            ex.wait(*ex_refs)

    res = pl.pallas_call(
        body, name=name, grid=(nk,),
        in_specs=[pl.BlockSpec((tk, M), lambda i: (i, 0)), pl.BlockSpec((tk, N), lambda i: (i, col))]
        + [_full((1, N))] * len(affine) + [ANY] * ex.n,
        out_specs=(_full((M, N)),) + (ANY,) * ex.n,
        out_shape=(jax.ShapeDtypeStruct((M, N), BF16), *ex.out_shape),
        scratch_shapes=[pltpu.VMEM((M, N), F32)] + ex.scratch,
        compiler_params=_params(56, 1),
    )(a, b, *affine, *ex.arrays)
    g = res[0].reshape(out_shape)
    return (g, res[1:]) if ex.n else g


def _adamw(w, g, m, v):
    m2 = ADAM_B1 * m + (1.0 - ADAM_B1) * g
    v2 = ADAM_B2 * v + (1.0 - ADAM_B2) * (g * g)
    m_hat = m2 / (1.0 - ADAM_B1 ** ADAM_STEP)
    v_hat = v2 / (1.0 - ADAM_B2 ** ADAM_STEP)
    delta = -ADAM_LR * (m_hat / (jnp.sqrt(v_hat) + ADAM_EPS) + ADAM_WD * w)
    return delta, m2, v2


def _sum_partials(r_ref):
    g = r_ref[0].astype(F32)
    for s in range(1, N_DEV):
        g = g + r_ref[s].astype(F32)
    return g


def _staged_call(name, groups, vmem_mib, ex=None):
    ex = ex or _Exchange([], [])
    inputs = [a for ins, _, _ in groups for a in ins]
    out_shapes = [s for _, outs, _ in groups for s in outs]
    n_in, n_out = len(inputs), len(out_shapes)

    def body(*refs):
        ins, outs, scratch, ex_refs = _hosted(ex, refs, n_in, n_out)
        in_bufs, out_bufs, sems = scratch[:n_in], scratch[n_in:n_in + n_out], scratch[n_in + n_out]
        ex.start(*ex_refs)
        loads = [pltpu.make_async_copy(ins[k], in_bufs[k], sems.at[k]) for k in range(n_in)]
        stores = [pltpu.make_async_copy(out_bufs[k], outs[k], sems.at[n_in + k]) for k in range(n_out)]
        for cp in loads:
            cp.start()
        i0 = o0 = 0
        for g_ins, g_outs, compute in groups:
            i1, o1 = i0 + len(g_ins), o0 + len(g_outs)
            for cp in loads[i0:i1]:
                cp.wait()
            compute(in_bufs[i0:i1], out_bufs[o0:o1])
            for cp in stores[o0:o1]:
                cp.start()
            i0, o0 = i1, o1
        for cp in stores:
            cp.wait()
        ex.forward(*ex_refs)
        ex.wait(*ex_refs)

    scratch = ([pltpu.VMEM(a.shape, a.dtype) for a in inputs] + [pltpu.VMEM(s.shape, s.dtype) for s in out_shapes]
               + [pltpu.SemaphoreType.DMA((n_in + n_out,))] + ex.scratch)
    res = pl.pallas_call(
        body, name=name, out_shape=(*[pltpu.HBM(s.shape, s.dtype) for s in out_shapes], *ex.out_shape),
        in_specs=[HBM] * n_in + [ANY] * ex.n, out_specs=(HBM,) * n_out + (ANY,) * ex.n,
        scratch_shapes=scratch, compiler_params=_params(vmem_mib),
    )(*[pltpu.with_memory_space_constraint(a, pltpu.HBM) for a in inputs], *ex.arrays)
    per_group, o0 = [], 0
    for _, g_outs, _ in groups:
        per_group.append(list(res[o0:o0 + len(g_outs)]))
        o0 += len(g_outs)
    return per_group, res[n_out:]


def _adamw_shard_group(parts, w, m, v, transposed):
    n = len(parts)

    def compute(ins, outs):
        w_ref, m_ref, v_ref = ins[n:]
        lo = 0
        for r_ref in ins[:n]:
            g = _sum_partials(r_ref)
            cols = g.shape[1]
            if transposed:
                g, at = g.T, (slice(lo, lo + cols), slice(None))
            else:
                at = (slice(None), slice(lo, lo + cols))
            delta, m2, v2 = _adamw(w_ref[at], g, m_ref[at], v_ref[at])
            for o, val in zip(outs, (g, delta, m2, v2)):
                o[at] = val
            lo += cols

    return [*parts, w, m, v], [jax.ShapeDtypeStruct(w.shape, F32)] * 4, compute


SMALL_NAMES = ["sgu_ln_g", "sgu_ln_b", "w_s", "b_s", "conv_b", "conv_ln_g", "conv_ln_b", "ln1_g", "ln1_b", "ln2_g", "ln2_b",
               "conv_w"]


def _finish_small_group(gws8, gbs8, gcw8, vmix8, vmlp8, vout8, small):
    names = SMALL_NAMES
    flat = []
    for n in names:
        flat += list(small[n])
    cw_block = D_CONV // N_DEV

    def compute(ins, outs):
        gws_ref, gbs_ref, gcw_ref, vmix_ref, vmlp_ref, vout_ref = ins[:6]
        wmv = ins[6:]
        loss_o = outs[0]
        outs = outs[1:]
        gws = _sum_partials(gws_ref)
        gbs = _sum_partials(gbs_ref)
        vmix = _sum_partials(vmix_ref)
        vmlp = _sum_partials(vmlp_ref)
        vout = _sum_partials(vout_ref)
        x, y, c = _mesh_position()
        first = (4 * x + 2 * y + c) * cw_block
        pick = (lax.broadcasted_iota(jnp.int32, (D_CONV, cw_block), 0)
                == first + lax.broadcasted_iota(jnp.int32, (D_CONV, cw_block), 1)).astype(F32)
        gcw = jnp.dot(_sum_partials(gcw_ref), pick, preferred_element_type=F32,
                      precision=lax.Precision.HIGHEST)[0:CONV_WIDTH, :]
        loss = (0.5 / D_MODEL) * jnp.sum(vout[0:1, :], axis=1, keepdims=True)
        loss_o[...] = jnp.broadcast_to(loss, loss_o.shape)
        rows = lax.broadcasted_iota(jnp.int32, (N_HEADS * CHUNK, CHUNK), 0)
        cols = lax.broadcasted_iota(jnp.int32, (N_HEADS * CHUNK, CHUNK), 1)
        gws = jnp.where((rows & (CHUNK - 1)) >= cols, gws, 0.0)
        grads = {
            "sgu_ln_g": vmix[0:1, :], "sgu_ln_b": vmix[1:2, :], "w_s": gws, "b_s": gbs,
            "conv_b": vmix[2:3, :], "conv_ln_g": vmix[3:4, :], "conv_ln_b": vmix[4:5, :],
            "ln1_g": vmlp[0:1, :], "ln1_b": vmlp[1:2, :], "ln2_g": vout[1:2, :], "ln2_b": vout[2:3, :],
            "conv_w": gcw,
        }
        for k, n in enumerate(names):
            w_ref, m_ref, v_ref = wmv[3 * k:3 * k + 3]
            g = grads[n]
            delta, m2, v2 = _adamw(w_ref[...], g, m_ref[...], v_ref[...])
            outs[4 * k][...] = g
            outs[4 * k + 1][...] = delta
            outs[4 * k + 2][...] = m2
            outs[4 * k + 3][...] = v2

    S = jax.ShapeDtypeStruct
    out_shape = [S((SUBLANES, 128), F32)]
    for n in names:
        out_shape += [S(small[n][0].shape, F32)] * 4
    return [gws8, gbs8, gcw8, vmix8, vmlp8, vout8, *flat], out_shape, compute


TOKEN_TILE_FWD_MIX = 512
TOKEN_TILE_BWD_MIX = 512
TOKEN_TILE_FWD_MLP = 512
TOKEN_TILE_BWD_MLP = 256
MLP_TILES_AHEAD = 3
MLP_RING = MLP_TILES_AHEAD + 1
TOKEN_TILE_WGRAD = 1024


def kernel(x, w_in, sgu_ln_g, sgu_ln_b, w_s, b_s, conv_w, conv_b, conv_ln_g, conv_ln_b, w_out, ln1_g, ln1_b, w_gate, w_up, w_down, ln2_g, ln2_b, loss_target, m_w_in, m_sgu_ln_g, m_sgu_ln_b, m_w_s, m_b_s, m_conv_w, m_conv_b, m_conv_ln_g, m_conv_ln_b, m_w_out, m_ln1_g, m_ln1_b, m_w_gate, m_w_up, m_w_down, m_ln2_g, m_ln2_b, v_w_in, v_sgu_ln_g, v_sgu_ln_b, v_w_s, v_b_s, v_conv_w, v_conv_b, v_conv_ln_g, v_conv_ln_b, v_w_out, v_ln1_g, v_ln1_b, v_w_gate, v_w_up, v_w_down, v_ln2_g, v_ln2_b):
    xs = x[0]
    tgt = loss_target[0]

    (win_b, wout_b, wgt_b, wut_b, wd_b, cw_b, wcat, wcatt, bs_full) = _prep_weights(
        w_in[0], w_out[0], w_gate[0].T, w_up[0].T, w_down[0], conv_w[0], w_s[0], b_s[0])
    win_g, wout_g, cw_g = _exchange("gather_mix_weights", [], [win_b, wout_b, cw_b])
    win_g = win_g.reshape(2 * D_MODEL, D_MODEL)
    wout_g = wout_g.reshape(D_MODEL, D_MODEL)
    cw = jnp.transpose(cw_g[:, :, :D_CONV // N_DEV], (1, 0, 2)).reshape(CONV_ROWS, D_CONV)

    (proj, ycat, n1, rstd1, phi, y_conv), (wgt_g, wut_g, wd_g) = _fwd_mix(
        xs, win_g, wout_g, sgu_ln_g, sgu_ln_b, wcat, bs_full, cw, conv_b, conv_ln_g, conv_ln_b, TOKEN_TILE_FWD_MIX,
        _Exchange([], [wgt_b, wut_b, wd_b]))
    wgt_g = wgt_g.reshape(D_FF, D_MODEL)
    wut_g = wut_g.reshape(D_FF, D_MODEL)
    wd_g = wd_g.reshape(D_FF, D_MODEL)
    gate, up, hh, dr2, vout = _fwd_mlp(n1, tgt, ln1_g, ln1_b, ln2_g, ln2_b, wgt_g, wut_g, wd_g, TOKEN_TILE_FWD_MLP)

    dgate, dup, dr1, vmlp = _bwd_mlp(dr2, gate, up, n1, rstd1, ln1_g, wgt_g, wut_g, wd_g, TOKEN_TILE_BWD_MLP)
    tk = TOKEN_TILE_WGRAD
    x1 = dict(b_affine=(ln1_g, ln1_b))
    g_wgt = _wgrad("wgrad_gate", dgate, n1, N_DEV, tk, **x1)
    g_wut = _wgrad("wgrad_up", dup, n1, N_DEV, tk, **x1)
    g_wd = _wgrad("wgrad_down", hh, dr2, N_DEV, tk)
    g_wout = _wgrad("wgrad_out", ycat, dr1, N_DEV, tk)
    (gx, dproj, gws, gbs, gcw, vmix), (r_wgt, r_wut, r_wd, r_wout) = _bwd_mix(
        dr1, proj, phi, y_conv, win_g, wout_g, sgu_ln_g, sgu_ln_b, wcat, wcatt, bs_full, cw, conv_ln_g, conv_ln_b,
        TOKEN_TILE_BWD_MIX, _Exchange([g_wgt, g_wut, g_wd, g_wout], []))
    half = D_MODEL // 2
    g_win_a, (gws8, gbs8, gcw8, vmix8, vmlp8, vout8) = _wgrad(
        "wgrad_in_a", dproj, xs, N_DEV, tk, _Exchange([], [gws, gbs, gcw, vmix, vmlp, vout]), b_cols=(0, half))
    g_win_b, (r_win_a,) = _wgrad("wgrad_in_b", dproj, xs, N_DEV, tk, _Exchange([g_win_a], []), b_cols=(1, half))
    (r_win_b,) = _exchange("exchange_grad_in", [g_win_b], [])
    (u_gate, u_up, u_down), _ = _staged_call(
        "adamw_mlp",
        [_adamw_shard_group([r_wgt], w_gate[0].T, m_w_gate[0].T, v_w_gate[0].T, False),
         _adamw_shard_group([r_wut], w_up[0].T, m_w_up[0].T, v_w_up[0].T, False),
         _adamw_shard_group([r_wd], w_down[0], m_w_down[0], v_w_down[0], False)], 56)
    small_in = {
        "conv_w": (conv_w[0], m_conv_w[0], v_conv_w[0]),
        "sgu_ln_g": (sgu_ln_g, m_sgu_ln_g, v_sgu_ln_g), "sgu_ln_b": (sgu_ln_b, m_sgu_ln_b, v_sgu_ln_b),
        "w_s": tuple(a.reshape(N_HEADS * CHUNK, CHUNK) for a in (w_s, m_w_s, v_w_s)),
        "b_s": (b_s[0], m_b_s[0], v_b_s[0]),
        "conv_b": (conv_b, m_conv_b, v_conv_b), "conv_ln_g": (conv_ln_g, m_conv_ln_g, v_conv_ln_g),
        "conv_ln_b": (conv_ln_b, m_conv_ln_b, v_conv_ln_b),
        "ln1_g": (ln1_g, m_ln1_g, v_ln1_g), "ln1_b": (ln1_b, m_ln1_b, v_ln1_b),
        "ln2_g": (ln2_g, m_ln2_g, v_ln2_g), "ln2_b": (ln2_b, m_ln2_b, v_ln2_b),
    }
    (u_in, u_out, fin), _ = _staged_call(
        "adamw_mix_small",
        [_adamw_shard_group([r_win_a, r_win_b], w_in[0], m_w_in[0], v_w_in[0], True),
         _adamw_shard_group([r_wout], w_out[0], m_w_out[0], v_w_out[0], False),
         _finish_small_group(gws8, gbs8, gcw8, vmix8, vmlp8, vout8, small_in)], 40)
    big = {"w_in": u_in, "w_out": u_out, "w_gate": u_gate, "w_up": u_up, "w_down": u_down}
    loss11 = fin[0]
    small = {n: fin[1 + 4 * k:5 + 4 * k] for k, n in enumerate(SMALL_NAMES)}

    shapes = {"w_s": w_s.shape, "b_s": b_s.shape, "conv_w": conv_w.shape}
    out = {}
    for n, r in big.items():
        out[n] = tuple((a.T if n in ("w_gate", "w_up") else a)[None] for a in r)
    for n, r in small.items():
        out[n] = tuple(a.reshape(shapes[n]) for a in r) if n in shapes else tuple(r)

    order = ["w_in", "sgu_ln_g", "sgu_ln_b", "w_s", "b_s", "conv_w", "conv_b", "conv_ln_g", "conv_ln_b", "w_out",
             "ln1_g", "ln1_b", "w_gate", "w_up", "w_down", "ln2_g", "ln2_b"]
    loss = loss11[0, 0]
    return (loss, gx[None], *[out[n][0] for n in order], *[out[n][1] for n in order],
            *[out[n][2] for n in order], *[out[n][3] for n in order])
```

```python
import jax
import jax.numpy as jnp
from jax import lax
from jax.experimental import pallas as pl
from jax.experimental.pallas import tpu as pltpu

F32 = jnp.float32
BF16 = jnp.bfloat16

D_MODEL = 1024
D_SGU = 512
D_CONV = 512
N_HEADS = 8
CHUNK = 128
CONV_WIDTH = 31
CONV_ROWS = 32
HALO = 32
D_FF = 2816
N_DEV = 8
FF_SHARD = D_FF // N_DEV
ALPHA = (2.0 * 1) ** 0.25
LN_EPS = 1e-5
INV_SQRT2 = 0.7071067811865476
INV_SQRT_2PI = 0.3989422804014327

ADAM_LR = 0.001
ADAM_B1 = 0.9
ADAM_B2 = 0.999
ADAM_EPS = 1e-08
ADAM_WD = 0.01
ADAM_STEP = 10

MXU_COLS = 256
SUBLANES = 8
CONV_ROW_BLOCK = 32
WGRAD_ROW_BLOCK = 32
SHIFT_ROWS = HALO - SUBLANES
MIB = 1024 * 1024

HBM = pl.BlockSpec(memory_space=pltpu.HBM)
ANY = pl.BlockSpec(memory_space=pl.ANY)
MESH = pl.DeviceIdType.MESH


def _params(vmem_mib, grid_dims=0):
    kw = dict(vmem_limit_bytes=vmem_mib * MIB)
    if grid_dims:
        kw["dimension_semantics"] = ("arbitrary",) * grid_dims
    return pltpu.CompilerParams(**kw)


def _full(shape):
    return pl.BlockSpec(shape, lambda i: (0,) * len(shape))


def _dot(a, b):
    return jnp.dot(a, b, preferred_element_type=F32)


def _dot_nt(a, b):
    return lax.dot_general(a, b, (((1,), (1,)), ((), ())), preferred_element_type=F32)


def _dot_tn(a, b):
    return lax.dot_general(a, b, (((0,), (0,)), ((), ())), preferred_element_type=F32)


def _normal_cdf(x):
    return 0.5 * (1.0 + lax.erf(x * INV_SQRT2))


def _gelu_grad(x, cdf):
    return cdf + x * jnp.exp(-0.5 * x * x) * INV_SQRT_2PI


def _ln_fwd(v):
    mu = jnp.mean(v, axis=-1, keepdims=True)
    d = v - mu
    var = jnp.mean(d * d, axis=-1, keepdims=True)
    rstd = lax.rsqrt(var + LN_EPS)
    return d * rstd, rstd


def _ln_bwd(dyhat, yhat, rstd):
    m1 = jnp.mean(dyhat, axis=-1, keepdims=True)
    m2 = jnp.mean(dyhat * yhat, axis=-1, keepdims=True)
    return rstd * (dyhat - m1 - yhat * m2)


def _colsum(v):
    return jnp.sum(v, axis=0, keepdims=True)


def _head_pair_stack(v, lo):
    return jnp.concatenate([jnp.where(lo, v, 0.0), jnp.where(lo, 0.0, v)], axis=0).astype(BF16)


def _lo_mask():
    return lax.broadcasted_iota(jnp.int32, (CHUNK, CHUNK), 1) < (CHUNK // 2)


def _head_selector():
    head = lax.broadcasted_iota(jnp.int32, (N_HEADS, D_SGU), 0)
    lane = lax.broadcasted_iota(jnp.int32, (N_HEADS, D_SGU), 1)
    width = D_SGU // N_HEADS
    return ((lane >= head * width) & (lane < (head + 1) * width)).astype(F32)


def _shifted_copies(pad_ref, sh_ref, rows):
    for r in range(1, SUBLANES):
        sh_ref[r - 1, 0:rows, :] = pad_ref[pl.ds(r, rows), :]


def _tap_groups(offset_of_tap):
    groups = {}
    for k in range(CONV_WIDTH):
        o = offset_of_tap(k)
        groups.setdefault(o % SUBLANES, []).append((k, o // SUBLANES))
    return groups


def _tap_window(pad_ref, sh_ref, r, taps, row0, rows):
    q0 = min(q for _, q in taps)
    q1 = max(q for _, q in taps)
    src = pad_ref if r == 0 else sh_ref.at[r - 1]
    win = src[pl.ds(row0 + SUBLANES * q0, SUBLANES * (q1 - q0) + rows), :]
    return win, [(k, SUBLANES * (q - q0)) for k, q in taps]


def _causal_conv(pad_ref, sh_ref, w_ref, out_ref, rows, offset_of_tap, bias=None):
    groups = _tap_groups(offset_of_tap)

    def block(b, carry):
        row0 = pl.multiple_of(b * CONV_ROW_BLOCK, CONV_ROW_BLOCK)
        if bias is None:
            acc = jnp.zeros((CONV_ROW_BLOCK, D_CONV), F32)
        else:
            acc = jnp.broadcast_to(bias, (CONV_ROW_BLOCK, D_CONV))
        for r, taps in groups.items():
            win, starts = _tap_window(pad_ref, sh_ref, r, taps, row0, CONV_ROW_BLOCK)
            for k, s in starts:
                acc = acc + w_ref[k:k + 1, :] * win[s:s + CONV_ROW_BLOCK, :]
        out_ref[pl.ds(row0, CONV_ROW_BLOCK), :] = acc
        return carry

    lax.fori_loop(0, rows // CONV_ROW_BLOCK, block, 0)


def _conv_weight_grad(dy_ref, pad_ref, sh_ref, acc_ref, rows, offset_of_tap):
    groups = _tap_groups(offset_of_tap)
    for r, taps in groups.items():

        def block(b, parts, r=r, taps=taps):
            row0 = pl.multiple_of(b * WGRAD_ROW_BLOCK, WGRAD_ROW_BLOCK)
            dyb = dy_ref[pl.ds(row0, WGRAD_ROW_BLOCK), :]
            win, starts = _tap_window(pad_ref, sh_ref, r, taps, row0, WGRAD_ROW_BLOCK)
            out = []
            for part, (_, s) in zip(parts, starts):
                pr = dyb * win[s:s + WGRAD_ROW_BLOCK, :]
                out.append(part + pr.reshape(WGRAD_ROW_BLOCK // SUBLANES, SUBLANES, D_CONV).sum(axis=0))
            return tuple(out)

        zeros = tuple(jnp.zeros((SUBLANES, D_CONV), F32) for _ in taps)
        parts = lax.fori_loop(0, rows // WGRAD_ROW_BLOCK, block, zeros)
        for part, (k, _) in zip(parts, taps):
            acc_ref[k] += part


def _prep_weights(w_in, w_out, w_gate_t, w_up_t, w_down, conv_w, w_s, b_s):
    def compute(ins, outs):
        win_ref, wout_ref, wgt_ref, wut_ref, wd_ref, cw_ref, ws_ref, bs_ref = ins
        win_o, wout_o, wgt_o, wut_o, wd_o, cw_o, wcat_o, wcatt_o, bsf_o = outs
        win_o[...] = win_ref[...].T.astype(BF16)
        wout_o[...] = wout_ref[...].astype(BF16)
        wgt_o[...] = wgt_ref[...].astype(BF16)
        wut_o[...] = wut_ref[...].astype(BF16)
        wd_o[...] = wd_ref[...].astype(BF16)
        cw_o[...] = jnp.zeros(cw_o.shape, F32)
        cw_o[0:CONV_WIDTH, 0:D_CONV // N_DEV] = cw_ref[...]
        row = lax.broadcasted_iota(jnp.int32, (CHUNK, CHUNK), 0)
        col = lax.broadcasted_iota(jnp.int32, (CHUNK, CHUNK), 1)
        causal = row >= col
        for h in range(N_HEADS):
            w = jnp.where(causal, ws_ref[h], 0.0)
            p, half = h // 2, (h % 2) * CHUNK
            wcat_o[p, :, half:half + CHUNK] = w.astype(BF16)
            wcatt_o[p, :, half:half + CHUNK] = w.T.astype(BF16)
        bsf_o[...] = lax.dot_general(bs_ref[...], _head_selector(), (((0,), (0,)), ((), ())),
                                     preferred_element_type=F32, precision=lax.Precision.HIGHEST)

    S = jax.ShapeDtypeStruct
    out_shapes = [S((256, D_MODEL), BF16), S((128, D_MODEL), BF16), S((FF_SHARD, D_MODEL), BF16),
                  S((FF_SHARD, D_MODEL), BF16), S((FF_SHARD, D_MODEL), BF16), S((CONV_ROWS, 128), F32),
                  S((4, CHUNK, 2 * CHUNK), BF16), S((4, CHUNK, 2 * CHUNK), BF16), S((CHUNK, D_SGU), F32)]
    (res,), _ = _staged_call(
        "prep_weights", [([w_in, w_out, w_gate_t, w_up_t, w_down, conv_w, w_s, b_s], out_shapes, compute)], 32)
    return res


def _mesh_position():
    x, y, c = lax.axis_index("x"), lax.axis_index("y"), lax.axis_index("c")
    return x, y, c


def _peers(x, y, c):
    out = []
    for k in range(1, N_DEV):
        px = 1 - x if (k >> 2) & 1 else x
        py = 1 - y if (k >> 1) & 1 else y
        pc = 1 - c if k & 1 else c
        out.append(((px, py, pc), 4 * px + 2 * py + pc))
    return out


class _Exchange:
    def __init__(self, scatter, gather):
        self.arrays = list(scatter) + list(gather)
        self.n_sc = len(scatter)
        self.n = len(self.arrays)
        self.out_shape = [jax.ShapeDtypeStruct(a.shape if k < self.n_sc else (N_DEV,) + a.shape, a.dtype)
                          for k, a in enumerate(self.arrays)]
        n_remote = self.n * (N_DEV - 1)
        self.scratch = [pltpu.SemaphoreType.DMA((n_remote,)), pltpu.SemaphoreType.DMA((n_remote,)),
                        pltpu.SemaphoreType.DMA((self.n,))] if self.n else []

    def _copies(self, src, dst, sems):
        send_sems, recv_sems, local_sems = sems
        x, y, c = _mesh_position()
        me = 4 * x + 2 * y + c
        locals_, first, arrivals, passed, last = [], [], [], [], []

        def remote(a, k, src_ref, slot, to):
            s = a * (N_DEV - 1) + k
            return pltpu.make_async_remote_copy(src_ref=src_ref, dst_ref=dst[a].at[slot], send_sem=send_sems.at[s],
                                                recv_sem=recv_sems.at[s], device_id=to, device_id_type=MESH)

        for a in range(self.n):
            if a < self.n_sc:
                locals_.append(pltpu.make_async_copy(src[a].at[me], dst[a].at[me], local_sems.at[a]))
                for k, (peer, pid) in enumerate(_peers(x, y, c)):
                    first.append(remote(a, k, src[a].at[pid], me, peer))
                    last.append(remote(a, k, src[a].at[pid], pid, peer))
                continue
            locals_.append(pltpu.make_async_copy(src[a], dst[a].at[me], local_sems.at[a]))
            sibling, sib_id = (x, y, 1 - c), 4 * x + 2 * y + (1 - c)
            chips = [(1 - x, y), (x, 1 - y), (1 - x, 1 - y)]
            first.append(remote(a, 0, src[a], me, sibling))
            last.append(remote(a, 0, src[a], sib_id, sibling))
            for j, (px, py) in enumerate(chips):
                same, other = 4 * px + 2 * py + c, 4 * px + 2 * py + (1 - c)
                first.append(remote(a, 1 + j, src[a], me, (px, py, c)))
                arrivals.append(remote(a, 1 + j, src[a], same, (px, py, c)))
                passed.append(remote(a, 4 + j, dst[a].at[same], same, sibling))
                last.append(remote(a, 4 + j, dst[a].at[other], other, sibling))
        return locals_, first, arrivals, passed, last

    def start(self, src, dst, sems):
        if not self.n:
            return
        locals_, first, _, _, _ = self._copies(src, dst, sems)
        for cp in locals_ + first:
            cp.start()

    def forward(self, src, dst, sems):
        if self.n == self.n_sc:
            return
        _, _, arrivals, passed, _ = self._copies(src, dst, sems)
        for arrived, cp in zip(arrivals, passed):
            arrived.wait_recv()
            cp.start()

    def wait(self, src, dst, sems):
        if not self.n:
            return
        locals_, first, _, passed, last = self._copies(src, dst, sems)
        for cp in last:
            cp.wait_recv()
        for cp in first + passed:
            cp.wait_send()
        for cp in locals_:
            cp.wait()


def _exchange(name, scatter, gather):
    ex = _Exchange(scatter, gather)
    n = ex.n

    def body(*refs):
        src, dst, sems = refs[:n], refs[n:2 * n], refs[2 * n:]
        ex.start(src, dst, sems)
        ex.forward(src, dst, sems)
        ex.wait(src, dst, sems)

    return pl.pallas_call(
        body, name=name, out_shape=tuple(ex.out_shape), in_specs=[ANY] * n, out_specs=(ANY,) * n,
        scratch_shapes=ex.scratch,
    )(*ex.arrays)


def _forward_step(n_steps):
    return (5 * n_steps) // 8


def _hosted(ex, refs, n_in, n_out):
    ins, ex_src = refs[:n_in], refs[n_in:n_in + ex.n]
    rest = refs[n_in + ex.n:]
    outs, ex_dst = rest[:n_out], rest[n_out:n_out + ex.n]
    rest = rest[n_out + ex.n:]
    n_own = len(rest) - len(ex.scratch)
    return ins, outs, rest[:n_own], (ex_src, ex_dst, rest[n_own:])


def _fwd_mix(x, win_g, wout_g, sgu_g, sgu_b, wcat, bs_full, cw, cb, cg, cbeta, tm, ex):
    T = x.shape[0]
    nt = T // tm

    def body(*refs):
        ins, outs, scratch, ex_refs = _hosted(ex, refs, 11, 6)
        x_ref, win_ref, wout_ref, sg_ref, sb_ref, wcat_ref, bs_ref, cw_ref, cb_ref, cg_ref, cbeta_ref = ins
        proj_ref, ycat_ref, n1_ref, rstd1_ref, phi_ref, y_ref = outs
        hpad, hshift = scratch
        i = pl.program_id(0)

        @pl.when(i == 0)
        def _():
            ex.start(*ex_refs)

        xf = x_ref[...]
        xb = xf.astype(BF16)
        proj_ref[...] = _dot_nt(xb, win_ref[...])
        cdf = _normal_cdf(proj_ref[:, 0:2 * D_SGU])
        phi_ref[...] = cdf.astype(BF16)
        u = proj_ref[:, 0:D_SGU] * cdf[:, 0:D_SGU]
        v = proj_ref[:, D_SGU:2 * D_SGU] * cdf[:, D_SGU:2 * D_SGU]
        vhat, _ = _ln_fwd(v)
        vn = vhat * sg_ref[...] + sb_ref[...]
        lo = _lo_mask()
        for c in range(tm // CHUNK):
            rows = slice(CHUNK * c, CHUNK * (c + 1))
            for p in range(4):
                lanes = slice(CHUNK * p, CHUNK * (p + 1))
                mixed = _dot(wcat_ref[p], _head_pair_stack(vn[rows, lanes], lo)) + bs_ref[:, lanes]
                ycat_ref[rows, lanes] = (u[rows, lanes] * mixed).astype(BF16)
        base = 2 * D_SGU
        a = proj_ref[:, base:base + D_CONV]
        g = proj_ref[:, base + D_CONV:base + 2 * D_CONV]

        @pl.when(i == 0)
        def _():
            hpad[0:HALO, :] = jnp.zeros((HALO, D_CONV), F32)

        hpad[HALO:HALO + tm, :] = a * jax.nn.sigmoid(g)
        _shifted_copies(hpad, hshift, tm + SHIFT_ROWS)
        _causal_conv(hpad, hshift, cw_ref, y_ref, tm, lambda k: HALO - (CONV_WIDTH - 1) + k, bias=cb_ref[...])
        hpad[0:HALO, :] = hpad[tm:tm + HALO, :]
        yhat, _ = _ln_fwd(y_ref[...])
        yn = yhat * cg_ref[...] + cbeta_ref[...]
        ycat_ref[:, D_SGU:D_SGU + D_CONV] = (yn * jax.nn.sigmoid(yn)).astype(BF16)
        r1 = ALPHA * xf + _dot(ycat_ref[...], wout_ref[...])
        n1, rstd1 = _ln_fwd(r1)
        n1_ref[...] = n1
        rstd1_ref[...] = jnp.broadcast_to(rstd1, rstd1_ref.shape)

        @pl.when(i == _forward_step(nt))
        def _():
            ex.forward(*ex_refs)

        @pl.when(i == nt - 1)
        def _():
            ex.wait(*ex_refs)

    S = jax.ShapeDtypeStruct
    row = lambda w: pl.BlockSpec((tm, w), lambda i: (i, 0))
    res = pl.pallas_call(
        body, name="fwd_mix", grid=(nt,),
        in_specs=[row(D_MODEL), _full(win_g.shape), _full(wout_g.shape), _full(sgu_g.shape), _full(sgu_b.shape),
                  _full(wcat.shape), _full(bs_full.shape), _full(cw.shape), _full(cb.shape), _full(cg.shape),
                  _full(cbeta.shape)] + [ANY] * ex.n,
        out_specs=(row(2 * D_MODEL), row(D_MODEL), row(D_MODEL), row(CHUNK), row(2 * D_SGU), row(D_CONV)) + (ANY,) * ex.n,
        out_shape=(S((T, 2 * D_MODEL), F32), S((T, D_MODEL), BF16), S((T, D_MODEL), F32), S((T, CHUNK), F32),
                   S((T, 2 * D_SGU), BF16), S((T, D_CONV), F32), *ex.out_shape),
        scratch_shapes=[pltpu.VMEM((tm + HALO, D_CONV), F32),
                        pltpu.VMEM((SUBLANES - 1, tm + SHIFT_ROWS, D_CONV), F32)] + ex.scratch,
        compiler_params=_params(56, 1),
    )(x, win_g, wout_g, sgu_g, sgu_b, wcat, bs_full, cw, cb, cg, cbeta, *ex.arrays)
    return res[:6], res[6:]


def _load_resident(pairs, sems):
    cps = [pltpu.make_async_copy(s, d, sems.at[k]) for k, (s, d) in enumerate(pairs)]
    for cp in cps:
        cp.start()
    for cp in cps:
        cp.wait()


def _fwd_mlp(n1, tgt, l1g, l1b, l2g, l2b, wgt, wut, wd, tm):
    T = n1.shape[0]
    nt = T // tm
    nf = D_FF // MXU_COLS

    def body(n1_ref, tgt_ref, l1g_ref, l1b_ref, l2g_ref, l2b_ref, wg_hbm, wu_hbm, wd_hbm,
             gate_ref, up_ref, hh_ref, dr2_ref, stat_ref, wg_s, wu_s, wd_s, sems):
        i = pl.program_id(0)

        @pl.when(i == 0)
        def _():
            _load_resident([(wg_hbm, wg_s), (wu_hbm, wu_s), (wd_hbm, wd_s)], sems)
            stat_ref[...] = jnp.zeros(stat_ref.shape, F32)

        x1 = n1_ref[...] * l1g_ref[...] + l1b_ref[...]
        x1b = x1.astype(BF16)
        for f in range(nf):
            cols = slice(MXU_COLS * f, MXU_COLS * (f + 1))
            gt = _dot_nt(x1b, wg_s[cols, :])
            ut = _dot_nt(x1b, wu_s[cols, :])
            gate_ref[:, cols] = gt.astype(BF16)
            up_ref[:, cols] = ut.astype(BF16)
            hh_ref[:, cols] = (gt * jax.nn.sigmoid(gt) * ut).astype(BF16)
        r2 = ALPHA * x1 + _dot(hh_ref[...], wd_s[...])
        n2, rstd2 = _ln_fwd(r2)
        x2 = n2 * l2g_ref[...] + l2b_ref[...]
        diff = x2 - tgt_ref[...]
        dx2 = diff * (1.0 / D_MODEL)
        stat_ref[0:1, :] += _colsum(diff * diff)
        stat_ref[1:2, :] += _colsum(dx2 * n2)
        stat_ref[2:3, :] += _colsum(dx2)
        dr2_ref[...] = _ln_bwd(dx2 * l2g_ref[...], n2, rstd2)

    S = jax.ShapeDtypeStruct
    row = lambda w: pl.BlockSpec((tm, w), lambda i: (i, 0))
    vec = _full((1, D_MODEL))
    return pl.pallas_call(
        body, name="fwd_mlp", grid=(nt,),
        in_specs=[row(D_MODEL), row(D_MODEL), vec, vec, vec, vec, ANY, ANY, ANY],
        out_specs=(row(D_FF), row(D_FF), row(D_FF), row(D_MODEL), _full((8, D_MODEL))),
        out_shape=(S((T, D_FF), BF16), S((T, D_FF), BF16), S((T, D_FF), BF16), S((T, D_MODEL), F32),
                   S((8, D_MODEL), F32)),
        scratch_shapes=[pltpu.VMEM((D_FF, D_MODEL), BF16)] * 3 + [pltpu.SemaphoreType.DMA((3,))],
        compiler_params=_params(56, 1),
    )(n1, tgt, l1g, l1b, l2g, l2b, wgt, wut, wd)


def _bwd_mlp(dr2, gate, up, n1, rstd1, l1g, wgt, wut, wd, tm):
    T = n1.shape[0]
    nt = T // tm
    nf = D_FF // MXU_COLS

    def body(dr2_ref, gate_ref, up_ref, n1_ref, rstd1_ref, l1g_ref, wg_hbm, wu_hbm, wd_hbm,
             dgate_ref, dup_ref, dr1_ref, stat_ref, wg_s, wu_s, wd_s, sems):
        i = pl.program_id(0)

        @pl.when(i == 0)
        def _():
            _load_resident([(wg_hbm, wg_s), (wu_hbm, wu_s), (wd_hbm, wd_s)], sems)
            stat_ref[...] = jnp.zeros(stat_ref.shape, F32)

        dr2 = dr2_ref[...]
        dr2b = dr2.astype(BF16)
        for f in range(nf):
            cols = slice(MXU_COLS * f, MXU_COLS * (f + 1))
            dhh = _dot_nt(dr2b, wd_s[cols, :])
            gt = gate_ref[:, cols].astype(F32)
            ut = up_ref[:, cols].astype(F32)
            sg = jax.nn.sigmoid(gt)
            dgate_ref[:, cols] = (dhh * ut * (sg * (1.0 + gt * (1.0 - sg)))).astype(BF16)
            dup_ref[:, cols] = (dhh * (gt * sg)).astype(BF16)
        dx1 = ALPHA * dr2 + _dot(dgate_ref[...], wg_s[...]) + _dot(dup_ref[...], wu_s[...])
        n1 = n1_ref[...]
        stat_ref[0:1, :] += _colsum(dx1 * n1)
        stat_ref[1:2, :] += _colsum(dx1)
        dr1_ref[...] = _ln_bwd(dx1 * l1g_ref[...], n1, rstd1_ref[:, 0:1])

    S = jax.ShapeDtypeStruct
    row = lambda w: pl.BlockSpec((tm, w), lambda i: (i, 0))
    return pl.pallas_call(
        body, name="bwd_mlp", grid=(nt,),
        in_specs=[row(D_MODEL), row(D_FF), row(D_FF), row(D_MODEL), row(CHUNK), _full((1, D_MODEL)), ANY, ANY, ANY],
        out_specs=(row(D_FF), row(D_FF), row(D_MODEL), _full((8, D_MODEL))),
        out_shape=(S((T, D_FF), BF16), S((T, D_FF), BF16), S((T, D_MODEL), F32), S((8, D_MODEL), F32)),
        scratch_shapes=[pltpu.VMEM((D_FF, D_MODEL), BF16)] * 3 + [pltpu.SemaphoreType.DMA((3,))],
        compiler_params=_params(56, 1),
    )(dr2, gate, up, n1, rstd1, l1g, wgt, wut, wd)


def _bwd_mix(dr1, proj, phi, y, win_g, wout_g, sgu_g, sgu_b, wcat, wcatt, bs_full, cw, cg, cbeta, tm, ex):
    T = dr1.shape[0]
    nt = T // tm
    halo_blocks = tm // HALO

    def body(*refs):
        ins, outs, scratch, ex_refs = _hosted(ex, refs, 15, 6)
        (dr1_ref, proj_ref, halo_ref, phi_ref, y_ref, win_ref, wout_ref, sg_ref, sb_ref, wcat_ref, wcatt_ref, bs_ref,
         cw_ref, cg_ref, cbeta_ref) = ins
        gx_ref, dproj_ref, gws_out, gbs_out, gcw_ref, vec_ref = outs
        hpad, shift, dypad, dhbuf, dubuf, dvnbuf, gcw_acc, gws_ref, gbs_ref = scratch
        i = pl.program_id(0)
        tile = nt - 1 - i

        @pl.when(i == 0)
        def _():
            ex.start(*ex_refs)
            gws_ref[...] = jnp.zeros(gws_ref.shape, F32)
            gbs_ref[...] = jnp.zeros(gbs_ref.shape, F32)
            gcw_ref[...] = jnp.zeros(gcw_ref.shape, F32)
            vec_ref[...] = jnp.zeros(vec_ref.shape, F32)
            gcw_acc[...] = jnp.zeros(gcw_acc.shape, F32)
            dypad[tm:tm + HALO, :] = jnp.zeros((HALO, D_CONV), F32)

        dr1 = dr1_ref[...]
        dycat = _dot_nt(dr1.astype(BF16), wout_ref[...])
        pu = proj_ref[:, 0:D_SGU]
        pv = proj_ref[:, D_SGU:2 * D_SGU]
        cdf_u = phi_ref[:, 0:D_SGU].astype(F32)
        cdf_v = phi_ref[:, D_SGU:2 * D_SGU].astype(F32)
        u = pu * cdf_u
        vhat, rstd_v = _ln_fwd(pv * cdf_v)
        vn = vhat * sg_ref[...] + sb_ref[...]
        lo = _lo_mask()
        for c in range(tm // CHUNK):
            rows = slice(CHUNK * c, CHUNK * (c + 1))
            for p in range(4):
                lanes = slice(CHUNK * p, CHUNK * (p + 1))
                vstack = _head_pair_stack(vn[rows, lanes], lo)
                mixed = _dot(wcat_ref[p], vstack) + bs_ref[:, lanes]
                d_a = dycat[rows, lanes]
                dubuf[rows, lanes] = d_a * mixed
                dm = d_a * u[rows, lanes]
                gbs_ref[:, lanes] += dm
                dstack = _head_pair_stack(dm, lo)
                gws_ref[2 * CHUNK * p:2 * CHUNK * (p + 1), :] += _dot_nt(dstack, vn[rows, lanes].astype(BF16))
                dvnbuf[rows, lanes] = _dot(wcatt_ref[p], dstack)
        dvn = dvnbuf[...]
        vec_ref[0:1, :] += _colsum(dvn * vhat)
        vec_ref[1:2, :] += _colsum(dvn)
        dv = _ln_bwd(dvn * sg_ref[...], vhat, rstd_v)
        dproj_ref[:, 0:D_SGU] = (dubuf[...] * _gelu_grad(pu, cdf_u)).astype(BF16)
        dproj_ref[:, D_SGU:2 * D_SGU] = (dv * _gelu_grad(pv, cdf_v)).astype(BF16)
        base = 2 * D_SGU
        a = proj_ref[:, base:base + D_CONV]
        sgm = jax.nn.sigmoid(proj_ref[:, base + D_CONV:base + 2 * D_CONV])
        h_before = halo_ref[:, 0:D_CONV] * jax.nn.sigmoid(halo_ref[:, D_CONV:2 * D_CONV])
        hpad[0:HALO, :] = jnp.where(tile > 0, h_before, 0.0)
        hpad[HALO:HALO + tm, :] = a * sgm
        _shifted_copies(hpad, shift, tm + SHIFT_ROWS)
        h_offset = lambda k: HALO - (CONV_WIDTH - 1) + k
        yhat, rstd_y = _ln_fwd(y_ref[...])
        yn = yhat * cg_ref[...] + cbeta_ref[...]
        s = jax.nn.sigmoid(yn)
        dyn = dycat[:, D_SGU:D_SGU + D_CONV] * (s * (1.0 + yn * (1.0 - s)))
        vec_ref[3:4, :] += _colsum(dyn * yhat)
        vec_ref[4:5, :] += _colsum(dyn)
        dy = _ln_bwd(dyn * cg_ref[...], yhat, rstd_y)
        vec_ref[2:3, :] += _colsum(dy)
        dypad[0:tm, :] = dy
        _conv_weight_grad(dypad, hpad, shift, gcw_acc, tm, h_offset)
        _shifted_copies(dypad, shift, tm + SHIFT_ROWS)
        _causal_conv(dypad, shift, cw_ref, dhbuf, tm, lambda k: (CONV_WIDTH - 1) - k)
        dypad[tm:tm + HALO, :] = dypad[0:HALO, :]
        dh = dhbuf[...]
        dproj_ref[:, base:base + D_CONV] = (dh * sgm).astype(BF16)
        dproj_ref[:, base + D_CONV:base + 2 * D_CONV] = (dh * a * sgm * (1.0 - sgm)).astype(BF16)
        gx_ref[...] = ALPHA * dr1 + _dot(dproj_ref[...], win_ref[...])

        @pl.when(i == _forward_step(nt))
        def _():
            ex.forward(*ex_refs)

        @pl.when(i == nt - 1)
        def _():
            gcw_ref[...] = gcw_acc[...].sum(axis=1)
            gws_out[...] = gws_ref[...].astype(BF16)
            gbs_out[...] = lax.dot_general(_head_selector(), gbs_ref[...], (((1,), (1,)), ((), ())),
                                           preferred_element_type=F32, precision=lax.Precision.HIGHEST)
            ex.wait(*ex_refs)

    S = jax.ShapeDtypeStruct
    row = lambda w: pl.BlockSpec((tm, w), lambda i: (nt - 1 - i, 0))
    halo = pl.BlockSpec((HALO, D_MODEL), lambda i: (jnp.maximum((nt - 1 - i) * halo_blocks - 1, 0), 1))
    res = pl.pallas_call(
        body, name="bwd_mix", grid=(nt,),
        in_specs=[row(D_MODEL), row(2 * D_MODEL), halo, row(2 * D_SGU), row(D_CONV), _full(win_g.shape),
                  _full(wout_g.shape), _full(sgu_g.shape), _full(sgu_b.shape), _full(wcat.shape), _full(wcatt.shape),
                  _full(bs_full.shape), _full(cw.shape), _full(cg.shape), _full(cbeta.shape)]
        + [ANY] * ex.n,
        out_specs=(row(D_MODEL), row(2 * D_MODEL), _full((N_HEADS * CHUNK, CHUNK)), _full((N_HEADS, CHUNK)),
                   _full((CONV_ROWS, D_CONV)), _full((8, D_CONV))) + (ANY,) * ex.n,
        out_shape=(S((T, D_MODEL), F32), S((T, 2 * D_MODEL), BF16), S((N_HEADS * CHUNK, CHUNK), BF16),
                   S((N_HEADS, CHUNK), F32), S((CONV_ROWS, D_CONV), F32), S((8, D_CONV), F32), *ex.out_shape),
        scratch_shapes=[pltpu.VMEM((tm + HALO, D_CONV), F32), pltpu.VMEM((SUBLANES - 1, tm + SHIFT_ROWS, D_CONV), F32),
                        pltpu.VMEM((tm + HALO, D_CONV), F32),
                        pltpu.VMEM((tm, D_CONV), F32), pltpu.VMEM((tm, D_SGU), F32),
                        pltpu.VMEM((tm, D_SGU), F32), pltpu.VMEM((CONV_ROWS, 8, D_CONV), F32),
                        pltpu.VMEM((N_HEADS * CHUNK, CHUNK), F32), pltpu.VMEM((CHUNK, D_SGU), F32)] + ex.scratch,
        compiler_params=_params(56, 1),
    )(dr1, proj, proj, phi, y, win_g, wout_g, sgu_g, sgu_b, wcat, wcatt, bs_full, cw, cg, cbeta, *ex.arrays)
    return res[:6], res[6:]


def _wgrad(name, a, b, blocks, tk, ex=None, b_cols=None, b_affine=None):
    T, M = a.shape
    col, N = b_cols or (0, b.shape[1])
    nk = T // tk
    out_shape = (blocks, M // blocks, N)
    ex = ex or _Exchange([], [])
    affine = list(b_affine or [])

    def body(*refs):
        ins, (o_ref,), (acc,), ex_refs = _hosted(ex, refs, 2 + len(affine), 1)
        a_ref, b_ref = ins[:2]
        i = pl.program_id(0)

        @pl.when(i == 0)
        def _():
            ex.start(*ex_refs)
            acc[...] = jnp.zeros(acc.shape, F32)

        right = b_ref[...]
        if affine:
            right = right * ins[2][...] + ins[3][...]
        acc[...] += _dot_tn(a_ref[...].astype(BF16), right.astype(BF16))

        @pl.when(i == _forward_step(nk))
        def _():
            ex.forward(*ex_refs)

        @pl.when(i == nk - 1)
        def _():
            o_ref[...] = acc[...].astype(BF16)
            ex.wait(*ex_refs)

    res = pl.pallas_call(
        body, name=name, grid=(nk,),
        in_specs=[pl.BlockSpec((tk, M), lambda i: (i, 0)), pl.BlockSpec((tk, N), lambda i: (i, col))]
        + [_full((1, N))] * len(affine) + [ANY] * ex.n,
        out_specs=(_full((M, N)),) + (ANY,) * ex.n,
        out_shape=(jax.ShapeDtypeStruct((M, N), BF16), *ex.out_shape),
        scratch_shapes=[pltpu.VMEM((M, N), F32)] + ex.scratch,
        compiler_params=_params(56, 1),
    )(a, b, *affine, *ex.arrays)
    g = res[0].reshape(out_shape)
    return (g, res[1:]) if ex.n else g


def _adamw(w, g, m, v):
    m2 = ADAM_B1 * m + (1.0 - ADAM_B1) * g
    v2 = ADAM_B2 * v + (1.0 - ADAM_B2) * (g * g)
    m_hat = m2 / (1.0 - ADAM_B1 ** ADAM_STEP)
    v_hat = v2 / (1.0 - ADAM_B2 ** ADAM_STEP)
    delta = -ADAM_LR * (m_hat / (jnp.sqrt(v_hat) + ADAM_EPS) + ADAM_WD * w)
    return delta, m2, v2


def _sum_partials(r_ref):
    g = r_ref[0].astype(F32)
    for s in range(1, N_DEV):
        g = g + r_ref[s].astype(F32)
    return g


def _staged_call(name, groups, vmem_mib, ex=None):
    ex = ex or _Exchange([], [])
    inputs = [a for ins, _, _ in groups for a in ins]
    out_shapes = [s for _, outs, _ in groups for s in outs]
    n_in, n_out = len(inputs), len(out_shapes)

    def body(*refs):
        ins, outs, scratch, ex_refs = _hosted(ex, refs, n_in, n_out)
        in_bufs, out_bufs, sems = scratch[:n_in], scratch[n_in:n_in + n_out], scratch[n_in + n_out]
        ex.start(*ex_refs)
        loads = [pltpu.make_async_copy(ins[k], in_bufs[k], sems.at[k]) for k in range(n_in)]
        stores = [pltpu.make_async_copy(out_bufs[k], outs[k], sems.at[n_in + k]) for k in range(n_out)]
        for cp in loads:
            cp.start()
        i0 = o0 = 0
        for g_ins, g_outs, compute in groups:
            i1, o1 = i0 + len(g_ins), o0 + len(g_outs)
            for cp in loads[i0:i1]:
                cp.wait()
            compute(in_bufs[i0:i1], out_bufs[o0:o1])
            for cp in stores[o0:o1]:
                cp.start()
            i0, o0 = i1, o1
        for cp in stores:
            cp.wait()
        ex.forward(*ex_refs)
        ex.wait(*ex_refs)

    scratch = ([pltpu.VMEM(a.shape, a.dtype) for a in inputs] + [pltpu.VMEM(s.shape, s.dtype) for s in out_shapes]
               + [pltpu.SemaphoreType.DMA((n_in + n_out,))] + ex.scratch)
    res = pl.pallas_call(
        body, name=name, out_shape=(*[pltpu.HBM(s.shape, s.dtype) for s in out_shapes], *ex.out_shape),
        in_specs=[HBM] * n_in + [ANY] * ex.n, out_specs=(HBM,) * n_out + (ANY,) * ex.n,
        scratch_shapes=scratch, compiler_params=_params(vmem_mib),
    )(*[pltpu.with_memory_space_constraint(a, pltpu.HBM) for a in inputs], *ex.arrays)
    per_group, o0 = [], 0
    for _, g_outs, _ in groups:
        per_group.append(list(res[o0:o0 + len(g_outs)]))
        o0 += len(g_outs)
    return per_group, res[n_out:]


def _adamw_shard_group(parts, w, m, v, transposed):
    n = len(parts)

    def compute(ins, outs):
        w_ref, m_ref, v_ref = ins[n:]
        lo = 0
        for r_ref in ins[:n]:
            g = _sum_partials(r_ref)
            cols = g.shape[1]
            if transposed:
                g, at = g.T, (slice(lo, lo + cols), slice(None))
            else:
                at = (slice(None), slice(lo, lo + cols))
            delta, m2, v2 = _adamw(w_ref[at], g, m_ref[at], v_ref[at])
            for o, val in zip(outs, (g, delta, m2, v2)):
                o[at] = val
            lo += cols

    return [*parts, w, m, v], [jax.ShapeDtypeStruct(w.shape, F32)] * 4, compute


SMALL_NAMES = ["sgu_ln_g", "sgu_ln_b", "w_s", "b_s", "conv_b", "conv_ln_g", "conv_ln_b", "ln1_g", "ln1_b", "ln2_g", "ln2_b",
               "conv_w"]


def _finish_small_group(gws8, gbs8, gcw8, vmix8, vmlp8, vout8, small):
    names = SMALL_NAMES
    flat = []
    for n in names:
        flat += list(small[n])
    cw_block = D_CONV // N_DEV

    def compute(ins, outs):
        gws_ref, gbs_ref, gcw_ref, vmix_ref, vmlp_ref, vout_ref = ins[:6]
        wmv = ins[6:]
        loss_o = outs[0]
        outs = outs[1:]
        gws = _sum_partials(gws_ref)
        gbs = _sum_partials(gbs_ref)
        vmix = _sum_partials(vmix_ref)
        vmlp = _sum_partials(vmlp_ref)
        vout = _sum_partials(vout_ref)
        x, y, c = _mesh_position()
        first = (4 * x + 2 * y + c) * cw_block
        pick = (lax.broadcasted_iota(jnp.int32, (D_CONV, cw_block), 0)
                == first + lax.broadcasted_iota(jnp.int32, (D_CONV, cw_block), 1)).astype(F32)
        gcw = jnp.dot(_sum_partials(gcw_ref), pick, preferred_element_type=F32,
                      precision=lax.Precision.HIGHEST)[0:CONV_WIDTH, :]
        loss = (0.5 / D_MODEL) * jnp.sum(vout[0:1, :], axis=1, keepdims=True)
        loss_o[...] = jnp.broadcast_to(loss, loss_o.shape)
        rows = lax.broadcasted_iota(jnp.int32, (N_HEADS * CHUNK, CHUNK), 0)
        cols = lax.broadcasted_iota(jnp.int32, (N_HEADS * CHUNK, CHUNK), 1)
        gws = jnp.where((rows & (CHUNK - 1)) >= cols, gws, 0.0)
        grads = {
            "sgu_ln_g": vmix[0:1, :], "sgu_ln_b": vmix[1:2, :], "w_s": gws, "b_s": gbs,
            "conv_b": vmix[2:3, :], "conv_ln_g": vmix[3:4, :], "conv_ln_b": vmix[4:5, :],
            "ln1_g": vmlp[0:1, :], "ln1_b": vmlp[1:2, :], "ln2_g": vout[1:2, :], "ln2_b": vout[2:3, :],
            "conv_w": gcw,
        }
        for k, n in enumerate(names):
            w_ref, m_ref, v_ref = wmv[3 * k:3 * k + 3]
            g = grads[n]
            delta, m2, v2 = _adamw(w_ref[...], g, m_ref[...], v_ref[...])
            outs[4 * k][...] = g
            outs[4 * k + 1][...] = delta
            outs[4 * k + 2][...] = m2
            outs[4 * k + 3][...] = v2

    S = jax.ShapeDtypeStruct
    out_shape = [S((SUBLANES, 128), F32)]
    for n in names:
        out_shape += [S(small[n][0].shape, F32)] * 4
    return [gws8, gbs8, gcw8, vmix8, vmlp8, vout8, *flat], out_shape, compute


TOKEN_TILE_FWD_MIX = 512
TOKEN_TILE_BWD_MIX = 512
TOKEN_TILE_FWD_MLP = 512
TOKEN_TILE_BWD_MLP = 512
TOKEN_TILE_WGRAD = 1024


def kernel(x, w_in, sgu_ln_g, sgu_ln_b, w_s, b_s, conv_w, conv_b, conv_ln_g, conv_ln_b, w_out, ln1_g, ln1_b, w_gate, w_up, w_down, ln2_g, ln2_b, loss_target, m_w_in, m_sgu_ln_g, m_sgu_ln_b, m_w_s, m_b_s, m_conv_w, m_conv_b, m_conv_ln_g, m_conv_ln_b, m_w_out, m_ln1_g, m_ln1_b, m_w_gate, m_w_up, m_w_down, m_ln2_g, m_ln2_b, v_w_in, v_sgu_ln_g, v_sgu_ln_b, v_w_s, v_b_s, v_conv_w, v_conv_b, v_conv_ln_g, v_conv_ln_b, v_w_out, v_ln1_g, v_ln1_b, v_w_gate, v_w_up, v_w_down, v_ln2_g, v_ln2_b):
    xs = x[0]
    tgt = loss_target[0]

    (win_b, wout_b, wgt_b, wut_b, wd_b, cw_b, wcat, wcatt, bs_full) = _prep_weights(
        w_in[0], w_out[0], w_gate[0].T, w_up[0].T, w_down[0], conv_w[0], w_s[0], b_s[0])
    win_g, wout_g, cw_g = _exchange("gather_mix_weights", [], [win_b, wout_b, cw_b])
    win_g = win_g.reshape(2 * D_MODEL, D_MODEL)
    wout_g = wout_g.reshape(D_MODEL, D_MODEL)
    cw = jnp.transpose(cw_g[:, :, :D_CONV // N_DEV], (1, 0, 2)).reshape(CONV_ROWS, D_CONV)

    (proj, ycat, n1, rstd1, phi, y_conv), (wgt_g, wut_g, wd_g) = _fwd_mix(
        xs, win_g, wout_g, sgu_ln_g, sgu_ln_b, wcat, bs_full, cw, conv_b, conv_ln_g, conv_ln_b, TOKEN_TILE_FWD_MIX,
        _Exchange([], [wgt_b, wut_b, wd_b]))
    wgt_g = wgt_g.reshape(D_FF, D_MODEL)
    wut_g = wut_g.reshape(D_FF, D_MODEL)
    wd_g = wd_g.reshape(D_FF, D_MODEL)
    gate, up, hh, dr2, vout = _fwd_mlp(n1, tgt, ln1_g, ln1_b, ln2_g, ln2_b, wgt_g, wut_g, wd_g, TOKEN_TILE_FWD_MLP)

    dgate, dup, dr1, vmlp = _bwd_mlp(dr2, gate, up, n1, rstd1, ln1_g, wgt_g, wut_g, wd_g, TOKEN_TILE_BWD_MLP)
    tk = TOKEN_TILE_WGRAD
    x1 = dict(b_affine=(ln1_g, ln1_b))
    g_wgt = _wgrad("wgrad_gate", dgate, n1, N_DEV, tk, **x1)
    g_wut = _wgrad("wgrad_up", dup, n1, N_DEV, tk, **x1)
    g_wd = _wgrad("wgrad_down", hh, dr2, N_DEV, tk)
    g_wout = _wgrad("wgrad_out", ycat, dr1, N_DEV, tk)
    (gx, dproj, gws, gbs, gcw, vmix), (r_wgt, r_wut, r_wd, r_wout) = _bwd_mix(
        dr1, proj, phi, y_conv, win_g, wout_g, sgu_ln_g, sgu_ln_b, wcat, wcatt, bs_full, cw, conv_ln_g, conv_ln_b,
        TOKEN_TILE_BWD_MIX, _Exchange([g_wgt, g_wut, g_wd, g_wout], []))
    half = D_MODEL // 2
    g_win_a, (gws8, gbs8, gcw8, vmix8, vmlp8, vout8) = _wgrad(
        "wgrad_in_a", dproj, xs, N_DEV, tk, _Exchange([], [gws, gbs, gcw, vmix, vmlp, vout]), b_cols=(0, half))
    g_win_b, (r_win_a,) = _wgrad("wgrad_in_b", dproj, xs, N_DEV, tk, _Exchange([g_win_a], []), b_cols=(1, half))
    (r_win_b,) = _exchange("exchange_grad_in", [g_win_b], [])
    (u_gate, u_up, u_down), _ = _staged_call(
        "adamw_mlp",
        [_adamw_shard_group([r_wgt], w_gate[0].T, m_w_gate[0].T, v_w_gate[0].T, False),
         _adamw_shard_group([r_wut], w_up[0].T, m_w_up[0].T, v_w_up[0].T, False),
         _adamw_shard_group([r_wd], w_down[0], m_w_down[0], v_w_down[0], False)], 56)
    small_in = {
        "conv_w": (conv_w[0], m_conv_w[0], v_conv_w[0]),
        "sgu_ln_g": (sgu_ln_g, m_sgu_ln_g, v_sgu_ln_g), "sgu_ln_b": (sgu_ln_b, m_sgu_ln_b, v_sgu_ln_b),
        "w_s": tuple(a.reshape(N_HEADS * CHUNK, CHUNK) for a in (w_s, m_w_s, v_w_s)),
        "b_s": (b_s[0], m_b_s[0], v_b_s[0]),
        "conv_b": (conv_b, m_conv_b, v_conv_b), "conv_ln_g": (conv_ln_g, m_conv_ln_g, v_conv_ln_g),
        "conv_ln_b": (conv_ln_b, m_conv_ln_b, v_conv_ln_b),
        "ln1_g": (ln1_g, m_ln1_g, v_ln1_g), "ln1_b": (ln1_b, m_ln1_b, v_ln1_b),
        "ln2_g": (ln2_g, m_ln2_g, v_ln2_g), "ln2_b": (ln2_b, m_ln2_b, v_ln2_b),
    }
    (u_in, u_out, fin), _ = _staged_call(
        "adamw_mix_small",
        [_adamw_shard_group([r_win_a, r_win_b], w_in[0], m_w_in[0], v_w_in[0], True),
         _adamw_shard_group([r_wout], w_out[0], m_w_out[0], v_w_out[0], False),
         _finish_small_group(gws8, gbs8, gcw8, vmix8, vmlp8, vout8, small_in)], 40)
    big = {"w_in": u_in, "w_out": u_out, "w_gate": u_gate, "w_up": u_up, "w_down": u_down}
    loss11 = fin[0]
    small = {n: fin[1 + 4 * k:5 + 4 * k] for k, n in enumerate(SMALL_NAMES)}

    shapes = {"w_s": w_s.shape, "b_s": b_s.shape, "conv_w": conv_w.shape}
    out = {}
    for n, r in big.items():
        out[n] = tuple((a.T if n in ("w_gate", "w_up") else a)[None] for a in r)
    for n, r in small.items():
        out[n] = tuple(a.reshape(shapes[n]) for a in r) if n in shapes else tuple(r)

    order = ["w_in", "sgu_ln_g", "sgu_ln_b", "w_s", "b_s", "conv_w", "conv_b", "conv_ln_g", "conv_ln_b", "w_out",
             "ln1_g", "ln1_b", "w_gate", "w_up", "w_down", "ln2_g", "ln2_b"]
    loss = loss11[0, 0]
    return (loss, gx[None], *[out[n][0] for n in order], *[out[n][1] for n in order],
            *[out[n][2] for n in order], *[out[n][3] for n in order])
```

```python
import jax
import jax.numpy as jnp
from jax import lax
from jax.experimental import pallas as pl
from jax.experimental.pallas import tpu as pltpu

F32 = jnp.float32
BF16 = jnp.bfloat16

D_MODEL = 1024
D_SGU = 512
D_CONV = 512
N_HEADS = 8
CHUNK = 128
CONV_WIDTH = 31
CONV_ROWS = 32
HALO = 32
D_FF = 2816
N_DEV = 8
FF_SHARD = D_FF // N_DEV
ALPHA = (2.0 * 1) ** 0.25
LN_EPS = 1e-5
INV_SQRT2 = 0.7071067811865476
INV_SQRT_2PI = 0.3989422804014327

ADAM_LR = 0.001
ADAM_B1 = 0.9
ADAM_B2 = 0.999
ADAM_EPS = 1e-08
ADAM_WD = 0.01
ADAM_STEP = 10

MXU_COLS = 256
SUBLANES = 8
CONV_ROW_BLOCK = 32
WGRAD_ROW_BLOCK = 32
SHIFT_ROWS = HALO - SUBLANES
MIB = 1024 * 1024

HBM = pl.BlockSpec(memory_space=pltpu.HBM)
ANY = pl.BlockSpec(memory_space=pl.ANY)
MESH = pl.DeviceIdType.MESH


def _params(vmem_mib, grid_dims=0):
    kw = dict(vmem_limit_bytes=vmem_mib * MIB)
    if grid_dims:
        kw["dimension_semantics"] = ("arbitrary",) * grid_dims
    return pltpu.CompilerParams(**kw)


def _full(shape):
    return pl.BlockSpec(shape, lambda i: (0,) * len(shape))


def _dot(a, b):
    return jnp.dot(a, b, preferred_element_type=F32)


def _dot_nt(a, b):
    return lax.dot_general(a, b, (((1,), (1,)), ((), ())), preferred_element_type=F32)


def _dot_tn(a, b):
    return lax.dot_general(a, b, (((0,), (0,)), ((), ())), preferred_element_type=F32)


def _normal_cdf(x):
    return 0.5 * (1.0 + lax.erf(x * INV_SQRT2))


def _gelu_grad(x, cdf):
    return cdf + x * jnp.exp(-0.5 * x * x) * INV_SQRT_2PI


def _ln_fwd(v):
    mu = jnp.mean(v, axis=-1, keepdims=True)
    d = v - mu
    var = jnp.mean(d * d, axis=-1, keepdims=True)
    rstd = lax.rsqrt(var + LN_EPS)
    return d * rstd, rstd


def _ln_bwd(dyhat, yhat, rstd):
    m1 = jnp.mean(dyhat, axis=-1, keepdims=True)
    m2 = jnp.mean(dyhat * yhat, axis=-1, keepdims=True)
    return rstd * (dyhat - m1 - yhat * m2)


def _colsum(v):
    return jnp.sum(v, axis=0, keepdims=True)


def _head_pair_stack(v, lo):
    return jnp.concatenate([jnp.where(lo, v, 0.0), jnp.where(lo, 0.0, v)], axis=0).astype(BF16)


def _lo_mask():
    return lax.broadcasted_iota(jnp.int32, (CHUNK, CHUNK), 1) < (CHUNK // 2)


def _head_selector():
    head = lax.broadcasted_iota(jnp.int32, (N_HEADS, D_SGU), 0)
    lane = lax.broadcasted_iota(jnp.int32, (N_HEADS, D_SGU), 1)
    width = D_SGU // N_HEADS
    return ((lane >= head * width) & (lane < (head + 1) * width)).astype(F32)


def _shifted_copies(pad_ref, sh_ref, rows):
    for r in range(1, SUBLANES):
        sh_ref[r - 1, 0:rows, :] = pad_ref[pl.ds(r, rows), :]


def _tap_groups(offset_of_tap):
    groups = {}
    for k in range(CONV_WIDTH):
        o = offset_of_tap(k)
        groups.setdefault(o % SUBLANES, []).append((k, o // SUBLANES))
    return groups


def _tap_window(pad_ref, sh_ref, r, taps, row0, rows):
    q0 = min(q for _, q in taps)
    q1 = max(q for _, q in taps)
    src = pad_ref if r == 0 else sh_ref.at[r - 1]
    win = src[pl.ds(row0 + SUBLANES * q0, SUBLANES * (q1 - q0) + rows), :]
    return win, [(k, SUBLANES * (q - q0)) for k, q in taps]


def _causal_conv(pad_ref, sh_ref, w_ref, out_ref, rows, offset_of_tap, bias=None):
    groups = _tap_groups(offset_of_tap)

    def block(b, carry):
        row0 = pl.multiple_of(b * CONV_ROW_BLOCK, CONV_ROW_BLOCK)
        if bias is None:
            acc = jnp.zeros((CONV_ROW_BLOCK, D_CONV), F32)
        else:
            acc = jnp.broadcast_to(bias, (CONV_ROW_BLOCK, D_CONV))
        for r, taps in groups.items():
            win, starts = _tap_window(pad_ref, sh_ref, r, taps, row0, CONV_ROW_BLOCK)
            for k, s in starts:
                acc = acc + w_ref[k:k + 1, :] * win[s:s + CONV_ROW_BLOCK, :]
        out_ref[pl.ds(row0, CONV_ROW_BLOCK), :] = acc
        return carry

    lax.fori_loop(0, rows // CONV_ROW_BLOCK, block, 0)


def _conv_weight_grad(dy_ref, pad_ref, sh_ref, acc_ref, rows, offset_of_tap):
    groups = _tap_groups(offset_of_tap)
    for r, taps in groups.items():

        def block(b, parts, r=r, taps=taps):
            row0 = pl.multiple_of(b * WGRAD_ROW_BLOCK, WGRAD_ROW_BLOCK)
            dyb = dy_ref[pl.ds(row0, WGRAD_ROW_BLOCK), :]
            win, starts = _tap_window(pad_ref, sh_ref, r, taps, row0, WGRAD_ROW_BLOCK)
            out = []
            for part, (_, s) in zip(parts, starts):
                pr = dyb * win[s:s + WGRAD_ROW_BLOCK, :]
                out.append(part + pr.reshape(WGRAD_ROW_BLOCK // SUBLANES, SUBLANES, D_CONV).sum(axis=0))
            return tuple(out)

        zeros = tuple(jnp.zeros((SUBLANES, D_CONV), F32) for _ in taps)
        parts = lax.fori_loop(0, rows // WGRAD_ROW_BLOCK, block, zeros)
        for part, (k, _) in zip(parts, taps):
            acc_ref[k] += part


def _prep_weights(w_in, w_out, w_gate_t, w_up_t, w_down, conv_w, w_s, b_s):
    def compute(ins, outs):
        win_ref, wout_ref, wgt_ref, wut_ref, wd_ref, cw_ref, ws_ref, bs_ref = ins
        win_o, wout_o, wgt_o, wut_o, wd_o, cw_o, wcat_o, wcatt_o, bsf_o = outs
        win_o[...] = win_ref[...].T.astype(BF16)
        wout_o[...] = wout_ref[...].astype(BF16)
        wgt_o[...] = wgt_ref[...].astype(BF16)
        wut_o[...] = wut_ref[...].astype(BF16)
        wd_o[...] = wd_ref[...].astype(BF16)
        cw_o[...] = jnp.zeros(cw_o.shape, F32)
        cw_o[0:CONV_WIDTH, 0:D_CONV // N_DEV] = cw_ref[...]
        row = lax.broadcasted_iota(jnp.int32, (CHUNK, CHUNK), 0)
        col = lax.broadcasted_iota(jnp.int32, (CHUNK, CHUNK), 1)
        causal = row >= col
        for h in range(N_HEADS):
            w = jnp.where(causal, ws_ref[h], 0.0)
            p, half = h // 2, (h % 2) * CHUNK
            wcat_o[p, :, half:half + CHUNK] = w.astype(BF16)
            wcatt_o[p, :, half:half + CHUNK] = w.T.astype(BF16)
        bsf_o[...] = lax.dot_general(bs_ref[...], _head_selector(), (((0,), (0,)), ((), ())),
                                     preferred_element_type=F32, precision=lax.Precision.HIGHEST)

    S = jax.ShapeDtypeStruct
    out_shapes = [S((256, D_MODEL), BF16), S((128, D_MODEL), BF16), S((FF_SHARD, D_MODEL), BF16),
                  S((FF_SHARD, D_MODEL), BF16), S((FF_SHARD, D_MODEL), BF16), S((CONV_ROWS, 128), F32),
                  S((4, CHUNK, 2 * CHUNK), BF16), S((4, CHUNK, 2 * CHUNK), BF16), S((CHUNK, D_SGU), F32)]
    (res,), _ = _staged_call(
        "prep_weights", [([w_in, w_out, w_gate_t, w_up_t, w_down, conv_w, w_s, b_s], out_shapes, compute)], 32)
    return res


def _mesh_position():
    x, y, c = lax.axis_index("x"), lax.axis_index("y"), lax.axis_index("c")
    return x, y, c


def _peers(x, y, c):
    out = []
    for k in range(1, N_DEV):
        px = 1 - x if (k >> 2) & 1 else x
        py = 1 - y if (k >> 1) & 1 else y
        pc = 1 - c if k & 1 else c
        out.append(((px, py, pc), 4 * px + 2 * py + pc))
    return out


class _Exchange:
    def __init__(self, scatter, gather):
        self.arrays = list(scatter) + list(gather)
        self.n_sc = len(scatter)
        self.n = len(self.arrays)
        self.out_shape = [jax.ShapeDtypeStruct(a.shape if k < self.n_sc else (N_DEV,) + a.shape, a.dtype)
                          for k, a in enumerate(self.arrays)]
        n_remote = self.n * (N_DEV - 1)
        self.scratch = [pltpu.SemaphoreType.DMA((n_remote,)), pltpu.SemaphoreType.DMA((n_remote,)),
                        pltpu.SemaphoreType.DMA((self.n,))] if self.n else []

    def _copies(self, src, dst, sems):
        send_sems, recv_sems, local_sems = sems
        x, y, c = _mesh_position()
        me = 4 * x + 2 * y + c
        locals_, first, arrivals, passed, last = [], [], [], [], []

        def remote(a, k, src_ref, slot, to):
            s = a * (N_DEV - 1) + k
            return pltpu.make_async_remote_copy(src_ref=src_ref, dst_ref=dst[a].at[slot], send_sem=send_sems.at[s],
                                                recv_sem=recv_sems.at[s], device_id=to, device_id_type=MESH)

        for a in range(self.n):
            if a < self.n_sc:
                locals_.append(pltpu.make_async_copy(src[a].at[me], dst[a].at[me], local_sems.at[a]))
                for k, (peer, pid) in enumerate(_peers(x, y, c)):
                    first.append(remote(a, k, src[a].at[pid], me, peer))
                    last.append(remote(a, k, src[a].at[pid], pid, peer))
                continue
            locals_.append(pltpu.make_async_copy(src[a], dst[a].at[me], local_sems.at[a]))
            sibling, sib_id = (x, y, 1 - c), 4 * x + 2 * y + (1 - c)
            chips = [(1 - x, y), (x, 1 - y), (1 - x, 1 - y)]
            first.append(remote(a, 0, src[a], me, sibling))
            last.append(remote(a, 0, src[a], sib_id, sibling))
            for j, (px, py) in enumerate(chips):
                same, other = 4 * px + 2 * py + c, 4 * px + 2 * py + (1 - c)
                first.append(remote(a, 1 + j, src[a], me, (px, py, c)))
                arrivals.append(remote(a, 1 + j, src[a], same, (px, py, c)))
                passed.append(remote(a, 4 + j, dst[a].at[same], same, sibling))
                last.append(remote(a, 4 + j, dst[a].at[other], other, sibling))
        return locals_, first, arrivals, passed, last

    def start(self, src, dst, sems):
        if not self.n:
            return
        locals_, first, _, _, _ = self._copies(src, dst, sems)
        for cp in locals_ + first:
            cp.start()

    def forward(self, src, dst, sems):
        if self.n == self.n_sc:
            return
        _, _, arrivals, passed, _ = self._copies(src, dst, sems)
        for arrived, cp in zip(arrivals, passed):
            arrived.wait_recv()
            cp.start()

    def wait(self, src, dst, sems):
        if not self.n:
            return
        locals_, first, _, passed, last = self._copies(src, dst, sems)
        for cp in last:
            cp.wait_recv()
        for cp in first + passed:
            cp.wait_send()
        for cp in locals_:
            cp.wait()


def _exchange(name, scatter, gather):
    ex = _Exchange(scatter, gather)
    n = ex.n

    def body(*refs):
        src, dst, sems = refs[:n], refs[n:2 * n], refs[2 * n:]
        ex.start(src, dst, sems)
        ex.forward(src, dst, sems)
        ex.wait(src, dst, sems)

    return pl.pallas_call(
        body, name=name, out_shape=tuple(ex.out_shape), in_specs=[ANY] * n, out_specs=(ANY,) * n,
        scratch_shapes=ex.scratch,
    )(*ex.arrays)


def _forward_step(n_steps):
    return (5 * n_steps) // 8


def _hosted(ex, refs, n_in, n_out):
    ins, ex_src = refs[:n_in], refs[n_in:n_in + ex.n]
    rest = refs[n_in + ex.n:]
    outs, ex_dst = rest[:n_out], rest[n_out:n_out + ex.n]
    rest = rest[n_out + ex.n:]
    n_own = len(rest) - len(ex.scratch)
    return ins, outs, rest[:n_own], (ex_src, ex_dst, rest[n_own:])


def _fwd_mix(x, win_g, wout_g, sgu_g, sgu_b, wcat, bs_full, cw, cb, cg, cbeta, tm, ex):
    T = x.shape[0]
    nt = T // tm

    def body(*refs):
        ins, outs, scratch, ex_refs = _hosted(ex, refs, 11, 6)
        x_ref, win_ref, wout_ref, sg_ref, sb_ref, wcat_ref, bs_ref, cw_ref, cb_ref, cg_ref, cbeta_ref = ins
        proj_ref, ycat_ref, n1_ref, rstd1_ref, phi_ref, y_ref = outs
        hpad, hshift = scratch
        i = pl.program_id(0)

        @pl.when(i == 0)
        def _():
            ex.start(*ex_refs)

        xf = x_ref[...]
        xb = xf.astype(BF16)
        proj_ref[...] = _dot_nt(xb, win_ref[...])
        cdf = _normal_cdf(proj_ref[:, 0:2 * D_SGU])
        phi_ref[...] = cdf.astype(BF16)
        u = proj_ref[:, 0:D_SGU] * cdf[:, 0:D_SGU]
        v = proj_ref[:, D_SGU:2 * D_SGU] * cdf[:, D_SGU:2 * D_SGU]
        vhat, _ = _ln_fwd(v)
        vn = vhat * sg_ref[...] + sb_ref[...]
        lo = _lo_mask()
        for c in range(tm // CHUNK):
            rows = slice(CHUNK * c, CHUNK * (c + 1))
            for p in range(4):
                lanes = slice(CHUNK * p, CHUNK * (p + 1))
                mixed = _dot(wcat_ref[p], _head_pair_stack(vn[rows, lanes], lo)) + bs_ref[:, lanes]
                ycat_ref[rows, lanes] = (u[rows, lanes] * mixed).astype(BF16)
        base = 2 * D_SGU
        a = proj_ref[:, base:base + D_CONV]
        g = proj_ref[:, base + D_CONV:base + 2 * D_CONV]

        @pl.when(i == 0)
        def _():
            hpad[0:HALO, :] = jnp.zeros((HALO, D_CONV), F32)

        hpad[HALO:HALO + tm, :] = a * jax.nn.sigmoid(g)
        _shifted_copies(hpad, hshift, tm + SHIFT_ROWS)
        _causal_conv(hpad, hshift, cw_ref, y_ref, tm, lambda k: HALO - (CONV_WIDTH - 1) + k, bias=cb_ref[...])
        hpad[0:HALO, :] = hpad[tm:tm + HALO, :]
        yhat, _ = _ln_fwd(y_ref[...])
        yn = yhat * cg_ref[...] + cbeta_ref[...]
        ycat_ref[:, D_SGU:D_SGU + D_CONV] = (yn * jax.nn.sigmoid(yn)).astype(BF16)
        r1 = ALPHA * xf + _dot(ycat_ref[...], wout_ref[...])
        n1, rstd1 = _ln_fwd(r1)
        n1_ref[...] = n1
        rstd1_ref[...] = rstd1

        @pl.when(i == _forward_step(nt))
        def _():
            ex.forward(*ex_refs)

        @pl.when(i == nt - 1)
        def _():
            ex.wait(*ex_refs)

    S = jax.ShapeDtypeStruct
    row = lambda w: pl.BlockSpec((tm, w), lambda i: (i, 0))
    res = pl.pallas_call(
        body, name="fwd_mix", grid=(nt,),
        in_specs=[row(D_MODEL), _full(win_g.shape), _full(wout_g.shape), _full(sgu_g.shape), _full(sgu_b.shape),
                  _full(wcat.shape), _full(bs_full.shape), _full(cw.shape), _full(cb.shape), _full(cg.shape),
                  _full(cbeta.shape)] + [ANY] * ex.n,
        out_specs=(row(2 * D_MODEL), row(D_MODEL), row(D_MODEL), row(1), row(2 * D_SGU), row(D_CONV)) + (ANY,) * ex.n,
        out_shape=(S((T, 2 * D_MODEL), F32), S((T, D_MODEL), BF16), S((T, D_MODEL), F32), S((T, 1), F32),
                   S((T, 2 * D_SGU), BF16), S((T, D_CONV), F32), *ex.out_shape),
        scratch_shapes=[pltpu.VMEM((tm + HALO, D_CONV), F32),
                        pltpu.VMEM((SUBLANES - 1, tm + SHIFT_ROWS, D_CONV), F32)] + ex.scratch,
        compiler_params=_params(56, 1),
    )(x, win_g, wout_g, sgu_g, sgu_b, wcat, bs_full, cw, cb, cg, cbeta, *ex.arrays)
    return res[:6], res[6:]


def _load_resident(pairs, sems):
    cps = [pltpu.make_async_copy(s, d, sems.at[k]) for k, (s, d) in enumerate(pairs)]
    for cp in cps:
        cp.start()
    for cp in cps:
        cp.wait()


def _fwd_mlp(n1, tgt, l1g, l1b, l2g, l2b, wgt, wut, wd, tm):
    T = n1.shape[0]
    nt = T // tm
    nf = D_FF // MXU_COLS

    def body(n1_ref, tgt_ref, l1g_ref, l1b_ref, l2g_ref, l2b_ref, wg_hbm, wu_hbm, wd_hbm,
             gate_ref, up_ref, hh_ref, dr2_ref, stat_ref, wg_s, wu_s, wd_s, sems):
        i = pl.program_id(0)

        @pl.when(i == 0)
        def _():
            _load_resident([(wg_hbm, wg_s), (wu_hbm, wu_s), (wd_hbm, wd_s)], sems)
            stat_ref[...] = jnp.zeros(stat_ref.shape, F32)

        x1 = n1_ref[...] * l1g_ref[...] + l1b_ref[...]
        x1b = x1.astype(BF16)
        for f in range(nf):
            cols = slice(MXU_COLS * f, MXU_COLS * (f + 1))
            gt = _dot_nt(x1b, wg_s[cols, :])
            ut = _dot_nt(x1b, wu_s[cols, :])
            gate_ref[:, cols] = gt.astype(BF16)
            up_ref[:, cols] = ut.astype(BF16)
            hh_ref[:, cols] = (gt * jax.nn.sigmoid(gt) * ut).astype(BF16)
        r2 = ALPHA * x1 + _dot(hh_ref[...], wd_s[...])
        n2, rstd2 = _ln_fwd(r2)
        x2 = n2 * l2g_ref[...] + l2b_ref[...]
        diff = x2 - tgt_ref[...]
        dx2 = diff * (1.0 / D_MODEL)
        stat_ref[0:1, :] += _colsum(diff * diff)
        stat_ref[1:2, :] += _colsum(dx2 * n2)
        stat_ref[2:3, :] += _colsum(dx2)
        dr2_ref[...] = _ln_bwd(dx2 * l2g_ref[...], n2, rstd2)

    S = jax.ShapeDtypeStruct
    row = lambda w: pl.BlockSpec((tm, w), lambda i: (i, 0))
    vec = _full((1, D_MODEL))
    return pl.pallas_call(
        body, name="fwd_mlp", grid=(nt,),
        in_specs=[row(D_MODEL), row(D_MODEL), vec, vec, vec, vec, ANY, ANY, ANY],
        out_specs=(row(D_FF), row(D_FF), row(D_FF), row(D_MODEL), _full((8, D_MODEL))),
        out_shape=(S((T, D_FF), BF16), S((T, D_FF), BF16), S((T, D_FF), BF16), S((T, D_MODEL), F32),
                   S((8, D_MODEL), F32)),
        scratch_shapes=[pltpu.VMEM((D_FF, D_MODEL), BF16)] * 3 + [pltpu.SemaphoreType.DMA((3,))],
        compiler_params=_params(56, 1),
    )(n1, tgt, l1g, l1b, l2g, l2b, wgt, wut, wd)


def _bwd_mlp(dr2, gate, up, n1, rstd1, l1g, wgt, wut, wd, tm):
    T = n1.shape[0]
    nt = T // tm
    nf = D_FF // MXU_COLS

    def body(dr2_ref, gate_ref, up_ref, n1_ref, rstd1_ref, l1g_ref, wg_hbm, wu_hbm, wd_hbm,
             dgate_ref, dup_ref, dr1_ref, stat_ref, wg_s, wu_s, wd_s, sems):
        i = pl.program_id(0)

        @pl.when(i == 0)
        def _():
            _load_resident([(wg_hbm, wg_s), (wu_hbm, wu_s), (wd_hbm, wd_s)], sems)
            stat_ref[...] = jnp.zeros(stat_ref.shape, F32)

        dr2 = dr2_ref[...]
        dr2b = dr2.astype(BF16)
        for f in range(nf):
            cols = slice(MXU_COLS * f, MXU_COLS * (f + 1))
            dhh = _dot_nt(dr2b, wd_s[cols, :])
            gt = gate_ref[:, cols].astype(F32)
            ut = up_ref[:, cols].astype(F32)
            sg = jax.nn.sigmoid(gt)
            dgate_ref[:, cols] = (dhh * ut * (sg * (1.0 + gt * (1.0 - sg)))).astype(BF16)
            dup_ref[:, cols] = (dhh * (gt * sg)).astype(BF16)
        dx1 = ALPHA * dr2 + _dot(dgate_ref[...], wg_s[...]) + _dot(dup_ref[...], wu_s[...])
        n1 = n1_ref[...]
        stat_ref[0:1, :] += _colsum(dx1 * n1)
        stat_ref[1:2, :] += _colsum(dx1)
        dr1_ref[...] = _ln_bwd(dx1 * l1g_ref[...], n1, rstd1_ref[...])

    S = jax.ShapeDtypeStruct
    row = lambda w: pl.BlockSpec((tm, w), lambda i: (i, 0))
    return pl.pallas_call(
        body, name="bwd_mlp", grid=(nt,),
        in_specs=[row(D_MODEL), row(D_FF), row(D_FF), row(D_MODEL), row(1), _full((1, D_MODEL)), ANY, ANY, ANY],
        out_specs=(row(D_FF), row(D_FF), row(D_MODEL), _full((8, D_MODEL))),
        out_shape=(S((T, D_FF), BF16), S((T, D_FF), BF16), S((T, D_MODEL), F32), S((8, D_MODEL), F32)),
        scratch_shapes=[pltpu.VMEM((D_FF, D_MODEL), BF16)] * 3 + [pltpu.SemaphoreType.DMA((3,))],
        compiler_params=_params(56, 1),
    )(dr2, gate, up, n1, rstd1, l1g, wgt, wut, wd)


def _bwd_mix(dr1, proj, phi, y, win_g, wout_g, sgu_g, sgu_b, wcat, wcatt, bs_full, cw, cg, cbeta, tm, ex):
    T = dr1.shape[0]
    nt = T // tm
    halo_blocks = tm // HALO

    def body(*refs):
        ins, outs, scratch, ex_refs = _hosted(ex, refs, 15, 6)
        (dr1_ref, proj_ref, halo_ref, phi_ref, y_ref, win_ref, wout_ref, sg_ref, sb_ref, wcat_ref, wcatt_ref, bs_ref,
         cw_ref, cg_ref, cbeta_ref) = ins
        gx_ref, dproj_ref, gws_out, gbs_out, gcw_ref, vec_ref = outs
        hpad, shift, dypad, dhbuf, dubuf, dvnbuf, gcw_acc, gws_ref, gbs_ref = scratch
        i = pl.program_id(0)
        tile = nt - 1 - i

        @pl.when(i == 0)
        def _():
            ex.start(*ex_refs)
            gws_ref[...] = jnp.zeros(gws_ref.shape, F32)
            gbs_ref[...] = jnp.zeros(gbs_ref.shape, F32)
            gcw_ref[...] = jnp.zeros(gcw_ref.shape, F32)
            vec_ref[...] = jnp.zeros(vec_ref.shape, F32)
            gcw_acc[...] = jnp.zeros(gcw_acc.shape, F32)
            dypad[tm:tm + HALO, :] = jnp.zeros((HALO, D_CONV), F32)

        dr1 = dr1_ref[...]
        dycat = _dot_nt(dr1.astype(BF16), wout_ref[...])
        pu = proj_ref[:, 0:D_SGU]
        pv = proj_ref[:, D_SGU:2 * D_SGU]
        cdf_u = phi_ref[:, 0:D_SGU].astype(F32)
        cdf_v = phi_ref[:, D_SGU:2 * D_SGU].astype(F32)
        u = pu * cdf_u
        vhat, rstd_v = _ln_fwd(pv * cdf_v)
        vn = vhat * sg_ref[...] + sb_ref[...]
        lo = _lo_mask()
        for c in range(tm // CHUNK):
            rows = slice(CHUNK * c, CHUNK * (c + 1))
            for p in range(4):
                lanes = slice(CHUNK * p, CHUNK * (p + 1))
                vstack = _head_pair_stack(vn[rows, lanes], lo)
                mixed = _dot(wcat_ref[p], vstack) + bs_ref[:, lanes]
                d_a = dycat[rows, lanes]
                dubuf[rows, lanes] = d_a * mixed
                dm = d_a * u[rows, lanes]
                gbs_ref[:, lanes] += dm
                dstack = _head_pair_stack(dm, lo)
                gws_ref[2 * CHUNK * p:2 * CHUNK * (p + 1), :] += _dot_nt(dstack, vn[rows, lanes].astype(BF16))
                dvnbuf[rows, lanes] = _dot(wcatt_ref[p], dstack)
        dvn = dvnbuf[...]
        vec_ref[0:1, :] += _colsum(dvn * vhat)
        vec_ref[1:2, :] += _colsum(dvn)
        dv = _ln_bwd(dvn * sg_ref[...], vhat, rstd_v)
        dproj_ref[:, 0:D_SGU] = (dubuf[...] * _gelu_grad(pu, cdf_u)).astype(BF16)
        dproj_ref[:, D_SGU:2 * D_SGU] = (dv * _gelu_grad(pv, cdf_v)).astype(BF16)
        base = 2 * D_SGU
        a = proj_ref[:, base:base + D_CONV]
        sgm = jax.nn.sigmoid(proj_ref[:, base + D_CONV:base + 2 * D_CONV])
        h_before = halo_ref[:, 0:D_CONV] * jax.nn.sigmoid(halo_ref[:, D_CONV:2 * D_CONV])
        hpad[0:HALO, :] = jnp.where(tile > 0, h_before, 0.0)
        hpad[HALO:HALO + tm, :] = a * sgm
        _shifted_copies(hpad, shift, tm + SHIFT_ROWS)
        h_offset = lambda k: HALO - (CONV_WIDTH - 1) + k
        yhat, rstd_y = _ln_fwd(y_ref[...])
        yn = yhat * cg_ref[...] + cbeta_ref[...]
        s = jax.nn.sigmoid(yn)
        dyn = dycat[:, D_SGU:D_SGU + D_CONV] * (s * (1.0 + yn * (1.0 - s)))
        vec_ref[3:4, :] += _colsum(dyn * yhat)
        vec_ref[4:5, :] += _colsum(dyn)
        dy = _ln_bwd(dyn * cg_ref[...], yhat, rstd_y)
        vec_ref[2:3, :] += _colsum(dy)
        dypad[0:tm, :] = dy
        _conv_weight_grad(dypad, hpad, shift, gcw_acc, tm, h_offset)
        _shifted_copies(dypad, shift, tm + SHIFT_ROWS)
        _causal_conv(dypad, shift, cw_ref, dhbuf, tm, lambda k: (CONV_WIDTH - 1) - k)
        dypad[tm:tm + HALO, :] = dypad[0:HALO, :]
        dh = dhbuf[...]
        dproj_ref[:, base:base + D_CONV] = (dh * sgm).astype(BF16)
        dproj_ref[:, base + D_CONV:base + 2 * D_CONV] = (dh * a * sgm * (1.0 - sgm)).astype(BF16)
        gx_ref[...] = ALPHA * dr1 + _dot(dproj_ref[...], win_ref[...])

        @pl.when(i == _forward_step(nt))
        def _():
            ex.forward(*ex_refs)

        @pl.when(i == nt - 1)
        def _():
            gcw_ref[...] = gcw_acc[...].sum(axis=1)
            gws_out[...] = gws_ref[...].astype(BF16)
            gbs_out[...] = lax.dot_general(_head_selector(), gbs_ref[...], (((1,), (1,)), ((), ())),
                                           preferred_element_type=F32, precision=lax.Precision.HIGHEST)
            ex.wait(*ex_refs)

    S = jax.ShapeDtypeStruct
    row = lambda w: pl.BlockSpec((tm, w), lambda i: (nt - 1 - i, 0))
    halo = pl.BlockSpec((HALO, D_MODEL), lambda i: (jnp.maximum((nt - 1 - i) * halo_blocks - 1, 0), 1))
    res = pl.pallas_call(
        body, name="bwd_mix", grid=(nt,),
        in_specs=[row(D_MODEL), row(2 * D_MODEL), halo, row(2 * D_SGU), row(D_CONV), _full(win_g.shape),
                  _full(wout_g.shape), _full(sgu_g.shape), _full(sgu_b.shape), _full(wcat.shape), _full(wcatt.shape),
                  _full(bs_full.shape), _full(cw.shape), _full(cg.shape), _full(cbeta.shape)]
        + [ANY] * ex.n,
        out_specs=(row(D_MODEL), row(2 * D_MODEL), _full((N_HEADS * CHUNK, CHUNK)), _full((N_HEADS, CHUNK)),
                   _full((CONV_ROWS, D_CONV)), _full((8, D_CONV))) + (ANY,) * ex.n,
        out_shape=(S((T, D_MODEL), F32), S((T, 2 * D_MODEL), BF16), S((N_HEADS * CHUNK, CHUNK), BF16),
                   S((N_HEADS, CHUNK), F32), S((CONV_ROWS, D_CONV), F32), S((8, D_CONV), F32), *ex.out_shape),
        scratch_shapes=[pltpu.VMEM((tm + HALO, D_CONV), F32), pltpu.VMEM((SUBLANES - 1, tm + SHIFT_ROWS, D_CONV), F32),
                        pltpu.VMEM((tm + HALO, D_CONV), F32),
                        pltpu.VMEM((tm, D_CONV), F32), pltpu.VMEM((tm, D_SGU), F32),
                        pltpu.VMEM((tm, D_SGU), F32), pltpu.VMEM((CONV_ROWS, 8, D_CONV), F32),
                        pltpu.VMEM((N_HEADS * CHUNK, CHUNK), F32), pltpu.VMEM((CHUNK, D_SGU), F32)] + ex.scratch,
        compiler_params=_params(56, 1),
    )(dr1, proj, proj, phi, y, win_g, wout_g, sgu_g, sgu_b, wcat, wcatt, bs_full, cw, cg, cbeta, *ex.arrays)
    return res[:6], res[6:]


def _wgrad(name, a, b, blocks, tk, ex=None, b_cols=None, b_affine=None):
    T, M = a.shape
    col, N = b_cols or (0, b.shape[1])
    nk = T // tk
    out_shape = (blocks, M // blocks, N)
    ex = ex or _Exchange([], [])
    affine = list(b_affine or [])

    def body(*refs):
        ins, (o_ref,), (acc,), ex_refs = _hosted(ex, refs, 2 + len(affine), 1)
        a_ref, b_ref = ins[:2]
        i = pl.program_id(0)

        @pl.when(i == 0)
        def _():
            ex.start(*ex_refs)
            acc[...] = jnp.zeros(acc.shape, F32)

        right = b_ref[...]
        if affine:
            right = right * ins[2][...] + ins[3][...]
        acc[...] += _dot_tn(a_ref[...].astype(BF16), right.astype(BF16))

        @pl.when(i == _forward_step(nk))
        def _():
            ex.forward(*ex_refs)

        @pl.when(i == nk - 1)
        def _():
            o_ref[...] = acc[...].astype(BF16)
            ex.wait(*ex_refs)

    res = pl.pallas_call(
        body, name=name, grid=(nk,),
        in_specs=[pl.BlockSpec((tk, M), lambda i: (i, 0)), pl.BlockSpec((tk, N), lambda i: (i, col))]
        + [_full((1, N))] * len(affine) + [ANY] * ex.n,
        out_specs=(_full((M, N)),) + (ANY,) * ex.n,
        out_shape=(jax.ShapeDtypeStruct((M, N), BF16), *ex.out_shape),
        scratch_shapes=[pltpu.VMEM((M, N), F32)] + ex.scratch,
        compiler_params=_params(56, 1),
    )(a, b, *affine, *ex.arrays)
    g = res[0].reshape(out_shape)
    return (g, res[1:]) if ex.n else g


def _adamw(w, g, m, v):
    m2 = ADAM_B1 * m + (1.0 - ADAM_B1) * g
    v2 = ADAM_B2 * v + (1.0 - ADAM_B2) * (g * g)
    m_hat = m2 / (1.0 - ADAM_B1 ** ADAM_STEP)
    v_hat = v2 / (1.0 - ADAM_B2 ** ADAM_STEP)
    delta = -ADAM_LR * (m_hat / (jnp.sqrt(v_hat) + ADAM_EPS) + ADAM_WD * w)
    return delta, m2, v2


def _sum_partials(r_ref):
    g = r_ref[0].astype(F32)
    for s in range(1, N_DEV):
        g = g + r_ref[s].astype(F32)
    return g


def _staged_call(name, groups, vmem_mib, ex=None):
    ex = ex or _Exchange([], [])
    inputs = [a for ins, _, _ in groups for a in ins]
    out_shapes = [s for _, outs, _ in groups for s in outs]
    n_in, n_out = len(inputs), len(out_shapes)

    def body(*refs):
        ins, outs, scratch, ex_refs = _hosted(ex, refs, n_in, n_out)
        in_bufs, out_bufs, sems = scratch[:n_in], scratch[n_in:n_in + n_out], scratch[n_in + n_out]
        ex.start(*ex_refs)
        loads = [pltpu.make_async_copy(ins[k], in_bufs[k], sems.at[k]) for k in range(n_in)]
        stores = [pltpu.make_async_copy(out_bufs[k], outs[k], sems.at[n_in + k]) for k in range(n_out)]
        for cp in loads:
            cp.start()
        i0 = o0 = 0
        for g_ins, g_outs, compute in groups:
            i1, o1 = i0 + len(g_ins), o0 + len(g_outs)
            for cp in loads[i0:i1]:
                cp.wait()
            compute(in_bufs[i0:i1], out_bufs[o0:o1])
            for cp in stores[o0:o1]:
                cp.start()
            i0, o0 = i1, o1
        for cp in stores:
            cp.wait()
        ex.forward(*ex_refs)
        ex.wait(*ex_refs)

    scratch = ([pltpu.VMEM(a.shape, a.dtype) for a in inputs] + [pltpu.VMEM(s.shape, s.dtype) for s in out_shapes]
               + [pltpu.SemaphoreType.DMA((n_in + n_out,))] + ex.scratch)
    res = pl.pallas_call(
        body, name=name, out_shape=(*[pltpu.HBM(s.shape, s.dtype) for s in out_shapes], *ex.out_shape),
        in_specs=[HBM] * n_in + [ANY] * ex.n, out_specs=(HBM,) * n_out + (ANY,) * ex.n,
        scratch_shapes=scratch, compiler_params=_params(vmem_mib),
    )(*[pltpu.with_memory_space_constraint(a, pltpu.HBM) for a in inputs], *ex.arrays)
    per_group, o0 = [], 0
    for _, g_outs, _ in groups:
        per_group.append(list(res[o0:o0 + len(g_outs)]))
        o0 += len(g_outs)
    return per_group, res[n_out:]


def _adamw_shard_group(parts, w, m, v, transposed):
    n = len(parts)

    def compute(ins, outs):
        w_ref, m_ref, v_ref = ins[n:]
        lo = 0
        for r_ref in ins[:n]:
            g = _sum_partials(r_ref)
            cols = g.shape[1]
            if transposed:
                g, at = g.T, (slice(lo, lo + cols), slice(None))
            else:
                at = (slice(None), slice(lo, lo + cols))
            delta, m2, v2 = _adamw(w_ref[at], g, m_ref[at], v_ref[at])
            for o, val in zip(outs, (g, delta, m2, v2)):
                o[at] = val
            lo += cols

    return [*parts, w, m, v], [jax.ShapeDtypeStruct(w.shape, F32)] * 4, compute


SMALL_NAMES = ["sgu_ln_g", "sgu_ln_b", "w_s", "b_s", "conv_b", "conv_ln_g", "conv_ln_b", "ln1_g", "ln1_b", "ln2_g", "ln2_b",
               "conv_w"]


def _finish_small_group(gws8, gbs8, gcw8, vmix8, vmlp8, vout8, small):
    names = SMALL_NAMES
    flat = []
    for n in names:
        flat += list(small[n])
    cw_block = D_CONV // N_DEV

    def compute(ins, outs):
        gws_ref, gbs_ref, gcw_ref, vmix_ref, vmlp_ref, vout_ref = ins[:6]
        wmv = ins[6:]
        loss_o = outs[0]
        outs = outs[1:]
        gws = _sum_partials(gws_ref)
        gbs = _sum_partials(gbs_ref)
        vmix = _sum_partials(vmix_ref)
        vmlp = _sum_partials(vmlp_ref)
        vout = _sum_partials(vout_ref)
        x, y, c = _mesh_position()
        first = (4 * x + 2 * y + c) * cw_block
        pick = (lax.broadcasted_iota(jnp.int32, (D_CONV, cw_block), 0)
                == first + lax.broadcasted_iota(jnp.int32, (D_CONV, cw_block), 1)).astype(F32)
        gcw = jnp.dot(_sum_partials(gcw_ref), pick, preferred_element_type=F32,
                      precision=lax.Precision.HIGHEST)[0:CONV_WIDTH, :]
        loss = (0.5 / D_MODEL) * jnp.sum(vout[0:1, :], axis=1, keepdims=True)
        loss_o[...] = jnp.broadcast_to(loss, loss_o.shape)
        rows = lax.broadcasted_iota(jnp.int32, (N_HEADS * CHUNK, CHUNK), 0)
        cols = lax.broadcasted_iota(jnp.int32, (N_HEADS * CHUNK, CHUNK), 1)
        gws = jnp.where((rows & (CHUNK - 1)) >= cols, gws, 0.0)
        grads = {
            "sgu_ln_g": vmix[0:1, :], "sgu_ln_b": vmix[1:2, :], "w_s": gws, "b_s": gbs,
            "conv_b": vmix[2:3, :], "conv_ln_g": vmix[3:4, :], "conv_ln_b": vmix[4:5, :],
            "ln1_g": vmlp[0:1, :], "ln1_b": vmlp[1:2, :], "ln2_g": vout[1:2, :], "ln2_b": vout[2:3, :],
            "conv_w": gcw,
        }
        for k, n in enumerate(names):
            w_ref, m_ref, v_ref = wmv[3 * k:3 * k + 3]
            g = grads[n]
            delta, m2, v2 = _adamw(w_ref[...], g, m_ref[...], v_ref[...])
            outs[4 * k][...] = g
            outs[4 * k + 1][...] = delta
            outs[4 * k + 2][...] = m2
            outs[4 * k + 3][...] = v2

    S = jax.ShapeDtypeStruct
    out_shape = [S((SUBLANES, 128), F32)]
    for n in names:
        out_shape += [S(small[n][0].shape, F32)] * 4
    return [gws8, gbs8, gcw8, vmix8, vmlp8, vout8, *flat], out_shape, compute


TOKEN_TILE_FWD_MIX = 512
TOKEN_TILE_BWD_MIX = 512
TOKEN_TILE_FWD_MLP = 512
TOKEN_TILE_BWD_MLP = 512
TOKEN_TILE_WGRAD = 1024


def kernel(x, w_in, sgu_ln_g, sgu_ln_b, w_s, b_s, conv_w, conv_b, conv_ln_g, conv_ln_b, w_out, ln1_g, ln1_b, w_gate, w_up, w_down, ln2_g, ln2_b, loss_target, m_w_in, m_sgu_ln_g, m_sgu_ln_b, m_w_s, m_b_s, m_conv_w, m_conv_b, m_conv_ln_g, m_conv_ln_b, m_w_out, m_ln1_g, m_ln1_b, m_w_gate, m_w_up, m_w_down, m_ln2_g, m_ln2_b, v_w_in, v_sgu_ln_g, v_sgu_ln_b, v_w_s, v_b_s, v_conv_w, v_conv_b, v_conv_ln_g, v_conv_ln_b, v_w_out, v_ln1_g, v_ln1_b, v_w_gate, v_w_up, v_w_down, v_ln2_g, v_ln2_b):
    xs = x[0]
    tgt = loss_target[0]

    (win_b, wout_b, wgt_b, wut_b, wd_b, cw_b, wcat, wcatt, bs_full) = _prep_weights(
        w_in[0], w_out[0], w_gate[0].T, w_up[0].T, w_down[0], conv_w[0], w_s[0], b_s[0])
    win_g, wout_g, cw_g = _exchange("gather_mix_weights", [], [win_b, wout_b, cw_b])
    win_g = win_g.reshape(2 * D_MODEL, D_MODEL)
    wout_g = wout_g.reshape(D_MODEL, D_MODEL)
    cw = jnp.transpose(cw_g[:, :, :D_CONV // N_DEV], (1, 0, 2)).reshape(CONV_ROWS, D_CONV)

    (proj, ycat, n1, rstd1, phi, y_conv), (wgt_g, wut_g, wd_g) = _fwd_mix(
        xs, win_g, wout_g, sgu_ln_g, sgu_ln_b, wcat, bs_full, cw, conv_b, conv_ln_g, conv_ln_b, TOKEN_TILE_FWD_MIX,
        _Exchange([], [wgt_b, wut_b, wd_b]))
    wgt_g = wgt_g.reshape(D_FF, D_MODEL)
    wut_g = wut_g.reshape(D_FF, D_MODEL)
    wd_g = wd_g.reshape(D_FF, D_MODEL)
    gate, up, hh, dr2, vout = _fwd_mlp(n1, tgt, ln1_g, ln1_b, ln2_g, ln2_b, wgt_g, wut_g, wd_g, TOKEN_TILE_FWD_MLP)

    dgate, dup, dr1, vmlp = _bwd_mlp(dr2, gate, up, n1, rstd1, ln1_g, wgt_g, wut_g, wd_g, TOKEN_TILE_BWD_MLP)
    tk = TOKEN_TILE_WGRAD
    x1 = dict(b_affine=(ln1_g, ln1_b))
    g_wgt = _wgrad("wgrad_gate", dgate, n1, N_DEV, tk, **x1)
    g_wut = _wgrad("wgrad_up", dup, n1, N_DEV, tk, **x1)
    g_wd = _wgrad("wgrad_down", hh, dr2, N_DEV, tk)
    g_wout = _wgrad("wgrad_out", ycat, dr1, N_DEV, tk)
    (gx, dproj, gws, gbs, gcw, vmix), (r_wgt, r_wut, r_wd, r_wout) = _bwd_mix(
        dr1, proj, phi, y_conv, win_g, wout_g, sgu_ln_g, sgu_ln_b, wcat, wcatt, bs_full, cw, conv_ln_g, conv_ln_b,
        TOKEN_TILE_BWD_MIX, _Exchange([g_wgt, g_wut, g_wd, g_wout], []))
    half = D_MODEL // 2
    g_win_a = _wgrad("wgrad_in_a", dproj, xs, N_DEV, tk, b_cols=(0, half))
    g_win_b, (r_win_a, gws8, gbs8, gcw8, vmix8, vmlp8, vout8) = _wgrad(
        "wgrad_in_b", dproj, xs, N_DEV, tk, _Exchange([g_win_a], [gws, gbs, gcw, vmix, vmlp, vout]), b_cols=(1, half))
    (r_win_b,) = _exchange("exchange_grad_in", [g_win_b], [])
    (u_gate, u_up, u_down), _ = _staged_call(
        "adamw_mlp",
        [_adamw_shard_group([r_wgt], w_gate[0].T, m_w_gate[0].T, v_w_gate[0].T, False),
         _adamw_shard_group([r_wut], w_up[0].T, m_w_up[0].T, v_w_up[0].T, False),
         _adamw_shard_group([r_wd], w_down[0], m_w_down[0], v_w_down[0], False)], 56)
    small_in = {
        "conv_w": (conv_w[0], m_conv_w[0], v_conv_w[0]),
        "sgu_ln_g": (sgu_ln_g, m_sgu_ln_g, v_sgu_ln_g), "sgu_ln_b": (sgu_ln_b, m_sgu_ln_b, v_sgu_ln_b),
        "w_s": tuple(a.reshape(N_HEADS * CHUNK, CHUNK) for a in (w_s, m_w_s, v_w_s)),
        "b_s": (b_s[0], m_b_s[0], v_b_s[0]),
        "conv_b": (conv_b, m_conv_b, v_conv_b), "conv_ln_g": (conv_ln_g, m_conv_ln_g, v_conv_ln_g),
        "conv_ln_b": (conv_ln_b, m_conv_ln_b, v_conv_ln_b),
        "ln1_g": (ln1_g, m_ln1_g, v_ln1_g), "ln1_b": (ln1_b, m_ln1_b, v_ln1_b),
        "ln2_g": (ln2_g, m_ln2_g, v_ln2_g), "ln2_b": (ln2_b, m_ln2_b, v_ln2_b),
    }
    (u_in, u_out, fin), _ = _staged_call(
        "adamw_mix_small",
        [_adamw_shard_group([r_win_a, r_win_b], w_in[0], m_w_in[0], v_w_in[0], True),
         _adamw_shard_group([r_wout], w_out[0], m_w_out[0], v_w_out[0], False),
         _finish_small_group(gws8, gbs8, gcw8, vmix8, vmlp8, vout8, small_in)], 40)
    big = {"w_in": u_in, "w_out": u_out, "w_gate": u_gate, "w_up": u_up, "w_down": u_down}
    loss11 = fin[0]
    small = {n: fin[1 + 4 * k:5 + 4 * k] for k, n in enumerate(SMALL_NAMES)}

    shapes = {"w_s": w_s.shape, "b_s": b_s.shape, "conv_w": conv_w.shape}
    out = {}
    for n, r in big.items():
        out[n] = tuple((a.T if n in ("w_gate", "w_up") else a)[None] for a in r)
    for n, r in small.items():
        out[n] = tuple(a.reshape(shapes[n]) for a in r) if n in shapes else tuple(r)

    order = ["w_in", "sgu_ln_g", "sgu_ln_b", "w_s", "b_s", "conv_w", "conv_b", "conv_ln_g", "conv_ln_b", "w_out",
             "ln1_g", "ln1_b", "w_gate", "w_up", "w_down", "ln2_g", "ln2_b"]
    loss = loss11[0, 0]
    return (loss, gx[None], *[out[n][0] for n in order], *[out[n][1] for n in order],
            *[out[n][2] for n in order], *[out[n][3] for n in order])
```

```python
import jax
import jax.numpy as jnp
from jax import lax
from jax.experimental import pallas as pl
from jax.experimental.pallas import tpu as pltpu

F32 = jnp.float32
BF16 = jnp.bfloat16

D_MODEL = 1024
D_SGU = 512
D_CONV = 512
N_HEADS = 8
CHUNK = 128
CONV_WIDTH = 31
CONV_ROWS = 32
HALO = 32
D_FF = 2816
N_DEV = 8
FF_SHARD = D_FF // N_DEV
ALPHA = (2.0 * 1) ** 0.25
LN_EPS = 1e-5
INV_SQRT2 = 0.7071067811865476
INV_SQRT_2PI = 0.3989422804014327

ADAM_LR = 0.001
ADAM_B1 = 0.9
ADAM_B2 = 0.999
ADAM_EPS = 1e-08
ADAM_WD = 0.01
ADAM_STEP = 10

MXU_COLS = 256
SUBLANES = 8
CONV_ROW_BLOCK = 32
WGRAD_ROW_BLOCK = 32
SHIFT_ROWS = HALO - SUBLANES
MIB = 1024 * 1024

HBM = pl.BlockSpec(memory_space=pltpu.HBM)
ANY = pl.BlockSpec(memory_space=pl.ANY)
MESH = pl.DeviceIdType.MESH


def _params(vmem_mib, grid_dims=0):
    kw = dict(vmem_limit_bytes=vmem_mib * MIB)
    if grid_dims:
        kw["dimension_semantics"] = ("arbitrary",) * grid_dims
    return pltpu.CompilerParams(**kw)


def _full(shape):
    return pl.BlockSpec(shape, lambda i: (0,) * len(shape))


def _dot(a, b):
    return jnp.dot(a, b, preferred_element_type=F32)


def _dot_nt(a, b):
    return lax.dot_general(a, b, (((1,), (1,)), ((), ())), preferred_element_type=F32)


def _dot_tn(a, b):
    return lax.dot_general(a, b, (((0,), (0,)), ((), ())), preferred_element_type=F32)


def _normal_cdf(x):
    return 0.5 * (1.0 + lax.erf(x * INV_SQRT2))


def _gelu_grad(x, cdf):
    return cdf + x * jnp.exp(-0.5 * x * x) * INV_SQRT_2PI


def _ln_fwd(v):
    mu = jnp.mean(v, axis=-1, keepdims=True)
    d = v - mu
    var = jnp.mean(d * d, axis=-1, keepdims=True)
    rstd = lax.rsqrt(var + LN_EPS)
    return d * rstd, rstd


def _ln_bwd(dyhat, yhat, rstd):
    m1 = jnp.mean(dyhat, axis=-1, keepdims=True)
    m2 = jnp.mean(dyhat * yhat, axis=-1, keepdims=True)
    return rstd * (dyhat - m1 - yhat * m2)


def _colsum(v):
    return jnp.sum(v, axis=0, keepdims=True)


def _head_pair_stack(v, lo):
    return jnp.concatenate([jnp.where(lo, v, 0.0), jnp.where(lo, 0.0, v)], axis=0).astype(BF16)


def _lo_mask():
    return lax.broadcasted_iota(jnp.int32, (CHUNK, CHUNK), 1) < (CHUNK // 2)


def _head_selector():
    head = lax.broadcasted_iota(jnp.int32, (N_HEADS, D_SGU), 0)
    lane = lax.broadcasted_iota(jnp.int32, (N_HEADS, D_SGU), 1)
    width = D_SGU // N_HEADS
    return ((lane >= head * width) & (lane < (head + 1) * width)).astype(F32)


def _shifted_copies(pad_ref, sh_ref, rows):
    for r in range(1, SUBLANES):
        sh_ref[r - 1, 0:rows, :] = pad_ref[pl.ds(r, rows), :]


def _tap_groups(offset_of_tap):
    groups = {}
    for k in range(CONV_WIDTH):
        o = offset_of_tap(k)
        groups.setdefault(o % SUBLANES, []).append((k, o // SUBLANES))
    return groups


def _tap_window(pad_ref, sh_ref, r, taps, row0, rows):
    q0 = min(q for _, q in taps)
    q1 = max(q for _, q in taps)
    src = pad_ref if r == 0 else sh_ref.at[r - 1]
    win = src[pl.ds(row0 + SUBLANES * q0, SUBLANES * (q1 - q0) + rows), :]
    return win, [(k, SUBLANES * (q - q0)) for k, q in taps]


def _causal_conv(pad_ref, sh_ref, w_ref, out_ref, rows, offset_of_tap, bias=None):
    groups = _tap_groups(offset_of_tap)

    def block(b, carry):
        row0 = pl.multiple_of(b * CONV_ROW_BLOCK, CONV_ROW_BLOCK)
        if bias is None:
            acc = jnp.zeros((CONV_ROW_BLOCK, D_CONV), F32)
        else:
            acc = jnp.broadcast_to(bias, (CONV_ROW_BLOCK, D_CONV))
        for r, taps in groups.items():
            win, starts = _tap_window(pad_ref, sh_ref, r, taps, row0, CONV_ROW_BLOCK)
            for k, s in starts:
                acc = acc + w_ref[k:k + 1, :] * win[s:s + CONV_ROW_BLOCK, :]
        out_ref[pl.ds(row0, CONV_ROW_BLOCK), :] = acc
        return carry

    lax.fori_loop(0, rows // CONV_ROW_BLOCK, block, 0)


def _conv_weight_grad(dy_ref, pad_ref, sh_ref, acc_ref, rows, offset_of_tap):
    groups = _tap_groups(offset_of_tap)
    for r, taps in groups.items():

        def block(b, parts, r=r, taps=taps):
            row0 = pl.multiple_of(b * WGRAD_ROW_BLOCK, WGRAD_ROW_BLOCK)
            dyb = dy_ref[pl.ds(row0, WGRAD_ROW_BLOCK), :]
            win, starts = _tap_window(pad_ref, sh_ref, r, taps, row0, WGRAD_ROW_BLOCK)
            out = []
            for part, (_, s) in zip(parts, starts):
                pr = dyb * win[s:s + WGRAD_ROW_BLOCK, :]
                out.append(part + pr.reshape(WGRAD_ROW_BLOCK // SUBLANES, SUBLANES, D_CONV).sum(axis=0))
            return tuple(out)

        zeros = tuple(jnp.zeros((SUBLANES, D_CONV), F32) for _ in taps)
        parts = lax.fori_loop(0, rows // WGRAD_ROW_BLOCK, block, zeros)
        for part, (k, _) in zip(parts, taps):
            acc_ref[k] += part


def _prep_weights(w_in, w_out, w_gate_t, w_up_t, w_down, conv_w, w_s, b_s):
    def compute(ins, outs):
        win_ref, wout_ref, wgt_ref, wut_ref, wd_ref, cw_ref, ws_ref, bs_ref = ins
        win_o, wout_o, wgt_o, wut_o, wd_o, cw_o, wcat_o, wcatt_o, bsf_o = outs
        win_o[...] = win_ref[...].T.astype(BF16)
        wout_o[...] = wout_ref[...].astype(BF16)
        wgt_o[...] = wgt_ref[...].astype(BF16)
        wut_o[...] = wut_ref[...].astype(BF16)
        wd_o[...] = wd_ref[...].astype(BF16)
        cw_o[...] = jnp.zeros(cw_o.shape, F32)
        cw_o[0:CONV_WIDTH, 0:D_CONV // N_DEV] = cw_ref[...]
        row = lax.broadcasted_iota(jnp.int32, (CHUNK, CHUNK), 0)
        col = lax.broadcasted_iota(jnp.int32, (CHUNK, CHUNK), 1)
        causal = row >= col
        for h in range(N_HEADS):
            w = jnp.where(causal, ws_ref[h], 0.0)
            p, half = h // 2, (h % 2) * CHUNK
            wcat_o[p, :, half:half + CHUNK] = w.astype(BF16)
            wcatt_o[p, :, half:half + CHUNK] = w.T.astype(BF16)
        bsf_o[...] = lax.dot_general(bs_ref[...], _head_selector(), (((0,), (0,)), ((), ())),
                                     preferred_element_type=F32, precision=lax.Precision.HIGHEST)

    S = jax.ShapeDtypeStruct
    out_shapes = [S((256, D_MODEL), BF16), S((128, D_MODEL), BF16), S((FF_SHARD, D_MODEL), BF16),
                  S((FF_SHARD, D_MODEL), BF16), S((FF_SHARD, D_MODEL), BF16), S((CONV_ROWS, 128), F32),
                  S((4, CHUNK, 2 * CHUNK), BF16), S((4, CHUNK, 2 * CHUNK), BF16), S((CHUNK, D_SGU), F32)]
    (res,), _ = _staged_call(
        "prep_weights", [([w_in, w_out, w_gate_t, w_up_t, w_down, conv_w, w_s, b_s], out_shapes, compute)], 32)
    return res


def _mesh_position():
    x, y, c = lax.axis_index("x"), lax.axis_index("y"), lax.axis_index("c")
    return x, y, c


def _peers(x, y, c):
    out = []
    for k in range(1, N_DEV):
        px = 1 - x if (k >> 2) & 1 else x
        py = 1 - y if (k >> 1) & 1 else y
        pc = 1 - c if k & 1 else c
        out.append(((px, py, pc), 4 * px + 2 * py + pc))
    return out


class _Exchange:
    def __init__(self, scatter, gather):
        self.arrays = list(scatter) + list(gather)
        self.n_sc = len(scatter)
        self.n = len(self.arrays)
        self.out_shape = [jax.ShapeDtypeStruct(a.shape if k < self.n_sc else (N_DEV,) + a.shape, a.dtype)
                          for k, a in enumerate(self.arrays)]
        n_remote = self.n * (N_DEV - 1)
        self.scratch = [pltpu.SemaphoreType.DMA((n_remote,)), pltpu.SemaphoreType.DMA((n_remote,)),
                        pltpu.SemaphoreType.DMA((self.n,))] if self.n else []

    def _copies(self, src, dst, sems):
        send_sems, recv_sems, local_sems = sems
        x, y, c = _mesh_position()
        me = 4 * x + 2 * y + c
        locals_, first, arrivals, passed, last = [], [], [], [], []

        def remote(a, k, src_ref, slot, to):
            s = a * (N_DEV - 1) + k
            return pltpu.make_async_remote_copy(src_ref=src_ref, dst_ref=dst[a].at[slot], send_sem=send_sems.at[s],
                                                recv_sem=recv_sems.at[s], device_id=to, device_id_type=MESH)

        for a in range(self.n):
            if a < self.n_sc:
                locals_.append(pltpu.make_async_copy(src[a].at[me], dst[a].at[me], local_sems.at[a]))
                for k, (peer, pid) in enumerate(_peers(x, y, c)):
                    first.append(remote(a, k, src[a].at[pid], me, peer))
                    last.append(remote(a, k, src[a].at[pid], pid, peer))
                continue
            locals_.append(pltpu.make_async_copy(src[a], dst[a].at[me], local_sems.at[a]))
            sibling, sib_id = (x, y, 1 - c), 4 * x + 2 * y + (1 - c)
            chips = [(1 - x, y), (x, 1 - y), (1 - x, 1 - y)]
            first.append(remote(a, 0, src[a], me, sibling))
            last.append(remote(a, 0, src[a], sib_id, sibling))
            for j, (px, py) in enumerate(chips):
                same, other = 4 * px + 2 * py + c, 4 * px + 2 * py + (1 - c)
                first.append(remote(a, 1 + j, src[a], me, (px, py, c)))
                arrivals.append(remote(a, 1 + j, src[a], same, (px, py, c)))
                passed.append(remote(a, 4 + j, dst[a].at[same], same, sibling))
                last.append(remote(a, 4 + j, dst[a].at[other], other, sibling))
        return locals_, first, arrivals, passed, last

    def start(self, src, dst, sems):
        if not self.n:
            return
        locals_, first, _, _, _ = self._copies(src, dst, sems)
        for cp in locals_ + first:
            cp.start()

    def forward(self, src, dst, sems):
        if self.n == self.n_sc:
            return
        _, _, arrivals, passed, _ = self._copies(src, dst, sems)
        for arrived, cp in zip(arrivals, passed):
            arrived.wait_recv()
            cp.start()

    def wait(self, src, dst, sems):
        if not self.n:
            return
        locals_, first, _, passed, last = self._copies(src, dst, sems)
        for cp in last:
            cp.wait_recv()
        for cp in first + passed:
            cp.wait_send()
        for cp in locals_:
            cp.wait()


def _exchange(name, scatter, gather):
    ex = _Exchange(scatter, gather)
    n = ex.n

    def body(*refs):
        src, dst, sems = refs[:n], refs[n:2 * n], refs[2 * n:]
        ex.start(src, dst, sems)
        ex.forward(src, dst, sems)
        ex.wait(src, dst, sems)

    return pl.pallas_call(
        body, name=name, out_shape=tuple(ex.out_shape), in_specs=[ANY] * n, out_specs=(ANY,) * n,
        scratch_shapes=ex.scratch,
    )(*ex.arrays)


def _forward_step(n_steps):
    return (5 * n_steps) // 8


def _hosted(ex, refs, n_in, n_out):
    ins, ex_src = refs[:n_in], refs[n_in:n_in + ex.n]
    rest = refs[n_in + ex.n:]
    outs, ex_dst = rest[:n_out], rest[n_out:n_out + ex.n]
    rest = rest[n_out + ex.n:]
    n_own = len(rest) - len(ex.scratch)
    return ins, outs, rest[:n_own], (ex_src, ex_dst, rest[n_own:])


def _fwd_mix(x, win_g, wout_g, sgu_g, sgu_b, wcat, bs_full, cw, cb, cg, cbeta, tm, ex):
    T = x.shape[0]
    nt = T // tm

    def body(*refs):
        ins, outs, scratch, ex_refs = _hosted(ex, refs, 11, 6)
        x_ref, win_ref, wout_ref, sg_ref, sb_ref, wcat_ref, bs_ref, cw_ref, cb_ref, cg_ref, cbeta_ref = ins
        proj_ref, ycat_ref, n1_ref, rstd1_ref, phi_ref, y_ref = outs
        hpad, hshift = scratch
        i = pl.program_id(0)

        @pl.when(i == 0)
        def _():
            ex.start(*ex_refs)

        xf = x_ref[...]
        xb = xf.astype(BF16)
        proj_ref[...] = _dot_nt(xb, win_ref[...])
        cdf = _normal_cdf(proj_ref[:, 0:2 * D_SGU])
        phi_ref[...] = cdf.astype(BF16)
        u = proj_ref[:, 0:D_SGU] * cdf[:, 0:D_SGU]
        v = proj_ref[:, D_SGU:2 * D_SGU] * cdf[:, D_SGU:2 * D_SGU]
        vhat, _ = _ln_fwd(v)
        vn = vhat * sg_ref[...] + sb_ref[...]
        lo = _lo_mask()
        for c in range(tm // CHUNK):
            rows = slice(CHUNK * c, CHUNK * (c + 1))
            for p in range(4):
                lanes = slice(CHUNK * p, CHUNK * (p + 1))
                mixed = _dot(wcat_ref[p], _head_pair_stack(vn[rows, lanes], lo)) + bs_ref[:, lanes]
                ycat_ref[rows, lanes] = (u[rows, lanes] * mixed).astype(BF16)
        base = 2 * D_SGU
        a = proj_ref[:, base:base + D_CONV]
        g = proj_ref[:, base + D_CONV:base + 2 * D_CONV]

        @pl.when(i == 0)
        def _():
            hpad[0:HALO, :] = jnp.zeros((HALO, D_CONV), F32)

        hpad[HALO:HALO + tm, :] = a * jax.nn.sigmoid(g)
        _shifted_copies(hpad, hshift, tm + SHIFT_ROWS)
        _causal_conv(hpad, hshift, cw_ref, y_ref, tm, lambda k: HALO - (CONV_WIDTH - 1) + k, bias=cb_ref[...])
        hpad[0:HALO, :] = hpad[tm:tm + HALO, :]
        yhat, _ = _ln_fwd(y_ref[...])
        yn = yhat * cg_ref[...] + cbeta_ref[...]
        ycat_ref[:, D_SGU:D_SGU + D_CONV] = (yn * jax.nn.sigmoid(yn)).astype(BF16)
        r1 = ALPHA * xf + _dot(ycat_ref[...], wout_ref[...])
        n1, rstd1 = _ln_fwd(r1)
        n1_ref[...] = n1
        rstd1_ref[...] = rstd1

        @pl.when(i == _forward_step(nt))
        def _():
            ex.forward(*ex_refs)

        @pl.when(i == nt - 1)
        def _():
            ex.wait(*ex_refs)

    S = jax.ShapeDtypeStruct
    row = lambda w: pl.BlockSpec((tm, w), lambda i: (i, 0))
    res = pl.pallas_call(
        body, name="fwd_mix", grid=(nt,),
        in_specs=[row(D_MODEL), _full(win_g.shape), _full(wout_g.shape), _full(sgu_g.shape), _full(sgu_b.shape),
                  _full(wcat.shape), _full(bs_full.shape), _full(cw.shape), _full(cb.shape), _full(cg.shape),
                  _full(cbeta.shape)] + [ANY] * ex.n,
        out_specs=(row(2 * D_MODEL), row(D_MODEL), row(D_MODEL), row(1), row(2 * D_SGU), row(D_CONV)) + (ANY,) * ex.n,
        out_shape=(S((T, 2 * D_MODEL), F32), S((T, D_MODEL), BF16), S((T, D_MODEL), F32), S((T, 1), F32),
                   S((T, 2 * D_SGU), BF16), S((T, D_CONV), F32), *ex.out_shape),
        scratch_shapes=[pltpu.VMEM((tm + HALO, D_CONV), F32),
                        pltpu.VMEM((SUBLANES - 1, tm + SHIFT_ROWS, D_CONV), F32)] + ex.scratch,
        compiler_params=_params(56, 1),
    )(x, win_g, wout_g, sgu_g, sgu_b, wcat, bs_full, cw, cb, cg, cbeta, *ex.arrays)
    return res[:6], res[6:]


def _load_resident(pairs, sems):
    cps = [pltpu.make_async_copy(s, d, sems.at[k]) for k, (s, d) in enumerate(pairs)]
    for cp in cps:
        cp.start()
    for cp in cps:
        cp.wait()


def _fwd_mlp(n1, tgt, l1g, l1b, l2g, l2b, wgt, wut, wd, tm):
    T = n1.shape[0]
    nt = T // tm
    nf = D_FF // MXU_COLS

    def body(n1_ref, tgt_ref, l1g_ref, l1b_ref, l2g_ref, l2b_ref, wg_hbm, wu_hbm, wd_hbm,
             gate_ref, up_ref, hh_ref, dr2_ref, stat_ref, wg_s, wu_s, wd_s, sems):
        i = pl.program_id(0)

        @pl.when(i == 0)
        def _():
            _load_resident([(wg_hbm, wg_s), (wu_hbm, wu_s), (wd_hbm, wd_s)], sems)
            stat_ref[...] = jnp.zeros(stat_ref.shape, F32)

        x1 = n1_ref[...] * l1g_ref[...] + l1b_ref[...]
        x1b = x1.astype(BF16)
        for f in range(nf):
            cols = slice(MXU_COLS * f, MXU_COLS * (f + 1))
            gt = _dot_nt(x1b, wg_s[cols, :])
            ut = _dot_nt(x1b, wu_s[cols, :])
            gate_ref[:, cols] = gt.astype(BF16)
            up_ref[:, cols] = ut.astype(BF16)
            hh_ref[:, cols] = (gt * jax.nn.sigmoid(gt) * ut).astype(BF16)
        r2 = ALPHA * x1 + _dot(hh_ref[...], wd_s[...])
        n2, rstd2 = _ln_fwd(r2)
        x2 = n2 * l2g_ref[...] + l2b_ref[...]
        diff = x2 - tgt_ref[...]
        dx2 = diff * (1.0 / D_MODEL)
        stat_ref[0:1, :] += _colsum(diff * diff)
        stat_ref[1:2, :] += _colsum(dx2 * n2)
        stat_ref[2:3, :] += _colsum(dx2)
        dr2_ref[...] = _ln_bwd(dx2 * l2g_ref[...], n2, rstd2)

    S = jax.ShapeDtypeStruct
    row = lambda w: pl.BlockSpec((tm, w), lambda i: (i, 0))
    vec = _full((1, D_MODEL))
    return pl.pallas_call(
        body, name="fwd_mlp", grid=(nt,),
        in_specs=[row(D_MODEL), row(D_MODEL), vec, vec, vec, vec, ANY, ANY, ANY],
        out_specs=(row(D_FF), row(D_FF), row(D_FF), row(D_MODEL), _full((8, D_MODEL))),
        out_shape=(S((T, D_FF), BF16), S((T, D_FF), BF16), S((T, D_FF), BF16), S((T, D_MODEL), F32),
                   S((8, D_MODEL), F32)),
        scratch_shapes=[pltpu.VMEM((D_FF, D_MODEL), BF16)] * 3 + [pltpu.SemaphoreType.DMA((3,))],
        compiler_params=_params(56, 1),
    )(n1, tgt, l1g, l1b, l2g, l2b, wgt, wut, wd)


def _bwd_mlp(dr2, gate, up, n1, rstd1, l1g, wgt, wut, wd, tm):
    T = n1.shape[0]
    nt = T // tm
    nf = D_FF // MXU_COLS

    def body(dr2_ref, gate_ref, up_ref, n1_ref, rstd1_ref, l1g_ref, wg_hbm, wu_hbm, wd_hbm,
             dgate_ref, dup_ref, dr1_ref, stat_ref, wg_s, wu_s, wd_s, sems):
        i = pl.program_id(0)

        @pl.when(i == 0)
        def _():
            _load_resident([(wg_hbm, wg_s), (wu_hbm, wu_s), (wd_hbm, wd_s)], sems)
            stat_ref[...] = jnp.zeros(stat_ref.shape, F32)

        dr2 = dr2_ref[...]
        dr2b = dr2.astype(BF16)
        for f in range(nf):
            cols = slice(MXU_COLS * f, MXU_COLS * (f + 1))
            dhh = _dot_nt(dr2b, wd_s[cols, :])
            gt = gate_ref[:, cols].astype(F32)
            ut = up_ref[:, cols].astype(F32)
            sg = jax.nn.sigmoid(gt)
            dgate_ref[:, cols] = (dhh * ut * (sg * (1.0 + gt * (1.0 - sg)))).astype(BF16)
            dup_ref[:, cols] = (dhh * (gt * sg)).astype(BF16)
        dx1 = ALPHA * dr2 + _dot(dgate_ref[...], wg_s[...]) + _dot(dup_ref[...], wu_s[...])
        n1 = n1_ref[...]
        stat_ref[0:1, :] += _colsum(dx1 * n1)
        stat_ref[1:2, :] += _colsum(dx1)
        dr1_ref[...] = _ln_bwd(dx1 * l1g_ref[...], n1, rstd1_ref[...])

    S = jax.ShapeDtypeStruct
    row = lambda w: pl.BlockSpec((tm, w), lambda i: (i, 0))
    return pl.pallas_call(
        body, name="bwd_mlp", grid=(nt,),
        in_specs=[row(D_MODEL), row(D_FF), row(D_FF), row(D_MODEL), row(1), _full((1, D_MODEL)), ANY, ANY, ANY],
        out_specs=(row(D_FF), row(D_FF), row(D_MODEL), _full((8, D_MODEL))),
        out_shape=(S((T, D_FF), BF16), S((T, D_FF), BF16), S((T, D_MODEL), F32), S((8, D_MODEL), F32)),
        scratch_shapes=[pltpu.VMEM((D_FF, D_MODEL), BF16)] * 3 + [pltpu.SemaphoreType.DMA((3,))],
        compiler_params=_params(56, 1),
    )(dr2, gate, up, n1, rstd1, l1g, wgt, wut, wd)


def _bwd_mix(dr1, proj, phi, y, win_g, wout_g, sgu_g, sgu_b, wcat, wcatt, bs_full, cw, cg, cbeta, tm, ex):
    T = dr1.shape[0]
    nt = T // tm
    halo_blocks = tm // HALO

    def body(*refs):
        ins, outs, scratch, ex_refs = _hosted(ex, refs, 15, 6)
        (dr1_ref, proj_ref, halo_ref, phi_ref, y_ref, win_ref, wout_ref, sg_ref, sb_ref, wcat_ref, wcatt_ref, bs_ref,
         cw_ref, cg_ref, cbeta_ref) = ins
        gx_ref, dproj_ref, gws_out, gbs_out, gcw_ref, vec_ref = outs
        hpad, shift, dypad, dhbuf, dubuf, dvnbuf, gcw_acc, gws_ref, gbs_ref = scratch
        i = pl.program_id(0)
        tile = nt - 1 - i

        @pl.when(i == 0)
        def _():
            ex.start(*ex_refs)
            gws_ref[...] = jnp.zeros(gws_ref.shape, F32)
            gbs_ref[...] = jnp.zeros(gbs_ref.shape, F32)
            gcw_ref[...] = jnp.zeros(gcw_ref.shape, F32)
            vec_ref[...] = jnp.zeros(vec_ref.shape, F32)
            gcw_acc[...] = jnp.zeros(gcw_acc.shape, F32)
            dypad[tm:tm + HALO, :] = jnp.zeros((HALO, D_CONV), F32)

        dr1 = dr1_ref[...]
        dycat = _dot_nt(dr1.astype(BF16), wout_ref[...])
        pu = proj_ref[:, 0:D_SGU]
        pv = proj_ref[:, D_SGU:2 * D_SGU]
        cdf_u = phi_ref[:, 0:D_SGU].astype(F32)
        cdf_v = phi_ref[:, D_SGU:2 * D_SGU].astype(F32)
        u = pu * cdf_u
        vhat, rstd_v = _ln_fwd(pv * cdf_v)
        vn = vhat * sg_ref[...] + sb_ref[...]
        lo = _lo_mask()
        for c in range(tm // CHUNK):
            rows = slice(CHUNK * c, CHUNK * (c + 1))
            for p in range(4):
                lanes = slice(CHUNK * p, CHUNK * (p + 1))
                vstack = _head_pair_stack(vn[rows, lanes], lo)
                mixed = _dot(wcat_ref[p], vstack) + bs_ref[:, lanes]
                d_a = dycat[rows, lanes]
                dubuf[rows, lanes] = d_a * mixed
                dm = d_a * u[rows, lanes]
                gbs_ref[:, lanes] += dm
                dstack = _head_pair_stack(dm, lo)
                gws_ref[2 * CHUNK * p:2 * CHUNK * (p + 1), :] += _dot_nt(dstack, vn[rows, lanes].astype(BF16))
                dvnbuf[rows, lanes] = _dot(wcatt_ref[p], dstack)
        dvn = dvnbuf[...]
        vec_ref[0:1, :] += _colsum(dvn * vhat)
        vec_ref[1:2, :] += _colsum(dvn)
        dv = _ln_bwd(dvn * sg_ref[...], vhat, rstd_v)
        dproj_ref[:, 0:D_SGU] = (dubuf[...] * _gelu_grad(pu, cdf_u)).astype(BF16)
        dproj_ref[:, D_SGU:2 * D_SGU] = (dv * _gelu_grad(pv, cdf_v)).astype(BF16)
        base = 2 * D_SGU
        a = proj_ref[:, base:base + D_CONV]
        sgm = jax.nn.sigmoid(proj_ref[:, base + D_CONV:base + 2 * D_CONV])
        h_before = halo_ref[:, 0:D_CONV] * jax.nn.sigmoid(halo_ref[:, D_CONV:2 * D_CONV])
        hpad[0:HALO, :] = jnp.where(tile > 0, h_before, 0.0)
        hpad[HALO:HALO + tm, :] = a * sgm
        _shifted_copies(hpad, shift, tm + SHIFT_ROWS)
        h_offset = lambda k: HALO - (CONV_WIDTH - 1) + k
        yhat, rstd_y = _ln_fwd(y_ref[...])
        yn = yhat * cg_ref[...] + cbeta_ref[...]
        s = jax.nn.sigmoid(yn)
        dyn = dycat[:, D_SGU:D_SGU + D_CONV] * (s * (1.0 + yn * (1.0 - s)))
        vec_ref[3:4, :] += _colsum(dyn * yhat)
        vec_ref[4:5, :] += _colsum(dyn)
        dy = _ln_bwd(dyn * cg_ref[...], yhat, rstd_y)
        vec_ref[2:3, :] += _colsum(dy)
        dypad[0:tm, :] = dy
        _conv_weight_grad(dypad, hpad, shift, gcw_acc, tm, h_offset)
        _shifted_copies(dypad, shift, tm + SHIFT_ROWS)
        _causal_conv(dypad, shift, cw_ref, dhbuf, tm, lambda k: (CONV_WIDTH - 1) - k)
        dypad[tm:tm + HALO, :] = dypad[0:HALO, :]
        dh = dhbuf[...]
        dproj_ref[:, base:base + D_CONV] = (dh * sgm).astype(BF16)
        dproj_ref[:, base + D_CONV:base + 2 * D_CONV] = (dh * a * sgm * (1.0 - sgm)).astype(BF16)
        gx_ref[...] = ALPHA * dr1 + _dot(dproj_ref[...], win_ref[...])

        @pl.when(i == _forward_step(nt))
        def _():
            ex.forward(*ex_refs)

        @pl.when(i == nt - 1)
        def _():
            gcw_ref[...] = gcw_acc[...].sum(axis=1)
            gws_out[...] = gws_ref[...].astype(BF16)
            gbs_out[...] = lax.dot_general(_head_selector(), gbs_ref[...], (((1,), (1,)), ((), ())),
                                           preferred_element_type=F32, precision=lax.Precision.HIGHEST)
            ex.wait(*ex_refs)

    S = jax.ShapeDtypeStruct
    row = lambda w: pl.BlockSpec((tm, w), lambda i: (nt - 1 - i, 0))
    halo = pl.BlockSpec((HALO, D_MODEL), lambda i: (jnp.maximum((nt - 1 - i) * halo_blocks - 1, 0), 1))
    res = pl.pallas_call(
        body, name="bwd_mix", grid=(nt,),
        in_specs=[row(D_MODEL), row(2 * D_MODEL), halo, row(2 * D_SGU), row(D_CONV), _full(win_g.shape),
                  _full(wout_g.shape), _full(sgu_g.shape), _full(sgu_b.shape), _full(wcat.shape), _full(wcatt.shape),
                  _full(bs_full.shape), _full(cw.shape), _full(cg.shape), _full(cbeta.shape)]
        + [ANY] * ex.n,
        out_specs=(row(D_MODEL), row(2 * D_MODEL), _full((N_HEADS * CHUNK, CHUNK)), _full((N_HEADS, CHUNK)),
                   _full((CONV_ROWS, D_CONV)), _full((8, D_CONV))) + (ANY,) * ex.n,
        out_shape=(S((T, D_MODEL), F32), S((T, 2 * D_MODEL), BF16), S((N_HEADS * CHUNK, CHUNK), BF16),
                   S((N_HEADS, CHUNK), F32), S((CONV_ROWS, D_CONV), F32), S((8, D_CONV), F32), *ex.out_shape),
        scratch_shapes=[pltpu.VMEM((tm + HALO, D_CONV), F32), pltpu.VMEM((SUBLANES - 1, tm + SHIFT_ROWS, D_CONV), F32),
                        pltpu.VMEM((tm + HALO, D_CONV), F32),
                        pltpu.VMEM((tm, D_CONV), F32), pltpu.VMEM((tm, D_SGU), F32),
                        pltpu.VMEM((tm, D_SGU), F32), pltpu.VMEM((CONV_ROWS, 8, D_CONV), F32),
                        pltpu.VMEM((N_HEADS * CHUNK, CHUNK), F32), pltpu.VMEM((CHUNK, D_SGU), F32)] + ex.scratch,
        compiler_params=_params(56, 1),
    )(dr1, proj, proj, phi, y, win_g, wout_g, sgu_g, sgu_b, wcat, wcatt, bs_full, cw, cg, cbeta, *ex.arrays)
    return res[:6], res[6:]


def _wgrad(name, a, b, blocks, tk, ex=None, b_cols=None, b_affine=None):
    T, M = a.shape
    col, N = b_cols or (0, b.shape[1])
    nk = T // tk
    out_shape = (blocks, M // blocks, N)
    ex = ex or _Exchange([], [])
    affine = list(b_affine or [])

    def body(*refs):
        ins, (o_ref,), (acc,), ex_refs = _hosted(ex, refs, 2 + len(affine), 1)
        a_ref, b_ref = ins[:2]
        i = pl.program_id(0)

        @pl.when(i == 0)
        def _():
            ex.start(*ex_refs)
            acc[...] = jnp.zeros(acc.shape, F32)

        right = b_ref[...]
        if affine:
            right = right * ins[2][...] + ins[3][...]
        acc[...] += _dot_tn(a_ref[...].astype(BF16), right.astype(BF16))

        @pl.when(i == _forward_step(nk))
        def _():
            ex.forward(*ex_refs)

        @pl.when(i == nk - 1)
        def _():
            o_ref[...] = acc[...].astype(BF16)
            ex.wait(*ex_refs)

    res = pl.pallas_call(
        body, name=name, grid=(nk,),
        in_specs=[pl.BlockSpec((tk, M), lambda i: (i, 0)), pl.BlockSpec((tk, N), lambda i: (i, col))]
        + [_full((1, N))] * len(affine) + [ANY] * ex.n,
        out_specs=(_full((M, N)),) + (ANY,) * ex.n,
        out_shape=(jax.ShapeDtypeStruct((M, N), BF16), *ex.out_shape),
        scratch_shapes=[pltpu.VMEM((M, N), F32)] + ex.scratch,
        compiler_params=_params(56, 1),
    )(a, b, *affine, *ex.arrays)
    g = res[0].reshape(out_shape)
    return (g, res[1:]) if ex.n else g


def _adamw(w, g, m, v):
    m2 = ADAM_B1 * m + (1.0 - ADAM_B1) * g
    v2 = ADAM_B2 * v + (1.0 - ADAM_B2) * (g * g)
    m_hat = m2 / (1.0 - ADAM_B1 ** ADAM_STEP)
    v_hat = v2 / (1.0 - ADAM_B2 ** ADAM_STEP)
    delta = -ADAM_LR * (m_hat / (jnp.sqrt(v_hat) + ADAM_EPS) + ADAM_WD * w)
    return delta, m2, v2


def _sum_partials(r_ref):
    g = r_ref[0].astype(F32)
    for s in range(1, N_DEV):
        g = g + r_ref[s].astype(F32)
    return g


def _staged_call(name, groups, vmem_mib, ex=None):
    ex = ex or _Exchange([], [])
    inputs = [a for ins, _, _ in groups for a in ins]
    out_shapes = [s for _, outs, _ in groups for s in outs]
    n_in, n_out = len(inputs), len(out_shapes)

    def body(*refs):
        ins, outs, scratch, ex_refs = _hosted(ex, refs, n_in, n_out)
        in_bufs, out_bufs, sems = scratch[:n_in], scratch[n_in:n_in + n_out], scratch[n_in + n_out]
        ex.start(*ex_refs)
        loads = [pltpu.make_async_copy(ins[k], in_bufs[k], sems.at[k]) for k in range(n_in)]
        stores = [pltpu.make_async_copy(out_bufs[k], outs[k], sems.at[n_in + k]) for k in range(n_out)]
        for cp in loads:
            cp.start()
        i0 = o0 = 0
        for g_ins, g_outs, compute in groups:
            i1, o1 = i0 + len(g_ins), o0 + len(g_outs)
            for cp in loads[i0:i1]:
                cp.wait()
            compute(in_bufs[i0:i1], out_bufs[o0:o1])
            for cp in stores[o0:o1]:
                cp.start()
            i0, o0 = i1, o1
        for cp in stores:
            cp.wait()
        ex.forward(*ex_refs)
        ex.wait(*ex_refs)

    scratch = ([pltpu.VMEM(a.shape, a.dtype) for a in inputs] + [pltpu.VMEM(s.shape, s.dtype) for s in out_shapes]
               + [pltpu.SemaphoreType.DMA((n_in + n_out,))] + ex.scratch)
    res = pl.pallas_call(
        body, name=name, out_shape=(*[pltpu.HBM(s.shape, s.dtype) for s in out_shapes], *ex.out_shape),
        in_specs=[HBM] * n_in + [ANY] * ex.n, out_specs=(HBM,) * n_out + (ANY,) * ex.n,
        scratch_shapes=scratch, compiler_params=_params(vmem_mib),
    )(*[pltpu.with_memory_space_constraint(a, pltpu.HBM) for a in inputs], *ex.arrays)
    per_group, o0 = [], 0
    for _, g_outs, _ in groups:
        per_group.append(list(res[o0:o0 + len(g_outs)]))
        o0 += len(g_outs)
    return per_group, res[n_out:]


def _adamw_shard_group(parts, w, m, v, transposed):
    n = len(parts)

    def compute(ins, outs):
        w_ref, m_ref, v_ref = ins[n:]
        lo = 0
        for r_ref in ins[:n]:
            g = _sum_partials(r_ref)
            cols = g.shape[1]
            if transposed:
                g, at = g.T, (slice(lo, lo + cols), slice(None))
            else:
                at = (slice(None), slice(lo, lo + cols))
            delta, m2, v2 = _adamw(w_ref[at], g, m_ref[at], v_ref[at])
            for o, val in zip(outs, (g, delta, m2, v2)):
                o[at] = val
            lo += cols

    return [*parts, w, m, v], [jax.ShapeDtypeStruct(w.shape, F32)] * 4, compute


SMALL_NAMES = ["sgu_ln_g", "sgu_ln_b", "w_s", "b_s", "conv_b", "conv_ln_g", "conv_ln_b", "ln1_g", "ln1_b", "ln2_g", "ln2_b",
               "conv_w"]


def _finish_small_group(gws8, gbs8, gcw8, vmix8, vmlp8, vout8, small):
    names = SMALL_NAMES
    flat = []
    for n in names:
        flat += list(small[n])
    cw_block = D_CONV // N_DEV

    def compute(ins, outs):
        gws_ref, gbs_ref, gcw_ref, vmix_ref, vmlp_ref, vout_ref = ins[:6]
        wmv = ins[6:]
        loss_o = outs[0]
        outs = outs[1:]
        gws = _sum_partials(gws_ref)
        gbs = _sum_partials(gbs_ref)
        vmix = _sum_partials(vmix_ref)
        vmlp = _sum_partials(vmlp_ref)
        vout = _sum_partials(vout_ref)
        x, y, c = _mesh_position()
        first = (4 * x + 2 * y + c) * cw_block
        pick = (lax.broadcasted_iota(jnp.int32, (D_CONV, cw_block), 0)
                == first + lax.broadcasted_iota(jnp.int32, (D_CONV, cw_block), 1)).astype(F32)
        gcw = jnp.dot(_sum_partials(gcw_ref), pick, preferred_element_type=F32,
                      precision=lax.Precision.HIGHEST)[0:CONV_WIDTH, :]
        loss = (0.5 / D_MODEL) * jnp.sum(vout[0:1, :], axis=1, keepdims=True)
        loss_o[...] = jnp.broadcast_to(loss, loss_o.shape)
        rows = lax.broadcasted_iota(jnp.int32, (N_HEADS * CHUNK, CHUNK), 0)
        cols = lax.broadcasted_iota(jnp.int32, (N_HEADS * CHUNK, CHUNK), 1)
        gws = jnp.where((rows & (CHUNK - 1)) >= cols, gws, 0.0)
        grads = {
            "sgu_ln_g": vmix[0:1, :], "sgu_ln_b": vmix[1:2, :], "w_s": gws, "b_s": gbs,
            "conv_b": vmix[2:3, :], "conv_ln_g": vmix[3:4, :], "conv_ln_b": vmix[4:5, :],
            "ln1_g": vmlp[0:1, :], "ln1_b": vmlp[1:2, :], "ln2_g": vout[1:2, :], "ln2_b": vout[2:3, :],
            "conv_w": gcw,
        }
        for k, n in enumerate(names):
            w_ref, m_ref, v_ref = wmv[3 * k:3 * k + 3]
            g = grads[n]
            delta, m2, v2 = _adamw(w_ref[...], g, m_ref[...], v_ref[...])
            outs[4 * k][...] = g
            outs[4 * k + 1][...] = delta
            outs[4 * k + 2][...] = m2
            outs[4 * k + 3][...] = v2

    S = jax.ShapeDtypeStruct
    out_shape = [S((SUBLANES, 128), F32)]
    for n in names:
        out_shape += [S(small[n][0].shape, F32)] * 4
    return [gws8, gbs8, gcw8, vmix8, vmlp8, vout8, *flat], out_shape, compute


TOKEN_TILE_FWD_MIX = 512
TOKEN_TILE_BWD_MIX = 512
TOKEN_TILE_FWD_MLP = 512
TOKEN_TILE_BWD_MLP = 512
TOKEN_TILE_WGRAD = 1024


def kernel(x, w_in, sgu_ln_g, sgu_ln_b, w_s, b_s, conv_w, conv_b, conv_ln_g, conv_ln_b, w_out, ln1_g, ln1_b, w_gate, w_up, w_down, ln2_g, ln2_b, loss_target, m_w_in, m_sgu_ln_g, m_sgu_ln_b, m_w_s, m_b_s, m_conv_w, m_conv_b, m_conv_ln_g, m_conv_ln_b, m_w_out, m_ln1_g, m_ln1_b, m_w_gate, m_w_up, m_w_down, m_ln2_g, m_ln2_b, v_w_in, v_sgu_ln_g, v_sgu_ln_b, v_w_s, v_b_s, v_conv_w, v_conv_b, v_conv_ln_g, v_conv_ln_b, v_w_out, v_ln1_g, v_ln1_b, v_w_gate, v_w_up, v_w_down, v_ln2_g, v_ln2_b):
    xs = x[0]
    tgt = loss_target[0]

    (win_b, wout_b, wgt_b, wut_b, wd_b, cw_b, wcat, wcatt, bs_full) = _prep_weights(
        w_in[0], w_out[0], w_gate[0].T, w_up[0].T, w_down[0], conv_w[0], w_s[0], b_s[0])
    win_g, wout_g, cw_g = _exchange("gather_mix_weights", [], [win_b, wout_b, cw_b])
    win_g = win_g.reshape(2 * D_MODEL, D_MODEL)
    wout_g = wout_g.reshape(D_MODEL, D_MODEL)
    cw = jnp.transpose(cw_g[:, :, :D_CONV // N_DEV], (1, 0, 2)).reshape(CONV_ROWS, D_CONV)

    (proj, ycat, n1, rstd1, phi, y_conv), (wgt_g, wut_g, wd_g) = _fwd_mix(
        xs, win_g, wout_g, sgu_ln_g, sgu_ln_b, wcat, bs_full, cw, conv_b, conv_ln_g, conv_ln_b, TOKEN_TILE_FWD_MIX,
        _Exchange([], [wgt_b, wut_b, wd_b]))
    wgt_g = wgt_g.reshape(D_FF, D_MODEL)
    wut_g = wut_g.reshape(D_FF, D_MODEL)
    wd_g = wd_g.reshape(D_FF, D_MODEL)
    gate, up, hh, dr2, vout = _fwd_mlp(n1, tgt, ln1_g, ln1_b, ln2_g, ln2_b, wgt_g, wut_g, wd_g, TOKEN_TILE_FWD_MLP)

    dgate, dup, dr1, vmlp = _bwd_mlp(dr2, gate, up, n1, rstd1, ln1_g, wgt_g, wut_g, wd_g, TOKEN_TILE_BWD_MLP)
    tk = TOKEN_TILE_WGRAD
    x1 = dict(b_affine=(ln1_g, ln1_b))
    g_wgt = _wgrad("wgrad_gate", dgate, n1, N_DEV, tk, **x1)
    g_wut = _wgrad("wgrad_up", dup, n1, N_DEV, tk, **x1)
    g_wd = _wgrad("wgrad_down", hh, dr2, N_DEV, tk)
    g_wout = _wgrad("wgrad_out", ycat, dr1, N_DEV, tk)
    (gx, dproj, gws, gbs, gcw, vmix), (r_wgt, r_wut, r_wd, r_wout, vmlp8, vout8) = _bwd_mix(
        dr1, proj, phi, y_conv, win_g, wout_g, sgu_ln_g, sgu_ln_b, wcat, wcatt, bs_full, cw, conv_ln_g, conv_ln_b,
        TOKEN_TILE_BWD_MIX, _Exchange([g_wgt, g_wut, g_wd, g_wout], [vmlp, vout]))
    half = D_MODEL // 2
    g_win_a, (gws8, gbs8, gcw8, vmix8) = _wgrad(
        "wgrad_in_a", dproj, xs, N_DEV, tk, _Exchange([], [gws, gbs, gcw, vmix]), b_cols=(0, half))
    g_win_b, (r_win_a,) = _wgrad("wgrad_in_b", dproj, xs, N_DEV, tk, _Exchange([g_win_a], []), b_cols=(1, half))
    (r_win_b,) = _exchange("exchange_grad_in", [g_win_b], [])
    (u_gate, u_up, u_down), _ = _staged_call(
        "adamw_mlp",
        [_adamw_shard_group([r_wgt], w_gate[0].T, m_w_gate[0].T, v_w_gate[0].T, False),
         _adamw_shard_group([r_wut], w_up[0].T, m_w_up[0].T, v_w_up[0].T, False),
         _adamw_shard_group([r_wd], w_down[0], m_w_down[0], v_w_down[0], False)], 56)
    small_in = {
        "conv_w": (conv_w[0], m_conv_w[0], v_conv_w[0]),
        "sgu_ln_g": (sgu_ln_g, m_sgu_ln_g, v_sgu_ln_g), "sgu_ln_b": (sgu_ln_b, m_sgu_ln_b, v_sgu_ln_b),
        "w_s": tuple(a.reshape(N_HEADS * CHUNK, CHUNK) for a in (w_s, m_w_s, v_w_s)),
        "b_s": (b_s[0], m_b_s[0], v_b_s[0]),
        "conv_b": (conv_b, m_conv_b, v_conv_b), "conv_ln_g": (conv_ln_g, m_conv_ln_g, v_conv_ln_g),
        "conv_ln_b": (conv_ln_b, m_conv_ln_b, v_conv_ln_b),
        "ln1_g": (ln1_g, m_ln1_g, v_ln1_g), "ln1_b": (ln1_b, m_ln1_b, v_ln1_b),
        "ln2_g": (ln2_g, m_ln2_g, v_ln2_g), "ln2_b": (ln2_b, m_ln2_b, v_ln2_b),
    }
    (u_in, u_out, fin), _ = _staged_call(
        "adamw_mix_small",
        [_adamw_shard_group([r_win_a, r_win_b], w_in[0], m_w_in[0], v_w_in[0], True),
         _adamw_shard_group([r_wout], w_out[0], m_w_out[0], v_w_out[0], False),
         _finish_small_group(gws8, gbs8, gcw8, vmix8, vmlp8, vout8, small_in)], 40)
    big = {"w_in": u_in, "w_out": u_out, "w_gate": u_gate, "w_up": u_up, "w_down": u_down}
    loss11 = fin[0]
    small = {n: fin[1 + 4 * k:5 + 4 * k] for k, n in enumerate(SMALL_NAMES)}

    shapes = {"w_s": w_s.shape, "b_s": b_s.shape, "conv_w": conv_w.shape}
    out = {}
    for n, r in big.items():
        out[n] = tuple((a.T if n in ("w_gate", "w_up") else a)[None] for a in r)
    for n, r in small.items():
        out[n] = tuple(a.reshape(shapes[n]) for a in r) if n in shapes else tuple(r)

    order = ["w_in", "sgu_ln_g", "sgu_ln_b", "w_s", "b_s", "conv_w", "conv_b", "conv_ln_g", "conv_ln_b", "w_out",
             "ln1_g", "ln1_b", "w_gate", "w_up", "w_down", "ln2_g", "ln2_b"]
    loss = loss11[0, 0]
    return (loss, gx[None], *[out[n][0] for n in order], *[out[n][1] for n in order],
            *[out[n][2] for n in order], *[out[n][3] for n in order])
```
